```python
import math
import jax, jax.numpy as jnp
from jax import lax
import numpy as np

D_MODEL = 1024
BATCH = 8
SEQ = 4096
DEPTH = 1

CHUNK = 64
MIX_WIDTH = D_MODEL
SB_WIDTH = D_MODEL // 2
SB_HEADS = 8
SB_HEAD_DIM = SB_WIDTH // SB_HEADS
SB_Q_BLOCK = 128
GLA_WIDTH = MIX_WIDTH - SB_WIDTH
GLA_HEADS = 4
GLA_KEY_DIM = GLA_WIDTH // 2 // GLA_HEADS
GLA_VAL_DIM = GLA_WIDTH // GLA_HEADS
GLA_GATE_RANK = 16
GLA_TAU = 16.0
D_FF = 2816
CONV_WIDTH = 3
LN_EPS = 1e-5
RMS_EPS = 1e-6
DN_ALPHA = (2.0 * DEPTH) ** 0.25
DN_BETA = (8.0 * DEPTH) ** -0.25

IN_SIZES = (SB_WIDTH, SB_WIDTH, SB_WIDTH,
            GLA_HEADS * GLA_KEY_DIM, GLA_HEADS * GLA_KEY_DIM, GLA_WIDTH, GLA_WIDTH,
            GLA_GATE_RANK)
IN_SPLITS = tuple(int(v) for v in np.cumsum(IN_SIZES)[:-1])
IN_WIDTH = int(sum(IN_SIZES))

kernel_name = "hybrid_stickbreak_gla_convffn_deepnorm"


def layer_norm(x, g, b):
    xf = x.astype(jnp.float32)
    mu = jnp.mean(xf, axis=-1, keepdims=True)
    var = jnp.mean(jnp.square(xf - mu), axis=-1, keepdims=True)
    y = (xf - mu) * lax.rsqrt(var + LN_EPS) * g.astype(jnp.float32) + b.astype(jnp.float32)
    return y.astype(x.dtype)


def stick_breaking_attention(q, k, v):
    S = q.shape[2]
    scale = SB_HEAD_DIM ** -0.5
    outs = []
    for q0 in range(0, S, SB_Q_BLOCK):
        L = q0 + SB_Q_BLOCK
        qb = q[:, :, q0:L].astype(jnp.float32)
        kb = k[:, :, :L].astype(jnp.float32)
        vb = v[:, :, :L].astype(jnp.float32)
        z = jnp.einsum("bhqd,bhkd->bhqk", qb, kb) * scale
        qpos = q0 + jnp.arange(SB_Q_BLOCK)[:, None]
        kpos = jnp.arange(L)[None, :]
        strict = kpos < qpos
        log_1m = jnp.where(strict, jax.nn.log_sigmoid(-z), 0.0)
        suffix = lax.cumsum(log_1m, axis=3, reverse=True) - log_1m
        log_w = jax.nn.log_sigmoid(z) + suffix
        w = jnp.where(strict, jnp.exp(log_w), 0.0)
        outs.append(jnp.einsum("bhqk,bhkd->bhqd", w, vb))
    return jnp.concatenate(outs, axis=2).astype(v.dtype)


def gla_chunked(q, k, v, log_a):
    B, S, H, Dk = q.shape
    Dv = v.shape[-1]
    N = S // CHUNK
    f32 = jnp.float32
    q = (q.astype(f32) * Dk ** -0.5).reshape(B, N, CHUNK, H, Dk)
    k = k.astype(f32).reshape(B, N, CHUNK, H, Dk)
    v = v.astype(f32).reshape(B, N, CHUNK, H, Dv)
    g = log_a.astype(f32).reshape(B, N, CHUNK, H, Dk)
    b = jnp.cumsum(g, axis=2)
    b_ref = b[:, :, CHUNK // 2 - 1:CHUNK // 2]
    q_in = q * jnp.exp(b - b_ref)
    k_in = k * jnp.exp(b_ref - b)
    scores = jnp.einsum("bnthd,bnshd->bnhts", q_in, k_in)
    causal = jnp.tril(jnp.ones((CHUNK, CHUNK), dtype=bool))
    scores = jnp.where(causal, scores, 0.0)
    o_intra = jnp.einsum("bnhts,bnshv->bnthv", scores, v)
    b_last = b[:, :, -1]
    k_dec = k * jnp.exp(b_last[:, :, None] - b)
    chunk_upd = jnp.einsum("bnshk,bnshv->bnhkv", k_dec, v)
    decay = jnp.exp(b_last)

    def step(state, inp):
        dec, upd = inp
        return dec[..., None] * state + upd, state

    init = jnp.zeros((B, H, Dk, Dv), f32)
    _, prev = lax.scan(step, init, (jnp.moveaxis(decay, 1, 0), jnp.moveaxis(chunk_upd, 1, 0)))
    prev = jnp.moveaxis(prev, 0, 1)
    o_inter = jnp.einsum("bnthk,bnhkv->bnthv", q * jnp.exp(b), prev)
    return (o_intra + o_inter).reshape(B, S, H, Dv)


def causal_depthwise_conv(u, w, bias):
    C = u.shape[-1]
    y = lax.conv_general_dilated(
        u, w[:, None, :].astype(u.dtype), window_strides=(1,),
        padding=[(CONV_WIDTH - 1, 0)], dimension_numbers=("NWC", "WIO", "NWC"),
        feature_group_count=C)
    return y + bias


def hybrid_layer(x, w_in, gate_up, gate_bias, gla_norm_g, w_out, ln1_g, ln1_b,
                 w_up, conv_w, conv_b, w_down, ln2_g, ln2_b):
    B, S, _ = x.shape
    proj = x @ w_in
    sb_q, sb_k, sb_v, gq, gk, gv, gg, ga = jnp.split(proj, IN_SPLITS, axis=-1)

    def to_heads(t):
        return t.reshape(B, S, SB_HEADS, SB_HEAD_DIM).transpose(0, 2, 1, 3)

    sb_o = stick_breaking_attention(to_heads(sb_q), to_heads(sb_k), to_heads(sb_v))
    sb_o = sb_o.transpose(0, 2, 1, 3).reshape(B, S, SB_WIDTH)

    log_a = jax.nn.log_sigmoid((ga @ gate_up + gate_bias).astype(jnp.float32)) / GLA_TAU
    o = gla_chunked(gq.reshape(B, S, GLA_HEADS, GLA_KEY_DIM),
                    gk.reshape(B, S, GLA_HEADS, GLA_KEY_DIM),
                    gv.reshape(B, S, GLA_HEADS, GLA_VAL_DIM),
                    log_a.reshape(B, S, GLA_HEADS, GLA_KEY_DIM))
    o = o * lax.rsqrt(jnp.mean(jnp.square(o), axis=-1, keepdims=True) + RMS_EPS)
    o = o * gla_norm_g.astype(jnp.float32)
    gla_o = (o.reshape(B, S, GLA_WIDTH) * jax.nn.silu(gg.astype(jnp.float32))).astype(x.dtype)

    mix = jnp.concatenate([sb_o, gla_o], axis=-1) @ w_out
    h = layer_norm(DN_ALPHA * x + mix, ln1_g, ln1_b)

    u = causal_depthwise_conv(h @ w_up, conv_w, conv_b)
    a, c = jnp.split(u, 2, axis=-1)
    f = (jax.nn.gelu(a, approximate=False) * c) @ w_down
    return layer_norm(DN_ALPHA * h + f, ln2_g, ln2_b)


def _fwd_setup_inputs(seed: int = 0) -> dict:
    key = jax.random.key(seed)
    ks = jax.random.split(key, 24)
    d = D_MODEL
    nrm = lambda k, shape, s: jax.random.normal(k, shape, jnp.float32) * s
    in_scale = d ** -0.5
    pieces = [
        nrm(ks[1], (DEPTH, d, SB_WIDTH), in_scale),
        nrm(ks[2], (DEPTH, d, SB_WIDTH), in_scale),
        nrm(ks[3], (DEPTH, d, SB_WIDTH), in_scale * DN_BETA),
        nrm(ks[4], (DEPTH, d, GLA_HEADS * GLA_KEY_DIM), in_scale),
        nrm(ks[5], (DEPTH, d, GLA_HEADS * GLA_KEY_DIM), in_scale),
        nrm(ks[6], (DEPTH, d, GLA_WIDTH), in_scale * DN_BETA),
        nrm(ks[7], (DEPTH, d, GLA_WIDTH), in_scale),
        nrm(ks[8], (DEPTH, d, GLA_GATE_RANK), in_scale),
    ]
    return {
        "x": nrm(ks[0], (BATCH, SEQ, d), 1.0),
        "w_in": jnp.concatenate(pieces, axis=-1),
        "gate_up": nrm(ks[9], (DEPTH, GLA_GATE_RANK, GLA_HEADS * GLA_KEY_DIM), GLA_GATE_RANK ** -0.5),
        "gate_bias": nrm(ks[10], (DEPTH, GLA_HEADS * GLA_KEY_DIM), 0.1),
        "gla_norm_g": 1.0 + nrm(ks[11], (DEPTH, GLA_VAL_DIM), 0.02),
        "w_out": nrm(ks[12], (DEPTH, MIX_WIDTH, d), MIX_WIDTH ** -0.5 * DN_BETA),
        "ln1_g": 1.0 + nrm(ks[13], (DEPTH, d), 0.02),
        "ln1_b": nrm(ks[14], (DEPTH, d), 0.02),
        "w_up": nrm(ks[15], (DEPTH, d, 2 * D_FF), d ** -0.5 * DN_BETA),
        "conv_w": nrm(ks[16], (DEPTH, CONV_WIDTH, 2 * D_FF), CONV_WIDTH ** -0.5),
        "conv_b": nrm(ks[17], (DEPTH, 2 * D_FF), 0.02),
        "w_down": nrm(ks[18], (DEPTH, D_FF, d), D_FF ** -0.5 * DN_BETA),
        "ln2_g": 1.0 + nrm(ks[19], (DEPTH, d), 0.02),
        "ln2_b": nrm(ks[20], (DEPTH, d), 0.02),
    }


def _fwd_reference(x, w_in, gate_up, gate_bias, gla_norm_g, w_out, ln1_g, ln1_b,
              w_up, conv_w, conv_b, w_down, ln2_g, ln2_b):
    for l in range(DEPTH):
        x = hybrid_layer(x, w_in[l], gate_up[l], gate_bias[l], gla_norm_g[l], w_out[l],
                         ln1_g[l], ln1_b[l], w_up[l], conv_w[l], conv_b[l], w_down[l],
                         ln2_g[l], ln2_b[l])
    return x


import jax as _jax
import jax.numpy as _jnp

TWIN_FORMAT = 'train_step'
FWD_PARAMS = ['x', 'w_in', 'gate_up', 'gate_bias', 'gla_norm_g', 'w_out', 'ln1_g', 'ln1_b', 'w_up', 'conv_w', 'conv_b', 'w_down', 'ln2_g', 'ln2_b']
TWIN_WEIGHTS = ['w_in', 'gate_up', 'gate_bias', 'gla_norm_g', 'w_out', 'ln1_g', 'ln1_b', 'w_up', 'conv_w', 'conv_b', 'w_down', 'ln2_g', 'ln2_b']
TWIN_DIFF_INPUT = 'x'
TWIN_INPUTS = ['x', 'w_in', 'gate_up', 'gate_bias', 'gla_norm_g', 'w_out', 'ln1_g', 'ln1_b', 'w_up', 'conv_w', 'conv_b', 'w_down', 'ln2_g', 'ln2_b', 'loss_target', 'm_w_in', 'm_gate_up', 'm_gate_bias', 'm_gla_norm_g', 'm_w_out', 'm_ln1_g', 'm_ln1_b', 'm_w_up', 'm_conv_w', 'm_conv_b', 'm_w_down', 'm_ln2_g', 'm_ln2_b', 'v_w_in', 'v_gate_up', 'v_gate_bias', 'v_gla_norm_g', 'v_w_out', 'v_ln1_g', 'v_ln1_b', 'v_w_up', 'v_conv_w', 'v_conv_b', 'v_w_down', 'v_ln2_g', 'v_ln2_b']
TWIN_OUTPUTS = ['loss', 'grad_x', 'grad_w_in', 'grad_gate_up', 'grad_gate_bias', 'grad_gla_norm_g', 'grad_w_out', 'grad_ln1_g', 'grad_ln1_b', 'grad_w_up', 'grad_conv_w', 'grad_conv_b', 'grad_w_down', 'grad_ln2_g', 'grad_ln2_b', 'delta_w_in', 'delta_gate_up', 'delta_gate_bias', 'delta_gla_norm_g', 'delta_w_out', 'delta_ln1_g', 'delta_ln1_b', 'delta_w_up', 'delta_conv_w', 'delta_conv_b', 'delta_w_down', 'delta_ln2_g', 'delta_ln2_b', 'new_m_w_in', 'new_m_gate_up', 'new_m_gate_bias', 'new_m_gla_norm_g', 'new_m_w_out', 'new_m_ln1_g', 'new_m_ln1_b', 'new_m_w_up', 'new_m_conv_w', 'new_m_conv_b', 'new_m_w_down', 'new_m_ln2_g', 'new_m_ln2_b', 'new_v_w_in', 'new_v_gate_up', 'new_v_gate_bias', 'new_v_gla_norm_g', 'new_v_w_out', 'new_v_ln1_g', 'new_v_ln1_b', 'new_v_w_up', 'new_v_conv_w', 'new_v_conv_b', 'new_v_w_down', 'new_v_ln2_g', 'new_v_ln2_b']
TWIN_LEAF_KINDS = {'loss': 'loss', 'grad_x': 'grad_x', 'grad_w_in': 'grad_w', 'grad_gate_up': 'grad_w', 'grad_gate_bias': 'grad_w', 'grad_gla_norm_g': 'grad_w', 'grad_w_out': 'grad_w', 'grad_ln1_g': 'grad_w', 'grad_ln1_b': 'grad_w', 'grad_w_up': 'grad_w', 'grad_conv_w': 'grad_w', 'grad_conv_b': 'grad_w', 'grad_w_down': 'grad_w', 'grad_ln2_g': 'grad_w', 'grad_ln2_b': 'grad_w', 'delta_w_in': 'delta_w', 'delta_gate_up': 'delta_w', 'delta_gate_bias': 'delta_w', 'delta_gla_norm_g': 'delta_w', 'delta_w_out': 'delta_w', 'delta_ln1_g': 'delta_w', 'delta_ln1_b': 'delta_w', 'delta_w_up': 'delta_w', 'delta_conv_w': 'delta_w', 'delta_conv_b': 'delta_w', 'delta_w_down': 'delta_w', 'delta_ln2_g': 'delta_w', 'delta_ln2_b': 'delta_w', 'new_m_w_in': 'new_m', 'new_m_gate_up': 'new_m', 'new_m_gate_bias': 'new_m', 'new_m_gla_norm_g': 'new_m', 'new_m_w_out': 'new_m', 'new_m_ln1_g': 'new_m', 'new_m_ln1_b': 'new_m', 'new_m_w_up': 'new_m', 'new_m_conv_w': 'new_m', 'new_m_conv_b': 'new_m', 'new_m_w_down': 'new_m', 'new_m_ln2_g': 'new_m', 'new_m_ln2_b': 'new_m', 'new_v_w_in': 'new_v', 'new_v_gate_up': 'new_v', 'new_v_gate_bias': 'new_v', 'new_v_gla_norm_g': 'new_v', 'new_v_w_out': 'new_v', 'new_v_ln1_g': 'new_v', 'new_v_ln1_b': 'new_v', 'new_v_w_up': 'new_v', 'new_v_conv_w': 'new_v', 'new_v_conv_b': 'new_v', 'new_v_w_down': 'new_v', 'new_v_ln2_g': 'new_v', 'new_v_ln2_b': 'new_v'}


def _forward(args):
    return _fwd_reference(*[args[k] for k in FWD_PARAMS])


def _output_shape():
    def fwd():
        inp = _fwd_setup_inputs(0)
        return _fwd_reference(*[inp[k] for k in FWD_PARAMS])
    out = _jax.eval_shape(fwd)
    return out.shape, out.dtype

N_MICROBATCH = 1
ADAM_LR = 0.001
ADAM_B1 = 0.9
ADAM_B2 = 0.999
ADAM_EPS = 1e-08
ADAM_WD = 0.01
ADAM_STEP = 10
PER_EXAMPLE_BATCH_AXIS = {'x': 0, 'loss_target': 0}
SHARED_INPUTS = []
_WEIGHT_DTYPES = {'w_in': _jnp.float32, 'gate_up': _jnp.float32, 'gate_bias': _jnp.float32, 'gla_norm_g': _jnp.float32, 'w_out': _jnp.float32, 'ln1_g': _jnp.float32, 'ln1_b': _jnp.float32, 'w_up': _jnp.float32, 'conv_w': _jnp.float32, 'conv_b': _jnp.float32, 'w_down': _jnp.float32, 'ln2_g': _jnp.float32, 'ln2_b': _jnp.float32}
MOMENT_SCALE = {'w_in': 5.770477e-02, 'gate_up': 8.059177e-03, 'gate_bias': 3.217395e-02, 'gla_norm_g': 1.086221e-01, 'w_out': 7.271737e-02, 'ln1_g': 1.171092e+00, 'ln1_b': 5.595890e-01, 'w_up': 1.968435e-02, 'conv_w': 1.180689e-02, 'conv_b': 2.369278e-02, 'w_down': 3.214525e-02, 'ln2_g': 3.200892e+01, 'ln2_b': 8.011130e-01}


def _to_microbatches(a, axis):
    t = _jnp.moveaxis(a, axis, 0)
    t = t.reshape((N_MICROBATCH, t.shape[0] // N_MICROBATCH) + t.shape[1:])
    return _jnp.moveaxis(t, 1, axis + 1)


def setup_inputs(seed: int = 0) -> dict:
    inp = _fwd_setup_inputs(seed)
    key = _jax.random.fold_in(_jax.random.key(seed), 7919)
    shape, _ = _output_shape()
    out = dict(inp)
    out["loss_target"] = _jax.random.normal(_jax.random.fold_in(key, 0), shape, _jnp.float32)
    for i, name in enumerate(TWIN_WEIGHTS):
        w = inp[name].astype(_jnp.float32)
        if MOMENT_SCALE is None:
            s = _jnp.sqrt(_jnp.mean(_jnp.square(w)) + 1e-30)
        else:
            s = MOMENT_SCALE[name]
        km, kv = _jax.random.split(_jax.random.fold_in(key, i + 1))
        out[name] = w
        out["m_" + name] = s * _jax.random.normal(km, w.shape, _jnp.float32)
        out["v_" + name] = (s * s) * _jax.random.uniform(kv, w.shape, _jnp.float32, 0.5, 1.5)
    if N_MICROBATCH > 1:
        for name, axis in PER_EXAMPLE_BATCH_AXIS.items():
            out[name] = _to_microbatches(out[name], axis)
    return {'x': out['x'], 'w_in': out['w_in'], 'gate_up': out['gate_up'], 'gate_bias': out['gate_bias'], 'gla_norm_g': out['gla_norm_g'], 'w_out': out['w_out'], 'ln1_g': out['ln1_g'], 'ln1_b': out['ln1_b'], 'w_up': out['w_up'], 'conv_w': out['conv_w'], 'conv_b': out['conv_b'], 'w_down': out['w_down'], 'ln2_g': out['ln2_g'], 'ln2_b': out['ln2_b'], 'loss_target': out['loss_target'], 'm_w_in': out['m_w_in'], 'm_gate_up': out['m_gate_up'], 'm_gate_bias': out['m_gate_bias'], 'm_gla_norm_g': out['m_gla_norm_g'], 'm_w_out': out['m_w_out'], 'm_ln1_g': out['m_ln1_g'], 'm_ln1_b': out['m_ln1_b'], 'm_w_up': out['m_w_up'], 'm_conv_w': out['m_conv_w'], 'm_conv_b': out['m_conv_b'], 'm_w_down': out['m_w_down'], 'm_ln2_g': out['m_ln2_g'], 'm_ln2_b': out['m_ln2_b'], 'v_w_in': out['v_w_in'], 'v_gate_up': out['v_gate_up'], 'v_gate_bias': out['v_gate_bias'], 'v_gla_norm_g': out['v_gla_norm_g'], 'v_w_out': out['v_w_out'], 'v_ln1_g': out['v_ln1_g'], 'v_ln1_b': out['v_ln1_b'], 'v_w_up': out['v_w_up'], 'v_conv_w': out['v_conv_w'], 'v_conv_b': out['v_conv_b'], 'v_w_down': out['v_w_down'], 'v_ln2_g': out['v_ln2_g'], 'v_ln2_b': out['v_ln2_b']}


def _loss(weights, diff, rest, loss_target):
    with _jax.named_scope("forward"):
        args = {**rest, TWIN_DIFF_INPUT: diff, **{k: w.astype(_WEIGHT_DTYPES[k]) for k, w in weights.items()}}
        y = _forward(args)
    with _jax.named_scope("loss_head"):
        err = _jnp.square(y.astype(_jnp.float32) - loss_target)
        return 0.5 * _jnp.sum(_jnp.mean(err, axis=-1)) if err.ndim else 0.5 * err


def _adamw(w, g, m, v):
    m = ADAM_B1 * m + (1.0 - ADAM_B1) * g
    v = ADAM_B2 * v + (1.0 - ADAM_B2) * _jnp.square(g)
    m_hat = m / (1.0 - ADAM_B1 ** ADAM_STEP)
    v_hat = v / (1.0 - ADAM_B2 ** ADAM_STEP)
    delta = -ADAM_LR * (m_hat / (_jnp.sqrt(v_hat) + ADAM_EPS) + ADAM_WD * w)
    return delta, m, v


def reference(x, w_in, gate_up, gate_bias, gla_norm_g, w_out, ln1_g, ln1_b, w_up, conv_w, conv_b, w_down, ln2_g, ln2_b, loss_target, m_w_in, m_gate_up, m_gate_bias, m_gla_norm_g, m_w_out, m_ln1_g, m_ln1_b, m_w_up, m_conv_w, m_conv_b, m_w_down, m_ln2_g, m_ln2_b, v_w_in, v_gate_up, v_gate_bias, v_gla_norm_g, v_w_out, v_ln1_g, v_ln1_b, v_w_up, v_conv_w, v_conv_b, v_w_down, v_ln2_g, v_ln2_b):
    given = dict(x=x, w_in=w_in, gate_up=gate_up, gate_bias=gate_bias, gla_norm_g=gla_norm_g, w_out=w_out, ln1_g=ln1_g, ln1_b=ln1_b, w_up=w_up, conv_w=conv_w, conv_b=conv_b, w_down=w_down, ln2_g=ln2_g, ln2_b=ln2_b, loss_target=loss_target, m_w_in=m_w_in, m_gate_up=m_gate_up, m_gate_bias=m_gate_bias, m_gla_norm_g=m_gla_norm_g, m_w_out=m_w_out, m_ln1_g=m_ln1_g, m_ln1_b=m_ln1_b, m_w_up=m_w_up, m_conv_w=m_conv_w, m_conv_b=m_conv_b, m_w_down=m_w_down, m_ln2_g=m_ln2_g, m_ln2_b=m_ln2_b, v_w_in=v_w_in, v_gate_up=v_gate_up, v_gate_bias=v_gate_bias, v_gla_norm_g=v_gla_norm_g, v_w_out=v_w_out, v_ln1_g=v_ln1_g, v_ln1_b=v_ln1_b, v_w_up=v_w_up, v_conv_w=v_conv_w, v_conv_b=v_conv_b, v_w_down=v_w_down, v_ln2_g=v_ln2_g, v_ln2_b=v_ln2_b)
    weights = {n: given[n] for n in TWIN_WEIGHTS}
    shared = {n: given[n] for n in SHARED_INPUTS}
    per_example = {n: given[n] for n in ['x']}
    grad_fn = _jax.value_and_grad(_loss, argnums=(0, 1))

    def one_microbatch(ex, loss_target):
        ex = dict(ex)
        diff = ex.pop(TWIN_DIFF_INPUT)
        return grad_fn(weights, diff, {**shared, **ex}, loss_target)

    if N_MICROBATCH == 1:
        loss, (grad_w, grad_x) = one_microbatch(per_example, given["loss_target"])
    else:
        def body(carry, xs):
            loss_sum, grad_sum = carry
            l_k, (gw_k, gx_k) = one_microbatch(xs[0], xs[1])
            with _jax.named_scope("update"):
                return (loss_sum + l_k, _jax.tree.map(_jnp.add, grad_sum, gw_k)), gx_k

        init = (_jnp.zeros((), _jnp.float32), _jax.tree.map(_jnp.zeros_like, weights))
        (loss, grad_w), grad_x = _jax.lax.scan(body, init, (per_example, given["loss_target"]))
    with _jax.named_scope("update"):
        delta_w, new_m, new_v = {}, {}, {}
        for n in TWIN_WEIGHTS:
            delta_w[n], new_m[n], new_v[n] = _adamw(weights[n], grad_w[n], given["m_" + n], given["v_" + n])
    return (loss, grad_x, *[grad_w[n] for n in TWIN_WEIGHTS], *[delta_w[n] for n in TWIN_WEIGHTS],
            *[new_m[n] for n in TWIN_WEIGHTS], *[new_v[n] for n in TWIN_WEIGHTS])
```

```python
import functools
import math

import jax
import jax.numpy as jnp
from jax import lax
from jax.experimental import pallas as pl
from jax.experimental.pallas import tpu as pltpu

F32 = jnp.float32
BF16 = jnp.bfloat16
MXU_DTYPE = jnp.bfloat16

N_DEV = 8
D_MODEL = 1024
SB_WIDTH = 512
SB_HEADS = 8
SB_HEAD_DIM = 64
GLA_HEADS = 4
GLA_KEY_DIM = 64
GLA_VAL_DIM = 128
GLA_WIDTH = 512
GLA_GATE_RANK = 16
GLA_TAU = 16.0
CHUNK = 64
D_FF = 2816
CONV_WIDTH = 3
LN_EPS = 1e-5
RMS_EPS = 1e-6
DN_ALPHA = 2.0 ** 0.25
IN_WIDTH = 3088
LANE = 128
IN_PAD = 3200
OFF_SBQ, OFF_SBK, OFF_SBV = 0, 512, 1024
OFF_GQ, OFF_GK, OFF_GV, OFF_GG, OFF_GA = 1536, 1792, 2048, 2560, 3072

ADAM_LR = 0.001
ADAM_B1 = 0.9
ADAM_B2 = 0.999
ADAM_EPS = 1e-08
ADAM_WD = 0.01
ADAM_STEP = 10

VMEM_LIMIT = 48 * 1024 * 1024
MESH_ID = pl.DeviceIdType.MESH


def _cparams(sem=None, **kw):
    return pltpu.CompilerParams(dimension_semantics=sem, vmem_limit_bytes=VMEM_LIMIT, **kw)


def _dot(a, b, dims):
    ca, cb = {"nn": (1, 0), "nt": (1, 1), "tn": (0, 0)}[dims]
    return lax.dot_general(a.astype(MXU_DTYPE), b.astype(MXU_DTYPE), (((ca,), (cb,)), ((), ())),
                           preferred_element_type=F32)


def _dot_split(a, b, dims):
    hi = a.astype(BF16)
    lo = (a - hi.astype(F32)).astype(BF16)
    ca, cb = {"nn": (1, 0), "nt": (1, 1), "tn": (0, 0)}[dims]
    dn = (((ca,), (cb,)), ((), ()))
    return (lax.dot_general(hi, b, dn, preferred_element_type=F32)
            + lax.dot_general(lo, b, dn, preferred_element_type=F32))


def _pick(dim, prefs):
    for p in prefs:
        if dim % p == 0:
            return p
    return dim


def _matmul(a, b, dims, out_dtype, name, res=None, res_scale=1.0):
    if dims == "nn":
        (M, K), (_, N) = a.shape, b.shape
    elif dims == "nt":
        (M, K), (N, _) = a.shape, b.shape
    else:
        (K, M), (_, N) = a.shape, b.shape
    tm = _pick(M, (512, 256, 128))
    tn = _pick(N, (640, 512, 256, 128))
    tk = _pick(K, (1024, 1408, 640, 512, 256, 128))
    nk = K // tk
    grid = (M // tm, N // tn, nk)
    if dims == "tn":
        a_spec = pl.BlockSpec((tk, tm), lambda i, j, k: (k, i))
    else:
        a_spec = pl.BlockSpec((tm, tk), lambda i, j, k: (i, k))
    if dims == "nt":
        b_spec = pl.BlockSpec((tn, tk), lambda i, j, k: (j, k))
    else:
        b_spec = pl.BlockSpec((tk, tn), lambda i, j, k: (k, j))
    o_spec = pl.BlockSpec((tm, tn), lambda i, j, k: (i, j))
    in_specs = [a_spec, b_spec]
    args = [a, b]
    if res is not None:
        in_specs.append(o_spec)
        args.append(res)

    def body(*refs):
        if res is not None:
            a_ref, b_ref, r_ref, o_ref, acc_ref = refs
        else:
            a_ref, b_ref, o_ref, acc_ref = refs
            r_ref = None
        k = pl.program_id(2)
        part = _dot(a_ref[...], b_ref[...], dims)

        def finish(total):
            if r_ref is not None:
                total = total + res_scale * r_ref[...]
            o_ref[...] = total.astype(o_ref.dtype)

        if nk == 1:
            finish(part)
        else:
            @pl.when(k == 0)
            def _():
                acc_ref[...] = part

            @pl.when(jnp.logical_and(k > 0, k < nk - 1))
            def _():
                acc_ref[...] += part

            @pl.when(k == nk - 1)
            def _():
                finish(acc_ref[...] + part)

    return pl.pallas_call(
        body, name=name, grid=grid, in_specs=in_specs, out_specs=o_spec,
        out_shape=jax.ShapeDtypeStruct((M, N), out_dtype),
        scratch_shapes=[pltpu.VMEM((tm, tn), F32)],
        compiler_params=_cparams(("parallel", "parallel", "arbitrary")),
    )(*args)


LN_ROWS = 256


def _ln_stats(r):
    mu = jnp.mean(r, axis=-1, keepdims=True)
    xc = r - mu
    var = jnp.mean(xc * xc, axis=-1, keepdims=True)
    return xc * lax.rsqrt(var + LN_EPS)


def _ln_fwd(r, g, b, name):
    S, D = r.shape

    def body(r_ref, g_ref, b_ref, h_ref):
        h_ref[...] = _ln_stats(r_ref[...]) * g_ref[...] + b_ref[...]

    row = pl.BlockSpec((LN_ROWS, D), lambda i: (i, 0))
    vec = pl.BlockSpec((1, D), lambda i: (0, 0))
    return pl.pallas_call(
        body, name=name, grid=(S // LN_ROWS,), in_specs=[row, vec, vec], out_specs=row,
        out_shape=jax.ShapeDtypeStruct((S, D), F32),
        compiler_params=_cparams(("parallel",)),
    )(r, g, b)


def _ln_bwd_core(xhat, dy, g):
    dxh = dy * g
    m1 = jnp.mean(dxh, axis=-1, keepdims=True)
    m2 = jnp.mean(dxh * xhat, axis=-1, keepdims=True)
    return dxh - m1 - xhat * m2


def _ln_bwd(r, dy, g, name):
    S, D = r.shape

    def body(r_ref, dy_ref, g_ref, dr_ref, gg_ref, gb_ref):
        x = r_ref[...]
        mu = jnp.mean(x, axis=-1, keepdims=True)
        xc = x - mu
        rstd = lax.rsqrt(jnp.mean(xc * xc, axis=-1, keepdims=True) + LN_EPS)
        xhat = xc * rstd
        dy = dy_ref[...]
        dr_ref[...] = rstd * _ln_bwd_core(xhat, dy, g_ref[...])

        @pl.when(pl.program_id(0) == 0)
        def _():
            gg_ref[...] = jnp.zeros_like(gg_ref)
            gb_ref[...] = jnp.zeros_like(gb_ref)

        gg_ref[...] += jnp.sum(dy * xhat, axis=0, keepdims=True)
        gb_ref[...] += jnp.sum(dy, axis=0, keepdims=True)

    row = pl.BlockSpec((LN_ROWS, D), lambda i: (i, 0))
    vec = pl.BlockSpec((1, D), lambda i: (0, 0))
    return pl.pallas_call(
        body, name=name, grid=(S // LN_ROWS,), in_specs=[row, row, vec], out_specs=[row, vec, vec],
        out_shape=[jax.ShapeDtypeStruct((S, D), F32), jax.ShapeDtypeStruct((1, D), F32),
                   jax.ShapeDtypeStruct((1, D), F32)],
        compiler_params=_cparams(("arbitrary",)),
    )(r, dy, g)


def _ln_loss_bwd(r, target, g, b, name):
    S, D = r.shape

    def body(r_ref, t_ref, g_ref, b_ref, dr_ref, loss_ref, gg_ref, gb_ref):
        x = r_ref[...]
        mu = jnp.mean(x, axis=-1, keepdims=True)
        xc = x - mu
        rstd = lax.rsqrt(jnp.mean(xc * xc, axis=-1, keepdims=True) + LN_EPS)
        xhat = xc * rstd
        y = xhat * g_ref[...] + b_ref[...]
        err = y - t_ref[...]
        dy = err * (1.0 / D)
        dr_ref[...] = rstd * _ln_bwd_core(xhat, dy, g_ref[...])

        @pl.when(pl.program_id(0) == 0)
        def _():
            loss_ref[...] = jnp.zeros_like(loss_ref)
            gg_ref[...] = jnp.zeros_like(gg_ref)
            gb_ref[...] = jnp.zeros_like(gb_ref)

        per_row = jnp.sum(err * err, axis=-1, keepdims=True) * (0.5 / D)
        loss_ref[...] += jnp.broadcast_to(jnp.sum(per_row, axis=0, keepdims=True), loss_ref.shape)
        gg_ref[...] += jnp.sum(dy * xhat, axis=0, keepdims=True)
        gb_ref[...] += jnp.sum(dy, axis=0, keepdims=True)

    row = pl.BlockSpec((LN_ROWS, D), lambda i: (i, 0))
    vec = pl.BlockSpec((1, D), lambda i: (0, 0))
    lvec = pl.BlockSpec((1, LANE), lambda i: (0, 0))
    return pl.pallas_call(
        body, name=name, grid=(S // LN_ROWS,), in_specs=[row, row, vec, vec],
        out_specs=[row, lvec, vec, vec],
        out_shape=[jax.ShapeDtypeStruct((S, D), F32), jax.ShapeDtypeStruct((1, LANE), F32),
                   jax.ShapeDtypeStruct((1, D), F32), jax.ShapeDtypeStruct((1, D), F32)],
        compiler_params=_cparams(("arbitrary",)),
    )(r, target, g, b)


CONV_COLS = 256
CONV_ROWS = 256
HALO = 8
INV_SQRT2 = 1.0 / math.sqrt(2.0)
INV_SQRT2PI = 1.0 / math.sqrt(2.0 * math.pi)


def _gelu(x):
    return 0.5 * x * (1.0 + lax.erf(x * INV_SQRT2))


def _gelu_grad(x):
    return 0.5 * (1.0 + lax.erf(x * INV_SQRT2)) + x * jnp.exp(-0.5 * x * x) * INV_SQRT2PI


def _conv_rows(ext, w_ref, b_ref, n):
    total = ext.shape[0]
    s1 = pltpu.roll(ext, 1, 0)
    s2 = pltpu.roll(ext, 2, 0)
    u = w_ref[2:3, :] * ext + w_ref[1:2, :] * s1 + w_ref[0:1, :] * s2 + b_ref[...]
    return u[HALO:total], s1[HALO:total], s2[HALO:total]


def _conv_gelu_fwd(u0, conv_w, conv_b, name):
    S, C2 = u0.shape
    F = C2 // 2
    ncb = F // CONV_COLS
    nrc = S // CONV_ROWS

    def body(ua_ref, uc_ref, wa_ref, wc_ref, ba_ref, bc_ref, p_ref):
        def chunk(ci, _):
            r0 = pl.multiple_of(ci * CONV_ROWS, CONV_ROWS)
            p0 = pl.multiple_of(jnp.maximum(r0 - HALO, 0), HALO)
            keep = (ci > 0).astype(F32)

            def load(ref):
                prev = ref[pl.ds(p0, HALO), :] * keep
                return jnp.concatenate([prev, ref[pl.ds(r0, CONV_ROWS), :]], axis=0)

            a, _, _ = _conv_rows(load(ua_ref), wa_ref, ba_ref, CONV_ROWS)
            c, _, _ = _conv_rows(load(uc_ref), wc_ref, bc_ref, CONV_ROWS)
            p_ref[pl.ds(r0, CONV_ROWS), :] = (_gelu(a) * c).astype(p_ref.dtype)
            return 0

        lax.fori_loop(0, nrc, chunk, 0)

    col_a = pl.BlockSpec((S, CONV_COLS), lambda j: (0, j))
    col_c = pl.BlockSpec((S, CONV_COLS), lambda j: (0, j + ncb))
    w_a = pl.BlockSpec((CONV_WIDTH, CONV_COLS), lambda j: (0, j))
    w_c = pl.BlockSpec((CONV_WIDTH, CONV_COLS), lambda j: (0, j + ncb))
    b_a = pl.BlockSpec((1, CONV_COLS), lambda j: (0, j))
    b_c = pl.BlockSpec((1, CONV_COLS), lambda j: (0, j + ncb))
    return pl.pallas_call(
        body, name=name, grid=(ncb,), in_specs=[col_a, col_c, w_a, w_c, b_a, b_c], out_specs=col_a,
        out_shape=jax.ShapeDtypeStruct((S, F), MXU_DTYPE),
        compiler_params=_cparams(("parallel",)),
    )(u0, u0, conv_w, conv_w, conv_b, conv_b)


def _conv_gelu_bwd(u0, dp, conv_w, conv_b, name):
    S, C2 = u0.shape
    F = C2 // 2
    ncb = F // CONV_COLS
    nrc = S // CONV_ROWS
    EXT = CONV_ROWS + HALO

    def body(ua_ref, uc_ref, dp_ref, wa_ref, wc_ref, ba_ref, bc_ref,
             da_ref, dc_ref, gwa_ref, gwc_ref, gba_ref, gbc_ref):
        gwa_ref[...] = jnp.zeros_like(gwa_ref)
        gwc_ref[...] = jnp.zeros_like(gwc_ref)
        gba_ref[...] = jnp.zeros_like(gba_ref)
        gbc_ref[...] = jnp.zeros_like(gbc_ref)
        rid = lax.broadcasted_iota(jnp.int32, (EXT, CONV_COLS), 0)

        def chunk(ci, _):
            r0 = pl.multiple_of(ci * CONV_ROWS, CONV_ROWS)
            p0 = pl.multiple_of(jnp.maximum(r0 - HALO, 0), HALO)
            n0 = pl.multiple_of(jnp.minimum(r0 + CONV_ROWS, S - HALO), HALO)
            keep_prev = (ci > 0).astype(F32)
            keep_next = (ci < nrc - 1).astype(F32)

            def load(ref):
                return jnp.concatenate([ref[pl.ds(p0, HALO), :] * keep_prev,
                                        ref[pl.ds(r0, CONV_ROWS), :],
                                        ref[pl.ds(n0, HALO), :] * keep_next], axis=0)

            ext_a = load(ua_ref)
            ext_c = load(uc_ref)
            a, a1, a2 = _conv_rows(ext_a, wa_ref, ba_ref, EXT)
            c, c1, c2 = _conv_rows(ext_c, wc_ref, bc_ref, EXT)
            a0 = ext_a[HALO:HALO + EXT]
            c0 = ext_c[HALO:HALO + EXT]
            dpe = jnp.concatenate([dp_ref[pl.ds(r0, CONV_ROWS), :].astype(F32),
                                   dp_ref[pl.ds(n0, HALO), :].astype(F32) * keep_next], axis=0)
            d_a = dpe * c * _gelu_grad(a)
            d_c = dpe * _gelu(a)
            own = rid < CONV_ROWS

            def back(d_u, w_ref, x0, x1, x2, d_ref, gw_ref, gb_ref):
                d_u0 = (w_ref[2:3, :] * d_u + w_ref[1:2, :] * pltpu.roll(d_u, EXT - 1, 0)
                        + w_ref[0:1, :] * pltpu.roll(d_u, EXT - 2, 0))
                d_ref[pl.ds(r0, CONV_ROWS), :] = d_u0[0:CONV_ROWS].astype(d_ref.dtype)
                d_own = jnp.where(own, d_u, 0.0)
                gw_ref[...] += jnp.concatenate(
                    [jnp.sum(d_own * x2, axis=0, keepdims=True),
                     jnp.sum(d_own * x1, axis=0, keepdims=True),
                     jnp.sum(d_own * x0, axis=0, keepdims=True)], axis=0)
                gb_ref[...] += jnp.sum(d_own, axis=0, keepdims=True)

            back(d_a, wa_ref, a0, a1, a2, da_ref, gwa_ref, gba_ref)
            back(d_c, wc_ref, c0, c1, c2, dc_ref, gwc_ref, gbc_ref)
            return 0

        lax.fori_loop(0, nrc, chunk, 0)

    col_a = pl.BlockSpec((S, CONV_COLS), lambda j: (0, j))
    col_c = pl.BlockSpec((S, CONV_COLS), lambda j: (0, j + ncb))
    w_a = pl.BlockSpec((CONV_WIDTH, CONV_COLS), lambda j: (0, j))
    w_c = pl.BlockSpec((CONV_WIDTH, CONV_COLS), lambda j: (0, j + ncb))
    b_a = pl.BlockSpec((1, CONV_COLS), lambda j: (0, j))
    b_c = pl.BlockSpec((1, CONV_COLS), lambda j: (0, j + ncb))
    outs = pl.pallas_call(
        body, name=name, grid=(ncb,),
        in_specs=[col_a, col_c, col_a, w_a, w_c, b_a, b_c],
        out_specs=[col_a, col_a, w_a, w_a, b_a, b_a],
        out_shape=[jax.ShapeDtypeStruct((S, F), MXU_DTYPE), jax.ShapeDtypeStruct((S, F), MXU_DTYPE),
                   jax.ShapeDtypeStruct((CONV_WIDTH, F), F32), jax.ShapeDtypeStruct((CONV_WIDTH, F), F32),
                   jax.ShapeDtypeStruct((1, F), F32), jax.ShapeDtypeStruct((1, F), F32)],
        compiler_params=_cparams(("parallel",)),
    )(u0, u0, dp, conv_w, conv_w, conv_b, conv_b)
    da, dc, gwa, gwc, gba, gbc = outs
    return (jnp.concatenate([da, dc], axis=1), jnp.concatenate([gwa, gwc], axis=1),
            jnp.concatenate([gba, gbc], axis=1))


SB_BLOCK = 128


def _softplus(z):
    return jnp.maximum(z, 0.0) + jnp.log(1.0 + jnp.exp(-jnp.abs(z)))


def _sb_fwd(q, k, v, name):
    H, S, Dh = q.shape
    T = SB_BLOCK
    scale = Dh ** -0.5

    def body(q_ref, k_ref, v_ref, o_ref, t_ref):
        i = pl.program_id(1)
        qb = q_ref[0]
        row = lax.broadcasted_iota(jnp.int32, (T, T), 0)
        col = lax.broadcasted_iota(jnp.int32, (T, T), 1)
        after = jnp.where(row > col, 1.0, 0.0).astype(BF16)

        def step(n, carry):
            acc, tail = carry
            j = i - n
            k0 = pl.multiple_of(j * T, T)
            kb = k_ref[0, pl.ds(k0, T), :]
            vb = v_ref[0, pl.ds(k0, T), :]
            z = _dot(qb, kb, "nt") * scale
            strict = (col + j * T) < (row + i * T)
            sp = _softplus(z)
            l1m = jnp.where(strict, -sp, 0.0)
            suffix = _dot_split(l1m, after, "nn") + tail
            w = jnp.where(strict, jnp.exp(z - sp + suffix), 0.0)
            acc = acc + _dot(w, vb, "nn")
            tail = tail + jnp.sum(l1m, axis=1, keepdims=True)
            return acc, tail

        acc, tail = lax.fori_loop(0, i + 1, step, (jnp.zeros((T, Dh), F32), jnp.zeros((T, 1), F32)))
        o_ref[0] = acc
        t_ref[0] = tail

    qspec = pl.BlockSpec((1, T, Dh), lambda h, i: (h, i, 0))
    full = pl.BlockSpec((1, S, Dh), lambda h, i: (h, 0, 0))
    tspec = pl.BlockSpec((1, T, 1), lambda h, i: (h, i, 0))
    return pl.pallas_call(
        body, name=name, grid=(H, S // T), in_specs=[qspec, full, full], out_specs=[qspec, tspec],
        out_shape=[jax.ShapeDtypeStruct((H, S, Dh), F32), jax.ShapeDtypeStruct((H, S, 1), F32)],
        compiler_params=_cparams(("parallel", "arbitrary")),
    )(q, k, v)


def _sb_bwd(q, k, v, do, total, name):
    H, S, Dh = q.shape
    T = SB_BLOCK
    scale = Dh ** -0.5

    def body(q_ref, k_ref, v_ref, do_ref, t_ref, dq_ref, dk_ref, dv_ref):
        i = pl.program_id(1)

        @pl.when(i == 0)
        def _():
            dk_ref[...] = jnp.zeros_like(dk_ref)
            dv_ref[...] = jnp.zeros_like(dv_ref)

        qb = q_ref[0]
        dob = do_ref[0].astype(MXU_DTYPE)
        tot = t_ref[0]
        row = lax.broadcasted_iota(jnp.int32, (T, T), 0)
        col = lax.broadcasted_iota(jnp.int32, (T, T), 1)
        after = jnp.where(row > col, 1.0, 0.0).astype(BF16)
        before = jnp.where(row < col, 1.0, 0.0).astype(BF16)

        def step(j, carry):
            dq, seen, gsum = carry
            k0 = pl.multiple_of(j * T, T)
            kb = k_ref[0, pl.ds(k0, T), :]
            vb = v_ref[0, pl.ds(k0, T), :]
            z = _dot(qb, kb, "nt") * scale
            strict = (col + j * T) < (row + i * T)
            sp = _softplus(z)
            l1m = jnp.where(strict, -sp, 0.0)
            seen = seen + jnp.sum(l1m, axis=1, keepdims=True)
            suffix = _dot_split(l1m, after, "nn") + (tot - seen)
            logsig = z - sp
            w = jnp.where(strict, jnp.exp(logsig + suffix), 0.0)
            dw = _dot(dob, vb, "nt")
            g = w * dw
            gpre = _dot_split(g, before, "nn") + gsum
            sig = jnp.exp(logsig)
            dz = jnp.where(strict, g * (1.0 - sig) - gpre * sig, 0.0) * scale
            dzb = dz.astype(MXU_DTYPE)
            dq = dq + _dot(dzb, kb, "nn")
            dk_ref[0, pl.ds(k0, T), :] += _dot(dzb, qb, "tn")
            dv_ref[0, pl.ds(k0, T), :] += _dot(w, dob, "tn")
            gsum = gsum + jnp.sum(g, axis=1, keepdims=True)
            return dq, seen, gsum

        zero = jnp.zeros((T, 1), F32)
        dq, _, _ = lax.fori_loop(0, i + 1, step, (jnp.zeros((T, Dh), F32), zero, zero))
        dq_ref[0] = dq

    qspec = pl.BlockSpec((1, T, Dh), lambda h, i: (h, i, 0))
    full = pl.BlockSpec((1, S, Dh), lambda h, i: (h, 0, 0))
    tspec = pl.BlockSpec((1, T, 1), lambda h, i: (h, i, 0))
    sds = jax.ShapeDtypeStruct((H, S, Dh), F32)
    return pl.pallas_call(
        body, name=name, grid=(H, S // T), in_specs=[qspec, full, full, qspec, tspec],
        out_specs=[qspec, full, full], out_shape=[sds, sds, sds],
        compiler_params=_cparams(("parallel", "arbitrary")),
    )(q, k, v, do, total)


GLA_GROUP = 2


def _gla_masks():
    C = CHUNK
    row = lax.broadcasted_iota(jnp.int32, (C, C), 0)
    col = lax.broadcasted_iota(jnp.int32, (C, C), 1)
    return row, col


def _log_sigmoid(x):
    return -_softplus(-x)


def _gla_chunk_fwd(qc, kc, vc, gate, row, col):
    C = CHUNK
    la = _log_sigmoid(gate) * (1.0 / GLA_TAU)
    incl = jnp.where(row >= col, 1.0, 0.0).astype(BF16)
    b = _dot_split_lhs01(incl, la)
    b_ref = jnp.sum(jnp.where(row == C // 2 - 1, b, 0.0), axis=0, keepdims=True)
    b_last = jnp.sum(la, axis=0, keepdims=True)
    qs = qc * (GLA_KEY_DIM ** -0.5)
    q_in = qs * jnp.exp(b - b_ref)
    k_in = kc * jnp.exp(b_ref - b)
    k_dec = kc * jnp.exp(b_last - b)
    q_b = qs * jnp.exp(b)
    sc = jnp.where(row >= col, _dot(q_in, k_in, "nt"), 0.0)
    o_intra = _dot(sc, vc, "nn")
    upd = _dot(k_dec, vc, "tn")
    ones = jnp.ones((C, GLA_VAL_DIM), BF16)
    dec_col = jnp.exp(_dot_split_tn(la, ones))
    return dict(la=la, b=b, b_ref=b_ref, b_last=b_last, qs=qs, q_in=q_in, k_in=k_in, k_dec=k_dec,
                q_b=q_b, sc=sc, o_intra=o_intra, upd=upd, dec_col=dec_col)


def _dot_split_lhs01(m01, x):
    hi = x.astype(BF16)
    lo = (x - hi.astype(F32)).astype(BF16)
    dn = (((1,), (0,)), ((), ()))
    return (lax.dot_general(m01, hi, dn, preferred_element_type=F32)
            + lax.dot_general(m01, lo, dn, preferred_element_type=F32))


def _dot_split_tn(x, m01):
    hi = x.astype(BF16)
    lo = (x - hi.astype(F32)).astype(BF16)
    dn = (((0,), (0,)), ((), ()))
    return (lax.dot_general(hi, m01, dn, preferred_element_type=F32)
            + lax.dot_general(lo, m01, dn, preferred_element_type=F32))


def _dot_split_nt01(m01, x):
    hi = x.astype(BF16)
    lo = (x - hi.astype(F32)).astype(BF16)
    dn = (((1,), (1,)), ((), ()))
    return (lax.dot_general(m01, hi, dn, preferred_element_type=F32)
            + lax.dot_general(m01, lo, dn, preferred_element_type=F32))


def _rms_gate(o, gg, gnorm):
    rinv = lax.rsqrt(jnp.mean(o * o, axis=-1, keepdims=True) + RMS_EPS)
    o_n = o * rinv
    sg = 1.0 / (1.0 + jnp.exp(-gg))
    return o_n, rinv, sg


def _gla_fwd(gq, gk, gv, gg, ga_pad, gate_up_h, gate_bias_h, gnorm, name):
    Hg, S, dk = gq.shape
    dv = GLA_VAL_DIM
    C = CHUNK
    G = GLA_GROUP
    nchunk = S // C
    ngroup = nchunk // G

    def body(q_ref, k_ref, v_ref, gg_ref, ga_ref, gu_ref, gb_ref, gn_ref, o_ref, prev_ref):
        row, col = _gla_masks()
        gu = gu_ref[0]
        gbias = gb_ref[0]
        gnorm_v = gn_ref[...]

        def group(gi, state):
            for u in range(G):
                ci = gi * G + u
                r0 = pl.multiple_of(ci * C, C)
                rows = pl.ds(r0, C)
                gate = _dot(ga_ref[rows, :], gu, "nn") + gbias
                f = _gla_chunk_fwd(q_ref[0, rows, :], k_ref[0, rows, :], v_ref[rows, :], gate, row, col)
                prev_ref[0, ci] = state
                o = f["o_intra"] + _dot(f["q_b"], state, "nn")
                state = f["dec_col"] * state + f["upd"]
                o_n, _, sg = _rms_gate(o, gg_ref[rows, :], gnorm_v)
                o_ref[rows, :] = o_n * gnorm_v * (gg_ref[rows, :] * sg)
            return state

        lax.fori_loop(0, ngroup, group, jnp.zeros((dk, dv), F32))

    hspec = pl.BlockSpec((1, S, dk), lambda h: (h, 0, 0))
    vspec = pl.BlockSpec((S, dv), lambda h: (0, h))
    return pl.pallas_call(
        body, name=name, grid=(Hg,),
        in_specs=[hspec, hspec, vspec, vspec,
                  pl.BlockSpec((S, LANE), lambda h: (0, 0)),
                  pl.BlockSpec((1, LANE, dk), lambda h: (h, 0, 0)),
                  pl.BlockSpec((1, 1, dk), lambda h: (h, 0, 0)),
                  pl.BlockSpec((1, dv), lambda h: (0, 0))],
        out_specs=[vspec, pl.BlockSpec((1, nchunk, dk, dv), lambda h: (h, 0, 0, 0))],
        out_shape=[jax.ShapeDtypeStruct((S, Hg * dv), F32),
                   jax.ShapeDtypeStruct((Hg, nchunk, dk, dv), F32)],
        compiler_params=_cparams(("parallel",)),
    )(gq, gk, gv, gg, ga_pad, gate_up_h, gate_bias_h, gnorm)


def _gla_bwd(gq, gk, gv, gg, ga_pad, gate_up_h, gate_bias_h, gnorm, prev, d_out, name):
    Hg, S, dk = gq.shape
    dv = GLA_VAL_DIM
    C = CHUNK
    G = GLA_GROUP
    nchunk = S // C
    ngroup = nchunk // G

    def body(q_ref, k_ref, v_ref, gg_ref, ga_ref, gu_ref, gb_ref, gn_ref, prev_ref, do_ref,
             dq_ref, dk_ref, dv_ref, dgg_ref, dga_ref, ggu_ref, ggb_ref, ggn_ref):
        h = pl.program_id(0)
        row, col = _gla_masks()
        gu = gu_ref[0]
        gbias = gb_ref[0]
        gnorm_v = gn_ref[...]
        ggu_ref[...] = jnp.zeros_like(ggu_ref)
        ggb_ref[...] = jnp.zeros_like(ggb_ref)

        @pl.when(h == 0)
        def _():
            dga_ref[...] = jnp.zeros_like(dga_ref)
            ggn_ref[...] = jnp.zeros_like(ggn_ref)

        upper_incl = jnp.where(col >= row, 1.0, 0.0).astype(BF16)
        ones_8 = jnp.ones((8, dv), BF16)

        def group(gn, dstate):
            gi = ngroup - 1 - gn
            for u in reversed(range(G)):
                ci = gi * G + u
                r0 = pl.multiple_of(ci * C, C)
                rows = pl.ds(r0, C)
                ga = ga_ref[rows, :]
                gate = _dot(ga, gu, "nn") + gbias
                qc, kc, vc = q_ref[0, rows, :], k_ref[0, rows, :], v_ref[rows, :]
                f = _gla_chunk_fwd(qc, kc, vc, gate, row, col)
                state = prev_ref[0, ci]
                o = f["o_intra"] + _dot(f["q_b"], state, "nn")
                ggv = gg_ref[rows, :]
                o_n, rinv, sg = _rms_gate(o, ggv, gnorm_v)
                dout = do_ref[rows, :]
                silu = ggv * sg
                dgg_ref[rows, :] = dout * o_n * gnorm_v * (sg * (1.0 + ggv * (1.0 - sg)))
                d_ong = dout * silu
                ggn_ref[...] += jnp.sum(d_ong * o_n, axis=0, keepdims=True)
                d_on = d_ong * gnorm_v
                d_o = rinv * (d_on - o_n * jnp.mean(d_on * o_n, axis=-1, keepdims=True))
                d_upd = dstate
                d_dec_col = dstate * state * f["dec_col"]
                dstate = f["dec_col"] * dstate + _dot(f["q_b"], d_o, "tn")
                dsc = jnp.where(row >= col, _dot(d_o, vc, "nt"), 0.0)
                dv_ref[rows, :] = _dot(f["sc"], d_o, "tn") + _dot(f["k_dec"], d_upd, "nn")
                dq_in = _dot(dsc, f["k_in"], "nn")
                dk_in = _dot(dsc, f["q_in"], "tn")
                dq_b = _dot(d_o, state, "nt")
                dkdec = _dot(vc, d_upd, "nt")
                b = f["b"]
                e1 = jnp.exp(b - f["b_ref"])
                e2 = jnp.exp(f["b_ref"] - b)
                e3 = jnp.exp(f["b_last"] - b)
                eb = jnp.exp(b)
                dq_ref[0, rows, :] = (dq_in * e1 + dq_b * eb) * (GLA_KEY_DIM ** -0.5)
                dk_ref[0, rows, :] = dk_in * e2 + dkdec * e3
                t_q = dq_in * f["q_in"]
                t_k = dk_in * f["k_in"]
                t_d = dkdec * f["k_dec"]
                db = t_q - t_k - t_d + dq_b * f["q_b"]
                db_ref = jnp.sum(t_k - t_q, axis=0, keepdims=True)
                db_last = (jnp.sum(t_d, axis=0, keepdims=True)
                           + jnp.max(_dot_split_nt01(ones_8, d_dec_col), axis=0, keepdims=True))
                db = db + jnp.where(row == C // 2 - 1, db_ref, 0.0) + jnp.where(row == C - 1, db_last, 0.0)
                dla = _dot_split_lhs01(upper_incl, db)
                d_gate = dla * (1.0 / GLA_TAU) * (1.0 / (1.0 + jnp.exp(gate)))
                ggb_ref[0] += jnp.sum(d_gate, axis=0, keepdims=True)
                ggu_ref[0] += _dot(ga, d_gate, "tn")
                dga_ref[rows, :] += _dot(d_gate, gu, "nt")
            return dstate

        lax.fori_loop(0, ngroup, group, jnp.zeros((dk, dv), F32))

    hspec = pl.BlockSpec((1, S, dk), lambda h: (h, 0, 0))
    vspec = pl.BlockSpec((S, dv), lambda h: (0, h))
    gaspec = pl.BlockSpec((S, LANE), lambda h: (0, 0))
    guspec = pl.BlockSpec((1, LANE, dk), lambda h: (h, 0, 0))
    gbspec = pl.BlockSpec((1, 1, dk), lambda h: (h, 0, 0))
    gnspec = pl.BlockSpec((1, dv), lambda h: (0, 0))
    return pl.pallas_call(
        body, name=name, grid=(Hg,),
        in_specs=[hspec, hspec, vspec, vspec, gaspec, guspec, gbspec,
                  pl.BlockSpec((1, dv), lambda h: (0, 0)),
                  pl.BlockSpec((1, nchunk, dk, dv), lambda h: (h, 0, 0, 0)), vspec],
        out_specs=[hspec, hspec, vspec, vspec, gaspec, guspec, gbspec, gnspec],
        out_shape=[jax.ShapeDtypeStruct((Hg, S, dk), F32), jax.ShapeDtypeStruct((Hg, S, dk), F32),
                   jax.ShapeDtypeStruct((S, Hg * dv), F32), jax.ShapeDtypeStruct((S, Hg * dv), F32),
                   jax.ShapeDtypeStruct((S, LANE), F32), jax.ShapeDtypeStruct((Hg, LANE, dk), F32),
                   jax.ShapeDtypeStruct((Hg, 1, dk), F32), jax.ShapeDtypeStruct((1, dv), F32)],
        compiler_params=_cparams(("arbitrary",)),
    )(gq, gk, gv, gg, ga_pad, gate_up_h, gate_bias_h, gnorm, prev, d_out)


def _exchange(items, name):
    n = len(items)
    n_peer = N_DEV - 1

    def body(*refs):
        ins, outs = refs[:n], refs[n:2 * n]
        send_sems, recv_sems, local_sems = refs[2 * n:]
        x, y, c = lax.axis_index("x"), lax.axis_index("y"), lax.axis_index("c")
        me = 4 * x + 2 * y + c
        local, remote = [], []
        for a, (_, scatter) in enumerate(items):
            own = ins[a].at[me] if scatter else ins[a]
            cp = pltpu.make_async_copy(own, outs[a].at[me], local_sems.at[a])
            cp.start()
            local.append(cp)
        for r in range(1, N_DEV):
            px = 1 - x if r & 4 else x
            py = 1 - y if r & 2 else y
            pc = 1 - c if r & 1 else c
            for a, (_, scatter) in enumerate(items):
                src = ins[a].at[4 * px + 2 * py + pc] if scatter else ins[a]
                cp = pltpu.make_async_remote_copy(
                    src_ref=src, dst_ref=outs[a].at[me],
                    send_sem=send_sems.at[a * n_peer + r - 1], recv_sem=recv_sems.at[a * n_peer + r - 1],
                    device_id=(px, py, pc), device_id_type=MESH_ID)
                cp.start()
                remote.append(cp)
        for cp in remote:
            cp.wait()
        for cp in local:
            cp.wait()

    out_shape = []
    for arr, scatter in items:
        shp = arr.shape if scatter else (N_DEV,) + arr.shape
        out_shape.append(jax.ShapeDtypeStruct(shp, arr.dtype))
    any_spec = pl.BlockSpec(memory_space=pl.ANY)
    return pl.pallas_call(
        body, name=name, in_specs=[any_spec] * n, out_specs=[any_spec] * n, out_shape=out_shape,
        scratch_shapes=[pltpu.SemaphoreType.DMA((n * n_peer,)), pltpu.SemaphoreType.DMA((n * n_peer,)),
                        pltpu.SemaphoreType.DMA((n,))],
        compiler_params=pltpu.CompilerParams(has_side_effects=True),
    )(*[arr for arr, _ in items])


def _adamw(grecv, w, m, v, tile, name):
    R = w.shape[0]
    bc1 = 1.0 - ADAM_B1 ** ADAM_STEP
    bc2 = 1.0 - ADAM_B2 ** ADAM_STEP

    def body(gr_ref, w_ref, m_ref, v_ref, g_ref, d_ref, nm_ref, nv_ref):
        g = gr_ref[0].astype(F32)
        for q in range(1, N_DEV):
            g = g + gr_ref[q].astype(F32)
        nm = ADAM_B1 * m_ref[...] + (1.0 - ADAM_B1) * g
        nv = ADAM_B2 * v_ref[...] + (1.0 - ADAM_B2) * (g * g)
        m_hat = nm / bc1
        v_hat = nv / bc2
        g_ref[...] = g
        d_ref[...] = -ADAM_LR * (m_hat / (jnp.sqrt(v_hat) + ADAM_EPS) + ADAM_WD * w_ref[...])
        nm_ref[...] = nm
        nv_ref[...] = nv

    blk = pl.BlockSpec((tile, LANE), lambda i: (i, 0))
    sds = jax.ShapeDtypeStruct((R, LANE), F32)
    return pl.pallas_call(
        body, name=name, grid=(R // tile,),
        in_specs=[pl.BlockSpec((N_DEV, tile, LANE), lambda i: (0, i, 0)), blk, blk, blk],
        out_specs=[blk, blk, blk, blk], out_shape=[sds, sds, sds, sds],
        compiler_params=_cparams(("parallel",)),
    )(grecv, w, m, v)


BIG_TILE = 512


def _pack_rows(parts, rows, dtype):
    flat = jnp.concatenate([p.reshape(-1).astype(dtype) for p in parts])
    return jnp.pad(flat, (0, rows * LANE - flat.shape[0])).reshape(rows, LANE)


def _unpack_rows(buf, shapes):
    flat = buf.reshape(-1)
    out, off = [], 0
    for shp in shapes:
        n = math.prod(shp)
        out.append(flat[off:off + n].reshape(shp))
        off += n
    return out


def _shard_cols(g):
    rows, cols = g.shape
    return g.reshape(rows, N_DEV, cols // N_DEV).transpose(1, 0, 2).reshape(N_DEV, -1)


def _unshard_cols(flat, rows):
    return flat.reshape(N_DEV, rows, -1).transpose(1, 0, 2).reshape(rows, -1)


def _heads(t, n, d):
    return t.reshape(t.shape[0], n, d).transpose(1, 0, 2)


def _unheads(t):
    return t.transpose(1, 0, 2).reshape(t.shape[1], -1)


def kernel(x, w_in, gate_up, gate_bias, gla_norm_g, w_out, ln1_g, ln1_b, w_up, conv_w, conv_b, w_down, ln2_g, ln2_b, loss_target, m_w_in, m_gate_up, m_gate_bias, m_gla_norm_g, m_w_out, m_ln1_g, m_ln1_b, m_w_up, m_conv_w, m_conv_b, m_w_down, m_ln2_g, m_ln2_b, v_w_in, v_gate_up, v_gate_bias, v_gla_norm_g, v_w_out, v_ln1_g, v_ln1_b, v_w_up, v_conv_w, v_conv_b, v_w_down, v_ln2_g, v_ln2_b):
    S, D = x.shape[1], x.shape[2]
    x2, tgt = x[0], loss_target[0]
    n_in, n_up, n_gu = w_in.shape[2], w_up.shape[2], gate_up.shape[2]
    r_out, r_down = w_out.shape[1], w_down.shape[1]
    F2 = N_DEV * n_up

    big_mm = [w_in[0], w_out[0], w_up[0], w_down[0]]
    n_mm = sum(p.size for p in big_mm)
    wpack = _pack_rows(big_mm, -(-n_mm // (16 * LANE)) * 16, MXU_DTYPE)
    spack = _pack_rows([gate_up[0], conv_w[0]], 24, F32)
    wall, sall = _exchange([(wpack, False), (spack, False)], "gather_weights")
    wflat = wall.reshape(N_DEV, -1)
    sizes = [D * n_in, r_out * D, D * n_up, r_down * D]
    offs = [0, sizes[0], sizes[0] + sizes[1], sizes[0] + sizes[1] + sizes[2]]
    w_in_f = _unshard_cols(wflat[:, offs[0]:offs[0] + sizes[0]], D)
    w_out_f = wflat[:, offs[1]:offs[1] + sizes[1]].reshape(N_DEV * r_out, D)
    w_up_f = _unshard_cols(wflat[:, offs[2]:offs[2] + sizes[2]], D)
    w_down_f = wflat[:, offs[3]:offs[3] + sizes[3]].reshape(N_DEV * r_down, D)
    sflat = sall.reshape(N_DEV, -1)
    gate_up_f = _unshard_cols(sflat[:, :GLA_GATE_RANK * n_gu], GLA_GATE_RANK)
    conv_w_f = _unshard_cols(sflat[:, GLA_GATE_RANK * n_gu:GLA_GATE_RANK * n_gu + CONV_WIDTH * n_up],
                             CONV_WIDTH)
    w_in_pad = jnp.pad(w_in_f, ((0, 0), (0, IN_PAD - IN_WIDTH)))
    gate_up_h = _heads(jnp.pad(gate_up_f, ((0, LANE - GLA_GATE_RANK), (0, 0))), GLA_HEADS, GLA_KEY_DIM)
    gate_bias_h = gate_bias.reshape(GLA_HEADS, 1, GLA_KEY_DIM)

    proj = _matmul(x2, w_in_pad, "nn", F32, "proj")
    sq = _heads(proj[:, OFF_SBQ:OFF_SBK], SB_HEADS, SB_HEAD_DIM).astype(MXU_DTYPE)
    sk = _heads(proj[:, OFF_SBK:OFF_SBV], SB_HEADS, SB_HEAD_DIM).astype(MXU_DTYPE)
    sv = _heads(proj[:, OFF_SBV:OFF_GQ], SB_HEADS, SB_HEAD_DIM).astype(MXU_DTYPE)
    sb_o_h, sb_tot = _sb_fwd(sq, sk, sv, "sb_fwd")
    gq = _heads(proj[:, OFF_GQ:OFF_GK], GLA_HEADS, GLA_KEY_DIM)
    gk = _heads(proj[:, OFF_GK:OFF_GV], GLA_HEADS, GLA_KEY_DIM)
    gv, gg, ga_pad = proj[:, OFF_GV:OFF_GG], proj[:, OFF_GG:OFF_GA], proj[:, OFF_GA:IN_PAD]
    gla_o, prev = _gla_fwd(gq, gk, gv, gg, ga_pad, gate_up_h, gate_bias_h, gla_norm_g, "gla_fwd")
    cat = jnp.concatenate([_unheads(sb_o_h), gla_o], axis=1)
    r1 = _matmul(cat, w_out_f, "nn", F32, "mix", res=x2, res_scale=DN_ALPHA)
    h = _ln_fwd(r1, ln1_g, ln1_b, "ln1")
    u0 = _matmul(h, w_up_f, "nn", F32, "ffn_up")
    p = _conv_gelu_fwd(u0, conv_w_f, conv_b, "conv_gelu")
    r2 = _matmul(p, w_down_f, "nn", F32, "ffn_down", res=h, res_scale=DN_ALPHA)
    d_r2, loss_p, g_ln2_g, g_ln2_b = _ln_loss_bwd(r2, tgt, ln2_g, ln2_b, "ln2_loss")

    d_p = _matmul(d_r2, w_down_f, "nt", MXU_DTYPE, "d_ffn_act")
    g_w_down = _matmul(p, d_r2, "tn", F32, "grad_w_down")
    d_u0, g_conv_w, g_conv_b = _conv_gelu_bwd(u0, d_p, conv_w_f, conv_b, "conv_gelu_bwd")
    g_w_up = _matmul(h, d_u0, "tn", F32, "grad_w_up")
    d_h = _matmul(d_u0, w_up_f, "nt", F32, "d_h", res=d_r2, res_scale=DN_ALPHA)
    d_r1, g_ln1_g, g_ln1_b = _ln_bwd(r1, d_h, ln1_g, "ln1_bwd")
    g_w_out = _matmul(cat, d_r1, "tn", F32, "grad_w_out")
    d_cat = _matmul(d_r1, w_out_f, "nt", F32, "d_cat")
    (d_gq, d_gk, d_gv, d_gg, d_ga_pad, g_gu_h, g_gb_h, g_gnorm) = _gla_bwd(
        gq, gk, gv, gg, ga_pad, gate_up_h, gate_bias_h, gla_norm_g, prev, d_cat[:, SB_WIDTH:], "gla_bwd")
    d_sq, d_sk, d_sv = _sb_bwd(sq, sk, sv, _heads(d_cat[:, :SB_WIDTH], SB_HEADS, SB_HEAD_DIM), sb_tot,
                               "sb_bwd")
    d_proj = jnp.concatenate([_unheads(d_sq), _unheads(d_sk), _unheads(d_sv), _unheads(d_gq), _unheads(d_gk),
                              d_gv, d_gg, d_ga_pad], axis=1)
    g_w_in = _matmul(x2, d_proj, "tn", F32, "grad_w_in")[:, :IN_WIDTH]
    d_x = _matmul(d_proj, w_in_pad, "nt", F32, "d_x", res=d_r1, res_scale=DN_ALPHA)
    g_gate_up = _unheads(g_gu_h[:, :GLA_GATE_RANK, :])
    g_gate_bias = g_gb_h.reshape(1, -1)

    big_shapes = [w_in.shape, w_out.shape, w_up.shape, w_down.shape, gate_up.shape, conv_w.shape]
    n_big = sum(math.prod(s) for s in big_shapes)
    rows_big = -(-n_big // (BIG_TILE * LANE)) * BIG_TILE
    gbig = jnp.concatenate([_shard_cols(g_w_in), g_w_out.reshape(N_DEV, -1), _shard_cols(g_w_up),
                            g_w_down.reshape(N_DEV, -1), _shard_cols(g_gate_up), _shard_cols(g_conv_w)],
                           axis=1)
    gbig = jnp.pad(gbig, ((0, 0), (0, rows_big * LANE - n_big))).astype(BF16).reshape(N_DEV, rows_big, LANE)
    small_g = [g_gate_bias, g_gnorm, g_ln1_g, g_ln1_b, g_conv_b, g_ln2_g, g_ln2_b, loss_p]
    n_small = sum(t.size for t in small_g)
    rows_small = -(-n_small // (8 * LANE)) * 8
    gsmall = _pack_rows(small_g, rows_small, F32)
    grecv, srecv = _exchange([(gbig, True), (gsmall, False)], "exchange_grads")

    def pk(parts):
        return _pack_rows(parts, rows_big, F32)

    big = _adamw(grecv, pk([w_in, w_out, w_up, w_down, gate_up, conv_w]),
                 pk([m_w_in, m_w_out, m_w_up, m_w_down, m_gate_up, m_conv_w]),
                 pk([v_w_in, v_w_out, v_w_up, v_w_down, v_gate_up, v_conv_w]), BIG_TILE, "adamw_sharded")
    zl = jnp.zeros((1, LANE), F32)

    def pks(parts):
        return _pack_rows(parts + [zl], rows_small, F32)

    small = _adamw(srecv, pks([gate_bias, gla_norm_g, ln1_g, ln1_b, conv_b, ln2_g, ln2_b]),
                   pks([m_gate_bias, m_gla_norm_g, m_ln1_g, m_ln1_b, m_conv_b, m_ln2_g, m_ln2_b]),
                   pks([v_gate_bias, v_gla_norm_g, v_ln1_g, v_ln1_b, v_conv_b, v_ln2_g, v_ln2_b]),
                   rows_small, "adamw_replicated")
    small_shapes = [gate_bias.shape, gla_norm_g.shape, ln1_g.shape, ln1_b.shape, conv_b.shape, ln2_g.shape,
                    ln2_b.shape, (1, LANE)]
    outs = []
    loss = None
    for kind in range(4):
        b_w_in, b_w_out, b_w_up, b_w_down, b_gate_up, b_conv_w = _unpack_rows(big[kind], big_shapes)
        s_gb, s_gn, s_l1g, s_l1b, s_cb, s_l2g, s_l2b, s_loss = _unpack_rows(small[kind], small_shapes)
        if kind == 0:
            loss = s_loss[0, 0]
        outs += [b_w_in, b_gate_up, s_gb, s_gn, b_w_out, s_l1g, s_l1b, b_w_up, b_conv_w, s_cb, b_w_down,
                 s_l2g, s_l2b]
    return (loss, d_x[None], *outs)
```

```python
import math

import jax
import jax.numpy as jnp
from jax import lax
from jax.experimental import pallas as pl
from jax.experimental.pallas import tpu as pltpu

F32 = jnp.float32
BF16 = jnp.bfloat16
MXU_DTYPE = jnp.bfloat16

N_DEV = 8
D_MODEL = 1024
SB_WIDTH = 512
SB_HEADS = 8
SB_HEAD_DIM = 64
GLA_HEADS = 4
GLA_KEY_DIM = 64
GLA_VAL_DIM = 128
GLA_WIDTH = 512
GLA_GATE_RANK = 16
GLA_TAU = 16.0
CHUNK = 64
D_FF = 2816
CONV_WIDTH = 3
LN_EPS = 1e-5
RMS_EPS = 1e-6
DN_ALPHA = 2.0 ** 0.25
IN_WIDTH = 3088
LANE = 128
IN_PAD = 3200
OFF_SBQ, OFF_SBK, OFF_SBV = 0, 512, 1024
OFF_GQ, OFF_GK, OFF_GV, OFF_GG, OFF_GA = 1536, 1792, 2048, 2560, 3072

ADAM_LR = 0.001
ADAM_B1 = 0.9
ADAM_B2 = 0.999
ADAM_EPS = 1e-08
ADAM_WD = 0.01
ADAM_STEP = 10

VMEM_LIMIT = 48 * 1024 * 1024
MESH_ID = pl.DeviceIdType.MESH


def _cparams(sem=None, **kw):
    return pltpu.CompilerParams(dimension_semantics=sem, vmem_limit_bytes=VMEM_LIMIT, **kw)


def _dot(a, b, dims):
    ca, cb = {"nn": (1, 0), "nt": (1, 1), "tn": (0, 0)}[dims]
    return lax.dot_general(a.astype(MXU_DTYPE), b.astype(MXU_DTYPE), (((ca,), (cb,)), ((), ())),
                           preferred_element_type=F32)


def _dot_split(a, b, dims):
    hi = a.astype(BF16)
    lo = (a - hi.astype(F32)).astype(BF16)
    ca, cb = {"nn": (1, 0), "nt": (1, 1), "tn": (0, 0)}[dims]
    dn = (((ca,), (cb,)), ((), ()))
    return (lax.dot_general(hi, b, dn, preferred_element_type=F32)
            + lax.dot_general(lo, b, dn, preferred_element_type=F32))


def _pick(dim, prefs):
    for p in prefs:
        if dim % p == 0:
            return p
    return dim


def _matmul(a, b, dims, out_dtype, name, res=None, res_scale=1.0):
    if dims == "nn":
        (M, K), (_, N) = a.shape, b.shape
    elif dims == "nt":
        (M, K), (N, _) = a.shape, b.shape
    else:
        (K, M), (_, N) = a.shape, b.shape
    tm = _pick(M, (1024, 1408, 512, 256, 128))
    tn = _pick(N, (640, 512, 256, 128))
    tk = _pick(K, (1024, 1408, 640, 512, 256, 128))
    nk = K // tk
    grid = (M // tm, N // tn, nk)
    if dims == "tn":
        a_spec = pl.BlockSpec((tk, tm), lambda i, j, k: (k, i))
    else:
        a_spec = pl.BlockSpec((tm, tk), lambda i, j, k: (i, k))
    if dims == "nt":
        b_spec = pl.BlockSpec((tn, tk), lambda i, j, k: (j, k))
    else:
        b_spec = pl.BlockSpec((tk, tn), lambda i, j, k: (k, j))
    o_spec = pl.BlockSpec((tm, tn), lambda i, j, k: (i, j))
    in_specs = [a_spec, b_spec]
    args = [a, b]
    if res is not None:
        in_specs.append(o_spec)
        args.append(res)

    def body(*refs):
        if res is not None:
            a_ref, b_ref, r_ref, o_ref, acc_ref = refs
        else:
            a_ref, b_ref, o_ref, acc_ref = refs
            r_ref = None
        k = pl.program_id(2)
        part = _dot(a_ref[...], b_ref[...], dims)

        def finish(total):
            if r_ref is not None:
                total = total + res_scale * r_ref[...]
            o_ref[...] = total.astype(o_ref.dtype)

        if nk == 1:
            finish(part)
        else:
            @pl.when(k == 0)
            def _():
                acc_ref[...] = part

            @pl.when(jnp.logical_and(k > 0, k < nk - 1))
            def _():
                acc_ref[...] += part

            @pl.when(k == nk - 1)
            def _():
                finish(acc_ref[...] + part)

    return pl.pallas_call(
        body, name=name, grid=grid, in_specs=in_specs, out_specs=o_spec,
        out_shape=jax.ShapeDtypeStruct((M, N), out_dtype),
        scratch_shapes=[pltpu.VMEM((tm, tn), F32)],
        compiler_params=_cparams(("parallel", "parallel", "arbitrary")),
    )(*args)


LN_ROWS = 256


def _ln_stats(r):
    mu = jnp.mean(r, axis=-1, keepdims=True)
    xc = r - mu
    var = jnp.mean(xc * xc, axis=-1, keepdims=True)
    return xc * lax.rsqrt(var + LN_EPS)


def _ln_fwd(r, g, b, name):
    S, D = r.shape

    def body(r_ref, g_ref, b_ref, h_ref):
        h_ref[...] = _ln_stats(r_ref[...]) * g_ref[...] + b_ref[...]

    row = pl.BlockSpec((LN_ROWS, D), lambda i: (i, 0))
    vec = pl.BlockSpec((1, D), lambda i: (0, 0))
    return pl.pallas_call(
        body, name=name, grid=(S // LN_ROWS,), in_specs=[row, vec, vec], out_specs=row,
        out_shape=jax.ShapeDtypeStruct((S, D), F32),
        compiler_params=_cparams(("parallel",)),
    )(r, g, b)


def _ln_bwd_core(xhat, dy, g):
    dxh = dy * g
    m1 = jnp.mean(dxh, axis=-1, keepdims=True)
    m2 = jnp.mean(dxh * xhat, axis=-1, keepdims=True)
    return dxh - m1 - xhat * m2


def _ln_bwd(r, dy, g, name):
    S, D = r.shape

    def body(r_ref, dy_ref, g_ref, dr_ref, gg_ref, gb_ref):
        x = r_ref[...]
        mu = jnp.mean(x, axis=-1, keepdims=True)
        xc = x - mu
        rstd = lax.rsqrt(jnp.mean(xc * xc, axis=-1, keepdims=True) + LN_EPS)
        xhat = xc * rstd
        dy = dy_ref[...]
        dr_ref[...] = rstd * _ln_bwd_core(xhat, dy, g_ref[...])

        @pl.when(pl.program_id(0) == 0)
        def _():
            gg_ref[...] = jnp.zeros_like(gg_ref)
            gb_ref[...] = jnp.zeros_like(gb_ref)

        gg_ref[...] += jnp.sum(dy * xhat, axis=0, keepdims=True)
        gb_ref[...] += jnp.sum(dy, axis=0, keepdims=True)

    row = pl.BlockSpec((LN_ROWS, D), lambda i: (i, 0))
    vec = pl.BlockSpec((1, D), lambda i: (0, 0))
    return pl.pallas_call(
        body, name=name, grid=(S // LN_ROWS,), in_specs=[row, row, vec], out_specs=[row, vec, vec],
        out_shape=[jax.ShapeDtypeStruct((S, D), F32), jax.ShapeDtypeStruct((1, D), F32),
                   jax.ShapeDtypeStruct((1, D), F32)],
        compiler_params=_cparams(("arbitrary",)),
    )(r, dy, g)


def _ln_loss_bwd(r, target, g, b, name):
    S, D = r.shape

    def body(r_ref, t_ref, g_ref, b_ref, dr_ref, loss_ref, gg_ref, gb_ref):
        x = r_ref[...]
        mu = jnp.mean(x, axis=-1, keepdims=True)
        xc = x - mu
        rstd = lax.rsqrt(jnp.mean(xc * xc, axis=-1, keepdims=True) + LN_EPS)
        xhat = xc * rstd
        y = xhat * g_ref[...] + b_ref[...]
        err = y - t_ref[...]
        dy = err * (1.0 / D)
        dr_ref[...] = rstd * _ln_bwd_core(xhat, dy, g_ref[...])

        @pl.when(pl.program_id(0) == 0)
        def _():
            loss_ref[...] = jnp.zeros_like(loss_ref)
            gg_ref[...] = jnp.zeros_like(gg_ref)
            gb_ref[...] = jnp.zeros_like(gb_ref)

        per_row = jnp.sum(err * err, axis=-1, keepdims=True) * (0.5 / D)
        loss_ref[...] += jnp.broadcast_to(jnp.sum(per_row, axis=0, keepdims=True), loss_ref.shape)
        gg_ref[...] += jnp.sum(dy * xhat, axis=0, keepdims=True)
        gb_ref[...] += jnp.sum(dy, axis=0, keepdims=True)

    row = pl.BlockSpec((LN_ROWS, D), lambda i: (i, 0))
    vec = pl.BlockSpec((1, D), lambda i: (0, 0))
    lvec = pl.BlockSpec((1, LANE), lambda i: (0, 0))
    return pl.pallas_call(
        body, name=name, grid=(S // LN_ROWS,), in_specs=[row, row, vec, vec],
        out_specs=[row, lvec, vec, vec],
        out_shape=[jax.ShapeDtypeStruct((S, D), F32), jax.ShapeDtypeStruct((1, LANE), F32),
                   jax.ShapeDtypeStruct((1, D), F32), jax.ShapeDtypeStruct((1, D), F32)],
        compiler_params=_cparams(("arbitrary",)),
    )(r, target, g, b)


CONV_COLS = 256
CONV_ROWS = 256
HALO = 8
INV_SQRT2 = 1.0 / math.sqrt(2.0)
INV_SQRT2PI = 1.0 / math.sqrt(2.0 * math.pi)


def _gelu(x):
    return 0.5 * x * (1.0 + lax.erf(x * INV_SQRT2))


def _gelu_grad(x):
    return 0.5 * (1.0 + lax.erf(x * INV_SQRT2)) + x * jnp.exp(-0.5 * x * x) * INV_SQRT2PI


def _conv_rows(ext, w_ref, b_ref, n):
    total = ext.shape[0]
    s1 = pltpu.roll(ext, 1, 0)
    s2 = pltpu.roll(ext, 2, 0)
    u = w_ref[2:3, :] * ext + w_ref[1:2, :] * s1 + w_ref[0:1, :] * s2 + b_ref[...]
    return u[HALO:total], s1[HALO:total], s2[HALO:total]


def _conv_gelu_fwd(u0, conv_w, conv_b, name):
    S, C2 = u0.shape
    F = C2 // 2
    ncb = F // CONV_COLS
    nrc = S // CONV_ROWS

    def body(ua_ref, uc_ref, wa_ref, wc_ref, ba_ref, bc_ref, p_ref):
        def chunk(ci, _):
            r0 = pl.multiple_of(ci * CONV_ROWS, CONV_ROWS)
            p0 = pl.multiple_of(jnp.maximum(r0 - HALO, 0), HALO)
            keep = (ci > 0).astype(F32)

            def load(ref):
                prev = ref[pl.ds(p0, HALO), :] * keep
                return jnp.concatenate([prev, ref[pl.ds(r0, CONV_ROWS), :]], axis=0)

            a, _, _ = _conv_rows(load(ua_ref), wa_ref, ba_ref, CONV_ROWS)
            c, _, _ = _conv_rows(load(uc_ref), wc_ref, bc_ref, CONV_ROWS)
            p_ref[pl.ds(r0, CONV_ROWS), :] = (_gelu(a) * c).astype(p_ref.dtype)
            return 0

        lax.fori_loop(0, nrc, chunk, 0)

    col_a = pl.BlockSpec((S, CONV_COLS), lambda j: (0, j))
    col_c = pl.BlockSpec((S, CONV_COLS), lambda j: (0, j + ncb))
    w_a = pl.BlockSpec((CONV_WIDTH, CONV_COLS), lambda j: (0, j))
    w_c = pl.BlockSpec((CONV_WIDTH, CONV_COLS), lambda j: (0, j + ncb))
    b_a = pl.BlockSpec((1, CONV_COLS), lambda j: (0, j))
    b_c = pl.BlockSpec((1, CONV_COLS), lambda j: (0, j + ncb))
    return pl.pallas_call(
        body, name=name, grid=(ncb,), in_specs=[col_a, col_c, w_a, w_c, b_a, b_c], out_specs=col_a,
        out_shape=jax.ShapeDtypeStruct((S, F), MXU_DTYPE),
        compiler_params=_cparams(("parallel",)),
    )(u0, u0, conv_w, conv_w, conv_b, conv_b)


def _conv_gelu_bwd(u0, dp, conv_w, conv_b, name):
    S, C2 = u0.shape
    F = C2 // 2
    ncb = F // CONV_COLS
    nrc = S // CONV_ROWS
    EXT = CONV_ROWS + HALO

    def body(ua_ref, uc_ref, dp_ref, wa_ref, wc_ref, ba_ref, bc_ref,
             da_ref, dc_ref, gwa_ref, gwc_ref, gba_ref, gbc_ref):
        gwa_ref[...] = jnp.zeros_like(gwa_ref)
        gwc_ref[...] = jnp.zeros_like(gwc_ref)
        gba_ref[...] = jnp.zeros_like(gba_ref)
        gbc_ref[...] = jnp.zeros_like(gbc_ref)
        rid = lax.broadcasted_iota(jnp.int32, (EXT, CONV_COLS), 0)

        def chunk(ci, _):
            r0 = pl.multiple_of(ci * CONV_ROWS, CONV_ROWS)
            p0 = pl.multiple_of(jnp.maximum(r0 - HALO, 0), HALO)
            n0 = pl.multiple_of(jnp.minimum(r0 + CONV_ROWS, S - HALO), HALO)
            keep_prev = (ci > 0).astype(F32)
            keep_next = (ci < nrc - 1).astype(F32)

            def load(ref):
                return jnp.concatenate([ref[pl.ds(p0, HALO), :] * keep_prev,
                                        ref[pl.ds(r0, CONV_ROWS), :],
                                        ref[pl.ds(n0, HALO), :] * keep_next], axis=0)

            ext_a = load(ua_ref)
            ext_c = load(uc_ref)
            a, a1, a2 = _conv_rows(ext_a, wa_ref, ba_ref, EXT)
            c, c1, c2 = _conv_rows(ext_c, wc_ref, bc_ref, EXT)
            a0 = ext_a[HALO:HALO + EXT]
            c0 = ext_c[HALO:HALO + EXT]
            dpe = jnp.concatenate([dp_ref[pl.ds(r0, CONV_ROWS), :].astype(F32),
                                   dp_ref[pl.ds(n0, HALO), :].astype(F32) * keep_next], axis=0)
            d_a = dpe * c * _gelu_grad(a)
            d_c = dpe * _gelu(a)
            own = rid < CONV_ROWS

            def back(d_u, w_ref, x0, x1, x2, d_ref, gw_ref, gb_ref):
                d_u0 = (w_ref[2:3, :] * d_u + w_ref[1:2, :] * pltpu.roll(d_u, EXT - 1, 0)
                        + w_ref[0:1, :] * pltpu.roll(d_u, EXT - 2, 0))
                d_ref[pl.ds(r0, CONV_ROWS), :] = d_u0[0:CONV_ROWS].astype(d_ref.dtype)
                d_own = jnp.where(own, d_u, 0.0)
                gw_ref[...] += jnp.concatenate(
                    [jnp.sum(d_own * x2, axis=0, keepdims=True),
                     jnp.sum(d_own * x1, axis=0, keepdims=True),
                     jnp.sum(d_own * x0, axis=0, keepdims=True)], axis=0)
                gb_ref[...] += jnp.sum(d_own, axis=0, keepdims=True)

            back(d_a, wa_ref, a0, a1, a2, da_ref, gwa_ref, gba_ref)
            back(d_c, wc_ref, c0, c1, c2, dc_ref, gwc_ref, gbc_ref)
            return 0

        lax.fori_loop(0, nrc, chunk, 0)

    col_a = pl.BlockSpec((S, CONV_COLS), lambda j: (0, j))
    col_c = pl.BlockSpec((S, CONV_COLS), lambda j: (0, j + ncb))
    w_a = pl.BlockSpec((CONV_WIDTH, CONV_COLS), lambda j: (0, j))
    w_c = pl.BlockSpec((CONV_WIDTH, CONV_COLS), lambda j: (0, j + ncb))
    b_a = pl.BlockSpec((1, CONV_COLS), lambda j: (0, j))
    b_c = pl.BlockSpec((1, CONV_COLS), lambda j: (0, j + ncb))
    outs = pl.pallas_call(
        body, name=name, grid=(ncb,),
        in_specs=[col_a, col_c, col_a, w_a, w_c, b_a, b_c],
        out_specs=[col_a, col_a, w_a, w_a, b_a, b_a],
        out_shape=[jax.ShapeDtypeStruct((S, F), MXU_DTYPE), jax.ShapeDtypeStruct((S, F), MXU_DTYPE),
                   jax.ShapeDtypeStruct((CONV_WIDTH, F), F32), jax.ShapeDtypeStruct((CONV_WIDTH, F), F32),
                   jax.ShapeDtypeStruct((1, F), F32), jax.ShapeDtypeStruct((1, F), F32)],
        compiler_params=_cparams(("parallel",)),
    )(u0, u0, dp, conv_w, conv_w, conv_b, conv_b)
    da, dc, gwa, gwc, gba, gbc = outs
    return (jnp.concatenate([da, dc], axis=1), jnp.concatenate([gwa, gwc], axis=1),
            jnp.concatenate([gba, gbc], axis=1))


SB_TK = 128
SB_TQ_FWD = 512
SB_TQ_BWD = 256


def _softplus(z):
    return jnp.maximum(z, 0.0) + jnp.log(1.0 + jnp.exp(-jnp.abs(z)))


def _tri_ones(after):
    r = lax.broadcasted_iota(jnp.int32, (SB_TK, 2 * SB_TK), 0)
    c = lax.broadcasted_iota(jnp.int32, (SB_TK, 2 * SB_TK), 1)
    tri = (r > c) if after else (r < c)
    return jnp.where(c >= SB_TK, 1.0, jnp.where(tri, 1.0, 0.0)).astype(BF16)


def _sb_fwd(q, k, v, name):
    H, S, Dh = q.shape
    TK = SB_TK
    TQ = min(SB_TQ_FWD, S)
    R = TQ // TK

    def body(q_ref, k_ref, v_ref, o_ref, t_ref):
        i = pl.program_id(1)
        qb = q_ref[0]
        row = lax.broadcasted_iota(jnp.int32, (TQ, TK), 0)
        col = lax.broadcasted_iota(jnp.int32, (TQ, TK), 1)
        after_ones = _tri_ones(True)

        def block(j, acc, tail, masked):
            k0 = pl.multiple_of(j * TK, TK)
            kb = k_ref[0, pl.ds(k0, TK), :]
            vb = v_ref[0, pl.ds(k0, TK), :]
            z = _dot(qb, kb, "nt")
            sp = _softplus(z)
            if masked:
                strict = (col + j * TK) < (row + i * TQ)
                sp = jnp.where(strict, sp, 0.0)
            cs = _dot_split(sp, after_ones, "nn")
            w = jnp.exp(z - sp - cs[:, :TK] - tail)
            if masked:
                w = jnp.where(strict, w, 0.0)
            return acc + _dot(w, vb, "nn"), tail + cs[:, TK:]

        acc, tail = jnp.zeros((TQ, Dh), F32), jnp.zeros((TQ, TK), F32)
        for u in reversed(range(R)):
            acc, tail = block(R * i + u, acc, tail, True)
        acc, tail = lax.fori_loop(0, R * i, lambda n, c: block(R * i - 1 - n, c[0], c[1], False), (acc, tail))
        o_ref[0] = acc
        t_ref[0] = tail

    qspec = pl.BlockSpec((1, TQ, Dh), lambda h, i: (h, i, 0))
    full = pl.BlockSpec((1, S, Dh), lambda h, i: (h, 0, 0))
    tspec = pl.BlockSpec((1, TQ, TK), lambda h, i: (h, i, 0))
    return pl.pallas_call(
        body, name=name, grid=(H, S // TQ), in_specs=[qspec, full, full], out_specs=[qspec, tspec],
        out_shape=[jax.ShapeDtypeStruct((H, S, Dh), F32), jax.ShapeDtypeStruct((H, S, TK), F32)],
        compiler_params=_cparams(("parallel", "arbitrary")),
    )(q, k, v)


def _sb_bwd(q, k, v, do, total, name):
    H, S, Dh = q.shape
    TK = SB_TK
    TQ = min(SB_TQ_BWD, S)
    R = TQ // TK
    scale = Dh ** -0.5

    def body(q_ref, k_ref, v_ref, do_ref, t_ref, dq_ref, dk_ref, dv_ref):
        i = pl.program_id(1)

        @pl.when(i == 0)
        def _():
            dk_ref[...] = jnp.zeros_like(dk_ref)
            dv_ref[...] = jnp.zeros_like(dv_ref)

        qb = q_ref[0]
        dob = do_ref[0].astype(MXU_DTYPE)
        tot = t_ref[0]
        row = lax.broadcasted_iota(jnp.int32, (TQ, TK), 0)
        col = lax.broadcasted_iota(jnp.int32, (TQ, TK), 1)
        after_ones = _tri_ones(True)
        before_ones = _tri_ones(False)

        def block(j, dq, seen, gsum, masked):
            k0 = pl.multiple_of(j * TK, TK)
            kb = k_ref[0, pl.ds(k0, TK), :]
            vb = v_ref[0, pl.ds(k0, TK), :]
            z = _dot(qb, kb, "nt")
            sp = _softplus(z)
            logsig = z - sp
            if masked:
                strict = (col + j * TK) < (row + i * TQ)
                sp = jnp.where(strict, sp, 0.0)
            cs = _dot_split(sp, after_ones, "nn")
            seen = seen + cs[:, TK:]
            w = jnp.exp(logsig - cs[:, :TK] - (tot - seen))
            if masked:
                w = jnp.where(strict, w, 0.0)
            g = w * _dot(dob, vb, "nt")
            cg = _dot_split(g, before_ones, "nn")
            dz = g - jnp.exp(logsig) * (g + cg[:, :TK] + gsum)
            if masked:
                dz = jnp.where(strict, dz, 0.0)
            dzb = dz.astype(MXU_DTYPE)
            dk_ref[0, pl.ds(k0, TK), :] += _dot(dzb, qb, "tn")
            dv_ref[0, pl.ds(k0, TK), :] += _dot(w, dob, "tn")
            return dq + _dot(dzb, kb, "nn"), seen, gsum + cg[:, TK:]

        zero = jnp.zeros((TQ, TK), F32)
        carry = lax.fori_loop(0, R * i, lambda j, c: block(j, c[0], c[1], c[2], False),
                              (jnp.zeros((TQ, Dh), F32), zero, zero))
        for u in range(R):
            carry = block(R * i + u, carry[0], carry[1], carry[2], True)
        dq_ref[0] = carry[0] * scale

    qspec = pl.BlockSpec((1, TQ, Dh), lambda h, i: (h, i, 0))
    full = pl.BlockSpec((1, S, Dh), lambda h, i: (h, 0, 0))
    tspec = pl.BlockSpec((1, TQ, TK), lambda h, i: (h, i, 0))
    sds = jax.ShapeDtypeStruct((H, S, Dh), F32)
    return pl.pallas_call(
        body, name=name, grid=(H, S // TQ), in_specs=[qspec, full, full, qspec, tspec],
        out_specs=[qspec, full, full], out_shape=[sds, sds, sds],
        compiler_params=_cparams(("parallel", "arbitrary")),
    )(q, k, v, do, total)


GLA_GROUP_FWD = 8
GLA_GROUP_BWD = 4


def _gla_masks():
    C = CHUNK
    row = lax.broadcasted_iota(jnp.int32, (C, C), 0)
    col = lax.broadcasted_iota(jnp.int32, (C, C), 1)
    return row, col


def _log_sigmoid(x):
    return -_softplus(-x)


def _gla_chunk_fwd(qc, kc, vc, gate, row, col):
    C = CHUNK
    la = _log_sigmoid(gate) * (1.0 / GLA_TAU)
    incl = jnp.where(row >= col, 1.0, 0.0).astype(BF16)
    b = _dot_split_lhs01(incl, la)
    b_ref = jnp.sum(jnp.where(row == C // 2 - 1, b, 0.0), axis=0, keepdims=True)
    b_last = jnp.sum(la, axis=0, keepdims=True)
    qs = qc * (GLA_KEY_DIM ** -0.5)
    q_in = qs * jnp.exp(b - b_ref)
    k_in = kc * jnp.exp(b_ref - b)
    k_dec = kc * jnp.exp(b_last - b)
    q_b = qs * jnp.exp(b)
    sc = jnp.where(row >= col, _dot(q_in, k_in, "nt"), 0.0)
    o_intra = _dot(sc, vc, "nn")
    upd = _dot(k_dec, vc, "tn")
    ones = jnp.ones((C, GLA_VAL_DIM), BF16)
    dec_col = jnp.exp(_dot_split_tn(la, ones))
    return dict(la=la, b=b, b_ref=b_ref, b_last=b_last, qs=qs, q_in=q_in, k_in=k_in, k_dec=k_dec,
                q_b=q_b, sc=sc, o_intra=o_intra, upd=upd, dec_col=dec_col)


def _dot_split_lhs01(m01, x):
    hi = x.astype(BF16)
    lo = (x - hi.astype(F32)).astype(BF16)
    dn = (((1,), (0,)), ((), ()))
    return (lax.dot_general(m01, hi, dn, preferred_element_type=F32)
            + lax.dot_general(m01, lo, dn, preferred_element_type=F32))


def _dot_split_tn(x, m01):
    hi = x.astype(BF16)
    lo = (x - hi.astype(F32)).astype(BF16)
    dn = (((0,), (0,)), ((), ()))
    return (lax.dot_general(hi, m01, dn, preferred_element_type=F32)
            + lax.dot_general(lo, m01, dn, preferred_element_type=F32))


def _dot_split_nt01(m01, x):
    hi = x.astype(BF16)
    lo = (x - hi.astype(F32)).astype(BF16)
    dn = (((1,), (1,)), ((), ()))
    return (lax.dot_general(m01, hi, dn, preferred_element_type=F32)
            + lax.dot_general(m01, lo, dn, preferred_element_type=F32))


def _rms_gate(o, gg, gnorm):
    rinv = lax.rsqrt(jnp.mean(o * o, axis=-1, keepdims=True) + RMS_EPS)
    o_n = o * rinv
    sg = 1.0 / (1.0 + jnp.exp(-gg))
    return o_n, rinv, sg


def _gla_fwd(gq, gk, gv, gg, ga_pad, gate_up_h, gate_bias_h, gnorm, name):
    Hg, S, dk = gq.shape
    dv = GLA_VAL_DIM
    C = CHUNK
    G = GLA_GROUP_FWD
    nchunk = S // C
    ngroup = nchunk // G

    def body(q_ref, k_ref, v_ref, gg_ref, ga_ref, gu_ref, gb_ref, gn_ref, o_ref, prev_ref):
        row, col = _gla_masks()
        gu = gu_ref[0]
        gbias = gb_ref[0]
        gnorm_v = gn_ref[...]

        def group(gi, state):
            for u in range(G):
                ci = gi * G + u
                r0 = pl.multiple_of(ci * C, C)
                rows = pl.ds(r0, C)
                gate = _dot(ga_ref[rows, :], gu, "nn") + gbias
                f = _gla_chunk_fwd(q_ref[0, rows, :], k_ref[0, rows, :], v_ref[rows, :], gate, row, col)
                prev_ref[0, ci] = state
                o = f["o_intra"] + _dot(f["q_b"], state, "nn")
                state = f["dec_col"] * state + f["upd"]
                o_n, _, sg = _rms_gate(o, gg_ref[rows, :], gnorm_v)
                o_ref[rows, :] = o_n * gnorm_v * (gg_ref[rows, :] * sg)
            return state

        lax.fori_loop(0, ngroup, group, jnp.zeros((dk, dv), F32))

    hspec = pl.BlockSpec((1, S, dk), lambda h: (h, 0, 0))
    vspec = pl.BlockSpec((S, dv), lambda h: (0, h))
    return pl.pallas_call(
        body, name=name, grid=(Hg,),
        in_specs=[hspec, hspec, vspec, vspec,
                  pl.BlockSpec((S, LANE), lambda h: (0, 0)),
                  pl.BlockSpec((1, LANE, dk), lambda h: (h, 0, 0)),
                  pl.BlockSpec((1, 1, dk), lambda h: (h, 0, 0)),
                  pl.BlockSpec((1, dv), lambda h: (0, 0))],
        out_specs=[vspec, pl.BlockSpec((1, nchunk, dk, dv), lambda h: (h, 0, 0, 0))],
        out_shape=[jax.ShapeDtypeStruct((S, Hg * dv), F32),
                   jax.ShapeDtypeStruct((Hg, nchunk, dk, dv), F32)],
        compiler_params=_cparams(("parallel",)),
    )(gq, gk, gv, gg, ga_pad, gate_up_h, gate_bias_h, gnorm)


def _gla_bwd(gq, gk, gv, gg, ga_pad, gate_up_h, gate_bias_h, gnorm, prev, d_out, name):
    Hg, S, dk = gq.shape
    dv = GLA_VAL_DIM
    C = CHUNK
    G = GLA_GROUP_BWD
    nchunk = S // C
    ngroup = nchunk // G

    def body(q_ref, k_ref, v_ref, gg_ref, ga_ref, gu_ref, gb_ref, gn_ref, prev_ref, do_ref,
             dq_ref, dk_ref, dv_ref, dgg_ref, dga_ref, ggu_ref, ggb_ref, ggn_ref):
        h = pl.program_id(0)
        row, col = _gla_masks()
        gu = gu_ref[0]
        gbias = gb_ref[0]
        gnorm_v = gn_ref[...]
        ggu_ref[...] = jnp.zeros_like(ggu_ref)
        ggb_ref[...] = jnp.zeros_like(ggb_ref)

        @pl.when(h == 0)
        def _():
            dga_ref[...] = jnp.zeros_like(dga_ref)
            ggn_ref[...] = jnp.zeros_like(ggn_ref)

        upper_incl = jnp.where(col >= row, 1.0, 0.0).astype(BF16)
        ones_8 = jnp.ones((8, dv), BF16)

        def group(gn, dstate):
            gi = ngroup - 1 - gn
            for u in reversed(range(G)):
                ci = gi * G + u
                r0 = pl.multiple_of(ci * C, C)
                rows = pl.ds(r0, C)
                ga = ga_ref[rows, :]
                gate = _dot(ga, gu, "nn") + gbias
                qc, kc, vc = q_ref[0, rows, :], k_ref[0, rows, :], v_ref[rows, :]
                f = _gla_chunk_fwd(qc, kc, vc, gate, row, col)
                state = prev_ref[0, ci]
                o = f["o_intra"] + _dot(f["q_b"], state, "nn")
                ggv = gg_ref[rows, :]
                o_n, rinv, sg = _rms_gate(o, ggv, gnorm_v)
                dout = do_ref[rows, :]
                silu = ggv * sg
                dgg_ref[rows, :] = dout * o_n * gnorm_v * (sg * (1.0 + ggv * (1.0 - sg)))
                d_ong = dout * silu
                ggn_ref[...] += jnp.sum(d_ong * o_n, axis=0, keepdims=True)
                d_on = d_ong * gnorm_v
                d_o = rinv * (d_on - o_n * jnp.mean(d_on * o_n, axis=-1, keepdims=True))
                d_upd = dstate
                d_dec_col = dstate * state * f["dec_col"]
                dstate = f["dec_col"] * dstate + _dot(f["q_b"], d_o, "tn")
                dsc = jnp.where(row >= col, _dot(d_o, vc, "nt"), 0.0)
                dv_ref[rows, :] = _dot(f["sc"], d_o, "tn") + _dot(f["k_dec"], d_upd, "nn")
                dq_in = _dot(dsc, f["k_in"], "nn")
                dk_in = _dot(dsc, f["q_in"], "tn")
                dq_b = _dot(d_o, state, "nt")
                dkdec = _dot(vc, d_upd, "nt")
                b = f["b"]
                e1 = jnp.exp(b - f["b_ref"])
                e2 = jnp.exp(f["b_ref"] - b)
                e3 = jnp.exp(f["b_last"] - b)
                eb = jnp.exp(b)
                dq_ref[0, rows, :] = (dq_in * e1 + dq_b * eb) * (GLA_KEY_DIM ** -0.5)
                dk_ref[0, rows, :] = dk_in * e2 + dkdec * e3
                t_q = dq_in * f["q_in"]
                t_k = dk_in * f["k_in"]
                t_d = dkdec * f["k_dec"]
                db = t_q - t_k - t_d + dq_b * f["q_b"]
                db_ref = jnp.sum(t_k - t_q, axis=0, keepdims=True)
                db_last = (jnp.sum(t_d, axis=0, keepdims=True)
                           + jnp.max(_dot_split_nt01(ones_8, d_dec_col), axis=0, keepdims=True))
                db = db + jnp.where(row == C // 2 - 1, db_ref, 0.0) + jnp.where(row == C - 1, db_last, 0.0)
                dla = _dot_split_lhs01(upper_incl, db)
                d_gate = dla * (1.0 / GLA_TAU) * (1.0 / (1.0 + jnp.exp(gate)))
                ggb_ref[0] += jnp.sum(d_gate, axis=0, keepdims=True)
                ggu_ref[0] += _dot(ga, d_gate, "tn")
                dga_ref[rows, :] += _dot(d_gate, gu, "nt")
            return dstate

        lax.fori_loop(0, ngroup, group, jnp.zeros((dk, dv), F32))

    hspec = pl.BlockSpec((1, S, dk), lambda h: (h, 0, 0))
    vspec = pl.BlockSpec((S, dv), lambda h: (0, h))
    gaspec = pl.BlockSpec((S, LANE), lambda h: (0, 0))
    guspec = pl.BlockSpec((1, LANE, dk), lambda h: (h, 0, 0))
    gbspec = pl.BlockSpec((1, 1, dk), lambda h: (h, 0, 0))
    gnspec = pl.BlockSpec((1, dv), lambda h: (0, 0))
    return pl.pallas_call(
        body, name=name, grid=(Hg,),
        in_specs=[hspec, hspec, vspec, vspec, gaspec, guspec, gbspec,
                  pl.BlockSpec((1, dv), lambda h: (0, 0)),
                  pl.BlockSpec((1, nchunk, dk, dv), lambda h: (h, 0, 0, 0)), vspec],
        out_specs=[hspec, hspec, vspec, vspec, gaspec, guspec, gbspec, gnspec],
        out_shape=[jax.ShapeDtypeStruct((Hg, S, dk), F32), jax.ShapeDtypeStruct((Hg, S, dk), F32),
                   jax.ShapeDtypeStruct((S, Hg * dv), F32), jax.ShapeDtypeStruct((S, Hg * dv), F32),
                   jax.ShapeDtypeStruct((S, LANE), F32), jax.ShapeDtypeStruct((Hg, LANE, dk), F32),
                   jax.ShapeDtypeStruct((Hg, 1, dk), F32), jax.ShapeDtypeStruct((1, dv), F32)],
        compiler_params=_cparams(("arbitrary",)),
    )(gq, gk, gv, gg, ga_pad, gate_up_h, gate_bias_h, gnorm, prev, d_out)


def _exchange(items, name):
    n = len(items)
    n_peer = N_DEV - 1

    def body(*refs):
        ins, outs = refs[:n], refs[n:2 * n]
        send_sems, recv_sems, local_sems = refs[2 * n:]
        x, y, c = lax.axis_index("x"), lax.axis_index("y"), lax.axis_index("c")
        me = 4 * x + 2 * y + c
        local, remote = [], []
        for a, (_, scatter) in enumerate(items):
            own = ins[a].at[me] if scatter else ins[a]
            cp = pltpu.make_async_copy(own, outs[a].at[me], local_sems.at[a])
            cp.start()
            local.append(cp)
        for r in range(1, N_DEV):
            px = 1 - x if r & 4 else x
            py = 1 - y if r & 2 else y
            pc = 1 - c if r & 1 else c
            for a, (_, scatter) in enumerate(items):
                src = ins[a].at[4 * px + 2 * py + pc] if scatter else ins[a]
                cp = pltpu.make_async_remote_copy(
                    src_ref=src, dst_ref=outs[a].at[me],
                    send_sem=send_sems.at[a * n_peer + r - 1], recv_sem=recv_sems.at[a * n_peer + r - 1],
                    device_id=(px, py, pc), device_id_type=MESH_ID)
                cp.start()
                remote.append(cp)
        for cp in remote:
            cp.wait()
        for cp in local:
            cp.wait()

    out_shape = []
    for arr, scatter in items:
        shp = arr.shape if scatter else (N_DEV,) + arr.shape
        out_shape.append(jax.ShapeDtypeStruct(shp, arr.dtype))
    any_spec = pl.BlockSpec(memory_space=pl.ANY)
    return pl.pallas_call(
        body, name=name, in_specs=[any_spec] * n, out_specs=[any_spec] * n, out_shape=out_shape,
        scratch_shapes=[pltpu.SemaphoreType.DMA((n * n_peer,)), pltpu.SemaphoreType.DMA((n * n_peer,)),
                        pltpu.SemaphoreType.DMA((n,))],
        compiler_params=pltpu.CompilerParams(has_side_effects=True),
    )(*[arr for arr, _ in items])


def _adamw(grecv, w, m, v, name):
    R, C = w.shape
    tile = _pick(R, (256, 176, 128)) if R * C > 65536 else R
    bc1 = 1.0 - ADAM_B1 ** ADAM_STEP
    bc2 = 1.0 - ADAM_B2 ** ADAM_STEP

    def body(gr_ref, w_ref, m_ref, v_ref, g_ref, d_ref, nm_ref, nv_ref):
        g = gr_ref[0].astype(F32)
        for q in range(1, N_DEV):
            g = g + gr_ref[q].astype(F32)
        nm = ADAM_B1 * m_ref[...] + (1.0 - ADAM_B1) * g
        nv = ADAM_B2 * v_ref[...] + (1.0 - ADAM_B2) * (g * g)
        m_hat = nm / bc1
        v_hat = nv / bc2
        g_ref[...] = g
        d_ref[...] = -ADAM_LR * (m_hat / (jnp.sqrt(v_hat) + ADAM_EPS) + ADAM_WD * w_ref[...])
        nm_ref[...] = nm
        nv_ref[...] = nv

    blk = pl.BlockSpec((tile, C), lambda i: (i, 0))
    sds = jax.ShapeDtypeStruct((R, C), F32)
    return pl.pallas_call(
        body, name=name, grid=(R // tile,),
        in_specs=[pl.BlockSpec((N_DEV, tile, C), lambda i: (0, i, 0)), blk, blk, blk],
        out_specs=[blk, blk, blk, blk], out_shape=[sds, sds, sds, sds],
        compiler_params=_cparams(("parallel",)),
    )(grecv, w, m, v)


def _pack_rows(parts, rows, dtype):
    flat = jnp.concatenate([p.reshape(-1).astype(dtype) for p in parts])
    return jnp.pad(flat, (0, rows * LANE - flat.shape[0])).reshape(rows, LANE)


def _unpack_rows(buf, shapes):
    flat = buf.reshape(-1)
    out, off = [], 0
    for shp in shapes:
        n = math.prod(shp)
        out.append(flat[off:off + n].reshape(shp))
        off += n
    return out


def _shard_cols(g):
    rows, cols = g.shape
    return g.reshape(rows, N_DEV, cols // N_DEV).transpose(1, 0, 2)


def _unshard_cols(blocks):
    return blocks.transpose(1, 0, 2).reshape(blocks.shape[1], -1)


def _heads(t, n, d):
    return t.reshape(t.shape[0], n, d).transpose(1, 0, 2)


def _unheads(t):
    return t.transpose(1, 0, 2).reshape(t.shape[1], -1)


def kernel(x, w_in, gate_up, gate_bias, gla_norm_g, w_out, ln1_g, ln1_b, w_up, conv_w, conv_b, w_down, ln2_g, ln2_b, loss_target, m_w_in, m_gate_up, m_gate_bias, m_gla_norm_g, m_w_out, m_ln1_g, m_ln1_b, m_w_up, m_conv_w, m_conv_b, m_w_down, m_ln2_g, m_ln2_b, v_w_in, v_gate_up, v_gate_bias, v_gla_norm_g, v_w_out, v_ln1_g, v_ln1_b, v_w_up, v_conv_w, v_conv_b, v_w_down, v_ln2_g, v_ln2_b):
    S, D = x.shape[1], x.shape[2]
    x2, tgt = x[0], loss_target[0]

    gathered = _exchange([(w_in[0].astype(MXU_DTYPE), False), (w_out[0].astype(MXU_DTYPE), False),
                          (w_up[0].astype(MXU_DTYPE), False), (w_down[0].astype(MXU_DTYPE), False),
                          (gate_up[0], False), (conv_w[0], False)], "gather_weights")
    w_in_f = _unshard_cols(gathered[0])
    w_out_f = gathered[1].reshape(-1, D)
    w_up_f = _unshard_cols(gathered[2])
    w_down_f = gathered[3].reshape(-1, D)
    gate_up_f = _unshard_cols(gathered[4])
    conv_w_f = _unshard_cols(gathered[5])
    w_in_pad = jnp.pad(w_in_f, ((0, 0), (0, IN_PAD - IN_WIDTH)))
    gate_up_h = _heads(jnp.pad(gate_up_f, ((0, LANE - GLA_GATE_RANK), (0, 0))), GLA_HEADS, GLA_KEY_DIM)
    gate_bias_h = gate_bias.reshape(GLA_HEADS, 1, GLA_KEY_DIM)

    proj = _matmul(x2, w_in_pad, "nn", F32, "proj")
    sq = _heads(proj[:, OFF_SBQ:OFF_SBK] * (SB_HEAD_DIM ** -0.5), SB_HEADS, SB_HEAD_DIM).astype(MXU_DTYPE)
    sk = _heads(proj[:, OFF_SBK:OFF_SBV], SB_HEADS, SB_HEAD_DIM).astype(MXU_DTYPE)
    sv = _heads(proj[:, OFF_SBV:OFF_GQ], SB_HEADS, SB_HEAD_DIM).astype(MXU_DTYPE)
    sb_o_h, sb_tot = _sb_fwd(sq, sk, sv, "sb_fwd")
    gq = _heads(proj[:, OFF_GQ:OFF_GK], GLA_HEADS, GLA_KEY_DIM)
    gk = _heads(proj[:, OFF_GK:OFF_GV], GLA_HEADS, GLA_KEY_DIM)
    gv, gg, ga_pad = proj[:, OFF_GV:OFF_GG], proj[:, OFF_GG:OFF_GA], proj[:, OFF_GA:IN_PAD]
    gla_o, prev = _gla_fwd(gq, gk, gv, gg, ga_pad, gate_up_h, gate_bias_h, gla_norm_g, "gla_fwd")
    cat = jnp.concatenate([_unheads(sb_o_h), gla_o], axis=1)
    r1 = _matmul(cat, w_out_f, "nn", F32, "mix", res=x2, res_scale=DN_ALPHA)
    h = _ln_fwd(r1, ln1_g, ln1_b, "ln1")
    u0 = _matmul(h, w_up_f, "nn", F32, "ffn_up")
    p = _conv_gelu_fwd(u0, conv_w_f, conv_b, "conv_gelu")
    r2 = _matmul(p, w_down_f, "nn", F32, "ffn_down", res=h, res_scale=DN_ALPHA)
    d_r2, loss_p, g_ln2_g, g_ln2_b = _ln_loss_bwd(r2, tgt, ln2_g, ln2_b, "ln2_loss")

    d_p = _matmul(d_r2, w_down_f, "nt", MXU_DTYPE, "d_ffn_act")
    g_w_down = _matmul(p, d_r2, "tn", F32, "grad_w_down")
    d_u0, g_conv_w, g_conv_b = _conv_gelu_bwd(u0, d_p, conv_w_f, conv_b, "conv_gelu_bwd")
    g_w_up = _matmul(h, d_u0, "tn", F32, "grad_w_up")
    d_h = _matmul(d_u0, w_up_f, "nt", F32, "d_h", res=d_r2, res_scale=DN_ALPHA)
    d_r1, g_ln1_g, g_ln1_b = _ln_bwd(r1, d_h, ln1_g, "ln1_bwd")
    g_w_out = _matmul(cat, d_r1, "tn", F32, "grad_w_out")
    d_cat = _matmul(d_r1, w_out_f, "nt", F32, "d_cat")
    (d_gq, d_gk, d_gv, d_gg, d_ga_pad, g_gu_h, g_gb_h, g_gnorm) = _gla_bwd(
        gq, gk, gv, gg, ga_pad, gate_up_h, gate_bias_h, gla_norm_g, prev, d_cat[:, SB_WIDTH:], "gla_bwd")
    d_sq, d_sk, d_sv = _sb_bwd(sq, sk, sv, _heads(d_cat[:, :SB_WIDTH], SB_HEADS, SB_HEAD_DIM), sb_tot,
                               "sb_bwd")
    d_proj = jnp.concatenate([_unheads(d_sq), _unheads(d_sk), _unheads(d_sv), _unheads(d_gq), _unheads(d_gk),
                              d_gv, d_gg, d_ga_pad], axis=1)
    g_w_in = _matmul(x2, d_proj, "tn", F32, "grad_w_in")[:, :IN_WIDTH]
    d_x = _matmul(d_proj, w_in_pad, "nt", F32, "d_x", res=d_r1, res_scale=DN_ALPHA)
    g_gate_up = _unheads(g_gu_h[:, :GLA_GATE_RANK, :])
    g_gate_bias = g_gb_h.reshape(1, -1)

    small_g = [g_gate_bias, g_gnorm, g_ln1_g, g_ln1_b, g_conv_b, g_ln2_g, g_ln2_b, loss_p]
    n_small = sum(t.size for t in small_g)
    rows_small = -(-n_small // (8 * LANE)) * 8
    recv = _exchange([(_shard_cols(g_w_in).astype(BF16), True),
                      (g_w_out.reshape(N_DEV, -1, D).astype(BF16), True),
                      (_shard_cols(g_w_up).astype(BF16), True),
                      (g_w_down.reshape(N_DEV, -1, D).astype(BF16), True),
                      (_shard_cols(g_gate_up), True), (_shard_cols(g_conv_w), True),
                      (_pack_rows(small_g, rows_small, F32), False)], "exchange_grads")
    sharded = [(w_in, m_w_in, v_w_in), (w_out, m_w_out, v_w_out), (w_up, m_w_up, v_w_up),
               (w_down, m_w_down, v_w_down), (gate_up, m_gate_up, v_gate_up), (conv_w, m_conv_w, v_conv_w)]
    upd = [_adamw(recv[n], w[0], m[0], v[0], "adamw_%d" % n) for n, (w, m, v) in enumerate(sharded)]
    zl = jnp.zeros((1, LANE), F32)

    def pks(parts):
        return _pack_rows(parts + [zl], rows_small, F32)

    small = _adamw(recv[6], pks([gate_bias, gla_norm_g, ln1_g, ln1_b, conv_b, ln2_g, ln2_b]),
                   pks([m_gate_bias, m_gla_norm_g, m_ln1_g, m_ln1_b, m_conv_b, m_ln2_g, m_ln2_b]),
                   pks([v_gate_bias, v_gla_norm_g, v_ln1_g, v_ln1_b, v_conv_b, v_ln2_g, v_ln2_b]),
                   "adamw_replicated")
    small_shapes = [gate_bias.shape, gla_norm_g.shape, ln1_g.shape, ln1_b.shape, conv_b.shape, ln2_g.shape,
                    ln2_b.shape, (1, LANE)]
    outs = []
    loss = None
    for kind in range(4):
        b_w_in, b_w_out, b_w_up, b_w_down, b_gate_up, b_conv_w = [u[kind][None] for u in upd]
        s_gb, s_gn, s_l1g, s_l1b, s_cb, s_l2g, s_l2b, s_loss = _unpack_rows(small[kind], small_shapes)
        if kind == 0:
            loss = s_loss[0, 0]
        outs += [b_w_in, b_gate_up, s_gb, s_gn, b_w_out, s_l1g, s_l1b, b_w_up, b_conv_w, s_cb, b_w_down,
                 s_l2g, s_l2b]
    return (loss, d_x[None], *outs)
```

```python
import math

import jax
import jax.numpy as jnp
from jax import lax
from jax.experimental import pallas as pl
from jax.experimental.pallas import tpu as pltpu

F32 = jnp.float32
BF16 = jnp.bfloat16
MXU_DTYPE = jnp.bfloat16

N_DEV = 8
D_MODEL = 1024
SB_WIDTH = 512
SB_HEADS = 8
SB_HEAD_DIM = 64
GLA_HEADS = 4
GLA_KEY_DIM = 64
GLA_VAL_DIM = 128
GLA_WIDTH = 512
GLA_GATE_RANK = 16
GLA_TAU = 16.0
CHUNK = 64
D_FF = 2816
CONV_WIDTH = 3
LN_EPS = 1e-5
RMS_EPS = 1e-6
DN_ALPHA = 2.0 ** 0.25
IN_WIDTH = 3088
LANE = 128
IN_PAD = 3200
OFF_SBQ, OFF_SBK, OFF_SBV = 0, 512, 1024
OFF_GQ, OFF_GK, OFF_GV, OFF_GG, OFF_GA = 1536, 1792, 2048, 2560, 3072

ADAM_LR = 0.001
ADAM_B1 = 0.9
ADAM_B2 = 0.999
ADAM_EPS = 1e-08
ADAM_WD = 0.01
ADAM_STEP = 10

VMEM_LIMIT = 48 * 1024 * 1024
MESH_ID = pl.DeviceIdType.MESH


def _cparams(sem=None, **kw):
    return pltpu.CompilerParams(dimension_semantics=sem, vmem_limit_bytes=VMEM_LIMIT, **kw)


def _dot(a, b, dims):
    ca, cb = {"nn": (1, 0), "nt": (1, 1), "tn": (0, 0)}[dims]
    return lax.dot_general(a.astype(MXU_DTYPE), b.astype(MXU_DTYPE), (((ca,), (cb,)), ((), ())),
                           preferred_element_type=F32)


def _dot_split(a, b, dims):
    assert dims == "nn"
    hi = a.astype(BF16)
    lo = (a - hi.astype(F32)).astype(BF16)
    return lax.dot_general(jnp.concatenate([hi, lo], axis=1), jnp.concatenate([b, b], axis=0),
                           (((1,), (0,)), ((), ())), preferred_element_type=F32)


def _pick(dim, prefs):
    for p in prefs:
        if dim % p == 0:
            return p
    return dim


def _matmul(a, b, dims, out_dtype, name, res=None, res_scale=1.0):
    if dims == "nn":
        (M, K), (_, N) = a.shape, b.shape
    elif dims == "nt":
        (M, K), (N, _) = a.shape, b.shape
    else:
        (K, M), (_, N) = a.shape, b.shape
    tm = _pick(M, (1024, 1408, 512, 256, 128))
    tn = _pick(N, (640, 512, 256, 128))
    tk = _pick(K, (1024, 1408, 640, 512, 256, 128))
    nk = K // tk
    grid = (M // tm, N // tn, nk)
    if dims == "tn":
        a_spec = pl.BlockSpec((tk, tm), lambda i, j, k: (k, i))
    else:
        a_spec = pl.BlockSpec((tm, tk), lambda i, j, k: (i, k))
    if dims == "nt":
        b_spec = pl.BlockSpec((tn, tk), lambda i, j, k: (j, k))
    else:
        b_spec = pl.BlockSpec((tk, tn), lambda i, j, k: (k, j))
    o_spec = pl.BlockSpec((tm, tn), lambda i, j, k: (i, j))
    in_specs = [a_spec, b_spec]
    args = [a, b]
    if res is not None:
        in_specs.append(o_spec)
        args.append(res)

    def body(*refs):
        if res is not None:
            a_ref, b_ref, r_ref, o_ref, acc_ref = refs
        else:
            a_ref, b_ref, o_ref, acc_ref = refs
            r_ref = None
        k = pl.program_id(2)
        part = _dot(a_ref[...], b_ref[...], dims)

        def finish(total):
            if r_ref is not None:
                total = total + res_scale * r_ref[...]
            o_ref[...] = total.astype(o_ref.dtype)

        if nk == 1:
            finish(part)
        else:
            @pl.when(k == 0)
            def _():
                acc_ref[...] = part

            @pl.when(jnp.logical_and(k > 0, k < nk - 1))
            def _():
                acc_ref[...] += part

            @pl.when(k == nk - 1)
            def _():
                finish(acc_ref[...] + part)

    return pl.pallas_call(
        body, name=name, grid=grid, in_specs=in_specs, out_specs=o_spec,
        out_shape=jax.ShapeDtypeStruct((M, N), out_dtype),
        scratch_shapes=[pltpu.VMEM((tm, tn), F32)],
        compiler_params=_cparams(("parallel", "parallel", "arbitrary")),
    )(*args)


LN_ROWS = 256


def _ln_stats(r):
    mu = jnp.mean(r, axis=-1, keepdims=True)
    xc = r - mu
    var = jnp.mean(xc * xc, axis=-1, keepdims=True)
    return xc * lax.rsqrt(var + LN_EPS)


def _ln_fwd(r, g, b, name):
    S, D = r.shape

    def body(r_ref, g_ref, b_ref, h_ref):
        h_ref[...] = _ln_stats(r_ref[...]) * g_ref[...] + b_ref[...]

    row = pl.BlockSpec((LN_ROWS, D), lambda i: (i, 0))
    vec = pl.BlockSpec((1, D), lambda i: (0, 0))
    return pl.pallas_call(
        body, name=name, grid=(S // LN_ROWS,), in_specs=[row, vec, vec], out_specs=row,
        out_shape=jax.ShapeDtypeStruct((S, D), F32),
        compiler_params=_cparams(("parallel",)),
    )(r, g, b)


def _ln_bwd_core(xhat, dy, g):
    dxh = dy * g
    m1 = jnp.mean(dxh, axis=-1, keepdims=True)
    m2 = jnp.mean(dxh * xhat, axis=-1, keepdims=True)
    return dxh - m1 - xhat * m2


def _ln_bwd(r, dy, g, name):
    S, D = r.shape

    def body(r_ref, dy_ref, g_ref, dr_ref, gg_ref, gb_ref):
        x = r_ref[...]
        mu = jnp.mean(x, axis=-1, keepdims=True)
        xc = x - mu
        rstd = lax.rsqrt(jnp.mean(xc * xc, axis=-1, keepdims=True) + LN_EPS)
        xhat = xc * rstd
        dy = dy_ref[...]
        dr_ref[...] = rstd * _ln_bwd_core(xhat, dy, g_ref[...])

        @pl.when(pl.program_id(0) == 0)
        def _():
            gg_ref[...] = jnp.zeros_like(gg_ref)
            gb_ref[...] = jnp.zeros_like(gb_ref)

        gg_ref[...] += jnp.sum(dy * xhat, axis=0, keepdims=True)
        gb_ref[...] += jnp.sum(dy, axis=0, keepdims=True)

    row = pl.BlockSpec((LN_ROWS, D), lambda i: (i, 0))
    vec = pl.BlockSpec((1, D), lambda i: (0, 0))
    return pl.pallas_call(
        body, name=name, grid=(S // LN_ROWS,), in_specs=[row, row, vec], out_specs=[row, vec, vec],
        out_shape=[jax.ShapeDtypeStruct((S, D), F32), jax.ShapeDtypeStruct((1, D), F32),
                   jax.ShapeDtypeStruct((1, D), F32)],
        compiler_params=_cparams(("arbitrary",)),
    )(r, dy, g)


def _ln_loss_bwd(r, target, g, b, name):
    S, D = r.shape

    def body(r_ref, t_ref, g_ref, b_ref, dr_ref, loss_ref, gg_ref, gb_ref):
        x = r_ref[...]
        mu = jnp.mean(x, axis=-1, keepdims=True)
        xc = x - mu
        rstd = lax.rsqrt(jnp.mean(xc * xc, axis=-1, keepdims=True) + LN_EPS)
        xhat = xc * rstd
        y = xhat * g_ref[...] + b_ref[...]
        err = y - t_ref[...]
        dy = err * (1.0 / D)
        dr_ref[...] = rstd * _ln_bwd_core(xhat, dy, g_ref[...])

        @pl.when(pl.program_id(0) == 0)
        def _():
            loss_ref[...] = jnp.zeros_like(loss_ref)
            gg_ref[...] = jnp.zeros_like(gg_ref)
            gb_ref[...] = jnp.zeros_like(gb_ref)

        per_row = jnp.sum(err * err, axis=-1, keepdims=True) * (0.5 / D)
        loss_ref[...] += jnp.broadcast_to(jnp.sum(per_row, axis=0, keepdims=True), loss_ref.shape)
        gg_ref[...] += jnp.sum(dy * xhat, axis=0, keepdims=True)
        gb_ref[...] += jnp.sum(dy, axis=0, keepdims=True)

    row = pl.BlockSpec((LN_ROWS, D), lambda i: (i, 0))
    vec = pl.BlockSpec((1, D), lambda i: (0, 0))
    lvec = pl.BlockSpec((1, LANE), lambda i: (0, 0))
    return pl.pallas_call(
        body, name=name, grid=(S // LN_ROWS,), in_specs=[row, row, vec, vec],
        out_specs=[row, lvec, vec, vec],
        out_shape=[jax.ShapeDtypeStruct((S, D), F32), jax.ShapeDtypeStruct((1, LANE), F32),
                   jax.ShapeDtypeStruct((1, D), F32), jax.ShapeDtypeStruct((1, D), F32)],
        compiler_params=_cparams(("arbitrary",)),
    )(r, target, g, b)


CONV_COLS = 256
CONV_ROWS = 256
HALO = 8
INV_SQRT2 = 1.0 / math.sqrt(2.0)
INV_SQRT2PI = 1.0 / math.sqrt(2.0 * math.pi)


def _gelu(x):
    return 0.5 * x * (1.0 + lax.erf(x * INV_SQRT2))


def _gelu_grad(x):
    return 0.5 * (1.0 + lax.erf(x * INV_SQRT2)) + x * jnp.exp(-0.5 * x * x) * INV_SQRT2PI


def _conv_rows(ext, w_ref, b_ref, n):
    total = ext.shape[0]
    s1 = pltpu.roll(ext, 1, 0)
    s2 = pltpu.roll(ext, 2, 0)
    u = w_ref[2:3, :] * ext + w_ref[1:2, :] * s1 + w_ref[0:1, :] * s2 + b_ref[...]
    return u[HALO:total], s1[HALO:total], s2[HALO:total]


def _conv_gelu_fwd(u0, conv_w, conv_b, name):
    S, C2 = u0.shape
    F = C2 // 2
    ncb = F // CONV_COLS
    nrc = S // CONV_ROWS

    def body(ua_ref, uc_ref, wa_ref, wc_ref, ba_ref, bc_ref, p_ref):
        def chunk(ci, _):
            r0 = pl.multiple_of(ci * CONV_ROWS, CONV_ROWS)
            p0 = pl.multiple_of(jnp.maximum(r0 - HALO, 0), HALO)
            keep = (ci > 0).astype(F32)

            def load(ref):
                prev = ref[pl.ds(p0, HALO), :] * keep
                return jnp.concatenate([prev, ref[pl.ds(r0, CONV_ROWS), :]], axis=0)

            a, _, _ = _conv_rows(load(ua_ref), wa_ref, ba_ref, CONV_ROWS)
            c, _, _ = _conv_rows(load(uc_ref), wc_ref, bc_ref, CONV_ROWS)
            p_ref[pl.ds(r0, CONV_ROWS), :] = (_gelu(a) * c).astype(p_ref.dtype)
            return 0

        lax.fori_loop(0, nrc, chunk, 0)

    col_a = pl.BlockSpec((S, CONV_COLS), lambda j: (0, j))
    col_c = pl.BlockSpec((S, CONV_COLS), lambda j: (0, j + ncb))
    w_a = pl.BlockSpec((CONV_WIDTH, CONV_COLS), lambda j: (0, j))
    w_c = pl.BlockSpec((CONV_WIDTH, CONV_COLS), lambda j: (0, j + ncb))
    b_a = pl.BlockSpec((1, CONV_COLS), lambda j: (0, j))
    b_c = pl.BlockSpec((1, CONV_COLS), lambda j: (0, j + ncb))
    return pl.pallas_call(
        body, name=name, grid=(ncb,), in_specs=[col_a, col_c, w_a, w_c, b_a, b_c], out_specs=col_a,
        out_shape=jax.ShapeDtypeStruct((S, F), MXU_DTYPE),
        compiler_params=_cparams(("parallel",)),
    )(u0, u0, conv_w, conv_w, conv_b, conv_b)


def _conv_gelu_bwd(u0, dp, conv_w, conv_b, name):
    S, C2 = u0.shape
    F = C2 // 2
    ncb = F // CONV_COLS
    nrc = S // CONV_ROWS
    EXT = CONV_ROWS + HALO

    def body(ua_ref, uc_ref, dp_ref, wa_ref, wc_ref, ba_ref, bc_ref,
             da_ref, dc_ref, gwa_ref, gwc_ref, gba_ref, gbc_ref):
        gwa_ref[...] = jnp.zeros_like(gwa_ref)
        gwc_ref[...] = jnp.zeros_like(gwc_ref)
        gba_ref[...] = jnp.zeros_like(gba_ref)
        gbc_ref[...] = jnp.zeros_like(gbc_ref)
        rid = lax.broadcasted_iota(jnp.int32, (EXT, CONV_COLS), 0)

        def chunk(ci, _):
            r0 = pl.multiple_of(ci * CONV_ROWS, CONV_ROWS)
            p0 = pl.multiple_of(jnp.maximum(r0 - HALO, 0), HALO)
            n0 = pl.multiple_of(jnp.minimum(r0 + CONV_ROWS, S - HALO), HALO)
            keep_prev = (ci > 0).astype(F32)
            keep_next = (ci < nrc - 1).astype(F32)

            def load(ref):
                return jnp.concatenate([ref[pl.ds(p0, HALO), :] * keep_prev,
                                        ref[pl.ds(r0, CONV_ROWS), :],
                                        ref[pl.ds(n0, HALO), :] * keep_next], axis=0)

            ext_a = load(ua_ref)
            ext_c = load(uc_ref)
            a, a1, a2 = _conv_rows(ext_a, wa_ref, ba_ref, EXT)
            c, c1, c2 = _conv_rows(ext_c, wc_ref, bc_ref, EXT)
            a0 = ext_a[HALO:HALO + EXT]
            c0 = ext_c[HALO:HALO + EXT]
            dpe = jnp.concatenate([dp_ref[pl.ds(r0, CONV_ROWS), :].astype(F32),
                                   dp_ref[pl.ds(n0, HALO), :].astype(F32) * keep_next], axis=0)
            d_a = dpe * c * _gelu_grad(a)
            d_c = dpe * _gelu(a)
            own = rid < CONV_ROWS

            def back(d_u, w_ref, x0, x1, x2, d_ref, gw_ref, gb_ref):
                d_u0 = (w_ref[2:3, :] * d_u + w_ref[1:2, :] * pltpu.roll(d_u, EXT - 1, 0)
                        + w_ref[0:1, :] * pltpu.roll(d_u, EXT - 2, 0))
                d_ref[pl.ds(r0, CONV_ROWS), :] = d_u0[0:CONV_ROWS].astype(d_ref.dtype)
                d_own = jnp.where(own, d_u, 0.0)
                gw_ref[...] += jnp.concatenate(
                    [jnp.sum(d_own * x2, axis=0, keepdims=True),
                     jnp.sum(d_own * x1, axis=0, keepdims=True),
                     jnp.sum(d_own * x0, axis=0, keepdims=True)], axis=0)
                gb_ref[...] += jnp.sum(d_own, axis=0, keepdims=True)

            back(d_a, wa_ref, a0, a1, a2, da_ref, gwa_ref, gba_ref)
            back(d_c, wc_ref, c0, c1, c2, dc_ref, gwc_ref, gbc_ref)
            return 0

        lax.fori_loop(0, nrc, chunk, 0)

    col_a = pl.BlockSpec((S, CONV_COLS), lambda j: (0, j))
    col_c = pl.BlockSpec((S, CONV_COLS), lambda j: (0, j + ncb))
    w_a = pl.BlockSpec((CONV_WIDTH, CONV_COLS), lambda j: (0, j))
    w_c = pl.BlockSpec((CONV_WIDTH, CONV_COLS), lambda j: (0, j + ncb))
    b_a = pl.BlockSpec((1, CONV_COLS), lambda j: (0, j))
    b_c = pl.BlockSpec((1, CONV_COLS), lambda j: (0, j + ncb))
    outs = pl.pallas_call(
        body, name=name, grid=(ncb,),
        in_specs=[col_a, col_c, col_a, w_a, w_c, b_a, b_c],
        out_specs=[col_a, col_a, w_a, w_a, b_a, b_a],
        out_shape=[jax.ShapeDtypeStruct((S, F), MXU_DTYPE), jax.ShapeDtypeStruct((S, F), MXU_DTYPE),
                   jax.ShapeDtypeStruct((CONV_WIDTH, F), F32), jax.ShapeDtypeStruct((CONV_WIDTH, F), F32),
                   jax.ShapeDtypeStruct((1, F), F32), jax.ShapeDtypeStruct((1, F), F32)],
        compiler_params=_cparams(("parallel",)),
    )(u0, u0, dp, conv_w, conv_w, conv_b, conv_b)
    da, dc, gwa, gwc, gba, gbc = outs
    return (jnp.concatenate([da, dc], axis=1), jnp.concatenate([gwa, gwc], axis=1),
            jnp.concatenate([gba, gbc], axis=1))


SB_TK = 128
SB_TQ_FWD = 1024
SB_TQ_BWD = 1024
SB_HPS = 2


def _softplus(z):
    return jnp.maximum(z, 0.0) + jnp.log(1.0 + jnp.exp(-jnp.abs(z)))


def _tri_ones(after):
    r = lax.broadcasted_iota(jnp.int32, (SB_TK, 2 * SB_TK), 0)
    c = lax.broadcasted_iota(jnp.int32, (SB_TK, 2 * SB_TK), 1)
    tri = (r > c) if after else (r < c)
    return jnp.where(c >= SB_TK, 1.0, jnp.where(tri, 1.0, 0.0)).astype(BF16)


def _sb_fwd(q, k, v, name):
    H, S, Dh = q.shape
    TK = SB_TK
    TQ = min(SB_TQ_FWD, S)
    R = TQ // TK
    HP = SB_HPS

    def body(q_ref, k_ref, v_ref, o_ref, t_ref):
        i = pl.program_id(1)
        row = lax.broadcasted_iota(jnp.int32, (TQ, TK), 0)
        col = lax.broadcasted_iota(jnp.int32, (TQ, TK), 1)
        after_ones = _tri_ones(True)

        def block(j, carry, r0):
            k0 = pl.multiple_of(j * TK, TK)
            masked = r0 is not None
            r0 = r0 or 0
            out = []
            for hh in range(HP):
                acc, tail = carry[hh]
                kb = k_ref[hh, pl.ds(k0, TK), :]
                vb = v_ref[hh, pl.ds(k0, TK), :]
                z = _dot(q_ref[hh, r0:, :], kb, "nt")
                sp = _softplus(z)
                if masked:
                    strict = col[r0:] < row[r0:] - r0
                    sp = jnp.where(strict, sp, 0.0)
                cs = _dot_split(sp, after_ones, "nn")
                w = jnp.exp(z - sp - cs[:, :TK] - tail[r0:])
                if masked:
                    w = jnp.where(strict, w, 0.0)
                pv = _dot(w, vb, "nn")
                if r0:
                    pv = jnp.concatenate([jnp.zeros((r0, Dh), F32), pv], axis=0)
                    tot = jnp.concatenate([jnp.zeros((r0, TK), F32), cs[:, TK:]], axis=0)
                else:
                    tot = cs[:, TK:]
                out.append((acc + pv, tail + tot))
            return tuple(out)

        carry = tuple((jnp.zeros((TQ, Dh), F32), jnp.zeros((TQ, TK), F32)) for _ in range(HP))
        for u in reversed(range(R)):
            carry = block(R * i + u, carry, u * TK)
        carry = lax.fori_loop(0, R * i, lambda n, c: block(R * i - 1 - n, c, None), carry)
        for hh in range(HP):
            o_ref[hh] = carry[hh][0]
            t_ref[hh] = carry[hh][1]

    qspec = pl.BlockSpec((HP, TQ, Dh), lambda h, i: (h, i, 0))
    full = pl.BlockSpec((HP, S, Dh), lambda h, i: (h, 0, 0))
    tspec = pl.BlockSpec((HP, TQ, TK), lambda h, i: (h, i, 0))
    return pl.pallas_call(
        body, name=name, grid=(H // HP, S // TQ), in_specs=[qspec, full, full], out_specs=[qspec, tspec],
        out_shape=[jax.ShapeDtypeStruct((H, S, Dh), F32), jax.ShapeDtypeStruct((H, S, TK), F32)],
        compiler_params=_cparams(("parallel", "arbitrary")),
    )(q, k, v)


def _sb_bwd(qdo, kz, zv, total, name):
    H, S, D2 = qdo.shape
    Dh = D2 // 2
    TK = SB_TK
    TQ = min(SB_TQ_BWD, S)
    R = TQ // TK
    HP = SB_HPS
    scale = Dh ** -0.5

    def body(qdo_ref, kz_ref, zv_ref, t_ref, dq_ref, dk_ref, dv_ref):
        i = pl.program_id(1)

        @pl.when(i == 0)
        def _():
            dk_ref[...] = jnp.zeros_like(dk_ref)
            dv_ref[...] = jnp.zeros_like(dv_ref)

        row = lax.broadcasted_iota(jnp.int32, (TQ, TK), 0)
        col = lax.broadcasted_iota(jnp.int32, (TQ, TK), 1)
        after_ones = _tri_ones(True)
        before_ones = _tri_ones(False)

        def block(j, carry, r0):
            k0 = pl.multiple_of(j * TK, TK)
            masked = r0 is not None
            r0 = r0 or 0
            out = []
            for hh in range(HP):
                dq, seen, gsum = carry[hh]
                qd = qdo_ref[hh, r0:, :]
                kzb = kz_ref[hh, pl.ds(k0, TK), :]
                zvb = zv_ref[hh, pl.ds(k0, TK), :]
                zdw = _dot(qd, jnp.concatenate([kzb, zvb], axis=0), "nt")
                z = zdw[:, :TK]
                sp = _softplus(z)
                logsig = z - sp
                if masked:
                    strict = col[r0:] < row[r0:] - r0
                    sp = jnp.where(strict, sp, 0.0)
                cs = _dot_split(sp, after_ones, "nn")
                seen_r = seen[r0:] + cs[:, TK:]
                w = jnp.exp(logsig - cs[:, :TK] - (t_ref[hh, r0:, :] - seen_r))
                if masked:
                    w = jnp.where(strict, w, 0.0)
                g = w * zdw[:, TK:]
                cg = _dot_split(g, before_ones, "nn")
                dz = g - jnp.exp(logsig) * (g + cg[:, :TK] + gsum[r0:])
                if masked:
                    dz = jnp.where(strict, dz, 0.0)
                dzb = dz.astype(MXU_DTYPE)
                dk_ref[hh, pl.ds(k0, TK), :] += _dot(dzb, qd, "tn")
                dv_ref[hh, pl.ds(k0, TK), :] += _dot(w, qd, "tn")
                dq_r = dq[r0:] + _dot(dzb, kzb, "nn")
                gsum_r = gsum[r0:] + cg[:, TK:]
                if r0:
                    dq_r = jnp.concatenate([dq[:r0], dq_r], axis=0)
                    seen_r = jnp.concatenate([seen[:r0], seen_r], axis=0)
                    gsum_r = jnp.concatenate([gsum[:r0], gsum_r], axis=0)
                out.append((dq_r, seen_r, gsum_r))
            return tuple(out)

        zero = jnp.zeros((TQ, TK), F32)
        carry = tuple((jnp.zeros((TQ, D2), F32), zero, zero) for _ in range(HP))
        carry = lax.fori_loop(0, R * i, lambda j, c: block(j, c, None), carry)
        for u in range(R):
            carry = block(R * i + u, carry, u * TK)
        for hh in range(HP):
            dq_ref[hh] = carry[hh][0] * scale

    qspec = pl.BlockSpec((HP, TQ, D2), lambda h, i: (h, i, 0))
    full = pl.BlockSpec((HP, S, D2), lambda h, i: (h, 0, 0))
    tspec = pl.BlockSpec((HP, TQ, TK), lambda h, i: (h, i, 0))
    sds = jax.ShapeDtypeStruct((H, S, D2), F32)
    dq, dk, dv = pl.pallas_call(
        body, name=name, grid=(H // HP, S // TQ), in_specs=[qspec, full, full, tspec],
        out_specs=[qspec, full, full], out_shape=[sds, sds, sds],
        compiler_params=_cparams(("parallel", "arbitrary")),
    )(qdo, kz, zv, total)
    return dq[:, :, :Dh], dk[:, :, :Dh], dv[:, :, Dh:]


GLA_GROUP_FWD = 8
GLA_GROUP_BWD = 4


def _gla_masks():
    C = CHUNK
    row = lax.broadcasted_iota(jnp.int32, (C, C), 0)
    col = lax.broadcasted_iota(jnp.int32, (C, C), 1)
    return row, col


def _log_sigmoid(x):
    return -_softplus(-x)


def _gla_chunk_fwd(qc, kc, vc, gate, row, col):
    C = CHUNK
    la = _log_sigmoid(gate) * (1.0 / GLA_TAU)
    incl = jnp.where(row >= col, 1.0, 0.0).astype(BF16)
    b = _dot_split_lhs01(incl, la)
    b_ref = jnp.sum(jnp.where(row == C // 2 - 1, b, 0.0), axis=0, keepdims=True)
    b_last = jnp.sum(la, axis=0, keepdims=True)
    qs = qc * (GLA_KEY_DIM ** -0.5)
    q_in = qs * jnp.exp(b - b_ref)
    k_in = kc * jnp.exp(b_ref - b)
    k_dec = kc * jnp.exp(b_last - b)
    q_b = qs * jnp.exp(b)
    sc = jnp.where(row >= col, _dot(q_in, k_in, "nt"), 0.0)
    o_intra = _dot(sc, vc, "nn")
    upd = _dot(k_dec, vc, "tn")
    ones = jnp.ones((C, GLA_VAL_DIM), BF16)
    dec_col = jnp.exp(_dot_split_tn(la, ones))
    return dict(la=la, b=b, b_ref=b_ref, b_last=b_last, qs=qs, q_in=q_in, k_in=k_in, k_dec=k_dec,
                q_b=q_b, sc=sc, o_intra=o_intra, upd=upd, dec_col=dec_col)


def _dot_split_lhs01(m01, x):
    hi = x.astype(BF16)
    lo = (x - hi.astype(F32)).astype(BF16)
    dn = (((1,), (0,)), ((), ()))
    return (lax.dot_general(m01, hi, dn, preferred_element_type=F32)
            + lax.dot_general(m01, lo, dn, preferred_element_type=F32))


def _dot_split_tn(x, m01):
    hi = x.astype(BF16)
    lo = (x - hi.astype(F32)).astype(BF16)
    dn = (((0,), (0,)), ((), ()))
    return (lax.dot_general(hi, m01, dn, preferred_element_type=F32)
            + lax.dot_general(lo, m01, dn, preferred_element_type=F32))


def _dot_split_nt01(m01, x):
    hi = x.astype(BF16)
    lo = (x - hi.astype(F32)).astype(BF16)
    dn = (((1,), (1,)), ((), ()))
    return (lax.dot_general(m01, hi, dn, preferred_element_type=F32)
            + lax.dot_general(m01, lo, dn, preferred_element_type=F32))


def _rms_gate(o, gg, gnorm):
    rinv = lax.rsqrt(jnp.mean(o * o, axis=-1, keepdims=True) + RMS_EPS)
    o_n = o * rinv
    sg = 1.0 / (1.0 + jnp.exp(-gg))
    return o_n, rinv, sg


def _gla_fwd(gq, gk, gv, gg, ga_pad, gate_up_h, gate_bias_h, gnorm, name):
    Hg, S, dk = gq.shape
    dv = GLA_VAL_DIM
    C = CHUNK
    G = GLA_GROUP_FWD
    nchunk = S // C
    ngroup = nchunk // G

    def body(q_ref, k_ref, v_ref, gg_ref, ga_ref, gu_ref, gb_ref, gn_ref, o_ref, prev_ref):
        row, col = _gla_masks()
        gu = gu_ref[0]
        gbias = gb_ref[0]
        gnorm_v = gn_ref[...]

        def group(gi, state):
            for u in range(G):
                ci = gi * G + u
                r0 = pl.multiple_of(ci * C, C)
                rows = pl.ds(r0, C)
                gate = _dot(ga_ref[rows, :], gu, "nn") + gbias
                f = _gla_chunk_fwd(q_ref[0, rows, :], k_ref[0, rows, :], v_ref[rows, :], gate, row, col)
                prev_ref[0, ci] = state
                o = f["o_intra"] + _dot(f["q_b"], state, "nn")
                state = f["dec_col"] * state + f["upd"]
                o_n, _, sg = _rms_gate(o, gg_ref[rows, :], gnorm_v)
                o_ref[rows, :] = o_n * gnorm_v * (gg_ref[rows, :] * sg)
            return state

        lax.fori_loop(0, ngroup, group, jnp.zeros((dk, dv), F32))

    hspec = pl.BlockSpec((1, S, dk), lambda h: (h, 0, 0))
    vspec = pl.BlockSpec((S, dv), lambda h: (0, h))
    return pl.pallas_call(
        body, name=name, grid=(Hg,),
        in_specs=[hspec, hspec, vspec, vspec,
                  pl.BlockSpec((S, LANE), lambda h: (0, 0)),
                  pl.BlockSpec((1, LANE, dk), lambda h: (h, 0, 0)),
                  pl.BlockSpec((1, 1, dk), lambda h: (h, 0, 0)),
                  pl.BlockSpec((1, dv), lambda h: (0, 0))],
        out_specs=[vspec, pl.BlockSpec((1, nchunk, dk, dv), lambda h: (h, 0, 0, 0))],
        out_shape=[jax.ShapeDtypeStruct((S, Hg * dv), F32),
                   jax.ShapeDtypeStruct((Hg, nchunk, dk, dv), F32)],
        compiler_params=_cparams(("parallel",)),
    )(gq, gk, gv, gg, ga_pad, gate_up_h, gate_bias_h, gnorm)


def _gla_bwd(gq, gk, gv, gg, ga_pad, gate_up_h, gate_bias_h, gnorm, prev, d_out, name):
    Hg, S, dk = gq.shape
    dv = GLA_VAL_DIM
    C = CHUNK
    G = GLA_GROUP_BWD
    nchunk = S // C
    ngroup = nchunk // G

    def body(q_ref, k_ref, v_ref, gg_ref, ga_ref, gu_ref, gb_ref, gn_ref, prev_ref, do_ref,
             dq_ref, dk_ref, dv_ref, dgg_ref, dga_ref, ggu_ref, ggb_ref, ggn_ref):
        h = pl.program_id(0)
        row, col = _gla_masks()
        gu = gu_ref[0]
        gbias = gb_ref[0]
        gnorm_v = gn_ref[...]
        ggu_ref[...] = jnp.zeros_like(ggu_ref)
        ggb_ref[...] = jnp.zeros_like(ggb_ref)

        @pl.when(h == 0)
        def _():
            dga_ref[...] = jnp.zeros_like(dga_ref)
            ggn_ref[...] = jnp.zeros_like(ggn_ref)

        upper_incl = jnp.where(col >= row, 1.0, 0.0).astype(BF16)
        ones_8 = jnp.ones((8, dv), BF16)

        def group(gn, dstate):
            gi = ngroup - 1 - gn
            for u in reversed(range(G)):
                ci = gi * G + u
                r0 = pl.multiple_of(ci * C, C)
                rows = pl.ds(r0, C)
                ga = ga_ref[rows, :]
                gate = _dot(ga, gu, "nn") + gbias
                qc, kc, vc = q_ref[0, rows, :], k_ref[0, rows, :], v_ref[rows, :]
                f = _gla_chunk_fwd(qc, kc, vc, gate, row, col)
                state = prev_ref[0, ci]
                o = f["o_intra"] + _dot(f["q_b"], state, "nn")
                ggv = gg_ref[rows, :]
                o_n, rinv, sg = _rms_gate(o, ggv, gnorm_v)
                dout = do_ref[rows, :]
                silu = ggv * sg
                dgg_ref[rows, :] = dout * o_n * gnorm_v * (sg * (1.0 + ggv * (1.0 - sg)))
                d_ong = dout * silu
                ggn_ref[...] += jnp.sum(d_ong * o_n, axis=0, keepdims=True)
                d_on = d_ong * gnorm_v
                d_o = rinv * (d_on - o_n * jnp.mean(d_on * o_n, axis=-1, keepdims=True))
                d_upd = dstate
                d_dec_col = dstate * state * f["dec_col"]
                dstate = f["dec_col"] * dstate + _dot(f["q_b"], d_o, "tn")
                dsc = jnp.where(row >= col, _dot(d_o, vc, "nt"), 0.0)
                dv_ref[rows, :] = _dot(f["sc"], d_o, "tn") + _dot(f["k_dec"], d_upd, "nn")
                dq_in = _dot(dsc, f["k_in"], "nn")
                dk_in = _dot(dsc, f["q_in"], "tn")
                dq_b = _dot(d_o, state, "nt")
                dkdec = _dot(vc, d_upd, "nt")
                b = f["b"]
                e1 = jnp.exp(b - f["b_ref"])
                e2 = jnp.exp(f["b_ref"] - b)
                e3 = jnp.exp(f["b_last"] - b)
                eb = jnp.exp(b)
                dq_ref[0, rows, :] = (dq_in * e1 + dq_b * eb) * (GLA_KEY_DIM ** -0.5)
                dk_ref[0, rows, :] = dk_in * e2 + dkdec * e3
                t_q = dq_in * f["q_in"]
                t_k = dk_in * f["k_in"]
                t_d = dkdec * f["k_dec"]
                db = t_q - t_k - t_d + dq_b * f["q_b"]
                db_ref = jnp.sum(t_k - t_q, axis=0, keepdims=True)
                db_last = (jnp.sum(t_d, axis=0, keepdims=True)
                           + jnp.max(_dot_split_nt01(ones_8, d_dec_col), axis=0, keepdims=True))
                db = db + jnp.where(row == C // 2 - 1, db_ref, 0.0) + jnp.where(row == C - 1, db_last, 0.0)
                dla = _dot_split_lhs01(upper_incl, db)
                d_gate = dla * (1.0 / GLA_TAU) * (1.0 / (1.0 + jnp.exp(gate)))
                ggb_ref[0] += jnp.sum(d_gate, axis=0, keepdims=True)
                ggu_ref[0] += _dot(ga, d_gate, "tn")
                dga_ref[rows, :] += _dot(d_gate, gu, "nt")
            return dstate

        lax.fori_loop(0, ngroup, group, jnp.zeros((dk, dv), F32))

    hspec = pl.BlockSpec((1, S, dk), lambda h: (h, 0, 0))
    vspec = pl.BlockSpec((S, dv), lambda h: (0, h))
    gaspec = pl.BlockSpec((S, LANE), lambda h: (0, 0))
    guspec = pl.BlockSpec((1, LANE, dk), lambda h: (h, 0, 0))
    gbspec = pl.BlockSpec((1, 1, dk), lambda h: (h, 0, 0))
    gnspec = pl.BlockSpec((1, dv), lambda h: (0, 0))
    return pl.pallas_call(
        body, name=name, grid=(Hg,),
        in_specs=[hspec, hspec, vspec, vspec, gaspec, guspec, gbspec,
                  pl.BlockSpec((1, dv), lambda h: (0, 0)),
                  pl.BlockSpec((1, nchunk, dk, dv), lambda h: (h, 0, 0, 0)), vspec],
        out_specs=[hspec, hspec, vspec, vspec, gaspec, guspec, gbspec, gnspec],
        out_shape=[jax.ShapeDtypeStruct((Hg, S, dk), F32), jax.ShapeDtypeStruct((Hg, S, dk), F32),
                   jax.ShapeDtypeStruct((S, Hg * dv), F32), jax.ShapeDtypeStruct((S, Hg * dv), F32),
                   jax.ShapeDtypeStruct((S, LANE), F32), jax.ShapeDtypeStruct((Hg, LANE, dk), F32),
                   jax.ShapeDtypeStruct((Hg, 1, dk), F32), jax.ShapeDtypeStruct((1, dv), F32)],
        compiler_params=_cparams(("arbitrary",)),
    )(gq, gk, gv, gg, ga_pad, gate_up_h, gate_bias_h, gnorm, prev, d_out)


def _exchange(items, name):
    n = len(items)
    n_peer = N_DEV - 1

    def body(*refs):
        ins, outs = refs[:n], refs[n:2 * n]
        send_sems, recv_sems, local_sems = refs[2 * n:]
        x, y, c = lax.axis_index("x"), lax.axis_index("y"), lax.axis_index("c")
        me = 4 * x + 2 * y + c
        local, remote = [], []
        for a, (_, scatter) in enumerate(items):
            own = ins[a].at[me] if scatter else ins[a]
            cp = pltpu.make_async_copy(own, outs[a].at[me], local_sems.at[a])
            cp.start()
            local.append(cp)
        for r in range(1, N_DEV):
            px = 1 - x if r & 4 else x
            py = 1 - y if r & 2 else y
            pc = 1 - c if r & 1 else c
            for a, (_, scatter) in enumerate(items):
                src = ins[a].at[4 * px + 2 * py + pc] if scatter else ins[a]
                cp = pltpu.make_async_remote_copy(
                    src_ref=src, dst_ref=outs[a].at[me],
                    send_sem=send_sems.at[a * n_peer + r - 1], recv_sem=recv_sems.at[a * n_peer + r - 1],
                    device_id=(px, py, pc), device_id_type=MESH_ID)
                cp.start()
                remote.append(cp)
        for cp in remote:
            cp.wait()
        for cp in local:
            cp.wait()

    out_shape = []
    for arr, scatter in items:
        shp = arr.shape if scatter else (N_DEV,) + arr.shape
        out_shape.append(jax.ShapeDtypeStruct(shp, arr.dtype))
    any_spec = pl.BlockSpec(memory_space=pl.ANY)
    return pl.pallas_call(
        body, name=name, in_specs=[any_spec] * n, out_specs=[any_spec] * n, out_shape=out_shape,
        scratch_shapes=[pltpu.SemaphoreType.DMA((n * n_peer,)), pltpu.SemaphoreType.DMA((n * n_peer,)),
                        pltpu.SemaphoreType.DMA((n,))],
        compiler_params=pltpu.CompilerParams(has_side_effects=True),
    )(*[arr for arr, _ in items])


def _adamw(grecv, w, m, v, name):
    R, C = w.shape
    tile = _pick(R, (256, 176, 128)) if R * C > 65536 else R
    bc1 = 1.0 - ADAM_B1 ** ADAM_STEP
    bc2 = 1.0 - ADAM_B2 ** ADAM_STEP

    def body(gr_ref, w_ref, m_ref, v_ref, g_ref, d_ref, nm_ref, nv_ref):
        g = gr_ref[0].astype(F32)
        for q in range(1, N_DEV):
            g = g + gr_ref[q].astype(F32)
        nm = ADAM_B1 * m_ref[...] + (1.0 - ADAM_B1) * g
        nv = ADAM_B2 * v_ref[...] + (1.0 - ADAM_B2) * (g * g)
        m_hat = nm / bc1
        v_hat = nv / bc2
        g_ref[...] = g
        d_ref[...] = -ADAM_LR * (m_hat / (jnp.sqrt(v_hat) + ADAM_EPS) + ADAM_WD * w_ref[...])
        nm_ref[...] = nm
        nv_ref[...] = nv

    blk = pl.BlockSpec((tile, C), lambda i: (i, 0))
    sds = jax.ShapeDtypeStruct((R, C), F32)
    return pl.pallas_call(
        body, name=name, grid=(R // tile,),
        in_specs=[pl.BlockSpec((N_DEV, tile, C), lambda i: (0, i, 0)), blk, blk, blk],
        out_specs=[blk, blk, blk, blk], out_shape=[sds, sds, sds, sds],
        compiler_params=_cparams(("parallel",)),
    )(grecv, w, m, v)


def _pack_rows(parts, rows, dtype):
    flat = jnp.concatenate([p.reshape(-1).astype(dtype) for p in parts])
    return jnp.pad(flat, (0, rows * LANE - flat.shape[0])).reshape(rows, LANE)


def _unpack_rows(buf, shapes):
    flat = buf.reshape(-1)
    out, off = [], 0
    for shp in shapes:
        n = math.prod(shp)
        out.append(flat[off:off + n].reshape(shp))
        off += n
    return out


def _shard_cols(g):
    rows, cols = g.shape
    return g.reshape(rows, N_DEV, cols // N_DEV).transpose(1, 0, 2)


def _unshard_cols(blocks):
    return blocks.transpose(1, 0, 2).reshape(blocks.shape[1], -1)


def _heads(t, n, d):
    return t.reshape(t.shape[0], n, d).transpose(1, 0, 2)


def _unheads(t):
    return t.transpose(1, 0, 2).reshape(t.shape[1], -1)


def kernel(x, w_in, gate_up, gate_bias, gla_norm_g, w_out, ln1_g, ln1_b, w_up, conv_w, conv_b, w_down, ln2_g, ln2_b, loss_target, m_w_in, m_gate_up, m_gate_bias, m_gla_norm_g, m_w_out, m_ln1_g, m_ln1_b, m_w_up, m_conv_w, m_conv_b, m_w_down, m_ln2_g, m_ln2_b, v_w_in, v_gate_up, v_gate_bias, v_gla_norm_g, v_w_out, v_ln1_g, v_ln1_b, v_w_up, v_conv_w, v_conv_b, v_w_down, v_ln2_g, v_ln2_b):
    S, D = x.shape[1], x.shape[2]
    x2, tgt = x[0], loss_target[0]

    gathered = _exchange([(w_in[0].astype(MXU_DTYPE), False), (w_out[0].astype(MXU_DTYPE), False),
                          (w_up[0].astype(MXU_DTYPE), False), (w_down[0].astype(MXU_DTYPE), False),
                          (gate_up[0], False), (conv_w[0], False)], "gather_weights")
    w_in_f = _unshard_cols(gathered[0])
    w_out_f = gathered[1].reshape(-1, D)
    w_up_f = _unshard_cols(gathered[2])
    w_down_f = gathered[3].reshape(-1, D)
    gate_up_f = _unshard_cols(gathered[4])
    conv_w_f = _unshard_cols(gathered[5])
    w_in_pad = jnp.pad(w_in_f, ((0, 0), (0, IN_PAD - IN_WIDTH)))
    gate_up_h = _heads(jnp.pad(gate_up_f, ((0, LANE - GLA_GATE_RANK), (0, 0))), GLA_HEADS, GLA_KEY_DIM)
    gate_bias_h = gate_bias.reshape(GLA_HEADS, 1, GLA_KEY_DIM)

    proj = _matmul(x2, w_in_pad, "nn", F32, "proj")
    sq = _heads(proj[:, OFF_SBQ:OFF_SBK] * (SB_HEAD_DIM ** -0.5), SB_HEADS, SB_HEAD_DIM).astype(MXU_DTYPE)
    sk = _heads(proj[:, OFF_SBK:OFF_SBV], SB_HEADS, SB_HEAD_DIM).astype(MXU_DTYPE)
    sv = _heads(proj[:, OFF_SBV:OFF_GQ], SB_HEADS, SB_HEAD_DIM).astype(MXU_DTYPE)
    sb_o_h, sb_tot = _sb_fwd(sq, sk, sv, "sb_fwd")
    gq = _heads(proj[:, OFF_GQ:OFF_GK], GLA_HEADS, GLA_KEY_DIM)
    gk = _heads(proj[:, OFF_GK:OFF_GV], GLA_HEADS, GLA_KEY_DIM)
    gv, gg, ga_pad = proj[:, OFF_GV:OFF_GG], proj[:, OFF_GG:OFF_GA], proj[:, OFF_GA:IN_PAD]
    gla_o, prev = _gla_fwd(gq, gk, gv, gg, ga_pad, gate_up_h, gate_bias_h, gla_norm_g, "gla_fwd")
    cat = jnp.concatenate([_unheads(sb_o_h), gla_o], axis=1)
    r1 = _matmul(cat, w_out_f, "nn", F32, "mix", res=x2, res_scale=DN_ALPHA)
    h = _ln_fwd(r1, ln1_g, ln1_b, "ln1")
    u0 = _matmul(h, w_up_f, "nn", F32, "ffn_up")
    p = _conv_gelu_fwd(u0, conv_w_f, conv_b, "conv_gelu")
    r2 = _matmul(p, w_down_f, "nn", F32, "ffn_down", res=h, res_scale=DN_ALPHA)
    d_r2, loss_p, g_ln2_g, g_ln2_b = _ln_loss_bwd(r2, tgt, ln2_g, ln2_b, "ln2_loss")

    d_p = _matmul(d_r2, w_down_f, "nt", MXU_DTYPE, "d_ffn_act")
    g_w_down = _matmul(p, d_r2, "tn", F32, "grad_w_down")
    d_u0, g_conv_w, g_conv_b = _conv_gelu_bwd(u0, d_p, conv_w_f, conv_b, "conv_gelu_bwd")
    g_w_up = _matmul(h, d_u0, "tn", F32, "grad_w_up")
    d_h = _matmul(d_u0, w_up_f, "nt", F32, "d_h", res=d_r2, res_scale=DN_ALPHA)
    d_r1, g_ln1_g, g_ln1_b = _ln_bwd(r1, d_h, ln1_g, "ln1_bwd")
    g_w_out = _matmul(cat, d_r1, "tn", F32, "grad_w_out")
    d_cat = _matmul(d_r1, w_out_f, "nt", F32, "d_cat")
    (d_gq, d_gk, d_gv, d_gg, d_ga_pad, g_gu_h, g_gb_h, g_gnorm) = _gla_bwd(
        gq, gk, gv, gg, ga_pad, gate_up_h, gate_bias_h, gla_norm_g, prev, d_cat[:, SB_WIDTH:], "gla_bwd")
    d_sb_o = _heads(d_cat[:, :SB_WIDTH], SB_HEADS, SB_HEAD_DIM).astype(MXU_DTYPE)
    d_sq, d_sk, d_sv = _sb_bwd(jnp.concatenate([sq, d_sb_o], axis=2),
                               jnp.concatenate([sk, jnp.zeros_like(sk)], axis=2),
                               jnp.concatenate([jnp.zeros_like(sv), sv], axis=2), sb_tot, "sb_bwd")
    d_proj = jnp.concatenate([_unheads(d_sq), _unheads(d_sk), _unheads(d_sv), _unheads(d_gq), _unheads(d_gk),
                              d_gv, d_gg, d_ga_pad], axis=1)
    g_w_in = _matmul(x2, d_proj, "tn", F32, "grad_w_in")[:, :IN_WIDTH]
    d_x = _matmul(d_proj, w_in_pad, "nt", F32, "d_x", res=d_r1, res_scale=DN_ALPHA)
    g_gate_up = _unheads(g_gu_h[:, :GLA_GATE_RANK, :])
    g_gate_bias = g_gb_h.reshape(1, -1)

    small_g = [g_gate_bias, g_gnorm, g_ln1_g, g_ln1_b, g_conv_b, g_ln2_g, g_ln2_b, loss_p]
    n_small = sum(t.size for t in small_g)
    rows_small = -(-n_small // (8 * LANE)) * 8
    recv = _exchange([(_shard_cols(g_w_in).astype(BF16), True),
                      (g_w_out.reshape(N_DEV, -1, D).astype(BF16), True),
                      (_shard_cols(g_w_up).astype(BF16), True),
                      (g_w_down.reshape(N_DEV, -1, D).astype(BF16), True),
                      (_shard_cols(g_gate_up), True), (_shard_cols(g_conv_w), True),
                      (_pack_rows(small_g, rows_small, F32), False)], "exchange_grads")
    sharded = [(w_in, m_w_in, v_w_in), (w_out, m_w_out, v_w_out), (w_up, m_w_up, v_w_up),
               (w_down, m_w_down, v_w_down), (gate_up, m_gate_up, v_gate_up), (conv_w, m_conv_w, v_conv_w)]
    upd = [_adamw(recv[n], w[0], m[0], v[0], "adamw_%d" % n) for n, (w, m, v) in enumerate(sharded)]
    zl = jnp.zeros((1, LANE), F32)

    def pks(parts):
        return _pack_rows(parts + [zl], rows_small, F32)

    small = _adamw(recv[6], pks([gate_bias, gla_norm_g, ln1_g, ln1_b, conv_b, ln2_g, ln2_b]),
                   pks([m_gate_bias, m_gla_norm_g, m_ln1_g, m_ln1_b, m_conv_b, m_ln2_g, m_ln2_b]),
                   pks([v_gate_bias, v_gla_norm_g, v_ln1_g, v_ln1_b, v_conv_b, v_ln2_g, v_ln2_b]),
                   "adamw_replicated")
    small_shapes = [gate_bias.shape, gla_norm_g.shape, ln1_g.shape, ln1_b.shape, conv_b.shape, ln2_g.shape,
                    ln2_b.shape, (1, LANE)]
    outs = []
    loss = None
    for kind in range(4):
        b_w_in, b_w_out, b_w_up, b_w_down, b_gate_up, b_conv_w = [u[kind][None] for u in upd]
        s_gb, s_gn, s_l1g, s_l1b, s_cb, s_l2g, s_l2b, s_loss = _unpack_rows(small[kind], small_shapes)
        if kind == 0:
            loss = s_loss[0, 0]
        outs += [b_w_in, b_gate_up, s_gb, s_gn, b_w_out, s_l1g, s_l1b, b_w_up, b_conv_w, s_cb, b_w_down,
                 s_l2g, s_l2b]
    return (loss, d_x[None], *outs)
```

```python
import math

import jax
import jax.numpy as jnp
from jax import lax
from jax.experimental import pallas as pl
from jax.experimental.pallas import tpu as pltpu

F32 = jnp.float32
BF16 = jnp.bfloat16
MXU_DTYPE = jnp.bfloat16

N_DEV = 8
D_MODEL = 1024
SB_WIDTH = 512
SB_HEADS = 8
SB_HEAD_DIM = 64
GLA_HEADS = 4
GLA_KEY_DIM = 64
GLA_VAL_DIM = 128
GLA_WIDTH = 512
GLA_GATE_RANK = 16
GLA_TAU = 16.0
CHUNK = 64
D_FF = 2816
CONV_WIDTH = 3
LN_EPS = 1e-5
RMS_EPS = 1e-6
DN_ALPHA = 2.0 ** 0.25
IN_WIDTH = 3088
LANE = 128
IN_PAD = 3200
OFF_SBQ, OFF_SBK, OFF_SBV = 0, 512, 1024
OFF_GQ, OFF_GK, OFF_GV, OFF_GG, OFF_GA = 1536, 1792, 2048, 2560, 3072

ADAM_LR = 0.001
ADAM_B1 = 0.9
ADAM_B2 = 0.999
ADAM_EPS = 1e-08
ADAM_WD = 0.01
ADAM_STEP = 10

VMEM_LIMIT = 48 * 1024 * 1024
MESH_ID = pl.DeviceIdType.MESH


def _cparams(sem=None, **kw):
    return pltpu.CompilerParams(dimension_semantics=sem, vmem_limit_bytes=VMEM_LIMIT, **kw)


def _dot(a, b, dims):
    ca, cb = {"nn": (1, 0), "nt": (1, 1), "tn": (0, 0)}[dims]
    return lax.dot_general(a.astype(MXU_DTYPE), b.astype(MXU_DTYPE), (((ca,), (cb,)), ((), ())),
                           preferred_element_type=F32)


def _dot_split(a, b, dims):
    assert dims == "nn"
    hi = a.astype(BF16)
    lo = (a - hi.astype(F32)).astype(BF16)
    return lax.dot_general(jnp.concatenate([hi, lo], axis=1), jnp.concatenate([b, b], axis=0),
                           (((1,), (0,)), ((), ())), preferred_element_type=F32)


def _pick(dim, prefs):
    for p in prefs:
        if dim % p == 0:
            return p
    return dim


def _matmul(a, b, dims, out_dtype, name, res=None, res_scale=1.0):
    if dims == "nn":
        (M, K), (_, N) = a.shape, b.shape
    elif dims == "nt":
        (M, K), (N, _) = a.shape, b.shape
    else:
        (K, M), (_, N) = a.shape, b.shape
    tm = _pick(M, (1024, 1408, 512, 256, 128))
    tn = _pick(N, (640, 512, 256, 128))
    tk = _pick(K, (1024, 1408, 640, 512, 256, 128))
    nk = K // tk
    grid = (M // tm, N // tn, nk)
    if dims == "tn":
        a_spec = pl.BlockSpec((tk, tm), lambda i, j, k: (k, i))
    else:
        a_spec = pl.BlockSpec((tm, tk), lambda i, j, k: (i, k))
    if dims == "nt":
        b_spec = pl.BlockSpec((tn, tk), lambda i, j, k: (j, k))
    else:
        b_spec = pl.BlockSpec((tk, tn), lambda i, j, k: (k, j))
    o_spec = pl.BlockSpec((tm, tn), lambda i, j, k: (i, j))
    in_specs = [a_spec, b_spec]
    args = [a, b]
    if res is not None:
        in_specs.append(o_spec)
        args.append(res)

    def body(*refs):
        if res is not None:
            a_ref, b_ref, r_ref, o_ref, acc_ref = refs
        else:
            a_ref, b_ref, o_ref, acc_ref = refs
            r_ref = None
        k = pl.program_id(2)
        part = _dot(a_ref[...], b_ref[...], dims)

        def finish(total):
            if r_ref is not None:
                total = total + res_scale * r_ref[...]
            o_ref[...] = total.astype(o_ref.dtype)

        if nk == 1:
            finish(part)
        else:
            @pl.when(k == 0)
            def _():
                acc_ref[...] = part

            @pl.when(jnp.logical_and(k > 0, k < nk - 1))
            def _():
                acc_ref[...] += part

            @pl.when(k == nk - 1)
            def _():
                finish(acc_ref[...] + part)

    return pl.pallas_call(
        body, name=name, grid=grid, in_specs=in_specs, out_specs=o_spec,
        out_shape=jax.ShapeDtypeStruct((M, N), out_dtype),
        scratch_shapes=[pltpu.VMEM((tm, tn), F32)],
        compiler_params=_cparams(("parallel", "parallel", "arbitrary")),
    )(*args)


LN_ROWS = 256


def _ln_stats(r):
    mu = jnp.mean(r, axis=-1, keepdims=True)
    xc = r - mu
    var = jnp.mean(xc * xc, axis=-1, keepdims=True)
    return xc * lax.rsqrt(var + LN_EPS)


def _ln_fwd(r, g, b, name):
    S, D = r.shape

    def body(r_ref, g_ref, b_ref, h_ref):
        h_ref[...] = _ln_stats(r_ref[...]) * g_ref[...] + b_ref[...]

    row = pl.BlockSpec((LN_ROWS, D), lambda i: (i, 0))
    vec = pl.BlockSpec((1, D), lambda i: (0, 0))
    return pl.pallas_call(
        body, name=name, grid=(S // LN_ROWS,), in_specs=[row, vec, vec], out_specs=row,
        out_shape=jax.ShapeDtypeStruct((S, D), F32),
        compiler_params=_cparams(("parallel",)),
    )(r, g, b)


def _ln_bwd_core(xhat, dy, g):
    dxh = dy * g
    m1 = jnp.mean(dxh, axis=-1, keepdims=True)
    m2 = jnp.mean(dxh * xhat, axis=-1, keepdims=True)
    return dxh - m1 - xhat * m2


def _ln_bwd(r, dy, g, name):
    S, D = r.shape

    def body(r_ref, dy_ref, g_ref, dr_ref, gg_ref, gb_ref):
        x = r_ref[...]
        mu = jnp.mean(x, axis=-1, keepdims=True)
        xc = x - mu
        rstd = lax.rsqrt(jnp.mean(xc * xc, axis=-1, keepdims=True) + LN_EPS)
        xhat = xc * rstd
        dy = dy_ref[...]
        dr_ref[...] = rstd * _ln_bwd_core(xhat, dy, g_ref[...])

        @pl.when(pl.program_id(0) == 0)
        def _():
            gg_ref[...] = jnp.zeros_like(gg_ref)
            gb_ref[...] = jnp.zeros_like(gb_ref)

        gg_ref[...] += jnp.sum(dy * xhat, axis=0, keepdims=True)
        gb_ref[...] += jnp.sum(dy, axis=0, keepdims=True)

    row = pl.BlockSpec((LN_ROWS, D), lambda i: (i, 0))
    vec = pl.BlockSpec((1, D), lambda i: (0, 0))
    return pl.pallas_call(
        body, name=name, grid=(S // LN_ROWS,), in_specs=[row, row, vec], out_specs=[row, vec, vec],
        out_shape=[jax.ShapeDtypeStruct((S, D), F32), jax.ShapeDtypeStruct((1, D), F32),
                   jax.ShapeDtypeStruct((1, D), F32)],
        compiler_params=_cparams(("arbitrary",)),
    )(r, dy, g)


def _ln_loss_bwd(r, target, g, b, name):
    S, D = r.shape

    def body(r_ref, t_ref, g_ref, b_ref, dr_ref, loss_ref, gg_ref, gb_ref):
        x = r_ref[...]
        mu = jnp.mean(x, axis=-1, keepdims=True)
        xc = x - mu
        rstd = lax.rsqrt(jnp.mean(xc * xc, axis=-1, keepdims=True) + LN_EPS)
        xhat = xc * rstd
        y = xhat * g_ref[...] + b_ref[...]
        err = y - t_ref[...]
        dy = err * (1.0 / D)
        dr_ref[...] = rstd * _ln_bwd_core(xhat, dy, g_ref[...])

        @pl.when(pl.program_id(0) == 0)
        def _():
            loss_ref[...] = jnp.zeros_like(loss_ref)
            gg_ref[...] = jnp.zeros_like(gg_ref)
            gb_ref[...] = jnp.zeros_like(gb_ref)

        per_row = jnp.sum(err * err, axis=-1, keepdims=True) * (0.5 / D)
        loss_ref[...] += jnp.broadcast_to(jnp.sum(per_row, axis=0, keepdims=True), loss_ref.shape)
        gg_ref[...] += jnp.sum(dy * xhat, axis=0, keepdims=True)
        gb_ref[...] += jnp.sum(dy, axis=0, keepdims=True)

    row = pl.BlockSpec((LN_ROWS, D), lambda i: (i, 0))
    vec = pl.BlockSpec((1, D), lambda i: (0, 0))
    lvec = pl.BlockSpec((1, LANE), lambda i: (0, 0))
    return pl.pallas_call(
        body, name=name, grid=(S // LN_ROWS,), in_specs=[row, row, vec, vec],
        out_specs=[row, lvec, vec, vec],
        out_shape=[jax.ShapeDtypeStruct((S, D), F32), jax.ShapeDtypeStruct((1, LANE), F32),
                   jax.ShapeDtypeStruct((1, D), F32), jax.ShapeDtypeStruct((1, D), F32)],
        compiler_params=_cparams(("arbitrary",)),
    )(r, target, g, b)


CONV_COLS = 256
CONV_ROWS = 256
HALO = 8
INV_SQRT2 = 1.0 / math.sqrt(2.0)
INV_SQRT2PI = 1.0 / math.sqrt(2.0 * math.pi)


def _gelu(x):
    return 0.5 * x * (1.0 + lax.erf(x * INV_SQRT2))


def _gelu_grad(x):
    return 0.5 * (1.0 + lax.erf(x * INV_SQRT2)) + x * jnp.exp(-0.5 * x * x) * INV_SQRT2PI


def _conv_rows(ext, w_ref, b_ref, n):
    total = ext.shape[0]
    s1 = pltpu.roll(ext, 1, 0)
    s2 = pltpu.roll(ext, 2, 0)
    u = w_ref[2:3, :] * ext + w_ref[1:2, :] * s1 + w_ref[0:1, :] * s2 + b_ref[...]
    return u[HALO:total], s1[HALO:total], s2[HALO:total]


def _conv_gelu_fwd(u0, conv_w, conv_b, name):
    S, C2 = u0.shape
    F = C2 // 2
    ncb = F // CONV_COLS
    nrc = S // CONV_ROWS

    def body(ua_ref, uc_ref, wa_ref, wc_ref, ba_ref, bc_ref, p_ref):
        def chunk(ci, _):
            r0 = pl.multiple_of(ci * CONV_ROWS, CONV_ROWS)
            p0 = pl.multiple_of(jnp.maximum(r0 - HALO, 0), HALO)
            keep = (ci > 0).astype(F32)

            def load(ref):
                prev = ref[pl.ds(p0, HALO), :] * keep
                return jnp.concatenate([prev, ref[pl.ds(r0, CONV_ROWS), :]], axis=0)

            a, _, _ = _conv_rows(load(ua_ref), wa_ref, ba_ref, CONV_ROWS)
            c, _, _ = _conv_rows(load(uc_ref), wc_ref, bc_ref, CONV_ROWS)
            p_ref[pl.ds(r0, CONV_ROWS), :] = (_gelu(a) * c).astype(p_ref.dtype)
            return 0

        lax.fori_loop(0, nrc, chunk, 0)

    col_a = pl.BlockSpec((S, CONV_COLS), lambda j: (0, j))
    col_c = pl.BlockSpec((S, CONV_COLS), lambda j: (0, j + ncb))
    w_a = pl.BlockSpec((CONV_WIDTH, CONV_COLS), lambda j: (0, j))
    w_c = pl.BlockSpec((CONV_WIDTH, CONV_COLS), lambda j: (0, j + ncb))
    b_a = pl.BlockSpec((1, CONV_COLS), lambda j: (0, j))
    b_c = pl.BlockSpec((1, CONV_COLS), lambda j: (0, j + ncb))
    return pl.pallas_call(
        body, name=name, grid=(ncb,), in_specs=[col_a, col_c, w_a, w_c, b_a, b_c], out_specs=col_a,
        out_shape=jax.ShapeDtypeStruct((S, F), MXU_DTYPE),
        compiler_params=_cparams(("parallel",)),
    )(u0, u0, conv_w, conv_w, conv_b, conv_b)


def _conv_gelu_bwd(u0, dp, conv_w, conv_b, name):
    S, C2 = u0.shape
    F = C2 // 2
    ncb = F // CONV_COLS
    nrc = S // CONV_ROWS
    EXT = CONV_ROWS + HALO

    def body(ua_ref, uc_ref, dp_ref, wa_ref, wc_ref, ba_ref, bc_ref,
             da_ref, dc_ref, gwa_ref, gwc_ref, gba_ref, gbc_ref):
        gwa_ref[...] = jnp.zeros_like(gwa_ref)
        gwc_ref[...] = jnp.zeros_like(gwc_ref)
        gba_ref[...] = jnp.zeros_like(gba_ref)
        gbc_ref[...] = jnp.zeros_like(gbc_ref)
        rid = lax.broadcasted_iota(jnp.int32, (EXT, CONV_COLS), 0)

        def chunk(ci, _):
            r0 = pl.multiple_of(ci * CONV_ROWS, CONV_ROWS)
            p0 = pl.multiple_of(jnp.maximum(r0 - HALO, 0), HALO)
            n0 = pl.multiple_of(jnp.minimum(r0 + CONV_ROWS, S - HALO), HALO)
            keep_prev = (ci > 0).astype(F32)
            keep_next = (ci < nrc - 1).astype(F32)

            def load(ref):
                return jnp.concatenate([ref[pl.ds(p0, HALO), :] * keep_prev,
                                        ref[pl.ds(r0, CONV_ROWS), :],
                                        ref[pl.ds(n0, HALO), :] * keep_next], axis=0)

            ext_a = load(ua_ref)
            ext_c = load(uc_ref)
            a, a1, a2 = _conv_rows(ext_a, wa_ref, ba_ref, EXT)
            c, c1, c2 = _conv_rows(ext_c, wc_ref, bc_ref, EXT)
            a0 = ext_a[HALO:HALO + EXT]
            c0 = ext_c[HALO:HALO + EXT]
            dpe = jnp.concatenate([dp_ref[pl.ds(r0, CONV_ROWS), :].astype(F32),
                                   dp_ref[pl.ds(n0, HALO), :].astype(F32) * keep_next], axis=0)
            d_a = dpe * c * _gelu_grad(a)
            d_c = dpe * _gelu(a)
            own = rid < CONV_ROWS

            def back(d_u, w_ref, x0, x1, x2, d_ref, gw_ref, gb_ref):
                d_u0 = (w_ref[2:3, :] * d_u + w_ref[1:2, :] * pltpu.roll(d_u, EXT - 1, 0)
                        + w_ref[0:1, :] * pltpu.roll(d_u, EXT - 2, 0))
                d_ref[pl.ds(r0, CONV_ROWS), :] = d_u0[0:CONV_ROWS].astype(d_ref.dtype)
                d_own = jnp.where(own, d_u, 0.0)
                gw_ref[...] += jnp.concatenate(
                    [jnp.sum(d_own * x2, axis=0, keepdims=True),
                     jnp.sum(d_own * x1, axis=0, keepdims=True),
                     jnp.sum(d_own * x0, axis=0, keepdims=True)], axis=0)
                gb_ref[...] += jnp.sum(d_own, axis=0, keepdims=True)

            back(d_a, wa_ref, a0, a1, a2, da_ref, gwa_ref, gba_ref)
            back(d_c, wc_ref, c0, c1, c2, dc_ref, gwc_ref, gbc_ref)
            return 0

        lax.fori_loop(0, nrc, chunk, 0)

    col_a = pl.BlockSpec((S, CONV_COLS), lambda j: (0, j))
    col_c = pl.BlockSpec((S, CONV_COLS), lambda j: (0, j + ncb))
    w_a = pl.BlockSpec((CONV_WIDTH, CONV_COLS), lambda j: (0, j))
    w_c = pl.BlockSpec((CONV_WIDTH, CONV_COLS), lambda j: (0, j + ncb))
    b_a = pl.BlockSpec((1, CONV_COLS), lambda j: (0, j))
    b_c = pl.BlockSpec((1, CONV_COLS), lambda j: (0, j + ncb))
    outs = pl.pallas_call(
        body, name=name, grid=(ncb,),
        in_specs=[col_a, col_c, col_a, w_a, w_c, b_a, b_c],
        out_specs=[col_a, col_a, w_a, w_a, b_a, b_a],
        out_shape=[jax.ShapeDtypeStruct((S, F), MXU_DTYPE), jax.ShapeDtypeStruct((S, F), MXU_DTYPE),
                   jax.ShapeDtypeStruct((CONV_WIDTH, F), F32), jax.ShapeDtypeStruct((CONV_WIDTH, F), F32),
                   jax.ShapeDtypeStruct((1, F), F32), jax.ShapeDtypeStruct((1, F), F32)],
        compiler_params=_cparams(("parallel",)),
    )(u0, u0, dp, conv_w, conv_w, conv_b, conv_b)
    da, dc, gwa, gwc, gba, gbc = outs
    return (jnp.concatenate([da, dc], axis=1), jnp.concatenate([gwa, gwc], axis=1),
            jnp.concatenate([gba, gbc], axis=1))


SB_TK = 128
SB_TQ_FWD = 1024
SB_TQ_BWD = 1024
SB_HPS = 2


def _softplus(z):
    return jnp.maximum(z, 0.0) + jnp.log(1.0 + jnp.exp(-jnp.abs(z)))


def _tri_ones(after):
    r = lax.broadcasted_iota(jnp.int32, (SB_TK, 2 * SB_TK), 0)
    c = lax.broadcasted_iota(jnp.int32, (SB_TK, 2 * SB_TK), 1)
    tri = (r > c) if after else (r < c)
    return jnp.where(c >= SB_TK, 1.0, jnp.where(tri, 1.0, 0.0)).astype(BF16)


def _sb_fwd(q, k, v, name, ride=()):
    H, S, Dh = q.shape
    TK = SB_TK
    TQ = min(SB_TQ_FWD, S)
    R = TQ // TK
    HP = SB_HPS

    def body(q_ref, k_ref, v_ref, o_ref, t_ref):
        i = pl.program_id(1)
        row = lax.broadcasted_iota(jnp.int32, (TQ, TK), 0)
        col = lax.broadcasted_iota(jnp.int32, (TQ, TK), 1)
        after_ones = _tri_ones(True)

        def block(j, carry, r0):
            k0 = pl.multiple_of(j * TK, TK)
            masked = r0 is not None
            r0 = r0 or 0
            out = []
            for hh in range(HP):
                acc, tail = carry[hh]
                kb = k_ref[hh, pl.ds(k0, TK), :]
                vb = v_ref[hh, pl.ds(k0, TK), :]
                z = _dot(q_ref[hh, r0:, :], kb, "nt")
                sp = _softplus(z)
                if masked:
                    strict = col[r0:] < row[r0:] - r0
                    sp = jnp.where(strict, sp, 0.0)
                cs = _dot_split(sp, after_ones, "nn")
                w = jnp.exp(z - sp - cs[:, :TK] - tail[r0:])
                if masked:
                    w = jnp.where(strict, w, 0.0)
                pv = _dot(w, vb, "nn")
                if r0:
                    pv = jnp.concatenate([jnp.zeros((r0, Dh), F32), pv], axis=0)
                    tot = jnp.concatenate([jnp.zeros((r0, TK), F32), cs[:, TK:]], axis=0)
                else:
                    tot = cs[:, TK:]
                out.append((acc + pv, tail + tot))
            return tuple(out)

        carry = tuple((jnp.zeros((TQ, Dh), F32), jnp.zeros((TQ, TK), F32)) for _ in range(HP))
        for u in reversed(range(R)):
            carry = block(R * i + u, carry, u * TK)
        carry = lax.fori_loop(0, R * i, lambda n, c: block(R * i - 1 - n, c, None), carry)
        for hh in range(HP):
            o_ref[hh] = carry[hh][0]
            t_ref[hh] = carry[hh][1]

    qspec = pl.BlockSpec((HP, TQ, Dh), lambda h, i: (h, i, 0))
    full = pl.BlockSpec((HP, S, Dh), lambda h, i: (h, 0, 0))
    tspec = pl.BlockSpec((HP, TQ, TK), lambda h, i: (h, i, 0))
    x_args, x_in, x_out, x_shapes, x_sems, wrap = _riding_exchange(list(ride), (H // HP, S // TQ))
    return pl.pallas_call(
        wrap(body, 3, 2) if ride else body, name=name, grid=(H // HP, S // TQ),
        in_specs=[qspec, full, full] + (x_in if ride else []),
        out_specs=[qspec, tspec] + (x_out if ride else []),
        out_shape=[jax.ShapeDtypeStruct((H, S, Dh), F32), jax.ShapeDtypeStruct((H, S, TK), F32)]
        + (x_shapes if ride else []),
        scratch_shapes=x_sems if ride else [],
        compiler_params=_cparams(("arbitrary", "arbitrary"), has_side_effects=bool(ride)),
    )(q, k, v, *(x_args if ride else []))


def _sb_bwd(qdo, kz, zv, total, name, ride=()):
    H, S, D2 = qdo.shape
    Dh = D2 // 2
    TK = SB_TK
    TQ = min(SB_TQ_BWD, S)
    R = TQ // TK
    HP = SB_HPS
    scale = Dh ** -0.5

    def body(qdo_ref, kz_ref, zv_ref, t_ref, dq_ref, dk_ref, dv_ref):
        i = pl.program_id(1)

        @pl.when(i == 0)
        def _():
            dk_ref[...] = jnp.zeros_like(dk_ref)
            dv_ref[...] = jnp.zeros_like(dv_ref)

        row = lax.broadcasted_iota(jnp.int32, (TQ, TK), 0)
        col = lax.broadcasted_iota(jnp.int32, (TQ, TK), 1)
        after_ones = _tri_ones(True)
        before_ones = _tri_ones(False)

        def block(j, carry, r0):
            k0 = pl.multiple_of(j * TK, TK)
            masked = r0 is not None
            r0 = r0 or 0
            out = []
            for hh in range(HP):
                dq, seen, gsum = carry[hh]
                qd = qdo_ref[hh, r0:, :]
                kzb = kz_ref[hh, pl.ds(k0, TK), :]
                zvb = zv_ref[hh, pl.ds(k0, TK), :]
                zdw = _dot(qd, jnp.concatenate([kzb, zvb], axis=0), "nt")
                z = zdw[:, :TK]
                sp = _softplus(z)
                logsig = z - sp
                if masked:
                    strict = col[r0:] < row[r0:] - r0
                    sp = jnp.where(strict, sp, 0.0)
                cs = _dot_split(sp, after_ones, "nn")
                seen_r = seen[r0:] + cs[:, TK:]
                w = jnp.exp(logsig - cs[:, :TK] - (t_ref[hh, r0:, :] - seen_r))
                if masked:
                    w = jnp.where(strict, w, 0.0)
                g = w * zdw[:, TK:]
                cg = _dot_split(g, before_ones, "nn")
                dz = g - jnp.exp(logsig) * (g + cg[:, :TK] + gsum[r0:])
                if masked:
                    dz = jnp.where(strict, dz, 0.0)
                dzb = dz.astype(MXU_DTYPE)
                dk_ref[hh, pl.ds(k0, TK), :] += _dot(dzb, qd, "tn")
                dv_ref[hh, pl.ds(k0, TK), :] += _dot(w, qd, "tn")
                dq_r = dq[r0:] + _dot(dzb, kzb, "nn")
                gsum_r = gsum[r0:] + cg[:, TK:]
                if r0:
                    dq_r = jnp.concatenate([dq[:r0], dq_r], axis=0)
                    seen_r = jnp.concatenate([seen[:r0], seen_r], axis=0)
                    gsum_r = jnp.concatenate([gsum[:r0], gsum_r], axis=0)
                out.append((dq_r, seen_r, gsum_r))
            return tuple(out)

        zero = jnp.zeros((TQ, TK), F32)
        carry = tuple((jnp.zeros((TQ, D2), F32), zero, zero) for _ in range(HP))
        carry = lax.fori_loop(0, R * i, lambda j, c: block(j, c, None), carry)
        for u in range(R):
            carry = block(R * i + u, carry, u * TK)
        for hh in range(HP):
            dq_ref[hh] = carry[hh][0] * scale

    qspec = pl.BlockSpec((HP, TQ, D2), lambda h, i: (h, i, 0))
    full = pl.BlockSpec((HP, S, D2), lambda h, i: (h, 0, 0))
    tspec = pl.BlockSpec((HP, TQ, TK), lambda h, i: (h, i, 0))
    sds = jax.ShapeDtypeStruct((H, S, D2), F32)
    x_args, x_in, x_out, x_shapes, x_sems, wrap = _riding_exchange(list(ride), (H // HP, S // TQ))
    dq, dk, dv, *rode = pl.pallas_call(
        wrap(body, 4, 3) if ride else body, name=name, grid=(H // HP, S // TQ),
        in_specs=[qspec, full, full, tspec] + (x_in if ride else []),
        out_specs=[qspec, full, full] + (x_out if ride else []),
        out_shape=[sds, sds, sds] + (x_shapes if ride else []),
        scratch_shapes=x_sems if ride else [],
        compiler_params=_cparams(("arbitrary", "arbitrary"), has_side_effects=bool(ride)),
    )(qdo, kz, zv, total, *(x_args if ride else []))
    return (dq[:, :, :Dh], dk[:, :, :Dh], dv[:, :, Dh:], *rode)


GLA_GROUP_FWD = 8
GLA_GROUP_BWD = 4


def _gla_masks():
    C = CHUNK
    row = lax.broadcasted_iota(jnp.int32, (C, C), 0)
    col = lax.broadcasted_iota(jnp.int32, (C, C), 1)
    return row, col


def _log_sigmoid(x):
    return -_softplus(-x)


def _gla_chunk_fwd(qc, kc, vc, gate, row, col):
    C = CHUNK
    la = _log_sigmoid(gate) * (1.0 / GLA_TAU)
    incl = jnp.where(row >= col, 1.0, 0.0).astype(BF16)
    b = _dot_split_lhs01(incl, la)
    b_ref = jnp.sum(jnp.where(row == C // 2 - 1, b, 0.0), axis=0, keepdims=True)
    b_last = jnp.sum(la, axis=0, keepdims=True)
    qs = qc * (GLA_KEY_DIM ** -0.5)
    q_in = qs * jnp.exp(b - b_ref)
    k_in = kc * jnp.exp(b_ref - b)
    k_dec = kc * jnp.exp(b_last - b)
    q_b = qs * jnp.exp(b)
    sc = jnp.where(row >= col, _dot(q_in, k_in, "nt"), 0.0)
    o_intra = _dot(sc, vc, "nn")
    upd = _dot(k_dec, vc, "tn")
    ones = jnp.ones((C, GLA_VAL_DIM), BF16)
    dec_col = jnp.exp(_dot_split_tn(la, ones))
    return dict(la=la, b=b, b_ref=b_ref, b_last=b_last, qs=qs, q_in=q_in, k_in=k_in, k_dec=k_dec,
                q_b=q_b, sc=sc, o_intra=o_intra, upd=upd, dec_col=dec_col)


def _dot_split_lhs01(m01, x):
    hi = x.astype(BF16)
    lo = (x - hi.astype(F32)).astype(BF16)
    dn = (((1,), (0,)), ((), ()))
    return (lax.dot_general(m01, hi, dn, preferred_element_type=F32)
            + lax.dot_general(m01, lo, dn, preferred_element_type=F32))


def _dot_split_tn(x, m01):
    hi = x.astype(BF16)
    lo = (x - hi.astype(F32)).astype(BF16)
    dn = (((0,), (0,)), ((), ()))
    return (lax.dot_general(hi, m01, dn, preferred_element_type=F32)
            + lax.dot_general(lo, m01, dn, preferred_element_type=F32))


def _dot_split_nt01(m01, x):
    hi = x.astype(BF16)
    lo = (x - hi.astype(F32)).astype(BF16)
    dn = (((1,), (1,)), ((), ()))
    return (lax.dot_general(m01, hi, dn, preferred_element_type=F32)
            + lax.dot_general(m01, lo, dn, preferred_element_type=F32))


def _rms_gate(o, gg, gnorm):
    rinv = lax.rsqrt(jnp.mean(o * o, axis=-1, keepdims=True) + RMS_EPS)
    o_n = o * rinv
    sg = 1.0 / (1.0 + jnp.exp(-gg))
    return o_n, rinv, sg


def _gla_fwd(gq, gk, proj, gate_up_h, gate_bias_h, gnorm, name):
    Hg, S, dk = gq.shape
    dv = GLA_VAL_DIM
    C = CHUNK
    G = GLA_GROUP_FWD
    nchunk = S // C
    ngroup = nchunk // G

    def body(q_ref, k_ref, v_ref, gg_ref, ga_ref, gu_ref, gb_ref, gn_ref, o_ref, prev_ref):
        row, col = _gla_masks()
        gu = gu_ref[0]
        gbias = gb_ref[0]
        gnorm_v = gn_ref[...]

        def group(gi, state):
            for u in range(G):
                ci = gi * G + u
                r0 = pl.multiple_of(ci * C, C)
                rows = pl.ds(r0, C)
                gate = _dot(ga_ref[rows, :], gu, "nn") + gbias
                f = _gla_chunk_fwd(q_ref[0, rows, :], k_ref[0, rows, :], v_ref[rows, :], gate, row, col)
                prev_ref[0, ci] = state
                o = f["o_intra"] + _dot(f["q_b"], state, "nn")
                state = f["dec_col"] * state + f["upd"]
                o_n, _, sg = _rms_gate(o, gg_ref[rows, :], gnorm_v)
                o_ref[rows, :] = o_n * gnorm_v * (gg_ref[rows, :] * sg)
            return state

        lax.fori_loop(0, ngroup, group, jnp.zeros((dk, dv), F32))

    hspec = pl.BlockSpec((1, S, dk), lambda h: (h, 0, 0))
    vspec = pl.BlockSpec((S, dv), lambda h: (0, h))
    return pl.pallas_call(
        body, name=name, grid=(Hg,),
        in_specs=[hspec, hspec,
                  pl.BlockSpec((S, dv), lambda h: (0, h + OFF_GV // LANE)),
                  pl.BlockSpec((S, dv), lambda h: (0, h + OFF_GG // LANE)),
                  pl.BlockSpec((S, LANE), lambda h: (0, OFF_GA // LANE)),
                  pl.BlockSpec((1, LANE, dk), lambda h: (h, 0, 0)),
                  pl.BlockSpec((1, 1, dk), lambda h: (h, 0, 0)),
                  pl.BlockSpec((1, dv), lambda h: (0, 0))],
        out_specs=[vspec, pl.BlockSpec((1, nchunk, dk, dv), lambda h: (h, 0, 0, 0))],
        out_shape=[jax.ShapeDtypeStruct((S, Hg * dv), F32),
                   jax.ShapeDtypeStruct((Hg, nchunk, dk, dv), F32)],
        compiler_params=_cparams(("parallel",)),
    )(gq, gk, proj, proj, proj, gate_up_h, gate_bias_h, gnorm)


def _gla_bwd(gq, gk, proj, gate_up_h, gate_bias_h, gnorm, prev, d_cat, name):
    Hg, S, dk = gq.shape
    dv = GLA_VAL_DIM
    C = CHUNK
    G = GLA_GROUP_BWD
    nchunk = S // C
    ngroup = nchunk // G

    def body(q_ref, k_ref, v_ref, gg_ref, ga_ref, gu_ref, gb_ref, gn_ref, prev_ref, do_ref,
             dq_ref, dk_ref, dv_ref, dgg_ref, dga_ref, ggu_ref, ggb_ref, ggn_ref):
        h = pl.program_id(0)
        row, col = _gla_masks()
        gu = gu_ref[0]
        gbias = gb_ref[0]
        gnorm_v = gn_ref[...]
        ggu_ref[...] = jnp.zeros_like(ggu_ref)
        ggb_ref[...] = jnp.zeros_like(ggb_ref)

        @pl.when(h == 0)
        def _():
            dga_ref[...] = jnp.zeros_like(dga_ref)
            ggn_ref[...] = jnp.zeros_like(ggn_ref)

        upper_incl = jnp.where(col >= row, 1.0, 0.0).astype(BF16)
        ones_8 = jnp.ones((8, dv), BF16)

        def group(gn, dstate):
            gi = ngroup - 1 - gn
            for u in reversed(range(G)):
                ci = gi * G + u
                r0 = pl.multiple_of(ci * C, C)
                rows = pl.ds(r0, C)
                ga = ga_ref[rows, :]
                gate = _dot(ga, gu, "nn") + gbias
                qc, kc, vc = q_ref[0, rows, :], k_ref[0, rows, :], v_ref[rows, :]
                f = _gla_chunk_fwd(qc, kc, vc, gate, row, col)
                state = prev_ref[0, ci]
                o = f["o_intra"] + _dot(f["q_b"], state, "nn")
                ggv = gg_ref[rows, :]
                o_n, rinv, sg = _rms_gate(o, ggv, gnorm_v)
                dout = do_ref[rows, :]
                silu = ggv * sg
                dgg_ref[rows, :] = dout * o_n * gnorm_v * (sg * (1.0 + ggv * (1.0 - sg)))
                d_ong = dout * silu
                ggn_ref[...] += jnp.sum(d_ong * o_n, axis=0, keepdims=True)
                d_on = d_ong * gnorm_v
                d_o = rinv * (d_on - o_n * jnp.mean(d_on * o_n, axis=-1, keepdims=True))
                d_upd = dstate
                d_dec_col = dstate * state * f["dec_col"]
                dstate = f["dec_col"] * dstate + _dot(f["q_b"], d_o, "tn")
                dsc = jnp.where(row >= col, _dot(d_o, vc, "nt"), 0.0)
                dv_ref[rows, :] = _dot(f["sc"], d_o, "tn") + _dot(f["k_dec"], d_upd, "nn")
                dq_in = _dot(dsc, f["k_in"], "nn")
                dk_in = _dot(dsc, f["q_in"], "tn")
                dq_b = _dot(d_o, state, "nt")
                dkdec = _dot(vc, d_upd, "nt")
                b = f["b"]
                e1 = jnp.exp(b - f["b_ref"])
                e2 = jnp.exp(f["b_ref"] - b)
                e3 = jnp.exp(f["b_last"] - b)
                eb = jnp.exp(b)
                dq_ref[0, rows, :] = (dq_in * e1 + dq_b * eb) * (GLA_KEY_DIM ** -0.5)
                dk_ref[0, rows, :] = dk_in * e2 + dkdec * e3
                t_q = dq_in * f["q_in"]
                t_k = dk_in * f["k_in"]
                t_d = dkdec * f["k_dec"]
                db = t_q - t_k - t_d + dq_b * f["q_b"]
                db_ref = jnp.sum(t_k - t_q, axis=0, keepdims=True)
                db_last = (jnp.sum(t_d, axis=0, keepdims=True)
                           + jnp.max(_dot_split_nt01(ones_8, d_dec_col), axis=0, keepdims=True))
                db = db + jnp.where(row == C // 2 - 1, db_ref, 0.0) + jnp.where(row == C - 1, db_last, 0.0)
                dla = _dot_split_lhs01(upper_incl, db)
                d_gate = dla * (1.0 / GLA_TAU) * (1.0 / (1.0 + jnp.exp(gate)))
                ggb_ref[0] += jnp.sum(d_gate, axis=0, keepdims=True)
                ggu_ref[0] += _dot(ga, d_gate, "tn")
                dga_ref[rows, :] += _dot(d_gate, gu, "nt")
            return dstate

        lax.fori_loop(0, ngroup, group, jnp.zeros((dk, dv), F32))

    hspec = pl.BlockSpec((1, S, dk), lambda h: (h, 0, 0))
    vspec = pl.BlockSpec((S, dv), lambda h: (0, h))
    gaspec = pl.BlockSpec((S, LANE), lambda h: (0, 0))
    guspec = pl.BlockSpec((1, LANE, dk), lambda h: (h, 0, 0))
    gbspec = pl.BlockSpec((1, 1, dk), lambda h: (h, 0, 0))
    gnspec = pl.BlockSpec((1, dv), lambda h: (0, 0))
    return pl.pallas_call(
        body, name=name, grid=(Hg,),
        in_specs=[hspec, hspec,
                  pl.BlockSpec((S, dv), lambda h: (0, h + OFF_GV // LANE)),
                  pl.BlockSpec((S, dv), lambda h: (0, h + OFF_GG // LANE)),
                  pl.BlockSpec((S, LANE), lambda h: (0, OFF_GA // LANE)), guspec, gbspec,
                  pl.BlockSpec((1, dv), lambda h: (0, 0)),
                  pl.BlockSpec((1, nchunk, dk, dv), lambda h: (h, 0, 0, 0)),
                  pl.BlockSpec((S, dv), lambda h: (0, h + SB_WIDTH // LANE))],
        out_specs=[hspec, hspec, vspec, vspec, gaspec, guspec, gbspec, gnspec],
        out_shape=[jax.ShapeDtypeStruct((Hg, S, dk), F32), jax.ShapeDtypeStruct((Hg, S, dk), F32),
                   jax.ShapeDtypeStruct((S, Hg * dv), F32), jax.ShapeDtypeStruct((S, Hg * dv), F32),
                   jax.ShapeDtypeStruct((S, LANE), F32), jax.ShapeDtypeStruct((Hg, LANE, dk), F32),
                   jax.ShapeDtypeStruct((Hg, 1, dk), F32), jax.ShapeDtypeStruct((1, dv), F32)],
        compiler_params=_cparams(("arbitrary",)),
    )(gq, gk, proj, proj, proj, gate_up_h, gate_bias_h, gnorm, prev, d_cat)


def _exchange_copies(scatter_flags, ins, outs, send_sems, recv_sems, local_sems):
    n_peer = N_DEV - 1
    x, y, c = lax.axis_index("x"), lax.axis_index("y"), lax.axis_index("c")
    me = 4 * x + 2 * y + c
    copies = []
    for a, scatter in enumerate(scatter_flags):
        own = ins[a].at[me] if scatter else ins[a]
        copies.append(pltpu.make_async_copy(own, outs[a].at[me], local_sems.at[a]))
    for r in range(1, N_DEV):
        px = 1 - x if r & 4 else x
        py = 1 - y if r & 2 else y
        pc = 1 - c if r & 1 else c
        for a, scatter in enumerate(scatter_flags):
            src = ins[a].at[4 * px + 2 * py + pc] if scatter else ins[a]
            copies.append(pltpu.make_async_remote_copy(
                src_ref=src, dst_ref=outs[a].at[me],
                send_sem=send_sems.at[a * n_peer + r - 1], recv_sem=recv_sems.at[a * n_peer + r - 1],
                device_id=(px, py, pc), device_id_type=MESH_ID))
    return copies


def _exchange_shapes(items):
    out_shape = []
    for arr, scatter in items:
        shp = arr.shape if scatter else (N_DEV,) + arr.shape
        out_shape.append(jax.ShapeDtypeStruct(shp, arr.dtype))
    n = len(items)
    sems = [pltpu.SemaphoreType.DMA((n * (N_DEV - 1),)), pltpu.SemaphoreType.DMA((n * (N_DEV - 1),)),
            pltpu.SemaphoreType.DMA((n,))]
    return out_shape, sems


def _exchange(items, name):
    n = len(items)
    flags = [sc for _, sc in items]

    def body(*refs):
        copies = _exchange_copies(flags, refs[:n], refs[n:2 * n], *refs[2 * n:])
        for cp in copies:
            cp.start()
        for cp in copies:
            cp.wait()

    out_shape, sems = _exchange_shapes(items)
    any_spec = pl.BlockSpec(memory_space=pl.ANY)
    return pl.pallas_call(
        body, name=name, in_specs=[any_spec] * n, out_specs=[any_spec] * n, out_shape=out_shape,
        scratch_shapes=sems, compiler_params=pltpu.CompilerParams(has_side_effects=True),
    )(*[arr for arr, _ in items])


def _riding_exchange(items, grid):
    n = len(items)
    flags = [sc for _, sc in items]
    out_shape, sems = _exchange_shapes(items)
    any_spec = pl.BlockSpec(memory_space=pl.ANY)

    def wrap(body, n_in, n_out):
        def fused(*refs):
            ins = refs[:n_in]
            x_ins = refs[n_in:n_in + n]
            outs = refs[n_in + n:n_in + n + n_out]
            x_outs = refs[n_in + n + n_out:n_in + 2 * n + n_out]
            rest = refs[n_in + 2 * n + n_out:]
            x_sems, scratch = rest[len(rest) - 3:], rest[:len(rest) - 3]
            first = jnp.logical_and(pl.program_id(0) == 0, pl.program_id(1) == 0)
            last = jnp.logical_and(pl.program_id(0) == grid[0] - 1, pl.program_id(1) == grid[1] - 1)

            @pl.when(first)
            def _():
                for cp in _exchange_copies(flags, x_ins, x_outs, *x_sems):
                    cp.start()

            body(*ins, *outs, *scratch)

            @pl.when(last)
            def _():
                for cp in _exchange_copies(flags, x_ins, x_outs, *x_sems):
                    cp.wait()

        return fused

    return [arr for arr, _ in items], [any_spec] * n, [any_spec] * n, out_shape, sems, wrap


def _adamw(grecv, w, m, v, name):
    R, C = w.shape
    tile = _pick(R, (256, 176, 128)) if R * C > 65536 else R
    bc1 = 1.0 - ADAM_B1 ** ADAM_STEP
    bc2 = 1.0 - ADAM_B2 ** ADAM_STEP

    def body(gr_ref, w_ref, m_ref, v_ref, g_ref, d_ref, nm_ref, nv_ref):
        g = gr_ref[0].astype(F32)
        for q in range(1, N_DEV):
            g = g + gr_ref[q].astype(F32)
        nm = ADAM_B1 * m_ref[...] + (1.0 - ADAM_B1) * g
        nv = ADAM_B2 * v_ref[...] + (1.0 - ADAM_B2) * (g * g)
        m_hat = nm / bc1
        v_hat = nv / bc2
        g_ref[...] = g
        d_ref[...] = -ADAM_LR * (m_hat / (jnp.sqrt(v_hat) + ADAM_EPS) + ADAM_WD * w_ref[...])
        nm_ref[...] = nm
        nv_ref[...] = nv

    blk = pl.BlockSpec((tile, C), lambda i: (i, 0))
    sds = jax.ShapeDtypeStruct((R, C), F32)
    return pl.pallas_call(
        body, name=name, grid=(R // tile,),
        in_specs=[pl.BlockSpec((N_DEV, tile, C), lambda i: (0, i, 0)), blk, blk, blk],
        out_specs=[blk, blk, blk, blk], out_shape=[sds, sds, sds, sds],
        compiler_params=_cparams(("parallel",)),
    )(grecv, w, m, v)


def _pack_rows(parts, rows, dtype):
    flat = jnp.concatenate([p.reshape(-1).astype(dtype) for p in parts])
    return jnp.pad(flat, (0, rows * LANE - flat.shape[0])).reshape(rows, LANE)


def _unpack_rows(buf, shapes):
    flat = buf.reshape(-1)
    out, off = [], 0
    for shp in shapes:
        n = math.prod(shp)
        out.append(flat[off:off + n].reshape(shp))
        off += n
    return out


def _shard_cols(g):
    rows, cols = g.shape
    return g.reshape(rows, N_DEV, cols // N_DEV).transpose(1, 0, 2)


def _unshard_cols(blocks):
    return blocks.transpose(1, 0, 2).reshape(blocks.shape[1], -1)


def _heads(t, n, d):
    return t.reshape(t.shape[0], n, d).transpose(1, 0, 2)


def _unheads(t):
    return t.transpose(1, 0, 2).reshape(t.shape[1], -1)


def kernel(x, w_in, gate_up, gate_bias, gla_norm_g, w_out, ln1_g, ln1_b, w_up, conv_w, conv_b, w_down, ln2_g, ln2_b, loss_target, m_w_in, m_gate_up, m_gate_bias, m_gla_norm_g, m_w_out, m_ln1_g, m_ln1_b, m_w_up, m_conv_w, m_conv_b, m_w_down, m_ln2_g, m_ln2_b, v_w_in, v_gate_up, v_gate_bias, v_gla_norm_g, v_w_out, v_ln1_g, v_ln1_b, v_w_up, v_conv_w, v_conv_b, v_w_down, v_ln2_g, v_ln2_b):
    S, D = x.shape[1], x.shape[2]
    x2, tgt = x[0], loss_target[0]

    gathered = _exchange([(w_in[0].astype(MXU_DTYPE), False), (gate_up[0], False), (conv_w[0], False)],
                         "gather_w_in")
    w_in_f = _unshard_cols(gathered[0])
    gate_up_f = _unshard_cols(gathered[1])
    conv_w_f = _unshard_cols(gathered[2])
    w_in_pad = jnp.pad(w_in_f, ((0, 0), (0, IN_PAD - IN_WIDTH)))
    gate_up_h = _heads(jnp.pad(gate_up_f, ((0, LANE - GLA_GATE_RANK), (0, 0))), GLA_HEADS, GLA_KEY_DIM)
    gate_bias_h = gate_bias.reshape(GLA_HEADS, 1, GLA_KEY_DIM)

    proj = _matmul(x2, w_in_pad, "nn", F32, "proj")
    sq = _heads(proj[:, OFF_SBQ:OFF_SBK] * (SB_HEAD_DIM ** -0.5), SB_HEADS, SB_HEAD_DIM).astype(MXU_DTYPE)
    sk = _heads(proj[:, OFF_SBK:OFF_SBV], SB_HEADS, SB_HEAD_DIM).astype(MXU_DTYPE)
    sv = _heads(proj[:, OFF_SBV:OFF_GQ], SB_HEADS, SB_HEAD_DIM).astype(MXU_DTYPE)
    sb_o_h, sb_tot, g_out, g_up, g_down = _sb_fwd(
        sq, sk, sv, "sb_fwd", ride=[(w_out[0].astype(MXU_DTYPE), False), (w_up[0].astype(MXU_DTYPE), False),
                                    (w_down[0].astype(MXU_DTYPE), False)])
    w_out_f = g_out.reshape(-1, D)
    w_up_f = _unshard_cols(g_up)
    w_down_f = g_down.reshape(-1, D)
    gq = _heads(proj[:, OFF_GQ:OFF_GK], GLA_HEADS, GLA_KEY_DIM)
    gk = _heads(proj[:, OFF_GK:OFF_GV], GLA_HEADS, GLA_KEY_DIM)
    gla_o, prev = _gla_fwd(gq, gk, proj, gate_up_h, gate_bias_h, gla_norm_g, "gla_fwd")
    cat = jnp.concatenate([_unheads(sb_o_h), gla_o], axis=1)
    r1 = _matmul(cat, w_out_f, "nn", F32, "mix", res=x2, res_scale=DN_ALPHA)
    h = _ln_fwd(r1, ln1_g, ln1_b, "ln1")
    u0 = _matmul(h, w_up_f, "nn", F32, "ffn_up")
    p = _conv_gelu_fwd(u0, conv_w_f, conv_b, "conv_gelu")
    r2 = _matmul(p, w_down_f, "nn", F32, "ffn_down", res=h, res_scale=DN_ALPHA)
    d_r2, loss_p, g_ln2_g, g_ln2_b = _ln_loss_bwd(r2, tgt, ln2_g, ln2_b, "ln2_loss")

    d_p = _matmul(d_r2, w_down_f, "nt", MXU_DTYPE, "d_ffn_act")
    g_w_down = _matmul(p, d_r2, "tn", F32, "grad_w_down")
    d_u0, g_conv_w, g_conv_b = _conv_gelu_bwd(u0, d_p, conv_w_f, conv_b, "conv_gelu_bwd")
    g_w_up = _matmul(h, d_u0, "tn", F32, "grad_w_up")
    d_h = _matmul(d_u0, w_up_f, "nt", F32, "d_h", res=d_r2, res_scale=DN_ALPHA)
    d_r1, g_ln1_g, g_ln1_b = _ln_bwd(r1, d_h, ln1_g, "ln1_bwd")
    g_w_out = _matmul(cat, d_r1, "tn", F32, "grad_w_out")
    d_cat = _matmul(d_r1, w_out_f, "nt", F32, "d_cat")
    (d_gq, d_gk, d_gv, d_gg, d_ga_pad, g_gu_h, g_gb_h, g_gnorm) = _gla_bwd(
        gq, gk, proj, gate_up_h, gate_bias_h, gla_norm_g, prev, d_cat, "gla_bwd")
    d_sb_o = _heads(d_cat[:, :SB_WIDTH], SB_HEADS, SB_HEAD_DIM).astype(MXU_DTYPE)
    g_gate_up = _unheads(g_gu_h[:, :GLA_GATE_RANK, :])
    g_gate_bias = g_gb_h.reshape(1, -1)
    small_g = [g_gate_bias, g_gnorm, g_ln1_g, g_ln1_b, g_conv_b, g_ln2_g, g_ln2_b, loss_p]
    n_small = sum(t.size for t in small_g)
    rows_small = -(-n_small // (8 * LANE)) * 8
    d_sq, d_sk, d_sv, *recv_rest = _sb_bwd(
        jnp.concatenate([sq, d_sb_o], axis=2), jnp.concatenate([sk, jnp.zeros_like(sk)], axis=2),
        jnp.concatenate([jnp.zeros_like(sv), sv], axis=2), sb_tot, "sb_bwd",
        ride=[(g_w_out.reshape(N_DEV, -1, D).astype(BF16), True), (_shard_cols(g_w_up).astype(BF16), True),
              (g_w_down.reshape(N_DEV, -1, D).astype(BF16), True), (_shard_cols(g_gate_up), True),
              (_shard_cols(g_conv_w), True), (_pack_rows(small_g, rows_small, F32), False)])
    d_proj = jnp.concatenate([_unheads(d_sq), _unheads(d_sk), _unheads(d_sv), _unheads(d_gq), _unheads(d_gk),
                              d_gv, d_gg, d_ga_pad], axis=1)
    g_w_in = _matmul(x2, d_proj, "tn", F32, "grad_w_in")[:, :IN_WIDTH]
    d_x = _matmul(d_proj, w_in_pad, "nt", F32, "d_x", res=d_r1, res_scale=DN_ALPHA)

    recv_in = _exchange([(_shard_cols(g_w_in).astype(BF16), True)], "exchange_grad_w_in")
    recv = [recv_in[0], recv_rest[0], recv_rest[1], recv_rest[2], recv_rest[3], recv_rest[4], recv_rest[5]]
    sharded = [(w_in, m_w_in, v_w_in), (w_out, m_w_out, v_w_out), (w_up, m_w_up, v_w_up),
               (w_down, m_w_down, v_w_down), (gate_up, m_gate_up, v_gate_up), (conv_w, m_conv_w, v_conv_w)]
    upd = [_adamw(recv[n], w[0], m[0], v[0], "adamw_%d" % n) for n, (w, m, v) in enumerate(sharded)]
    zl = jnp.zeros((1, LANE), F32)

    def pks(parts):
        return _pack_rows(parts + [zl], rows_small, F32)

    small = _adamw(recv[6], pks([gate_bias, gla_norm_g, ln1_g, ln1_b, conv_b, ln2_g, ln2_b]),
                   pks([m_gate_bias, m_gla_norm_g, m_ln1_g, m_ln1_b, m_conv_b, m_ln2_g, m_ln2_b]),
                   pks([v_gate_bias, v_gla_norm_g, v_ln1_g, v_ln1_b, v_conv_b, v_ln2_g, v_ln2_b]),
                   "adamw_replicated")
    small_shapes = [gate_bias.shape, gla_norm_g.shape, ln1_g.shape, ln1_b.shape, conv_b.shape, ln2_g.shape,
                    ln2_b.shape, (1, LANE)]
    outs = []
    loss = None
    for kind in range(4):
        b_w_in, b_w_out, b_w_up, b_w_down, b_gate_up, b_conv_w = [u[kind][None] for u in upd]
        s_gb, s_gn, s_l1g, s_l1b, s_cb, s_l2g, s_l2b, s_loss = _unpack_rows(small[kind], small_shapes)
        if kind == 0:
            loss = s_loss[0, 0]
        outs += [b_w_in, b_gate_up, s_gb, s_gn, b_w_out, s_l1g, s_l1b, b_w_up, b_conv_w, s_cb, b_w_down,
                 s_l2g, s_l2b]
    return (loss, d_x[None], *outs)
```

```python
import math

import jax
import jax.numpy as jnp
from jax import lax
from jax.experimental import pallas as pl
from jax.experimental.pallas import tpu as pltpu

F32 = jnp.float32
BF16 = jnp.bfloat16
MXU_DTYPE = jnp.bfloat16

N_DEV = 8
D_MODEL = 1024
SB_WIDTH = 512
SB_HEADS = 8
SB_HEAD_DIM = 64
GLA_HEADS = 4
GLA_KEY_DIM = 64
GLA_VAL_DIM = 128
GLA_WIDTH = 512
GLA_GATE_RANK = 16
GLA_TAU = 16.0
CHUNK = 64
D_FF = 2816
CONV_WIDTH = 3
LN_EPS = 1e-5
RMS_EPS = 1e-6
DN_ALPHA = 2.0 ** 0.25
IN_WIDTH = 3088
LANE = 128
IN_PAD = 3200
OFF_SBQ, OFF_SBK, OFF_SBV = 0, 512, 1024
OFF_GQ, OFF_GK, OFF_GV, OFF_GG, OFF_GA = 1536, 1792, 2048, 2560, 3072
GLA_PAD = IN_PAD - OFF_GQ

ADAM_LR = 0.001
ADAM_B1 = 0.9
ADAM_B2 = 0.999
ADAM_EPS = 1e-08
ADAM_WD = 0.01
ADAM_STEP = 10

VMEM_LIMIT = 48 * 1024 * 1024
MESH_ID = pl.DeviceIdType.MESH


def _cparams(sem=None, **kw):
    return pltpu.CompilerParams(dimension_semantics=sem, vmem_limit_bytes=VMEM_LIMIT, **kw)


def _dot(a, b, dims):
    ca, cb = {"nn": (1, 0), "nt": (1, 1), "tn": (0, 0)}[dims]
    return lax.dot_general(a.astype(MXU_DTYPE), b.astype(MXU_DTYPE), (((ca,), (cb,)), ((), ())),
                           preferred_element_type=F32)


def _dot_split(a, b, dims):
    assert dims == "nn"
    hi = a.astype(BF16)
    lo = (a - hi.astype(F32)).astype(BF16)
    return lax.dot_general(jnp.concatenate([hi, lo], axis=1), jnp.concatenate([b, b], axis=0),
                           (((1,), (0,)), ((), ())), preferred_element_type=F32)


def _pick(dim, prefs):
    for p in prefs:
        if dim % p == 0:
            return p
    return dim


def _matmul(a, b, dims, out_dtype, name, res=None, res_scale=1.0, ride=()):
    if dims == "nn":
        (M, K), (_, N) = a.shape, b.shape
    elif dims == "nt":
        (M, K), (N, _) = a.shape, b.shape
    else:
        (K, M), (_, N) = a.shape, b.shape
    tm = _pick(M, (1024, 1408, 512, 256, 128))
    tn = _pick(N, (1408, 1024, 640, 512))
    if tn == N and N > 2048:
        tn = _pick(N, (256, 128))
    tk = _pick(K, (1024, 1408, 640, 512, 256, 128))
    nk = K // tk
    grid = (M // tm, N // tn, nk)
    if dims == "tn":
        a_spec = pl.BlockSpec((tk, tm), lambda i, j, k: (k, i))
    else:
        a_spec = pl.BlockSpec((tm, tk), lambda i, j, k: (i, k))
    if dims == "nt":
        b_spec = pl.BlockSpec((tn, tk), lambda i, j, k: (j, k))
    else:
        b_spec = pl.BlockSpec((tk, tn), lambda i, j, k: (k, j))
    o_spec = pl.BlockSpec((tm, tn), lambda i, j, k: (i, j))
    in_specs = [a_spec, b_spec]
    args = [a, b]
    if res is not None:
        in_specs.append(o_spec)
        args.append(res)

    def body(*refs):
        if res is not None:
            a_ref, b_ref, r_ref, o_ref, acc_ref = refs
        else:
            a_ref, b_ref, o_ref, acc_ref = refs
            r_ref = None
        k = pl.program_id(2)
        part = _dot(a_ref[...], b_ref[...], dims)

        def finish(total):
            if r_ref is not None:
                total = total + res_scale * r_ref[...]
            o_ref[...] = total.astype(o_ref.dtype)

        if nk == 1:
            finish(part)
        else:
            @pl.when(k == 0)
            def _():
                acc_ref[...] = part

            @pl.when(jnp.logical_and(k > 0, k < nk - 1))
            def _():
                acc_ref[...] += part

            @pl.when(k == nk - 1)
            def _():
                finish(acc_ref[...] + part)

    out_sds = jax.ShapeDtypeStruct((M, N), out_dtype)
    acc = pltpu.VMEM((tm, tn), F32)
    if not ride:
        return pl.pallas_call(
            body, name=name, grid=grid, in_specs=in_specs, out_specs=o_spec, out_shape=out_sds,
            scratch_shapes=[acc], compiler_params=_cparams(("parallel", "parallel", "arbitrary")),
        )(*args)
    x_args, x_in, x_out, x_shapes, x_sems, wrap = _riding_exchange(list(ride), grid)
    return pl.pallas_call(
        wrap(body, len(args), 1), name=name, grid=grid, in_specs=in_specs + x_in, out_specs=[o_spec] + x_out,
        out_shape=[out_sds] + x_shapes, scratch_shapes=[acc] + x_sems,
        compiler_params=_cparams(("arbitrary",) * 3, has_side_effects=True),
    )(*args, *x_args)


LN_ROWS = 256


def _ln_stats(r):
    mu = jnp.mean(r, axis=-1, keepdims=True)
    xc = r - mu
    var = jnp.mean(xc * xc, axis=-1, keepdims=True)
    return xc * lax.rsqrt(var + LN_EPS)


def _ln_fwd(r, g, b, name):
    S, D = r.shape

    def body(r_ref, g_ref, b_ref, h_ref):
        h_ref[...] = _ln_stats(r_ref[...]) * g_ref[...] + b_ref[...]

    row = pl.BlockSpec((LN_ROWS, D), lambda i: (i, 0))
    vec = pl.BlockSpec((1, D), lambda i: (0, 0))
    return pl.pallas_call(
        body, name=name, grid=(S // LN_ROWS,), in_specs=[row, vec, vec], out_specs=row,
        out_shape=jax.ShapeDtypeStruct((S, D), F32),
        compiler_params=_cparams(("parallel",)),
    )(r, g, b)


def _ln_bwd_core(xhat, dy, g):
    dxh = dy * g
    m1 = jnp.mean(dxh, axis=-1, keepdims=True)
    m2 = jnp.mean(dxh * xhat, axis=-1, keepdims=True)
    return dxh - m1 - xhat * m2


def _ln_bwd(r, dy, g, name):
    S, D = r.shape

    def body(r_ref, dy_ref, g_ref, dr_ref, gg_ref, gb_ref):
        x = r_ref[...]
        mu = jnp.mean(x, axis=-1, keepdims=True)
        xc = x - mu
        rstd = lax.rsqrt(jnp.mean(xc * xc, axis=-1, keepdims=True) + LN_EPS)
        xhat = xc * rstd
        dy = dy_ref[...]
        dr_ref[...] = rstd * _ln_bwd_core(xhat, dy, g_ref[...])

        @pl.when(pl.program_id(0) == 0)
        def _():
            gg_ref[...] = jnp.zeros_like(gg_ref)
            gb_ref[...] = jnp.zeros_like(gb_ref)

        gg_ref[...] += jnp.sum(dy * xhat, axis=0, keepdims=True)
        gb_ref[...] += jnp.sum(dy, axis=0, keepdims=True)

    row = pl.BlockSpec((LN_ROWS, D), lambda i: (i, 0))
    vec = pl.BlockSpec((1, D), lambda i: (0, 0))
    return pl.pallas_call(
        body, name=name, grid=(S // LN_ROWS,), in_specs=[row, row, vec], out_specs=[row, vec, vec],
        out_shape=[jax.ShapeDtypeStruct((S, D), F32), jax.ShapeDtypeStruct((1, D), F32),
                   jax.ShapeDtypeStruct((1, D), F32)],
        compiler_params=_cparams(("arbitrary",)),
    )(r, dy, g)


def _ln_loss_bwd(r, target, g, b, name):
    S, D = r.shape

    def body(r_ref, t_ref, g_ref, b_ref, dr_ref, loss_ref, gg_ref, gb_ref):
        x = r_ref[...]
        mu = jnp.mean(x, axis=-1, keepdims=True)
        xc = x - mu
        rstd = lax.rsqrt(jnp.mean(xc * xc, axis=-1, keepdims=True) + LN_EPS)
        xhat = xc * rstd
        y = xhat * g_ref[...] + b_ref[...]
        err = y - t_ref[...]
        dy = err * (1.0 / D)
        dr_ref[...] = rstd * _ln_bwd_core(xhat, dy, g_ref[...])

        @pl.when(pl.program_id(0) == 0)
        def _():
            loss_ref[...] = jnp.zeros_like(loss_ref)
            gg_ref[...] = jnp.zeros_like(gg_ref)
            gb_ref[...] = jnp.zeros_like(gb_ref)

        per_row = jnp.sum(err * err, axis=-1, keepdims=True) * (0.5 / D)
        loss_ref[...] += jnp.broadcast_to(jnp.sum(per_row, axis=0, keepdims=True), loss_ref.shape)
        gg_ref[...] += jnp.sum(dy * xhat, axis=0, keepdims=True)
        gb_ref[...] += jnp.sum(dy, axis=0, keepdims=True)

    row = pl.BlockSpec((LN_ROWS, D), lambda i: (i, 0))
    vec = pl.BlockSpec((1, D), lambda i: (0, 0))
    lvec = pl.BlockSpec((1, LANE), lambda i: (0, 0))
    return pl.pallas_call(
        body, name=name, grid=(S // LN_ROWS,), in_specs=[row, row, vec, vec],
        out_specs=[row, lvec, vec, vec],
        out_shape=[jax.ShapeDtypeStruct((S, D), F32), jax.ShapeDtypeStruct((1, LANE), F32),
                   jax.ShapeDtypeStruct((1, D), F32), jax.ShapeDtypeStruct((1, D), F32)],
        compiler_params=_cparams(("arbitrary",)),
    )(r, target, g, b)


CONV_COLS = 256
CONV_ROWS = 256
HALO = 8
INV_SQRT2 = 1.0 / math.sqrt(2.0)
INV_SQRT2PI = 1.0 / math.sqrt(2.0 * math.pi)


def _gelu(x):
    return 0.5 * x * (1.0 + lax.erf(x * INV_SQRT2))


def _gelu_grad(x):
    return 0.5 * (1.0 + lax.erf(x * INV_SQRT2)) + x * jnp.exp(-0.5 * x * x) * INV_SQRT2PI


def _conv_rows(ext, w_ref, b_ref, n):
    total = ext.shape[0]
    s1 = pltpu.roll(ext, 1, 0)
    s2 = pltpu.roll(ext, 2, 0)
    u = w_ref[2:3, :] * ext + w_ref[1:2, :] * s1 + w_ref[0:1, :] * s2 + b_ref[...]
    return u[HALO:total], s1[HALO:total], s2[HALO:total]


def _conv_gelu_fwd(u0, conv_w, conv_b, name):
    S, C2 = u0.shape
    F = C2 // 2
    ncb = F // CONV_COLS
    nrc = S // CONV_ROWS

    def body(ua_ref, uc_ref, wa_ref, wc_ref, ba_ref, bc_ref, p_ref):
        def chunk(ci, _):
            r0 = pl.multiple_of(ci * CONV_ROWS, CONV_ROWS)
            p0 = pl.multiple_of(jnp.maximum(r0 - HALO, 0), HALO)
            keep = (ci > 0).astype(F32)

            def load(ref):
                prev = ref[pl.ds(p0, HALO), :] * keep
                return jnp.concatenate([prev, ref[pl.ds(r0, CONV_ROWS), :]], axis=0)

            a, _, _ = _conv_rows(load(ua_ref), wa_ref, ba_ref, CONV_ROWS)
            c, _, _ = _conv_rows(load(uc_ref), wc_ref, bc_ref, CONV_ROWS)
            p_ref[pl.ds(r0, CONV_ROWS), :] = (_gelu(a) * c).astype(p_ref.dtype)
            return 0

        lax.fori_loop(0, nrc, chunk, 0)

    col_a = pl.BlockSpec((S, CONV_COLS), lambda j: (0, j))
    col_c = pl.BlockSpec((S, CONV_COLS), lambda j: (0, j + ncb))
    w_a = pl.BlockSpec((CONV_WIDTH, CONV_COLS), lambda j: (0, j))
    w_c = pl.BlockSpec((CONV_WIDTH, CONV_COLS), lambda j: (0, j + ncb))
    b_a = pl.BlockSpec((1, CONV_COLS), lambda j: (0, j))
    b_c = pl.BlockSpec((1, CONV_COLS), lambda j: (0, j + ncb))
    return pl.pallas_call(
        body, name=name, grid=(ncb,), in_specs=[col_a, col_c, w_a, w_c, b_a, b_c], out_specs=col_a,
        out_shape=jax.ShapeDtypeStruct((S, F), MXU_DTYPE),
        compiler_params=_cparams(("parallel",)),
    )(u0, u0, conv_w, conv_w, conv_b, conv_b)


def _conv_gelu_bwd(u0, dp, conv_w, conv_b, name):
    S, C2 = u0.shape
    F = C2 // 2
    ncb = F // CONV_COLS
    nrc = S // CONV_ROWS
    EXT = CONV_ROWS + HALO

    def body(ua_ref, uc_ref, dp_ref, wa_ref, wc_ref, ba_ref, bc_ref,
             da_ref, dc_ref, gwa_ref, gwc_ref, gba_ref, gbc_ref):
        gwa_ref[...] = jnp.zeros_like(gwa_ref)
        gwc_ref[...] = jnp.zeros_like(gwc_ref)
        gba_ref[...] = jnp.zeros_like(gba_ref)
        gbc_ref[...] = jnp.zeros_like(gbc_ref)
        rid = lax.broadcasted_iota(jnp.int32, (EXT, CONV_COLS), 0)

        def chunk(ci, _):
            r0 = pl.multiple_of(ci * CONV_ROWS, CONV_ROWS)
            p0 = pl.multiple_of(jnp.maximum(r0 - HALO, 0), HALO)
            n0 = pl.multiple_of(jnp.minimum(r0 + CONV_ROWS, S - HALO), HALO)
            keep_prev = (ci > 0).astype(F32)
            keep_next = (ci < nrc - 1).astype(F32)

            def load(ref):
                return jnp.concatenate([ref[pl.ds(p0, HALO), :] * keep_prev,
                                        ref[pl.ds(r0, CONV_ROWS), :],
                                        ref[pl.ds(n0, HALO), :] * keep_next], axis=0)

            ext_a = load(ua_ref)
            ext_c = load(uc_ref)
            a, a1, a2 = _conv_rows(ext_a, wa_ref, ba_ref, EXT)
            c, c1, c2 = _conv_rows(ext_c, wc_ref, bc_ref, EXT)
            a0 = ext_a[HALO:HALO + EXT]
            c0 = ext_c[HALO:HALO + EXT]
            dpe = jnp.concatenate([dp_ref[pl.ds(r0, CONV_ROWS), :].astype(F32),
                                   dp_ref[pl.ds(n0, HALO), :].astype(F32) * keep_next], axis=0)
            d_a = dpe * c * _gelu_grad(a)
            d_c = dpe * _gelu(a)
            own = rid < CONV_ROWS

            def back(d_u, w_ref, x0, x1, x2, d_ref, gw_ref, gb_ref):
                d_u0 = (w_ref[2:3, :] * d_u + w_ref[1:2, :] * pltpu.roll(d_u, EXT - 1, 0)
                        + w_ref[0:1, :] * pltpu.roll(d_u, EXT - 2, 0))
                d_ref[pl.ds(r0, CONV_ROWS), :] = d_u0[0:CONV_ROWS].astype(d_ref.dtype)
                d_own = jnp.where(own, d_u, 0.0)
                gw_ref[...] += jnp.concatenate(
                    [jnp.sum(d_own * x2, axis=0, keepdims=True),
                     jnp.sum(d_own * x1, axis=0, keepdims=True),
                     jnp.sum(d_own * x0, axis=0, keepdims=True)], axis=0)
                gb_ref[...] += jnp.sum(d_own, axis=0, keepdims=True)

            back(d_a, wa_ref, a0, a1, a2, da_ref, gwa_ref, gba_ref)
            back(d_c, wc_ref, c0, c1, c2, dc_ref, gwc_ref, gbc_ref)
            return 0

        lax.fori_loop(0, nrc, chunk, 0)

    col_a = pl.BlockSpec((S, CONV_COLS), lambda j: (0, j))
    col_c = pl.BlockSpec((S, CONV_COLS), lambda j: (0, j + ncb))
    w_a = pl.BlockSpec((CONV_WIDTH, CONV_COLS), lambda j: (0, j))
    w_c = pl.BlockSpec((CONV_WIDTH, CONV_COLS), lambda j: (0, j + ncb))
    b_a = pl.BlockSpec((1, CONV_COLS), lambda j: (0, j))
    b_c = pl.BlockSpec((1, CONV_COLS), lambda j: (0, j + ncb))
    outs = pl.pallas_call(
        body, name=name, grid=(ncb,),
        in_specs=[col_a, col_c, col_a, w_a, w_c, b_a, b_c],
        out_specs=[col_a, col_a, w_a, w_a, b_a, b_a],
        out_shape=[jax.ShapeDtypeStruct((S, F), MXU_DTYPE), jax.ShapeDtypeStruct((S, F), MXU_DTYPE),
                   jax.ShapeDtypeStruct((CONV_WIDTH, F), F32), jax.ShapeDtypeStruct((CONV_WIDTH, F), F32),
                   jax.ShapeDtypeStruct((1, F), F32), jax.ShapeDtypeStruct((1, F), F32)],
        compiler_params=_cparams(("parallel",)),
    )(u0, u0, dp, conv_w, conv_w, conv_b, conv_b)
    da, dc, gwa, gwc, gba, gbc = outs
    return (jnp.concatenate([da, dc], axis=1), jnp.concatenate([gwa, gwc], axis=1),
            jnp.concatenate([gba, gbc], axis=1))


SB_TK = 128
SB_TQ_FWD = 1024
SB_TQ_BWD = 1024


def _softplus(z):
    return jnp.maximum(z, 0.0) + jnp.log(1.0 + jnp.exp(-jnp.abs(z)))


def _tri_ones(after):
    r = lax.broadcasted_iota(jnp.int32, (SB_TK, 2 * SB_TK), 0)
    c = lax.broadcasted_iota(jnp.int32, (SB_TK, 2 * SB_TK), 1)
    tri = (r > c) if after else (r < c)
    return jnp.where(c >= SB_TK, 1.0, jnp.where(tri, 1.0, 0.0)).astype(BF16)


def _sb_block_specs(S, TQ):
    NP = SB_WIDTH // LANE
    return [pl.BlockSpec((TQ, LANE), lambda p, i: (i, p)),
            pl.BlockSpec((S, LANE), lambda p, i: (0, NP + p)),
            pl.BlockSpec((S, LANE), lambda p, i: (0, 2 * NP + p))]


def _sb_fwd(qkv, name, ride=()):
    S = qkv.shape[0]
    TK = SB_TK
    TQ = min(SB_TQ_FWD, S)
    R = TQ // TK
    NP = SB_WIDTH // LANE
    grid = (NP, S // TQ)

    def body(q_ref, k_ref, v_ref, o_ref, t_ref):
        i = pl.program_id(1)
        row = lax.broadcasted_iota(jnp.int32, (TQ, TK), 0)
        col = lax.broadcasted_iota(jnp.int32, (TQ, TK), 1)
        after_ones = _tri_ones(True)
        first = lax.broadcasted_iota(jnp.int32, (TK, LANE), 1) < SB_HEAD_DIM

        def block(j, carry, r0):
            k0 = pl.multiple_of(j * TK, TK)
            masked = r0 is not None
            r0 = r0 or 0
            kab = k_ref[pl.ds(k0, TK), :]
            vab = v_ref[pl.ds(k0, TK), :]
            none = jnp.zeros_like(kab)
            acc_r = carry[0][r0:]
            tails = []
            for hh in range(2):
                kh = jnp.where(first, kab, none) if hh == 0 else jnp.where(first, none, kab)
                vh = jnp.where(first, vab, none) if hh == 0 else jnp.where(first, none, vab)
                tail = carry[1 + hh]
                z = _dot(q_ref[r0:, :], kh, "nt")
                sp = _softplus(z)
                if masked:
                    strict = col[r0:] < row[r0:] - r0
                    sp = jnp.where(strict, sp, 0.0)
                cs = _dot_split(sp, after_ones, "nn")
                w = jnp.exp(z - sp - cs[:, :TK] - tail[r0:])
                if masked:
                    w = jnp.where(strict, w, 0.0)
                acc_r = acc_r + _dot(w, vh, "nn")
                tot = cs[:, TK:]
                if r0:
                    tot = jnp.concatenate([jnp.zeros((r0, TK), F32), tot], axis=0)
                tails.append(tail + tot)
            acc = jnp.concatenate([carry[0][:r0], acc_r], axis=0) if r0 else acc_r
            return acc, tails[0], tails[1]

        carry = (jnp.zeros((TQ, LANE), F32), jnp.zeros((TQ, TK), F32), jnp.zeros((TQ, TK), F32))
        for u in reversed(range(R)):
            carry = block(R * i + u, carry, u * TK)
        carry = lax.fori_loop(0, R * i, lambda n, c: block(R * i - 1 - n, c, None), carry)
        o_ref[...] = carry[0]
        t_ref[0] = carry[1]
        t_ref[1] = carry[2]

    ospec = pl.BlockSpec((TQ, LANE), lambda p, i: (i, p))
    tspec = pl.BlockSpec((2, TQ, TK), lambda p, i: (p, i, 0))
    x_args, x_in, x_out, x_shapes, x_sems, wrap = _riding_exchange(list(ride), grid)
    return pl.pallas_call(
        wrap(body, 3, 2) if ride else body, name=name, grid=grid,
        in_specs=_sb_block_specs(S, TQ) + (x_in if ride else []),
        out_specs=[ospec, tspec] + (x_out if ride else []),
        out_shape=[jax.ShapeDtypeStruct((S, SB_WIDTH), F32), jax.ShapeDtypeStruct((SB_HEADS, S, TK), F32)]
        + (x_shapes if ride else []),
        scratch_shapes=x_sems if ride else [],
        compiler_params=_cparams(("arbitrary", "arbitrary"), has_side_effects=bool(ride)),
    )(qkv, qkv, qkv, *(x_args if ride else []))


def _sb_bwd(qkv, d_cat, total, name, ride=()):
    S = qkv.shape[0]
    TK = SB_TK
    TQ = min(SB_TQ_BWD, S)
    R = TQ // TK
    NP = SB_WIDTH // LANE
    grid = (NP, S // TQ)
    scale = SB_HEAD_DIM ** -0.5

    def body(q_ref, k_ref, v_ref, do_ref, t_ref, dq_ref, dk_ref, dv_ref):
        i = pl.program_id(1)

        @pl.when(i == 0)
        def _():
            dk_ref[...] = jnp.zeros_like(dk_ref)
            dv_ref[...] = jnp.zeros_like(dv_ref)

        row = lax.broadcasted_iota(jnp.int32, (TQ, TK), 0)
        col = lax.broadcasted_iota(jnp.int32, (TQ, TK), 1)
        after_ones = _tri_ones(True)
        before_ones = _tri_ones(False)
        first = lax.broadcasted_iota(jnp.int32, (TK, LANE), 1) < SB_HEAD_DIM
        first_q = lax.broadcasted_iota(jnp.int32, (TQ, LANE), 1) < SB_HEAD_DIM
        qab = q_ref[...]
        doab = do_ref[...].astype(MXU_DTYPE)
        qdo = jnp.concatenate([qab, doab], axis=1)
        none_q = jnp.zeros_like(qab)
        q_h = [jnp.where(first_q, qab, none_q), jnp.where(first_q, none_q, qab)]
        do_h = [jnp.where(first_q, doab, none_q), jnp.where(first_q, none_q, doab)]

        def block(j, carry, r0):
            k0 = pl.multiple_of(j * TK, TK)
            masked = r0 is not None
            r0 = r0 or 0
            kab = k_ref[pl.ds(k0, TK), :]
            vab = v_ref[pl.ds(k0, TK), :]
            none = jnp.zeros_like(kab)
            dq_r = carry[0][r0:]
            dk_blk = dv_blk = None
            sums = []
            for hh in range(2):
                kh = jnp.where(first, kab, none) if hh == 0 else jnp.where(first, none, kab)
                vh = jnp.where(first, vab, none) if hh == 0 else jnp.where(first, none, vab)
                seen, gsum = carry[1 + 2 * hh], carry[2 + 2 * hh]
                kv = jnp.concatenate([jnp.concatenate([kh, none], axis=1),
                                      jnp.concatenate([none, vh], axis=1)], axis=0)
                zdw = _dot(qdo[r0:], kv, "nt")
                z = zdw[:, :TK]
                sp = _softplus(z)
                logsig = z - sp
                if masked:
                    strict = col[r0:] < row[r0:] - r0
                    sp = jnp.where(strict, sp, 0.0)
                cs = _dot_split(sp, after_ones, "nn")
                seen_r = seen[r0:] + cs[:, TK:]
                w = jnp.exp(logsig - cs[:, :TK] - (t_ref[hh, r0:, :] - seen_r))
                if masked:
                    w = jnp.where(strict, w, 0.0)
                g = w * zdw[:, TK:]
                cg = _dot_split(g, before_ones, "nn")
                dz = g - jnp.exp(logsig) * (g + cg[:, :TK] + gsum[r0:])
                if masked:
                    dz = jnp.where(strict, dz, 0.0)
                dzb = dz.astype(MXU_DTYPE)
                dkc = _dot(dzb, q_h[hh][r0:], "tn")
                dvc = _dot(w, do_h[hh][r0:], "tn")
                dk_blk = dkc if dk_blk is None else dk_blk + dkc
                dv_blk = dvc if dv_blk is None else dv_blk + dvc
                dq_r = dq_r + _dot(dzb, kh, "nn")
                gsum_r = gsum[r0:] + cg[:, TK:]
                if r0:
                    seen_r = jnp.concatenate([seen[:r0], seen_r], axis=0)
                    gsum_r = jnp.concatenate([gsum[:r0], gsum_r], axis=0)
                sums += [seen_r, gsum_r]
            dk_ref[pl.ds(k0, TK), :] += dk_blk
            dv_ref[pl.ds(k0, TK), :] += dv_blk
            dq = jnp.concatenate([carry[0][:r0], dq_r], axis=0) if r0 else dq_r
            return (dq, *sums)

        zero = jnp.zeros((TQ, TK), F32)
        carry = (jnp.zeros((TQ, LANE), F32), zero, zero, zero, zero)
        carry = lax.fori_loop(0, R * i, lambda j, c: block(j, c, None), carry)
        for u in range(R):
            carry = block(R * i + u, carry, u * TK)
        dq_ref[...] = carry[0] * scale

    qspec = pl.BlockSpec((TQ, LANE), lambda p, i: (i, p))
    full = pl.BlockSpec((S, LANE), lambda p, i: (0, p))
    tspec = pl.BlockSpec((2, TQ, TK), lambda p, i: (p, i, 0))
    sds = jax.ShapeDtypeStruct((S, SB_WIDTH), F32)
    x_args, x_in, x_out, x_shapes, x_sems, wrap = _riding_exchange(list(ride), grid)
    return pl.pallas_call(
        wrap(body, 5, 3) if ride else body, name=name, grid=grid,
        in_specs=_sb_block_specs(S, TQ) + [qspec, tspec] + (x_in if ride else []),
        out_specs=[qspec, full, full] + (x_out if ride else []),
        out_shape=[sds, sds, sds] + (x_shapes if ride else []),
        scratch_shapes=x_sems if ride else [],
        compiler_params=_cparams(("arbitrary", "arbitrary"), has_side_effects=bool(ride)),
    )(qkv, qkv, qkv, d_cat, total, *(x_args if ride else []))


GLA_GROUP_FWD = 8
GLA_GROUP_BWD = 4


def _gla_masks():
    C = CHUNK
    row = lax.broadcasted_iota(jnp.int32, (C, C), 0)
    col = lax.broadcasted_iota(jnp.int32, (C, C), 1)
    return row, col


def _log_sigmoid(x):
    return -_softplus(-x)


def _gla_chunk_fwd(qc, kc, vc, gate, row, col):
    C = CHUNK
    la = _log_sigmoid(gate) * (1.0 / GLA_TAU)
    incl = jnp.where(row >= col, 1.0, 0.0).astype(BF16)
    b = _dot_split_lhs01(incl, la)
    b_ref = jnp.sum(jnp.where(row == C // 2 - 1, b, 0.0), axis=0, keepdims=True)
    b_last = jnp.sum(la, axis=0, keepdims=True)
    qs = qc * (GLA_KEY_DIM ** -0.5)
    q_in = qs * jnp.exp(b - b_ref)
    k_in = kc * jnp.exp(b_ref - b)
    k_dec = kc * jnp.exp(b_last - b)
    q_b = qs * jnp.exp(b)
    sc = jnp.where(row >= col, _dot(q_in, k_in, "nt"), 0.0)
    o_intra = _dot(sc, vc, "nn")
    upd = _dot(k_dec, vc, "tn")
    ones = jnp.ones((C, GLA_VAL_DIM), BF16)
    dec_col = jnp.exp(_dot_split_tn(la, ones))
    return dict(la=la, b=b, b_ref=b_ref, b_last=b_last, qs=qs, q_in=q_in, k_in=k_in, k_dec=k_dec,
                q_b=q_b, sc=sc, o_intra=o_intra, upd=upd, dec_col=dec_col)


def _dot_split_lhs01(m01, x):
    hi = x.astype(BF16)
    lo = (x - hi.astype(F32)).astype(BF16)
    dn = (((1,), (0,)), ((), ()))
    return (lax.dot_general(m01, hi, dn, preferred_element_type=F32)
            + lax.dot_general(m01, lo, dn, preferred_element_type=F32))


def _dot_split_tn(x, m01):
    hi = x.astype(BF16)
    lo = (x - hi.astype(F32)).astype(BF16)
    dn = (((0,), (0,)), ((), ()))
    return (lax.dot_general(hi, m01, dn, preferred_element_type=F32)
            + lax.dot_general(lo, m01, dn, preferred_element_type=F32))


def _dot_split_nt01(m01, x):
    hi = x.astype(BF16)
    lo = (x - hi.astype(F32)).astype(BF16)
    dn = (((1,), (1,)), ((), ()))
    return (lax.dot_general(m01, hi, dn, preferred_element_type=F32)
            + lax.dot_general(m01, lo, dn, preferred_element_type=F32))


def _rms_gate(o, gg, gnorm):
    rinv = lax.rsqrt(jnp.mean(o * o, axis=-1, keepdims=True) + RMS_EPS)
    o_n = o * rinv
    sg = 1.0 / (1.0 + jnp.exp(-gg))
    return o_n, rinv, sg


def _gla_fwd(gq, gk, proj, gate_up_h, gate_bias_h, gnorm, name):
    Hg, S, dk = gq.shape
    dv = GLA_VAL_DIM
    C = CHUNK
    G = GLA_GROUP_FWD
    nchunk = S // C
    ngroup = nchunk // G

    def body(q_ref, k_ref, v_ref, gg_ref, ga_ref, gu_ref, gb_ref, gn_ref, o_ref, prev_ref):
        row, col = _gla_masks()
        gu = gu_ref[0]
        gbias = gb_ref[0]
        gnorm_v = gn_ref[...]

        def group(gi, state):
            for u in range(G):
                ci = gi * G + u
                r0 = pl.multiple_of(ci * C, C)
                rows = pl.ds(r0, C)
                gate = _dot(ga_ref[rows, :], gu, "nn") + gbias
                f = _gla_chunk_fwd(q_ref[0, rows, :], k_ref[0, rows, :], v_ref[rows, :], gate, row, col)
                prev_ref[0, ci] = state
                o = f["o_intra"] + _dot(f["q_b"], state, "nn")
                state = f["dec_col"] * state + f["upd"]
                o_n, _, sg = _rms_gate(o, gg_ref[rows, :], gnorm_v)
                o_ref[rows, :] = o_n * gnorm_v * (gg_ref[rows, :] * sg)
            return state

        lax.fori_loop(0, ngroup, group, jnp.zeros((dk, dv), F32))

    hspec = pl.BlockSpec((1, S, dk), lambda h: (h, 0, 0))
    vspec = pl.BlockSpec((S, dv), lambda h: (0, h))
    return pl.pallas_call(
        body, name=name, grid=(Hg,),
        in_specs=[hspec, hspec,
                  pl.BlockSpec((S, dv), lambda h: (0, h + (OFF_GV - OFF_GQ) // LANE)),
                  pl.BlockSpec((S, dv), lambda h: (0, h + (OFF_GG - OFF_GQ) // LANE)),
                  pl.BlockSpec((S, LANE), lambda h: (0, (OFF_GA - OFF_GQ) // LANE)),
                  pl.BlockSpec((1, LANE, dk), lambda h: (h, 0, 0)),
                  pl.BlockSpec((1, 1, dk), lambda h: (h, 0, 0)),
                  pl.BlockSpec((1, dv), lambda h: (0, 0))],
        out_specs=[vspec, pl.BlockSpec((1, nchunk, dk, dv), lambda h: (h, 0, 0, 0))],
        out_shape=[jax.ShapeDtypeStruct((S, Hg * dv), F32),
                   jax.ShapeDtypeStruct((Hg, nchunk, dk, dv), F32)],
        compiler_params=_cparams(("parallel",)),
    )(gq, gk, proj, proj, proj, gate_up_h, gate_bias_h, gnorm)


def _gla_bwd(gq, gk, proj, gate_up_h, gate_bias_h, gnorm, prev, d_cat, name):
    Hg, S, dk = gq.shape
    dv = GLA_VAL_DIM
    C = CHUNK
    G = GLA_GROUP_BWD
    nchunk = S // C
    ngroup = nchunk // G

    def body(q_ref, k_ref, v_ref, gg_ref, ga_ref, gu_ref, gb_ref, gn_ref, prev_ref, do_ref,
             dq_ref, dk_ref, dv_ref, dgg_ref, dga_ref, ggu_ref, ggb_ref, ggn_ref):
        h = pl.program_id(0)
        row, col = _gla_masks()
        gu = gu_ref[0]
        gbias = gb_ref[0]
        gnorm_v = gn_ref[...]
        ggu_ref[...] = jnp.zeros_like(ggu_ref)
        ggb_ref[...] = jnp.zeros_like(ggb_ref)

        @pl.when(h == 0)
        def _():
            dga_ref[...] = jnp.zeros_like(dga_ref)
            ggn_ref[...] = jnp.zeros_like(ggn_ref)

        upper_incl = jnp.where(col >= row, 1.0, 0.0).astype(BF16)
        ones_8 = jnp.ones((8, dv), BF16)

        def group(gn, dstate):
            gi = ngroup - 1 - gn
            for u in reversed(range(G)):
                ci = gi * G + u
                r0 = pl.multiple_of(ci * C, C)
                rows = pl.ds(r0, C)
                ga = ga_ref[rows, :]
                gate = _dot(ga, gu, "nn") + gbias
                qc, kc, vc = q_ref[0, rows, :], k_ref[0, rows, :], v_ref[rows, :]
                f = _gla_chunk_fwd(qc, kc, vc, gate, row, col)
                state = prev_ref[0, ci]
                o = f["o_intra"] + _dot(f["q_b"], state, "nn")
                ggv = gg_ref[rows, :]
                o_n, rinv, sg = _rms_gate(o, ggv, gnorm_v)
                dout = do_ref[rows, :]
                silu = ggv * sg
                dgg_ref[rows, :] = dout * o_n * gnorm_v * (sg * (1.0 + ggv * (1.0 - sg)))
                d_ong = dout * silu
                ggn_ref[...] += jnp.sum(d_ong * o_n, axis=0, keepdims=True)
                d_on = d_ong * gnorm_v
                d_o = rinv * (d_on - o_n * jnp.mean(d_on * o_n, axis=-1, keepdims=True))
                d_upd = dstate
                d_dec_col = dstate * state * f["dec_col"]
                dstate = f["dec_col"] * dstate + _dot(f["q_b"], d_o, "tn")
                dsc = jnp.where(row >= col, _dot(d_o, vc, "nt"), 0.0)
                dv_ref[rows, :] = _dot(f["sc"], d_o, "tn") + _dot(f["k_dec"], d_upd, "nn")
                dq_in = _dot(dsc, f["k_in"], "nn")
                dk_in = _dot(dsc, f["q_in"], "tn")
                dq_b = _dot(d_o, state, "nt")
                dkdec = _dot(vc, d_upd, "nt")
                b = f["b"]
                e1 = jnp.exp(b - f["b_ref"])
                e2 = jnp.exp(f["b_ref"] - b)
                e3 = jnp.exp(f["b_last"] - b)
                eb = jnp.exp(b)
                dq_ref[0, rows, :] = (dq_in * e1 + dq_b * eb) * (GLA_KEY_DIM ** -0.5)
                dk_ref[0, rows, :] = dk_in * e2 + dkdec * e3
                t_q = dq_in * f["q_in"]
                t_k = dk_in * f["k_in"]
                t_d = dkdec * f["k_dec"]
                db = t_q - t_k - t_d + dq_b * f["q_b"]
                db_ref = jnp.sum(t_k - t_q, axis=0, keepdims=True)
                db_last = (jnp.sum(t_d, axis=0, keepdims=True)
                           + jnp.max(_dot_split_nt01(ones_8, d_dec_col), axis=0, keepdims=True))
                db = db + jnp.where(row == C // 2 - 1, db_ref, 0.0) + jnp.where(row == C - 1, db_last, 0.0)
                dla = _dot_split_lhs01(upper_incl, db)
                d_gate = dla * (1.0 / GLA_TAU) * (1.0 / (1.0 + jnp.exp(gate)))
                ggb_ref[0] += jnp.sum(d_gate, axis=0, keepdims=True)
                ggu_ref[0] += _dot(ga, d_gate, "tn")
                dga_ref[rows, :] += _dot(d_gate, gu, "nt")
            return dstate

        lax.fori_loop(0, ngroup, group, jnp.zeros((dk, dv), F32))

    hspec = pl.BlockSpec((1, S, dk), lambda h: (h, 0, 0))
    vspec = pl.BlockSpec((S, dv), lambda h: (0, h))
    gaspec = pl.BlockSpec((S, LANE), lambda h: (0, 0))
    guspec = pl.BlockSpec((1, LANE, dk), lambda h: (h, 0, 0))
    gbspec = pl.BlockSpec((1, 1, dk), lambda h: (h, 0, 0))
    gnspec = pl.BlockSpec((1, dv), lambda h: (0, 0))
    return pl.pallas_call(
        body, name=name, grid=(Hg,),
        in_specs=[hspec, hspec,
                  pl.BlockSpec((S, dv), lambda h: (0, h + (OFF_GV - OFF_GQ) // LANE)),
                  pl.BlockSpec((S, dv), lambda h: (0, h + (OFF_GG - OFF_GQ) // LANE)),
                  pl.BlockSpec((S, LANE), lambda h: (0, (OFF_GA - OFF_GQ) // LANE)), guspec, gbspec,
                  pl.BlockSpec((1, dv), lambda h: (0, 0)),
                  pl.BlockSpec((1, nchunk, dk, dv), lambda h: (h, 0, 0, 0)),
                  pl.BlockSpec((S, dv), lambda h: (0, h + SB_WIDTH // LANE))],
        out_specs=[hspec, hspec, vspec, vspec, gaspec, guspec, gbspec, gnspec],
        out_shape=[jax.ShapeDtypeStruct((Hg, S, dk), F32), jax.ShapeDtypeStruct((Hg, S, dk), F32),
                   jax.ShapeDtypeStruct((S, Hg * dv), F32), jax.ShapeDtypeStruct((S, Hg * dv), F32),
                   jax.ShapeDtypeStruct((S, LANE), F32), jax.ShapeDtypeStruct((Hg, LANE, dk), F32),
                   jax.ShapeDtypeStruct((Hg, 1, dk), F32), jax.ShapeDtypeStruct((1, dv), F32)],
        compiler_params=_cparams(("arbitrary",)),
    )(gq, gk, proj, proj, proj, gate_up_h, gate_bias_h, gnorm, prev, d_cat)


def _exchange_copies(scatter_flags, ins, outs, send_sems, recv_sems, local_sems):
    n_peer = N_DEV - 1
    x, y, c = lax.axis_index("x"), lax.axis_index("y"), lax.axis_index("c")
    me = 4 * x + 2 * y + c
    copies = []
    for a, scatter in enumerate(scatter_flags):
        own = ins[a].at[me] if scatter else ins[a]
        copies.append(pltpu.make_async_copy(own, outs[a].at[me], local_sems.at[a]))
    for r in range(1, N_DEV):
        px = 1 - x if r & 4 else x
        py = 1 - y if r & 2 else y
        pc = 1 - c if r & 1 else c
        for a, scatter in enumerate(scatter_flags):
            src = ins[a].at[4 * px + 2 * py + pc] if scatter else ins[a]
            copies.append(pltpu.make_async_remote_copy(
                src_ref=src, dst_ref=outs[a].at[me],
                send_sem=send_sems.at[a * n_peer + r - 1], recv_sem=recv_sems.at[a * n_peer + r - 1],
                device_id=(px, py, pc), device_id_type=MESH_ID))
    return copies


def _exchange_shapes(items):
    out_shape = []
    for arr, scatter in items:
        shp = arr.shape if scatter else (N_DEV,) + arr.shape
        out_shape.append(jax.ShapeDtypeStruct(shp, arr.dtype))
    n = len(items)
    sems = [pltpu.SemaphoreType.DMA((n * (N_DEV - 1),)), pltpu.SemaphoreType.DMA((n * (N_DEV - 1),)),
            pltpu.SemaphoreType.DMA((n,))]
    return out_shape, sems


def _exchange(items, name):
    n = len(items)
    flags = [sc for _, sc in items]

    def body(*refs):
        copies = _exchange_copies(flags, refs[:n], refs[n:2 * n], *refs[2 * n:])
        for cp in copies:
            cp.start()
        for cp in copies:
            cp.wait()

    out_shape, sems = _exchange_shapes(items)
    any_spec = pl.BlockSpec(memory_space=pl.ANY)
    return pl.pallas_call(
        body, name=name, in_specs=[any_spec] * n, out_specs=[any_spec] * n, out_shape=out_shape,
        scratch_shapes=sems, compiler_params=pltpu.CompilerParams(has_side_effects=True),
    )(*[arr for arr, _ in items])


def _riding_exchange(items, grid):
    n = len(items)
    flags = [sc for _, sc in items]
    out_shape, sems = _exchange_shapes(items)
    any_spec = pl.BlockSpec(memory_space=pl.ANY)

    def wrap(body, n_in, n_out):
        def fused(*refs):
            ins = refs[:n_in]
            x_ins = refs[n_in:n_in + n]
            outs = refs[n_in + n:n_in + n + n_out]
            x_outs = refs[n_in + n + n_out:n_in + 2 * n + n_out]
            rest = refs[n_in + 2 * n + n_out:]
            x_sems, scratch = rest[len(rest) - 3:], rest[:len(rest) - 3]
            first = last = True
            for d, n_d in enumerate(grid):
                first = jnp.logical_and(first, pl.program_id(d) == 0)
                last = jnp.logical_and(last, pl.program_id(d) == n_d - 1)

            @pl.when(first)
            def _():
                for cp in _exchange_copies(flags, x_ins, x_outs, *x_sems):
                    cp.start()

            body(*ins, *outs, *scratch)

            @pl.when(last)
            def _():
                for cp in _exchange_copies(flags, x_ins, x_outs, *x_sems):
                    cp.wait()

        return fused

    return [arr for arr, _ in items], [any_spec] * n, [any_spec] * n, out_shape, sems, wrap


def _adamw(grecv, w, m, v, name):
    R, C = w.shape
    tile = _pick(R, (256, 176, 128)) if R * C > 65536 else R
    bc1 = 1.0 - ADAM_B1 ** ADAM_STEP
    bc2 = 1.0 - ADAM_B2 ** ADAM_STEP

    def body(gr_ref, w_ref, m_ref, v_ref, g_ref, d_ref, nm_ref, nv_ref):
        g = gr_ref[0].astype(F32)
        for q in range(1, N_DEV):
            g = g + gr_ref[q].astype(F32)
        nm = ADAM_B1 * m_ref[...] + (1.0 - ADAM_B1) * g
        nv = ADAM_B2 * v_ref[...] + (1.0 - ADAM_B2) * (g * g)
        m_hat = nm / bc1
        v_hat = nv / bc2
        g_ref[...] = g
        d_ref[...] = -ADAM_LR * (m_hat / (jnp.sqrt(v_hat) + ADAM_EPS) + ADAM_WD * w_ref[...])
        nm_ref[...] = nm
        nv_ref[...] = nv

    blk = pl.BlockSpec((tile, C), lambda i: (i, 0))
    sds = jax.ShapeDtypeStruct((R, C), F32)
    return pl.pallas_call(
        body, name=name, grid=(R // tile,),
        in_specs=[pl.BlockSpec((N_DEV, tile, C), lambda i: (0, i, 0)), blk, blk, blk],
        out_specs=[blk, blk, blk, blk], out_shape=[sds, sds, sds, sds],
        compiler_params=_cparams(("parallel",)),
    )(grecv, w, m, v)


def _pack_rows(parts, rows, dtype):
    flat = jnp.concatenate([p.reshape(-1).astype(dtype) for p in parts])
    return jnp.pad(flat, (0, rows * LANE - flat.shape[0])).reshape(rows, LANE)


def _unpack_rows(buf, shapes):
    flat = buf.reshape(-1)
    out, off = [], 0
    for shp in shapes:
        n = math.prod(shp)
        out.append(flat[off:off + n].reshape(shp))
        off += n
    return out


def _shard_cols(g):
    rows, cols = g.shape
    return g.reshape(rows, N_DEV, cols // N_DEV).transpose(1, 0, 2)


def _unshard_cols(blocks):
    return blocks.transpose(1, 0, 2).reshape(blocks.shape[1], -1)


def _heads(t, n, d):
    return t.reshape(t.shape[0], n, d).transpose(1, 0, 2)


def _unheads(t):
    return t.transpose(1, 0, 2).reshape(t.shape[1], -1)


def kernel(x, w_in, gate_up, gate_bias, gla_norm_g, w_out, ln1_g, ln1_b, w_up, conv_w, conv_b, w_down, ln2_g, ln2_b, loss_target, m_w_in, m_gate_up, m_gate_bias, m_gla_norm_g, m_w_out, m_ln1_g, m_ln1_b, m_w_up, m_conv_w, m_conv_b, m_w_down, m_ln2_g, m_ln2_b, v_w_in, v_gate_up, v_gate_bias, v_gla_norm_g, v_w_out, v_ln1_g, v_ln1_b, v_w_up, v_conv_w, v_conv_b, v_w_down, v_ln2_g, v_ln2_b):
    S, D = x.shape[1], x.shape[2]
    x2, tgt = x[0], loss_target[0]

    gathered = _exchange([(w_in[0].astype(MXU_DTYPE), False), (gate_up[0], False), (conv_w[0], False)],
                         "gather_w_in")
    w_in_f = _unshard_cols(gathered[0])
    gate_up_f = _unshard_cols(gathered[1])
    conv_w_f = _unshard_cols(gathered[2])
    w_in_pad = jnp.pad(w_in_f, ((0, 0), (0, IN_PAD - IN_WIDTH)))
    gate_up_h = _heads(jnp.pad(gate_up_f, ((0, LANE - GLA_GATE_RANK), (0, 0))), GLA_HEADS, GLA_KEY_DIM)
    gate_bias_h = gate_bias.reshape(GLA_HEADS, 1, GLA_KEY_DIM)

    w_qkv = jnp.concatenate([w_in_f[:, :OFF_SBK] * (SB_HEAD_DIM ** -0.5), w_in_f[:, OFF_SBK:OFF_GQ]], axis=1)
    qkv = _matmul(x2, w_qkv, "nn", MXU_DTYPE, "proj_sb")
    proj = _matmul(x2, w_in_pad[:, OFF_GQ:], "nn", F32, "proj_gla")
    sb_o, sb_tot, g_out, g_up, g_down = _sb_fwd(
        qkv, "sb_fwd", ride=[(w_out[0].astype(MXU_DTYPE), False), (w_up[0].astype(MXU_DTYPE), False),
                             (w_down[0].astype(MXU_DTYPE), False)])
    w_out_f = g_out.reshape(-1, D)
    w_up_f = _unshard_cols(g_up)
    w_down_f = g_down.reshape(-1, D)
    gq = _heads(proj[:, :OFF_GK - OFF_GQ], GLA_HEADS, GLA_KEY_DIM)
    gk = _heads(proj[:, OFF_GK - OFF_GQ:OFF_GV - OFF_GQ], GLA_HEADS, GLA_KEY_DIM)
    gla_o, prev = _gla_fwd(gq, gk, proj, gate_up_h, gate_bias_h, gla_norm_g, "gla_fwd")
    cat = jnp.concatenate([sb_o, gla_o], axis=1)
    r1 = _matmul(cat, w_out_f, "nn", F32, "mix", res=x2, res_scale=DN_ALPHA)
    h = _ln_fwd(r1, ln1_g, ln1_b, "ln1")
    u0 = _matmul(h, w_up_f, "nn", F32, "ffn_up")
    p = _conv_gelu_fwd(u0, conv_w_f, conv_b, "conv_gelu")
    r2 = _matmul(p, w_down_f, "nn", F32, "ffn_down", res=h, res_scale=DN_ALPHA)
    d_r2, loss_p, g_ln2_g, g_ln2_b = _ln_loss_bwd(r2, tgt, ln2_g, ln2_b, "ln2_loss")

    d_p = _matmul(d_r2, w_down_f, "nt", MXU_DTYPE, "d_ffn_act")
    g_w_down = _matmul(p, d_r2, "tn", F32, "grad_w_down")
    d_u0, g_conv_w, g_conv_b = _conv_gelu_bwd(u0, d_p, conv_w_f, conv_b, "conv_gelu_bwd")
    g_w_up = _matmul(h, d_u0, "tn", F32, "grad_w_up")
    d_h = _matmul(d_u0, w_up_f, "nt", F32, "d_h", res=d_r2, res_scale=DN_ALPHA)
    d_r1, g_ln1_g, g_ln1_b = _ln_bwd(r1, d_h, ln1_g, "ln1_bwd")
    g_w_out = _matmul(cat, d_r1, "tn", F32, "grad_w_out")
    d_cat = _matmul(d_r1, w_out_f, "nt", F32, "d_cat")
    (d_gq, d_gk, d_gv, d_gg, d_ga_pad, g_gu_h, g_gb_h, g_gnorm) = _gla_bwd(
        gq, gk, proj, gate_up_h, gate_bias_h, gla_norm_g, prev, d_cat, "gla_bwd")
    g_gate_up = _unheads(g_gu_h[:, :GLA_GATE_RANK, :])
    g_gate_bias = g_gb_h.reshape(1, -1)
    small_g = [g_gate_bias, g_gnorm, g_ln1_g, g_ln1_b, g_conv_b, g_ln2_g, g_ln2_b, loss_p]
    n_small = sum(t.size for t in small_g)
    rows_small = -(-n_small // (8 * LANE)) * 8
    d_sq, d_sk, d_sv, *recv_rest = _sb_bwd(
        qkv, d_cat, sb_tot, "sb_bwd",
        ride=[(g_w_out.reshape(N_DEV, -1, D).astype(BF16), True), (_shard_cols(g_w_up).astype(BF16), True),
              (g_w_down.reshape(N_DEV, -1, D).astype(BF16), True), (_shard_cols(g_gate_up), True),
              (_shard_cols(g_conv_w), True), (_pack_rows(small_g, rows_small, F32), False)])
    d_proj = jnp.concatenate([d_sq, d_sk, d_sv, _unheads(d_gq), _unheads(d_gk), d_gv, d_gg, d_ga_pad],
                             axis=1)
    g_w_in = _matmul(x2, d_proj, "tn", F32, "grad_w_in")[:, :IN_WIDTH]
    d_x, recv_in = _matmul(d_proj, w_in_pad, "nt", F32, "d_x", res=d_r1, res_scale=DN_ALPHA,
                           ride=[(_shard_cols(g_w_in).astype(BF16), True)])

    recv = [recv_in, recv_rest[0], recv_rest[1], recv_rest[2], recv_rest[3], recv_rest[4], recv_rest[5]]
    sharded = [(w_in, m_w_in, v_w_in), (w_out, m_w_out, v_w_out), (w_up, m_w_up, v_w_up),
               (w_down, m_w_down, v_w_down), (gate_up, m_gate_up, v_gate_up), (conv_w, m_conv_w, v_conv_w)]
    upd = [_adamw(recv[n], w[0], m[0], v[0], "adamw_%d" % n) for n, (w, m, v) in enumerate(sharded)]
    zl = jnp.zeros((1, LANE), F32)

    def pks(parts):
        return _pack_rows(parts + [zl], rows_small, F32)

    small = _adamw(recv[6], pks([gate_bias, gla_norm_g, ln1_g, ln1_b, conv_b, ln2_g, ln2_b]),
                   pks([m_gate_bias, m_gla_norm_g, m_ln1_g, m_ln1_b, m_conv_b, m_ln2_g, m_ln2_b]),
                   pks([v_gate_bias, v_gla_norm_g, v_ln1_g, v_ln1_b, v_conv_b, v_ln2_g, v_ln2_b]),
                   "adamw_replicated")
    small_shapes = [gate_bias.shape, gla_norm_g.shape, ln1_g.shape, ln1_b.shape, conv_b.shape, ln2_g.shape,
                    ln2_b.shape, (1, LANE)]
    outs = []
    loss = None
    for kind in range(4):
        b_w_in, b_w_out, b_w_up, b_w_down, b_gate_up, b_conv_w = [u[kind][None] for u in upd]
        s_gb, s_gn, s_l1g, s_l1b, s_cb, s_l2g, s_l2b, s_loss = _unpack_rows(small[kind], small_shapes)
        if kind == 0:
            loss = s_loss[0, 0]
        outs += [b_w_in, b_gate_up, s_gb, s_gn, b_w_out, s_l1g, s_l1b, b_w_up, b_conv_w, s_cb, b_w_down,
                 s_l2g, s_l2b]
    return (loss, d_x[None], *outs)
```

```python
import math

import jax
import jax.numpy as jnp
from jax import lax
from jax.experimental import pallas as pl
from jax.experimental.pallas import tpu as pltpu

F32 = jnp.float32
BF16 = jnp.bfloat16
MXU_DTYPE = jnp.bfloat16

N_DEV = 8
D_MODEL = 1024
SB_WIDTH = 512
SB_HEADS = 8
SB_HEAD_DIM = 64
GLA_HEADS = 4
GLA_KEY_DIM = 64
GLA_VAL_DIM = 128
GLA_WIDTH = 512
GLA_GATE_RANK = 16
GLA_TAU = 16.0
CHUNK = 64
D_FF = 2816
CONV_WIDTH = 3
LN_EPS = 1e-5
RMS_EPS = 1e-6
DN_ALPHA = 2.0 ** 0.25
IN_WIDTH = 3088
LANE = 128
IN_PAD = 3200
OFF_SBQ, OFF_SBK, OFF_SBV = 0, 512, 1024
OFF_GQ, OFF_GK, OFF_GV, OFF_GG, OFF_GA = 1536, 1792, 2048, 2560, 3072
GLA_PAD = IN_PAD - OFF_GQ

ADAM_LR = 0.001
ADAM_B1 = 0.9
ADAM_B2 = 0.999
ADAM_EPS = 1e-08
ADAM_WD = 0.01
ADAM_STEP = 10

VMEM_LIMIT = 48 * 1024 * 1024
MESH_ID = pl.DeviceIdType.MESH


def _cparams(sem=None, **kw):
    return pltpu.CompilerParams(dimension_semantics=sem, vmem_limit_bytes=VMEM_LIMIT, **kw)


def _dot(a, b, dims):
    ca, cb = {"nn": (1, 0), "nt": (1, 1), "tn": (0, 0)}[dims]
    return lax.dot_general(a.astype(MXU_DTYPE), b.astype(MXU_DTYPE), (((ca,), (cb,)), ((), ())),
                           preferred_element_type=F32)


def _dot_split(a, b, dims):
    assert dims == "nn"
    hi = a.astype(BF16)
    lo = (a - hi.astype(F32)).astype(BF16)
    return lax.dot_general(jnp.concatenate([hi, lo], axis=1), jnp.concatenate([b, b], axis=0),
                           (((1,), (0,)), ((), ())), preferred_element_type=F32)


def _pick(dim, prefs):
    for p in prefs:
        if dim % p == 0:
            return p
    return dim


def _matmul(a, b, dims, out_dtype, name, res=None, res_scale=1.0, ride=()):
    if dims == "nn":
        (M, K), (_, N) = a.shape, b.shape
    elif dims == "nt":
        (M, K), (N, _) = a.shape, b.shape
    else:
        (K, M), (_, N) = a.shape, b.shape
    tm = _pick(M, (1024, 1408, 512, 256, 128))
    tn = _pick(N, (1408, 1024, 640, 512))
    if tn == N and N > 2048:
        tn = _pick(N, (256, 128))
    tk = _pick(K, (1024, 1408, 640, 512, 256, 128))
    nk = K // tk
    grid = (M // tm, N // tn, nk)
    if dims == "tn":
        a_spec = pl.BlockSpec((tk, tm), lambda i, j, k: (k, i))
    else:
        a_spec = pl.BlockSpec((tm, tk), lambda i, j, k: (i, k))
    if dims == "nt":
        b_spec = pl.BlockSpec((tn, tk), lambda i, j, k: (j, k))
    else:
        b_spec = pl.BlockSpec((tk, tn), lambda i, j, k: (k, j))
    o_spec = pl.BlockSpec((tm, tn), lambda i, j, k: (i, j))
    in_specs = [a_spec, b_spec]
    args = [a, b]
    if res is not None:
        in_specs.append(o_spec)
        args.append(res)

    def body(*refs):
        if res is not None:
            a_ref, b_ref, r_ref, o_ref, acc_ref = refs
        else:
            a_ref, b_ref, o_ref, acc_ref = refs
            r_ref = None
        k = pl.program_id(2)
        part = _dot(a_ref[...], b_ref[...], dims)

        def finish(total):
            if r_ref is not None:
                total = total + res_scale * r_ref[...]
            o_ref[...] = total.astype(o_ref.dtype)

        if nk == 1:
            finish(part)
        else:
            @pl.when(k == 0)
            def _():
                acc_ref[...] = part

            @pl.when(jnp.logical_and(k > 0, k < nk - 1))
            def _():
                acc_ref[...] += part

            @pl.when(k == nk - 1)
            def _():
                finish(acc_ref[...] + part)

    out_sds = jax.ShapeDtypeStruct((M, N), out_dtype)
    acc = pltpu.VMEM((tm, tn), F32)
    if not ride:
        return pl.pallas_call(
            body, name=name, grid=grid, in_specs=in_specs, out_specs=o_spec, out_shape=out_sds,
            scratch_shapes=[acc], compiler_params=_cparams(("parallel", "parallel", "arbitrary")),
        )(*args)
    x_args, x_in, x_out, x_shapes, x_sems, wrap = _riding_exchange(list(ride), grid)
    return pl.pallas_call(
        wrap(body, len(args), 1), name=name, grid=grid, in_specs=in_specs + x_in, out_specs=[o_spec] + x_out,
        out_shape=[out_sds] + x_shapes, scratch_shapes=[acc] + x_sems,
        compiler_params=_cparams(("arbitrary",) * 3, has_side_effects=True),
    )(*args, *x_args)


LN_ROWS = 256


def _ln_stats(r):
    mu = jnp.mean(r, axis=-1, keepdims=True)
    xc = r - mu
    var = jnp.mean(xc * xc, axis=-1, keepdims=True)
    return xc * lax.rsqrt(var + LN_EPS)


def _ln_fwd(r, g, b, name):
    S, D = r.shape

    def body(r_ref, g_ref, b_ref, h_ref):
        h_ref[...] = _ln_stats(r_ref[...]) * g_ref[...] + b_ref[...]

    row = pl.BlockSpec((LN_ROWS, D), lambda i: (i, 0))
    vec = pl.BlockSpec((1, D), lambda i: (0, 0))
    return pl.pallas_call(
        body, name=name, grid=(S // LN_ROWS,), in_specs=[row, vec, vec], out_specs=row,
        out_shape=jax.ShapeDtypeStruct((S, D), F32),
        compiler_params=_cparams(("parallel",)),
    )(r, g, b)


def _ln_bwd_core(xhat, dy, g):
    dxh = dy * g
    m1 = jnp.mean(dxh, axis=-1, keepdims=True)
    m2 = jnp.mean(dxh * xhat, axis=-1, keepdims=True)
    return dxh - m1 - xhat * m2


def _ln_bwd(r, dy, g, name):
    S, D = r.shape

    def body(r_ref, dy_ref, g_ref, dr_ref, gg_ref, gb_ref):
        x = r_ref[...]
        mu = jnp.mean(x, axis=-1, keepdims=True)
        xc = x - mu
        rstd = lax.rsqrt(jnp.mean(xc * xc, axis=-1, keepdims=True) + LN_EPS)
        xhat = xc * rstd
        dy = dy_ref[...]
        dr_ref[...] = rstd * _ln_bwd_core(xhat, dy, g_ref[...])

        @pl.when(pl.program_id(0) == 0)
        def _():
            gg_ref[...] = jnp.zeros_like(gg_ref)
            gb_ref[...] = jnp.zeros_like(gb_ref)

        gg_ref[...] += jnp.sum(dy * xhat, axis=0, keepdims=True)
        gb_ref[...] += jnp.sum(dy, axis=0, keepdims=True)

    row = pl.BlockSpec((LN_ROWS, D), lambda i: (i, 0))
    vec = pl.BlockSpec((1, D), lambda i: (0, 0))
    return pl.pallas_call(
        body, name=name, grid=(S // LN_ROWS,), in_specs=[row, row, vec], out_specs=[row, vec, vec],
        out_shape=[jax.ShapeDtypeStruct((S, D), F32), jax.ShapeDtypeStruct((1, D), F32),
                   jax.ShapeDtypeStruct((1, D), F32)],
        compiler_params=_cparams(("arbitrary",)),
    )(r, dy, g)


def _ln_loss_bwd(r, target, g, b, name):
    S, D = r.shape

    def body(r_ref, t_ref, g_ref, b_ref, dr_ref, loss_ref, gg_ref, gb_ref):
        x = r_ref[...]
        mu = jnp.mean(x, axis=-1, keepdims=True)
        xc = x - mu
        rstd = lax.rsqrt(jnp.mean(xc * xc, axis=-1, keepdims=True) + LN_EPS)
        xhat = xc * rstd
        y = xhat * g_ref[...] + b_ref[...]
        err = y - t_ref[...]
        dy = err * (1.0 / D)
        dr_ref[...] = rstd * _ln_bwd_core(xhat, dy, g_ref[...])

        @pl.when(pl.program_id(0) == 0)
        def _():
            loss_ref[...] = jnp.zeros_like(loss_ref)
            gg_ref[...] = jnp.zeros_like(gg_ref)
            gb_ref[...] = jnp.zeros_like(gb_ref)

        per_row = jnp.sum(err * err, axis=-1, keepdims=True) * (0.5 / D)
        loss_ref[...] += jnp.broadcast_to(jnp.sum(per_row, axis=0, keepdims=True), loss_ref.shape)
        gg_ref[...] += jnp.sum(dy * xhat, axis=0, keepdims=True)
        gb_ref[...] += jnp.sum(dy, axis=0, keepdims=True)

    row = pl.BlockSpec((LN_ROWS, D), lambda i: (i, 0))
    vec = pl.BlockSpec((1, D), lambda i: (0, 0))
    lvec = pl.BlockSpec((1, LANE), lambda i: (0, 0))
    return pl.pallas_call(
        body, name=name, grid=(S // LN_ROWS,), in_specs=[row, row, vec, vec],
        out_specs=[row, lvec, vec, vec],
        out_shape=[jax.ShapeDtypeStruct((S, D), F32), jax.ShapeDtypeStruct((1, LANE), F32),
                   jax.ShapeDtypeStruct((1, D), F32), jax.ShapeDtypeStruct((1, D), F32)],
        compiler_params=_cparams(("arbitrary",)),
    )(r, target, g, b)


CONV_COLS = 256
CONV_ROWS = 256
HALO = 8
INV_SQRT2 = 1.0 / math.sqrt(2.0)
INV_SQRT2PI = 1.0 / math.sqrt(2.0 * math.pi)


def _gelu(x):
    return 0.5 * x * (1.0 + lax.erf(x * INV_SQRT2))


def _gelu_and_grad(x):
    cdf = 0.5 * (1.0 + lax.erf(x * INV_SQRT2))
    return x * cdf, cdf + x * jnp.exp(-0.5 * x * x) * INV_SQRT2PI


def _conv_rows(ext, w_ref, b_ref, n):
    total = ext.shape[0]
    s1 = pltpu.roll(ext, 1, 0)
    s2 = pltpu.roll(ext, 2, 0)
    u = w_ref[2:3, :] * ext + w_ref[1:2, :] * s1 + w_ref[0:1, :] * s2 + b_ref[...]
    return u[HALO:total], s1[HALO:total], s2[HALO:total]


def _conv_gelu_fwd(u0, conv_w, conv_b, name):
    S, C2 = u0.shape
    F = C2 // 2
    ncb = F // CONV_COLS
    nrc = S // CONV_ROWS

    def body(ua_ref, uc_ref, wa_ref, wc_ref, ba_ref, bc_ref, p_ref):
        def chunk(ci, _):
            r0 = pl.multiple_of(ci * CONV_ROWS, CONV_ROWS)
            p0 = pl.multiple_of(jnp.maximum(r0 - HALO, 0), HALO)
            keep = (ci > 0).astype(F32)

            def load(ref):
                prev = ref[pl.ds(p0, HALO), :] * keep
                return jnp.concatenate([prev, ref[pl.ds(r0, CONV_ROWS), :]], axis=0)

            a, _, _ = _conv_rows(load(ua_ref), wa_ref, ba_ref, CONV_ROWS)
            c, _, _ = _conv_rows(load(uc_ref), wc_ref, bc_ref, CONV_ROWS)
            p_ref[pl.ds(r0, CONV_ROWS), :] = (_gelu(a) * c).astype(p_ref.dtype)
            return 0

        lax.fori_loop(0, nrc, chunk, 0)

    col_a = pl.BlockSpec((S, CONV_COLS), lambda j: (0, j))
    col_c = pl.BlockSpec((S, CONV_COLS), lambda j: (0, j + ncb))
    w_a = pl.BlockSpec((CONV_WIDTH, CONV_COLS), lambda j: (0, j))
    w_c = pl.BlockSpec((CONV_WIDTH, CONV_COLS), lambda j: (0, j + ncb))
    b_a = pl.BlockSpec((1, CONV_COLS), lambda j: (0, j))
    b_c = pl.BlockSpec((1, CONV_COLS), lambda j: (0, j + ncb))
    return pl.pallas_call(
        body, name=name, grid=(ncb,), in_specs=[col_a, col_c, w_a, w_c, b_a, b_c], out_specs=col_a,
        out_shape=jax.ShapeDtypeStruct((S, F), MXU_DTYPE),
        compiler_params=_cparams(("parallel",)),
    )(u0, u0, conv_w, conv_w, conv_b, conv_b)


def _conv_gelu_bwd(u0, dp, conv_w, conv_b, name):
    S, C2 = u0.shape
    F = C2 // 2
    ncb = F // CONV_COLS
    nrc = S // CONV_ROWS
    EXT = CONV_ROWS + HALO

    def body(ua_ref, uc_ref, dp_ref, wa_ref, wc_ref, ba_ref, bc_ref,
             da_ref, dc_ref, gwa_ref, gwc_ref, gba_ref, gbc_ref):
        gwa_ref[...] = jnp.zeros_like(gwa_ref)
        gwc_ref[...] = jnp.zeros_like(gwc_ref)
        gba_ref[...] = jnp.zeros_like(gba_ref)
        gbc_ref[...] = jnp.zeros_like(gbc_ref)
        rid = lax.broadcasted_iota(jnp.int32, (EXT, CONV_COLS), 0)

        def chunk(ci, _):
            r0 = pl.multiple_of(ci * CONV_ROWS, CONV_ROWS)
            p0 = pl.multiple_of(jnp.maximum(r0 - HALO, 0), HALO)
            n0 = pl.multiple_of(jnp.minimum(r0 + CONV_ROWS, S - HALO), HALO)
            keep_prev = (ci > 0).astype(F32)
            keep_next = (ci < nrc - 1).astype(F32)

            def load(ref):
                return jnp.concatenate([ref[pl.ds(p0, HALO), :] * keep_prev,
                                        ref[pl.ds(r0, CONV_ROWS), :],
                                        ref[pl.ds(n0, HALO), :] * keep_next], axis=0)

            ext_a = load(ua_ref)
            ext_c = load(uc_ref)
            a, a1, a2 = _conv_rows(ext_a, wa_ref, ba_ref, EXT)
            c, c1, c2 = _conv_rows(ext_c, wc_ref, bc_ref, EXT)
            a0 = ext_a[HALO:HALO + EXT]
            c0 = ext_c[HALO:HALO + EXT]
            dpe = jnp.concatenate([dp_ref[pl.ds(r0, CONV_ROWS), :].astype(F32),
                                   dp_ref[pl.ds(n0, HALO), :].astype(F32) * keep_next], axis=0)
            gelu_a, slope_a = _gelu_and_grad(a)
            d_a = dpe * c * slope_a
            d_c = dpe * gelu_a
            own = rid < CONV_ROWS

            def back(d_u, w_ref, x0, x1, x2, d_ref, gw_ref, gb_ref):
                d_u0 = (w_ref[2:3, :] * d_u + w_ref[1:2, :] * pltpu.roll(d_u, EXT - 1, 0)
                        + w_ref[0:1, :] * pltpu.roll(d_u, EXT - 2, 0))
                d_ref[pl.ds(r0, CONV_ROWS), :] = d_u0[0:CONV_ROWS].astype(d_ref.dtype)
                d_own = jnp.where(own, d_u, 0.0)
                gw_ref[...] += jnp.concatenate(
                    [jnp.sum(d_own * x2, axis=0, keepdims=True),
                     jnp.sum(d_own * x1, axis=0, keepdims=True),
                     jnp.sum(d_own * x0, axis=0, keepdims=True)], axis=0)
                gb_ref[...] += jnp.sum(d_own, axis=0, keepdims=True)

            back(d_a, wa_ref, a0, a1, a2, da_ref, gwa_ref, gba_ref)
            back(d_c, wc_ref, c0, c1, c2, dc_ref, gwc_ref, gbc_ref)
            return 0

        lax.fori_loop(0, nrc, chunk, 0)

    col_a = pl.BlockSpec((S, CONV_COLS), lambda j: (0, j))
    col_c = pl.BlockSpec((S, CONV_COLS), lambda j: (0, j + ncb))
    w_a = pl.BlockSpec((CONV_WIDTH, CONV_COLS), lambda j: (0, j))
    w_c = pl.BlockSpec((CONV_WIDTH, CONV_COLS), lambda j: (0, j + ncb))
    b_a = pl.BlockSpec((1, CONV_COLS), lambda j: (0, j))
    b_c = pl.BlockSpec((1, CONV_COLS), lambda j: (0, j + ncb))
    outs = pl.pallas_call(
        body, name=name, grid=(ncb,),
        in_specs=[col_a, col_c, col_a, w_a, w_c, b_a, b_c],
        out_specs=[col_a, col_a, w_a, w_a, b_a, b_a],
        out_shape=[jax.ShapeDtypeStruct((S, F), MXU_DTYPE), jax.ShapeDtypeStruct((S, F), MXU_DTYPE),
                   jax.ShapeDtypeStruct((CONV_WIDTH, F), F32), jax.ShapeDtypeStruct((CONV_WIDTH, F), F32),
                   jax.ShapeDtypeStruct((1, F), F32), jax.ShapeDtypeStruct((1, F), F32)],
        compiler_params=_cparams(("parallel",)),
    )(u0, u0, dp, conv_w, conv_w, conv_b, conv_b)
    da, dc, gwa, gwc, gba, gbc = outs
    return (jnp.concatenate([da, dc], axis=1), jnp.concatenate([gwa, gwc], axis=1),
            jnp.concatenate([gba, gbc], axis=1))


SB_TK = 128
SB_TQ_FWD = 1024
SB_TQ_BWD = 1024


def _softplus(z):
    return jnp.maximum(z, 0.0) + jnp.log(1.0 + jnp.exp(-jnp.abs(z)))


def _tri_ones(after):
    r = lax.broadcasted_iota(jnp.int32, (SB_TK, 2 * SB_TK), 0)
    c = lax.broadcasted_iota(jnp.int32, (SB_TK, 2 * SB_TK), 1)
    tri = (r > c) if after else (r < c)
    return jnp.where(c >= SB_TK, 1.0, jnp.where(tri, 1.0, 0.0)).astype(BF16)


def _sb_block_specs(S, TQ):
    NP = SB_WIDTH // LANE
    return [pl.BlockSpec((TQ, LANE), lambda p, i: (i, p)),
            pl.BlockSpec((S, LANE), lambda p, i: (0, NP + p)),
            pl.BlockSpec((S, LANE), lambda p, i: (0, 2 * NP + p))]


def _sb_fwd(qkv, name, ride=()):
    S = qkv.shape[0]
    TK = SB_TK
    TQ = min(SB_TQ_FWD, S)
    R = TQ // TK
    NP = SB_WIDTH // LANE
    grid = (NP, S // TQ)

    def body(q_ref, k_ref, v_ref, o_ref, t_ref):
        i = pl.program_id(1)
        row = lax.broadcasted_iota(jnp.int32, (TQ, TK), 0)
        col = lax.broadcasted_iota(jnp.int32, (TQ, TK), 1)
        after_ones = _tri_ones(True)
        first = lax.broadcasted_iota(jnp.int32, (TK, LANE), 1) < SB_HEAD_DIM

        def block(j, carry, r0):
            k0 = pl.multiple_of(j * TK, TK)
            masked = r0 is not None
            r0 = r0 or 0
            kab = k_ref[pl.ds(k0, TK), :]
            vab = v_ref[pl.ds(k0, TK), :]
            none = jnp.zeros_like(kab)
            k2 = jnp.concatenate([jnp.where(first, kab, none), jnp.where(first, none, kab)], axis=0)
            v2 = jnp.concatenate([jnp.where(first, vab, none), jnp.where(first, none, vab)], axis=0)
            z2 = _dot(q_ref[r0:, :], k2, "nt")
            tails, ws = [], []
            for hh in range(2):
                tail = carry[1 + hh]
                z = z2[:, hh * TK:(hh + 1) * TK]
                sp = _softplus(z)
                if masked:
                    strict = col[r0:] < row[r0:] - r0
                    sp = jnp.where(strict, sp, 0.0)
                cs = _dot_split(sp, after_ones, "nn")
                w = jnp.exp(z - sp - cs[:, :TK] - tail[r0:])
                if masked:
                    w = jnp.where(strict, w, 0.0)
                ws.append(w.astype(MXU_DTYPE))
                tot = cs[:, TK:]
                if r0:
                    tot = jnp.concatenate([jnp.zeros((r0, TK), F32), tot], axis=0)
                tails.append(tail + tot)
            acc_r = carry[0][r0:] + _dot(jnp.concatenate(ws, axis=1), v2, "nn")
            acc = jnp.concatenate([carry[0][:r0], acc_r], axis=0) if r0 else acc_r
            return acc, tails[0], tails[1]

        carry = (jnp.zeros((TQ, LANE), F32), jnp.zeros((TQ, TK), F32), jnp.zeros((TQ, TK), F32))
        for u in reversed(range(R)):
            carry = block(R * i + u, carry, u * TK)
        carry = lax.fori_loop(0, R * i, lambda n, c: block(R * i - 1 - n, c, None), carry)
        o_ref[...] = carry[0].astype(o_ref.dtype)
        t_ref[0] = carry[1]
        t_ref[1] = carry[2]

    ospec = pl.BlockSpec((TQ, LANE), lambda p, i: (i, p))
    tspec = pl.BlockSpec((2, TQ, TK), lambda p, i: (p, i, 0))
    x_args, x_in, x_out, x_shapes, x_sems, wrap = _riding_exchange(list(ride), grid)
    return pl.pallas_call(
        wrap(body, 3, 2) if ride else body, name=name, grid=grid,
        in_specs=_sb_block_specs(S, TQ) + (x_in if ride else []),
        out_specs=[ospec, tspec] + (x_out if ride else []),
        out_shape=[jax.ShapeDtypeStruct((S, SB_WIDTH), MXU_DTYPE), jax.ShapeDtypeStruct((SB_HEADS, S, TK), F32)]
        + (x_shapes if ride else []),
        scratch_shapes=x_sems if ride else [],
        compiler_params=_cparams(("arbitrary", "arbitrary"), has_side_effects=bool(ride)),
    )(qkv, qkv, qkv, *(x_args if ride else []))


def _sb_bwd(qkv, d_cat, total, name, ride=()):
    S = qkv.shape[0]
    TK = SB_TK
    TQ = min(SB_TQ_BWD, S)
    R = TQ // TK
    NP = SB_WIDTH // LANE
    grid = (NP, S // TQ)
    scale = SB_HEAD_DIM ** -0.5

    def body(q_ref, k_ref, v_ref, do_ref, t_ref, dq_ref, dk_ref, dv_ref, dk_acc, dv_acc):
        i = pl.program_id(1)

        @pl.when(i == 0)
        def _():
            dk_acc[...] = jnp.zeros_like(dk_acc)
            dv_acc[...] = jnp.zeros_like(dv_acc)

        row = lax.broadcasted_iota(jnp.int32, (TQ, TK), 0)
        col = lax.broadcasted_iota(jnp.int32, (TQ, TK), 1)
        after_ones = _tri_ones(True)
        before_ones = _tri_ones(False)
        first = lax.broadcasted_iota(jnp.int32, (TK, LANE), 1) < SB_HEAD_DIM
        first_q = lax.broadcasted_iota(jnp.int32, (TQ, LANE), 1) < SB_HEAD_DIM
        qab = q_ref[...]
        doab = do_ref[...].astype(MXU_DTYPE)
        qdo = jnp.concatenate([qab, doab], axis=1)
        none_q = jnp.zeros_like(qab)
        q_h = [jnp.where(first_q, qab, none_q), jnp.where(first_q, none_q, qab)]
        do_h = [jnp.where(first_q, doab, none_q), jnp.where(first_q, none_q, doab)]

        def block(j, carry, r0):
            k0 = pl.multiple_of(j * TK, TK)
            masked = r0 is not None
            r0 = r0 or 0
            kab = k_ref[pl.ds(k0, TK), :]
            vab = v_ref[pl.ds(k0, TK), :]
            none = jnp.zeros_like(kab)
            k_h = [jnp.where(first, kab, none), jnp.where(first, none, kab)]
            v_h = [jnp.where(first, vab, none), jnp.where(first, none, vab)]
            dk_blk = dv_blk = None
            sums, dzs = [], []
            for hh in range(2):
                seen, gsum = carry[1 + 2 * hh], carry[2 + 2 * hh]
                kv = jnp.concatenate([jnp.concatenate([k_h[hh], none], axis=1),
                                      jnp.concatenate([none, v_h[hh]], axis=1)], axis=0)
                zdw = _dot(qdo[r0:], kv, "nt")
                z = zdw[:, :TK]
                sp = _softplus(z)
                logsig = z - sp
                if masked:
                    strict = col[r0:] < row[r0:] - r0
                    sp = jnp.where(strict, sp, 0.0)
                cs = _dot_split(sp, after_ones, "nn")
                seen_r = seen[r0:] + cs[:, TK:]
                w = jnp.exp(logsig - cs[:, :TK] - (t_ref[hh, r0:, :] - seen_r))
                if masked:
                    w = jnp.where(strict, w, 0.0)
                g = w * zdw[:, TK:]
                cg = _dot_split(g, before_ones, "nn")
                dz = g - jnp.exp(logsig) * (g + cg[:, :TK] + gsum[r0:])
                if masked:
                    dz = jnp.where(strict, dz, 0.0)
                dzb = dz.astype(MXU_DTYPE)
                dzs.append(dzb)
                dkc = _dot(dzb, q_h[hh][r0:], "tn")
                dvc = _dot(w, do_h[hh][r0:], "tn")
                dk_blk = dkc if dk_blk is None else dk_blk + dkc
                dv_blk = dvc if dv_blk is None else dv_blk + dvc
                gsum_r = gsum[r0:] + cg[:, TK:]
                if r0:
                    seen_r = jnp.concatenate([seen[:r0], seen_r], axis=0)
                    gsum_r = jnp.concatenate([gsum[:r0], gsum_r], axis=0)
                sums += [seen_r, gsum_r]
            dk_acc[pl.ds(k0, TK), :] += dk_blk
            dv_acc[pl.ds(k0, TK), :] += dv_blk
            dq_r = carry[0][r0:] + _dot(jnp.concatenate(dzs, axis=1), jnp.concatenate(k_h, axis=0), "nn")
            dq = jnp.concatenate([carry[0][:r0], dq_r], axis=0) if r0 else dq_r
            return (dq, *sums)

        zero = jnp.zeros((TQ, TK), F32)
        carry = (jnp.zeros((TQ, LANE), F32), zero, zero, zero, zero)
        carry = lax.fori_loop(0, R * i, lambda j, c: block(j, c, None), carry)
        for u in range(R):
            carry = block(R * i + u, carry, u * TK)
        dq_ref[...] = (carry[0] * scale).astype(dq_ref.dtype)

        @pl.when(i == S // TQ - 1)
        def _():
            dk_ref[...] = dk_acc[...].astype(dk_ref.dtype)
            dv_ref[...] = dv_acc[...].astype(dv_ref.dtype)

    qspec = pl.BlockSpec((TQ, LANE), lambda p, i: (i, p))
    full = pl.BlockSpec((S, LANE), lambda p, i: (0, p))
    tspec = pl.BlockSpec((2, TQ, TK), lambda p, i: (p, i, 0))
    sds = jax.ShapeDtypeStruct((S, SB_WIDTH), MXU_DTYPE)
    accs = [pltpu.VMEM((S, LANE), F32), pltpu.VMEM((S, LANE), F32)]
    x_args, x_in, x_out, x_shapes, x_sems, wrap = _riding_exchange(list(ride), grid)
    return pl.pallas_call(
        wrap(body, 5, 3) if ride else body, name=name, grid=grid,
        in_specs=_sb_block_specs(S, TQ) + [qspec, tspec] + (x_in if ride else []),
        out_specs=[qspec, full, full] + (x_out if ride else []),
        out_shape=[sds, sds, sds] + (x_shapes if ride else []),
        scratch_shapes=accs + (x_sems if ride else []),
        compiler_params=_cparams(("arbitrary", "arbitrary"), has_side_effects=bool(ride)),
    )(qkv, qkv, qkv, d_cat, total, *(x_args if ride else []))


GLA_GROUP_FWD = 8
GLA_GROUP_BWD = 4


def _gla_masks():
    C = CHUNK
    row = lax.broadcasted_iota(jnp.int32, (C, C), 0)
    col = lax.broadcasted_iota(jnp.int32, (C, C), 1)
    return row, col


def _log_sigmoid(x):
    return -_softplus(-x)


def _gla_chunk_fwd(qc, kc, vc, gate, row, col):
    C = CHUNK
    la = _log_sigmoid(gate) * (1.0 / GLA_TAU)
    incl = jnp.where(row >= col, 1.0, 0.0).astype(BF16)
    b = _dot_split_lhs01(incl, la)
    b_ref = jnp.sum(jnp.where(row == C // 2 - 1, b, 0.0), axis=0, keepdims=True)
    b_last = jnp.sum(la, axis=0, keepdims=True)
    qs = qc * (GLA_KEY_DIM ** -0.5)
    q_in = qs * jnp.exp(b - b_ref)
    k_in = kc * jnp.exp(b_ref - b)
    k_dec = kc * jnp.exp(b_last - b)
    q_b = qs * jnp.exp(b)
    sc = jnp.where(row >= col, _dot(q_in, k_in, "nt"), 0.0)
    o_intra = _dot(sc, vc, "nn")
    upd = _dot(k_dec, vc, "tn")
    ones = jnp.ones((C, GLA_VAL_DIM), BF16)
    dec_col = jnp.exp(_dot_split_tn(la, ones))
    return dict(la=la, b=b, b_ref=b_ref, b_last=b_last, qs=qs, q_in=q_in, k_in=k_in, k_dec=k_dec,
                q_b=q_b, sc=sc, o_intra=o_intra, upd=upd, dec_col=dec_col)


def _dot_split_lhs01(m01, x):
    hi = x.astype(BF16)
    lo = (x - hi.astype(F32)).astype(BF16)
    dn = (((1,), (0,)), ((), ()))
    return (lax.dot_general(m01, hi, dn, preferred_element_type=F32)
            + lax.dot_general(m01, lo, dn, preferred_element_type=F32))


def _dot_split_tn(x, m01):
    hi = x.astype(BF16)
    lo = (x - hi.astype(F32)).astype(BF16)
    dn = (((0,), (0,)), ((), ()))
    return (lax.dot_general(hi, m01, dn, preferred_element_type=F32)
            + lax.dot_general(lo, m01, dn, preferred_element_type=F32))


def _dot_split_nt01(m01, x):
    hi = x.astype(BF16)
    lo = (x - hi.astype(F32)).astype(BF16)
    dn = (((1,), (1,)), ((), ()))
    return (lax.dot_general(m01, hi, dn, preferred_element_type=F32)
            + lax.dot_general(m01, lo, dn, preferred_element_type=F32))


def _rms_gate(o, gg, gnorm):
    rinv = lax.rsqrt(jnp.mean(o * o, axis=-1, keepdims=True) + RMS_EPS)
    o_n = o * rinv
    sg = 1.0 / (1.0 + jnp.exp(-gg))
    return o_n, rinv, sg


def _gla_fwd(gq, gk, proj, gate_up_h, gate_bias_h, gnorm, name):
    Hg, S, dk = gq.shape
    dv = GLA_VAL_DIM
    C = CHUNK
    G = GLA_GROUP_FWD
    nchunk = S // C
    ngroup = nchunk // G

    def body(q_ref, k_ref, v_ref, gg_ref, ga_ref, gu_ref, gb_ref, gn_ref, o_ref, prev_ref):
        row, col = _gla_masks()
        gu = gu_ref[0]
        gbias = gb_ref[0]
        gnorm_v = gn_ref[...]

        def group(gi, state):
            for u in range(G):
                ci = gi * G + u
                r0 = pl.multiple_of(ci * C, C)
                rows = pl.ds(r0, C)
                gate = _dot(ga_ref[rows, :], gu, "nn") + gbias
                f = _gla_chunk_fwd(q_ref[0, rows, :], k_ref[0, rows, :], v_ref[rows, :], gate, row, col)
                prev_ref[0, ci] = state
                o = f["o_intra"] + _dot(f["q_b"], state, "nn")
                state = f["dec_col"] * state + f["upd"]
                o_n, _, sg = _rms_gate(o, gg_ref[rows, :], gnorm_v)
                o_ref[rows, :] = (o_n * gnorm_v * (gg_ref[rows, :] * sg)).astype(o_ref.dtype)
            return state

        lax.fori_loop(0, ngroup, group, jnp.zeros((dk, dv), F32))

    hspec = pl.BlockSpec((1, S, dk), lambda h: (h, 0, 0))
    vspec = pl.BlockSpec((S, dv), lambda h: (0, h))
    return pl.pallas_call(
        body, name=name, grid=(Hg,),
        in_specs=[hspec, hspec,
                  pl.BlockSpec((S, dv), lambda h: (0, h + (OFF_GV - OFF_GQ) // LANE)),
                  pl.BlockSpec((S, dv), lambda h: (0, h + (OFF_GG - OFF_GQ) // LANE)),
                  pl.BlockSpec((S, LANE), lambda h: (0, (OFF_GA - OFF_GQ) // LANE)),
                  pl.BlockSpec((1, LANE, dk), lambda h: (h, 0, 0)),
                  pl.BlockSpec((1, 1, dk), lambda h: (h, 0, 0)),
                  pl.BlockSpec((1, dv), lambda h: (0, 0))],
        out_specs=[vspec, pl.BlockSpec((1, nchunk, dk, dv), lambda h: (h, 0, 0, 0))],
        out_shape=[jax.ShapeDtypeStruct((S, Hg * dv), MXU_DTYPE),
                   jax.ShapeDtypeStruct((Hg, nchunk, dk, dv), F32)],
        compiler_params=_cparams(("parallel",)),
    )(gq, gk, proj, proj, proj, gate_up_h, gate_bias_h, gnorm)


def _gla_bwd(gq, gk, proj, gate_up_h, gate_bias_h, gnorm, prev, d_cat, name):
    Hg, S, dk = gq.shape
    dv = GLA_VAL_DIM
    C = CHUNK
    G = GLA_GROUP_BWD
    nchunk = S // C
    ngroup = nchunk // G

    def body(q_ref, k_ref, v_ref, gg_ref, ga_ref, gu_ref, gb_ref, gn_ref, prev_ref, do_ref,
             dq_ref, dk_ref, dv_ref, dgg_ref, dga_ref, ggu_ref, ggb_ref, ggn_ref):
        h = pl.program_id(0)
        row, col = _gla_masks()
        gu = gu_ref[0]
        gbias = gb_ref[0]
        gnorm_v = gn_ref[...]
        ggu_ref[...] = jnp.zeros_like(ggu_ref)
        ggb_ref[...] = jnp.zeros_like(ggb_ref)

        @pl.when(h == 0)
        def _():
            dga_ref[...] = jnp.zeros_like(dga_ref)
            ggn_ref[...] = jnp.zeros_like(ggn_ref)

        upper_incl = jnp.where(col >= row, 1.0, 0.0).astype(BF16)
        ones_8 = jnp.ones((8, dv), BF16)

        def group(gn, dstate):
            gi = ngroup - 1 - gn
            for u in reversed(range(G)):
                ci = gi * G + u
                r0 = pl.multiple_of(ci * C, C)
                rows = pl.ds(r0, C)
                ga = ga_ref[rows, :]
                gate = _dot(ga, gu, "nn") + gbias
                qc, kc, vc = q_ref[0, rows, :], k_ref[0, rows, :], v_ref[rows, :]
                f = _gla_chunk_fwd(qc, kc, vc, gate, row, col)
                state = prev_ref[0, ci]
                o = f["o_intra"] + _dot(f["q_b"], state, "nn")
                ggv = gg_ref[rows, :]
                o_n, rinv, sg = _rms_gate(o, ggv, gnorm_v)
                dout = do_ref[rows, :]
                silu = ggv * sg
                dgg_ref[rows, :] = (dout * o_n * gnorm_v * (sg * (1.0 + ggv * (1.0 - sg)))).astype(dgg_ref.dtype)
                d_ong = dout * silu
                ggn_ref[...] += jnp.sum(d_ong * o_n, axis=0, keepdims=True)
                d_on = d_ong * gnorm_v
                d_o = rinv * (d_on - o_n * jnp.mean(d_on * o_n, axis=-1, keepdims=True))
                d_upd = dstate
                d_dec_col = dstate * state * f["dec_col"]
                dstate = f["dec_col"] * dstate + _dot(f["q_b"], d_o, "tn")
                dsc = jnp.where(row >= col, _dot(d_o, vc, "nt"), 0.0)
                dv_ref[rows, :] = (_dot(f["sc"], d_o, "tn") + _dot(f["k_dec"], d_upd, "nn")).astype(dv_ref.dtype)
                dq_in = _dot(dsc, f["k_in"], "nn")
                dk_in = _dot(dsc, f["q_in"], "tn")
                dq_b = _dot(d_o, state, "nt")
                dkdec = _dot(vc, d_upd, "nt")
                b = f["b"]
                e1 = jnp.exp(b - f["b_ref"])
                e2 = jnp.exp(f["b_ref"] - b)
                e3 = jnp.exp(f["b_last"] - b)
                eb = jnp.exp(b)
                dq_ref[0, rows, :] = ((dq_in * e1 + dq_b * eb) * (GLA_KEY_DIM ** -0.5)).astype(dq_ref.dtype)
                dk_ref[0, rows, :] = (dk_in * e2 + dkdec * e3).astype(dk_ref.dtype)
                t_q = dq_in * f["q_in"]
                t_k = dk_in * f["k_in"]
                t_d = dkdec * f["k_dec"]
                db = t_q - t_k - t_d + dq_b * f["q_b"]
                db_ref = jnp.sum(t_k - t_q, axis=0, keepdims=True)
                db_last = (jnp.sum(t_d, axis=0, keepdims=True)
                           + jnp.max(_dot_split_nt01(ones_8, d_dec_col), axis=0, keepdims=True))
                db = db + jnp.where(row == C // 2 - 1, db_ref, 0.0) + jnp.where(row == C - 1, db_last, 0.0)
                dla = _dot_split_lhs01(upper_incl, db)
                d_gate = dla * (1.0 / GLA_TAU) * (1.0 / (1.0 + jnp.exp(gate)))
                ggb_ref[0] += jnp.sum(d_gate, axis=0, keepdims=True)
                ggu_ref[0] += _dot(ga, d_gate, "tn")
                dga_ref[rows, :] += _dot(d_gate, gu, "nt")
            return dstate

        lax.fori_loop(0, ngroup, group, jnp.zeros((dk, dv), F32))

    hspec = pl.BlockSpec((1, S, dk), lambda h: (h, 0, 0))
    vspec = pl.BlockSpec((S, dv), lambda h: (0, h))
    gaspec = pl.BlockSpec((S, LANE), lambda h: (0, 0))
    guspec = pl.BlockSpec((1, LANE, dk), lambda h: (h, 0, 0))
    gbspec = pl.BlockSpec((1, 1, dk), lambda h: (h, 0, 0))
    gnspec = pl.BlockSpec((1, dv), lambda h: (0, 0))
    return pl.pallas_call(
        body, name=name, grid=(Hg,),
        in_specs=[hspec, hspec,
                  pl.BlockSpec((S, dv), lambda h: (0, h + (OFF_GV - OFF_GQ) // LANE)),
                  pl.BlockSpec((S, dv), lambda h: (0, h + (OFF_GG - OFF_GQ) // LANE)),
                  pl.BlockSpec((S, LANE), lambda h: (0, (OFF_GA - OFF_GQ) // LANE)), guspec, gbspec,
                  pl.BlockSpec((1, dv), lambda h: (0, 0)),
                  pl.BlockSpec((1, nchunk, dk, dv), lambda h: (h, 0, 0, 0)),
                  pl.BlockSpec((S, dv), lambda h: (0, h + SB_WIDTH // LANE))],
        out_specs=[hspec, hspec, vspec, vspec, gaspec, guspec, gbspec, gnspec],
        out_shape=[jax.ShapeDtypeStruct((Hg, S, dk), MXU_DTYPE), jax.ShapeDtypeStruct((Hg, S, dk), MXU_DTYPE),
                   jax.ShapeDtypeStruct((S, Hg * dv), MXU_DTYPE), jax.ShapeDtypeStruct((S, Hg * dv), MXU_DTYPE),
                   jax.ShapeDtypeStruct((S, LANE), F32), jax.ShapeDtypeStruct((Hg, LANE, dk), F32),
                   jax.ShapeDtypeStruct((Hg, 1, dk), F32), jax.ShapeDtypeStruct((1, dv), F32)],
        compiler_params=_cparams(("arbitrary",)),
    )(gq, gk, proj, proj, proj, gate_up_h, gate_bias_h, gnorm, prev, d_cat)


def _exchange_copies(scatter_flags, ins, outs, send_sems, recv_sems, local_sems):
    n_peer = N_DEV - 1
    x, y, c = lax.axis_index("x"), lax.axis_index("y"), lax.axis_index("c")
    me = 4 * x + 2 * y + c
    copies = []
    for a, scatter in enumerate(scatter_flags):
        own = ins[a].at[me] if scatter else ins[a]
        copies.append(pltpu.make_async_copy(own, outs[a].at[me], local_sems.at[a]))
    for r in range(1, N_DEV):
        px = 1 - x if r & 4 else x
        py = 1 - y if r & 2 else y
        pc = 1 - c if r & 1 else c
        for a, scatter in enumerate(scatter_flags):
            src = ins[a].at[4 * px + 2 * py + pc] if scatter else ins[a]
            copies.append(pltpu.make_async_remote_copy(
                src_ref=src, dst_ref=outs[a].at[me],
                send_sem=send_sems.at[a * n_peer + r - 1], recv_sem=recv_sems.at[a * n_peer + r - 1],
                device_id=(px, py, pc), device_id_type=MESH_ID))
    return copies


def _exchange_shapes(items):
    out_shape = []
    for arr, scatter in items:
        shp = arr.shape if scatter else (N_DEV,) + arr.shape
        out_shape.append(jax.ShapeDtypeStruct(shp, arr.dtype))
    n = len(items)
    sems = [pltpu.SemaphoreType.DMA((n * (N_DEV - 1),)), pltpu.SemaphoreType.DMA((n * (N_DEV - 1),)),
            pltpu.SemaphoreType.DMA((n,))]
    return out_shape, sems


def _exchange(items, name):
    n = len(items)
    flags = [sc for _, sc in items]

    def body(*refs):
        copies = _exchange_copies(flags, refs[:n], refs[n:2 * n], *refs[2 * n:])
        for cp in copies:
            cp.start()
        for cp in copies:
            cp.wait()

    out_shape, sems = _exchange_shapes(items)
    any_spec = pl.BlockSpec(memory_space=pl.ANY)
    return pl.pallas_call(
        body, name=name, in_specs=[any_spec] * n, out_specs=[any_spec] * n, out_shape=out_shape,
        scratch_shapes=sems, compiler_params=pltpu.CompilerParams(has_side_effects=True),
    )(*[arr for arr, _ in items])


def _riding_exchange(items, grid):
    n = len(items)
    flags = [sc for _, sc in items]
    out_shape, sems = _exchange_shapes(items)
    any_spec = pl.BlockSpec(memory_space=pl.ANY)

    def wrap(body, n_in, n_out):
        def fused(*refs):
            ins = refs[:n_in]
            x_ins = refs[n_in:n_in + n]
            outs = refs[n_in + n:n_in + n + n_out]
            x_outs = refs[n_in + n + n_out:n_in + 2 * n + n_out]
            rest = refs[n_in + 2 * n + n_out:]
            x_sems, scratch = rest[len(rest) - 3:], rest[:len(rest) - 3]
            first = last = True
            for d, n_d in enumerate(grid):
                first = jnp.logical_and(first, pl.program_id(d) == 0)
                last = jnp.logical_and(last, pl.program_id(d) == n_d - 1)

            @pl.when(first)
            def _():
                for cp in _exchange_copies(flags, x_ins, x_outs, *x_sems):
                    cp.start()

            body(*ins, *outs, *scratch)

            @pl.when(last)
            def _():
                for cp in _exchange_copies(flags, x_ins, x_outs, *x_sems):
                    cp.wait()

        return fused

    return [arr for arr, _ in items], [any_spec] * n, [any_spec] * n, out_shape, sems, wrap


def _sum_devices(ref):
    g = ref[0].astype(F32)
    for q in range(1, N_DEV):
        g = g + ref[q].astype(F32)
    return g


def _adam_math(g, w, m, v):
    nm = ADAM_B1 * m + (1.0 - ADAM_B1) * g
    nv = ADAM_B2 * v + (1.0 - ADAM_B2) * (g * g)
    m_hat = nm / (1.0 - ADAM_B1 ** ADAM_STEP)
    v_hat = nv / (1.0 - ADAM_B2 ** ADAM_STEP)
    return -ADAM_LR * (m_hat / (jnp.sqrt(v_hat) + ADAM_EPS) + ADAM_WD * w), nm, nv


def _adamw(grecv, w, m, v, name):
    R, C = w.shape
    tile = _pick(R, (256, 176, 128)) if R * C > 65536 else R

    def body(gr_ref, w_ref, m_ref, v_ref, g_ref, d_ref, nm_ref, nv_ref):
        g = _sum_devices(gr_ref)
        g_ref[...] = g
        d_ref[...], nm_ref[...], nv_ref[...] = _adam_math(g, w_ref[...], m_ref[...], v_ref[...])

    blk = pl.BlockSpec((tile, C), lambda i: (i, 0))
    sds = jax.ShapeDtypeStruct((R, C), F32)
    return pl.pallas_call(
        body, name=name, grid=(R // tile,),
        in_specs=[pl.BlockSpec((N_DEV, tile, C), lambda i: (0, i, 0)), blk, blk, blk],
        out_specs=[blk, blk, blk, blk], out_shape=[sds, sds, sds, sds],
        compiler_params=_cparams(("parallel",)),
    )(grecv, w, m, v)


def _adamw_replicated(grecvs, loss_recv, ws, ms, vs, name):
    nt = len(ws)

    def body(*refs):
        gr, lr = refs[:nt], refs[nt]
        w, m, v = refs[nt + 1:2 * nt + 1], refs[2 * nt + 1:3 * nt + 1], refs[3 * nt + 1:4 * nt + 1]
        outs = refs[4 * nt + 1:]
        outs[0][...] = _sum_devices(lr)
        for t in range(nt):
            g_ref, d_ref, nm_ref, nv_ref = outs[1 + 4 * t:5 + 4 * t]
            g = _sum_devices(gr[t])
            g_ref[...] = g
            d_ref[...], nm_ref[...], nv_ref[...] = _adam_math(g, w[t][...], m[t][...], v[t][...])

    out_shape = [jax.ShapeDtypeStruct((1, LANE), F32)]
    for t in range(nt):
        out_shape += [jax.ShapeDtypeStruct(ws[t].shape, F32)] * 4
    outs = pl.pallas_call(body, name=name, out_shape=out_shape, compiler_params=_cparams())(
        *grecvs, loss_recv, *ws, *ms, *vs)
    return outs[0], [outs[1 + 4 * t:5 + 4 * t] for t in range(nt)]


def _shard_cols(g):
    rows, cols = g.shape
    return g.reshape(rows, N_DEV, cols // N_DEV).transpose(1, 0, 2)


def _unshard_cols(blocks):
    return blocks.transpose(1, 0, 2).reshape(blocks.shape[1], -1)


def _heads(t, n, d):
    return t.reshape(t.shape[0], n, d).transpose(1, 0, 2)


def _unheads(t):
    return t.transpose(1, 0, 2).reshape(t.shape[1], -1)


def kernel(x, w_in, gate_up, gate_bias, gla_norm_g, w_out, ln1_g, ln1_b, w_up, conv_w, conv_b, w_down, ln2_g, ln2_b, loss_target, m_w_in, m_gate_up, m_gate_bias, m_gla_norm_g, m_w_out, m_ln1_g, m_ln1_b, m_w_up, m_conv_w, m_conv_b, m_w_down, m_ln2_g, m_ln2_b, v_w_in, v_gate_up, v_gate_bias, v_gla_norm_g, v_w_out, v_ln1_g, v_ln1_b, v_w_up, v_conv_w, v_conv_b, v_w_down, v_ln2_g, v_ln2_b):
    S, D = x.shape[1], x.shape[2]
    x2, tgt = x[0], loss_target[0]

    gathered = _exchange([(w_in[0].astype(MXU_DTYPE), False), (gate_up[0], False), (conv_w[0], False)],
                         "gather_w_in")
    w_in_f = _unshard_cols(gathered[0])
    gate_up_f = _unshard_cols(gathered[1])
    conv_w_f = _unshard_cols(gathered[2])
    w_in_pad = jnp.pad(w_in_f, ((0, 0), (0, IN_PAD - IN_WIDTH)))
    gate_up_h = _heads(jnp.pad(gate_up_f, ((0, LANE - GLA_GATE_RANK), (0, 0))), GLA_HEADS, GLA_KEY_DIM)
    gate_bias_h = gate_bias.reshape(GLA_HEADS, 1, GLA_KEY_DIM)

    w_qkv = jnp.concatenate([w_in_f[:, :OFF_SBK] * (SB_HEAD_DIM ** -0.5), w_in_f[:, OFF_SBK:OFF_GQ]], axis=1)
    qkv = _matmul(x2, w_qkv, "nn", MXU_DTYPE, "proj_sb")
    proj = _matmul(x2, w_in_pad[:, OFF_GQ:], "nn", F32, "proj_gla")
    sb_o, sb_tot, g_out, g_up, g_down = _sb_fwd(
        qkv, "sb_fwd", ride=[(w_out[0].astype(MXU_DTYPE), False), (w_up[0].astype(MXU_DTYPE), False),
                             (w_down[0].astype(MXU_DTYPE), False)])
    w_out_f = g_out.reshape(-1, D)
    w_up_f = _unshard_cols(g_up)
    w_down_f = g_down.reshape(-1, D)
    gq = _heads(proj[:, :OFF_GK - OFF_GQ], GLA_HEADS, GLA_KEY_DIM)
    gk = _heads(proj[:, OFF_GK - OFF_GQ:OFF_GV - OFF_GQ], GLA_HEADS, GLA_KEY_DIM)
    gla_o, prev = _gla_fwd(gq, gk, proj, gate_up_h, gate_bias_h, gla_norm_g, "gla_fwd")
    cat = jnp.concatenate([sb_o, gla_o], axis=1)
    r1 = _matmul(cat, w_out_f, "nn", F32, "mix", res=x2, res_scale=DN_ALPHA)
    h = _ln_fwd(r1, ln1_g, ln1_b, "ln1")
    u0 = _matmul(h, w_up_f, "nn", F32, "ffn_up")
    p = _conv_gelu_fwd(u0, conv_w_f, conv_b, "conv_gelu")
    r2 = _matmul(p, w_down_f, "nn", F32, "ffn_down", res=h, res_scale=DN_ALPHA)
    d_r2, loss_p, g_ln2_g, g_ln2_b = _ln_loss_bwd(r2, tgt, ln2_g, ln2_b, "ln2_loss")

    d_p = _matmul(d_r2, w_down_f, "nt", MXU_DTYPE, "d_ffn_act")
    g_w_down = _matmul(p, d_r2, "tn", F32, "grad_w_down")
    d_u0, g_conv_w, g_conv_b = _conv_gelu_bwd(u0, d_p, conv_w_f, conv_b, "conv_gelu_bwd")
    g_w_up = _matmul(h, d_u0, "tn", F32, "grad_w_up")
    d_h = _matmul(d_u0, w_up_f, "nt", F32, "d_h", res=d_r2, res_scale=DN_ALPHA)
    d_r1, g_ln1_g, g_ln1_b = _ln_bwd(r1, d_h, ln1_g, "ln1_bwd")
    g_w_out = _matmul(cat, d_r1, "tn", F32, "grad_w_out")
    d_cat = _matmul(d_r1, w_out_f, "nt", F32, "d_cat")
    (d_gq, d_gk, d_gv, d_gg, d_ga_pad, g_gu_h, g_gb_h, g_gnorm) = _gla_bwd(
        gq, gk, proj, gate_up_h, gate_bias_h, gla_norm_g, prev, d_cat, "gla_bwd")
    g_gate_up = _unheads(g_gu_h[:, :GLA_GATE_RANK, :])
    g_gate_bias = g_gb_h.reshape(1, -1)
    small_g = [g_gate_bias, g_gnorm, g_ln1_g, g_ln1_b, g_conv_b, g_ln2_g, g_ln2_b]
    d_sq, d_sk, d_sv, *recv_rest = _sb_bwd(
        qkv, d_cat, sb_tot, "sb_bwd",
        ride=[(g_w_out.reshape(N_DEV, -1, D).astype(BF16), True), (_shard_cols(g_w_up).astype(BF16), True),
              (g_w_down.reshape(N_DEV, -1, D).astype(BF16), True), (_shard_cols(g_gate_up), True),
              (_shard_cols(g_conv_w), True)] + [(t, False) for t in small_g] + [(loss_p, False)])
    d_proj = jnp.concatenate([d_sq, d_sk, d_sv, _unheads(d_gq), _unheads(d_gk), d_gv, d_gg,
                              d_ga_pad.astype(MXU_DTYPE)], axis=1)
    g_w_in = _matmul(x2, d_proj, "tn", F32, "grad_w_in")[:, :IN_WIDTH]
    d_x, recv_in = _matmul(d_proj, w_in_pad, "nt", F32, "d_x", res=d_r1, res_scale=DN_ALPHA,
                           ride=[(_shard_cols(g_w_in).astype(BF16), True)])

    recv = [recv_in] + recv_rest[:5]
    sharded = [(w_in, m_w_in, v_w_in), (w_out, m_w_out, v_w_out), (w_up, m_w_up, v_w_up),
               (w_down, m_w_down, v_w_down), (gate_up, m_gate_up, v_gate_up), (conv_w, m_conv_w, v_conv_w)]
    upd = [_adamw(recv[n], w[0], m[0], v[0], "adamw_%d" % n) for n, (w, m, v) in enumerate(sharded)]
    loss_row, small = _adamw_replicated(
        recv_rest[5:12], recv_rest[12], [gate_bias, gla_norm_g, ln1_g, ln1_b, conv_b, ln2_g, ln2_b],
        [m_gate_bias, m_gla_norm_g, m_ln1_g, m_ln1_b, m_conv_b, m_ln2_g, m_ln2_b],
        [v_gate_bias, v_gla_norm_g, v_ln1_g, v_ln1_b, v_conv_b, v_ln2_g, v_ln2_b], "adamw_replicated")
    outs = []
    for kind in range(4):
        b_w_in, b_w_out, b_w_up, b_w_down, b_gate_up, b_conv_w = [u[kind][None] for u in upd]
        s_gb, s_gn, s_l1g, s_l1b, s_cb, s_l2g, s_l2b = [t[kind] for t in small]
        outs += [b_w_in, b_gate_up, s_gb, s_gn, b_w_out, s_l1g, s_l1b, b_w_up, b_conv_w, s_cb, b_w_down,
                 s_l2g, s_l2b]
    return (loss_row[0, 0], d_x[None], *outs)
```

```python
import math

import jax
import jax.numpy as jnp
from jax import lax
from jax.experimental import pallas as pl
from jax.experimental.pallas import tpu as pltpu

F32 = jnp.float32
BF16 = jnp.bfloat16
MXU_DTYPE = jnp.bfloat16

N_DEV = 8
D_MODEL = 1024
SB_WIDTH = 512
SB_HEADS = 8
SB_HEAD_DIM = 64
GLA_HEADS = 4
GLA_KEY_DIM = 64
GLA_VAL_DIM = 128
GLA_WIDTH = 512
GLA_GATE_RANK = 16
GLA_TAU = 16.0
CHUNK = 64
D_FF = 2816
CONV_WIDTH = 3
LN_EPS = 1e-5
RMS_EPS = 1e-6
DN_ALPHA = 2.0 ** 0.25
IN_WIDTH = 3088
LANE = 128
IN_PAD = 3200
OFF_SBQ, OFF_SBK, OFF_SBV = 0, 512, 1024
OFF_GQ, OFF_GK, OFF_GV, OFF_GG, OFF_GA = 1536, 1792, 2048, 2560, 3072
GLA_PAD = IN_PAD - OFF_GQ

ADAM_LR = 0.001
ADAM_B1 = 0.9
ADAM_B2 = 0.999
ADAM_EPS = 1e-08
ADAM_WD = 0.01
ADAM_STEP = 10

VMEM_LIMIT = 48 * 1024 * 1024
MESH_ID = pl.DeviceIdType.MESH


def _cparams(sem=None, **kw):
    return pltpu.CompilerParams(dimension_semantics=sem, vmem_limit_bytes=VMEM_LIMIT, **kw)


def _dot(a, b, dims):
    ca, cb = {"nn": (1, 0), "nt": (1, 1), "tn": (0, 0)}[dims]
    return lax.dot_general(a.astype(MXU_DTYPE), b.astype(MXU_DTYPE), (((ca,), (cb,)), ((), ())),
                           preferred_element_type=F32)


def _dot_split(a, b, dims):
    assert dims == "nn"
    hi = a.astype(BF16)
    lo = (a - hi.astype(F32)).astype(BF16)
    return lax.dot_general(jnp.concatenate([hi, lo], axis=1), jnp.concatenate([b, b], axis=0),
                           (((1,), (0,)), ((), ())), preferred_element_type=F32)


def _pick(dim, prefs):
    for p in prefs:
        if dim % p == 0:
            return p
    return dim


def _matmul(a, b, dims, out_dtype, name, res=None, res_scale=1.0, ride=()):
    if dims == "nn":
        (M, K), (_, N) = a.shape, b.shape
    elif dims == "nt":
        (M, K), (N, _) = a.shape, b.shape
    else:
        (K, M), (_, N) = a.shape, b.shape
    tm = _pick(M, (1024, 1408, 512, 256, 128))
    tn = _pick(N, (1408, 1024, 640, 512))
    if tn == N and N > 2048:
        tn = _pick(N, (256, 128))
    tk = _pick(K, (1024, 1408, 640, 512, 256, 128))
    nk = K // tk
    grid = (M // tm, N // tn, nk)
    if dims == "tn":
        a_spec = pl.BlockSpec((tk, tm), lambda i, j, k: (k, i))
    else:
        a_spec = pl.BlockSpec((tm, tk), lambda i, j, k: (i, k))
    if dims == "nt":
        b_spec = pl.BlockSpec((tn, tk), lambda i, j, k: (j, k))
    else:
        b_spec = pl.BlockSpec((tk, tn), lambda i, j, k: (k, j))
    o_spec = pl.BlockSpec((tm, tn), lambda i, j, k: (i, j))
    in_specs = [a_spec, b_spec]
    args = [a, b]
    if res is not None:
        in_specs.append(o_spec)
        args.append(res)

    def body(*refs):
        if res is not None:
            a_ref, b_ref, r_ref, o_ref, acc_ref = refs
        else:
            a_ref, b_ref, o_ref, acc_ref = refs
            r_ref = None
        k = pl.program_id(2)
        part = _dot(a_ref[...], b_ref[...], dims)

        def finish(total):
            if r_ref is not None:
                total = total + res_scale * r_ref[...]
            o_ref[...] = total.astype(o_ref.dtype)

        if nk == 1:
            finish(part)
        else:
            @pl.when(k == 0)
            def _():
                acc_ref[...] = part

            @pl.when(jnp.logical_and(k > 0, k < nk - 1))
            def _():
                acc_ref[...] += part

            @pl.when(k == nk - 1)
            def _():
                finish(acc_ref[...] + part)

    out_sds = jax.ShapeDtypeStruct((M, N), out_dtype)
    acc = pltpu.VMEM((tm, tn), F32)
    if not ride:
        return pl.pallas_call(
            body, name=name, grid=grid, in_specs=in_specs, out_specs=o_spec, out_shape=out_sds,
            scratch_shapes=[acc], compiler_params=_cparams(("parallel", "parallel", "arbitrary")),
        )(*args)
    x_args, x_in, x_out, x_shapes, x_sems, wrap = _riding_exchange(list(ride), grid)
    return pl.pallas_call(
        wrap(body, len(args), 1), name=name, grid=grid, in_specs=in_specs + x_in, out_specs=[o_spec] + x_out,
        out_shape=[out_sds] + x_shapes, scratch_shapes=[acc] + x_sems,
        compiler_params=_cparams(("arbitrary",) * 3, has_side_effects=True),
    )(*args, *x_args)


LN_ROWS = 256


def _ln_stats(r):
    mu = jnp.mean(r, axis=-1, keepdims=True)
    xc = r - mu
    var = jnp.mean(xc * xc, axis=-1, keepdims=True)
    return xc * lax.rsqrt(var + LN_EPS)


def _ln_fwd(r, g, b, name):
    S, D = r.shape

    def body(r_ref, g_ref, b_ref, h_ref):
        h_ref[...] = _ln_stats(r_ref[...]) * g_ref[...] + b_ref[...]

    row = pl.BlockSpec((LN_ROWS, D), lambda i: (i, 0))
    vec = pl.BlockSpec((1, D), lambda i: (0, 0))
    return pl.pallas_call(
        body, name=name, grid=(S // LN_ROWS,), in_specs=[row, vec, vec], out_specs=row,
        out_shape=jax.ShapeDtypeStruct((S, D), F32),
        compiler_params=_cparams(("parallel",)),
    )(r, g, b)


def _ln_bwd_core(xhat, dy, g):
    dxh = dy * g
    m1 = jnp.mean(dxh, axis=-1, keepdims=True)
    m2 = jnp.mean(dxh * xhat, axis=-1, keepdims=True)
    return dxh - m1 - xhat * m2


def _ln_bwd(r, dy, g, name):
    S, D = r.shape

    def body(r_ref, dy_ref, g_ref, dr_ref, gg_ref, gb_ref):
        x = r_ref[...]
        mu = jnp.mean(x, axis=-1, keepdims=True)
        xc = x - mu
        rstd = lax.rsqrt(jnp.mean(xc * xc, axis=-1, keepdims=True) + LN_EPS)
        xhat = xc * rstd
        dy = dy_ref[...]
        dr_ref[...] = rstd * _ln_bwd_core(xhat, dy, g_ref[...])

        @pl.when(pl.program_id(0) == 0)
        def _():
            gg_ref[...] = jnp.zeros_like(gg_ref)
            gb_ref[...] = jnp.zeros_like(gb_ref)

        gg_ref[...] += jnp.sum(dy * xhat, axis=0, keepdims=True)
        gb_ref[...] += jnp.sum(dy, axis=0, keepdims=True)

    row = pl.BlockSpec((LN_ROWS, D), lambda i: (i, 0))
    vec = pl.BlockSpec((1, D), lambda i: (0, 0))
    return pl.pallas_call(
        body, name=name, grid=(S // LN_ROWS,), in_specs=[row, row, vec], out_specs=[row, vec, vec],
        out_shape=[jax.ShapeDtypeStruct((S, D), F32), jax.ShapeDtypeStruct((1, D), F32),
                   jax.ShapeDtypeStruct((1, D), F32)],
        compiler_params=_cparams(("arbitrary",)),
    )(r, dy, g)


def _ln_loss_bwd(r, target, g, b, name):
    S, D = r.shape

    def body(r_ref, t_ref, g_ref, b_ref, dr_ref, loss_ref, gg_ref, gb_ref):
        x = r_ref[...]
        mu = jnp.mean(x, axis=-1, keepdims=True)
        xc = x - mu
        rstd = lax.rsqrt(jnp.mean(xc * xc, axis=-1, keepdims=True) + LN_EPS)
        xhat = xc * rstd
        y = xhat * g_ref[...] + b_ref[...]
        err = y - t_ref[...]
        dy = err * (1.0 / D)
        dr_ref[...] = rstd * _ln_bwd_core(xhat, dy, g_ref[...])

        @pl.when(pl.program_id(0) == 0)
        def _():
            loss_ref[...] = jnp.zeros_like(loss_ref)
            gg_ref[...] = jnp.zeros_like(gg_ref)
            gb_ref[...] = jnp.zeros_like(gb_ref)

        per_row = jnp.sum(err * err, axis=-1, keepdims=True) * (0.5 / D)
        loss_ref[...] += jnp.broadcast_to(jnp.sum(per_row, axis=0, keepdims=True), loss_ref.shape)
        gg_ref[...] += jnp.sum(dy * xhat, axis=0, keepdims=True)
        gb_ref[...] += jnp.sum(dy, axis=0, keepdims=True)

    row = pl.BlockSpec((LN_ROWS, D), lambda i: (i, 0))
    vec = pl.BlockSpec((1, D), lambda i: (0, 0))
    lvec = pl.BlockSpec((1, LANE), lambda i: (0, 0))
    return pl.pallas_call(
        body, name=name, grid=(S // LN_ROWS,), in_specs=[row, row, vec, vec],
        out_specs=[row, lvec, vec, vec],
        out_shape=[jax.ShapeDtypeStruct((S, D), F32), jax.ShapeDtypeStruct((1, LANE), F32),
                   jax.ShapeDtypeStruct((1, D), F32), jax.ShapeDtypeStruct((1, D), F32)],
        compiler_params=_cparams(("arbitrary",)),
    )(r, target, g, b)


CONV_COLS = 256
CONV_ROWS = 256
HALO = 8
INV_SQRT2 = 1.0 / math.sqrt(2.0)
INV_SQRT2PI = 1.0 / math.sqrt(2.0 * math.pi)


def _gelu(x):
    return 0.5 * x * (1.0 + lax.erf(x * INV_SQRT2))


def _gelu_and_grad(x):
    cdf = 0.5 * (1.0 + lax.erf(x * INV_SQRT2))
    return x * cdf, cdf + x * jnp.exp(-0.5 * x * x) * INV_SQRT2PI


def _conv_rows(ext, w_ref, b_ref, n):
    total = ext.shape[0]
    s1 = pltpu.roll(ext, 1, 0)
    s2 = pltpu.roll(ext, 2, 0)
    u = w_ref[2:3, :] * ext + w_ref[1:2, :] * s1 + w_ref[0:1, :] * s2 + b_ref[...]
    return u[HALO:total], s1[HALO:total], s2[HALO:total]


def _conv_gelu_fwd(u0, conv_w, conv_b, name):
    S, C2 = u0.shape
    F = C2 // 2
    ncb = F // CONV_COLS
    nrc = S // CONV_ROWS

    def body(ua_ref, uc_ref, wa_ref, wc_ref, ba_ref, bc_ref, p_ref):
        def chunk(ci, _):
            r0 = pl.multiple_of(ci * CONV_ROWS, CONV_ROWS)
            p0 = pl.multiple_of(jnp.maximum(r0 - HALO, 0), HALO)
            keep = (ci > 0).astype(F32)

            def load(ref):
                prev = ref[pl.ds(p0, HALO), :] * keep
                return jnp.concatenate([prev, ref[pl.ds(r0, CONV_ROWS), :]], axis=0)

            a, _, _ = _conv_rows(load(ua_ref), wa_ref, ba_ref, CONV_ROWS)
            c, _, _ = _conv_rows(load(uc_ref), wc_ref, bc_ref, CONV_ROWS)
            p_ref[pl.ds(r0, CONV_ROWS), :] = (_gelu(a) * c).astype(p_ref.dtype)
            return 0

        lax.fori_loop(0, nrc, chunk, 0)

    col_a = pl.BlockSpec((S, CONV_COLS), lambda j: (0, j))
    col_c = pl.BlockSpec((S, CONV_COLS), lambda j: (0, j + ncb))
    w_a = pl.BlockSpec((CONV_WIDTH, CONV_COLS), lambda j: (0, j))
    w_c = pl.BlockSpec((CONV_WIDTH, CONV_COLS), lambda j: (0, j + ncb))
    b_a = pl.BlockSpec((1, CONV_COLS), lambda j: (0, j))
    b_c = pl.BlockSpec((1, CONV_COLS), lambda j: (0, j + ncb))
    return pl.pallas_call(
        body, name=name, grid=(ncb,), in_specs=[col_a, col_c, w_a, w_c, b_a, b_c], out_specs=col_a,
        out_shape=jax.ShapeDtypeStruct((S, F), MXU_DTYPE),
        compiler_params=_cparams(("parallel",)),
    )(u0, u0, conv_w, conv_w, conv_b, conv_b)


def _conv_gelu_bwd(u0, dp, conv_w, conv_b, name):
    S, C2 = u0.shape
    F = C2 // 2
    ncb = F // CONV_COLS
    nrc = S // CONV_ROWS
    EXT = CONV_ROWS + HALO

    def body(ua_ref, uc_ref, dp_ref, wa_ref, wc_ref, ba_ref, bc_ref,
             da_ref, dc_ref, gwa_ref, gwc_ref, gba_ref, gbc_ref):
        gwa_ref[...] = jnp.zeros_like(gwa_ref)
        gwc_ref[...] = jnp.zeros_like(gwc_ref)
        gba_ref[...] = jnp.zeros_like(gba_ref)
        gbc_ref[...] = jnp.zeros_like(gbc_ref)
        rid = lax.broadcasted_iota(jnp.int32, (EXT, CONV_COLS), 0)

        def chunk(ci, _):
            r0 = pl.multiple_of(ci * CONV_ROWS, CONV_ROWS)
            p0 = pl.multiple_of(jnp.maximum(r0 - HALO, 0), HALO)
            n0 = pl.multiple_of(jnp.minimum(r0 + CONV_ROWS, S - HALO), HALO)
            keep_prev = (ci > 0).astype(F32)
            keep_next = (ci < nrc - 1).astype(F32)

            def load(ref):
                return jnp.concatenate([ref[pl.ds(p0, HALO), :] * keep_prev,
                                        ref[pl.ds(r0, CONV_ROWS), :],
                                        ref[pl.ds(n0, HALO), :] * keep_next], axis=0)

            ext_a = load(ua_ref)
            ext_c = load(uc_ref)
            a, a1, a2 = _conv_rows(ext_a, wa_ref, ba_ref, EXT)
            c, c1, c2 = _conv_rows(ext_c, wc_ref, bc_ref, EXT)
            a0 = ext_a[HALO:HALO + EXT]
            c0 = ext_c[HALO:HALO + EXT]
            dpe = jnp.concatenate([dp_ref[pl.ds(r0, CONV_ROWS), :].astype(F32),
                                   dp_ref[pl.ds(n0, HALO), :].astype(F32) * keep_next], axis=0)
            gelu_a, slope_a = _gelu_and_grad(a)
            d_a = dpe * c * slope_a
            d_c = dpe * gelu_a
            own = rid < CONV_ROWS

            def back(d_u, w_ref, x0, x1, x2, d_ref, gw_ref, gb_ref):
                d_u0 = (w_ref[2:3, :] * d_u + w_ref[1:2, :] * pltpu.roll(d_u, EXT - 1, 0)
                        + w_ref[0:1, :] * pltpu.roll(d_u, EXT - 2, 0))
                d_ref[pl.ds(r0, CONV_ROWS), :] = d_u0[0:CONV_ROWS].astype(d_ref.dtype)
                d_own = jnp.where(own, d_u, 0.0)
                gw_ref[...] += jnp.concatenate(
                    [jnp.sum(d_own * x2, axis=0, keepdims=True),
                     jnp.sum(d_own * x1, axis=0, keepdims=True),
                     jnp.sum(d_own * x0, axis=0, keepdims=True)], axis=0)
                gb_ref[...] += jnp.sum(d_own, axis=0, keepdims=True)

            back(d_a, wa_ref, a0, a1, a2, da_ref, gwa_ref, gba_ref)
            back(d_c, wc_ref, c0, c1, c2, dc_ref, gwc_ref, gbc_ref)
            return 0

        lax.fori_loop(0, nrc, chunk, 0)

    col_a = pl.BlockSpec((S, CONV_COLS), lambda j: (0, j))
    col_c = pl.BlockSpec((S, CONV_COLS), lambda j: (0, j + ncb))
    w_a = pl.BlockSpec((CONV_WIDTH, CONV_COLS), lambda j: (0, j))
    w_c = pl.BlockSpec((CONV_WIDTH, CONV_COLS), lambda j: (0, j + ncb))
    b_a = pl.BlockSpec((1, CONV_COLS), lambda j: (0, j))
    b_c = pl.BlockSpec((1, CONV_COLS), lambda j: (0, j + ncb))
    outs = pl.pallas_call(
        body, name=name, grid=(ncb,),
        in_specs=[col_a, col_c, col_a, w_a, w_c, b_a, b_c],
        out_specs=[col_a, col_a, w_a, w_a, b_a, b_a],
        out_shape=[jax.ShapeDtypeStruct((S, F), MXU_DTYPE), jax.ShapeDtypeStruct((S, F), MXU_DTYPE),
                   jax.ShapeDtypeStruct((CONV_WIDTH, F), F32), jax.ShapeDtypeStruct((CONV_WIDTH, F), F32),
                   jax.ShapeDtypeStruct((1, F), F32), jax.ShapeDtypeStruct((1, F), F32)],
        compiler_params=_cparams(("parallel",)),
    )(u0, u0, dp, conv_w, conv_w, conv_b, conv_b)
    da, dc, gwa, gwc, gba, gbc = outs
    return (jnp.concatenate([da, dc], axis=1), jnp.concatenate([gwa, gwc], axis=1),
            jnp.concatenate([gba, gbc], axis=1))


SB_TK = 128
SB_TQ_FWD = 1024
SB_TQ_BWD = 1024


def _softplus(z):
    return jnp.maximum(z, 0.0) + jnp.log(1.0 + jnp.exp(-jnp.abs(z)))


def _tri_ones(after):
    r = lax.broadcasted_iota(jnp.int32, (SB_TK, 2 * SB_TK), 0)
    c = lax.broadcasted_iota(jnp.int32, (SB_TK, 2 * SB_TK), 1)
    tri = (r > c) if after else (r < c)
    return jnp.where(c >= SB_TK, 1.0, jnp.where(tri, 1.0, 0.0)).astype(BF16)


def _sb_block_specs(S, TQ):
    NP = SB_WIDTH // LANE
    return [pl.BlockSpec((TQ, LANE), lambda p, i: (i, p)),
            pl.BlockSpec((S, LANE), lambda p, i: (0, NP + p)),
            pl.BlockSpec((S, LANE), lambda p, i: (0, 2 * NP + p))]


def _sb_fwd(qkv, name, ride=()):
    S = qkv.shape[0]
    TK = SB_TK
    TQ = min(SB_TQ_FWD, S)
    R = TQ // TK
    NP = SB_WIDTH // LANE
    grid = (NP, S // TQ)

    def body(q_ref, k_ref, v_ref, o_ref, t_ref):
        i = pl.program_id(1)
        row = lax.broadcasted_iota(jnp.int32, (TQ, TK), 0)
        col = lax.broadcasted_iota(jnp.int32, (TQ, TK), 1)
        after_ones = _tri_ones(True)
        first = lax.broadcasted_iota(jnp.int32, (TK, LANE), 1) < SB_HEAD_DIM

        def block(j, carry, r0):
            k0 = pl.multiple_of(j * TK, TK)
            masked = r0 is not None
            r0 = r0 or 0
            kab = k_ref[pl.ds(k0, TK), :]
            vab = v_ref[pl.ds(k0, TK), :]
            none = jnp.zeros_like(kab)
            k2 = jnp.concatenate([jnp.where(first, kab, none), jnp.where(first, none, kab)], axis=0)
            v2 = jnp.concatenate([jnp.where(first, vab, none), jnp.where(first, none, vab)], axis=0)
            z2 = _dot(q_ref[r0:, :], k2, "nt")
            tails, ws = [], []
            for hh in range(2):
                tail = carry[1 + hh]
                z = z2[:, hh * TK:(hh + 1) * TK]
                sp = _softplus(z)
                if masked:
                    strict = col[r0:] < row[r0:] - r0
                    sp = jnp.where(strict, sp, 0.0)
                cs = _dot_split(sp, after_ones, "nn")
                w = jnp.exp(z - sp - cs[:, :TK] - tail[r0:])
                if masked:
                    w = jnp.where(strict, w, 0.0)
                ws.append(w.astype(MXU_DTYPE))
                tot = cs[:, TK:]
                if r0:
                    tot = jnp.concatenate([jnp.zeros((r0, TK), F32), tot], axis=0)
                tails.append(tail + tot)
            acc_r = carry[0][r0:] + _dot(jnp.concatenate(ws, axis=1), v2, "nn")
            acc = jnp.concatenate([carry[0][:r0], acc_r], axis=0) if r0 else acc_r
            return acc, tails[0], tails[1]

        carry = (jnp.zeros((TQ, LANE), F32), jnp.zeros((TQ, TK), F32), jnp.zeros((TQ, TK), F32))
        for u in reversed(range(R)):
            carry = block(R * i + u, carry, u * TK)
        carry = lax.fori_loop(0, R * i, lambda n, c: block(R * i - 1 - n, c, None), carry)
        o_ref[...] = carry[0].astype(o_ref.dtype)
        t_ref[0] = carry[1]
        t_ref[1] = carry[2]

    ospec = pl.BlockSpec((TQ, LANE), lambda p, i: (i, p))
    tspec = pl.BlockSpec((2, TQ, TK), lambda p, i: (p, i, 0))
    x_args, x_in, x_out, x_shapes, x_sems, wrap = _riding_exchange(list(ride), grid)
    return pl.pallas_call(
        wrap(body, 3, 2) if ride else body, name=name, grid=grid,
        in_specs=_sb_block_specs(S, TQ) + (x_in if ride else []),
        out_specs=[ospec, tspec] + (x_out if ride else []),
        out_shape=[jax.ShapeDtypeStruct((S, SB_WIDTH), MXU_DTYPE), jax.ShapeDtypeStruct((SB_HEADS, S, TK), F32)]
        + (x_shapes if ride else []),
        scratch_shapes=x_sems if ride else [],
        compiler_params=_cparams(("arbitrary", "arbitrary"), has_side_effects=bool(ride)),
    )(qkv, qkv, qkv, *(x_args if ride else []))


def _sb_bwd(qkv, d_cat, total, name, ride=()):
    S = qkv.shape[0]
    TK = SB_TK
    TQ = min(SB_TQ_BWD, S)
    R = TQ // TK
    NP = SB_WIDTH // LANE
    grid = (NP, S // TQ)
    scale = SB_HEAD_DIM ** -0.5

    def body(q_ref, k_ref, v_ref, do_ref, t_ref, dq_ref, dk_ref, dv_ref, dk_acc, dv_acc):
        i = pl.program_id(1)

        @pl.when(i == 0)
        def _():
            dk_acc[...] = jnp.zeros_like(dk_acc)
            dv_acc[...] = jnp.zeros_like(dv_acc)

        row = lax.broadcasted_iota(jnp.int32, (TQ, TK), 0)
        col = lax.broadcasted_iota(jnp.int32, (TQ, TK), 1)
        after_ones = _tri_ones(True)
        before_ones = _tri_ones(False)
        first = lax.broadcasted_iota(jnp.int32, (TK, LANE), 1) < SB_HEAD_DIM
        first_q = lax.broadcasted_iota(jnp.int32, (TQ, LANE), 1) < SB_HEAD_DIM
        qab = q_ref[...]
        doab = do_ref[...].astype(MXU_DTYPE)
        qdo = jnp.concatenate([qab, doab], axis=1)
        none_q = jnp.zeros_like(qab)
        q_h = [jnp.where(first_q, qab, none_q), jnp.where(first_q, none_q, qab)]
        do_h = [jnp.where(first_q, doab, none_q), jnp.where(first_q, none_q, doab)]

        def block(j, carry, r0):
            k0 = pl.multiple_of(j * TK, TK)
            masked = r0 is not None
            r0 = r0 or 0
            kab = k_ref[pl.ds(k0, TK), :]
            vab = v_ref[pl.ds(k0, TK), :]
            none = jnp.zeros_like(kab)
            k_h = [jnp.where(first, kab, none), jnp.where(first, none, kab)]
            v_h = [jnp.where(first, vab, none), jnp.where(first, none, vab)]
            dk_blk = dv_blk = None
            sums, dzs = [], []
            for hh in range(2):
                seen, gsum = carry[1 + 2 * hh], carry[2 + 2 * hh]
                kv = jnp.concatenate([jnp.concatenate([k_h[hh], none], axis=1),
                                      jnp.concatenate([none, v_h[hh]], axis=1)], axis=0)
                zdw = _dot(qdo[r0:], kv, "nt")
                z = zdw[:, :TK]
                sp = _softplus(z)
                logsig = z - sp
                if masked:
                    strict = col[r0:] < row[r0:] - r0
                    sp = jnp.where(strict, sp, 0.0)
                cs = _dot_split(sp, after_ones, "nn")
                seen_r = seen[r0:] + cs[:, TK:]
                w = jnp.exp(logsig - cs[:, :TK] - (t_ref[hh, r0:, :] - seen_r))
                if masked:
                    w = jnp.where(strict, w, 0.0)
                g = w * zdw[:, TK:]
                cg = _dot_split(g, before_ones, "nn")
                dz = g - jnp.exp(logsig) * (g + cg[:, :TK] + gsum[r0:])
                if masked:
                    dz = jnp.where(strict, dz, 0.0)
                dzb = dz.astype(MXU_DTYPE)
                dzs.append(dzb)
                dkc = _dot(dzb, q_h[hh][r0:], "tn")
                dvc = _dot(w, do_h[hh][r0:], "tn")
                dk_blk = dkc if dk_blk is None else dk_blk + dkc
                dv_blk = dvc if dv_blk is None else dv_blk + dvc
                gsum_r = gsum[r0:] + cg[:, TK:]
                if r0:
                    seen_r = jnp.concatenate([seen[:r0], seen_r], axis=0)
                    gsum_r = jnp.concatenate([gsum[:r0], gsum_r], axis=0)
                sums += [seen_r, gsum_r]
            dk_acc[pl.ds(k0, TK), :] += dk_blk
            dv_acc[pl.ds(k0, TK), :] += dv_blk
            dq_r = carry[0][r0:] + _dot(jnp.concatenate(dzs, axis=1), jnp.concatenate(k_h, axis=0), "nn")
            dq = jnp.concatenate([carry[0][:r0], dq_r], axis=0) if r0 else dq_r
            return (dq, *sums)

        zero = jnp.zeros((TQ, TK), F32)
        carry = (jnp.zeros((TQ, LANE), F32), zero, zero, zero, zero)
        carry = lax.fori_loop(0, R * i, lambda j, c: block(j, c, None), carry)
        for u in range(R):
            carry = block(R * i + u, carry, u * TK)
        dq_ref[...] = (carry[0] * scale).astype(dq_ref.dtype)

        @pl.when(i == S // TQ - 1)
        def _():
            dk_ref[...] = dk_acc[...].astype(dk_ref.dtype)
            dv_ref[...] = dv_acc[...].astype(dv_ref.dtype)

    qspec = pl.BlockSpec((TQ, LANE), lambda p, i: (i, p))
    full = pl.BlockSpec((S, LANE), lambda p, i: (0, p))
    tspec = pl.BlockSpec((2, TQ, TK), lambda p, i: (p, i, 0))
    sds = jax.ShapeDtypeStruct((S, SB_WIDTH), MXU_DTYPE)
    accs = [pltpu.VMEM((S, LANE), F32), pltpu.VMEM((S, LANE), F32)]
    x_args, x_in, x_out, x_shapes, x_sems, wrap = _riding_exchange(list(ride), grid)
    return pl.pallas_call(
        wrap(body, 5, 3) if ride else body, name=name, grid=grid,
        in_specs=_sb_block_specs(S, TQ) + [qspec, tspec] + (x_in if ride else []),
        out_specs=[qspec, full, full] + (x_out if ride else []),
        out_shape=[sds, sds, sds] + (x_shapes if ride else []),
        scratch_shapes=accs + (x_sems if ride else []),
        compiler_params=_cparams(("arbitrary", "arbitrary"), has_side_effects=bool(ride)),
    )(qkv, qkv, qkv, d_cat, total, *(x_args if ride else []))


GLA_ROWS = 1024
GLA_GROUP_FWD = 8
GLA_GROUP_BWD = 4
GLA_PAIR_K = 2 * GLA_KEY_DIM
GLA_PAIR_V = 2 * GLA_VAL_DIM


def _log_sigmoid(x):
    return -_softplus(-x)


def _dot_split_lhs01(m01, x):
    hi = x.astype(BF16)
    lo = (x - hi.astype(F32)).astype(BF16)
    return lax.dot_general(jnp.concatenate([m01, m01], axis=1), jnp.concatenate([hi, lo], axis=0),
                           (((1,), (0,)), ((), ())), preferred_element_type=F32)


def _dot_split_tn(x, m01):
    hi = x.astype(BF16)
    lo = (x - hi.astype(F32)).astype(BF16)
    return lax.dot_general(jnp.concatenate([hi, lo], axis=0), jnp.concatenate([m01, m01], axis=0),
                           (((0,), (0,)), ((), ())), preferred_element_type=F32)


def _dot_split_nt01(m01, x):
    hi = x.astype(BF16)
    lo = (x - hi.astype(F32)).astype(BF16)
    return lax.dot_general(jnp.concatenate([m01, m01], axis=1), jnp.concatenate([hi, lo], axis=1),
                           (((1,), (1,)), ((), ())), preferred_element_type=F32)


def _gla_consts():
    C = CHUNK
    row = lax.broadcasted_iota(jnp.int32, (C, C), 0)
    col = lax.broadcasted_iota(jnp.int32, (C, C), 1)
    first = lax.broadcasted_iota(jnp.int32, (C, GLA_PAIR_K), 1) < GLA_KEY_DIM
    r = lax.broadcasted_iota(jnp.int32, (GLA_PAIR_K, GLA_PAIR_V), 0)
    c = lax.broadcasted_iota(jnp.int32, (GLA_PAIR_K, GLA_PAIR_V), 1)
    own = (r < GLA_KEY_DIM) == (c < GLA_VAL_DIM)
    rowk = lax.broadcasted_iota(jnp.int32, (C, GLA_PAIR_K), 0)
    return dict(row=row, col=col, first=first, own=own, rowk=rowk,
                incl=jnp.where(row >= col, 1.0, 0.0).astype(BF16),
                ones=jnp.ones((C, GLA_PAIR_V), BF16))


def _pack_state(state):
    top = lax.broadcasted_iota(jnp.int32, (GLA_PAIR_K, GLA_VAL_DIM), 0) < GLA_KEY_DIM
    return jnp.where(top, state[:, :GLA_VAL_DIM], state[:, GLA_VAL_DIM:])


def _unpack_state(packed):
    top = lax.broadcasted_iota(jnp.int32, (GLA_PAIR_K, GLA_VAL_DIM), 0) < GLA_KEY_DIM
    return jnp.concatenate([jnp.where(top, packed, 0.0), jnp.where(top, 0.0, packed)], axis=1)


def _gla_chunk(qc, kc, vc, gate, k):
    C = CHUNK
    row, col, first, rowk = k["row"], k["col"], k["first"], k["rowk"]
    la = _log_sigmoid(gate) * (1.0 / GLA_TAU)
    b = _dot_split_lhs01(k["incl"], la)
    b_ref = jnp.sum(jnp.where(rowk == C // 2 - 1, b, 0.0), axis=0, keepdims=True)
    b_last = jnp.sum(la, axis=0, keepdims=True)
    qs = qc * (GLA_KEY_DIM ** -0.5)
    q_in = qs * jnp.exp(b - b_ref)
    k_in = kc * jnp.exp(b_ref - b)
    k_dec = kc * jnp.exp(b_last - b)
    q_b = qs * jnp.exp(b)
    k_in_h = [jnp.where(first, k_in, 0.0), jnp.where(first, 0.0, k_in)]
    v_h = [vc[:, :GLA_VAL_DIM], vc[:, GLA_VAL_DIM:]]
    sc = [jnp.where(row >= col, _dot(q_in, k_in_h[hh], "nt"), 0.0) for hh in range(2)]
    o_intra = jnp.concatenate([_dot(sc[hh], v_h[hh], "nn") for hh in range(2)], axis=1)
    upd = jnp.where(k["own"], _dot(k_dec, vc, "tn"), 0.0)
    dec_col = jnp.exp(_dot_split_tn(la, k["ones"]))
    return dict(la=la, b=b, b_ref=b_ref, b_last=b_last, q_in=q_in, k_in=k_in, k_dec=k_dec, q_b=q_b,
                k_in_h=k_in_h, v_h=v_h, sc=sc, o_intra=o_intra, upd=upd, dec_col=dec_col)


def _rms_gate(o, gg):
    rinv = lax.rsqrt(jnp.mean(o * o, axis=-1, keepdims=True) + RMS_EPS)
    o_n = o * rinv
    sg = 1.0 / (1.0 + jnp.exp(-gg))
    return o_n, rinv, sg


def _gla_in_specs(rows_of, RB):
    PK, PV = GLA_PAIR_K, GLA_PAIR_V
    return [pl.BlockSpec((RB, PK), lambda i, p: (rows_of(i), p)),
            pl.BlockSpec((RB, PK), lambda i, p: (rows_of(i), (OFF_GK - OFF_GQ) // PK + p)),
            pl.BlockSpec((RB, PV), lambda i, p: (rows_of(i), (OFF_GV - OFF_GQ) // PV + p)),
            pl.BlockSpec((RB, PV), lambda i, p: (rows_of(i), (OFF_GG - OFF_GQ) // PV + p)),
            pl.BlockSpec((RB, LANE), lambda i, p: (rows_of(i), (OFF_GA - OFF_GQ) // LANE)),
            pl.BlockSpec((1, LANE, PK), lambda i, p: (p, 0, 0)),
            pl.BlockSpec((1, 1, PK), lambda i, p: (p, 0, 0)),
            pl.BlockSpec((1, GLA_VAL_DIM), lambda i, p: (0, 0))]


def _gla_fwd(proj, gate_up_p, gate_bias_p, gnorm, name):
    S = proj.shape[0]
    C, RB, PK, PV, dv = CHUNK, min(GLA_ROWS, S), GLA_PAIR_K, GLA_PAIR_V, GLA_VAL_DIM
    NP = GLA_HEADS // 2
    G = GLA_GROUP_FWD
    nchunk = S // C
    ngroup = RB // (C * G)

    def body(q_ref, k_ref, v_ref, gg_ref, ga_ref, gu_ref, gb_ref, gn_ref, o_ref, prev_ref, st_ref):
        i, p = pl.program_id(0), pl.program_id(1)
        k = _gla_consts()

        @pl.when(i == 0)
        def _():
            st_ref[p] = jnp.zeros((PK, PV), F32)

        def group(gi, state):
            for u in range(G):
                ci = gi * G + u
                rows = pl.ds(pl.multiple_of(ci * C, C), C)
                gate = _dot(ga_ref[rows, :], gu_ref[0], "nn") + gb_ref[0]
                f = _gla_chunk(q_ref[rows, :], k_ref[rows, :], v_ref[rows, :], gate, k)
                prev_ref[0, ci] = _pack_state(state)
                o = f["o_intra"] + _dot(f["q_b"], state, "nn")
                state = f["dec_col"] * state + f["upd"]
                ggv = gg_ref[rows, :]
                halves = []
                for hh in range(2):
                    lanes = slice(hh * dv, (hh + 1) * dv)
                    o_n, _, sg = _rms_gate(o[:, lanes], ggv[:, lanes])
                    halves.append(o_n * gn_ref[...] * (ggv[:, lanes] * sg))
                o_ref[rows, :] = jnp.concatenate(halves, axis=1).astype(o_ref.dtype)
            return state

        st_ref[p] = lax.fori_loop(0, ngroup, group, st_ref[p])

    return pl.pallas_call(
        body, name=name, grid=(S // RB, NP), in_specs=_gla_in_specs(lambda i: i, RB),
        out_specs=[pl.BlockSpec((RB, PV), lambda i, p: (i, p)),
                   pl.BlockSpec((1, RB // C, PK, dv), lambda i, p: (p, i, 0, 0))],
        out_shape=[jax.ShapeDtypeStruct((S, GLA_WIDTH), MXU_DTYPE),
                   jax.ShapeDtypeStruct((NP, nchunk, PK, dv), F32)],
        scratch_shapes=[pltpu.VMEM((NP, PK, PV), F32)],
        compiler_params=_cparams(("arbitrary", "arbitrary")),
    )(proj, proj, proj, proj, proj, gate_up_p, gate_bias_p, gnorm)


def _gla_bwd(proj, gate_up_p, gate_bias_p, gnorm, prev, d_cat, name):
    S = proj.shape[0]
    C, RB, PK, PV, dv = CHUNK, min(GLA_ROWS, S), GLA_PAIR_K, GLA_PAIR_V, GLA_VAL_DIM
    NP = GLA_HEADS // 2
    G = GLA_GROUP_BWD
    nb = S // RB
    ngroup = RB // (C * G)

    def body(q_ref, k_ref, v_ref, gg_ref, ga_ref, gu_ref, gb_ref, gn_ref, prev_ref, do_ref,
             dq_ref, dk_ref, dv_ref, dgg_ref, dga_ref, ggu_ref, ggb_ref, ggn_ref, st_ref):
        i, p = pl.program_id(0), pl.program_id(1)
        k = _gla_consts()
        row, col, first, rowk = k["row"], k["col"], k["first"], k["rowk"]
        gu = gu_ref[0]

        @pl.when(i == 0)
        def _():
            st_ref[p] = jnp.zeros((PK, PV), F32)
            ggu_ref[p] = jnp.zeros((LANE, PK), F32)
            ggb_ref[p] = jnp.zeros((1, PK), F32)

        @pl.when(jnp.logical_and(i == 0, p == 0))
        def _():
            ggn_ref[...] = jnp.zeros_like(ggn_ref)

        @pl.when(p == 0)
        def _():
            dga_ref[...] = jnp.zeros_like(dga_ref)

        upper_incl = jnp.where(col >= row, 1.0, 0.0).astype(BF16)
        ones_8 = jnp.ones((8, PV), BF16)

        def group(gn, dstate):
            gi = ngroup - 1 - gn
            for u in reversed(range(G)):
                ci = gi * G + u
                rows = pl.ds(pl.multiple_of(ci * C, C), C)
                ga = ga_ref[rows, :]
                gate = _dot(ga, gu, "nn") + gb_ref[0]
                vc = v_ref[rows, :]
                f = _gla_chunk(q_ref[rows, :], k_ref[rows, :], vc, gate, k)
                state = _unpack_state(prev_ref[0, ci])
                o = f["o_intra"] + _dot(f["q_b"], state, "nn")
                ggv = gg_ref[rows, :]
                dout = do_ref[rows, :]
                d_o_h, dgg_h = [], []
                for hh in range(2):
                    lanes = slice(hh * dv, (hh + 1) * dv)
                    o_n, rinv, sg = _rms_gate(o[:, lanes], ggv[:, lanes])
                    silu = ggv[:, lanes] * sg
                    dgg_h.append(dout[:, lanes] * o_n * gn_ref[...] * (sg * (1.0 + ggv[:, lanes] * (1.0 - sg))))
                    d_ong = dout[:, lanes] * silu
                    ggn_ref[...] += jnp.sum(d_ong * o_n, axis=0, keepdims=True)
                    d_on = d_ong * gn_ref[...]
                    d_o_h.append(rinv * (d_on - o_n * jnp.mean(d_on * o_n, axis=-1, keepdims=True)))
                dgg_ref[rows, :] = jnp.concatenate(dgg_h, axis=1).astype(dgg_ref.dtype)
                d_o = jnp.concatenate(d_o_h, axis=1)
                d_upd = jnp.where(k["own"], dstate, 0.0)
                d_dec_col = dstate * state * f["dec_col"]
                dstate = f["dec_col"] * dstate + _dot(f["q_b"], d_o, "tn")
                dsc = [jnp.where(row >= col, _dot(d_o_h[hh], f["v_h"][hh], "nt"), 0.0) for hh in range(2)]
                dv_ref[rows, :] = (jnp.concatenate([_dot(f["sc"][hh], d_o_h[hh], "tn") for hh in range(2)], axis=1)
                                   + _dot(f["k_dec"], d_upd, "nn")).astype(dv_ref.dtype)
                q_in_h = [jnp.where(first, f["q_in"], 0.0), jnp.where(first, 0.0, f["q_in"])]
                dq_in = _dot(dsc[0], f["k_in_h"][0], "nn") + _dot(dsc[1], f["k_in_h"][1], "nn")
                dk_in = _dot(dsc[0], q_in_h[0], "tn") + _dot(dsc[1], q_in_h[1], "tn")
                dq_b = _dot(d_o, state, "nt")
                dkdec = _dot(vc, d_upd, "nt")
                b = f["b"]
                e1 = jnp.exp(b - f["b_ref"])
                e2 = jnp.exp(f["b_ref"] - b)
                e3 = jnp.exp(f["b_last"] - b)
                eb = jnp.exp(b)
                dq_ref[rows, :] = ((dq_in * e1 + dq_b * eb) * (GLA_KEY_DIM ** -0.5)).astype(dq_ref.dtype)
                dk_ref[rows, :] = (dk_in * e2 + dkdec * e3).astype(dk_ref.dtype)
                t_q = dq_in * f["q_in"]
                t_k = dk_in * f["k_in"]
                t_d = dkdec * f["k_dec"]
                db = t_q - t_k - t_d + dq_b * f["q_b"]
                db_ref = jnp.sum(t_k - t_q, axis=0, keepdims=True)
                db_last = (jnp.sum(t_d, axis=0, keepdims=True)
                           + jnp.max(_dot_split_nt01(ones_8, d_dec_col), axis=0, keepdims=True))
                db = db + jnp.where(rowk == C // 2 - 1, db_ref, 0.0) + jnp.where(rowk == C - 1, db_last, 0.0)
                dla = _dot_split_lhs01(upper_incl, db)
                d_gate = dla * (1.0 / GLA_TAU) * (1.0 / (1.0 + jnp.exp(gate)))
                ggb_ref[p] += jnp.sum(d_gate, axis=0, keepdims=True)
                ggu_ref[p] += _dot(ga, d_gate, "tn")
                dga_ref[rows, :] += _dot(d_gate, gu, "nt")
            return dstate

        st_ref[p] = lax.fori_loop(0, ngroup, group, st_ref[p])

    back = lambda i: nb - 1 - i
    NPV = SB_WIDTH // PV
    return pl.pallas_call(
        body, name=name, grid=(nb, NP),
        in_specs=_gla_in_specs(back, RB) + [pl.BlockSpec((1, RB // C, PK, dv), lambda i, p: (p, back(i), 0, 0)),
                                            pl.BlockSpec((RB, PV), lambda i, p: (back(i), NPV + p))],
        out_specs=[pl.BlockSpec((RB, PK), lambda i, p: (back(i), p)),
                   pl.BlockSpec((RB, PK), lambda i, p: (back(i), p)),
                   pl.BlockSpec((RB, PV), lambda i, p: (back(i), p)),
                   pl.BlockSpec((RB, PV), lambda i, p: (back(i), p)),
                   pl.BlockSpec((RB, LANE), lambda i, p: (back(i), 0)),
                   pl.BlockSpec((NP, LANE, PK), lambda i, p: (0, 0, 0)),
                   pl.BlockSpec((NP, 1, PK), lambda i, p: (0, 0, 0)),
                   pl.BlockSpec((1, dv), lambda i, p: (0, 0))],
        out_shape=[jax.ShapeDtypeStruct((S, NP * PK), MXU_DTYPE), jax.ShapeDtypeStruct((S, NP * PK), MXU_DTYPE),
                   jax.ShapeDtypeStruct((S, GLA_WIDTH), MXU_DTYPE), jax.ShapeDtypeStruct((S, GLA_WIDTH), MXU_DTYPE),
                   jax.ShapeDtypeStruct((S, LANE), F32), jax.ShapeDtypeStruct((NP, LANE, PK), F32),
                   jax.ShapeDtypeStruct((NP, 1, PK), F32), jax.ShapeDtypeStruct((1, dv), F32)],
        scratch_shapes=[pltpu.VMEM((NP, PK, PV), F32)],
        compiler_params=_cparams(("arbitrary", "arbitrary")),
    )(proj, proj, proj, proj, proj, gate_up_p, gate_bias_p, gnorm, prev, d_cat)


def _exchange_copies(scatter_flags, ins, outs, send_sems, recv_sems, local_sems):
    n_peer = N_DEV - 1
    x, y, c = lax.axis_index("x"), lax.axis_index("y"), lax.axis_index("c")
    me = 4 * x + 2 * y + c
    copies = []
    for a, scatter in enumerate(scatter_flags):
        own = ins[a].at[me] if scatter else ins[a]
        copies.append(pltpu.make_async_copy(own, outs[a].at[me], local_sems.at[a]))
    for r in range(1, N_DEV):
        px = 1 - x if r & 4 else x
        py = 1 - y if r & 2 else y
        pc = 1 - c if r & 1 else c
        for a, scatter in enumerate(scatter_flags):
            src = ins[a].at[4 * px + 2 * py + pc] if scatter else ins[a]
            copies.append(pltpu.make_async_remote_copy(
                src_ref=src, dst_ref=outs[a].at[me],
                send_sem=send_sems.at[a * n_peer + r - 1], recv_sem=recv_sems.at[a * n_peer + r - 1],
                device_id=(px, py, pc), device_id_type=MESH_ID))
    return copies


def _exchange_shapes(items):
    out_shape = []
    for arr, scatter in items:
        shp = arr.shape if scatter else (N_DEV,) + arr.shape
        out_shape.append(jax.ShapeDtypeStruct(shp, arr.dtype))
    n = len(items)
    sems = [pltpu.SemaphoreType.DMA((n * (N_DEV - 1),)), pltpu.SemaphoreType.DMA((n * (N_DEV - 1),)),
            pltpu.SemaphoreType.DMA((n,))]
    return out_shape, sems


def _exchange(items, name):
    n = len(items)
    flags = [sc for _, sc in items]

    def body(*refs):
        copies = _exchange_copies(flags, refs[:n], refs[n:2 * n], *refs[2 * n:])
        for cp in copies:
            cp.start()
        for cp in copies:
            cp.wait()

    out_shape, sems = _exchange_shapes(items)
    any_spec = pl.BlockSpec(memory_space=pl.ANY)
    return pl.pallas_call(
        body, name=name, in_specs=[any_spec] * n, out_specs=[any_spec] * n, out_shape=out_shape,
        scratch_shapes=sems, compiler_params=pltpu.CompilerParams(has_side_effects=True),
    )(*[arr for arr, _ in items])


def _riding_exchange(items, grid):
    n = len(items)
    flags = [sc for _, sc in items]
    out_shape, sems = _exchange_shapes(items)
    any_spec = pl.BlockSpec(memory_space=pl.ANY)

    def wrap(body, n_in, n_out):
        def fused(*refs):
            ins = refs[:n_in]
            x_ins = refs[n_in:n_in + n]
            outs = refs[n_in + n:n_in + n + n_out]
            x_outs = refs[n_in + n + n_out:n_in + 2 * n + n_out]
            rest = refs[n_in + 2 * n + n_out:]
            x_sems, scratch = rest[len(rest) - 3:], rest[:len(rest) - 3]
            first = last = True
            for d, n_d in enumerate(grid):
                first = jnp.logical_and(first, pl.program_id(d) == 0)
                last = jnp.logical_and(last, pl.program_id(d) == n_d - 1)

            @pl.when(first)
            def _():
                for cp in _exchange_copies(flags, x_ins, x_outs, *x_sems):
                    cp.start()

            body(*ins, *outs, *scratch)

            @pl.when(last)
            def _():
                for cp in _exchange_copies(flags, x_ins, x_outs, *x_sems):
                    cp.wait()

        return fused

    return [arr for arr, _ in items], [any_spec] * n, [any_spec] * n, out_shape, sems, wrap


def _sum_devices(ref):
    g = ref[0].astype(F32)
    for q in range(1, N_DEV):
        g = g + ref[q].astype(F32)
    return g


def _adam_math(g, w, m, v):
    nm = ADAM_B1 * m + (1.0 - ADAM_B1) * g
    nv = ADAM_B2 * v + (1.0 - ADAM_B2) * (g * g)
    m_hat = nm / (1.0 - ADAM_B1 ** ADAM_STEP)
    v_hat = nv / (1.0 - ADAM_B2 ** ADAM_STEP)
    return -ADAM_LR * (m_hat / (jnp.sqrt(v_hat) + ADAM_EPS) + ADAM_WD * w), nm, nv


def _adamw(grecv, w, m, v, name):
    R, C = w.shape
    tile = _pick(R, (256, 176, 128)) if R * C > 65536 else R

    def body(gr_ref, w_ref, m_ref, v_ref, g_ref, d_ref, nm_ref, nv_ref):
        g = _sum_devices(gr_ref)
        g_ref[...] = g
        d_ref[...], nm_ref[...], nv_ref[...] = _adam_math(g, w_ref[...], m_ref[...], v_ref[...])

    blk = pl.BlockSpec((tile, C), lambda i: (i, 0))
    sds = jax.ShapeDtypeStruct((R, C), F32)
    return pl.pallas_call(
        body, name=name, grid=(R // tile,),
        in_specs=[pl.BlockSpec((N_DEV, tile, C), lambda i: (0, i, 0)), blk, blk, blk],
        out_specs=[blk, blk, blk, blk], out_shape=[sds, sds, sds, sds],
        compiler_params=_cparams(("parallel",)),
    )(grecv, w, m, v)


def _adamw_replicated(grecvs, loss_recv, ws, ms, vs, name):
    nt = len(ws)

    def body(*refs):
        gr, lr = refs[:nt], refs[nt]
        w, m, v = refs[nt + 1:2 * nt + 1], refs[2 * nt + 1:3 * nt + 1], refs[3 * nt + 1:4 * nt + 1]
        outs = refs[4 * nt + 1:]
        outs[0][...] = _sum_devices(lr)
        for t in range(nt):
            g_ref, d_ref, nm_ref, nv_ref = outs[1 + 4 * t:5 + 4 * t]
            g = _sum_devices(gr[t])
            g_ref[...] = g
            d_ref[...], nm_ref[...], nv_ref[...] = _adam_math(g, w[t][...], m[t][...], v[t][...])

    out_shape = [jax.ShapeDtypeStruct((1, LANE), F32)]
    for t in range(nt):
        out_shape += [jax.ShapeDtypeStruct(ws[t].shape, F32)] * 4
    outs = pl.pallas_call(body, name=name, out_shape=out_shape, compiler_params=_cparams())(
        *grecvs, loss_recv, *ws, *ms, *vs)
    return outs[0], [outs[1 + 4 * t:5 + 4 * t] for t in range(nt)]


def _shard_cols(g):
    rows, cols = g.shape
    return g.reshape(rows, N_DEV, cols // N_DEV).transpose(1, 0, 2)


def _unshard_cols(blocks):
    return blocks.transpose(1, 0, 2).reshape(blocks.shape[1], -1)


def _heads(t, n, d):
    return t.reshape(t.shape[0], n, d).transpose(1, 0, 2)


def _unheads(t):
    return t.transpose(1, 0, 2).reshape(t.shape[1], -1)


def kernel(x, w_in, gate_up, gate_bias, gla_norm_g, w_out, ln1_g, ln1_b, w_up, conv_w, conv_b, w_down, ln2_g, ln2_b, loss_target, m_w_in, m_gate_up, m_gate_bias, m_gla_norm_g, m_w_out, m_ln1_g, m_ln1_b, m_w_up, m_conv_w, m_conv_b, m_w_down, m_ln2_g, m_ln2_b, v_w_in, v_gate_up, v_gate_bias, v_gla_norm_g, v_w_out, v_ln1_g, v_ln1_b, v_w_up, v_conv_w, v_conv_b, v_w_down, v_ln2_g, v_ln2_b):
    S, D = x.shape[1], x.shape[2]
    x2, tgt = x[0], loss_target[0]

    gathered = _exchange([(w_in[0].astype(MXU_DTYPE), False), (gate_up[0], False), (conv_w[0], False)],
                         "gather_w_in")
    w_in_f = _unshard_cols(gathered[0])
    gate_up_f = _unshard_cols(gathered[1])
    conv_w_f = _unshard_cols(gathered[2])
    w_in_pad = jnp.pad(w_in_f, ((0, 0), (0, IN_PAD - IN_WIDTH)))
    gate_up_p = _heads(jnp.pad(gate_up_f, ((0, LANE - GLA_GATE_RANK), (0, 0))), GLA_HEADS // 2, GLA_PAIR_K)
    gate_bias_p = gate_bias.reshape(GLA_HEADS // 2, 1, GLA_PAIR_K)

    w_qkv = jnp.concatenate([w_in_f[:, :OFF_SBK] * (SB_HEAD_DIM ** -0.5), w_in_f[:, OFF_SBK:OFF_GQ]], axis=1)
    qkv = _matmul(x2, w_qkv, "nn", MXU_DTYPE, "proj_sb")
    proj = _matmul(x2, w_in_pad[:, OFF_GQ:], "nn", F32, "proj_gla")
    sb_o, sb_tot, g_out, g_up, g_down = _sb_fwd(
        qkv, "sb_fwd", ride=[(w_out[0].astype(MXU_DTYPE), False), (w_up[0].astype(MXU_DTYPE), False),
                             (w_down[0].astype(MXU_DTYPE), False)])
    w_out_f = g_out.reshape(-1, D)
    w_up_f = _unshard_cols(g_up)
    w_down_f = g_down.reshape(-1, D)
    gla_o, prev = _gla_fwd(proj, gate_up_p, gate_bias_p, gla_norm_g, "gla_fwd")
    cat = jnp.concatenate([sb_o, gla_o], axis=1)
    r1 = _matmul(cat, w_out_f, "nn", F32, "mix", res=x2, res_scale=DN_ALPHA)
    h = _ln_fwd(r1, ln1_g, ln1_b, "ln1")
    u0 = _matmul(h, w_up_f, "nn", F32, "ffn_up")
    p = _conv_gelu_fwd(u0, conv_w_f, conv_b, "conv_gelu")
    r2 = _matmul(p, w_down_f, "nn", F32, "ffn_down", res=h, res_scale=DN_ALPHA)
    d_r2, loss_p, g_ln2_g, g_ln2_b = _ln_loss_bwd(r2, tgt, ln2_g, ln2_b, "ln2_loss")

    d_p = _matmul(d_r2, w_down_f, "nt", MXU_DTYPE, "d_ffn_act")
    g_w_down = _matmul(p, d_r2, "tn", F32, "grad_w_down")
    d_u0, g_conv_w, g_conv_b = _conv_gelu_bwd(u0, d_p, conv_w_f, conv_b, "conv_gelu_bwd")
    g_w_up = _matmul(h, d_u0, "tn", F32, "grad_w_up")
    d_h = _matmul(d_u0, w_up_f, "nt", F32, "d_h", res=d_r2, res_scale=DN_ALPHA)
    d_r1, g_ln1_g, g_ln1_b = _ln_bwd(r1, d_h, ln1_g, "ln1_bwd")
    g_w_out = _matmul(cat, d_r1, "tn", F32, "grad_w_out")
    d_cat = _matmul(d_r1, w_out_f, "nt", F32, "d_cat")
    (d_gq, d_gk, d_gv, d_gg, d_ga_pad, g_gu_p, g_gb_p, g_gnorm) = _gla_bwd(
        proj, gate_up_p, gate_bias_p, gla_norm_g, prev, d_cat, "gla_bwd")
    g_gate_up = _unheads(g_gu_p[:, :GLA_GATE_RANK, :])
    g_gate_bias = g_gb_p.reshape(1, -1)
    small_g = [g_gate_bias, g_gnorm, g_ln1_g, g_ln1_b, g_conv_b, g_ln2_g, g_ln2_b]
    d_sq, d_sk, d_sv, *recv_rest = _sb_bwd(
        qkv, d_cat, sb_tot, "sb_bwd",
        ride=[(g_w_out.reshape(N_DEV, -1, D).astype(BF16), True), (_shard_cols(g_w_up).astype(BF16), True),
              (g_w_down.reshape(N_DEV, -1, D).astype(BF16), True), (_shard_cols(g_gate_up), True),
              (_shard_cols(g_conv_w), True)] + [(t, False) for t in small_g] + [(loss_p, False)])
    d_proj = jnp.concatenate([d_sq, d_sk, d_sv, d_gq, d_gk, d_gv, d_gg,
                              d_ga_pad.astype(MXU_DTYPE)], axis=1)
    g_w_in = _matmul(x2, d_proj, "tn", F32, "grad_w_in")[:, :IN_WIDTH]
    d_x, recv_in = _matmul(d_proj, w_in_pad, "nt", F32, "d_x", res=d_r1, res_scale=DN_ALPHA,
                           ride=[(_shard_cols(g_w_in).astype(BF16), True)])

    recv = [recv_in] + recv_rest[:5]
    sharded = [(w_in, m_w_in, v_w_in), (w_out, m_w_out, v_w_out), (w_up, m_w_up, v_w_up),
               (w_down, m_w_down, v_w_down), (gate_up, m_gate_up, v_gate_up), (conv_w, m_conv_w, v_conv_w)]
    upd = [_adamw(recv[n], w[0], m[0], v[0], "adamw_%d" % n) for n, (w, m, v) in enumerate(sharded)]
    loss_row, small = _adamw_replicated(
        recv_rest[5:12], recv_rest[12], [gate_bias, gla_norm_g, ln1_g, ln1_b, conv_b, ln2_g, ln2_b],
        [m_gate_bias, m_gla_norm_g, m_ln1_g, m_ln1_b, m_conv_b, m_ln2_g, m_ln2_b],
        [v_gate_bias, v_gla_norm_g, v_ln1_g, v_ln1_b, v_conv_b, v_ln2_g, v_ln2_b], "adamw_replicated")
    outs = []
    for kind in range(4):
        b_w_in, b_w_out, b_w_up, b_w_down, b_gate_up, b_conv_w = [u[kind][None] for u in upd]
        s_gb, s_gn, s_l1g, s_l1b, s_cb, s_l2g, s_l2b = [t[kind] for t in small]
        outs += [b_w_in, b_gate_up, s_gb, s_gn, b_w_out, s_l1g, s_l1b, b_w_up, b_conv_w, s_cb, b_w_down,
                 s_l2g, s_l2b]
    return (loss_row[0, 0], d_x[None], *outs)
```

```python
import math

import jax
import jax.numpy as jnp
from jax import lax
from jax.experimental import pallas as pl
from jax.experimental.pallas import tpu as pltpu

F32 = jnp.float32
BF16 = jnp.bfloat16
MXU_DTYPE = jnp.bfloat16

N_DEV = 8
D_MODEL = 1024
SB_WIDTH = 512
SB_HEADS = 8
SB_HEAD_DIM = 64
GLA_HEADS = 4
GLA_KEY_DIM = 64
GLA_VAL_DIM = 128
GLA_WIDTH = 512
GLA_GATE_RANK = 16
GLA_TAU = 16.0
CHUNK = 64
D_FF = 2816
CONV_WIDTH = 3
LN_EPS = 1e-5
RMS_EPS = 1e-6
DN_ALPHA = 2.0 ** 0.25
IN_WIDTH = 3088
LANE = 128
IN_PAD = 3200
OFF_SBQ, OFF_SBK, OFF_SBV = 0, 512, 1024
OFF_GQ, OFF_GK, OFF_GV, OFF_GG, OFF_GA = 1536, 1792, 2048, 2560, 3072
GLA_PAD = IN_PAD - OFF_GQ

ADAM_LR = 0.001
ADAM_B1 = 0.9
ADAM_B2 = 0.999
ADAM_EPS = 1e-08
ADAM_WD = 0.01
ADAM_STEP = 10

VMEM_LIMIT = 48 * 1024 * 1024
MESH_ID = pl.DeviceIdType.MESH


def _cparams(sem=None, **kw):
    return pltpu.CompilerParams(dimension_semantics=sem, vmem_limit_bytes=VMEM_LIMIT, **kw)


def _dot(a, b, dims):
    ca, cb = {"nn": (1, 0), "nt": (1, 1), "tn": (0, 0)}[dims]
    return lax.dot_general(a.astype(MXU_DTYPE), b.astype(MXU_DTYPE), (((ca,), (cb,)), ((), ())),
                           preferred_element_type=F32)


def _dot_split(a, b, dims):
    assert dims == "nn"
    hi = a.astype(BF16)
    lo = (a - hi.astype(F32)).astype(BF16)
    return lax.dot_general(jnp.concatenate([hi, lo], axis=1), jnp.concatenate([b, b], axis=0),
                           (((1,), (0,)), ((), ())), preferred_element_type=F32)


def _pick(dim, prefs):
    for p in prefs:
        if dim % p == 0:
            return p
    return dim


def _matmul(a, b, dims, out_dtype, name, res=None, res_scale=1.0, ride=()):
    if dims == "nn":
        (M, K), (_, N) = a.shape, b.shape
    elif dims == "nt":
        (M, K), (N, _) = a.shape, b.shape
    else:
        (K, M), (_, N) = a.shape, b.shape
    tm = _pick(M, (1024, 1408, 512, 256, 128))
    tn = _pick(N, (1408, 1024, 640, 512))
    if tn == N and N > 2048:
        tn = _pick(N, (256, 128))
    tk = _pick(K, (1024, 1408, 640, 512, 256, 128))
    nk = K // tk
    grid = (M // tm, N // tn, nk)
    if dims == "tn":
        a_spec = pl.BlockSpec((tk, tm), lambda i, j, k: (k, i))
    else:
        a_spec = pl.BlockSpec((tm, tk), lambda i, j, k: (i, k))
    if dims == "nt":
        b_spec = pl.BlockSpec((tn, tk), lambda i, j, k: (j, k))
    else:
        b_spec = pl.BlockSpec((tk, tn), lambda i, j, k: (k, j))
    o_spec = pl.BlockSpec((tm, tn), lambda i, j, k: (i, j))
    in_specs = [a_spec, b_spec]
    args = [a, b]
    if res is not None:
        in_specs.append(o_spec)
        args.append(res)

    def body(*refs):
        if res is not None:
            a_ref, b_ref, r_ref, o_ref, acc_ref = refs
        else:
            a_ref, b_ref, o_ref, acc_ref = refs
            r_ref = None
        k = pl.program_id(2)
        part = _dot(a_ref[...], b_ref[...], dims)

        def finish(total):
            if r_ref is not None:
                total = total + res_scale * r_ref[...]
            o_ref[...] = total.astype(o_ref.dtype)

        if nk == 1:
            finish(part)
        else:
            @pl.when(k == 0)
            def _():
                acc_ref[...] = part

            @pl.when(jnp.logical_and(k > 0, k < nk - 1))
            def _():
                acc_ref[...] += part

            @pl.when(k == nk - 1)
            def _():
                finish(acc_ref[...] + part)

    out_sds = jax.ShapeDtypeStruct((M, N), out_dtype)
    acc = pltpu.VMEM((tm, tn), F32)
    if not ride:
        return pl.pallas_call(
            body, name=name, grid=grid, in_specs=in_specs, out_specs=o_spec, out_shape=out_sds,
            scratch_shapes=[acc], compiler_params=_cparams(("parallel", "parallel", "arbitrary")),
        )(*args)
    x_args, x_in, x_out, x_shapes, x_sems, wrap = _riding_exchange(list(ride), grid)
    return pl.pallas_call(
        wrap(body, len(args), 1), name=name, grid=grid, in_specs=in_specs + x_in, out_specs=[o_spec] + x_out,
        out_shape=[out_sds] + x_shapes, scratch_shapes=[acc] + x_sems,
        compiler_params=_cparams(("arbitrary",) * 3, has_side_effects=True),
    )(*args, *x_args)


LN_ROWS = 256


def _ln_stats(r):
    mu = jnp.mean(r, axis=-1, keepdims=True)
    xc = r - mu
    var = jnp.mean(xc * xc, axis=-1, keepdims=True)
    return xc * lax.rsqrt(var + LN_EPS)


def _ln_fwd(r, g, b, name):
    S, D = r.shape

    def body(r_ref, g_ref, b_ref, h_ref):
        h_ref[...] = _ln_stats(r_ref[...]) * g_ref[...] + b_ref[...]

    row = pl.BlockSpec((LN_ROWS, D), lambda i: (i, 0))
    vec = pl.BlockSpec((1, D), lambda i: (0, 0))
    return pl.pallas_call(
        body, name=name, grid=(S // LN_ROWS,), in_specs=[row, vec, vec], out_specs=row,
        out_shape=jax.ShapeDtypeStruct((S, D), F32),
        compiler_params=_cparams(("parallel",)),
    )(r, g, b)


def _ln_bwd_core(xhat, dy, g):
    dxh = dy * g
    m1 = jnp.mean(dxh, axis=-1, keepdims=True)
    m2 = jnp.mean(dxh * xhat, axis=-1, keepdims=True)
    return dxh - m1 - xhat * m2


def _ln_bwd(r, dy, g, name):
    S, D = r.shape

    def body(r_ref, dy_ref, g_ref, dr_ref, gg_ref, gb_ref):
        x = r_ref[...]
        mu = jnp.mean(x, axis=-1, keepdims=True)
        xc = x - mu
        rstd = lax.rsqrt(jnp.mean(xc * xc, axis=-1, keepdims=True) + LN_EPS)
        xhat = xc * rstd
        dy = dy_ref[...]
        dr_ref[...] = rstd * _ln_bwd_core(xhat, dy, g_ref[...])

        @pl.when(pl.program_id(0) == 0)
        def _():
            gg_ref[...] = jnp.zeros_like(gg_ref)
            gb_ref[...] = jnp.zeros_like(gb_ref)

        gg_ref[...] += jnp.sum(dy * xhat, axis=0, keepdims=True)
        gb_ref[...] += jnp.sum(dy, axis=0, keepdims=True)

    row = pl.BlockSpec((LN_ROWS, D), lambda i: (i, 0))
    vec = pl.BlockSpec((1, D), lambda i: (0, 0))
    return pl.pallas_call(
        body, name=name, grid=(S // LN_ROWS,), in_specs=[row, row, vec], out_specs=[row, vec, vec],
        out_shape=[jax.ShapeDtypeStruct((S, D), F32), jax.ShapeDtypeStruct((1, D), F32),
                   jax.ShapeDtypeStruct((1, D), F32)],
        compiler_params=_cparams(("arbitrary",)),
    )(r, dy, g)


def _ln_loss_bwd(r, target, g, b, name):
    S, D = r.shape

    def body(r_ref, t_ref, g_ref, b_ref, dr_ref, loss_ref, gg_ref, gb_ref):
        x = r_ref[...]
        mu = jnp.mean(x, axis=-1, keepdims=True)
        xc = x - mu
        rstd = lax.rsqrt(jnp.mean(xc * xc, axis=-1, keepdims=True) + LN_EPS)
        xhat = xc * rstd
        y = xhat * g_ref[...] + b_ref[...]
        err = y - t_ref[...]
        dy = err * (1.0 / D)
        dr_ref[...] = rstd * _ln_bwd_core(xhat, dy, g_ref[...])

        @pl.when(pl.program_id(0) == 0)
        def _():
            loss_ref[...] = jnp.zeros_like(loss_ref)
            gg_ref[...] = jnp.zeros_like(gg_ref)
            gb_ref[...] = jnp.zeros_like(gb_ref)

        per_row = jnp.sum(err * err, axis=-1, keepdims=True) * (0.5 / D)
        loss_ref[...] += jnp.broadcast_to(jnp.sum(per_row, axis=0, keepdims=True), loss_ref.shape)
        gg_ref[...] += jnp.sum(dy * xhat, axis=0, keepdims=True)
        gb_ref[...] += jnp.sum(dy, axis=0, keepdims=True)

    row = pl.BlockSpec((LN_ROWS, D), lambda i: (i, 0))
    vec = pl.BlockSpec((1, D), lambda i: (0, 0))
    lvec = pl.BlockSpec((1, LANE), lambda i: (0, 0))
    return pl.pallas_call(
        body, name=name, grid=(S // LN_ROWS,), in_specs=[row, row, vec, vec],
        out_specs=[row, lvec, vec, vec],
        out_shape=[jax.ShapeDtypeStruct((S, D), F32), jax.ShapeDtypeStruct((1, LANE), F32),
                   jax.ShapeDtypeStruct((1, D), F32), jax.ShapeDtypeStruct((1, D), F32)],
        compiler_params=_cparams(("arbitrary",)),
    )(r, target, g, b)


CONV_COLS = 256
CONV_ROWS = 256
HALO = 8
INV_SQRT2 = 1.0 / math.sqrt(2.0)
INV_SQRT2PI = 1.0 / math.sqrt(2.0 * math.pi)


def _gelu(x):
    return 0.5 * x * (1.0 + lax.erf(x * INV_SQRT2))


def _gelu_and_grad(x):
    cdf = 0.5 * (1.0 + lax.erf(x * INV_SQRT2))
    return x * cdf, cdf + x * jnp.exp(-0.5 * x * x) * INV_SQRT2PI


def _conv_rows(ext, w_ref, b_ref, n):
    total = ext.shape[0]
    s1 = pltpu.roll(ext, 1, 0)
    s2 = pltpu.roll(ext, 2, 0)
    u = w_ref[2:3, :] * ext + w_ref[1:2, :] * s1 + w_ref[0:1, :] * s2 + b_ref[...]
    return u[HALO:total], s1[HALO:total], s2[HALO:total]


def _conv_gelu_fwd(u0, conv_w, conv_b, name):
    S, C2 = u0.shape
    F = C2 // 2
    ncb = F // CONV_COLS
    nrc = S // CONV_ROWS

    def body(ua_ref, uc_ref, wa_ref, wc_ref, ba_ref, bc_ref, p_ref):
        def chunk(ci, _):
            r0 = pl.multiple_of(ci * CONV_ROWS, CONV_ROWS)
            p0 = pl.multiple_of(jnp.maximum(r0 - HALO, 0), HALO)
            keep = (ci > 0).astype(F32)

            def load(ref):
                prev = ref[pl.ds(p0, HALO), :] * keep
                return jnp.concatenate([prev, ref[pl.ds(r0, CONV_ROWS), :]], axis=0)

            a, _, _ = _conv_rows(load(ua_ref), wa_ref, ba_ref, CONV_ROWS)
            c, _, _ = _conv_rows(load(uc_ref), wc_ref, bc_ref, CONV_ROWS)
            p_ref[pl.ds(r0, CONV_ROWS), :] = (_gelu(a) * c).astype(p_ref.dtype)
            return 0

        lax.fori_loop(0, nrc, chunk, 0)

    col_a = pl.BlockSpec((S, CONV_COLS), lambda j: (0, j))
    col_c = pl.BlockSpec((S, CONV_COLS), lambda j: (0, j + ncb))
    w_a = pl.BlockSpec((CONV_WIDTH, CONV_COLS), lambda j: (0, j))
    w_c = pl.BlockSpec((CONV_WIDTH, CONV_COLS), lambda j: (0, j + ncb))
    b_a = pl.BlockSpec((1, CONV_COLS), lambda j: (0, j))
    b_c = pl.BlockSpec((1, CONV_COLS), lambda j: (0, j + ncb))
    return pl.pallas_call(
        body, name=name, grid=(ncb,), in_specs=[col_a, col_c, w_a, w_c, b_a, b_c], out_specs=col_a,
        out_shape=jax.ShapeDtypeStruct((S, F), MXU_DTYPE),
        compiler_params=_cparams(("parallel",)),
    )(u0, u0, conv_w, conv_w, conv_b, conv_b)


def _conv_gelu_bwd(u0, dp, conv_w, conv_b, name):
    S, C2 = u0.shape
    F = C2 // 2
    ncb = F // CONV_COLS
    nrc = S // CONV_ROWS
    EXT = CONV_ROWS + HALO

    def body(ua_ref, uc_ref, dp_ref, wa_ref, wc_ref, ba_ref, bc_ref,
             da_ref, dc_ref, gwa_ref, gwc_ref, gba_ref, gbc_ref):
        gwa_ref[...] = jnp.zeros_like(gwa_ref)
        gwc_ref[...] = jnp.zeros_like(gwc_ref)
        gba_ref[...] = jnp.zeros_like(gba_ref)
        gbc_ref[...] = jnp.zeros_like(gbc_ref)
        rid = lax.broadcasted_iota(jnp.int32, (EXT, CONV_COLS), 0)

        def chunk(ci, _):
            r0 = pl.multiple_of(ci * CONV_ROWS, CONV_ROWS)
            p0 = pl.multiple_of(jnp.maximum(r0 - HALO, 0), HALO)
            n0 = pl.multiple_of(jnp.minimum(r0 + CONV_ROWS, S - HALO), HALO)
            keep_prev = (ci > 0).astype(F32)
            keep_next = (ci < nrc - 1).astype(F32)

            def load(ref):
                return jnp.concatenate([ref[pl.ds(p0, HALO), :] * keep_prev,
                                        ref[pl.ds(r0, CONV_ROWS), :],
                                        ref[pl.ds(n0, HALO), :] * keep_next], axis=0)

            ext_a = load(ua_ref)
            ext_c = load(uc_ref)
            a, a1, a2 = _conv_rows(ext_a, wa_ref, ba_ref, EXT)
            c, c1, c2 = _conv_rows(ext_c, wc_ref, bc_ref, EXT)
            a0 = ext_a[HALO:HALO + EXT]
            c0 = ext_c[HALO:HALO + EXT]
            dpe = jnp.concatenate([dp_ref[pl.ds(r0, CONV_ROWS), :].astype(F32),
                                   dp_ref[pl.ds(n0, HALO), :].astype(F32) * keep_next], axis=0)
            gelu_a, slope_a = _gelu_and_grad(a)
            d_a = dpe * c * slope_a
            d_c = dpe * gelu_a
            own = rid < CONV_ROWS

            def back(d_u, w_ref, x0, x1, x2, d_ref, gw_ref, gb_ref):
                d_u0 = (w_ref[2:3, :] * d_u + w_ref[1:2, :] * pltpu.roll(d_u, EXT - 1, 0)
                        + w_ref[0:1, :] * pltpu.roll(d_u, EXT - 2, 0))
                d_ref[pl.ds(r0, CONV_ROWS), :] = d_u0[0:CONV_ROWS].astype(d_ref.dtype)
                d_own = jnp.where(own, d_u, 0.0)
                gw_ref[...] += jnp.concatenate(
                    [jnp.sum(d_own * x2, axis=0, keepdims=True),
                     jnp.sum(d_own * x1, axis=0, keepdims=True),
                     jnp.sum(d_own * x0, axis=0, keepdims=True)], axis=0)
                gb_ref[...] += jnp.sum(d_own, axis=0, keepdims=True)

            back(d_a, wa_ref, a0, a1, a2, da_ref, gwa_ref, gba_ref)
            back(d_c, wc_ref, c0, c1, c2, dc_ref, gwc_ref, gbc_ref)
            return 0

        lax.fori_loop(0, nrc, chunk, 0)

    col_a = pl.BlockSpec((S, CONV_COLS), lambda j: (0, j))
    col_c = pl.BlockSpec((S, CONV_COLS), lambda j: (0, j + ncb))
    w_a = pl.BlockSpec((CONV_WIDTH, CONV_COLS), lambda j: (0, j))
    w_c = pl.BlockSpec((CONV_WIDTH, CONV_COLS), lambda j: (0, j + ncb))
    b_a = pl.BlockSpec((1, CONV_COLS), lambda j: (0, j))
    b_c = pl.BlockSpec((1, CONV_COLS), lambda j: (0, j + ncb))
    outs = pl.pallas_call(
        body, name=name, grid=(ncb,),
        in_specs=[col_a, col_c, col_a, w_a, w_c, b_a, b_c],
        out_specs=[col_a, col_a, w_a, w_a, b_a, b_a],
        out_shape=[jax.ShapeDtypeStruct((S, F), MXU_DTYPE), jax.ShapeDtypeStruct((S, F), MXU_DTYPE),
                   jax.ShapeDtypeStruct((CONV_WIDTH, F), F32), jax.ShapeDtypeStruct((CONV_WIDTH, F), F32),
                   jax.ShapeDtypeStruct((1, F), F32), jax.ShapeDtypeStruct((1, F), F32)],
        compiler_params=_cparams(("parallel",)),
    )(u0, u0, dp, conv_w, conv_w, conv_b, conv_b)
    da, dc, gwa, gwc, gba, gbc = outs
    return (jnp.concatenate([da, dc], axis=1), jnp.concatenate([gwa, gwc], axis=1),
            jnp.concatenate([gba, gbc], axis=1))


SB_TK = 128
SB_TQ_FWD = 1024
SB_TQ_BWD = 1024
SB_PAIRS_FWD = 2


def _softplus(z):
    return jnp.maximum(z, 0.0) + jnp.log(1.0 + jnp.exp(-jnp.abs(z)))


def _tri_ones(after):
    r = lax.broadcasted_iota(jnp.int32, (SB_TK, 2 * SB_TK), 0)
    c = lax.broadcasted_iota(jnp.int32, (SB_TK, 2 * SB_TK), 1)
    tri = (r > c) if after else (r < c)
    return jnp.where(c >= SB_TK, 1.0, jnp.where(tri, 1.0, 0.0)).astype(BF16)


def _sb_block_specs(S, TQ):
    NP = SB_WIDTH // LANE
    return [pl.BlockSpec((TQ, LANE), lambda p, i: (i, p)),
            pl.BlockSpec((S, LANE), lambda p, i: (0, NP + p)),
            pl.BlockSpec((S, LANE), lambda p, i: (0, 2 * NP + p))]


def _sb_fwd(qkv, name, ride=()):
    S = qkv.shape[0]
    TK = SB_TK
    TQ = min(SB_TQ_FWD, S)
    R = TQ // TK
    NP = SB_WIDTH // LANE
    PS = SB_PAIRS_FWD
    W = PS * LANE
    grid = (NP // PS, S // TQ)

    def body(q_ref, k_ref, v_ref, o_ref, t_ref):
        i = pl.program_id(1)
        row = lax.broadcasted_iota(jnp.int32, (TQ, TK), 0)
        col = lax.broadcasted_iota(jnp.int32, (TQ, TK), 1)
        after_ones = _tri_ones(True)
        first = lax.broadcasted_iota(jnp.int32, (TK, LANE), 1) < SB_HEAD_DIM

        def block(j, carry, r0):
            k0 = pl.multiple_of(j * TK, TK)
            masked = r0 is not None
            r0 = r0 or 0
            out = []
            for pr in range(PS):
                lanes = slice(pr * LANE, (pr + 1) * LANE)
                acc0, tail_a, tail_b = carry[3 * pr:3 * pr + 3]
                kab = k_ref[pl.ds(k0, TK), lanes]
                vab = v_ref[pl.ds(k0, TK), lanes]
                none = jnp.zeros_like(kab)
                k2 = jnp.concatenate([jnp.where(first, kab, none), jnp.where(first, none, kab)], axis=0)
                v2 = jnp.concatenate([jnp.where(first, vab, none), jnp.where(first, none, vab)], axis=0)
                z2 = _dot(q_ref[r0:, lanes], k2, "nt")
                tails, ws = [], []
                for hh, tail in enumerate((tail_a, tail_b)):
                    z = z2[:, hh * TK:(hh + 1) * TK]
                    sp = _softplus(z)
                    if masked:
                        strict = col[r0:] < row[r0:] - r0
                        sp = jnp.where(strict, sp, 0.0)
                    cs = _dot_split(sp, after_ones, "nn")
                    w = jnp.exp(z - sp - cs[:, :TK] - tail[r0:])
                    if masked:
                        w = jnp.where(strict, w, 0.0)
                    ws.append(w.astype(MXU_DTYPE))
                    tot = cs[:, TK:]
                    if r0:
                        tot = jnp.concatenate([jnp.zeros((r0, TK), F32), tot], axis=0)
                    tails.append(tail + tot)
                acc_r = acc0[r0:] + _dot(jnp.concatenate(ws, axis=1), v2, "nn")
                acc = jnp.concatenate([acc0[:r0], acc_r], axis=0) if r0 else acc_r
                out += [acc, tails[0], tails[1]]
            return tuple(out)

        carry = (jnp.zeros((TQ, LANE), F32), jnp.zeros((TQ, TK), F32), jnp.zeros((TQ, TK), F32)) * PS
        for u in reversed(range(R)):
            carry = block(R * i + u, carry, u * TK)
        carry = lax.fori_loop(0, R * i, lambda n, c: block(R * i - 1 - n, c, None), carry)
        for pr in range(PS):
            o_ref[:, pr * LANE:(pr + 1) * LANE] = carry[3 * pr].astype(o_ref.dtype)
            t_ref[2 * pr] = carry[3 * pr + 1]
            t_ref[2 * pr + 1] = carry[3 * pr + 2]

    in_specs = [pl.BlockSpec((TQ, W), lambda p, i: (i, p)),
                pl.BlockSpec((S, W), lambda p, i: (0, NP // PS + p)),
                pl.BlockSpec((S, W), lambda p, i: (0, 2 * (NP // PS) + p))]
    ospec = pl.BlockSpec((TQ, W), lambda p, i: (i, p))
    tspec = pl.BlockSpec((2 * PS, TQ, TK), lambda p, i: (p, i, 0))
    x_args, x_in, x_out, x_shapes, x_sems, wrap = _riding_exchange(list(ride), grid)
    return pl.pallas_call(
        wrap(body, 3, 2) if ride else body, name=name, grid=grid,
        in_specs=in_specs + (x_in if ride else []),
        out_specs=[ospec, tspec] + (x_out if ride else []),
        out_shape=[jax.ShapeDtypeStruct((S, SB_WIDTH), MXU_DTYPE), jax.ShapeDtypeStruct((SB_HEADS, S, TK), F32)]
        + (x_shapes if ride else []),
        scratch_shapes=x_sems if ride else [],
        compiler_params=_cparams(("arbitrary", "arbitrary"), has_side_effects=bool(ride)),
    )(qkv, qkv, qkv, *(x_args if ride else []))


def _sb_bwd(qkv, d_cat, total, name, ride=()):
    S = qkv.shape[0]
    TK = SB_TK
    TQ = min(SB_TQ_BWD, S)
    R = TQ // TK
    NP = SB_WIDTH // LANE
    grid = (NP, S // TQ)
    scale = SB_HEAD_DIM ** -0.5

    def body(q_ref, k_ref, v_ref, do_ref, t_ref, dq_ref, dk_ref, dv_ref, dk_acc, dv_acc):
        i = pl.program_id(1)

        @pl.when(i == 0)
        def _():
            dk_acc[...] = jnp.zeros_like(dk_acc)
            dv_acc[...] = jnp.zeros_like(dv_acc)

        row = lax.broadcasted_iota(jnp.int32, (TQ, TK), 0)
        col = lax.broadcasted_iota(jnp.int32, (TQ, TK), 1)
        after_ones = _tri_ones(True)
        before_ones = _tri_ones(False)
        first = lax.broadcasted_iota(jnp.int32, (TK, LANE), 1) < SB_HEAD_DIM
        first_q = lax.broadcasted_iota(jnp.int32, (TQ, LANE), 1) < SB_HEAD_DIM
        qab = q_ref[...]
        doab = do_ref[...].astype(MXU_DTYPE)
        qdo = jnp.concatenate([qab, doab], axis=1)
        none_q = jnp.zeros_like(qab)
        q_h = [jnp.where(first_q, qab, none_q), jnp.where(first_q, none_q, qab)]
        do_h = [jnp.where(first_q, doab, none_q), jnp.where(first_q, none_q, doab)]

        def block(j, carry, r0):
            k0 = pl.multiple_of(j * TK, TK)
            masked = r0 is not None
            r0 = r0 or 0
            kab = k_ref[pl.ds(k0, TK), :]
            vab = v_ref[pl.ds(k0, TK), :]
            none = jnp.zeros_like(kab)
            k_h = [jnp.where(first, kab, none), jnp.where(first, none, kab)]
            v_h = [jnp.where(first, vab, none), jnp.where(first, none, vab)]
            dk_blk = dv_blk = None
            sums, dzs = [], []
            for hh in range(2):
                seen, gsum = carry[1 + 2 * hh], carry[2 + 2 * hh]
                kv = jnp.concatenate([jnp.concatenate([k_h[hh], none], axis=1),
                                      jnp.concatenate([none, v_h[hh]], axis=1)], axis=0)
                zdw = _dot(qdo[r0:], kv, "nt")
                z = zdw[:, :TK]
                sp = _softplus(z)
                logsig = z - sp
                if masked:
                    strict = col[r0:] < row[r0:] - r0
                    sp = jnp.where(strict, sp, 0.0)
                cs = _dot_split(sp, after_ones, "nn")
                seen_r = seen[r0:] + cs[:, TK:]
                w = jnp.exp(logsig - cs[:, :TK] - (t_ref[hh, r0:, :] - seen_r))
                if masked:
                    w = jnp.where(strict, w, 0.0)
                g = w * zdw[:, TK:]
                cg = _dot_split(g, before_ones, "nn")
                dz = g - jnp.exp(logsig) * (g + cg[:, :TK] + gsum[r0:])
                if masked:
                    dz = jnp.where(strict, dz, 0.0)
                dzb = dz.astype(MXU_DTYPE)
                dzs.append(dzb)
                dkc = _dot(dzb, q_h[hh][r0:], "tn")
                dvc = _dot(w, do_h[hh][r0:], "tn")
                dk_blk = dkc if dk_blk is None else dk_blk + dkc
                dv_blk = dvc if dv_blk is None else dv_blk + dvc
                gsum_r = gsum[r0:] + cg[:, TK:]
                if r0:
                    seen_r = jnp.concatenate([seen[:r0], seen_r], axis=0)
                    gsum_r = jnp.concatenate([gsum[:r0], gsum_r], axis=0)
                sums += [seen_r, gsum_r]
            dk_acc[pl.ds(k0, TK), :] += dk_blk
            dv_acc[pl.ds(k0, TK), :] += dv_blk
            dq_r = carry[0][r0:] + _dot(jnp.concatenate(dzs, axis=1), jnp.concatenate(k_h, axis=0), "nn")
            dq = jnp.concatenate([carry[0][:r0], dq_r], axis=0) if r0 else dq_r
            return (dq, *sums)

        zero = jnp.zeros((TQ, TK), F32)
        carry = (jnp.zeros((TQ, LANE), F32), zero, zero, zero, zero)
        carry = lax.fori_loop(0, R * i, lambda j, c: block(j, c, None), carry)
        for u in range(R):
            carry = block(R * i + u, carry, u * TK)
        dq_ref[...] = (carry[0] * scale).astype(dq_ref.dtype)

        @pl.when(i == S // TQ - 1)
        def _():
            dk_ref[...] = dk_acc[...].astype(dk_ref.dtype)
            dv_ref[...] = dv_acc[...].astype(dv_ref.dtype)

    qspec = pl.BlockSpec((TQ, LANE), lambda p, i: (i, p))
    full = pl.BlockSpec((S, LANE), lambda p, i: (0, p))
    tspec = pl.BlockSpec((2, TQ, TK), lambda p, i: (p, i, 0))
    sds = jax.ShapeDtypeStruct((S, SB_WIDTH), MXU_DTYPE)
    accs = [pltpu.VMEM((S, LANE), F32), pltpu.VMEM((S, LANE), F32)]
    x_args, x_in, x_out, x_shapes, x_sems, wrap = _riding_exchange(list(ride), grid)
    return pl.pallas_call(
        wrap(body, 5, 3) if ride else body, name=name, grid=grid,
        in_specs=_sb_block_specs(S, TQ) + [qspec, tspec] + (x_in if ride else []),
        out_specs=[qspec, full, full] + (x_out if ride else []),
        out_shape=[sds, sds, sds] + (x_shapes if ride else []),
        scratch_shapes=accs + (x_sems if ride else []),
        compiler_params=_cparams(("arbitrary", "arbitrary"), has_side_effects=bool(ride)),
    )(qkv, qkv, qkv, d_cat, total, *(x_args if ride else []))


GLA_ROWS = 1024
GLA_GROUP_FWD = 8
GLA_GROUP_BWD = 4
GLA_PAIR_K = 2 * GLA_KEY_DIM
GLA_PAIR_V = 2 * GLA_VAL_DIM


def _log_sigmoid(x):
    return -_softplus(-x)


def _dot_split_lhs01(m01, x):
    hi = x.astype(BF16)
    lo = (x - hi.astype(F32)).astype(BF16)
    return lax.dot_general(jnp.concatenate([m01, m01], axis=1), jnp.concatenate([hi, lo], axis=0),
                           (((1,), (0,)), ((), ())), preferred_element_type=F32)


def _dot_split_tn(x, m01):
    hi = x.astype(BF16)
    lo = (x - hi.astype(F32)).astype(BF16)
    return lax.dot_general(jnp.concatenate([hi, lo], axis=0), jnp.concatenate([m01, m01], axis=0),
                           (((0,), (0,)), ((), ())), preferred_element_type=F32)


def _dot_split_nt01(m01, x):
    hi = x.astype(BF16)
    lo = (x - hi.astype(F32)).astype(BF16)
    return lax.dot_general(jnp.concatenate([m01, m01], axis=1), jnp.concatenate([hi, lo], axis=1),
                           (((1,), (1,)), ((), ())), preferred_element_type=F32)


def _gla_consts():
    C = CHUNK
    row = lax.broadcasted_iota(jnp.int32, (C, C), 0)
    col = lax.broadcasted_iota(jnp.int32, (C, C), 1)
    first = lax.broadcasted_iota(jnp.int32, (C, GLA_PAIR_K), 1) < GLA_KEY_DIM
    r = lax.broadcasted_iota(jnp.int32, (GLA_PAIR_K, GLA_PAIR_V), 0)
    c = lax.broadcasted_iota(jnp.int32, (GLA_PAIR_K, GLA_PAIR_V), 1)
    own = (r < GLA_KEY_DIM) == (c < GLA_VAL_DIM)
    rowk = lax.broadcasted_iota(jnp.int32, (C, GLA_PAIR_K), 0)
    return dict(row=row, col=col, first=first, own=own, rowk=rowk,
                incl=jnp.where(row >= col, 1.0, 0.0).astype(BF16),
                ones=jnp.ones((C, GLA_PAIR_V), BF16))


def _pack_state(state):
    top = lax.broadcasted_iota(jnp.int32, (GLA_PAIR_K, GLA_VAL_DIM), 0) < GLA_KEY_DIM
    return jnp.where(top, state[:, :GLA_VAL_DIM], state[:, GLA_VAL_DIM:])


def _unpack_state(packed):
    top = lax.broadcasted_iota(jnp.int32, (GLA_PAIR_K, GLA_VAL_DIM), 0) < GLA_KEY_DIM
    return jnp.concatenate([jnp.where(top, packed, 0.0), jnp.where(top, 0.0, packed)], axis=1)


def _gla_chunk(qc, kc, vc, gate, k):
    C = CHUNK
    row, col, first, rowk = k["row"], k["col"], k["first"], k["rowk"]
    la = _log_sigmoid(gate) * (1.0 / GLA_TAU)
    b = _dot_split_lhs01(k["incl"], la)
    b_ref = jnp.sum(jnp.where(rowk == C // 2 - 1, b, 0.0), axis=0, keepdims=True)
    b_last = jnp.sum(la, axis=0, keepdims=True)
    qs = qc * (GLA_KEY_DIM ** -0.5)
    q_in = qs * jnp.exp(b - b_ref)
    k_in = kc * jnp.exp(b_ref - b)
    k_dec = kc * jnp.exp(b_last - b)
    q_b = qs * jnp.exp(b)
    k_in_h = [jnp.where(first, k_in, 0.0), jnp.where(first, 0.0, k_in)]
    v_h = [vc[:, :GLA_VAL_DIM], vc[:, GLA_VAL_DIM:]]
    sc = [jnp.where(row >= col, _dot(q_in, k_in_h[hh], "nt"), 0.0) for hh in range(2)]
    o_intra = jnp.concatenate([_dot(sc[hh], v_h[hh], "nn") for hh in range(2)], axis=1)
    upd = jnp.where(k["own"], _dot(k_dec, vc, "tn"), 0.0)
    dec_col = jnp.exp(_dot_split_tn(la, k["ones"]))
    return dict(la=la, b=b, b_ref=b_ref, b_last=b_last, q_in=q_in, k_in=k_in, k_dec=k_dec, q_b=q_b,
                k_in_h=k_in_h, v_h=v_h, sc=sc, o_intra=o_intra, upd=upd, dec_col=dec_col)


def _rms_gate(o, gg):
    rinv = lax.rsqrt(jnp.mean(o * o, axis=-1, keepdims=True) + RMS_EPS)
    o_n = o * rinv
    sg = 1.0 / (1.0 + jnp.exp(-gg))
    return o_n, rinv, sg


def _gla_in_specs(rows_of, RB):
    PK, PV = GLA_PAIR_K, GLA_PAIR_V
    return [pl.BlockSpec((RB, PK), lambda i, p: (rows_of(i), p)),
            pl.BlockSpec((RB, PK), lambda i, p: (rows_of(i), (OFF_GK - OFF_GQ) // PK + p)),
            pl.BlockSpec((RB, PV), lambda i, p: (rows_of(i), (OFF_GV - OFF_GQ) // PV + p)),
            pl.BlockSpec((RB, PV), lambda i, p: (rows_of(i), (OFF_GG - OFF_GQ) // PV + p)),
            pl.BlockSpec((RB, LANE), lambda i, p: (rows_of(i), (OFF_GA - OFF_GQ) // LANE)),
            pl.BlockSpec((1, LANE, PK), lambda i, p: (p, 0, 0)),
            pl.BlockSpec((1, 1, PK), lambda i, p: (p, 0, 0)),
            pl.BlockSpec((1, GLA_VAL_DIM), lambda i, p: (0, 0))]


def _gla_fwd(proj, gate_up_p, gate_bias_p, gnorm, name):
    S = proj.shape[0]
    C, RB, PK, PV, dv = CHUNK, min(GLA_ROWS, S), GLA_PAIR_K, GLA_PAIR_V, GLA_VAL_DIM
    NP = GLA_HEADS // 2
    G = GLA_GROUP_FWD
    nchunk = S // C
    ngroup = RB // (C * G)

    def body(q_ref, k_ref, v_ref, gg_ref, ga_ref, gu_ref, gb_ref, gn_ref, o_ref, prev_ref, st_ref):
        i, p = pl.program_id(0), pl.program_id(1)
        k = _gla_consts()

        @pl.when(i == 0)
        def _():
            st_ref[p] = jnp.zeros((PK, PV), F32)

        def group(gi, state):
            for u in range(G):
                ci = gi * G + u
                rows = pl.ds(pl.multiple_of(ci * C, C), C)
                gate = _dot(ga_ref[rows, :], gu_ref[0], "nn") + gb_ref[0]
                f = _gla_chunk(q_ref[rows, :], k_ref[rows, :], v_ref[rows, :], gate, k)
                prev_ref[0, ci] = _pack_state(state)
                o = f["o_intra"] + _dot(f["q_b"], state, "nn")
                state = f["dec_col"] * state + f["upd"]
                ggv = gg_ref[rows, :]
                halves = []
                for hh in range(2):
                    lanes = slice(hh * dv, (hh + 1) * dv)
                    o_n, _, sg = _rms_gate(o[:, lanes], ggv[:, lanes])
                    halves.append(o_n * gn_ref[...] * (ggv[:, lanes] * sg))
                o_ref[rows, :] = jnp.concatenate(halves, axis=1).astype(o_ref.dtype)
            return state

        st_ref[p] = lax.fori_loop(0, ngroup, group, st_ref[p])

    return pl.pallas_call(
        body, name=name, grid=(S // RB, NP), in_specs=_gla_in_specs(lambda i: i, RB),
        out_specs=[pl.BlockSpec((RB, PV), lambda i, p: (i, p)),
                   pl.BlockSpec((1, RB // C, PK, dv), lambda i, p: (p, i, 0, 0))],
        out_shape=[jax.ShapeDtypeStruct((S, GLA_WIDTH), MXU_DTYPE),
                   jax.ShapeDtypeStruct((NP, nchunk, PK, dv), F32)],
        scratch_shapes=[pltpu.VMEM((NP, PK, PV), F32)],
        compiler_params=_cparams(("arbitrary", "arbitrary")),
    )(proj, proj, proj, proj, proj, gate_up_p, gate_bias_p, gnorm)


def _gla_bwd(proj, gate_up_p, gate_bias_p, gnorm, prev, d_cat, name):
    S = proj.shape[0]
    C, RB, PK, PV, dv = CHUNK, min(GLA_ROWS, S), GLA_PAIR_K, GLA_PAIR_V, GLA_VAL_DIM
    NP = GLA_HEADS // 2
    G = GLA_GROUP_BWD
    nb = S // RB
    ngroup = RB // (C * G)

    def body(q_ref, k_ref, v_ref, gg_ref, ga_ref, gu_ref, gb_ref, gn_ref, prev_ref, do_ref,
             dq_ref, dk_ref, dv_ref, dgg_ref, dga_ref, ggu_ref, ggb_ref, ggn_ref, st_ref):
        i, p = pl.program_id(0), pl.program_id(1)
        k = _gla_consts()
        row, col, first, rowk = k["row"], k["col"], k["first"], k["rowk"]
        gu = gu_ref[0]

        @pl.when(i == 0)
        def _():
            st_ref[p] = jnp.zeros((PK, PV), F32)
            ggu_ref[p] = jnp.zeros((LANE, PK), F32)
            ggb_ref[p] = jnp.zeros((1, PK), F32)

        @pl.when(jnp.logical_and(i == 0, p == 0))
        def _():
            ggn_ref[...] = jnp.zeros_like(ggn_ref)

        @pl.when(p == 0)
        def _():
            dga_ref[...] = jnp.zeros_like(dga_ref)

        upper_incl = jnp.where(col >= row, 1.0, 0.0).astype(BF16)
        ones_8 = jnp.ones((8, PV), BF16)

        def group(gn, dstate):
            gi = ngroup - 1 - gn
            for u in reversed(range(G)):
                ci = gi * G + u
                rows = pl.ds(pl.multiple_of(ci * C, C), C)
                ga = ga_ref[rows, :]
                gate = _dot(ga, gu, "nn") + gb_ref[0]
                vc = v_ref[rows, :]
                f = _gla_chunk(q_ref[rows, :], k_ref[rows, :], vc, gate, k)
                state = _unpack_state(prev_ref[0, ci])
                o = f["o_intra"] + _dot(f["q_b"], state, "nn")
                ggv = gg_ref[rows, :]
                dout = do_ref[rows, :]
                d_o_h, dgg_h = [], []
                for hh in range(2):
                    lanes = slice(hh * dv, (hh + 1) * dv)
                    o_n, rinv, sg = _rms_gate(o[:, lanes], ggv[:, lanes])
                    silu = ggv[:, lanes] * sg
                    dgg_h.append(dout[:, lanes] * o_n * gn_ref[...] * (sg * (1.0 + ggv[:, lanes] * (1.0 - sg))))
                    d_ong = dout[:, lanes] * silu
                    ggn_ref[...] += jnp.sum(d_ong * o_n, axis=0, keepdims=True)
                    d_on = d_ong * gn_ref[...]
                    d_o_h.append(rinv * (d_on - o_n * jnp.mean(d_on * o_n, axis=-1, keepdims=True)))
                dgg_ref[rows, :] = jnp.concatenate(dgg_h, axis=1).astype(dgg_ref.dtype)
                d_o = jnp.concatenate(d_o_h, axis=1)
                d_upd = jnp.where(k["own"], dstate, 0.0)
                d_dec_col = dstate * state * f["dec_col"]
                dstate = f["dec_col"] * dstate + _dot(f["q_b"], d_o, "tn")
                dsc = [jnp.where(row >= col, _dot(d_o_h[hh], f["v_h"][hh], "nt"), 0.0) for hh in range(2)]
                dv_ref[rows, :] = (jnp.concatenate([_dot(f["sc"][hh], d_o_h[hh], "tn") for hh in range(2)], axis=1)
                                   + _dot(f["k_dec"], d_upd, "nn")).astype(dv_ref.dtype)
                q_in_h = [jnp.where(first, f["q_in"], 0.0), jnp.where(first, 0.0, f["q_in"])]
                dq_in = _dot(dsc[0], f["k_in_h"][0], "nn") + _dot(dsc[1], f["k_in_h"][1], "nn")
                dk_in = _dot(dsc[0], q_in_h[0], "tn") + _dot(dsc[1], q_in_h[1], "tn")
                dq_b = _dot(d_o, state, "nt")
                dkdec = _dot(vc, d_upd, "nt")
                b = f["b"]
                e1 = jnp.exp(b - f["b_ref"])
                e2 = jnp.exp(f["b_ref"] - b)
                e3 = jnp.exp(f["b_last"] - b)
                eb = jnp.exp(b)
                dq_ref[rows, :] = ((dq_in * e1 + dq_b * eb) * (GLA_KEY_DIM ** -0.5)).astype(dq_ref.dtype)
                dk_ref[rows, :] = (dk_in * e2 + dkdec * e3).astype(dk_ref.dtype)
                t_q = dq_in * f["q_in"]
                t_k = dk_in * f["k_in"]
                t_d = dkdec * f["k_dec"]
                db = t_q - t_k - t_d + dq_b * f["q_b"]
                db_ref = jnp.sum(t_k - t_q, axis=0, keepdims=True)
                db_last = (jnp.sum(t_d, axis=0, keepdims=True)
                           + jnp.max(_dot_split_nt01(ones_8, d_dec_col), axis=0, keepdims=True))
                db = db + jnp.where(rowk == C // 2 - 1, db_ref, 0.0) + jnp.where(rowk == C - 1, db_last, 0.0)
                dla = _dot_split_lhs01(upper_incl, db)
                d_gate = dla * (1.0 / GLA_TAU) * (1.0 / (1.0 + jnp.exp(gate)))
                ggb_ref[p] += jnp.sum(d_gate, axis=0, keepdims=True)
                ggu_ref[p] += _dot(ga, d_gate, "tn")
                dga_ref[rows, :] += _dot(d_gate, gu, "nt")
            return dstate

        st_ref[p] = lax.fori_loop(0, ngroup, group, st_ref[p])

    back = lambda i: nb - 1 - i
    NPV = SB_WIDTH // PV
    return pl.pallas_call(
        body, name=name, grid=(nb, NP),
        in_specs=_gla_in_specs(back, RB) + [pl.BlockSpec((1, RB // C, PK, dv), lambda i, p: (p, back(i), 0, 0)),
                                            pl.BlockSpec((RB, PV), lambda i, p: (back(i), NPV + p))],
        out_specs=[pl.BlockSpec((RB, PK), lambda i, p: (back(i), p)),
                   pl.BlockSpec((RB, PK), lambda i, p: (back(i), p)),
                   pl.BlockSpec((RB, PV), lambda i, p: (back(i), p)),
                   pl.BlockSpec((RB, PV), lambda i, p: (back(i), p)),
                   pl.BlockSpec((RB, LANE), lambda i, p: (back(i), 0)),
                   pl.BlockSpec((NP, LANE, PK), lambda i, p: (0, 0, 0)),
                   pl.BlockSpec((NP, 1, PK), lambda i, p: (0, 0, 0)),
                   pl.BlockSpec((1, dv), lambda i, p: (0, 0))],
        out_shape=[jax.ShapeDtypeStruct((S, NP * PK), MXU_DTYPE), jax.ShapeDtypeStruct((S, NP * PK), MXU_DTYPE),
                   jax.ShapeDtypeStruct((S, GLA_WIDTH), MXU_DTYPE), jax.ShapeDtypeStruct((S, GLA_WIDTH), MXU_DTYPE),
                   jax.ShapeDtypeStruct((S, LANE), F32), jax.ShapeDtypeStruct((NP, LANE, PK), F32),
                   jax.ShapeDtypeStruct((NP, 1, PK), F32), jax.ShapeDtypeStruct((1, dv), F32)],
        scratch_shapes=[pltpu.VMEM((NP, PK, PV), F32)],
        compiler_params=_cparams(("arbitrary", "arbitrary")),
    )(proj, proj, proj, proj, proj, gate_up_p, gate_bias_p, gnorm, prev, d_cat)


def _exchange_copies(scatter_flags, ins, outs, send_sems, recv_sems, local_sems):
    n_peer = N_DEV - 1
    x, y, c = lax.axis_index("x"), lax.axis_index("y"), lax.axis_index("c")
    me = 4 * x + 2 * y + c
    copies = []
    for a, scatter in enumerate(scatter_flags):
        own = ins[a].at[me] if scatter else ins[a]
        copies.append(pltpu.make_async_copy(own, outs[a].at[me], local_sems.at[a]))
    for r in range(1, N_DEV):
        px = 1 - x if r & 4 else x
        py = 1 - y if r & 2 else y
        pc = 1 - c if r & 1 else c
        for a, scatter in enumerate(scatter_flags):
            src = ins[a].at[4 * px + 2 * py + pc] if scatter else ins[a]
            copies.append(pltpu.make_async_remote_copy(
                src_ref=src, dst_ref=outs[a].at[me],
                send_sem=send_sems.at[a * n_peer + r - 1], recv_sem=recv_sems.at[a * n_peer + r - 1],
                device_id=(px, py, pc), device_id_type=MESH_ID))
    return copies


def _exchange_shapes(items):
    out_shape = []
    for arr, scatter in items:
        shp = arr.shape if scatter else (N_DEV,) + arr.shape
        out_shape.append(jax.ShapeDtypeStruct(shp, arr.dtype))
    n = len(items)
    sems = [pltpu.SemaphoreType.DMA((n * (N_DEV - 1),)), pltpu.SemaphoreType.DMA((n * (N_DEV - 1),)),
            pltpu.SemaphoreType.DMA((n,))]
    return out_shape, sems


def _exchange(items, name):
    n = len(items)
    flags = [sc for _, sc in items]

    def body(*refs):
        copies = _exchange_copies(flags, refs[:n], refs[n:2 * n], *refs[2 * n:])
        for cp in copies:
            cp.start()
        for cp in copies:
            cp.wait()

    out_shape, sems = _exchange_shapes(items)
    any_spec = pl.BlockSpec(memory_space=pl.ANY)
    return pl.pallas_call(
        body, name=name, in_specs=[any_spec] * n, out_specs=[any_spec] * n, out_shape=out_shape,
        scratch_shapes=sems, compiler_params=pltpu.CompilerParams(has_side_effects=True),
    )(*[arr for arr, _ in items])


def _riding_exchange(items, grid):
    n = len(items)
    flags = [sc for _, sc in items]
    out_shape, sems = _exchange_shapes(items)
    any_spec = pl.BlockSpec(memory_space=pl.ANY)

    def wrap(body, n_in, n_out):
        def fused(*refs):
            ins = refs[:n_in]
            x_ins = refs[n_in:n_in + n]
            outs = refs[n_in + n:n_in + n + n_out]
            x_outs = refs[n_in + n + n_out:n_in + 2 * n + n_out]
            rest = refs[n_in + 2 * n + n_out:]
            x_sems, scratch = rest[len(rest) - 3:], rest[:len(rest) - 3]
            first = last = True
            for d, n_d in enumerate(grid):
                first = jnp.logical_and(first, pl.program_id(d) == 0)
                last = jnp.logical_and(last, pl.program_id(d) == n_d - 1)

            @pl.when(first)
            def _():
                for cp in _exchange_copies(flags, x_ins, x_outs, *x_sems):
                    cp.start()

            body(*ins, *outs, *scratch)

            @pl.when(last)
            def _():
                for cp in _exchange_copies(flags, x_ins, x_outs, *x_sems):
                    cp.wait()

        return fused

    return [arr for arr, _ in items], [any_spec] * n, [any_spec] * n, out_shape, sems, wrap


def _sum_devices(ref):
    g = ref[0].astype(F32)
    for q in range(1, N_DEV):
        g = g + ref[q].astype(F32)
    return g


def _adam_math(g, w, m, v):
    nm = ADAM_B1 * m + (1.0 - ADAM_B1) * g
    nv = ADAM_B2 * v + (1.0 - ADAM_B2) * (g * g)
    m_hat = nm / (1.0 - ADAM_B1 ** ADAM_STEP)
    v_hat = nv / (1.0 - ADAM_B2 ** ADAM_STEP)
    return -ADAM_LR * (m_hat / (jnp.sqrt(v_hat) + ADAM_EPS) + ADAM_WD * w), nm, nv


def _adamw(grecv, w, m, v, name):
    R, C = w.shape
    tile = _pick(R, (256, 176, 128)) if R * C > 65536 else R

    def body(gr_ref, w_ref, m_ref, v_ref, g_ref, d_ref, nm_ref, nv_ref):
        g = _sum_devices(gr_ref)
        g_ref[...] = g
        d_ref[...], nm_ref[...], nv_ref[...] = _adam_math(g, w_ref[...], m_ref[...], v_ref[...])

    blk = pl.BlockSpec((tile, C), lambda i: (i, 0))
    sds = jax.ShapeDtypeStruct((R, C), F32)
    return pl.pallas_call(
        body, name=name, grid=(R // tile,),
        in_specs=[pl.BlockSpec((N_DEV, tile, C), lambda i: (0, i, 0)), blk, blk, blk],
        out_specs=[blk, blk, blk, blk], out_shape=[sds, sds, sds, sds],
        compiler_params=_cparams(("parallel",)),
    )(grecv, w, m, v)


def _adamw_replicated(grecvs, loss_recv, ws, ms, vs, name):
    nt = len(ws)

    def body(*refs):
        gr, lr = refs[:nt], refs[nt]
        w, m, v = refs[nt + 1:2 * nt + 1], refs[2 * nt + 1:3 * nt + 1], refs[3 * nt + 1:4 * nt + 1]
        outs = refs[4 * nt + 1:]
        outs[0][...] = _sum_devices(lr)
        for t in range(nt):
            g_ref, d_ref, nm_ref, nv_ref = outs[1 + 4 * t:5 + 4 * t]
            g = _sum_devices(gr[t])
            g_ref[...] = g
            d_ref[...], nm_ref[...], nv_ref[...] = _adam_math(g, w[t][...], m[t][...], v[t][...])

    out_shape = [jax.ShapeDtypeStruct((1, LANE), F32)]
    for t in range(nt):
        out_shape += [jax.ShapeDtypeStruct(ws[t].shape, F32)] * 4
    outs = pl.pallas_call(body, name=name, out_shape=out_shape, compiler_params=_cparams())(
        *grecvs, loss_recv, *ws, *ms, *vs)
    return outs[0], [outs[1 + 4 * t:5 + 4 * t] for t in range(nt)]


def _shard_cols(g):
    rows, cols = g.shape
    return g.reshape(rows, N_DEV, cols // N_DEV).transpose(1, 0, 2)


def _unshard_cols(blocks):
    return blocks.transpose(1, 0, 2).reshape(blocks.shape[1], -1)


def _heads(t, n, d):
    return t.reshape(t.shape[0], n, d).transpose(1, 0, 2)


def _unheads(t):
    return t.transpose(1, 0, 2).reshape(t.shape[1], -1)


def kernel(x, w_in, gate_up, gate_bias, gla_norm_g, w_out, ln1_g, ln1_b, w_up, conv_w, conv_b, w_down, ln2_g, ln2_b, loss_target, m_w_in, m_gate_up, m_gate_bias, m_gla_norm_g, m_w_out, m_ln1_g, m_ln1_b, m_w_up, m_conv_w, m_conv_b, m_w_down, m_ln2_g, m_ln2_b, v_w_in, v_gate_up, v_gate_bias, v_gla_norm_g, v_w_out, v_ln1_g, v_ln1_b, v_w_up, v_conv_w, v_conv_b, v_w_down, v_ln2_g, v_ln2_b):
    S, D = x.shape[1], x.shape[2]
    x2, tgt = x[0], loss_target[0]

    gathered = _exchange([(w_in[0].astype(MXU_DTYPE), False), (gate_up[0], False), (conv_w[0], False)],
                         "gather_w_in")
    w_in_f = _unshard_cols(gathered[0])
    gate_up_f = _unshard_cols(gathered[1])
    conv_w_f = _unshard_cols(gathered[2])
    w_in_pad = jnp.pad(w_in_f, ((0, 0), (0, IN_PAD - IN_WIDTH)))
    gate_up_p = _heads(jnp.pad(gate_up_f, ((0, LANE - GLA_GATE_RANK), (0, 0))), GLA_HEADS // 2, GLA_PAIR_K)
    gate_bias_p = gate_bias.reshape(GLA_HEADS // 2, 1, GLA_PAIR_K)

    w_qkv = jnp.concatenate([w_in_f[:, :OFF_SBK] * (SB_HEAD_DIM ** -0.5), w_in_f[:, OFF_SBK:OFF_GQ]], axis=1)
    qkv = _matmul(x2, w_qkv, "nn", MXU_DTYPE, "proj_sb")
    proj = _matmul(x2, w_in_pad[:, OFF_GQ:], "nn", F32, "proj_gla")
    sb_o, sb_tot, g_out, g_up, g_down = _sb_fwd(
        qkv, "sb_fwd", ride=[(w_out[0].astype(MXU_DTYPE), False), (w_up[0].astype(MXU_DTYPE), False),
                             (w_down[0].astype(MXU_DTYPE), False)])
    w_out_f = g_out.reshape(-1, D)
    w_up_f = _unshard_cols(g_up)
    w_down_f = g_down.reshape(-1, D)
    gla_o, prev = _gla_fwd(proj, gate_up_p, gate_bias_p, gla_norm_g, "gla_fwd")
    cat = jnp.concatenate([sb_o, gla_o], axis=1)
    r1 = _matmul(cat, w_out_f, "nn", F32, "mix", res=x2, res_scale=DN_ALPHA)
    h = _ln_fwd(r1, ln1_g, ln1_b, "ln1")
    u0 = _matmul(h, w_up_f, "nn", F32, "ffn_up")
    p = _conv_gelu_fwd(u0, conv_w_f, conv_b, "conv_gelu")
    r2 = _matmul(p, w_down_f, "nn", F32, "ffn_down", res=h, res_scale=DN_ALPHA)
    d_r2, loss_p, g_ln2_g, g_ln2_b = _ln_loss_bwd(r2, tgt, ln2_g, ln2_b, "ln2_loss")

    d_p = _matmul(d_r2, w_down_f, "nt", MXU_DTYPE, "d_ffn_act")
    g_w_down = _matmul(p, d_r2, "tn", BF16, "grad_w_down")
    d_u0, g_conv_w, g_conv_b = _conv_gelu_bwd(u0, d_p, conv_w_f, conv_b, "conv_gelu_bwd")
    g_w_up = _matmul(h, d_u0, "tn", BF16, "grad_w_up")
    d_h = _matmul(d_u0, w_up_f, "nt", F32, "d_h", res=d_r2, res_scale=DN_ALPHA)
    d_r1, g_ln1_g, g_ln1_b = _ln_bwd(r1, d_h, ln1_g, "ln1_bwd")
    g_w_out = _matmul(cat, d_r1, "tn", BF16, "grad_w_out")
    d_cat = _matmul(d_r1, w_out_f, "nt", F32, "d_cat")
    (d_gq, d_gk, d_gv, d_gg, d_ga_pad, g_gu_p, g_gb_p, g_gnorm) = _gla_bwd(
        proj, gate_up_p, gate_bias_p, gla_norm_g, prev, d_cat, "gla_bwd")
    g_gate_up = _unheads(g_gu_p[:, :GLA_GATE_RANK, :])
    g_gate_bias = g_gb_p.reshape(1, -1)
    small_g = [g_gate_bias, g_gnorm, g_ln1_g, g_ln1_b, g_conv_b, g_ln2_g, g_ln2_b]
    d_sq, d_sk, d_sv, *recv_rest = _sb_bwd(
        qkv, d_cat, sb_tot, "sb_bwd",
        ride=[(g_w_out.reshape(N_DEV, -1, D), True), (_shard_cols(g_w_up), True),
              (g_w_down.reshape(N_DEV, -1, D), True), (_shard_cols(g_gate_up), True),
              (_shard_cols(g_conv_w), True)] + [(t, False) for t in small_g] + [(loss_p, False)])
    d_proj = jnp.concatenate([d_sq, d_sk, d_sv, d_gq, d_gk, d_gv, d_gg,
                              d_ga_pad.astype(MXU_DTYPE)], axis=1)
    g_w_in = _matmul(x2, d_proj, "tn", BF16, "grad_w_in")[:, :IN_WIDTH]
    d_x, recv_in = _matmul(d_proj, w_in_pad, "nt", F32, "d_x", res=d_r1, res_scale=DN_ALPHA,
                           ride=[(_shard_cols(g_w_in), True)])

    recv = [recv_in] + recv_rest[:5]
    sharded = [(w_in, m_w_in, v_w_in), (w_out, m_w_out, v_w_out), (w_up, m_w_up, v_w_up),
               (w_down, m_w_down, v_w_down), (gate_up, m_gate_up, v_gate_up), (conv_w, m_conv_w, v_conv_w)]
    upd = [_adamw(recv[n], w[0], m[0], v[0], "adamw_%d" % n) for n, (w, m, v) in enumerate(sharded)]
    loss_row, small = _adamw_replicated(
        recv_rest[5:12], recv_rest[12], [gate_bias, gla_norm_g, ln1_g, ln1_b, conv_b, ln2_g, ln2_b],
        [m_gate_bias, m_gla_norm_g, m_ln1_g, m_ln1_b, m_conv_b, m_ln2_g, m_ln2_b],
        [v_gate_bias, v_gla_norm_g, v_ln1_g, v_ln1_b, v_conv_b, v_ln2_g, v_ln2_b], "adamw_replicated")
    outs = []
    for kind in range(4):
        b_w_in, b_w_out, b_w_up, b_w_down, b_gate_up, b_conv_w = [u[kind][None] for u in upd]
        s_gb, s_gn, s_l1g, s_l1b, s_cb, s_l2g, s_l2b = [t[kind] for t in small]
        outs += [b_w_in, b_gate_up, s_gb, s_gn, b_w_out, s_l1g, s_l1b, b_w_up, b_conv_w, s_cb, b_w_down,
                 s_l2g, s_l2b]
    return (loss_row[0, 0], d_x[None], *outs)
```

```python
import math

import jax
import jax.numpy as jnp
from jax import lax
from jax.experimental import pallas as pl
from jax.experimental.pallas import tpu as pltpu

F32 = jnp.float32
BF16 = jnp.bfloat16
MXU_DTYPE = jnp.bfloat16

N_DEV = 8
D_MODEL = 1024
SB_WIDTH = 512
SB_HEADS = 8
SB_HEAD_DIM = 64
GLA_HEADS = 4
GLA_KEY_DIM = 64
GLA_VAL_DIM = 128
GLA_WIDTH = 512
GLA_GATE_RANK = 16
GLA_TAU = 16.0
CHUNK = 64
D_FF = 2816
CONV_WIDTH = 3
LN_EPS = 1e-5
RMS_EPS = 1e-6
DN_ALPHA = 2.0 ** 0.25
IN_WIDTH = 3088
LANE = 128
IN_PAD = 3200
OFF_SBQ, OFF_SBK, OFF_SBV = 0, 512, 1024
OFF_GQ, OFF_GK, OFF_GV, OFF_GG, OFF_GA = 1536, 1792, 2048, 2560, 3072
GLA_PAD = IN_PAD - OFF_GQ

ADAM_LR = 0.001
ADAM_B1 = 0.9
ADAM_B2 = 0.999
ADAM_EPS = 1e-08
ADAM_WD = 0.01
ADAM_STEP = 10

VMEM_LIMIT = 48 * 1024 * 1024
MESH_ID = pl.DeviceIdType.MESH


def _cparams(sem=None, **kw):
    return pltpu.CompilerParams(dimension_semantics=sem, vmem_limit_bytes=VMEM_LIMIT, **kw)


def _dot(a, b, dims):
    ca, cb = {"nn": (1, 0), "nt": (1, 1), "tn": (0, 0)}[dims]
    return lax.dot_general(a.astype(MXU_DTYPE), b.astype(MXU_DTYPE), (((ca,), (cb,)), ((), ())),
                           preferred_element_type=F32)


def _dot_split(a, b, dims):
    assert dims == "nn"
    hi = a.astype(BF16)
    lo = (a - hi.astype(F32)).astype(BF16)
    return lax.dot_general(jnp.concatenate([hi, lo], axis=1), jnp.concatenate([b, b], axis=0),
                           (((1,), (0,)), ((), ())), preferred_element_type=F32)


def _pick(dim, prefs):
    for p in prefs:
        if dim % p == 0:
            return p
    return dim


def _matmul(a, b, dims, out_dtype, name, res=None, res_scale=1.0, ride=()):
    if dims == "nn":
        (M, K), (_, N) = a.shape, b.shape
    elif dims == "nt":
        (M, K), (N, _) = a.shape, b.shape
    else:
        (K, M), (_, N) = a.shape, b.shape
    tm = _pick(M, (1024, 1408, 512, 256, 128))
    tn = _pick(N, (1408, 1024, 640, 512))
    if tn == N and N > 2048:
        tn = _pick(N, (256, 128))
    tk = _pick(K, (1024, 1408, 640, 512, 256, 128))
    nk = K // tk
    grid = (M // tm, N // tn, nk)
    if dims == "tn":
        a_spec = pl.BlockSpec((tk, tm), lambda i, j, k: (k, i))
    else:
        a_spec = pl.BlockSpec((tm, tk), lambda i, j, k: (i, k))
    if dims == "nt":
        b_spec = pl.BlockSpec((tn, tk), lambda i, j, k: (j, k))
    else:
        b_spec = pl.BlockSpec((tk, tn), lambda i, j, k: (k, j))
    o_spec = pl.BlockSpec((tm, tn), lambda i, j, k: (i, j))
    in_specs = [a_spec, b_spec]
    args = [a, b]
    if res is not None:
        in_specs.append(o_spec)
        args.append(res)

    def body(*refs):
        if res is not None:
            a_ref, b_ref, r_ref, o_ref, acc_ref = refs
        else:
            a_ref, b_ref, o_ref, acc_ref = refs
            r_ref = None
        k = pl.program_id(2)
        part = _dot(a_ref[...], b_ref[...], dims)

        def finish(total):
            if r_ref is not None:
                total = total + res_scale * r_ref[...]
            o_ref[...] = total.astype(o_ref.dtype)

        if nk == 1:
            finish(part)
        else:
            @pl.when(k == 0)
            def _():
                acc_ref[...] = part

            @pl.when(jnp.logical_and(k > 0, k < nk - 1))
            def _():
                acc_ref[...] += part

            @pl.when(k == nk - 1)
            def _():
                finish(acc_ref[...] + part)

    out_sds = jax.ShapeDtypeStruct((M, N), out_dtype)
    acc = pltpu.VMEM((tm, tn), F32)
    if not ride:
        return pl.pallas_call(
            body, name=name, grid=grid, in_specs=in_specs, out_specs=o_spec, out_shape=out_sds,
            scratch_shapes=[acc], compiler_params=_cparams(("parallel", "parallel", "arbitrary")),
        )(*args)
    x_args, x_in, x_out, x_shapes, x_sems, wrap = _riding_exchange(list(ride), grid)
    return pl.pallas_call(
        wrap(body, len(args), 1), name=name, grid=grid, in_specs=in_specs + x_in, out_specs=[o_spec] + x_out,
        out_shape=[out_sds] + x_shapes, scratch_shapes=[acc] + x_sems,
        compiler_params=_cparams(("arbitrary",) * 3, has_side_effects=True),
    )(*args, *x_args)


LN_ROWS = 256


def _ln_stats(r):
    mu = jnp.mean(r, axis=-1, keepdims=True)
    xc = r - mu
    var = jnp.mean(xc * xc, axis=-1, keepdims=True)
    return xc * lax.rsqrt(var + LN_EPS)


def _ln_fwd(r, g, b, name):
    S, D = r.shape

    def body(r_ref, g_ref, b_ref, h_ref):
        h_ref[...] = _ln_stats(r_ref[...]) * g_ref[...] + b_ref[...]

    row = pl.BlockSpec((LN_ROWS, D), lambda i: (i, 0))
    vec = pl.BlockSpec((1, D), lambda i: (0, 0))
    return pl.pallas_call(
        body, name=name, grid=(S // LN_ROWS,), in_specs=[row, vec, vec], out_specs=row,
        out_shape=jax.ShapeDtypeStruct((S, D), F32),
        compiler_params=_cparams(("parallel",)),
    )(r, g, b)


def _ln_bwd_core(xhat, dy, g):
    dxh = dy * g
    m1 = jnp.mean(dxh, axis=-1, keepdims=True)
    m2 = jnp.mean(dxh * xhat, axis=-1, keepdims=True)
    return dxh - m1 - xhat * m2


def _ln_bwd(r, dy, g, name):
    S, D = r.shape

    def body(r_ref, dy_ref, g_ref, dr_ref, gg_ref, gb_ref):
        x = r_ref[...]
        mu = jnp.mean(x, axis=-1, keepdims=True)
        xc = x - mu
        rstd = lax.rsqrt(jnp.mean(xc * xc, axis=-1, keepdims=True) + LN_EPS)
        xhat = xc * rstd
        dy = dy_ref[...]
        dr_ref[...] = rstd * _ln_bwd_core(xhat, dy, g_ref[...])

        @pl.when(pl.program_id(0) == 0)
        def _():
            gg_ref[...] = jnp.zeros_like(gg_ref)
            gb_ref[...] = jnp.zeros_like(gb_ref)

        gg_ref[...] += jnp.sum(dy * xhat, axis=0, keepdims=True)
        gb_ref[...] += jnp.sum(dy, axis=0, keepdims=True)

    row = pl.BlockSpec((LN_ROWS, D), lambda i: (i, 0))
    vec = pl.BlockSpec((1, D), lambda i: (0, 0))
    return pl.pallas_call(
        body, name=name, grid=(S // LN_ROWS,), in_specs=[row, row, vec], out_specs=[row, vec, vec],
        out_shape=[jax.ShapeDtypeStruct((S, D), F32), jax.ShapeDtypeStruct((1, D), F32),
                   jax.ShapeDtypeStruct((1, D), F32)],
        compiler_params=_cparams(("arbitrary",)),
    )(r, dy, g)


def _ln_loss_bwd(r, target, g, b, name):
    S, D = r.shape

    def body(r_ref, t_ref, g_ref, b_ref, dr_ref, loss_ref, gg_ref, gb_ref):
        x = r_ref[...]
        mu = jnp.mean(x, axis=-1, keepdims=True)
        xc = x - mu
        rstd = lax.rsqrt(jnp.mean(xc * xc, axis=-1, keepdims=True) + LN_EPS)
        xhat = xc * rstd
        y = xhat * g_ref[...] + b_ref[...]
        err = y - t_ref[...]
        dy = err * (1.0 / D)
        dr_ref[...] = rstd * _ln_bwd_core(xhat, dy, g_ref[...])

        @pl.when(pl.program_id(0) == 0)
        def _():
            loss_ref[...] = jnp.zeros_like(loss_ref)
            gg_ref[...] = jnp.zeros_like(gg_ref)
            gb_ref[...] = jnp.zeros_like(gb_ref)

        per_row = jnp.sum(err * err, axis=-1, keepdims=True) * (0.5 / D)
        loss_ref[...] += jnp.broadcast_to(jnp.sum(per_row, axis=0, keepdims=True), loss_ref.shape)
        gg_ref[...] += jnp.sum(dy * xhat, axis=0, keepdims=True)
        gb_ref[...] += jnp.sum(dy, axis=0, keepdims=True)

    row = pl.BlockSpec((LN_ROWS, D), lambda i: (i, 0))
    vec = pl.BlockSpec((1, D), lambda i: (0, 0))
    lvec = pl.BlockSpec((1, LANE), lambda i: (0, 0))
    return pl.pallas_call(
        body, name=name, grid=(S // LN_ROWS,), in_specs=[row, row, vec, vec],
        out_specs=[row, lvec, vec, vec],
        out_shape=[jax.ShapeDtypeStruct((S, D), F32), jax.ShapeDtypeStruct((1, LANE), F32),
                   jax.ShapeDtypeStruct((1, D), F32), jax.ShapeDtypeStruct((1, D), F32)],
        compiler_params=_cparams(("arbitrary",)),
    )(r, target, g, b)


CONV_COLS = 256
CONV_ROWS = 256
HALO = 8
INV_SQRT2 = 1.0 / math.sqrt(2.0)
INV_SQRT2PI = 1.0 / math.sqrt(2.0 * math.pi)


def _gelu(x):
    return 0.5 * x * (1.0 + lax.erf(x * INV_SQRT2))


def _gelu_and_grad(x):
    cdf = 0.5 * (1.0 + lax.erf(x * INV_SQRT2))
    return x * cdf, cdf + x * jnp.exp(-0.5 * x * x) * INV_SQRT2PI


def _conv_rows(ext, w_ref, b_ref, n):
    total = ext.shape[0]
    s1 = pltpu.roll(ext, 1, 0)
    s2 = pltpu.roll(ext, 2, 0)
    u = w_ref[2:3, :] * ext + w_ref[1:2, :] * s1 + w_ref[0:1, :] * s2 + b_ref[...]
    return u[HALO:total], s1[HALO:total], s2[HALO:total]


def _conv_gelu_fwd(u0, conv_w, conv_b, name):
    S, C2 = u0.shape
    F = C2 // 2
    ncb = F // CONV_COLS
    nrc = S // CONV_ROWS

    def body(ua_ref, uc_ref, wa_ref, wc_ref, ba_ref, bc_ref, p_ref):
        def chunk(ci, _):
            r0 = pl.multiple_of(ci * CONV_ROWS, CONV_ROWS)
            p0 = pl.multiple_of(jnp.maximum(r0 - HALO, 0), HALO)
            keep = (ci > 0).astype(F32)

            def load(ref):
                prev = ref[pl.ds(p0, HALO), :] * keep
                return jnp.concatenate([prev, ref[pl.ds(r0, CONV_ROWS), :]], axis=0)

            a, _, _ = _conv_rows(load(ua_ref), wa_ref, ba_ref, CONV_ROWS)
            c, _, _ = _conv_rows(load(uc_ref), wc_ref, bc_ref, CONV_ROWS)
            p_ref[pl.ds(r0, CONV_ROWS), :] = (_gelu(a) * c).astype(p_ref.dtype)
            return 0

        lax.fori_loop(0, nrc, chunk, 0)

    col_a = pl.BlockSpec((S, CONV_COLS), lambda j: (0, j))
    col_c = pl.BlockSpec((S, CONV_COLS), lambda j: (0, j + ncb))
    w_a = pl.BlockSpec((CONV_WIDTH, CONV_COLS), lambda j: (0, j))
    w_c = pl.BlockSpec((CONV_WIDTH, CONV_COLS), lambda j: (0, j + ncb))
    b_a = pl.BlockSpec((1, CONV_COLS), lambda j: (0, j))
    b_c = pl.BlockSpec((1, CONV_COLS), lambda j: (0, j + ncb))
    return pl.pallas_call(
        body, name=name, grid=(ncb,), in_specs=[col_a, col_c, w_a, w_c, b_a, b_c], out_specs=col_a,
        out_shape=jax.ShapeDtypeStruct((S, F), MXU_DTYPE),
        compiler_params=_cparams(("parallel",)),
    )(u0, u0, conv_w, conv_w, conv_b, conv_b)


def _conv_gelu_bwd(u0, dp, conv_w, conv_b, name):
    S, C2 = u0.shape
    F = C2 // 2
    ncb = F // CONV_COLS
    nrc = S // CONV_ROWS
    EXT = CONV_ROWS + HALO

    def body(ua_ref, uc_ref, dp_ref, wa_ref, wc_ref, ba_ref, bc_ref,
             da_ref, dc_ref, gwa_ref, gwc_ref, gba_ref, gbc_ref):
        gwa_ref[...] = jnp.zeros_like(gwa_ref)
        gwc_ref[...] = jnp.zeros_like(gwc_ref)
        gba_ref[...] = jnp.zeros_like(gba_ref)
        gbc_ref[...] = jnp.zeros_like(gbc_ref)
        rid = lax.broadcasted_iota(jnp.int32, (EXT, CONV_COLS), 0)

        def chunk(ci, _):
            r0 = pl.multiple_of(ci * CONV_ROWS, CONV_ROWS)
            p0 = pl.multiple_of(jnp.maximum(r0 - HALO, 0), HALO)
            n0 = pl.multiple_of(jnp.minimum(r0 + CONV_ROWS, S - HALO), HALO)
            keep_prev = (ci > 0).astype(F32)
            keep_next = (ci < nrc - 1).astype(F32)

            def load(ref):
                return jnp.concatenate([ref[pl.ds(p0, HALO), :] * keep_prev,
                                        ref[pl.ds(r0, CONV_ROWS), :],
                                        ref[pl.ds(n0, HALO), :] * keep_next], axis=0)

            ext_a = load(ua_ref)
            ext_c = load(uc_ref)
            a, a1, a2 = _conv_rows(ext_a, wa_ref, ba_ref, EXT)
            c, c1, c2 = _conv_rows(ext_c, wc_ref, bc_ref, EXT)
            a0 = ext_a[HALO:HALO + EXT]
            c0 = ext_c[HALO:HALO + EXT]
            dpe = jnp.concatenate([dp_ref[pl.ds(r0, CONV_ROWS), :].astype(F32),
                                   dp_ref[pl.ds(n0, HALO), :].astype(F32) * keep_next], axis=0)
            gelu_a, slope_a = _gelu_and_grad(a)
            d_a = dpe * c * slope_a
            d_c = dpe * gelu_a
            own = rid < CONV_ROWS

            def back(d_u, w_ref, x0, x1, x2, d_ref, gw_ref, gb_ref):
                d_u0 = (w_ref[2:3, :] * d_u + w_ref[1:2, :] * pltpu.roll(d_u, EXT - 1, 0)
                        + w_ref[0:1, :] * pltpu.roll(d_u, EXT - 2, 0))
                d_ref[pl.ds(r0, CONV_ROWS), :] = d_u0[0:CONV_ROWS].astype(d_ref.dtype)
                d_own = jnp.where(own, d_u, 0.0)
                gw_ref[...] += jnp.concatenate(
                    [jnp.sum(d_own * x2, axis=0, keepdims=True),
                     jnp.sum(d_own * x1, axis=0, keepdims=True),
                     jnp.sum(d_own * x0, axis=0, keepdims=True)], axis=0)
                gb_ref[...] += jnp.sum(d_own, axis=0, keepdims=True)

            back(d_a, wa_ref, a0, a1, a2, da_ref, gwa_ref, gba_ref)
            back(d_c, wc_ref, c0, c1, c2, dc_ref, gwc_ref, gbc_ref)
            return 0

        lax.fori_loop(0, nrc, chunk, 0)

    col_a = pl.BlockSpec((S, CONV_COLS), lambda j: (0, j))
    col_c = pl.BlockSpec((S, CONV_COLS), lambda j: (0, j + ncb))
    w_a = pl.BlockSpec((CONV_WIDTH, CONV_COLS), lambda j: (0, j))
    w_c = pl.BlockSpec((CONV_WIDTH, CONV_COLS), lambda j: (0, j + ncb))
    b_a = pl.BlockSpec((1, CONV_COLS), lambda j: (0, j))
    b_c = pl.BlockSpec((1, CONV_COLS), lambda j: (0, j + ncb))
    outs = pl.pallas_call(
        body, name=name, grid=(ncb,),
        in_specs=[col_a, col_c, col_a, w_a, w_c, b_a, b_c],
        out_specs=[col_a, col_a, w_a, w_a, b_a, b_a],
        out_shape=[jax.ShapeDtypeStruct((S, F), MXU_DTYPE), jax.ShapeDtypeStruct((S, F), MXU_DTYPE),
                   jax.ShapeDtypeStruct((CONV_WIDTH, F), F32), jax.ShapeDtypeStruct((CONV_WIDTH, F), F32),
                   jax.ShapeDtypeStruct((1, F), F32), jax.ShapeDtypeStruct((1, F), F32)],
        compiler_params=_cparams(("parallel",)),
    )(u0, u0, dp, conv_w, conv_w, conv_b, conv_b)
    da, dc, gwa, gwc, gba, gbc = outs
    return (jnp.concatenate([da, dc], axis=1), jnp.concatenate([gwa, gwc], axis=1),
            jnp.concatenate([gba, gbc], axis=1))


SB_TK = 128
SB_TQ_FWD = 1024
SB_TQ_BWD = 1024
SB_PAIRS_FWD = 2


def _softplus(z):
    return jnp.maximum(z, 0.0) + jnp.log(1.0 + jnp.exp(-jnp.abs(z)))


def _tri_ones(after):
    r = lax.broadcasted_iota(jnp.int32, (SB_TK, 2 * SB_TK), 0)
    c = lax.broadcasted_iota(jnp.int32, (SB_TK, 2 * SB_TK), 1)
    tri = (r > c) if after else (r < c)
    return jnp.where(c >= SB_TK, 1.0, jnp.where(tri, 1.0, 0.0)).astype(BF16)


def _sb_block_specs(S, TQ):
    NP = SB_WIDTH // LANE
    return [pl.BlockSpec((TQ, LANE), lambda p, i: (i, p)),
            pl.BlockSpec((S, LANE), lambda p, i: (0, NP + p)),
            pl.BlockSpec((S, LANE), lambda p, i: (0, 2 * NP + p))]


def _sb_fwd(qkv, name, ride=()):
    S = qkv.shape[0]
    TK = SB_TK
    TQ = min(SB_TQ_FWD, S)
    R = TQ // TK
    NP = SB_WIDTH // LANE
    PS = SB_PAIRS_FWD
    W = PS * LANE
    grid = (NP // PS, S // TQ)

    def body(q_ref, k_ref, v_ref, o_ref, t_ref):
        i = pl.program_id(1)
        row = lax.broadcasted_iota(jnp.int32, (TQ, TK), 0)
        col = lax.broadcasted_iota(jnp.int32, (TQ, TK), 1)
        after_ones = _tri_ones(True)
        first = lax.broadcasted_iota(jnp.int32, (TK, LANE), 1) < SB_HEAD_DIM

        def block(j, carry, r0):
            k0 = pl.multiple_of(j * TK, TK)
            masked = r0 is not None
            r0 = r0 or 0
            out = []
            for pr in range(PS):
                lanes = slice(pr * LANE, (pr + 1) * LANE)
                acc0, tail_a, tail_b = carry[3 * pr:3 * pr + 3]
                kab = k_ref[pl.ds(k0, TK), lanes]
                vab = v_ref[pl.ds(k0, TK), lanes]
                none = jnp.zeros_like(kab)
                k2 = jnp.concatenate([jnp.where(first, kab, none), jnp.where(first, none, kab)], axis=0)
                v2 = jnp.concatenate([jnp.where(first, vab, none), jnp.where(first, none, vab)], axis=0)
                z2 = _dot(q_ref[r0:, lanes], k2, "nt")
                tails, ws = [], []
                for hh, tail in enumerate((tail_a, tail_b)):
                    z = z2[:, hh * TK:(hh + 1) * TK]
                    sp = _softplus(z)
                    if masked:
                        strict = col[r0:] < row[r0:] - r0
                        sp = jnp.where(strict, sp, 0.0)
                    cs = _dot_split(sp, after_ones, "nn")
                    w = jnp.exp(z - sp - cs[:, :TK] - tail[r0:])
                    if masked:
                        w = jnp.where(strict, w, 0.0)
                    ws.append(w.astype(MXU_DTYPE))
                    tot = cs[:, TK:]
                    if r0:
                        tot = jnp.concatenate([jnp.zeros((r0, TK), F32), tot], axis=0)
                    tails.append(tail + tot)
                acc_r = acc0[r0:] + _dot(jnp.concatenate(ws, axis=1), v2, "nn")
                acc = jnp.concatenate([acc0[:r0], acc_r], axis=0) if r0 else acc_r
                out += [acc, tails[0], tails[1]]
            return tuple(out)

        carry = (jnp.zeros((TQ, LANE), F32), jnp.zeros((TQ, TK), F32), jnp.zeros((TQ, TK), F32)) * PS
        for u in reversed(range(R)):
            carry = block(R * i + u, carry, u * TK)
        carry = lax.fori_loop(0, R * i, lambda n, c: block(R * i - 1 - n, c, None), carry)
        for pr in range(PS):
            o_ref[:, pr * LANE:(pr + 1) * LANE] = carry[3 * pr].astype(o_ref.dtype)
            t_ref[2 * pr] = carry[3 * pr + 1]
            t_ref[2 * pr + 1] = carry[3 * pr + 2]

    in_specs = [pl.BlockSpec((TQ, W), lambda p, i: (i, p)),
                pl.BlockSpec((S, W), lambda p, i: (0, NP // PS + p)),
                pl.BlockSpec((S, W), lambda p, i: (0, 2 * (NP // PS) + p))]
    ospec = pl.BlockSpec((TQ, W), lambda p, i: (i, p))
    tspec = pl.BlockSpec((2 * PS, TQ, TK), lambda p, i: (p, i, 0))
    x_args, x_in, x_out, x_shapes, x_sems, wrap = _riding_exchange(list(ride), grid)
    return pl.pallas_call(
        wrap(body, 3, 2) if ride else body, name=name, grid=grid,
        in_specs=in_specs + (x_in if ride else []),
        out_specs=[ospec, tspec] + (x_out if ride else []),
        out_shape=[jax.ShapeDtypeStruct((S, SB_WIDTH), MXU_DTYPE), jax.ShapeDtypeStruct((SB_HEADS, S, TK), F32)]
        + (x_shapes if ride else []),
        scratch_shapes=x_sems if ride else [],
        compiler_params=_cparams(("arbitrary", "arbitrary"), has_side_effects=bool(ride)),
    )(qkv, qkv, qkv, *(x_args if ride else []))


def _sb_bwd(qkv, d_cat, total, name, ride=()):
    S = qkv.shape[0]
    TK = SB_TK
    TQ = min(SB_TQ_BWD, S)
    R = TQ // TK
    NP = SB_WIDTH // LANE
    grid = (NP, S // TQ)
    scale = SB_HEAD_DIM ** -0.5

    def body(q_ref, k_ref, v_ref, do_ref, t_ref, dq_ref, dk_ref, dv_ref, dk_acc, dv_acc):
        i = pl.program_id(1)

        @pl.when(i == 0)
        def _():
            dk_acc[...] = jnp.zeros_like(dk_acc)
            dv_acc[...] = jnp.zeros_like(dv_acc)

        row = lax.broadcasted_iota(jnp.int32, (TQ, TK), 0)
        col = lax.broadcasted_iota(jnp.int32, (TQ, TK), 1)
        after_ones = _tri_ones(True)
        before_ones = _tri_ones(False)
        first = lax.broadcasted_iota(jnp.int32, (TK, LANE), 1) < SB_HEAD_DIM
        qab = q_ref[...]
        doab = do_ref[...].astype(MXU_DTYPE)
        qdo = jnp.concatenate([qab, doab], axis=1)

        def block(j, carry, r0):
            k0 = pl.multiple_of(j * TK, TK)
            masked = r0 is not None
            r0 = r0 or 0
            kab = k_ref[pl.ds(k0, TK), :]
            vab = v_ref[pl.ds(k0, TK), :]
            none = jnp.zeros_like(kab)
            k_h = [jnp.where(first, kab, none), jnp.where(first, none, kab)]
            v_h = [jnp.where(first, vab, none), jnp.where(first, none, vab)]
            sums, dzs, ws = [], [], []
            for hh in range(2):
                seen, gsum = carry[1 + 2 * hh], carry[2 + 2 * hh]
                kv = jnp.concatenate([jnp.concatenate([k_h[hh], none], axis=1),
                                      jnp.concatenate([none, v_h[hh]], axis=1)], axis=0)
                zdw = _dot(qdo[r0:], kv, "nt")
                z = zdw[:, :TK]
                sp = _softplus(z)
                logsig = z - sp
                if masked:
                    strict = col[r0:] < row[r0:] - r0
                    sp = jnp.where(strict, sp, 0.0)
                cs = _dot_split(sp, after_ones, "nn")
                seen_r = seen[r0:] + cs[:, TK:]
                w = jnp.exp(logsig - cs[:, :TK] - (t_ref[hh, r0:, :] - seen_r))
                if masked:
                    w = jnp.where(strict, w, 0.0)
                g = w * zdw[:, TK:]
                cg = _dot_split(g, before_ones, "nn")
                dz = g - jnp.exp(logsig) * (g + cg[:, :TK] + gsum[r0:])
                if masked:
                    dz = jnp.where(strict, dz, 0.0)
                dzs.append(dz.astype(MXU_DTYPE))
                ws.append(w.astype(MXU_DTYPE))
                gsum_r = gsum[r0:] + cg[:, TK:]
                if r0:
                    seen_r = jnp.concatenate([seen[:r0], seen_r], axis=0)
                    gsum_r = jnp.concatenate([gsum[:r0], gsum_r], axis=0)
                sums += [seen_r, gsum_r]
            kvg = _dot(jnp.concatenate(dzs + ws, axis=1), qdo[r0:], "tn")
            dk_acc[pl.ds(k0, TK), :] += jnp.where(first, kvg[:TK, :LANE], kvg[TK:2 * TK, :LANE])
            dv_acc[pl.ds(k0, TK), :] += jnp.where(first, kvg[2 * TK:3 * TK, LANE:], kvg[3 * TK:, LANE:])
            dq_r = carry[0][r0:] + _dot(jnp.concatenate(dzs, axis=1), jnp.concatenate(k_h, axis=0), "nn")
            dq = jnp.concatenate([carry[0][:r0], dq_r], axis=0) if r0 else dq_r
            return (dq, *sums)

        zero = jnp.zeros((TQ, TK), F32)
        carry = (jnp.zeros((TQ, LANE), F32), zero, zero, zero, zero)
        carry = lax.fori_loop(0, R * i, lambda j, c: block(j, c, None), carry)
        for u in range(R):
            carry = block(R * i + u, carry, u * TK)
        dq_ref[...] = (carry[0] * scale).astype(dq_ref.dtype)

        @pl.when(i == S // TQ - 1)
        def _():
            dk_ref[...] = dk_acc[...].astype(dk_ref.dtype)
            dv_ref[...] = dv_acc[...].astype(dv_ref.dtype)

    qspec = pl.BlockSpec((TQ, LANE), lambda p, i: (i, p))
    full = pl.BlockSpec((S, LANE), lambda p, i: (0, p))
    tspec = pl.BlockSpec((2, TQ, TK), lambda p, i: (p, i, 0))
    sds = jax.ShapeDtypeStruct((S, SB_WIDTH), MXU_DTYPE)
    accs = [pltpu.VMEM((S, LANE), F32), pltpu.VMEM((S, LANE), F32)]
    x_args, x_in, x_out, x_shapes, x_sems, wrap = _riding_exchange(list(ride), grid)
    return pl.pallas_call(
        wrap(body, 5, 3) if ride else body, name=name, grid=grid,
        in_specs=_sb_block_specs(S, TQ) + [qspec, tspec] + (x_in if ride else []),
        out_specs=[qspec, full, full] + (x_out if ride else []),
        out_shape=[sds, sds, sds] + (x_shapes if ride else []),
        scratch_shapes=accs + (x_sems if ride else []),
        compiler_params=_cparams(("arbitrary", "arbitrary"), has_side_effects=bool(ride)),
    )(qkv, qkv, qkv, d_cat, total, *(x_args if ride else []))


GLA_ROWS = 1024
GLA_GROUP_FWD = 8
GLA_GROUP_BWD = 4
GLA_PAIR_K = 2 * GLA_KEY_DIM
GLA_PAIR_V = 2 * GLA_VAL_DIM


def _log_sigmoid(x):
    return -_softplus(-x)


def _dot_split_lhs01(m01, x):
    hi = x.astype(BF16)
    lo = (x - hi.astype(F32)).astype(BF16)
    return lax.dot_general(jnp.concatenate([m01, m01], axis=1), jnp.concatenate([hi, lo], axis=0),
                           (((1,), (0,)), ((), ())), preferred_element_type=F32)


def _dot_split_tn(x, m01):
    hi = x.astype(BF16)
    lo = (x - hi.astype(F32)).astype(BF16)
    return lax.dot_general(jnp.concatenate([hi, lo], axis=0), jnp.concatenate([m01, m01], axis=0),
                           (((0,), (0,)), ((), ())), preferred_element_type=F32)


def _dot_split_nt01(m01, x):
    hi = x.astype(BF16)
    lo = (x - hi.astype(F32)).astype(BF16)
    return lax.dot_general(jnp.concatenate([m01, m01], axis=1), jnp.concatenate([hi, lo], axis=1),
                           (((1,), (1,)), ((), ())), preferred_element_type=F32)


def _gla_consts():
    C = CHUNK
    row = lax.broadcasted_iota(jnp.int32, (C, C), 0)
    col = lax.broadcasted_iota(jnp.int32, (C, C), 1)
    first = lax.broadcasted_iota(jnp.int32, (C, GLA_PAIR_K), 1) < GLA_KEY_DIM
    r = lax.broadcasted_iota(jnp.int32, (GLA_PAIR_K, GLA_PAIR_V), 0)
    c = lax.broadcasted_iota(jnp.int32, (GLA_PAIR_K, GLA_PAIR_V), 1)
    own = (r < GLA_KEY_DIM) == (c < GLA_VAL_DIM)
    rowk = lax.broadcasted_iota(jnp.int32, (C, GLA_PAIR_K), 0)
    return dict(row=row, col=col, first=first, own=own, rowk=rowk,
                incl=jnp.where(row >= col, 1.0, 0.0).astype(BF16),
                ones=jnp.ones((C, GLA_PAIR_V), BF16))


def _pack_state(state):
    top = lax.broadcasted_iota(jnp.int32, (GLA_PAIR_K, GLA_VAL_DIM), 0) < GLA_KEY_DIM
    return jnp.where(top, state[:, :GLA_VAL_DIM], state[:, GLA_VAL_DIM:])


def _unpack_state(packed):
    top = lax.broadcasted_iota(jnp.int32, (GLA_PAIR_K, GLA_VAL_DIM), 0) < GLA_KEY_DIM
    return jnp.concatenate([jnp.where(top, packed, 0.0), jnp.where(top, 0.0, packed)], axis=1)


def _gla_chunk(qc, kc, vc, gate, k):
    C = CHUNK
    row, col, first, rowk = k["row"], k["col"], k["first"], k["rowk"]
    la = _log_sigmoid(gate) * (1.0 / GLA_TAU)
    b = _dot_split_lhs01(k["incl"], la)
    b_ref = jnp.sum(jnp.where(rowk == C // 2 - 1, b, 0.0), axis=0, keepdims=True)
    b_last = jnp.sum(la, axis=0, keepdims=True)
    qs = qc * (GLA_KEY_DIM ** -0.5)
    q_in = qs * jnp.exp(b - b_ref)
    k_in = kc * jnp.exp(b_ref - b)
    k_dec = kc * jnp.exp(b_last - b)
    q_b = qs * jnp.exp(b)
    k_in_h = [jnp.where(first, k_in, 0.0), jnp.where(first, 0.0, k_in)]
    v_h = [vc[:, :GLA_VAL_DIM], vc[:, GLA_VAL_DIM:]]
    sc = [jnp.where(row >= col, _dot(q_in, k_in_h[hh], "nt"), 0.0) for hh in range(2)]
    o_intra = jnp.concatenate([_dot(sc[hh], v_h[hh], "nn") for hh in range(2)], axis=1)
    upd = jnp.where(k["own"], _dot(k_dec, vc, "tn"), 0.0)
    dec_col = jnp.exp(_dot_split_tn(la, k["ones"]))
    return dict(la=la, b=b, b_ref=b_ref, b_last=b_last, q_in=q_in, k_in=k_in, k_dec=k_dec, q_b=q_b,
                k_in_h=k_in_h, v_h=v_h, sc=sc, o_intra=o_intra, upd=upd, dec_col=dec_col)


def _rms_gate(o, gg):
    rinv = lax.rsqrt(jnp.mean(o * o, axis=-1, keepdims=True) + RMS_EPS)
    o_n = o * rinv
    sg = 1.0 / (1.0 + jnp.exp(-gg))
    return o_n, rinv, sg


def _gla_in_specs(rows_of, RB):
    PK, PV = GLA_PAIR_K, GLA_PAIR_V
    return [pl.BlockSpec((RB, PK), lambda i, p: (rows_of(i), p)),
            pl.BlockSpec((RB, PK), lambda i, p: (rows_of(i), (OFF_GK - OFF_GQ) // PK + p)),
            pl.BlockSpec((RB, PV), lambda i, p: (rows_of(i), (OFF_GV - OFF_GQ) // PV + p)),
            pl.BlockSpec((RB, PV), lambda i, p: (rows_of(i), (OFF_GG - OFF_GQ) // PV + p)),
            pl.BlockSpec((RB, LANE), lambda i, p: (rows_of(i), (OFF_GA - OFF_GQ) // LANE)),
            pl.BlockSpec((1, LANE, PK), lambda i, p: (p, 0, 0)),
            pl.BlockSpec((1, 1, PK), lambda i, p: (p, 0, 0)),
            pl.BlockSpec((1, GLA_VAL_DIM), lambda i, p: (0, 0))]


def _gla_fwd(proj, gate_up_p, gate_bias_p, gnorm, name):
    S = proj.shape[0]
    C, RB, PK, PV, dv = CHUNK, min(GLA_ROWS, S), GLA_PAIR_K, GLA_PAIR_V, GLA_VAL_DIM
    NP = GLA_HEADS // 2
    G = GLA_GROUP_FWD
    nchunk = S // C
    ngroup = RB // (C * G)

    def body(q_ref, k_ref, v_ref, gg_ref, ga_ref, gu_ref, gb_ref, gn_ref, o_ref, prev_ref, st_ref):
        i, p = pl.program_id(0), pl.program_id(1)
        k = _gla_consts()

        @pl.when(i == 0)
        def _():
            st_ref[p] = jnp.zeros((PK, PV), F32)

        def group(gi, state):
            for u in range(G):
                ci = gi * G + u
                rows = pl.ds(pl.multiple_of(ci * C, C), C)
                gate = _dot(ga_ref[rows, :], gu_ref[0], "nn") + gb_ref[0]
                f = _gla_chunk(q_ref[rows, :], k_ref[rows, :], v_ref[rows, :], gate, k)
                prev_ref[0, ci] = _pack_state(state)
                o = f["o_intra"] + _dot(f["q_b"], state, "nn")
                state = f["dec_col"] * state + f["upd"]
                ggv = gg_ref[rows, :]
                halves = []
                for hh in range(2):
                    lanes = slice(hh * dv, (hh + 1) * dv)
                    o_n, _, sg = _rms_gate(o[:, lanes], ggv[:, lanes])
                    halves.append(o_n * gn_ref[...] * (ggv[:, lanes] * sg))
                o_ref[rows, :] = jnp.concatenate(halves, axis=1).astype(o_ref.dtype)
            return state

        st_ref[p] = lax.fori_loop(0, ngroup, group, st_ref[p])

    return pl.pallas_call(
        body, name=name, grid=(S // RB, NP), in_specs=_gla_in_specs(lambda i: i, RB),
        out_specs=[pl.BlockSpec((RB, PV), lambda i, p: (i, p)),
                   pl.BlockSpec((1, RB // C, PK, dv), lambda i, p: (p, i, 0, 0))],
        out_shape=[jax.ShapeDtypeStruct((S, GLA_WIDTH), MXU_DTYPE),
                   jax.ShapeDtypeStruct((NP, nchunk, PK, dv), F32)],
        scratch_shapes=[pltpu.VMEM((NP, PK, PV), F32)],
        compiler_params=_cparams(("arbitrary", "arbitrary")),
    )(proj, proj, proj, proj, proj, gate_up_p, gate_bias_p, gnorm)


def _gla_bwd(proj, gate_up_p, gate_bias_p, gnorm, prev, d_cat, name):
    S = proj.shape[0]
    C, RB, PK, PV, dv = CHUNK, min(GLA_ROWS, S), GLA_PAIR_K, GLA_PAIR_V, GLA_VAL_DIM
    NP = GLA_HEADS // 2
    G = GLA_GROUP_BWD
    nb = S // RB
    ngroup = RB // (C * G)

    def body(q_ref, k_ref, v_ref, gg_ref, ga_ref, gu_ref, gb_ref, gn_ref, prev_ref, do_ref,
             dq_ref, dk_ref, dv_ref, dgg_ref, dga_ref, ggu_ref, ggb_ref, ggn_ref, st_ref):
        i, p = pl.program_id(0), pl.program_id(1)
        k = _gla_consts()
        row, col, first, rowk = k["row"], k["col"], k["first"], k["rowk"]
        gu = gu_ref[0]

        @pl.when(i == 0)
        def _():
            st_ref[p] = jnp.zeros((PK, PV), F32)
            ggu_ref[p] = jnp.zeros((LANE, PK), F32)
            ggb_ref[p] = jnp.zeros((1, PK), F32)

        @pl.when(jnp.logical_and(i == 0, p == 0))
        def _():
            ggn_ref[...] = jnp.zeros_like(ggn_ref)

        @pl.when(p == 0)
        def _():
            dga_ref[...] = jnp.zeros_like(dga_ref)

        upper_incl = jnp.where(col >= row, 1.0, 0.0).astype(BF16)
        ones_8 = jnp.ones((8, PV), BF16)

        def group(gn, dstate):
            gi = ngroup - 1 - gn
            for u in reversed(range(G)):
                ci = gi * G + u
                rows = pl.ds(pl.multiple_of(ci * C, C), C)
                ga = ga_ref[rows, :]
                gate = _dot(ga, gu, "nn") + gb_ref[0]
                vc = v_ref[rows, :]
                f = _gla_chunk(q_ref[rows, :], k_ref[rows, :], vc, gate, k)
                state = _unpack_state(prev_ref[0, ci])
                o = f["o_intra"] + _dot(f["q_b"], state, "nn")
                ggv = gg_ref[rows, :]
                dout = do_ref[rows, :]
                d_o_h, dgg_h = [], []
                for hh in range(2):
                    lanes = slice(hh * dv, (hh + 1) * dv)
                    o_n, rinv, sg = _rms_gate(o[:, lanes], ggv[:, lanes])
                    silu = ggv[:, lanes] * sg
                    dgg_h.append(dout[:, lanes] * o_n * gn_ref[...] * (sg * (1.0 + ggv[:, lanes] * (1.0 - sg))))
                    d_ong = dout[:, lanes] * silu
                    ggn_ref[...] += jnp.sum(d_ong * o_n, axis=0, keepdims=True)
                    d_on = d_ong * gn_ref[...]
                    d_o_h.append(rinv * (d_on - o_n * jnp.mean(d_on * o_n, axis=-1, keepdims=True)))
                dgg_ref[rows, :] = jnp.concatenate(dgg_h, axis=1).astype(dgg_ref.dtype)
                d_o = jnp.concatenate(d_o_h, axis=1)
                d_upd = jnp.where(k["own"], dstate, 0.0)
                d_dec_col = dstate * state * f["dec_col"]
                dstate = f["dec_col"] * dstate + _dot(f["q_b"], d_o, "tn")
                dsc = [jnp.where(row >= col, _dot(d_o_h[hh], f["v_h"][hh], "nt"), 0.0) for hh in range(2)]
                dv_ref[rows, :] = (jnp.concatenate([_dot(f["sc"][hh], d_o_h[hh], "tn") for hh in range(2)], axis=1)
                                   + _dot(f["k_dec"], d_upd, "nn")).astype(dv_ref.dtype)
                q_in_h = [jnp.where(first, f["q_in"], 0.0), jnp.where(first, 0.0, f["q_in"])]
                dq_in = _dot(dsc[0], f["k_in_h"][0], "nn") + _dot(dsc[1], f["k_in_h"][1], "nn")
                dk_in = _dot(dsc[0], q_in_h[0], "tn") + _dot(dsc[1], q_in_h[1], "tn")
                dq_b = _dot(d_o, state, "nt")
                dkdec = _dot(vc, d_upd, "nt")
                b = f["b"]
                e1 = jnp.exp(b - f["b_ref"])
                e2 = jnp.exp(f["b_ref"] - b)
                e3 = jnp.exp(f["b_last"] - b)
                eb = jnp.exp(b)
                dq_ref[rows, :] = ((dq_in * e1 + dq_b * eb) * (GLA_KEY_DIM ** -0.5)).astype(dq_ref.dtype)
                dk_ref[rows, :] = (dk_in * e2 + dkdec * e3).astype(dk_ref.dtype)
                t_q = dq_in * f["q_in"]
                t_k = dk_in * f["k_in"]
                t_d = dkdec * f["k_dec"]
                db = t_q - t_k - t_d + dq_b * f["q_b"]
                db_ref = jnp.sum(t_k - t_q, axis=0, keepdims=True)
                db_last = (jnp.sum(t_d, axis=0, keepdims=True)
                           + jnp.max(_dot_split_nt01(ones_8, d_dec_col), axis=0, keepdims=True))
                db = db + jnp.where(rowk == C // 2 - 1, db_ref, 0.0) + jnp.where(rowk == C - 1, db_last, 0.0)
                dla = _dot_split_lhs01(upper_incl, db)
                d_gate = dla * (1.0 / GLA_TAU) * (1.0 / (1.0 + jnp.exp(gate)))
                ggb_ref[p] += jnp.sum(d_gate, axis=0, keepdims=True)
                ggu_ref[p] += _dot(ga, d_gate, "tn")
                dga_ref[rows, :] += _dot(d_gate, gu, "nt")
            return dstate

        st_ref[p] = lax.fori_loop(0, ngroup, group, st_ref[p])

    back = lambda i: nb - 1 - i
    NPV = SB_WIDTH // PV
    return pl.pallas_call(
        body, name=name, grid=(nb, NP),
        in_specs=_gla_in_specs(back, RB) + [pl.BlockSpec((1, RB // C, PK, dv), lambda i, p: (p, back(i), 0, 0)),
                                            pl.BlockSpec((RB, PV), lambda i, p: (back(i), NPV + p))],
        out_specs=[pl.BlockSpec((RB, PK), lambda i, p: (back(i), p)),
                   pl.BlockSpec((RB, PK), lambda i, p: (back(i), p)),
                   pl.BlockSpec((RB, PV), lambda i, p: (back(i), p)),
                   pl.BlockSpec((RB, PV), lambda i, p: (back(i), p)),
                   pl.BlockSpec((RB, LANE), lambda i, p: (back(i), 0)),
                   pl.BlockSpec((NP, LANE, PK), lambda i, p: (0, 0, 0)),
                   pl.BlockSpec((NP, 1, PK), lambda i, p: (0, 0, 0)),
                   pl.BlockSpec((1, dv), lambda i, p: (0, 0))],
        out_shape=[jax.ShapeDtypeStruct((S, NP * PK), MXU_DTYPE), jax.ShapeDtypeStruct((S, NP * PK), MXU_DTYPE),
                   jax.ShapeDtypeStruct((S, GLA_WIDTH), MXU_DTYPE), jax.ShapeDtypeStruct((S, GLA_WIDTH), MXU_DTYPE),
                   jax.ShapeDtypeStruct((S, LANE), F32), jax.ShapeDtypeStruct((NP, LANE, PK), F32),
                   jax.ShapeDtypeStruct((NP, 1, PK), F32), jax.ShapeDtypeStruct((1, dv), F32)],
        scratch_shapes=[pltpu.VMEM((NP, PK, PV), F32)],
        compiler_params=_cparams(("arbitrary", "arbitrary")),
    )(proj, proj, proj, proj, proj, gate_up_p, gate_bias_p, gnorm, prev, d_cat)


def _exchange_copies(scatter_flags, ins, outs, send_sems, recv_sems, local_sems):
    n_peer = N_DEV - 1
    x, y, c = lax.axis_index("x"), lax.axis_index("y"), lax.axis_index("c")
    me = 4 * x + 2 * y + c
    copies = []
    for a, scatter in enumerate(scatter_flags):
        own = ins[a].at[me] if scatter else ins[a]
        copies.append(pltpu.make_async_copy(own, outs[a].at[me], local_sems.at[a]))
    for r in range(1, N_DEV):
        px = 1 - x if r & 4 else x
        py = 1 - y if r & 2 else y
        pc = 1 - c if r & 1 else c
        for a, scatter in enumerate(scatter_flags):
            src = ins[a].at[4 * px + 2 * py + pc] if scatter else ins[a]
            copies.append(pltpu.make_async_remote_copy(
                src_ref=src, dst_ref=outs[a].at[me],
                send_sem=send_sems.at[a * n_peer + r - 1], recv_sem=recv_sems.at[a * n_peer + r - 1],
                device_id=(px, py, pc), device_id_type=MESH_ID))
    return copies


def _exchange_shapes(items):
    out_shape = []
    for arr, scatter in items:
        shp = arr.shape if scatter else (N_DEV,) + arr.shape
        out_shape.append(jax.ShapeDtypeStruct(shp, arr.dtype))
    n = len(items)
    sems = [pltpu.SemaphoreType.DMA((n * (N_DEV - 1),)), pltpu.SemaphoreType.DMA((n * (N_DEV - 1),)),
            pltpu.SemaphoreType.DMA((n,))]
    return out_shape, sems


def _exchange(items, name):
    n = len(items)
    flags = [sc for _, sc in items]

    def body(*refs):
        copies = _exchange_copies(flags, refs[:n], refs[n:2 * n], *refs[2 * n:])
        for cp in copies:
            cp.start()
        for cp in copies:
            cp.wait()

    out_shape, sems = _exchange_shapes(items)
    any_spec = pl.BlockSpec(memory_space=pl.ANY)
    return pl.pallas_call(
        body, name=name, in_specs=[any_spec] * n, out_specs=[any_spec] * n, out_shape=out_shape,
        scratch_shapes=sems, compiler_params=pltpu.CompilerParams(has_side_effects=True),
    )(*[arr for arr, _ in items])


def _all_gather_two_level(arrays, name):
    n = len(arrays)
    K = N_DEV - 1

    def body(*refs):
        ins, outs = refs[:n], refs[n:2 * n]
        send_sems, recv_sems, local_sems = refs[2 * n:]
        x, y, c = lax.axis_index("x"), lax.axis_index("y"), lax.axis_index("c")
        sibling = (x, y, 1 - c)
        chips = [(1 - x, y), (x, 1 - y), (1 - x, 1 - y)]

        def slot(px, py, pc):
            return 4 * px + 2 * py + pc

        def copy(a, k, block, to, src=None):
            rows = outs[a].at[slot(*block)]
            return pltpu.make_async_remote_copy(
                src_ref=rows if src is None else src, dst_ref=rows,
                send_sem=send_sems.at[a * K + k], recv_sem=recv_sems.at[a * K + k],
                device_id=to, device_id_type=MESH_ID)

        me = (x, y, c)
        local, first, passed = [], [], []
        for a in range(n):
            cp = pltpu.make_async_copy(ins[a], outs[a].at[slot(*me)], local_sems.at[a])
            cp.start()
            local.append(cp)
            first.append(copy(a, 0, me, sibling, src=ins[a]))
            first += [copy(a, 1 + j, me, (*chip, c), src=ins[a]) for j, chip in enumerate(chips)]
        for cp in first:
            cp.start()
        for j, chip in enumerate(chips):
            for a in range(n):
                copy(a, 1 + j, (*chip, c), me).wait_recv()
                cp = copy(a, 4 + j, (*chip, c), sibling)
                cp.start()
                passed.append(cp)
        for a in range(n):
            copy(a, 0, sibling, me).wait_recv()
            for j, chip in enumerate(chips):
                copy(a, 4 + j, (*chip, 1 - c), me).wait_recv()
        for cp in first + passed:
            cp.wait_send()
        for cp in local:
            cp.wait()

    out_shape, sems = _exchange_shapes([(arr, False) for arr in arrays])
    any_spec = pl.BlockSpec(memory_space=pl.ANY)
    return pl.pallas_call(
        body, name=name, in_specs=[any_spec] * n, out_specs=[any_spec] * n, out_shape=out_shape,
        scratch_shapes=sems, compiler_params=pltpu.CompilerParams(has_side_effects=True),
    )(*arrays)


def _riding_exchange(items, grid):
    n = len(items)
    flags = [sc for _, sc in items]
    out_shape, sems = _exchange_shapes(items)
    any_spec = pl.BlockSpec(memory_space=pl.ANY)

    def wrap(body, n_in, n_out):
        def fused(*refs):
            ins = refs[:n_in]
            x_ins = refs[n_in:n_in + n]
            outs = refs[n_in + n:n_in + n + n_out]
            x_outs = refs[n_in + n + n_out:n_in + 2 * n + n_out]
            rest = refs[n_in + 2 * n + n_out:]
            x_sems, scratch = rest[len(rest) - 3:], rest[:len(rest) - 3]
            first = last = True
            for d, n_d in enumerate(grid):
                first = jnp.logical_and(first, pl.program_id(d) == 0)
                last = jnp.logical_and(last, pl.program_id(d) == n_d - 1)

            @pl.when(first)
            def _():
                for cp in _exchange_copies(flags, x_ins, x_outs, *x_sems):
                    cp.start()

            body(*ins, *outs, *scratch)

            @pl.when(last)
            def _():
                for cp in _exchange_copies(flags, x_ins, x_outs, *x_sems):
                    cp.wait()

        return fused

    return [arr for arr, _ in items], [any_spec] * n, [any_spec] * n, out_shape, sems, wrap


def _sum_devices(ref):
    g = ref[0].astype(F32)
    for q in range(1, N_DEV):
        g = g + ref[q].astype(F32)
    return g


def _adam_math(g, w, m, v):
    nm = ADAM_B1 * m + (1.0 - ADAM_B1) * g
    nv = ADAM_B2 * v + (1.0 - ADAM_B2) * (g * g)
    m_hat = nm / (1.0 - ADAM_B1 ** ADAM_STEP)
    v_hat = nv / (1.0 - ADAM_B2 ** ADAM_STEP)
    return -ADAM_LR * (m_hat / (jnp.sqrt(v_hat) + ADAM_EPS) + ADAM_WD * w), nm, nv


def _adamw(grecv, w, m, v, name):
    R, C = w.shape
    tile = _pick(R, (256, 176, 128)) if R * C > 65536 else R

    def body(gr_ref, w_ref, m_ref, v_ref, g_ref, d_ref, nm_ref, nv_ref):
        g = _sum_devices(gr_ref)
        g_ref[...] = g
        d_ref[...], nm_ref[...], nv_ref[...] = _adam_math(g, w_ref[...], m_ref[...], v_ref[...])

    blk = pl.BlockSpec((tile, C), lambda i: (i, 0))
    sds = jax.ShapeDtypeStruct((R, C), F32)
    return pl.pallas_call(
        body, name=name, grid=(R // tile,),
        in_specs=[pl.BlockSpec((N_DEV, tile, C), lambda i: (0, i, 0)), blk, blk, blk],
        out_specs=[blk, blk, blk, blk], out_shape=[sds, sds, sds, sds],
        compiler_params=_cparams(("parallel",)),
    )(grecv, w, m, v)


def _adamw_replicated(grecvs, loss_recv, ws, ms, vs, name):
    nt = len(ws)

    def body(*refs):
        gr, lr = refs[:nt], refs[nt]
        w, m, v = refs[nt + 1:2 * nt + 1], refs[2 * nt + 1:3 * nt + 1], refs[3 * nt + 1:4 * nt + 1]
        outs = refs[4 * nt + 1:]
        outs[0][...] = _sum_devices(lr)
        for t in range(nt):
            g_ref, d_ref, nm_ref, nv_ref = outs[1 + 4 * t:5 + 4 * t]
            g = _sum_devices(gr[t])
            g_ref[...] = g
            d_ref[...], nm_ref[...], nv_ref[...] = _adam_math(g, w[t][...], m[t][...], v[t][...])

    out_shape = [jax.ShapeDtypeStruct((1, LANE), F32)]
    for t in range(nt):
        out_shape += [jax.ShapeDtypeStruct(ws[t].shape, F32)] * 4
    outs = pl.pallas_call(body, name=name, out_shape=out_shape, compiler_params=_cparams())(
        *grecvs, loss_recv, *ws, *ms, *vs)
    return outs[0], [outs[1 + 4 * t:5 + 4 * t] for t in range(nt)]


def _shard_cols(g):
    rows, cols = g.shape
    return g.reshape(rows, N_DEV, cols // N_DEV).transpose(1, 0, 2)


def _unshard_cols(blocks):
    return blocks.transpose(1, 0, 2).reshape(blocks.shape[1], -1)


def _heads(t, n, d):
    return t.reshape(t.shape[0], n, d).transpose(1, 0, 2)


def _unheads(t):
    return t.transpose(1, 0, 2).reshape(t.shape[1], -1)


def kernel(x, w_in, gate_up, gate_bias, gla_norm_g, w_out, ln1_g, ln1_b, w_up, conv_w, conv_b, w_down, ln2_g, ln2_b, loss_target, m_w_in, m_gate_up, m_gate_bias, m_gla_norm_g, m_w_out, m_ln1_g, m_ln1_b, m_w_up, m_conv_w, m_conv_b, m_w_down, m_ln2_g, m_ln2_b, v_w_in, v_gate_up, v_gate_bias, v_gla_norm_g, v_w_out, v_ln1_g, v_ln1_b, v_w_up, v_conv_w, v_conv_b, v_w_down, v_ln2_g, v_ln2_b):
    S, D = x.shape[1], x.shape[2]
    x2, tgt = x[0], loss_target[0]

    gathered = _all_gather_two_level([w_in[0].astype(MXU_DTYPE), gate_up[0], conv_w[0]], "gather_w_in")
    w_in_f = _unshard_cols(gathered[0])
    gate_up_f = _unshard_cols(gathered[1])
    conv_w_f = _unshard_cols(gathered[2])
    w_in_pad = jnp.pad(w_in_f, ((0, 0), (0, IN_PAD - IN_WIDTH)))
    gate_up_p = _heads(jnp.pad(gate_up_f, ((0, LANE - GLA_GATE_RANK), (0, 0))), GLA_HEADS // 2, GLA_PAIR_K)
    gate_bias_p = gate_bias.reshape(GLA_HEADS // 2, 1, GLA_PAIR_K)

    w_qkv = jnp.concatenate([w_in_f[:, :OFF_SBK] * (SB_HEAD_DIM ** -0.5), w_in_f[:, OFF_SBK:OFF_GQ]], axis=1)
    qkv = _matmul(x2, w_qkv, "nn", MXU_DTYPE, "proj_sb")
    proj = _matmul(x2, w_in_pad[:, OFF_GQ:], "nn", F32, "proj_gla")
    sb_o, sb_tot, g_out, g_up, g_down = _sb_fwd(
        qkv, "sb_fwd", ride=[(w_out[0].astype(MXU_DTYPE), False), (w_up[0].astype(MXU_DTYPE), False),
                             (w_down[0].astype(MXU_DTYPE), False)])
    w_out_f = g_out.reshape(-1, D)
    w_up_f = _unshard_cols(g_up)
    w_down_f = g_down.reshape(-1, D)
    gla_o, prev = _gla_fwd(proj, gate_up_p, gate_bias_p, gla_norm_g, "gla_fwd")
    cat = jnp.concatenate([sb_o, gla_o], axis=1)
    r1 = _matmul(cat, w_out_f, "nn", F32, "mix", res=x2, res_scale=DN_ALPHA)
    h = _ln_fwd(r1, ln1_g, ln1_b, "ln1")
    u0 = _matmul(h, w_up_f, "nn", F32, "ffn_up")
    p = _conv_gelu_fwd(u0, conv_w_f, conv_b, "conv_gelu")
    r2 = _matmul(p, w_down_f, "nn", F32, "ffn_down", res=h, res_scale=DN_ALPHA)
    d_r2, loss_p, g_ln2_g, g_ln2_b = _ln_loss_bwd(r2, tgt, ln2_g, ln2_b, "ln2_loss")

    d_p = _matmul(d_r2, w_down_f, "nt", MXU_DTYPE, "d_ffn_act")
    g_w_down = _matmul(p, d_r2, "tn", BF16, "grad_w_down")
    d_u0, g_conv_w, g_conv_b = _conv_gelu_bwd(u0, d_p, conv_w_f, conv_b, "conv_gelu_bwd")
    g_w_up = _matmul(h, d_u0, "tn", BF16, "grad_w_up")
    d_h = _matmul(d_u0, w_up_f, "nt", F32, "d_h", res=d_r2, res_scale=DN_ALPHA)
    d_r1, g_ln1_g, g_ln1_b = _ln_bwd(r1, d_h, ln1_g, "ln1_bwd")
    g_w_out = _matmul(cat, d_r1, "tn", BF16, "grad_w_out")
    d_cat = _matmul(d_r1, w_out_f, "nt", F32, "d_cat")
    (d_gq, d_gk, d_gv, d_gg, d_ga_pad, g_gu_p, g_gb_p, g_gnorm) = _gla_bwd(
        proj, gate_up_p, gate_bias_p, gla_norm_g, prev, d_cat, "gla_bwd")
    g_gate_up = _unheads(g_gu_p[:, :GLA_GATE_RANK, :])
    g_gate_bias = g_gb_p.reshape(1, -1)
    small_g = [g_gate_bias, g_gnorm, g_ln1_g, g_ln1_b, g_conv_b, g_ln2_g, g_ln2_b]
    d_sq, d_sk, d_sv, *recv_rest = _sb_bwd(
        qkv, d_cat, sb_tot, "sb_bwd",
        ride=[(g_w_out.reshape(N_DEV, -1, D), True), (_shard_cols(g_w_up), True),
              (g_w_down.reshape(N_DEV, -1, D), True), (_shard_cols(g_gate_up), True),
              (_shard_cols(g_conv_w), True)] + [(t, False) for t in small_g] + [(loss_p, False)])
    d_proj = jnp.concatenate([d_sq, d_sk, d_sv, d_gq, d_gk, d_gv, d_gg,
                              d_ga_pad.astype(MXU_DTYPE)], axis=1)
    g_w_in = _matmul(x2, d_proj, "tn", BF16, "grad_w_in")[:, :IN_WIDTH]
    d_x, recv_in = _matmul(d_proj, w_in_pad, "nt", F32, "d_x", res=d_r1, res_scale=DN_ALPHA,
                           ride=[(_shard_cols(g_w_in), True)])

    recv = [recv_in] + recv_rest[:5]
    sharded = [(w_in, m_w_in, v_w_in), (w_out, m_w_out, v_w_out), (w_up, m_w_up, v_w_up),
               (w_down, m_w_down, v_w_down), (gate_up, m_gate_up, v_gate_up), (conv_w, m_conv_w, v_conv_w)]
    upd = [_adamw(recv[n], w[0], m[0], v[0], "adamw_%d" % n) for n, (w, m, v) in enumerate(sharded)]
    loss_row, small = _adamw_replicated(
        recv_rest[5:12], recv_rest[12], [gate_bias, gla_norm_g, ln1_g, ln1_b, conv_b, ln2_g, ln2_b],
        [m_gate_bias, m_gla_norm_g, m_ln1_g, m_ln1_b, m_conv_b, m_ln2_g, m_ln2_b],
        [v_gate_bias, v_gla_norm_g, v_ln1_g, v_ln1_b, v_conv_b, v_ln2_g, v_ln2_b], "adamw_replicated")
    outs = []
    for kind in range(4):
        b_w_in, b_w_out, b_w_up, b_w_down, b_gate_up, b_conv_w = [u[kind][None] for u in upd]
        s_gb, s_gn, s_l1g, s_l1b, s_cb, s_l2g, s_l2b = [t[kind] for t in small]
        outs += [b_w_in, b_gate_up, s_gb, s_gn, b_w_out, s_l1g, s_l1b, b_w_up, b_conv_w, s_cb, b_w_down,
                 s_l2g, s_l2b]
    return (loss_row[0, 0], d_x[None], *outs)
```

```python
import math

import jax
import jax.numpy as jnp
from jax import lax
from jax.experimental import pallas as pl
from jax.experimental.pallas import tpu as pltpu

F32 = jnp.float32
BF16 = jnp.bfloat16
MXU_DTYPE = jnp.bfloat16

N_DEV = 8
D_MODEL = 1024
SB_WIDTH = 512
SB_HEADS = 8
SB_HEAD_DIM = 64
GLA_HEADS = 4
GLA_KEY_DIM = 64
GLA_VAL_DIM = 128
GLA_WIDTH = 512
GLA_GATE_RANK = 16
GLA_TAU = 16.0
CHUNK = 64
D_FF = 2816
CONV_WIDTH = 3
LN_EPS = 1e-5
RMS_EPS = 1e-6
DN_ALPHA = 2.0 ** 0.25
IN_WIDTH = 3088
LANE = 128
IN_PAD = 3200
OFF_SBQ, OFF_SBK, OFF_SBV = 0, 512, 1024
OFF_GQ, OFF_GK, OFF_GV, OFF_GG, OFF_GA = 1536, 1792, 2048, 2560, 3072
GLA_PAD = IN_PAD - OFF_GQ

ADAM_LR = 0.001
ADAM_B1 = 0.9
ADAM_B2 = 0.999
ADAM_EPS = 1e-08
ADAM_WD = 0.01
ADAM_STEP = 10

VMEM_LIMIT = 48 * 1024 * 1024
MESH_ID = pl.DeviceIdType.MESH


def _cparams(sem=None, **kw):
    return pltpu.CompilerParams(dimension_semantics=sem, vmem_limit_bytes=VMEM_LIMIT, **kw)


def _dot(a, b, dims):
    ca, cb = {"nn": (1, 0), "nt": (1, 1), "tn": (0, 0)}[dims]
    return lax.dot_general(a.astype(MXU_DTYPE), b.astype(MXU_DTYPE), (((ca,), (cb,)), ((), ())),
                           preferred_element_type=F32)


def _dot_split(a, b, dims):
    assert dims == "nn"
    hi = a.astype(BF16)
    lo = (a - hi.astype(F32)).astype(BF16)
    return lax.dot_general(jnp.concatenate([hi, lo], axis=1), jnp.concatenate([b, b], axis=0),
                           (((1,), (0,)), ((), ())), preferred_element_type=F32)


def _pick(dim, prefs):
    for p in prefs:
        if dim % p == 0:
            return p
    return dim


def _matmul(a, b, dims, out_dtype, name, res=None, res_scale=1.0, ride=()):
    if dims == "nn":
        (M, K), (_, N) = a.shape, b.shape
    elif dims == "nt":
        (M, K), (N, _) = a.shape, b.shape
    else:
        (K, M), (_, N) = a.shape, b.shape
    tm = _pick(M, (1024, 1408, 512, 256, 128))
    tn = _pick(N, (1408, 1024, 640, 512))
    if tn == N and N > 2048:
        tn = _pick(N, (256, 128))
    tk = _pick(K, (1024, 1408, 640, 512, 256, 128))
    nk = K // tk
    grid = (M // tm, N // tn, nk)
    if dims == "tn":
        a_spec = pl.BlockSpec((tk, tm), lambda i, j, k: (k, i))
    else:
        a_spec = pl.BlockSpec((tm, tk), lambda i, j, k: (i, k))
    if dims == "nt":
        b_spec = pl.BlockSpec((tn, tk), lambda i, j, k: (j, k))
    else:
        b_spec = pl.BlockSpec((tk, tn), lambda i, j, k: (k, j))
    o_spec = pl.BlockSpec((tm, tn), lambda i, j, k: (i, j))
    in_specs = [a_spec, b_spec]
    args = [a, b]
    if res is not None:
        in_specs.append(o_spec)
        args.append(res)

    def body(*refs):
        if res is not None:
            a_ref, b_ref, r_ref, o_ref, acc_ref = refs
        else:
            a_ref, b_ref, o_ref, acc_ref = refs
            r_ref = None
        k = pl.program_id(2)
        part = _dot(a_ref[...], b_ref[...], dims)

        def finish(total):
            if r_ref is not None:
                total = total + res_scale * r_ref[...]
            o_ref[...] = total.astype(o_ref.dtype)

        if nk == 1:
            finish(part)
        else:
            @pl.when(k == 0)
            def _():
                acc_ref[...] = part

            @pl.when(jnp.logical_and(k > 0, k < nk - 1))
            def _():
                acc_ref[...] += part

            @pl.when(k == nk - 1)
            def _():
                finish(acc_ref[...] + part)

    out_sds = jax.ShapeDtypeStruct((M, N), out_dtype)
    acc = pltpu.VMEM((tm, tn), F32)
    if not ride:
        return pl.pallas_call(
            body, name=name, grid=grid, in_specs=in_specs, out_specs=o_spec, out_shape=out_sds,
            scratch_shapes=[acc], compiler_params=_cparams(("parallel", "parallel", "arbitrary")),
        )(*args)
    x_args, x_in, x_out, x_shapes, x_sems, wrap = _riding_exchange(list(ride), grid)
    return pl.pallas_call(
        wrap(body, len(args), 1), name=name, grid=grid, in_specs=in_specs + x_in, out_specs=[o_spec] + x_out,
        out_shape=[out_sds] + x_shapes, scratch_shapes=[acc] + x_sems,
        compiler_params=_cparams(("arbitrary",) * 3, has_side_effects=True),
    )(*args, *x_args)


LN_ROWS = 256


def _ln_stats(r):
    mu = jnp.mean(r, axis=-1, keepdims=True)
    xc = r - mu
    var = jnp.mean(xc * xc, axis=-1, keepdims=True)
    return xc * lax.rsqrt(var + LN_EPS)


def _ln_fwd(r, g, b, name):
    S, D = r.shape

    def body(r_ref, g_ref, b_ref, h_ref):
        h_ref[...] = _ln_stats(r_ref[...]) * g_ref[...] + b_ref[...]

    row = pl.BlockSpec((LN_ROWS, D), lambda i: (i, 0))
    vec = pl.BlockSpec((1, D), lambda i: (0, 0))
    return pl.pallas_call(
        body, name=name, grid=(S // LN_ROWS,), in_specs=[row, vec, vec], out_specs=row,
        out_shape=jax.ShapeDtypeStruct((S, D), F32),
        compiler_params=_cparams(("parallel",)),
    )(r, g, b)


def _ln_bwd_core(xhat, dy, g):
    dxh = dy * g
    m1 = jnp.mean(dxh, axis=-1, keepdims=True)
    m2 = jnp.mean(dxh * xhat, axis=-1, keepdims=True)
    return dxh - m1 - xhat * m2


def _ln_bwd(r, dy, g, name):
    S, D = r.shape

    def body(r_ref, dy_ref, g_ref, dr_ref, gg_ref, gb_ref):
        x = r_ref[...]
        mu = jnp.mean(x, axis=-1, keepdims=True)
        xc = x - mu
        rstd = lax.rsqrt(jnp.mean(xc * xc, axis=-1, keepdims=True) + LN_EPS)
        xhat = xc * rstd
        dy = dy_ref[...]
        dr_ref[...] = rstd * _ln_bwd_core(xhat, dy, g_ref[...])

        @pl.when(pl.program_id(0) == 0)
        def _():
            gg_ref[...] = jnp.zeros_like(gg_ref)
            gb_ref[...] = jnp.zeros_like(gb_ref)

        gg_ref[...] += jnp.sum(dy * xhat, axis=0, keepdims=True)
        gb_ref[...] += jnp.sum(dy, axis=0, keepdims=True)

    row = pl.BlockSpec((LN_ROWS, D), lambda i: (i, 0))
    vec = pl.BlockSpec((1, D), lambda i: (0, 0))
    return pl.pallas_call(
        body, name=name, grid=(S // LN_ROWS,), in_specs=[row, row, vec], out_specs=[row, vec, vec],
        out_shape=[jax.ShapeDtypeStruct((S, D), F32), jax.ShapeDtypeStruct((1, D), F32),
                   jax.ShapeDtypeStruct((1, D), F32)],
        compiler_params=_cparams(("arbitrary",)),
    )(r, dy, g)


def _ln_loss_bwd(r, target, g, b, name):
    S, D = r.shape

    def body(r_ref, t_ref, g_ref, b_ref, dr_ref, loss_ref, gg_ref, gb_ref):
        x = r_ref[...]
        mu = jnp.mean(x, axis=-1, keepdims=True)
        xc = x - mu
        rstd = lax.rsqrt(jnp.mean(xc * xc, axis=-1, keepdims=True) + LN_EPS)
        xhat = xc * rstd
        y = xhat * g_ref[...] + b_ref[...]
        err = y - t_ref[...]
        dy = err * (1.0 / D)
        dr_ref[...] = rstd * _ln_bwd_core(xhat, dy, g_ref[...])

        @pl.when(pl.program_id(0) == 0)
        def _():
            loss_ref[...] = jnp.zeros_like(loss_ref)
            gg_ref[...] = jnp.zeros_like(gg_ref)
            gb_ref[...] = jnp.zeros_like(gb_ref)

        per_row = jnp.sum(err * err, axis=-1, keepdims=True) * (0.5 / D)
        loss_ref[...] += jnp.broadcast_to(jnp.sum(per_row, axis=0, keepdims=True), loss_ref.shape)
        gg_ref[...] += jnp.sum(dy * xhat, axis=0, keepdims=True)
        gb_ref[...] += jnp.sum(dy, axis=0, keepdims=True)

    row = pl.BlockSpec((LN_ROWS, D), lambda i: (i, 0))
    vec = pl.BlockSpec((1, D), lambda i: (0, 0))
    lvec = pl.BlockSpec((1, LANE), lambda i: (0, 0))
    return pl.pallas_call(
        body, name=name, grid=(S // LN_ROWS,), in_specs=[row, row, vec, vec],
        out_specs=[row, lvec, vec, vec],
        out_shape=[jax.ShapeDtypeStruct((S, D), F32), jax.ShapeDtypeStruct((1, LANE), F32),
                   jax.ShapeDtypeStruct((1, D), F32), jax.ShapeDtypeStruct((1, D), F32)],
        compiler_params=_cparams(("arbitrary",)),
    )(r, target, g, b)


CONV_COLS = 256
CONV_ROWS = 256
HALO = 8
INV_SQRT2 = 1.0 / math.sqrt(2.0)
INV_SQRT2PI = 1.0 / math.sqrt(2.0 * math.pi)


def _gelu(x):
    return 0.5 * x * (1.0 + lax.erf(x * INV_SQRT2))


def _gelu_and_grad(x):
    cdf = 0.5 * (1.0 + lax.erf(x * INV_SQRT2))
    return x * cdf, cdf + x * jnp.exp(-0.5 * x * x) * INV_SQRT2PI


def _conv_rows(ext, w_ref, b_ref, n):
    total = ext.shape[0]
    s1 = pltpu.roll(ext, 1, 0)
    s2 = pltpu.roll(ext, 2, 0)
    u = w_ref[2:3, :] * ext + w_ref[1:2, :] * s1 + w_ref[0:1, :] * s2 + b_ref[...]
    return u[HALO:total], s1[HALO:total], s2[HALO:total]


def _conv_gelu_fwd(u0, conv_w, conv_b, name):
    S, C2 = u0.shape
    F = C2 // 2
    ncb = F // CONV_COLS
    nrc = S // CONV_ROWS

    def body(ua_ref, uc_ref, wa_ref, wc_ref, ba_ref, bc_ref, p_ref):
        def chunk(ci, _):
            r0 = pl.multiple_of(ci * CONV_ROWS, CONV_ROWS)
            p0 = pl.multiple_of(jnp.maximum(r0 - HALO, 0), HALO)
            keep = (ci > 0).astype(F32)

            def load(ref):
                prev = ref[pl.ds(p0, HALO), :] * keep
                return jnp.concatenate([prev, ref[pl.ds(r0, CONV_ROWS), :]], axis=0)

            a, _, _ = _conv_rows(load(ua_ref), wa_ref, ba_ref, CONV_ROWS)
            c, _, _ = _conv_rows(load(uc_ref), wc_ref, bc_ref, CONV_ROWS)
            p_ref[pl.ds(r0, CONV_ROWS), :] = (_gelu(a) * c).astype(p_ref.dtype)
            return 0

        lax.fori_loop(0, nrc, chunk, 0)

    col_a = pl.BlockSpec((S, CONV_COLS), lambda j: (0, j))
    col_c = pl.BlockSpec((S, CONV_COLS), lambda j: (0, j + ncb))
    w_a = pl.BlockSpec((CONV_WIDTH, CONV_COLS), lambda j: (0, j))
    w_c = pl.BlockSpec((CONV_WIDTH, CONV_COLS), lambda j: (0, j + ncb))
    b_a = pl.BlockSpec((1, CONV_COLS), lambda j: (0, j))
    b_c = pl.BlockSpec((1, CONV_COLS), lambda j: (0, j + ncb))
    return pl.pallas_call(
        body, name=name, grid=(ncb,), in_specs=[col_a, col_c, w_a, w_c, b_a, b_c], out_specs=col_a,
        out_shape=jax.ShapeDtypeStruct((S, F), MXU_DTYPE),
        compiler_params=_cparams(("parallel",)),
    )(u0, u0, conv_w, conv_w, conv_b, conv_b)


def _conv_gelu_bwd(u0, dp, conv_w, conv_b, name):
    S, C2 = u0.shape
    F = C2 // 2
    ncb = F // CONV_COLS
    nrc = S // CONV_ROWS
    EXT = CONV_ROWS + HALO

    def body(ua_ref, uc_ref, dp_ref, wa_ref, wc_ref, ba_ref, bc_ref,
             da_ref, dc_ref, gwa_ref, gwc_ref, gba_ref, gbc_ref):
        gwa_ref[...] = jnp.zeros_like(gwa_ref)
        gwc_ref[...] = jnp.zeros_like(gwc_ref)
        gba_ref[...] = jnp.zeros_like(gba_ref)
        gbc_ref[...] = jnp.zeros_like(gbc_ref)
        rid = lax.broadcasted_iota(jnp.int32, (EXT, CONV_COLS), 0)

        def chunk(ci, _):
            r0 = pl.multiple_of(ci * CONV_ROWS, CONV_ROWS)
            p0 = pl.multiple_of(jnp.maximum(r0 - HALO, 0), HALO)
            n0 = pl.multiple_of(jnp.minimum(r0 + CONV_ROWS, S - HALO), HALO)
            keep_prev = (ci > 0).astype(F32)
            keep_next = (ci < nrc - 1).astype(F32)

            def load(ref):
                return jnp.concatenate([ref[pl.ds(p0, HALO), :] * keep_prev,
                                        ref[pl.ds(r0, CONV_ROWS), :],
                                        ref[pl.ds(n0, HALO), :] * keep_next], axis=0)

            ext_a = load(ua_ref)
            ext_c = load(uc_ref)
            a, a1, a2 = _conv_rows(ext_a, wa_ref, ba_ref, EXT)
            c, c1, c2 = _conv_rows(ext_c, wc_ref, bc_ref, EXT)
            a0 = ext_a[HALO:HALO + EXT]
            c0 = ext_c[HALO:HALO + EXT]
            dpe = jnp.concatenate([dp_ref[pl.ds(r0, CONV_ROWS), :].astype(F32),
                                   dp_ref[pl.ds(n0, HALO), :].astype(F32) * keep_next], axis=0)
            gelu_a, slope_a = _gelu_and_grad(a)
            d_a = dpe * c * slope_a
            d_c = dpe * gelu_a
            own = rid < CONV_ROWS

            def back(d_u, w_ref, x0, x1, x2, d_ref, gw_ref, gb_ref):
                d_u0 = (w_ref[2:3, :] * d_u + w_ref[1:2, :] * pltpu.roll(d_u, EXT - 1, 0)
                        + w_ref[0:1, :] * pltpu.roll(d_u, EXT - 2, 0))
                d_ref[pl.ds(r0, CONV_ROWS), :] = d_u0[0:CONV_ROWS].astype(d_ref.dtype)
                d_own = jnp.where(own, d_u, 0.0)
                gw_ref[...] += jnp.concatenate(
                    [jnp.sum(d_own * x2, axis=0, keepdims=True),
                     jnp.sum(d_own * x1, axis=0, keepdims=True),
                     jnp.sum(d_own * x0, axis=0, keepdims=True)], axis=0)
                gb_ref[...] += jnp.sum(d_own, axis=0, keepdims=True)

            back(d_a, wa_ref, a0, a1, a2, da_ref, gwa_ref, gba_ref)
            back(d_c, wc_ref, c0, c1, c2, dc_ref, gwc_ref, gbc_ref)
            return 0

        lax.fori_loop(0, nrc, chunk, 0)

    col_a = pl.BlockSpec((S, CONV_COLS), lambda j: (0, j))
    col_c = pl.BlockSpec((S, CONV_COLS), lambda j: (0, j + ncb))
    w_a = pl.BlockSpec((CONV_WIDTH, CONV_COLS), lambda j: (0, j))
    w_c = pl.BlockSpec((CONV_WIDTH, CONV_COLS), lambda j: (0, j + ncb))
    b_a = pl.BlockSpec((1, CONV_COLS), lambda j: (0, j))
    b_c = pl.BlockSpec((1, CONV_COLS), lambda j: (0, j + ncb))
    outs = pl.pallas_call(
        body, name=name, grid=(ncb,),
        in_specs=[col_a, col_c, col_a, w_a, w_c, b_a, b_c],
        out_specs=[col_a, col_a, w_a, w_a, b_a, b_a],
        out_shape=[jax.ShapeDtypeStruct((S, F), MXU_DTYPE), jax.ShapeDtypeStruct((S, F), MXU_DTYPE),
                   jax.ShapeDtypeStruct((CONV_WIDTH, F), F32), jax.ShapeDtypeStruct((CONV_WIDTH, F), F32),
                   jax.ShapeDtypeStruct((1, F), F32), jax.ShapeDtypeStruct((1, F), F32)],
        compiler_params=_cparams(("parallel",)),
    )(u0, u0, dp, conv_w, conv_w, conv_b, conv_b)
    da, dc, gwa, gwc, gba, gbc = outs
    return (jnp.concatenate([da, dc], axis=1), jnp.concatenate([gwa, gwc], axis=1),
            jnp.concatenate([gba, gbc], axis=1))


SB_TK = 128
SB_TQ_FWD = 1024
SB_TQ_BWD = 1024
SB_PAIRS_FWD = 2


def _softplus(z):
    return jnp.maximum(z, 0.0) + jnp.log(1.0 + jnp.exp(-jnp.abs(z)))


def _tri_ones(after):
    r = lax.broadcasted_iota(jnp.int32, (SB_TK, 2 * SB_TK), 0)
    c = lax.broadcasted_iota(jnp.int32, (SB_TK, 2 * SB_TK), 1)
    tri = (r > c) if after else (r < c)
    return jnp.where(c >= SB_TK, 1.0, jnp.where(tri, 1.0, 0.0)).astype(BF16)


def _sb_block_specs(S, TQ):
    NP = SB_WIDTH // LANE
    return [pl.BlockSpec((TQ, LANE), lambda p, i: (i, p)),
            pl.BlockSpec((S, LANE), lambda p, i: (0, NP + p)),
            pl.BlockSpec((S, LANE), lambda p, i: (0, 2 * NP + p))]


def _sb_fwd(qkv, name, ride=()):
    S = qkv.shape[0]
    TK = SB_TK
    TQ = min(SB_TQ_FWD, S)
    R = TQ // TK
    NP = SB_WIDTH // LANE
    PS = SB_PAIRS_FWD
    W = PS * LANE
    NH = 2 * PS
    nq = S // TQ
    nkb = max(R * (nq - 1), 1)
    grid = (NP // PS, nq)

    def body(q_ref, k_ref, v_ref, o_ref, t_ref, b_ref, wk_ref, wbuf, wsem):
        p, i = pl.program_id(0), pl.program_id(1)
        row = lax.broadcasted_iota(jnp.int32, (TQ, TK), 0)
        col = lax.broadcasted_iota(jnp.int32, (TQ, TK), 1)
        after_ones = _tri_ones(True)
        first = lax.broadcasted_iota(jnp.int32, (TK, LANE), 1) < SB_HEAD_DIM

        def keep(slot, j):
            return pltpu.make_async_copy(wbuf.at[slot], wk_ref.at[pl.ds(p * NH, NH), i, j], wsem.at[slot])

        def block(j, carry, r0, slot=None):
            k0 = pl.multiple_of(j * TK, TK)
            masked = r0 is not None
            r0 = r0 or 0
            out = []
            for pr in range(PS):
                lanes = slice(pr * LANE, (pr + 1) * LANE)
                acc0, tail_a, tail_b = carry[3 * pr:3 * pr + 3]
                kab = k_ref[pl.ds(k0, TK), lanes]
                vab = v_ref[pl.ds(k0, TK), lanes]
                none = jnp.zeros_like(kab)
                k2 = jnp.concatenate([jnp.where(first, kab, none), jnp.where(first, none, kab)], axis=0)
                v2 = jnp.concatenate([jnp.where(first, vab, none), jnp.where(first, none, vab)], axis=0)
                z2 = _dot(q_ref[r0:, lanes], k2, "nt")
                tails, ws = [], []
                for hh, tail in enumerate((tail_a, tail_b)):
                    z = z2[:, hh * TK:(hh + 1) * TK]
                    sp = _softplus(z)
                    if masked:
                        strict = col[r0:] < row[r0:] - r0
                        sp = jnp.where(strict, sp, 0.0)
                    cs = _dot_split(sp, after_ones, "nn")
                    w = jnp.exp(z - sp - cs[:, :TK] - tail[r0:])
                    if masked:
                        w = jnp.where(strict, w, 0.0)
                    ws.append(w.astype(MXU_DTYPE))
                    if slot is not None:
                        wbuf[slot, 2 * pr + hh] = ws[-1]
                    tot = cs[:, TK:]
                    if r0:
                        tot = jnp.concatenate([jnp.zeros((r0, TK), F32), tot], axis=0)
                    tails.append(tail + tot)
                acc_r = acc0[r0:] + _dot(jnp.concatenate(ws, axis=1), v2, "nn")
                acc = jnp.concatenate([acc0[:r0], acc_r], axis=0) if r0 else acc_r
                out += [acc, tails[0], tails[1]]
            return tuple(out)

        carry = (jnp.zeros((TQ, LANE), F32), jnp.zeros((TQ, TK), F32), jnp.zeros((TQ, TK), F32)) * PS
        for u in reversed(range(R)):
            carry = block(R * i + u, carry, u * TK)
        for pr in range(PS):
            b_ref[2 * pr] = carry[3 * pr + 1]
            b_ref[2 * pr + 1] = carry[3 * pr + 2]
        trips = R * i

        def below(n, c):
            slot = n % 2
            j = trips - 1 - n

            @pl.when(n >= 2)
            def _():
                keep(slot, j).wait()

            c = block(j, c, None, slot)
            keep(slot, j).start()
            return c

        carry = lax.fori_loop(0, trips, below, carry)
        for back in (1, 2):

            @pl.when(trips >= back)
            def _():
                keep((trips - back) % 2, 0).wait()

        for pr in range(PS):
            o_ref[:, pr * LANE:(pr + 1) * LANE] = carry[3 * pr].astype(o_ref.dtype)
            t_ref[2 * pr] = carry[3 * pr + 1]
            t_ref[2 * pr + 1] = carry[3 * pr + 2]

    in_specs = [pl.BlockSpec((TQ, W), lambda p, i: (i, p)),
                pl.BlockSpec((S, W), lambda p, i: (0, NP // PS + p)),
                pl.BlockSpec((S, W), lambda p, i: (0, 2 * (NP // PS) + p))]
    ospec = pl.BlockSpec((TQ, W), lambda p, i: (i, p))
    tspec = pl.BlockSpec((NH, TQ, TK), lambda p, i: (p, i, 0))
    x_args, x_in, x_out, x_shapes, x_sems, wrap = _riding_exchange(list(ride), grid)
    tsds = jax.ShapeDtypeStruct((SB_HEADS, S, TK), F32)
    return pl.pallas_call(
        wrap(body, 3, 4) if ride else body, name=name, grid=grid,
        in_specs=in_specs + (x_in if ride else []),
        out_specs=[ospec, tspec, tspec, pl.BlockSpec(memory_space=pl.ANY)] + (x_out if ride else []),
        out_shape=[jax.ShapeDtypeStruct((S, SB_WIDTH), MXU_DTYPE), tsds, tsds,
                   jax.ShapeDtypeStruct((SB_HEADS, nq, nkb, TQ, TK), MXU_DTYPE)] + (x_shapes if ride else []),
        scratch_shapes=[pltpu.VMEM((2, NH, TQ, TK), MXU_DTYPE), pltpu.SemaphoreType.DMA((2,))]
        + (x_sems if ride else []),
        compiler_params=_cparams(("arbitrary", "arbitrary"), has_side_effects=True),
    )(qkv, qkv, qkv, *(x_args if ride else []))


def _sb_bwd(qkv, d_cat, total, band, w_kept, name, ride=()):
    S = qkv.shape[0]
    TK = SB_TK
    TQ = min(SB_TQ_BWD, S)
    assert TQ == min(SB_TQ_FWD, S)
    R = TQ // TK
    NP = SB_WIDTH // LANE
    grid = (NP, S // TQ)
    scale = SB_HEAD_DIM ** -0.5

    def body(q_ref, k_ref, v_ref, do_ref, t_ref, b_ref, wk_ref, dq_ref, dk_ref, dv_ref, dk_acc, dv_acc, wbuf, wsem):
        p, i = pl.program_id(0), pl.program_id(1)

        @pl.when(i == 0)
        def _():
            dk_acc[...] = jnp.zeros_like(dk_acc)
            dv_acc[...] = jnp.zeros_like(dv_acc)

        row = lax.broadcasted_iota(jnp.int32, (TQ, TK), 0)
        col = lax.broadcasted_iota(jnp.int32, (TQ, TK), 1)
        after_ones = _tri_ones(True)
        before_ones = _tri_ones(False)
        first = lax.broadcasted_iota(jnp.int32, (TK, LANE), 1) < SB_HEAD_DIM
        qab = q_ref[...]
        doab = do_ref[...].astype(MXU_DTYPE)
        qdo = jnp.concatenate([qab, doab], axis=1)

        def fetch(slot, j):
            return pltpu.make_async_copy(wk_ref.at[pl.ds(2 * p, 2), i, j], wbuf.at[slot], wsem.at[slot])

        def block(j, carry, r0, slot=None):
            k0 = pl.multiple_of(j * TK, TK)
            masked = r0 is not None
            r0 = r0 or 0
            kab = k_ref[pl.ds(k0, TK), :]
            vab = v_ref[pl.ds(k0, TK), :]
            none = jnp.zeros_like(kab)
            k_h = [jnp.where(first, kab, none), jnp.where(first, none, kab)]
            v_h = [jnp.where(first, vab, none), jnp.where(first, none, vab)]
            sums, dzs, ws = [], [], []
            for hh in range(2):
                seen, gsum = carry[1 + 2 * hh], carry[2 + 2 * hh]
                kv = jnp.concatenate([jnp.concatenate([k_h[hh], none], axis=1),
                                      jnp.concatenate([none, v_h[hh]], axis=1)], axis=0)
                zdw = _dot(qdo[r0:], kv, "nt")
                z = zdw[:, :TK]
                sp = _softplus(z)
                logsig = z - sp
                if masked:
                    strict = col[r0:] < row[r0:] - r0
                    sp = jnp.where(strict, sp, 0.0)
                    cs = _dot_split(sp, after_ones, "nn")
                    seen_r = seen[r0:] + cs[:, TK:]
                    w = jnp.exp(logsig - cs[:, :TK] - (t_ref[hh, r0:, :] - seen_r))
                    w = jnp.where(strict, w, 0.0)
                    wb = w.astype(MXU_DTYPE)
                else:
                    seen_r = seen
                    wb = wbuf[slot, hh]
                    w = wb.astype(F32)
                g = w * zdw[:, TK:]
                cg = _dot_split(g, before_ones, "nn")
                dz = g - jnp.exp(logsig) * (g + cg[:, :TK] + gsum[r0:])
                if masked:
                    dz = jnp.where(strict, dz, 0.0)
                dzs.append(dz.astype(MXU_DTYPE))
                ws.append(wb)
                gsum_r = gsum[r0:] + cg[:, TK:]
                if r0:
                    seen_r = jnp.concatenate([seen[:r0], seen_r], axis=0)
                    gsum_r = jnp.concatenate([gsum[:r0], gsum_r], axis=0)
                sums += [seen_r, gsum_r]
            kvg = _dot(jnp.concatenate(dzs + ws, axis=1), qdo[r0:], "tn")
            dk_acc[pl.ds(k0, TK), :] += jnp.where(first, kvg[:TK, :LANE], kvg[TK:2 * TK, :LANE])
            dv_acc[pl.ds(k0, TK), :] += jnp.where(first, kvg[2 * TK:3 * TK, LANE:], kvg[3 * TK:, LANE:])
            dq_r = carry[0][r0:] + _dot(jnp.concatenate(dzs, axis=1), jnp.concatenate(k_h, axis=0), "nn")
            dq = jnp.concatenate([carry[0][:r0], dq_r], axis=0) if r0 else dq_r
            return (dq, *sums)

        trips = R * i

        @pl.when(trips >= 1)
        def _():
            fetch(0, 0).start()

        def below(j, c):
            slot = j % 2

            @pl.when(j + 1 < trips)
            def _():
                fetch(1 - slot, j + 1).start()

            fetch(slot, j).wait()
            return block(j, c, None, slot)

        zero = jnp.zeros((TQ, TK), F32)
        carry = (jnp.zeros((TQ, LANE), F32), t_ref[0] - b_ref[0], zero, t_ref[1] - b_ref[1], zero)
        carry = lax.fori_loop(0, trips, below, carry)
        for u in range(R):
            carry = block(R * i + u, carry, u * TK)
        dq_ref[...] = (carry[0] * scale).astype(dq_ref.dtype)

        @pl.when(i == S // TQ - 1)
        def _():
            dk_ref[...] = dk_acc[...].astype(dk_ref.dtype)
            dv_ref[...] = dv_acc[...].astype(dv_ref.dtype)

    qspec = pl.BlockSpec((TQ, LANE), lambda p, i: (i, p))
    full = pl.BlockSpec((S, LANE), lambda p, i: (0, p))
    tspec = pl.BlockSpec((2, TQ, TK), lambda p, i: (p, i, 0))
    sds = jax.ShapeDtypeStruct((S, SB_WIDTH), MXU_DTYPE)
    scratch = [pltpu.VMEM((S, LANE), F32), pltpu.VMEM((S, LANE), F32),
               pltpu.VMEM((2, 2, TQ, TK), MXU_DTYPE), pltpu.SemaphoreType.DMA((2,))]
    x_args, x_in, x_out, x_shapes, x_sems, wrap = _riding_exchange(list(ride), grid)
    return pl.pallas_call(
        wrap(body, 7, 3) if ride else body, name=name, grid=grid,
        in_specs=_sb_block_specs(S, TQ) + [qspec, tspec, tspec, pl.BlockSpec(memory_space=pl.ANY)]
        + (x_in if ride else []),
        out_specs=[qspec, full, full] + (x_out if ride else []),
        out_shape=[sds, sds, sds] + (x_shapes if ride else []),
        scratch_shapes=scratch + (x_sems if ride else []),
        compiler_params=_cparams(("arbitrary", "arbitrary"), has_side_effects=bool(ride)),
    )(qkv, qkv, qkv, d_cat, total, band, w_kept, *(x_args if ride else []))


GLA_ROWS = 1024
GLA_GROUP_FWD = 8
GLA_GROUP_BWD = 4
GLA_PAIR_K = 2 * GLA_KEY_DIM
GLA_PAIR_V = 2 * GLA_VAL_DIM


def _log_sigmoid(x):
    return -_softplus(-x)


def _dot_split_lhs01(m01, x):
    hi = x.astype(BF16)
    lo = (x - hi.astype(F32)).astype(BF16)
    return lax.dot_general(jnp.concatenate([m01, m01], axis=1), jnp.concatenate([hi, lo], axis=0),
                           (((1,), (0,)), ((), ())), preferred_element_type=F32)


def _dot_split_tn(x, m01):
    hi = x.astype(BF16)
    lo = (x - hi.astype(F32)).astype(BF16)
    return lax.dot_general(jnp.concatenate([hi, lo], axis=0), jnp.concatenate([m01, m01], axis=0),
                           (((0,), (0,)), ((), ())), preferred_element_type=F32)


def _dot_split_nt01(m01, x):
    hi = x.astype(BF16)
    lo = (x - hi.astype(F32)).astype(BF16)
    return lax.dot_general(jnp.concatenate([m01, m01], axis=1), jnp.concatenate([hi, lo], axis=1),
                           (((1,), (1,)), ((), ())), preferred_element_type=F32)


def _gla_consts():
    C = CHUNK
    row = lax.broadcasted_iota(jnp.int32, (C, C), 0)
    col = lax.broadcasted_iota(jnp.int32, (C, C), 1)
    first = lax.broadcasted_iota(jnp.int32, (C, GLA_PAIR_K), 1) < GLA_KEY_DIM
    r = lax.broadcasted_iota(jnp.int32, (GLA_PAIR_K, GLA_PAIR_V), 0)
    c = lax.broadcasted_iota(jnp.int32, (GLA_PAIR_K, GLA_PAIR_V), 1)
    own = (r < GLA_KEY_DIM) == (c < GLA_VAL_DIM)
    rowk = lax.broadcasted_iota(jnp.int32, (C, GLA_PAIR_K), 0)
    return dict(row=row, col=col, first=first, own=own, rowk=rowk,
                incl=jnp.where(row >= col, 1.0, 0.0).astype(BF16),
                ones=jnp.ones((C, GLA_PAIR_V), BF16))


def _pack_state(state):
    top = lax.broadcasted_iota(jnp.int32, (GLA_PAIR_K, GLA_VAL_DIM), 0) < GLA_KEY_DIM
    return jnp.where(top, state[:, :GLA_VAL_DIM], state[:, GLA_VAL_DIM:])


def _unpack_state(packed):
    top = lax.broadcasted_iota(jnp.int32, (GLA_PAIR_K, GLA_VAL_DIM), 0) < GLA_KEY_DIM
    return jnp.concatenate([jnp.where(top, packed, 0.0), jnp.where(top, 0.0, packed)], axis=1)


def _gla_chunk(qc, kc, vc, gate, k):
    C = CHUNK
    row, col, first, rowk = k["row"], k["col"], k["first"], k["rowk"]
    la = _log_sigmoid(gate) * (1.0 / GLA_TAU)
    b = _dot_split_lhs01(k["incl"], la)
    b_ref = jnp.sum(jnp.where(rowk == C // 2 - 1, b, 0.0), axis=0, keepdims=True)
    b_last = jnp.sum(la, axis=0, keepdims=True)
    qs = qc * (GLA_KEY_DIM ** -0.5)
    q_in = qs * jnp.exp(b - b_ref)
    k_in = kc * jnp.exp(b_ref - b)
    k_dec = kc * jnp.exp(b_last - b)
    q_b = qs * jnp.exp(b)
    k_in_h = [jnp.where(first, k_in, 0.0), jnp.where(first, 0.0, k_in)]
    v_h = [vc[:, :GLA_VAL_DIM], vc[:, GLA_VAL_DIM:]]
    sc = [jnp.where(row >= col, _dot(q_in, k_in_h[hh], "nt"), 0.0) for hh in range(2)]
    o_intra = jnp.concatenate([_dot(sc[hh], v_h[hh], "nn") for hh in range(2)], axis=1)
    upd = jnp.where(k["own"], _dot(k_dec, vc, "tn"), 0.0)
    dec_col = jnp.exp(_dot_split_tn(la, k["ones"]))
    return dict(la=la, b=b, b_ref=b_ref, b_last=b_last, q_in=q_in, k_in=k_in, k_dec=k_dec, q_b=q_b,
                k_in_h=k_in_h, v_h=v_h, sc=sc, o_intra=o_intra, upd=upd, dec_col=dec_col)


def _rms_gate(o, gg):
    rinv = lax.rsqrt(jnp.mean(o * o, axis=-1, keepdims=True) + RMS_EPS)
    o_n = o * rinv
    sg = 1.0 / (1.0 + jnp.exp(-gg))
    return o_n, rinv, sg


def _gla_in_specs(rows_of, RB):
    PK, PV = GLA_PAIR_K, GLA_PAIR_V
    return [pl.BlockSpec((RB, PK), lambda i, p: (rows_of(i), p)),
            pl.BlockSpec((RB, PK), lambda i, p: (rows_of(i), (OFF_GK - OFF_GQ) // PK + p)),
            pl.BlockSpec((RB, PV), lambda i, p: (rows_of(i), (OFF_GV - OFF_GQ) // PV + p)),
            pl.BlockSpec((RB, PV), lambda i, p: (rows_of(i), (OFF_GG - OFF_GQ) // PV + p)),
            pl.BlockSpec((RB, LANE), lambda i, p: (rows_of(i), (OFF_GA - OFF_GQ) // LANE)),
            pl.BlockSpec((1, LANE, PK), lambda i, p: (p, 0, 0)),
            pl.BlockSpec((1, 1, PK), lambda i, p: (p, 0, 0)),
            pl.BlockSpec((1, GLA_VAL_DIM), lambda i, p: (0, 0))]


def _gla_fwd(proj, gate_up_p, gate_bias_p, gnorm, name):
    S = proj.shape[0]
    C, RB, PK, PV, dv = CHUNK, min(GLA_ROWS, S), GLA_PAIR_K, GLA_PAIR_V, GLA_VAL_DIM
    NP = GLA_HEADS // 2
    G = GLA_GROUP_FWD
    nchunk = S // C
    ngroup = RB // (C * G)

    def body(q_ref, k_ref, v_ref, gg_ref, ga_ref, gu_ref, gb_ref, gn_ref, o_ref, prev_ref, st_ref):
        i, p = pl.program_id(0), pl.program_id(1)
        k = _gla_consts()

        @pl.when(i == 0)
        def _():
            st_ref[p] = jnp.zeros((PK, PV), F32)

        def group(gi, state):
            for u in range(G):
                ci = gi * G + u
                rows = pl.ds(pl.multiple_of(ci * C, C), C)
                gate = _dot(ga_ref[rows, :], gu_ref[0], "nn") + gb_ref[0]
                f = _gla_chunk(q_ref[rows, :], k_ref[rows, :], v_ref[rows, :], gate, k)
                prev_ref[0, ci] = _pack_state(state)
                o = f["o_intra"] + _dot(f["q_b"], state, "nn")
                state = f["dec_col"] * state + f["upd"]
                ggv = gg_ref[rows, :]
                halves = []
                for hh in range(2):
                    lanes = slice(hh * dv, (hh + 1) * dv)
                    o_n, _, sg = _rms_gate(o[:, lanes], ggv[:, lanes])
                    halves.append(o_n * gn_ref[...] * (ggv[:, lanes] * sg))
                o_ref[rows, :] = jnp.concatenate(halves, axis=1).astype(o_ref.dtype)
            return state

        st_ref[p] = lax.fori_loop(0, ngroup, group, st_ref[p])

    return pl.pallas_call(
        body, name=name, grid=(S // RB, NP), in_specs=_gla_in_specs(lambda i: i, RB),
        out_specs=[pl.BlockSpec((RB, PV), lambda i, p: (i, p)),
                   pl.BlockSpec((1, RB // C, PK, dv), lambda i, p: (p, i, 0, 0))],
        out_shape=[jax.ShapeDtypeStruct((S, GLA_WIDTH), MXU_DTYPE),
                   jax.ShapeDtypeStruct((NP, nchunk, PK, dv), F32)],
        scratch_shapes=[pltpu.VMEM((NP, PK, PV), F32)],
        compiler_params=_cparams(("arbitrary", "arbitrary")),
    )(proj, proj, proj, proj, proj, gate_up_p, gate_bias_p, gnorm)


def _gla_bwd(proj, gate_up_p, gate_bias_p, gnorm, prev, d_cat, name):
    S = proj.shape[0]
    C, RB, PK, PV, dv = CHUNK, min(GLA_ROWS, S), GLA_PAIR_K, GLA_PAIR_V, GLA_VAL_DIM
    NP = GLA_HEADS // 2
    G = GLA_GROUP_BWD
    nb = S // RB
    ngroup = RB // (C * G)

    def body(q_ref, k_ref, v_ref, gg_ref, ga_ref, gu_ref, gb_ref, gn_ref, prev_ref, do_ref,
             dq_ref, dk_ref, dv_ref, dgg_ref, dga_ref, ggu_ref, ggb_ref, ggn_ref, st_ref):
        i, p = pl.program_id(0), pl.program_id(1)
        k = _gla_consts()
        row, col, first, rowk = k["row"], k["col"], k["first"], k["rowk"]
        gu = gu_ref[0]

        @pl.when(i == 0)
        def _():
            st_ref[p] = jnp.zeros((PK, PV), F32)
            ggu_ref[p] = jnp.zeros((LANE, PK), F32)
            ggb_ref[p] = jnp.zeros((1, PK), F32)

        @pl.when(jnp.logical_and(i == 0, p == 0))
        def _():
            ggn_ref[...] = jnp.zeros_like(ggn_ref)

        @pl.when(p == 0)
        def _():
            dga_ref[...] = jnp.zeros_like(dga_ref)

        upper_incl = jnp.where(col >= row, 1.0, 0.0).astype(BF16)
        ones_8 = jnp.ones((8, PV), BF16)

        def group(gn, dstate):
            gi = ngroup - 1 - gn
            for u in reversed(range(G)):
                ci = gi * G + u
                rows = pl.ds(pl.multiple_of(ci * C, C), C)
                ga = ga_ref[rows, :]
                gate = _dot(ga, gu, "nn") + gb_ref[0]
                vc = v_ref[rows, :]
                f = _gla_chunk(q_ref[rows, :], k_ref[rows, :], vc, gate, k)
                state = _unpack_state(prev_ref[0, ci])
                o = f["o_intra"] + _dot(f["q_b"], state, "nn")
                ggv = gg_ref[rows, :]
                dout = do_ref[rows, :]
                d_o_h, dgg_h = [], []
                for hh in range(2):
                    lanes = slice(hh * dv, (hh + 1) * dv)
                    o_n, rinv, sg = _rms_gate(o[:, lanes], ggv[:, lanes])
                    silu = ggv[:, lanes] * sg
                    dgg_h.append(dout[:, lanes] * o_n * gn_ref[...] * (sg * (1.0 + ggv[:, lanes] * (1.0 - sg))))
                    d_ong = dout[:, lanes] * silu
                    ggn_ref[...] += jnp.sum(d_ong * o_n, axis=0, keepdims=True)
                    d_on = d_ong * gn_ref[...]
                    d_o_h.append(rinv * (d_on - o_n * jnp.mean(d_on * o_n, axis=-1, keepdims=True)))
                dgg_ref[rows, :] = jnp.concatenate(dgg_h, axis=1).astype(dgg_ref.dtype)
                d_o = jnp.concatenate(d_o_h, axis=1)
                d_upd = jnp.where(k["own"], dstate, 0.0)
                d_dec_col = dstate * state * f["dec_col"]
                dstate = f["dec_col"] * dstate + _dot(f["q_b"], d_o, "tn")
                dsc = [jnp.where(row >= col, _dot(d_o_h[hh], f["v_h"][hh], "nt"), 0.0) for hh in range(2)]
                dv_ref[rows, :] = (jnp.concatenate([_dot(f["sc"][hh], d_o_h[hh], "tn") for hh in range(2)], axis=1)
                                   + _dot(f["k_dec"], d_upd, "nn")).astype(dv_ref.dtype)
                q_in_h = [jnp.where(first, f["q_in"], 0.0), jnp.where(first, 0.0, f["q_in"])]
                dq_in = _dot(dsc[0], f["k_in_h"][0], "nn") + _dot(dsc[1], f["k_in_h"][1], "nn")
                dk_in = _dot(dsc[0], q_in_h[0], "tn") + _dot(dsc[1], q_in_h[1], "tn")
                dq_b = _dot(d_o, state, "nt")
                dkdec = _dot(vc, d_upd, "nt")
                b = f["b"]
                e1 = jnp.exp(b - f["b_ref"])
                e2 = jnp.exp(f["b_ref"] - b)
                e3 = jnp.exp(f["b_last"] - b)
                eb = jnp.exp(b)
                dq_ref[rows, :] = ((dq_in * e1 + dq_b * eb) * (GLA_KEY_DIM ** -0.5)).astype(dq_ref.dtype)
                dk_ref[rows, :] = (dk_in * e2 + dkdec * e3).astype(dk_ref.dtype)
                t_q = dq_in * f["q_in"]
                t_k = dk_in * f["k_in"]
                t_d = dkdec * f["k_dec"]
                db = t_q - t_k - t_d + dq_b * f["q_b"]
                db_ref = jnp.sum(t_k - t_q, axis=0, keepdims=True)
                db_last = (jnp.sum(t_d, axis=0, keepdims=True)
                           + jnp.max(_dot_split_nt01(ones_8, d_dec_col), axis=0, keepdims=True))
                db = db + jnp.where(rowk == C // 2 - 1, db_ref, 0.0) + jnp.where(rowk == C - 1, db_last, 0.0)
                dla = _dot_split_lhs01(upper_incl, db)
                d_gate = dla * (1.0 / GLA_TAU) * (1.0 / (1.0 + jnp.exp(gate)))
                ggb_ref[p] += jnp.sum(d_gate, axis=0, keepdims=True)
                ggu_ref[p] += _dot(ga, d_gate, "tn")
                dga_ref[rows, :] += _dot(d_gate, gu, "nt")
            return dstate

        st_ref[p] = lax.fori_loop(0, ngroup, group, st_ref[p])

    back = lambda i: nb - 1 - i
    NPV = SB_WIDTH // PV
    return pl.pallas_call(
        body, name=name, grid=(nb, NP),
        in_specs=_gla_in_specs(back, RB) + [pl.BlockSpec((1, RB // C, PK, dv), lambda i, p: (p, back(i), 0, 0)),
                                            pl.BlockSpec((RB, PV), lambda i, p: (back(i), NPV + p))],
        out_specs=[pl.BlockSpec((RB, PK), lambda i, p: (back(i), p)),
                   pl.BlockSpec((RB, PK), lambda i, p: (back(i), p)),
                   pl.BlockSpec((RB, PV), lambda i, p: (back(i), p)),
                   pl.BlockSpec((RB, PV), lambda i, p: (back(i), p)),
                   pl.BlockSpec((RB, LANE), lambda i, p: (back(i), 0)),
                   pl.BlockSpec((NP, LANE, PK), lambda i, p: (0, 0, 0)),
                   pl.BlockSpec((NP, 1, PK), lambda i, p: (0, 0, 0)),
                   pl.BlockSpec((1, dv), lambda i, p: (0, 0))],
        out_shape=[jax.ShapeDtypeStruct((S, NP * PK), MXU_DTYPE), jax.ShapeDtypeStruct((S, NP * PK), MXU_DTYPE),
                   jax.ShapeDtypeStruct((S, GLA_WIDTH), MXU_DTYPE), jax.ShapeDtypeStruct((S, GLA_WIDTH), MXU_DTYPE),
                   jax.ShapeDtypeStruct((S, LANE), F32), jax.ShapeDtypeStruct((NP, LANE, PK), F32),
                   jax.ShapeDtypeStruct((NP, 1, PK), F32), jax.ShapeDtypeStruct((1, dv), F32)],
        scratch_shapes=[pltpu.VMEM((NP, PK, PV), F32)],
        compiler_params=_cparams(("arbitrary", "arbitrary")),
    )(proj, proj, proj, proj, proj, gate_up_p, gate_bias_p, gnorm, prev, d_cat)


def _exchange_copies(scatter_flags, ins, outs, send_sems, recv_sems, local_sems):
    n_peer = N_DEV - 1
    x, y, c = lax.axis_index("x"), lax.axis_index("y"), lax.axis_index("c")
    me = 4 * x + 2 * y + c
    copies = []
    for a, scatter in enumerate(scatter_flags):
        own = ins[a].at[me] if scatter else ins[a]
        copies.append(pltpu.make_async_copy(own, outs[a].at[me], local_sems.at[a]))
    for r in range(1, N_DEV):
        px = 1 - x if r & 4 else x
        py = 1 - y if r & 2 else y
        pc = 1 - c if r & 1 else c
        for a, scatter in enumerate(scatter_flags):
            src = ins[a].at[4 * px + 2 * py + pc] if scatter else ins[a]
            copies.append(pltpu.make_async_remote_copy(
                src_ref=src, dst_ref=outs[a].at[me],
                send_sem=send_sems.at[a * n_peer + r - 1], recv_sem=recv_sems.at[a * n_peer + r - 1],
                device_id=(px, py, pc), device_id_type=MESH_ID))
    return copies


def _exchange_shapes(items):
    out_shape = []
    for arr, scatter in items:
        shp = arr.shape if scatter else (N_DEV,) + arr.shape
        out_shape.append(jax.ShapeDtypeStruct(shp, arr.dtype))
    n = len(items)
    sems = [pltpu.SemaphoreType.DMA((n * (N_DEV - 1),)), pltpu.SemaphoreType.DMA((n * (N_DEV - 1),)),
            pltpu.SemaphoreType.DMA((n,))]
    return out_shape, sems


def _exchange(items, name):
    n = len(items)
    flags = [sc for _, sc in items]

    def body(*refs):
        copies = _exchange_copies(flags, refs[:n], refs[n:2 * n], *refs[2 * n:])
        for cp in copies:
            cp.start()
        for cp in copies:
            cp.wait()

    out_shape, sems = _exchange_shapes(items)
    any_spec = pl.BlockSpec(memory_space=pl.ANY)
    return pl.pallas_call(
        body, name=name, in_specs=[any_spec] * n, out_specs=[any_spec] * n, out_shape=out_shape,
        scratch_shapes=sems, compiler_params=pltpu.CompilerParams(has_side_effects=True),
    )(*[arr for arr, _ in items])


def _all_gather_two_level(arrays, name):
    n = len(arrays)
    K = N_DEV - 1

    def body(*refs):
        ins, outs = refs[:n], refs[n:2 * n]
        send_sems, recv_sems, local_sems = refs[2 * n:]
        x, y, c = lax.axis_index("x"), lax.axis_index("y"), lax.axis_index("c")
        sibling = (x, y, 1 - c)
        chips = [(1 - x, y), (x, 1 - y), (1 - x, 1 - y)]

        def slot(px, py, pc):
            return 4 * px + 2 * py + pc

        def copy(a, k, block, to, src=None):
            rows = outs[a].at[slot(*block)]
            return pltpu.make_async_remote_copy(
                src_ref=rows if src is None else src, dst_ref=rows,
                send_sem=send_sems.at[a * K + k], recv_sem=recv_sems.at[a * K + k],
                device_id=to, device_id_type=MESH_ID)

        me = (x, y, c)
        local, first, passed = [], [], []
        for a in range(n):
            cp = pltpu.make_async_copy(ins[a], outs[a].at[slot(*me)], local_sems.at[a])
            cp.start()
            local.append(cp)
            first.append(copy(a, 0, me, sibling, src=ins[a]))
            first += [copy(a, 1 + j, me, (*chip, c), src=ins[a]) for j, chip in enumerate(chips)]
        for cp in first:
            cp.start()
        for j, chip in enumerate(chips):
            for a in range(n):
                copy(a, 1 + j, (*chip, c), me).wait_recv()
                cp = copy(a, 4 + j, (*chip, c), sibling)
                cp.start()
                passed.append(cp)
        for a in range(n):
            copy(a, 0, sibling, me).wait_recv()
            for j, chip in enumerate(chips):
                copy(a, 4 + j, (*chip, 1 - c), me).wait_recv()
        for cp in first + passed:
            cp.wait_send()
        for cp in local:
            cp.wait()

    out_shape, sems = _exchange_shapes([(arr, False) for arr in arrays])
    any_spec = pl.BlockSpec(memory_space=pl.ANY)
    return pl.pallas_call(
        body, name=name, in_specs=[any_spec] * n, out_specs=[any_spec] * n, out_shape=out_shape,
        scratch_shapes=sems, compiler_params=pltpu.CompilerParams(has_side_effects=True),
    )(*arrays)


def _riding_exchange(items, grid):
    n = len(items)
    flags = [sc for _, sc in items]
    out_shape, sems = _exchange_shapes(items)
    any_spec = pl.BlockSpec(memory_space=pl.ANY)

    def wrap(body, n_in, n_out):
        def fused(*refs):
            ins = refs[:n_in]
            x_ins = refs[n_in:n_in + n]
            outs = refs[n_in + n:n_in + n + n_out]
            x_outs = refs[n_in + n + n_out:n_in + 2 * n + n_out]
            rest = refs[n_in + 2 * n + n_out:]
            x_sems, scratch = rest[len(rest) - 3:], rest[:len(rest) - 3]
            first = last = True
            for d, n_d in enumerate(grid):
                first = jnp.logical_and(first, pl.program_id(d) == 0)
                last = jnp.logical_and(last, pl.program_id(d) == n_d - 1)

            @pl.when(first)
            def _():
                for cp in _exchange_copies(flags, x_ins, x_outs, *x_sems):
                    cp.start()

            body(*ins, *outs, *scratch)

            @pl.when(last)
            def _():
                for cp in _exchange_copies(flags, x_ins, x_outs, *x_sems):
                    cp.wait()

        return fused

    return [arr for arr, _ in items], [any_spec] * n, [any_spec] * n, out_shape, sems, wrap


def _sum_devices(ref):
    g = ref[0].astype(F32)
    for q in range(1, N_DEV):
        g = g + ref[q].astype(F32)
    return g


def _adam_math(g, w, m, v):
    nm = ADAM_B1 * m + (1.0 - ADAM_B1) * g
    nv = ADAM_B2 * v + (1.0 - ADAM_B2) * (g * g)
    m_hat = nm / (1.0 - ADAM_B1 ** ADAM_STEP)
    v_hat = nv / (1.0 - ADAM_B2 ** ADAM_STEP)
    return -ADAM_LR * (m_hat / (jnp.sqrt(v_hat) + ADAM_EPS) + ADAM_WD * w), nm, nv


def _adamw(grecv, w, m, v, name):
    R, C = w.shape
    tile = _pick(R, (256, 176, 128)) if R * C > 65536 else R

    def body(gr_ref, w_ref, m_ref, v_ref, g_ref, d_ref, nm_ref, nv_ref):
        g = _sum_devices(gr_ref)
        g_ref[...] = g
        d_ref[...], nm_ref[...], nv_ref[...] = _adam_math(g, w_ref[...], m_ref[...], v_ref[...])

    blk = pl.BlockSpec((tile, C), lambda i: (i, 0))
    sds = jax.ShapeDtypeStruct((R, C), F32)
    return pl.pallas_call(
        body, name=name, grid=(R // tile,),
        in_specs=[pl.BlockSpec((N_DEV, tile, C), lambda i: (0, i, 0)), blk, blk, blk],
        out_specs=[blk, blk, blk, blk], out_shape=[sds, sds, sds, sds],
        compiler_params=_cparams(("parallel",)),
    )(grecv, w, m, v)


def _adamw_replicated(grecvs, loss_recv, ws, ms, vs, name):
    nt = len(ws)

    def body(*refs):
        gr, lr = refs[:nt], refs[nt]
        w, m, v = refs[nt + 1:2 * nt + 1], refs[2 * nt + 1:3 * nt + 1], refs[3 * nt + 1:4 * nt + 1]
        outs = refs[4 * nt + 1:]
        outs[0][...] = _sum_devices(lr)
        for t in range(nt):
            g_ref, d_ref, nm_ref, nv_ref = outs[1 + 4 * t:5 + 4 * t]
            g = _sum_devices(gr[t])
            g_ref[...] = g
            d_ref[...], nm_ref[...], nv_ref[...] = _adam_math(g, w[t][...], m[t][...], v[t][...])

    out_shape = [jax.ShapeDtypeStruct((1, LANE), F32)]
    for t in range(nt):
        out_shape += [jax.ShapeDtypeStruct(ws[t].shape, F32)] * 4
    outs = pl.pallas_call(body, name=name, out_shape=out_shape, compiler_params=_cparams())(
        *grecvs, loss_recv, *ws, *ms, *vs)
    return outs[0], [outs[1 + 4 * t:5 + 4 * t] for t in range(nt)]


def _shard_cols(g):
    rows, cols = g.shape
    return g.reshape(rows, N_DEV, cols // N_DEV).transpose(1, 0, 2)


def _unshard_cols(blocks):
    return blocks.transpose(1, 0, 2).reshape(blocks.shape[1], -1)


def _heads(t, n, d):
    return t.reshape(t.shape[0], n, d).transpose(1, 0, 2)


def _unheads(t):
    return t.transpose(1, 0, 2).reshape(t.shape[1], -1)


def kernel(x, w_in, gate_up, gate_bias, gla_norm_g, w_out, ln1_g, ln1_b, w_up, conv_w, conv_b, w_down, ln2_g, ln2_b, loss_target, m_w_in, m_gate_up, m_gate_bias, m_gla_norm_g, m_w_out, m_ln1_g, m_ln1_b, m_w_up, m_conv_w, m_conv_b, m_w_down, m_ln2_g, m_ln2_b, v_w_in, v_gate_up, v_gate_bias, v_gla_norm_g, v_w_out, v_ln1_g, v_ln1_b, v_w_up, v_conv_w, v_conv_b, v_w_down, v_ln2_g, v_ln2_b):
    S, D = x.shape[1], x.shape[2]
    x2, tgt = x[0], loss_target[0]

    gathered = _all_gather_two_level([w_in[0].astype(MXU_DTYPE), gate_up[0], conv_w[0]], "gather_w_in")
    w_in_f = _unshard_cols(gathered[0])
    gate_up_f = _unshard_cols(gathered[1])
    conv_w_f = _unshard_cols(gathered[2])
    w_in_pad = jnp.pad(w_in_f, ((0, 0), (0, IN_PAD - IN_WIDTH)))
    gate_up_p = _heads(jnp.pad(gate_up_f, ((0, LANE - GLA_GATE_RANK), (0, 0))), GLA_HEADS // 2, GLA_PAIR_K)
    gate_bias_p = gate_bias.reshape(GLA_HEADS // 2, 1, GLA_PAIR_K)

    w_qkv = jnp.concatenate([w_in_f[:, :OFF_SBK] * (SB_HEAD_DIM ** -0.5), w_in_f[:, OFF_SBK:OFF_GQ]], axis=1)
    qkv = _matmul(x2, w_qkv, "nn", MXU_DTYPE, "proj_sb")
    proj = _matmul(x2, w_in_pad[:, OFF_GQ:], "nn", F32, "proj_gla")
    sb_o, sb_tot, sb_band, sb_w, g_out, g_up, g_down = _sb_fwd(
        qkv, "sb_fwd", ride=[(w_out[0].astype(MXU_DTYPE), False), (w_up[0].astype(MXU_DTYPE), False),
                             (w_down[0].astype(MXU_DTYPE), False)])
    w_out_f = g_out.reshape(-1, D)
    w_up_f = _unshard_cols(g_up)
    w_down_f = g_down.reshape(-1, D)
    gla_o, prev = _gla_fwd(proj, gate_up_p, gate_bias_p, gla_norm_g, "gla_fwd")
    cat = jnp.concatenate([sb_o, gla_o], axis=1)
    r1 = _matmul(cat, w_out_f, "nn", F32, "mix", res=x2, res_scale=DN_ALPHA)
    h = _ln_fwd(r1, ln1_g, ln1_b, "ln1")
    u0 = _matmul(h, w_up_f, "nn", F32, "ffn_up")
    p = _conv_gelu_fwd(u0, conv_w_f, conv_b, "conv_gelu")
    r2 = _matmul(p, w_down_f, "nn", F32, "ffn_down", res=h, res_scale=DN_ALPHA)
    d_r2, loss_p, g_ln2_g, g_ln2_b = _ln_loss_bwd(r2, tgt, ln2_g, ln2_b, "ln2_loss")

    d_p = _matmul(d_r2, w_down_f, "nt", MXU_DTYPE, "d_ffn_act")
    g_w_down = _matmul(p, d_r2, "tn", BF16, "grad_w_down")
    d_u0, g_conv_w, g_conv_b = _conv_gelu_bwd(u0, d_p, conv_w_f, conv_b, "conv_gelu_bwd")
    g_w_up = _matmul(h, d_u0, "tn", BF16, "grad_w_up")
    d_h = _matmul(d_u0, w_up_f, "nt", F32, "d_h", res=d_r2, res_scale=DN_ALPHA)
    d_r1, g_ln1_g, g_ln1_b = _ln_bwd(r1, d_h, ln1_g, "ln1_bwd")
    g_w_out = _matmul(cat, d_r1, "tn", BF16, "grad_w_out")
    d_cat = _matmul(d_r1, w_out_f, "nt", F32, "d_cat")
    (d_gq, d_gk, d_gv, d_gg, d_ga_pad, g_gu_p, g_gb_p, g_gnorm) = _gla_bwd(
        proj, gate_up_p, gate_bias_p, gla_norm_g, prev, d_cat, "gla_bwd")
    g_gate_up = _unheads(g_gu_p[:, :GLA_GATE_RANK, :])
    g_gate_bias = g_gb_p.reshape(1, -1)
    small_g = [g_gate_bias, g_gnorm, g_ln1_g, g_ln1_b, g_conv_b, g_ln2_g, g_ln2_b]
    d_sq, d_sk, d_sv, *recv_rest = _sb_bwd(
        qkv, d_cat, sb_tot, sb_band, sb_w, "sb_bwd",
        ride=[(g_w_out.reshape(N_DEV, -1, D), True), (_shard_cols(g_w_up), True),
              (g_w_down.reshape(N_DEV, -1, D), True), (_shard_cols(g_gate_up), True),
              (_shard_cols(g_conv_w), True)] + [(t, False) for t in small_g] + [(loss_p, False)])
    d_proj = jnp.concatenate([d_sq, d_sk, d_sv, d_gq, d_gk, d_gv, d_gg,
                              d_ga_pad.astype(MXU_DTYPE)], axis=1)
    g_w_in = _matmul(x2, d_proj, "tn", BF16, "grad_w_in")[:, :IN_WIDTH]
    d_x, recv_in = _matmul(d_proj, w_in_pad, "nt", F32, "d_x", res=d_r1, res_scale=DN_ALPHA,
                           ride=[(_shard_cols(g_w_in), True)])

    recv = [recv_in] + recv_rest[:5]
    sharded = [(w_in, m_w_in, v_w_in), (w_out, m_w_out, v_w_out), (w_up, m_w_up, v_w_up),
               (w_down, m_w_down, v_w_down), (gate_up, m_gate_up, v_gate_up), (conv_w, m_conv_w, v_conv_w)]
    upd = [_adamw(recv[n], w[0], m[0], v[0], "adamw_%d" % n) for n, (w, m, v) in enumerate(sharded)]
    loss_row, small = _adamw_replicated(
        recv_rest[5:12], recv_rest[12], [gate_bias, gla_norm_g, ln1_g, ln1_b, conv_b, ln2_g, ln2_b],
        [m_gate_bias, m_gla_norm_g, m_ln1_g, m_ln1_b, m_conv_b, m_ln2_g, m_ln2_b],
        [v_gate_bias, v_gla_norm_g, v_ln1_g, v_ln1_b, v_conv_b, v_ln2_g, v_ln2_b], "adamw_replicated")
    outs = []
    for kind in range(4):
        b_w_in, b_w_out, b_w_up, b_w_down, b_gate_up, b_conv_w = [u[kind][None] for u in upd]
        s_gb, s_gn, s_l1g, s_l1b, s_cb, s_l2g, s_l2b = [t[kind] for t in small]
        outs += [b_w_in, b_gate_up, s_gb, s_gn, b_w_out, s_l1g, s_l1b, b_w_up, b_conv_w, s_cb, b_w_down,
                 s_l2g, s_l2b]
    return (loss_row[0, 0], d_x[None], *outs)
```

```python
import math

import jax
import jax.numpy as jnp
from jax import lax
from jax.experimental import pallas as pl
from jax.experimental.pallas import tpu as pltpu

F32 = jnp.float32
BF16 = jnp.bfloat16
MXU_DTYPE = jnp.bfloat16

N_DEV = 8
D_MODEL = 1024
SB_WIDTH = 512
SB_HEADS = 8
SB_HEAD_DIM = 64
GLA_HEADS = 4
GLA_KEY_DIM = 64
GLA_VAL_DIM = 128
GLA_WIDTH = 512
GLA_GATE_RANK = 16
GLA_TAU = 16.0
CHUNK = 64
D_FF = 2816
CONV_WIDTH = 3
LN_EPS = 1e-5
RMS_EPS = 1e-6
DN_ALPHA = 2.0 ** 0.25
IN_WIDTH = 3088
LANE = 128
IN_PAD = 3200
OFF_SBQ, OFF_SBK, OFF_SBV = 0, 512, 1024
OFF_GQ, OFF_GK, OFF_GV, OFF_GG, OFF_GA = 1536, 1792, 2048, 2560, 3072
GLA_PAD = IN_PAD - OFF_GQ

ADAM_LR = 0.001
ADAM_B1 = 0.9
ADAM_B2 = 0.999
ADAM_EPS = 1e-08
ADAM_WD = 0.01
ADAM_STEP = 10

VMEM_LIMIT = 48 * 1024 * 1024
MESH_ID = pl.DeviceIdType.MESH


def _cparams(sem=None, **kw):
    return pltpu.CompilerParams(dimension_semantics=sem, vmem_limit_bytes=VMEM_LIMIT, **kw)


def _dot(a, b, dims):
    ca, cb = {"nn": (1, 0), "nt": (1, 1), "tn": (0, 0)}[dims]
    return lax.dot_general(a.astype(MXU_DTYPE), b.astype(MXU_DTYPE), (((ca,), (cb,)), ((), ())),
                           preferred_element_type=F32)


def _dot_split(a, b, dims):
    assert dims == "nn"
    hi = a.astype(BF16)
    lo = (a - hi.astype(F32)).astype(BF16)
    return lax.dot_general(jnp.concatenate([hi, lo], axis=1), jnp.concatenate([b, b], axis=0),
                           (((1,), (0,)), ((), ())), preferred_element_type=F32)


def _pick(dim, prefs):
    for p in prefs:
        if dim % p == 0:
            return p
    return dim


def _matmul(a, b, dims, out_dtype, name, res=None, res_scale=1.0, ride=()):
    if dims == "nn":
        (M, K), (_, N) = a.shape, b.shape
    elif dims == "nt":
        (M, K), (N, _) = a.shape, b.shape
    else:
        (K, M), (_, N) = a.shape, b.shape
    tm = _pick(M, (1024, 1408, 512, 256, 128))
    tn = _pick(N, (1408, 1024, 640, 512))
    if tn == N and N > 2048:
        tn = _pick(N, (256, 128))
    tk = _pick(K, (1024, 1408, 640, 512, 256, 128))
    nk = K // tk
    grid = (M // tm, N // tn, nk)
    if dims == "tn":
        a_spec = pl.BlockSpec((tk, tm), lambda i, j, k: (k, i))
    else:
        a_spec = pl.BlockSpec((tm, tk), lambda i, j, k: (i, k))
    if dims == "nt":
        b_spec = pl.BlockSpec((tn, tk), lambda i, j, k: (j, k))
    else:
        b_spec = pl.BlockSpec((tk, tn), lambda i, j, k: (k, j))
    o_spec = pl.BlockSpec((tm, tn), lambda i, j, k: (i, j))
    in_specs = [a_spec, b_spec]
    args = [a, b]
    if res is not None:
        in_specs.append(o_spec)
        args.append(res)

    def body(*refs):
        if res is not None:
            a_ref, b_ref, r_ref, o_ref, acc_ref = refs
        else:
            a_ref, b_ref, o_ref, acc_ref = refs
            r_ref = None
        k = pl.program_id(2)
        part = _dot(a_ref[...], b_ref[...], dims)

        def finish(total):
            if r_ref is not None:
                total = total + res_scale * r_ref[...]
            o_ref[...] = total.astype(o_ref.dtype)

        if nk == 1:
            finish(part)
        else:
            @pl.when(k == 0)
            def _():
                acc_ref[...] = part

            @pl.when(jnp.logical_and(k > 0, k < nk - 1))
            def _():
                acc_ref[...] += part

            @pl.when(k == nk - 1)
            def _():
                finish(acc_ref[...] + part)

    out_sds = jax.ShapeDtypeStruct((M, N), out_dtype)
    acc = pltpu.VMEM((tm, tn), F32)
    if not ride:
        return pl.pallas_call(
            body, name=name, grid=grid, in_specs=in_specs, out_specs=o_spec, out_shape=out_sds,
            scratch_shapes=[acc], compiler_params=_cparams(("parallel", "parallel", "arbitrary")),
        )(*args)
    x_args, x_in, x_out, x_shapes, x_sems, wrap = _riding_exchange(list(ride), grid)
    return pl.pallas_call(
        wrap(body, len(args), 1), name=name, grid=grid, in_specs=in_specs + x_in, out_specs=[o_spec] + x_out,
        out_shape=[out_sds] + x_shapes, scratch_shapes=[acc] + x_sems,
        compiler_params=_cparams(("arbitrary",) * 3, has_side_effects=True),
    )(*args, *x_args)


LN_ROWS = 256


def _ln_stats(r):
    mu = jnp.mean(r, axis=-1, keepdims=True)
    xc = r - mu
    var = jnp.mean(xc * xc, axis=-1, keepdims=True)
    return xc * lax.rsqrt(var + LN_EPS)


def _ln_fwd(r, g, b, name):
    S, D = r.shape

    def body(r_ref, g_ref, b_ref, h_ref):
        h_ref[...] = _ln_stats(r_ref[...]) * g_ref[...] + b_ref[...]

    row = pl.BlockSpec((LN_ROWS, D), lambda i: (i, 0))
    vec = pl.BlockSpec((1, D), lambda i: (0, 0))
    return pl.pallas_call(
        body, name=name, grid=(S // LN_ROWS,), in_specs=[row, vec, vec], out_specs=row,
        out_shape=jax.ShapeDtypeStruct((S, D), F32),
        compiler_params=_cparams(("parallel",)),
    )(r, g, b)


def _ln_bwd_core(xhat, dy, g):
    dxh = dy * g
    m1 = jnp.mean(dxh, axis=-1, keepdims=True)
    m2 = jnp.mean(dxh * xhat, axis=-1, keepdims=True)
    return dxh - m1 - xhat * m2


def _ln_bwd(r, dy, g, name):
    S, D = r.shape

    def body(r_ref, dy_ref, g_ref, dr_ref, gg_ref, gb_ref):
        x = r_ref[...]
        mu = jnp.mean(x, axis=-1, keepdims=True)
        xc = x - mu
        rstd = lax.rsqrt(jnp.mean(xc * xc, axis=-1, keepdims=True) + LN_EPS)
        xhat = xc * rstd
        dy = dy_ref[...]
        dr_ref[...] = rstd * _ln_bwd_core(xhat, dy, g_ref[...])

        @pl.when(pl.program_id(0) == 0)
        def _():
            gg_ref[...] = jnp.zeros_like(gg_ref)
            gb_ref[...] = jnp.zeros_like(gb_ref)

        gg_ref[...] += jnp.sum(dy * xhat, axis=0, keepdims=True)
        gb_ref[...] += jnp.sum(dy, axis=0, keepdims=True)

    row = pl.BlockSpec((LN_ROWS, D), lambda i: (i, 0))
    vec = pl.BlockSpec((1, D), lambda i: (0, 0))
    return pl.pallas_call(
        body, name=name, grid=(S // LN_ROWS,), in_specs=[row, row, vec], out_specs=[row, vec, vec],
        out_shape=[jax.ShapeDtypeStruct((S, D), F32), jax.ShapeDtypeStruct((1, D), F32),
                   jax.ShapeDtypeStruct((1, D), F32)],
        compiler_params=_cparams(("arbitrary",)),
    )(r, dy, g)


def _ln_loss_bwd(r, target, g, b, name):
    S, D = r.shape

    def body(r_ref, t_ref, g_ref, b_ref, dr_ref, loss_ref, gg_ref, gb_ref):
        x = r_ref[...]
        mu = jnp.mean(x, axis=-1, keepdims=True)
        xc = x - mu
        rstd = lax.rsqrt(jnp.mean(xc * xc, axis=-1, keepdims=True) + LN_EPS)
        xhat = xc * rstd
        y = xhat * g_ref[...] + b_ref[...]
        err = y - t_ref[...]
        dy = err * (1.0 / D)
        dr_ref[...] = rstd * _ln_bwd_core(xhat, dy, g_ref[...])

        @pl.when(pl.program_id(0) == 0)
        def _():
            loss_ref[...] = jnp.zeros_like(loss_ref)
            gg_ref[...] = jnp.zeros_like(gg_ref)
            gb_ref[...] = jnp.zeros_like(gb_ref)

        per_row = jnp.sum(err * err, axis=-1, keepdims=True) * (0.5 / D)
        loss_ref[...] += jnp.broadcast_to(jnp.sum(per_row, axis=0, keepdims=True), loss_ref.shape)
        gg_ref[...] += jnp.sum(dy * xhat, axis=0, keepdims=True)
        gb_ref[...] += jnp.sum(dy, axis=0, keepdims=True)

    row = pl.BlockSpec((LN_ROWS, D), lambda i: (i, 0))
    vec = pl.BlockSpec((1, D), lambda i: (0, 0))
    lvec = pl.BlockSpec((1, LANE), lambda i: (0, 0))
    return pl.pallas_call(
        body, name=name, grid=(S // LN_ROWS,), in_specs=[row, row, vec, vec],
        out_specs=[row, lvec, vec, vec],
        out_shape=[jax.ShapeDtypeStruct((S, D), F32), jax.ShapeDtypeStruct((1, LANE), F32),
                   jax.ShapeDtypeStruct((1, D), F32), jax.ShapeDtypeStruct((1, D), F32)],
        compiler_params=_cparams(("arbitrary",)),
    )(r, target, g, b)


CONV_COLS = 256
CONV_ROWS = 256
HALO = 8
INV_SQRT2 = 1.0 / math.sqrt(2.0)
INV_SQRT2PI = 1.0 / math.sqrt(2.0 * math.pi)


def _gelu(x):
    return 0.5 * x * (1.0 + lax.erf(x * INV_SQRT2))


def _gelu_and_grad(x):
    cdf = 0.5 * (1.0 + lax.erf(x * INV_SQRT2))
    return x * cdf, cdf + x * jnp.exp(-0.5 * x * x) * INV_SQRT2PI


def _conv_rows(ext, w_ref, b_ref, n):
    total = ext.shape[0]
    s1 = pltpu.roll(ext, 1, 0)
    s2 = pltpu.roll(ext, 2, 0)
    u = w_ref[2:3, :] * ext + w_ref[1:2, :] * s1 + w_ref[0:1, :] * s2 + b_ref[...]
    return u[HALO:total], s1[HALO:total], s2[HALO:total]


def _conv_gelu_fwd(u0, conv_w, conv_b, name):
    S, C2 = u0.shape
    F = C2 // 2
    ncb = F // CONV_COLS
    nrc = S // CONV_ROWS

    def body(ua_ref, uc_ref, wa_ref, wc_ref, ba_ref, bc_ref, p_ref):
        def chunk(ci, _):
            r0 = pl.multiple_of(ci * CONV_ROWS, CONV_ROWS)
            p0 = pl.multiple_of(jnp.maximum(r0 - HALO, 0), HALO)
            keep = (ci > 0).astype(F32)

            def load(ref):
                prev = ref[pl.ds(p0, HALO), :] * keep
                return jnp.concatenate([prev, ref[pl.ds(r0, CONV_ROWS), :]], axis=0)

            a, _, _ = _conv_rows(load(ua_ref), wa_ref, ba_ref, CONV_ROWS)
            c, _, _ = _conv_rows(load(uc_ref), wc_ref, bc_ref, CONV_ROWS)
            p_ref[pl.ds(r0, CONV_ROWS), :] = (_gelu(a) * c).astype(p_ref.dtype)
            return 0

        lax.fori_loop(0, nrc, chunk, 0)

    col_a = pl.BlockSpec((S, CONV_COLS), lambda j: (0, j))
    col_c = pl.BlockSpec((S, CONV_COLS), lambda j: (0, j + ncb))
    w_a = pl.BlockSpec((CONV_WIDTH, CONV_COLS), lambda j: (0, j))
    w_c = pl.BlockSpec((CONV_WIDTH, CONV_COLS), lambda j: (0, j + ncb))
    b_a = pl.BlockSpec((1, CONV_COLS), lambda j: (0, j))
    b_c = pl.BlockSpec((1, CONV_COLS), lambda j: (0, j + ncb))
    return pl.pallas_call(
        body, name=name, grid=(ncb,), in_specs=[col_a, col_c, w_a, w_c, b_a, b_c], out_specs=col_a,
        out_shape=jax.ShapeDtypeStruct((S, F), MXU_DTYPE),
        compiler_params=_cparams(("parallel",)),
    )(u0, u0, conv_w, conv_w, conv_b, conv_b)


def _conv_gelu_bwd(u0, dp, conv_w, conv_b, name):
    S, C2 = u0.shape
    F = C2 // 2
    ncb = F // CONV_COLS
    nrc = S // CONV_ROWS
    EXT = CONV_ROWS + HALO

    def body(ua_ref, uc_ref, dp_ref, wa_ref, wc_ref, ba_ref, bc_ref,
             da_ref, dc_ref, gwa_ref, gwc_ref, gba_ref, gbc_ref):
        gwa_ref[...] = jnp.zeros_like(gwa_ref)
        gwc_ref[...] = jnp.zeros_like(gwc_ref)
        gba_ref[...] = jnp.zeros_like(gba_ref)
        gbc_ref[...] = jnp.zeros_like(gbc_ref)
        rid = lax.broadcasted_iota(jnp.int32, (EXT, CONV_COLS), 0)

        def chunk(ci, _):
            r0 = pl.multiple_of(ci * CONV_ROWS, CONV_ROWS)
            p0 = pl.multiple_of(jnp.maximum(r0 - HALO, 0), HALO)
            n0 = pl.multiple_of(jnp.minimum(r0 + CONV_ROWS, S - HALO), HALO)
            keep_prev = (ci > 0).astype(F32)
            keep_next = (ci < nrc - 1).astype(F32)

            def load(ref):
                return jnp.concatenate([ref[pl.ds(p0, HALO), :] * keep_prev,
                                        ref[pl.ds(r0, CONV_ROWS), :],
                                        ref[pl.ds(n0, HALO), :] * keep_next], axis=0)

            ext_a = load(ua_ref)
            ext_c = load(uc_ref)
            a, a1, a2 = _conv_rows(ext_a, wa_ref, ba_ref, EXT)
            c, c1, c2 = _conv_rows(ext_c, wc_ref, bc_ref, EXT)
            a0 = ext_a[HALO:HALO + EXT]
            c0 = ext_c[HALO:HALO + EXT]
            dpe = jnp.concatenate([dp_ref[pl.ds(r0, CONV_ROWS), :].astype(F32),
                                   dp_ref[pl.ds(n0, HALO), :].astype(F32) * keep_next], axis=0)
            gelu_a, slope_a = _gelu_and_grad(a)
            d_a = dpe * c * slope_a
            d_c = dpe * gelu_a
            own = rid < CONV_ROWS

            def back(d_u, w_ref, x0, x1, x2, d_ref, gw_ref, gb_ref):
                d_u0 = (w_ref[2:3, :] * d_u + w_ref[1:2, :] * pltpu.roll(d_u, EXT - 1, 0)
                        + w_ref[0:1, :] * pltpu.roll(d_u, EXT - 2, 0))
                d_ref[pl.ds(r0, CONV_ROWS), :] = d_u0[0:CONV_ROWS].astype(d_ref.dtype)
                d_own = jnp.where(own, d_u, 0.0)
                gw_ref[...] += jnp.concatenate(
                    [jnp.sum(d_own * x2, axis=0, keepdims=True),
                     jnp.sum(d_own * x1, axis=0, keepdims=True),
                     jnp.sum(d_own * x0, axis=0, keepdims=True)], axis=0)
                gb_ref[...] += jnp.sum(d_own, axis=0, keepdims=True)

            back(d_a, wa_ref, a0, a1, a2, da_ref, gwa_ref, gba_ref)
            back(d_c, wc_ref, c0, c1, c2, dc_ref, gwc_ref, gbc_ref)
            return 0

        lax.fori_loop(0, nrc, chunk, 0)

    col_a = pl.BlockSpec((S, CONV_COLS), lambda j: (0, j))
    col_c = pl.BlockSpec((S, CONV_COLS), lambda j: (0, j + ncb))
    w_a = pl.BlockSpec((CONV_WIDTH, CONV_COLS), lambda j: (0, j))
    w_c = pl.BlockSpec((CONV_WIDTH, CONV_COLS), lambda j: (0, j + ncb))
    b_a = pl.BlockSpec((1, CONV_COLS), lambda j: (0, j))
    b_c = pl.BlockSpec((1, CONV_COLS), lambda j: (0, j + ncb))
    outs = pl.pallas_call(
        body, name=name, grid=(ncb,),
        in_specs=[col_a, col_c, col_a, w_a, w_c, b_a, b_c],
        out_specs=[col_a, col_a, w_a, w_a, b_a, b_a],
        out_shape=[jax.ShapeDtypeStruct((S, F), MXU_DTYPE), jax.ShapeDtypeStruct((S, F), MXU_DTYPE),
                   jax.ShapeDtypeStruct((CONV_WIDTH, F), F32), jax.ShapeDtypeStruct((CONV_WIDTH, F), F32),
                   jax.ShapeDtypeStruct((1, F), F32), jax.ShapeDtypeStruct((1, F), F32)],
        compiler_params=_cparams(("parallel",)),
    )(u0, u0, dp, conv_w, conv_w, conv_b, conv_b)
    da, dc, gwa, gwc, gba, gbc = outs
    return (jnp.concatenate([da, dc], axis=1), jnp.concatenate([gwa, gwc], axis=1),
            jnp.concatenate([gba, gbc], axis=1))


SB_TK = 128
SB_TQ_FWD = 1024
SB_TQ_BWD = 1024
SB_PAIRS_FWD = 2


def _softplus(z):
    return jnp.maximum(z, 0.0) + jnp.log(1.0 + jnp.exp(-jnp.abs(z)))


def _tri_ones(after):
    r = lax.broadcasted_iota(jnp.int32, (SB_TK, 2 * SB_TK), 0)
    c = lax.broadcasted_iota(jnp.int32, (SB_TK, 2 * SB_TK), 1)
    tri = (r > c) if after else (r < c)
    return jnp.where(c >= SB_TK, 1.0, jnp.where(tri, 1.0, 0.0)).astype(BF16)


def _sb_block_specs(S, TQ):
    NP = SB_WIDTH // LANE
    return [pl.BlockSpec((TQ, LANE), lambda p, i: (i, p)),
            pl.BlockSpec((S, LANE), lambda p, i: (0, NP + p)),
            pl.BlockSpec((S, LANE), lambda p, i: (0, 2 * NP + p))]


def _sb_fwd(qkv, name, ride=()):
    S = qkv.shape[0]
    TK = SB_TK
    TQ = min(SB_TQ_FWD, S)
    R = TQ // TK
    NP = SB_WIDTH // LANE
    PS = SB_PAIRS_FWD
    W = PS * LANE
    NH = 2 * PS
    nq = S // TQ
    nkb = S // TK
    grid = (NP // PS, nq)

    def body(q_ref, k_ref, v_ref, o_ref, wk_ref, wbuf, wsem, bbuf, bsem):
        p, i = pl.program_id(0), pl.program_id(1)
        row = lax.broadcasted_iota(jnp.int32, (TQ, TK), 0)
        col = lax.broadcasted_iota(jnp.int32, (TQ, TK), 1)
        after_ones = _tri_ones(True)
        first = lax.broadcasted_iota(jnp.int32, (TK, LANE), 1) < SB_HEAD_DIM

        def keep(slot, j):
            return pltpu.make_async_copy(wbuf.at[slot], wk_ref.at[pl.ds(p * NH, NH), i, j], wsem.at[slot])

        def keep_band(u):
            return pltpu.make_async_copy(bbuf.at[u], wk_ref.at[pl.ds(p * NH, NH), i, R * i + u], bsem.at[u])

        def block(j, carry, r0, stage):
            k0 = pl.multiple_of(j * TK, TK)
            masked = r0 is not None
            r0 = r0 or 0
            out = []
            for pr in range(PS):
                lanes = slice(pr * LANE, (pr + 1) * LANE)
                acc0, tail_a, tail_b = carry[3 * pr:3 * pr + 3]
                kab = k_ref[pl.ds(k0, TK), lanes]
                vab = v_ref[pl.ds(k0, TK), lanes]
                none = jnp.zeros_like(kab)
                k2 = jnp.concatenate([jnp.where(first, kab, none), jnp.where(first, none, kab)], axis=0)
                v2 = jnp.concatenate([jnp.where(first, vab, none), jnp.where(first, none, vab)], axis=0)
                z2 = _dot(q_ref[r0:, lanes], k2, "nt")
                tails, ws = [], []
                for hh, tail in enumerate((tail_a, tail_b)):
                    z = z2[:, hh * TK:(hh + 1) * TK]
                    sp = _softplus(z)
                    if masked:
                        strict = col[r0:] < row[r0:] - r0
                        sp = jnp.where(strict, sp, 0.0)
                    cs = _dot_split(sp, after_ones, "nn")
                    w = jnp.exp(z - sp - cs[:, :TK] - tail[r0:])
                    if masked:
                        w = jnp.where(strict, w, 0.0)
                    ws.append(w.astype(MXU_DTYPE))
                    stage[2 * pr + hh, r0:, :] = ws[-1]
                    tot = cs[:, TK:]
                    if r0:
                        tot = jnp.concatenate([jnp.zeros((r0, TK), F32), tot], axis=0)
                    tails.append(tail + tot)
                acc_r = acc0[r0:] + _dot(jnp.concatenate(ws, axis=1), v2, "nn")
                acc = jnp.concatenate([acc0[:r0], acc_r], axis=0) if r0 else acc_r
                out += [acc, tails[0], tails[1]]
            return tuple(out)

        carry = (jnp.zeros((TQ, LANE), F32), jnp.zeros((TQ, TK), F32), jnp.zeros((TQ, TK), F32)) * PS
        for u in reversed(range(R)):
            carry = block(R * i + u, carry, u * TK, bbuf.at[u])
        for u in range(R):
            keep_band(u).start()
        trips = R * i

        def below(n, c):
            slot = n % 2
            j = trips - 1 - n

            @pl.when(n >= 2)
            def _():
                keep(slot, j).wait()

            c = block(j, c, None, wbuf.at[slot])
            keep(slot, j).start()
            return c

        carry = lax.fori_loop(0, trips, below, carry)
        for back in (1, 2):

            @pl.when(trips >= back)
            def _():
                keep((trips - back) % 2, 0).wait()

        for u in range(R):
            keep_band(u).wait()
        for pr in range(PS):
            o_ref[:, pr * LANE:(pr + 1) * LANE] = carry[3 * pr].astype(o_ref.dtype)

    in_specs = [pl.BlockSpec((TQ, W), lambda p, i: (i, p)),
                pl.BlockSpec((S, W), lambda p, i: (0, NP // PS + p)),
                pl.BlockSpec((S, W), lambda p, i: (0, 2 * (NP // PS) + p))]
    ospec = pl.BlockSpec((TQ, W), lambda p, i: (i, p))
    x_args, x_in, x_out, x_shapes, x_sems, wrap = _riding_exchange(list(ride), grid)
    return pl.pallas_call(
        wrap(body, 3, 2) if ride else body, name=name, grid=grid,
        in_specs=in_specs + (x_in if ride else []),
        out_specs=[ospec, pl.BlockSpec(memory_space=pl.ANY)] + (x_out if ride else []),
        out_shape=[jax.ShapeDtypeStruct((S, SB_WIDTH), MXU_DTYPE),
                   jax.ShapeDtypeStruct((SB_HEADS, nq, nkb, TQ, TK), MXU_DTYPE)] + (x_shapes if ride else []),
        scratch_shapes=[pltpu.VMEM((2, NH, TQ, TK), MXU_DTYPE), pltpu.SemaphoreType.DMA((2,)),
                        pltpu.VMEM((R, NH, TQ, TK), MXU_DTYPE), pltpu.SemaphoreType.DMA((R,))]
        + (x_sems if ride else []),
        compiler_params=_cparams(("arbitrary", "arbitrary"), has_side_effects=True),
    )(qkv, qkv, qkv, *(x_args if ride else []))


def _sb_bwd(qkv, d_cat, w_kept, name, ride=()):
    S = qkv.shape[0]
    TK = SB_TK
    TQ = min(SB_TQ_BWD, S)
    assert TQ == min(SB_TQ_FWD, S)
    R = TQ // TK
    NP = SB_WIDTH // LANE
    grid = (NP, S // TQ)
    scale = SB_HEAD_DIM ** -0.5

    def body(q_ref, k_ref, v_ref, do_ref, wk_ref, dq_ref, dk_ref, dv_ref, dk_acc, dv_acc, wbuf, wsem, bbuf, bsem):
        p, i = pl.program_id(0), pl.program_id(1)

        @pl.when(i == 0)
        def _():
            dk_acc[...] = jnp.zeros_like(dk_acc)
            dv_acc[...] = jnp.zeros_like(dv_acc)

        row = lax.broadcasted_iota(jnp.int32, (TQ, TK), 0)
        col = lax.broadcasted_iota(jnp.int32, (TQ, TK), 1)
        before_ones = _tri_ones(False)
        first = lax.broadcasted_iota(jnp.int32, (TK, LANE), 1) < SB_HEAD_DIM
        qab = q_ref[...]
        doab = do_ref[...].astype(MXU_DTYPE)
        qdo = jnp.concatenate([qab, doab], axis=1)

        def fetch(slot, j):
            return pltpu.make_async_copy(wk_ref.at[pl.ds(2 * p, 2), i, j], wbuf.at[slot], wsem.at[slot])

        def fetch_band(u):
            return pltpu.make_async_copy(wk_ref.at[pl.ds(2 * p, 2), i, R * i + u], bbuf.at[u], bsem.at[u])

        def block(j, carry, r0, stage):
            k0 = pl.multiple_of(j * TK, TK)
            masked = r0 is not None
            r0 = r0 or 0
            kab = k_ref[pl.ds(k0, TK), :]
            vab = v_ref[pl.ds(k0, TK), :]
            none = jnp.zeros_like(kab)
            k_h = [jnp.where(first, kab, none), jnp.where(first, none, kab)]
            v_h = [jnp.where(first, vab, none), jnp.where(first, none, vab)]
            sums, dzs, ws = [], [], []
            for hh in range(2):
                gsum = carry[1 + hh]
                kv = jnp.concatenate([jnp.concatenate([k_h[hh], none], axis=1),
                                      jnp.concatenate([none, v_h[hh]], axis=1)], axis=0)
                zdw = _dot(qdo[r0:], kv, "nt")
                z = zdw[:, :TK]
                sig = jnp.exp(z - _softplus(z))
                wb = stage[hh, r0:, :]
                g = wb.astype(F32) * zdw[:, TK:]
                cg = _dot_split(g, before_ones, "nn")
                dz = g - sig * (g + cg[:, :TK] + gsum[r0:])
                if masked:
                    dz = jnp.where(col[r0:] < row[r0:] - r0, dz, 0.0)
                dzs.append(dz.astype(MXU_DTYPE))
                ws.append(wb)
                gsum_r = gsum[r0:] + cg[:, TK:]
                if r0:
                    gsum_r = jnp.concatenate([gsum[:r0], gsum_r], axis=0)
                sums.append(gsum_r)
            kvg = _dot(jnp.concatenate(dzs + ws, axis=1), qdo[r0:], "tn")
            dk_acc[pl.ds(k0, TK), :] += jnp.where(first, kvg[:TK, :LANE], kvg[TK:2 * TK, :LANE])
            dv_acc[pl.ds(k0, TK), :] += jnp.where(first, kvg[2 * TK:3 * TK, LANE:], kvg[3 * TK:, LANE:])
            dq_r = carry[0][r0:] + _dot(jnp.concatenate(dzs, axis=1), jnp.concatenate(k_h, axis=0), "nn")
            dq = jnp.concatenate([carry[0][:r0], dq_r], axis=0) if r0 else dq_r
            return (dq, *sums)

        trips = R * i
        for u in range(R):
            fetch_band(u).start()

        @pl.when(trips >= 1)
        def _():
            fetch(0, 0).start()

        def below(j, c):
            slot = j % 2

            @pl.when(j + 1 < trips)
            def _():
                fetch(1 - slot, j + 1).start()

            fetch(slot, j).wait()
            return block(j, c, None, wbuf.at[slot])

        zero = jnp.zeros((TQ, TK), F32)
        carry = lax.fori_loop(0, trips, below, (jnp.zeros((TQ, LANE), F32), zero, zero))
        for u in range(R):
            fetch_band(u).wait()
        for u in range(R):
            carry = block(trips + u, carry, u * TK, bbuf.at[u])
        dq_ref[...] = (carry[0] * scale).astype(dq_ref.dtype)

        @pl.when(i == S // TQ - 1)
        def _():
            dk_ref[...] = dk_acc[...].astype(dk_ref.dtype)
            dv_ref[...] = dv_acc[...].astype(dv_ref.dtype)

    qspec = pl.BlockSpec((TQ, LANE), lambda p, i: (i, p))
    full = pl.BlockSpec((S, LANE), lambda p, i: (0, p))
    sds = jax.ShapeDtypeStruct((S, SB_WIDTH), MXU_DTYPE)
    scratch = [pltpu.VMEM((S, LANE), F32), pltpu.VMEM((S, LANE), F32),
               pltpu.VMEM((2, 2, TQ, TK), MXU_DTYPE), pltpu.SemaphoreType.DMA((2,)),
               pltpu.VMEM((R, 2, TQ, TK), MXU_DTYPE), pltpu.SemaphoreType.DMA((R,))]
    x_args, x_in, x_out, x_shapes, x_sems, wrap = _riding_exchange(list(ride), grid)
    return pl.pallas_call(
        wrap(body, 5, 3) if ride else body, name=name, grid=grid,
        in_specs=_sb_block_specs(S, TQ) + [qspec, pl.BlockSpec(memory_space=pl.ANY)] + (x_in if ride else []),
        out_specs=[qspec, full, full] + (x_out if ride else []),
        out_shape=[sds, sds, sds] + (x_shapes if ride else []),
        scratch_shapes=scratch + (x_sems if ride else []),
        compiler_params=_cparams(("arbitrary", "arbitrary"), has_side_effects=bool(ride)),
    )(qkv, qkv, qkv, d_cat, w_kept, *(x_args if ride else []))


GLA_ROWS = 1024
GLA_GROUP_FWD = 8
GLA_GROUP_BWD = 4
GLA_PAIR_K = 2 * GLA_KEY_DIM
GLA_PAIR_V = 2 * GLA_VAL_DIM


def _log_sigmoid(x):
    return -_softplus(-x)


def _dot_split_lhs01(m01, x):
    hi = x.astype(BF16)
    lo = (x - hi.astype(F32)).astype(BF16)
    return lax.dot_general(jnp.concatenate([m01, m01], axis=1), jnp.concatenate([hi, lo], axis=0),
                           (((1,), (0,)), ((), ())), preferred_element_type=F32)


def _dot_split_tn(x, m01):
    hi = x.astype(BF16)
    lo = (x - hi.astype(F32)).astype(BF16)
    return lax.dot_general(jnp.concatenate([hi, lo], axis=0), jnp.concatenate([m01, m01], axis=0),
                           (((0,), (0,)), ((), ())), preferred_element_type=F32)


def _dot_split_nt01(m01, x):
    hi = x.astype(BF16)
    lo = (x - hi.astype(F32)).astype(BF16)
    return lax.dot_general(jnp.concatenate([m01, m01], axis=1), jnp.concatenate([hi, lo], axis=1),
                           (((1,), (1,)), ((), ())), preferred_element_type=F32)


def _gla_consts():
    C = CHUNK
    row = lax.broadcasted_iota(jnp.int32, (C, C), 0)
    col = lax.broadcasted_iota(jnp.int32, (C, C), 1)
    first = lax.broadcasted_iota(jnp.int32, (C, GLA_PAIR_K), 1) < GLA_KEY_DIM
    r = lax.broadcasted_iota(jnp.int32, (GLA_PAIR_K, GLA_PAIR_V), 0)
    c = lax.broadcasted_iota(jnp.int32, (GLA_PAIR_K, GLA_PAIR_V), 1)
    own = (r < GLA_KEY_DIM) == (c < GLA_VAL_DIM)
    rowk = lax.broadcasted_iota(jnp.int32, (C, GLA_PAIR_K), 0)
    return dict(row=row, col=col, first=first, own=own, rowk=rowk,
                incl=jnp.where(row >= col, 1.0, 0.0).astype(BF16),
                ones=jnp.ones((C, GLA_PAIR_V), BF16))


def _pack_state(state):
    top = lax.broadcasted_iota(jnp.int32, (GLA_PAIR_K, GLA_VAL_DIM), 0) < GLA_KEY_DIM
    return jnp.where(top, state[:, :GLA_VAL_DIM], state[:, GLA_VAL_DIM:])


def _unpack_state(packed):
    top = lax.broadcasted_iota(jnp.int32, (GLA_PAIR_K, GLA_VAL_DIM), 0) < GLA_KEY_DIM
    return jnp.concatenate([jnp.where(top, packed, 0.0), jnp.where(top, 0.0, packed)], axis=1)


def _gla_chunk(qc, kc, vc, gate, k):
    C = CHUNK
    row, col, first, rowk = k["row"], k["col"], k["first"], k["rowk"]
    la = _log_sigmoid(gate) * (1.0 / GLA_TAU)
    b = _dot_split_lhs01(k["incl"], la)
    b_ref = jnp.sum(jnp.where(rowk == C // 2 - 1, b, 0.0), axis=0, keepdims=True)
    b_last = jnp.sum(la, axis=0, keepdims=True)
    qs = qc * (GLA_KEY_DIM ** -0.5)
    q_in = qs * jnp.exp(b - b_ref)
    k_in = kc * jnp.exp(b_ref - b)
    k_dec = kc * jnp.exp(b_last - b)
    q_b = qs * jnp.exp(b)
    k_in_h = [jnp.where(first, k_in, 0.0), jnp.where(first, 0.0, k_in)]
    v_h = [vc[:, :GLA_VAL_DIM], vc[:, GLA_VAL_DIM:]]
    sc = [jnp.where(row >= col, _dot(q_in, k_in_h[hh], "nt"), 0.0) for hh in range(2)]
    o_intra = jnp.concatenate([_dot(sc[hh], v_h[hh], "nn") for hh in range(2)], axis=1)
    upd = jnp.where(k["own"], _dot(k_dec, vc, "tn"), 0.0)
    dec_col = jnp.exp(_dot_split_tn(la, k["ones"]))
    return dict(la=la, b=b, b_ref=b_ref, b_last=b_last, q_in=q_in, k_in=k_in, k_dec=k_dec, q_b=q_b,
                k_in_h=k_in_h, v_h=v_h, sc=sc, o_intra=o_intra, upd=upd, dec_col=dec_col)


def _rms_gate(o, gg):
    rinv = lax.rsqrt(jnp.mean(o * o, axis=-1, keepdims=True) + RMS_EPS)
    o_n = o * rinv
    sg = 1.0 / (1.0 + jnp.exp(-gg))
    return o_n, rinv, sg


def _gla_in_specs(rows_of, RB):
    PK, PV = GLA_PAIR_K, GLA_PAIR_V
    return [pl.BlockSpec((RB, PK), lambda i, p: (rows_of(i), p)),
            pl.BlockSpec((RB, PK), lambda i, p: (rows_of(i), (OFF_GK - OFF_GQ) // PK + p)),
            pl.BlockSpec((RB, PV), lambda i, p: (rows_of(i), (OFF_GV - OFF_GQ) // PV + p)),
            pl.BlockSpec((RB, PV), lambda i, p: (rows_of(i), (OFF_GG - OFF_GQ) // PV + p)),
            pl.BlockSpec((RB, LANE), lambda i, p: (rows_of(i), (OFF_GA - OFF_GQ) // LANE)),
            pl.BlockSpec((1, LANE, PK), lambda i, p: (p, 0, 0)),
            pl.BlockSpec((1, 1, PK), lambda i, p: (p, 0, 0)),
            pl.BlockSpec((1, GLA_VAL_DIM), lambda i, p: (0, 0))]


def _gla_fwd(proj, gate_up_p, gate_bias_p, gnorm, name):
    S = proj.shape[0]
    C, RB, PK, PV, dv = CHUNK, min(GLA_ROWS, S), GLA_PAIR_K, GLA_PAIR_V, GLA_VAL_DIM
    NP = GLA_HEADS // 2
    G = GLA_GROUP_FWD
    nchunk = S // C
    ngroup = RB // (C * G)

    def body(q_ref, k_ref, v_ref, gg_ref, ga_ref, gu_ref, gb_ref, gn_ref, o_ref, prev_ref, st_ref):
        i, p = pl.program_id(0), pl.program_id(1)
        k = _gla_consts()

        @pl.when(i == 0)
        def _():
            st_ref[p] = jnp.zeros((PK, PV), F32)

        def group(gi, state):
            for u in range(G):
                ci = gi * G + u
                rows = pl.ds(pl.multiple_of(ci * C, C), C)
                gate = _dot(ga_ref[rows, :], gu_ref[0], "nn") + gb_ref[0]
                f = _gla_chunk(q_ref[rows, :], k_ref[rows, :], v_ref[rows, :], gate, k)
                prev_ref[0, ci] = _pack_state(state)
                o = f["o_intra"] + _dot(f["q_b"], state, "nn")
                state = f["dec_col"] * state + f["upd"]
                ggv = gg_ref[rows, :]
                halves = []
                for hh in range(2):
                    lanes = slice(hh * dv, (hh + 1) * dv)
                    o_n, _, sg = _rms_gate(o[:, lanes], ggv[:, lanes])
                    halves.append(o_n * gn_ref[...] * (ggv[:, lanes] * sg))
                o_ref[rows, :] = jnp.concatenate(halves, axis=1).astype(o_ref.dtype)
            return state

        st_ref[p] = lax.fori_loop(0, ngroup, group, st_ref[p])

    return pl.pallas_call(
        body, name=name, grid=(S // RB, NP), in_specs=_gla_in_specs(lambda i: i, RB),
        out_specs=[pl.BlockSpec((RB, PV), lambda i, p: (i, p)),
                   pl.BlockSpec((1, RB // C, PK, dv), lambda i, p: (p, i, 0, 0))],
        out_shape=[jax.ShapeDtypeStruct((S, GLA_WIDTH), MXU_DTYPE),
                   jax.ShapeDtypeStruct((NP, nchunk, PK, dv), F32)],
        scratch_shapes=[pltpu.VMEM((NP, PK, PV), F32)],
        compiler_params=_cparams(("arbitrary", "arbitrary")),
    )(proj, proj, proj, proj, proj, gate_up_p, gate_bias_p, gnorm)


def _gla_bwd(proj, gate_up_p, gate_bias_p, gnorm, prev, d_cat, name):
    S = proj.shape[0]
    C, RB, PK, PV, dv = CHUNK, min(GLA_ROWS, S), GLA_PAIR_K, GLA_PAIR_V, GLA_VAL_DIM
    NP = GLA_HEADS // 2
    G = GLA_GROUP_BWD
    nb = S // RB
    ngroup = RB // (C * G)

    def body(q_ref, k_ref, v_ref, gg_ref, ga_ref, gu_ref, gb_ref, gn_ref, prev_ref, do_ref,
             dq_ref, dk_ref, dv_ref, dgg_ref, dga_ref, ggu_ref, ggb_ref, ggn_ref, st_ref):
        i, p = pl.program_id(0), pl.program_id(1)
        k = _gla_consts()
        row, col, first, rowk = k["row"], k["col"], k["first"], k["rowk"]
        gu = gu_ref[0]

        @pl.when(i == 0)
        def _():
            st_ref[p] = jnp.zeros((PK, PV), F32)
            ggu_ref[p] = jnp.zeros((LANE, PK), F32)
            ggb_ref[p] = jnp.zeros((1, PK), F32)

        @pl.when(jnp.logical_and(i == 0, p == 0))
        def _():
            ggn_ref[...] = jnp.zeros_like(ggn_ref)

        @pl.when(p == 0)
        def _():
            dga_ref[...] = jnp.zeros_like(dga_ref)

        upper_incl = jnp.where(col >= row, 1.0, 0.0).astype(BF16)
        ones_8 = jnp.ones((8, PV), BF16)

        def group(gn, dstate):
            gi = ngroup - 1 - gn
            for u in reversed(range(G)):
                ci = gi * G + u
                rows = pl.ds(pl.multiple_of(ci * C, C), C)
                ga = ga_ref[rows, :]
                gate = _dot(ga, gu, "nn") + gb_ref[0]
                vc = v_ref[rows, :]
                f = _gla_chunk(q_ref[rows, :], k_ref[rows, :], vc, gate, k)
                state = _unpack_state(prev_ref[0, ci])
                o = f["o_intra"] + _dot(f["q_b"], state, "nn")
                ggv = gg_ref[rows, :]
                dout = do_ref[rows, :]
                d_o_h, dgg_h = [], []
                for hh in range(2):
                    lanes = slice(hh * dv, (hh + 1) * dv)
                    o_n, rinv, sg = _rms_gate(o[:, lanes], ggv[:, lanes])
                    silu = ggv[:, lanes] * sg
                    dgg_h.append(dout[:, lanes] * o_n * gn_ref[...] * (sg * (1.0 + ggv[:, lanes] * (1.0 - sg))))
                    d_ong = dout[:, lanes] * silu
                    ggn_ref[...] += jnp.sum(d_ong * o_n, axis=0, keepdims=True)
                    d_on = d_ong * gn_ref[...]
                    d_o_h.append(rinv * (d_on - o_n * jnp.mean(d_on * o_n, axis=-1, keepdims=True)))
                dgg_ref[rows, :] = jnp.concatenate(dgg_h, axis=1).astype(dgg_ref.dtype)
                d_o = jnp.concatenate(d_o_h, axis=1)
                d_upd = jnp.where(k["own"], dstate, 0.0)
                d_dec_col = dstate * state * f["dec_col"]
                dstate = f["dec_col"] * dstate + _dot(f["q_b"], d_o, "tn")
                dsc = [jnp.where(row >= col, _dot(d_o_h[hh], f["v_h"][hh], "nt"), 0.0) for hh in range(2)]
                dv_ref[rows, :] = (jnp.concatenate([_dot(f["sc"][hh], d_o_h[hh], "tn") for hh in range(2)], axis=1)
                                   + _dot(f["k_dec"], d_upd, "nn")).astype(dv_ref.dtype)
                q_in_h = [jnp.where(first, f["q_in"], 0.0), jnp.where(first, 0.0, f["q_in"])]
                dq_in = _dot(dsc[0], f["k_in_h"][0], "nn") + _dot(dsc[1], f["k_in_h"][1], "nn")
                dk_in = _dot(dsc[0], q_in_h[0], "tn") + _dot(dsc[1], q_in_h[1], "tn")
                dq_b = _dot(d_o, state, "nt")
                dkdec = _dot(vc, d_upd, "nt")
                b = f["b"]
                e1 = jnp.exp(b - f["b_ref"])
                e2 = jnp.exp(f["b_ref"] - b)
                e3 = jnp.exp(f["b_last"] - b)
                eb = jnp.exp(b)
                dq_ref[rows, :] = ((dq_in * e1 + dq_b * eb) * (GLA_KEY_DIM ** -0.5)).astype(dq_ref.dtype)
                dk_ref[rows, :] = (dk_in * e2 + dkdec * e3).astype(dk_ref.dtype)
                t_q = dq_in * f["q_in"]
                t_k = dk_in * f["k_in"]
                t_d = dkdec * f["k_dec"]
                db = t_q - t_k - t_d + dq_b * f["q_b"]
                db_ref = jnp.sum(t_k - t_q, axis=0, keepdims=True)
                db_last = (jnp.sum(t_d, axis=0, keepdims=True)
                           + jnp.max(_dot_split_nt01(ones_8, d_dec_col), axis=0, keepdims=True))
                db = db + jnp.where(rowk == C // 2 - 1, db_ref, 0.0) + jnp.where(rowk == C - 1, db_last, 0.0)
                dla = _dot_split_lhs01(upper_incl, db)
                d_gate = dla * (1.0 / GLA_TAU) * (1.0 / (1.0 + jnp.exp(gate)))
                ggb_ref[p] += jnp.sum(d_gate, axis=0, keepdims=True)
                ggu_ref[p] += _dot(ga, d_gate, "tn")
                dga_ref[rows, :] += _dot(d_gate, gu, "nt")
            return dstate

        st_ref[p] = lax.fori_loop(0, ngroup, group, st_ref[p])

    back = lambda i: nb - 1 - i
    NPV = SB_WIDTH // PV
    return pl.pallas_call(
        body, name=name, grid=(nb, NP),
        in_specs=_gla_in_specs(back, RB) + [pl.BlockSpec((1, RB // C, PK, dv), lambda i, p: (p, back(i), 0, 0)),
                                            pl.BlockSpec((RB, PV), lambda i, p: (back(i), NPV + p))],
        out_specs=[pl.BlockSpec((RB, PK), lambda i, p: (back(i), p)),
                   pl.BlockSpec((RB, PK), lambda i, p: (back(i), p)),
                   pl.BlockSpec((RB, PV), lambda i, p: (back(i), p)),
                   pl.BlockSpec((RB, PV), lambda i, p: (back(i), p)),
                   pl.BlockSpec((RB, LANE), lambda i, p: (back(i), 0)),
                   pl.BlockSpec((NP, LANE, PK), lambda i, p: (0, 0, 0)),
                   pl.BlockSpec((NP, 1, PK), lambda i, p: (0, 0, 0)),
                   pl.BlockSpec((1, dv), lambda i, p: (0, 0))],
        out_shape=[jax.ShapeDtypeStruct((S, NP * PK), MXU_DTYPE), jax.ShapeDtypeStruct((S, NP * PK), MXU_DTYPE),
                   jax.ShapeDtypeStruct((S, GLA_WIDTH), MXU_DTYPE), jax.ShapeDtypeStruct((S, GLA_WIDTH), MXU_DTYPE),
                   jax.ShapeDtypeStruct((S, LANE), F32), jax.ShapeDtypeStruct((NP, LANE, PK), F32),
                   jax.ShapeDtypeStruct((NP, 1, PK), F32), jax.ShapeDtypeStruct((1, dv), F32)],
        scratch_shapes=[pltpu.VMEM((NP, PK, PV), F32)],
        compiler_params=_cparams(("arbitrary", "arbitrary")),
    )(proj, proj, proj, proj, proj, gate_up_p, gate_bias_p, gnorm, prev, d_cat)


def _exchange_copies(scatter_flags, ins, outs, send_sems, recv_sems, local_sems):
    n_peer = N_DEV - 1
    x, y, c = lax.axis_index("x"), lax.axis_index("y"), lax.axis_index("c")
    me = 4 * x + 2 * y + c
    copies = []
    for a, scatter in enumerate(scatter_flags):
        own = ins[a].at[me] if scatter else ins[a]
        copies.append(pltpu.make_async_copy(own, outs[a].at[me], local_sems.at[a]))
    for r in range(1, N_DEV):
        px = 1 - x if r & 4 else x
        py = 1 - y if r & 2 else y
        pc = 1 - c if r & 1 else c
        for a, scatter in enumerate(scatter_flags):
            src = ins[a].at[4 * px + 2 * py + pc] if scatter else ins[a]
            copies.append(pltpu.make_async_remote_copy(
                src_ref=src, dst_ref=outs[a].at[me],
                send_sem=send_sems.at[a * n_peer + r - 1], recv_sem=recv_sems.at[a * n_peer + r - 1],
                device_id=(px, py, pc), device_id_type=MESH_ID))
    return copies


def _exchange_shapes(items):
    out_shape = []
    for arr, scatter in items:
        shp = arr.shape if scatter else (N_DEV,) + arr.shape
        out_shape.append(jax.ShapeDtypeStruct(shp, arr.dtype))
    n = len(items)
    sems = [pltpu.SemaphoreType.DMA((n * (N_DEV - 1),)), pltpu.SemaphoreType.DMA((n * (N_DEV - 1),)),
            pltpu.SemaphoreType.DMA((n,))]
    return out_shape, sems


def _exchange(items, name):
    n = len(items)
    flags = [sc for _, sc in items]

    def body(*refs):
        copies = _exchange_copies(flags, refs[:n], refs[n:2 * n], *refs[2 * n:])
        for cp in copies:
            cp.start()
        for cp in copies:
            cp.wait()

    out_shape, sems = _exchange_shapes(items)
    any_spec = pl.BlockSpec(memory_space=pl.ANY)
    return pl.pallas_call(
        body, name=name, in_specs=[any_spec] * n, out_specs=[any_spec] * n, out_shape=out_shape,
        scratch_shapes=sems, compiler_params=pltpu.CompilerParams(has_side_effects=True),
    )(*[arr for arr, _ in items])


def _all_gather_two_level(arrays, name):
    n = len(arrays)
    K = N_DEV - 1

    def body(*refs):
        ins, outs = refs[:n], refs[n:2 * n]
        send_sems, recv_sems, local_sems = refs[2 * n:]
        x, y, c = lax.axis_index("x"), lax.axis_index("y"), lax.axis_index("c")
        sibling = (x, y, 1 - c)
        chips = [(1 - x, y), (x, 1 - y), (1 - x, 1 - y)]

        def slot(px, py, pc):
            return 4 * px + 2 * py + pc

        def copy(a, k, block, to, src=None):
            rows = outs[a].at[slot(*block)]
            return pltpu.make_async_remote_copy(
                src_ref=rows if src is None else src, dst_ref=rows,
                send_sem=send_sems.at[a * K + k], recv_sem=recv_sems.at[a * K + k],
                device_id=to, device_id_type=MESH_ID)

        me = (x, y, c)
        local, first, passed = [], [], []
        for a in range(n):
            cp = pltpu.make_async_copy(ins[a], outs[a].at[slot(*me)], local_sems.at[a])
            cp.start()
            local.append(cp)
            first.append(copy(a, 0, me, sibling, src=ins[a]))
            first += [copy(a, 1 + j, me, (*chip, c), src=ins[a]) for j, chip in enumerate(chips)]
        for cp in first:
            cp.start()
        for j, chip in enumerate(chips):
            for a in range(n):
                copy(a, 1 + j, (*chip, c), me).wait_recv()
                cp = copy(a, 4 + j, (*chip, c), sibling)
                cp.start()
                passed.append(cp)
        for a in range(n):
            copy(a, 0, sibling, me).wait_recv()
            for j, chip in enumerate(chips):
                copy(a, 4 + j, (*chip, 1 - c), me).wait_recv()
        for cp in first + passed:
            cp.wait_send()
        for cp in local:
            cp.wait()

    out_shape, sems = _exchange_shapes([(arr, False) for arr in arrays])
    any_spec = pl.BlockSpec(memory_space=pl.ANY)
    return pl.pallas_call(
        body, name=name, in_specs=[any_spec] * n, out_specs=[any_spec] * n, out_shape=out_shape,
        scratch_shapes=sems, compiler_params=pltpu.CompilerParams(has_side_effects=True),
    )(*arrays)


def _riding_exchange(items, grid):
    n = len(items)
    flags = [sc for _, sc in items]
    out_shape, sems = _exchange_shapes(items)
    any_spec = pl.BlockSpec(memory_space=pl.ANY)

    def wrap(body, n_in, n_out):
        def fused(*refs):
            ins = refs[:n_in]
            x_ins = refs[n_in:n_in + n]
            outs = refs[n_in + n:n_in + n + n_out]
            x_outs = refs[n_in + n + n_out:n_in + 2 * n + n_out]
            rest = refs[n_in + 2 * n + n_out:]
            x_sems, scratch = rest[len(rest) - 3:], rest[:len(rest) - 3]
            first = last = True
            for d, n_d in enumerate(grid):
                first = jnp.logical_and(first, pl.program_id(d) == 0)
                last = jnp.logical_and(last, pl.program_id(d) == n_d - 1)

            @pl.when(first)
            def _():
                for cp in _exchange_copies(flags, x_ins, x_outs, *x_sems):
                    cp.start()

            body(*ins, *outs, *scratch)

            @pl.when(last)
            def _():
                for cp in _exchange_copies(flags, x_ins, x_outs, *x_sems):
                    cp.wait()

        return fused

    return [arr for arr, _ in items], [any_spec] * n, [any_spec] * n, out_shape, sems, wrap


def _sum_devices(ref):
    g = ref[0].astype(F32)
    for q in range(1, N_DEV):
        g = g + ref[q].astype(F32)
    return g


def _adam_math(g, w, m, v):
    nm = ADAM_B1 * m + (1.0 - ADAM_B1) * g
    nv = ADAM_B2 * v + (1.0 - ADAM_B2) * (g * g)
    m_hat = nm / (1.0 - ADAM_B1 ** ADAM_STEP)
    v_hat = nv / (1.0 - ADAM_B2 ** ADAM_STEP)
    return -ADAM_LR * (m_hat / (jnp.sqrt(v_hat) + ADAM_EPS) + ADAM_WD * w), nm, nv


def _adamw(grecv, w, m, v, name):
    R, C = w.shape
    tile = _pick(R, (256, 176, 128)) if R * C > 65536 else R

    def body(gr_ref, w_ref, m_ref, v_ref, g_ref, d_ref, nm_ref, nv_ref):
        g = _sum_devices(gr_ref)
        g_ref[...] = g
        d_ref[...], nm_ref[...], nv_ref[...] = _adam_math(g, w_ref[...], m_ref[...], v_ref[...])

    blk = pl.BlockSpec((tile, C), lambda i: (i, 0))
    sds = jax.ShapeDtypeStruct((R, C), F32)
    return pl.pallas_call(
        body, name=name, grid=(R // tile,),
        in_specs=[pl.BlockSpec((N_DEV, tile, C), lambda i: (0, i, 0)), blk, blk, blk],
        out_specs=[blk, blk, blk, blk], out_shape=[sds, sds, sds, sds],
        compiler_params=_cparams(("parallel",)),
    )(grecv, w, m, v)


def _adamw_replicated(grecvs, loss_recv, ws, ms, vs, name):
    nt = len(ws)

    def body(*refs):
        gr, lr = refs[:nt], refs[nt]
        w, m, v = refs[nt + 1:2 * nt + 1], refs[2 * nt + 1:3 * nt + 1], refs[3 * nt + 1:4 * nt + 1]
        outs = refs[4 * nt + 1:]
        outs[0][...] = _sum_devices(lr)
        for t in range(nt):
            g_ref, d_ref, nm_ref, nv_ref = outs[1 + 4 * t:5 + 4 * t]
            g = _sum_devices(gr[t])
            g_ref[...] = g
            d_ref[...], nm_ref[...], nv_ref[...] = _adam_math(g, w[t][...], m[t][...], v[t][...])

    out_shape = [jax.ShapeDtypeStruct((1, LANE), F32)]
    for t in range(nt):
        out_shape += [jax.ShapeDtypeStruct(ws[t].shape, F32)] * 4
    outs = pl.pallas_call(body, name=name, out_shape=out_shape, compiler_params=_cparams())(
        *grecvs, loss_recv, *ws, *ms, *vs)
    return outs[0], [outs[1 + 4 * t:5 + 4 * t] for t in range(nt)]


def _shard_cols(g):
    rows, cols = g.shape
    return g.reshape(rows, N_DEV, cols // N_DEV).transpose(1, 0, 2)


def _unshard_cols(blocks):
    return blocks.transpose(1, 0, 2).reshape(blocks.shape[1], -1)


def _heads(t, n, d):
    return t.reshape(t.shape[0], n, d).transpose(1, 0, 2)


def _unheads(t):
    return t.transpose(1, 0, 2).reshape(t.shape[1], -1)


def kernel(x, w_in, gate_up, gate_bias, gla_norm_g, w_out, ln1_g, ln1_b, w_up, conv_w, conv_b, w_down, ln2_g, ln2_b, loss_target, m_w_in, m_gate_up, m_gate_bias, m_gla_norm_g, m_w_out, m_ln1_g, m_ln1_b, m_w_up, m_conv_w, m_conv_b, m_w_down, m_ln2_g, m_ln2_b, v_w_in, v_gate_up, v_gate_bias, v_gla_norm_g, v_w_out, v_ln1_g, v_ln1_b, v_w_up, v_conv_w, v_conv_b, v_w_down, v_ln2_g, v_ln2_b):
    S, D = x.shape[1], x.shape[2]
    x2, tgt = x[0], loss_target[0]

    gathered = _all_gather_two_level([w_in[0].astype(MXU_DTYPE), gate_up[0], conv_w[0]], "gather_w_in")
    w_in_f = _unshard_cols(gathered[0])
    gate_up_f = _unshard_cols(gathered[1])
    conv_w_f = _unshard_cols(gathered[2])
    w_in_pad = jnp.pad(w_in_f, ((0, 0), (0, IN_PAD - IN_WIDTH)))
    gate_up_p = _heads(jnp.pad(gate_up_f, ((0, LANE - GLA_GATE_RANK), (0, 0))), GLA_HEADS // 2, GLA_PAIR_K)
    gate_bias_p = gate_bias.reshape(GLA_HEADS // 2, 1, GLA_PAIR_K)

    w_qkv = jnp.concatenate([w_in_f[:, :OFF_SBK] * (SB_HEAD_DIM ** -0.5), w_in_f[:, OFF_SBK:OFF_GQ]], axis=1)
    qkv = _matmul(x2, w_qkv, "nn", MXU_DTYPE, "proj_sb")
    proj = _matmul(x2, w_in_pad[:, OFF_GQ:], "nn", F32, "proj_gla")
    sb_o, sb_w, g_out, g_up, g_down = _sb_fwd(
        qkv, "sb_fwd", ride=[(w_out[0].astype(MXU_DTYPE), False), (w_up[0].astype(MXU_DTYPE), False),
                             (w_down[0].astype(MXU_DTYPE), False)])
    w_out_f = g_out.reshape(-1, D)
    w_up_f = _unshard_cols(g_up)
    w_down_f = g_down.reshape(-1, D)
    gla_o, prev = _gla_fwd(proj, gate_up_p, gate_bias_p, gla_norm_g, "gla_fwd")
    cat = jnp.concatenate([sb_o, gla_o], axis=1)
    r1 = _matmul(cat, w_out_f, "nn", F32, "mix", res=x2, res_scale=DN_ALPHA)
    h = _ln_fwd(r1, ln1_g, ln1_b, "ln1")
    u0 = _matmul(h, w_up_f, "nn", F32, "ffn_up")
    p = _conv_gelu_fwd(u0, conv_w_f, conv_b, "conv_gelu")
    r2 = _matmul(p, w_down_f, "nn", F32, "ffn_down", res=h, res_scale=DN_ALPHA)
    d_r2, loss_p, g_ln2_g, g_ln2_b = _ln_loss_bwd(r2, tgt, ln2_g, ln2_b, "ln2_loss")

    d_p = _matmul(d_r2, w_down_f, "nt", MXU_DTYPE, "d_ffn_act")
    g_w_down = _matmul(p, d_r2, "tn", BF16, "grad_w_down")
    d_u0, g_conv_w, g_conv_b = _conv_gelu_bwd(u0, d_p, conv_w_f, conv_b, "conv_gelu_bwd")
    g_w_up = _matmul(h, d_u0, "tn", BF16, "grad_w_up")
    d_h = _matmul(d_u0, w_up_f, "nt", F32, "d_h", res=d_r2, res_scale=DN_ALPHA)
    d_r1, g_ln1_g, g_ln1_b = _ln_bwd(r1, d_h, ln1_g, "ln1_bwd")
    g_w_out = _matmul(cat, d_r1, "tn", BF16, "grad_w_out")
    d_cat = _matmul(d_r1, w_out_f, "nt", F32, "d_cat")
    (d_gq, d_gk, d_gv, d_gg, d_ga_pad, g_gu_p, g_gb_p, g_gnorm) = _gla_bwd(
        proj, gate_up_p, gate_bias_p, gla_norm_g, prev, d_cat, "gla_bwd")
    g_gate_up = _unheads(g_gu_p[:, :GLA_GATE_RANK, :])
    g_gate_bias = g_gb_p.reshape(1, -1)
    small_g = [g_gate_bias, g_gnorm, g_ln1_g, g_ln1_b, g_conv_b, g_ln2_g, g_ln2_b]
    d_sq, d_sk, d_sv, *recv_rest = _sb_bwd(
        qkv, d_cat, sb_w, "sb_bwd",
        ride=[(g_w_out.reshape(N_DEV, -1, D), True), (_shard_cols(g_w_up), True),
              (g_w_down.reshape(N_DEV, -1, D), True), (_shard_cols(g_gate_up), True),
              (_shard_cols(g_conv_w), True)] + [(t, False) for t in small_g] + [(loss_p, False)])
    d_proj = jnp.concatenate([d_sq, d_sk, d_sv, d_gq, d_gk, d_gv, d_gg,
                              d_ga_pad.astype(MXU_DTYPE)], axis=1)
    g_w_in = _matmul(x2, d_proj, "tn", BF16, "grad_w_in")[:, :IN_WIDTH]
    d_x, recv_in = _matmul(d_proj, w_in_pad, "nt", F32, "d_x", res=d_r1, res_scale=DN_ALPHA,
                           ride=[(_shard_cols(g_w_in), True)])

    recv = [recv_in] + recv_rest[:5]
    sharded = [(w_in, m_w_in, v_w_in), (w_out, m_w_out, v_w_out), (w_up, m_w_up, v_w_up),
               (w_down, m_w_down, v_w_down), (gate_up, m_gate_up, v_gate_up), (conv_w, m_conv_w, v_conv_w)]
    upd = [_adamw(recv[n], w[0], m[0], v[0], "adamw_%d" % n) for n, (w, m, v) in enumerate(sharded)]
    loss_row, small = _adamw_replicated(
        recv_rest[5:12], recv_rest[12], [gate_bias, gla_norm_g, ln1_g, ln1_b, conv_b, ln2_g, ln2_b],
        [m_gate_bias, m_gla_norm_g, m_ln1_g, m_ln1_b, m_conv_b, m_ln2_g, m_ln2_b],
        [v_gate_bias, v_gla_norm_g, v_ln1_g, v_ln1_b, v_conv_b, v_ln2_g, v_ln2_b], "adamw_replicated")
    outs = []
    for kind in range(4):
        b_w_in, b_w_out, b_w_up, b_w_down, b_gate_up, b_conv_w = [u[kind][None] for u in upd]
        s_gb, s_gn, s_l1g, s_l1b, s_cb, s_l2g, s_l2b = [t[kind] for t in small]
        outs += [b_w_in, b_gate_up, s_gb, s_gn, b_w_out, s_l1g, s_l1b, b_w_up, b_conv_w, s_cb, b_w_down,
                 s_l2g, s_l2b]
    return (loss_row[0, 0], d_x[None], *outs)
```

```python
import math

import jax
import jax.numpy as jnp
from jax import lax
from jax.experimental import pallas as pl
from jax.experimental.pallas import tpu as pltpu

F32 = jnp.float32
BF16 = jnp.bfloat16
MXU_DTYPE = jnp.bfloat16

N_DEV = 8
D_MODEL = 1024
SB_WIDTH = 512
SB_HEADS = 8
SB_HEAD_DIM = 64
GLA_HEADS = 4
GLA_KEY_DIM = 64
GLA_VAL_DIM = 128
GLA_WIDTH = 512
GLA_GATE_RANK = 16
GLA_TAU = 16.0
CHUNK = 64
D_FF = 2816
CONV_WIDTH = 3
LN_EPS = 1e-5
RMS_EPS = 1e-6
DN_ALPHA = 2.0 ** 0.25
IN_WIDTH = 3088
LANE = 128
IN_PAD = 3200
OFF_SBQ, OFF_SBK, OFF_SBV = 0, 512, 1024
OFF_GQ, OFF_GK, OFF_GV, OFF_GG, OFF_GA = 1536, 1792, 2048, 2560, 3072
GLA_PAD = IN_PAD - OFF_GQ

ADAM_LR = 0.001
ADAM_B1 = 0.9
ADAM_B2 = 0.999
ADAM_EPS = 1e-08
ADAM_WD = 0.01
ADAM_STEP = 10

VMEM_LIMIT = 48 * 1024 * 1024
MESH_ID = pl.DeviceIdType.MESH


def _cparams(sem=None, **kw):
    return pltpu.CompilerParams(dimension_semantics=sem, vmem_limit_bytes=VMEM_LIMIT, **kw)


def _dot(a, b, dims):
    ca, cb = {"nn": (1, 0), "nt": (1, 1), "tn": (0, 0)}[dims]
    return lax.dot_general(a.astype(MXU_DTYPE), b.astype(MXU_DTYPE), (((ca,), (cb,)), ((), ())),
                           preferred_element_type=F32)


def _dot_split(a, b, dims):
    assert dims == "nn"
    hi = a.astype(BF16)
    lo = (a - hi.astype(F32)).astype(BF16)
    return lax.dot_general(jnp.concatenate([hi, lo], axis=1), jnp.concatenate([b, b], axis=0),
                           (((1,), (0,)), ((), ())), preferred_element_type=F32)


def _pick(dim, prefs):
    for p in prefs:
        if dim % p == 0:
            return p
    return dim


def _matmul(a, b, dims, out_dtype, name, res=None, res_scale=1.0, ride=()):
    if dims == "nn":
        (M, K), (_, N) = a.shape, b.shape
    elif dims == "nt":
        (M, K), (N, _) = a.shape, b.shape
    else:
        (K, M), (_, N) = a.shape, b.shape
    tm = _pick(M, (1024, 1408, 512, 256, 128))
    tn = _pick(N, (1408, 1024, 640, 512))
    if tn == N and N > 2048:
        tn = _pick(N, (256, 128))
    tk = _pick(K, (1024, 1408, 640, 512, 256, 128))
    nk = K // tk
    grid = (M // tm, N // tn, nk)
    if dims == "tn":
        a_spec = pl.BlockSpec((tk, tm), lambda i, j, k: (k, i))
    else:
        a_spec = pl.BlockSpec((tm, tk), lambda i, j, k: (i, k))
    if dims == "nt":
        b_spec = pl.BlockSpec((tn, tk), lambda i, j, k: (j, k))
    else:
        b_spec = pl.BlockSpec((tk, tn), lambda i, j, k: (k, j))
    o_spec = pl.BlockSpec((tm, tn), lambda i, j, k: (i, j))
    in_specs = [a_spec, b_spec]
    args = [a, b]
    if res is not None:
        in_specs.append(o_spec)
        args.append(res)

    def body(*refs):
        if res is not None:
            a_ref, b_ref, r_ref, o_ref, acc_ref = refs
        else:
            a_ref, b_ref, o_ref, acc_ref = refs
            r_ref = None
        k = pl.program_id(2)
        part = _dot(a_ref[...], b_ref[...], dims)

        def finish(total):
            if r_ref is not None:
                total = total + res_scale * r_ref[...]
            o_ref[...] = total.astype(o_ref.dtype)

        if nk == 1:
            finish(part)
        else:
            @pl.when(k == 0)
            def _():
                acc_ref[...] = part

            @pl.when(jnp.logical_and(k > 0, k < nk - 1))
            def _():
                acc_ref[...] += part

            @pl.when(k == nk - 1)
            def _():
                finish(acc_ref[...] + part)

    out_sds = jax.ShapeDtypeStruct((M, N), out_dtype)
    acc = pltpu.VMEM((tm, tn), F32)
    if not ride:
        return pl.pallas_call(
            body, name=name, grid=grid, in_specs=in_specs, out_specs=o_spec, out_shape=out_sds,
            scratch_shapes=[acc], compiler_params=_cparams(("parallel", "parallel", "arbitrary")),
        )(*args)
    x_args, x_in, x_out, x_shapes, x_sems, wrap = _riding_exchange(list(ride), grid)
    return pl.pallas_call(
        wrap(body, len(args), 1), name=name, grid=grid, in_specs=in_specs + x_in, out_specs=[o_spec] + x_out,
        out_shape=[out_sds] + x_shapes, scratch_shapes=[acc] + x_sems,
        compiler_params=_cparams(("arbitrary",) * 3, has_side_effects=True),
    )(*args, *x_args)


LN_ROWS = 512


def _ln_stats(r):
    mu = jnp.mean(r, axis=-1, keepdims=True)
    xc = r - mu
    var = jnp.mean(xc * xc, axis=-1, keepdims=True)
    return xc * lax.rsqrt(var + LN_EPS)


def _ln_fwd(r, g, b, name):
    S, D = r.shape

    def body(r_ref, g_ref, b_ref, h_ref):
        h_ref[...] = _ln_stats(r_ref[...]) * g_ref[...] + b_ref[...]

    row = pl.BlockSpec((LN_ROWS, D), lambda i: (i, 0))
    vec = pl.BlockSpec((1, D), lambda i: (0, 0))
    return pl.pallas_call(
        body, name=name, grid=(S // LN_ROWS,), in_specs=[row, vec, vec], out_specs=row,
        out_shape=jax.ShapeDtypeStruct((S, D), F32),
        compiler_params=_cparams(("parallel",)),
    )(r, g, b)


def _ln_bwd_core(xhat, dy, g):
    dxh = dy * g
    m1 = jnp.mean(dxh, axis=-1, keepdims=True)
    m2 = jnp.mean(dxh * xhat, axis=-1, keepdims=True)
    return dxh - m1 - xhat * m2


def _ln_bwd(r, dy, g, name):
    S, D = r.shape

    def body(r_ref, dy_ref, g_ref, dr_ref, gg_ref, gb_ref):
        x = r_ref[...]
        mu = jnp.mean(x, axis=-1, keepdims=True)
        xc = x - mu
        rstd = lax.rsqrt(jnp.mean(xc * xc, axis=-1, keepdims=True) + LN_EPS)
        xhat = xc * rstd
        dy = dy_ref[...]
        dr_ref[...] = rstd * _ln_bwd_core(xhat, dy, g_ref[...])

        @pl.when(pl.program_id(0) == 0)
        def _():
            gg_ref[...] = jnp.zeros_like(gg_ref)
            gb_ref[...] = jnp.zeros_like(gb_ref)

        gg_ref[...] += jnp.sum(dy * xhat, axis=0, keepdims=True)
        gb_ref[...] += jnp.sum(dy, axis=0, keepdims=True)

    row = pl.BlockSpec((LN_ROWS, D), lambda i: (i, 0))
    vec = pl.BlockSpec((1, D), lambda i: (0, 0))
    return pl.pallas_call(
        body, name=name, grid=(S // LN_ROWS,), in_specs=[row, row, vec], out_specs=[row, vec, vec],
        out_shape=[jax.ShapeDtypeStruct((S, D), F32), jax.ShapeDtypeStruct((1, D), F32),
                   jax.ShapeDtypeStruct((1, D), F32)],
        compiler_params=_cparams(("arbitrary",)),
    )(r, dy, g)


def _ln_loss_bwd(r, target, g, b, name):
    S, D = r.shape

    def body(r_ref, t_ref, g_ref, b_ref, dr_ref, loss_ref, gg_ref, gb_ref):
        x = r_ref[...]
        mu = jnp.mean(x, axis=-1, keepdims=True)
        xc = x - mu
        rstd = lax.rsqrt(jnp.mean(xc * xc, axis=-1, keepdims=True) + LN_EPS)
        xhat = xc * rstd
        y = xhat * g_ref[...] + b_ref[...]
        err = y - t_ref[...]
        dy = err * (1.0 / D)
        dr_ref[...] = rstd * _ln_bwd_core(xhat, dy, g_ref[...])

        @pl.when(pl.program_id(0) == 0)
        def _():
            loss_ref[...] = jnp.zeros_like(loss_ref)
            gg_ref[...] = jnp.zeros_like(gg_ref)
            gb_ref[...] = jnp.zeros_like(gb_ref)

        per_row = jnp.sum(err * err, axis=-1, keepdims=True) * (0.5 / D)
        loss_ref[...] += jnp.broadcast_to(jnp.sum(per_row, axis=0, keepdims=True), loss_ref.shape)
        gg_ref[...] += jnp.sum(dy * xhat, axis=0, keepdims=True)
        gb_ref[...] += jnp.sum(dy, axis=0, keepdims=True)

    row = pl.BlockSpec((LN_ROWS, D), lambda i: (i, 0))
    vec = pl.BlockSpec((1, D), lambda i: (0, 0))
    lvec = pl.BlockSpec((1, LANE), lambda i: (0, 0))
    return pl.pallas_call(
        body, name=name, grid=(S // LN_ROWS,), in_specs=[row, row, vec, vec],
        out_specs=[row, lvec, vec, vec],
        out_shape=[jax.ShapeDtypeStruct((S, D), F32), jax.ShapeDtypeStruct((1, LANE), F32),
                   jax.ShapeDtypeStruct((1, D), F32), jax.ShapeDtypeStruct((1, D), F32)],
        compiler_params=_cparams(("arbitrary",)),
    )(r, target, g, b)


CONV_COLS = 256
CONV_ROWS = 256
HALO = 8
INV_SQRT2 = 1.0 / math.sqrt(2.0)
INV_SQRT2PI = 1.0 / math.sqrt(2.0 * math.pi)


def _gelu(x):
    return 0.5 * x * (1.0 + lax.erf(x * INV_SQRT2))


def _gelu_and_grad(x):
    cdf = 0.5 * (1.0 + lax.erf(x * INV_SQRT2))
    return x * cdf, cdf + x * jnp.exp(-0.5 * x * x) * INV_SQRT2PI


def _conv_rows(ext, w_ref, b_ref, n):
    total = ext.shape[0]
    s1 = pltpu.roll(ext, 1, 0)
    s2 = pltpu.roll(ext, 2, 0)
    u = w_ref[2:3, :] * ext + w_ref[1:2, :] * s1 + w_ref[0:1, :] * s2 + b_ref[...]
    return u[HALO:total], s1[HALO:total], s2[HALO:total]


def _conv_gelu_fwd(u0, conv_w, conv_b, name):
    S, C2 = u0.shape
    F = C2 // 2
    ncb = F // CONV_COLS
    nrc = S // CONV_ROWS

    def body(ua_ref, uc_ref, wa_ref, wc_ref, ba_ref, bc_ref, p_ref):
        def chunk(ci, _):
            r0 = pl.multiple_of(ci * CONV_ROWS, CONV_ROWS)
            p0 = pl.multiple_of(jnp.maximum(r0 - HALO, 0), HALO)
            keep = (ci > 0).astype(F32)

            def load(ref):
                prev = ref[pl.ds(p0, HALO), :] * keep
                return jnp.concatenate([prev, ref[pl.ds(r0, CONV_ROWS), :]], axis=0)

            a, _, _ = _conv_rows(load(ua_ref), wa_ref, ba_ref, CONV_ROWS)
            c, _, _ = _conv_rows(load(uc_ref), wc_ref, bc_ref, CONV_ROWS)
            p_ref[pl.ds(r0, CONV_ROWS), :] = (_gelu(a) * c).astype(p_ref.dtype)
            return 0

        lax.fori_loop(0, nrc, chunk, 0)

    col_a = pl.BlockSpec((S, CONV_COLS), lambda j: (0, j))
    col_c = pl.BlockSpec((S, CONV_COLS), lambda j: (0, j + ncb))
    w_a = pl.BlockSpec((CONV_WIDTH, CONV_COLS), lambda j: (0, j))
    w_c = pl.BlockSpec((CONV_WIDTH, CONV_COLS), lambda j: (0, j + ncb))
    b_a = pl.BlockSpec((1, CONV_COLS), lambda j: (0, j))
    b_c = pl.BlockSpec((1, CONV_COLS), lambda j: (0, j + ncb))
    return pl.pallas_call(
        body, name=name, grid=(ncb,), in_specs=[col_a, col_c, w_a, w_c, b_a, b_c], out_specs=col_a,
        out_shape=jax.ShapeDtypeStruct((S, F), MXU_DTYPE),
        compiler_params=_cparams(("parallel",)),
    )(u0, u0, conv_w, conv_w, conv_b, conv_b)


def _conv_gelu_bwd(u0, dp, conv_w, conv_b, name):
    S, C2 = u0.shape
    F = C2 // 2
    ncb = F // CONV_COLS
    nrc = S // CONV_ROWS
    EXT = CONV_ROWS + HALO

    def body(ua_ref, uc_ref, dp_ref, wa_ref, wc_ref, ba_ref, bc_ref,
             da_ref, dc_ref, gwa_ref, gwc_ref, gba_ref, gbc_ref):
        gwa_ref[...] = jnp.zeros_like(gwa_ref)
        gwc_ref[...] = jnp.zeros_like(gwc_ref)
        gba_ref[...] = jnp.zeros_like(gba_ref)
        gbc_ref[...] = jnp.zeros_like(gbc_ref)
        rid = lax.broadcasted_iota(jnp.int32, (EXT, CONV_COLS), 0)

        def chunk(ci, _):
            r0 = pl.multiple_of(ci * CONV_ROWS, CONV_ROWS)
            p0 = pl.multiple_of(jnp.maximum(r0 - HALO, 0), HALO)
            n0 = pl.multiple_of(jnp.minimum(r0 + CONV_ROWS, S - HALO), HALO)
            keep_prev = (ci > 0).astype(F32)
            keep_next = (ci < nrc - 1).astype(F32)

            def load(ref):
                return jnp.concatenate([ref[pl.ds(p0, HALO), :] * keep_prev,
                                        ref[pl.ds(r0, CONV_ROWS), :],
                                        ref[pl.ds(n0, HALO), :] * keep_next], axis=0)

            ext_a = load(ua_ref)
            ext_c = load(uc_ref)
            a, a1, a2 = _conv_rows(ext_a, wa_ref, ba_ref, EXT)
            c, c1, c2 = _conv_rows(ext_c, wc_ref, bc_ref, EXT)
            a0 = ext_a[HALO:HALO + EXT]
            c0 = ext_c[HALO:HALO + EXT]
            dpe = jnp.concatenate([dp_ref[pl.ds(r0, CONV_ROWS), :].astype(F32),
                                   dp_ref[pl.ds(n0, HALO), :].astype(F32) * keep_next], axis=0)
            gelu_a, slope_a = _gelu_and_grad(a)
            d_a = dpe * c * slope_a
            d_c = dpe * gelu_a
            own = rid < CONV_ROWS

            def back(d_u, w_ref, x0, x1, x2, d_ref, gw_ref, gb_ref):
                d_u0 = (w_ref[2:3, :] * d_u + w_ref[1:2, :] * pltpu.roll(d_u, EXT - 1, 0)
                        + w_ref[0:1, :] * pltpu.roll(d_u, EXT - 2, 0))
                d_ref[pl.ds(r0, CONV_ROWS), :] = d_u0[0:CONV_ROWS].astype(d_ref.dtype)
                d_own = jnp.where(own, d_u, 0.0)
                gw_ref[...] += jnp.concatenate(
                    [jnp.sum(d_own * x2, axis=0, keepdims=True),
                     jnp.sum(d_own * x1, axis=0, keepdims=True),
                     jnp.sum(d_own * x0, axis=0, keepdims=True)], axis=0)
                gb_ref[...] += jnp.sum(d_own, axis=0, keepdims=True)

            back(d_a, wa_ref, a0, a1, a2, da_ref, gwa_ref, gba_ref)
            back(d_c, wc_ref, c0, c1, c2, dc_ref, gwc_ref, gbc_ref)
            return 0

        lax.fori_loop(0, nrc, chunk, 0)

    col_a = pl.BlockSpec((S, CONV_COLS), lambda j: (0, j))
    col_c = pl.BlockSpec((S, CONV_COLS), lambda j: (0, j + ncb))
    w_a = pl.BlockSpec((CONV_WIDTH, CONV_COLS), lambda j: (0, j))
    w_c = pl.BlockSpec((CONV_WIDTH, CONV_COLS), lambda j: (0, j + ncb))
    b_a = pl.BlockSpec((1, CONV_COLS), lambda j: (0, j))
    b_c = pl.BlockSpec((1, CONV_COLS), lambda j: (0, j + ncb))
    outs = pl.pallas_call(
        body, name=name, grid=(ncb,),
        in_specs=[col_a, col_c, col_a, w_a, w_c, b_a, b_c],
        out_specs=[col_a, col_a, w_a, w_a, b_a, b_a],
        out_shape=[jax.ShapeDtypeStruct((S, F), MXU_DTYPE), jax.ShapeDtypeStruct((S, F), MXU_DTYPE),
                   jax.ShapeDtypeStruct((CONV_WIDTH, F), F32), jax.ShapeDtypeStruct((CONV_WIDTH, F), F32),
                   jax.ShapeDtypeStruct((1, F), F32), jax.ShapeDtypeStruct((1, F), F32)],
        compiler_params=_cparams(("parallel",)),
    )(u0, u0, dp, conv_w, conv_w, conv_b, conv_b)
    da, dc, gwa, gwc, gba, gbc = outs
    return (jnp.concatenate([da, dc], axis=1), jnp.concatenate([gwa, gwc], axis=1),
            jnp.concatenate([gba, gbc], axis=1))


SB_TK = 128
SB_TQ_FWD = 1024
SB_TQ_BWD = 1024
SB_PAIRS_FWD = 2


def _softplus(z):
    return jnp.maximum(z, 0.0) + jnp.log(1.0 + jnp.exp(-jnp.abs(z)))


def _tri_ones(after):
    r = lax.broadcasted_iota(jnp.int32, (SB_TK, 2 * SB_TK), 0)
    c = lax.broadcasted_iota(jnp.int32, (SB_TK, 2 * SB_TK), 1)
    tri = (r > c) if after else (r < c)
    return jnp.where(c >= SB_TK, 1.0, jnp.where(tri, 1.0, 0.0)).astype(BF16)


def _sb_block_specs(S, TQ):
    NP = SB_WIDTH // LANE
    return [pl.BlockSpec((TQ, LANE), lambda p, i: (i, p)),
            pl.BlockSpec((S, LANE), lambda p, i: (0, NP + p)),
            pl.BlockSpec((S, LANE), lambda p, i: (0, 2 * NP + p))]


def _sb_fwd(qkv, name, ride=()):
    S = qkv.shape[0]
    TK = SB_TK
    TQ = min(SB_TQ_FWD, S)
    R = TQ // TK
    NP = SB_WIDTH // LANE
    PS = SB_PAIRS_FWD
    W = PS * LANE
    NH = 2 * PS
    nq = S // TQ
    nkb = S // TK
    grid = (NP // PS, nq)

    def body(q_ref, k_ref, v_ref, o_ref, wk_ref, wbuf, wsem, bbuf, bsem):
        p, i = pl.program_id(0), pl.program_id(1)
        row = lax.broadcasted_iota(jnp.int32, (TQ, TK), 0)
        col = lax.broadcasted_iota(jnp.int32, (TQ, TK), 1)
        after_ones = _tri_ones(True)
        first = lax.broadcasted_iota(jnp.int32, (TK, LANE), 1) < SB_HEAD_DIM

        def keep(slot, j):
            return pltpu.make_async_copy(wbuf.at[slot], wk_ref.at[pl.ds(p * NH, NH), i, j], wsem.at[slot])

        def keep_band(u):
            return pltpu.make_async_copy(bbuf.at[u], wk_ref.at[pl.ds(p * NH, NH), i, R * i + u], bsem.at[u])

        def block(j, carry, r0, stage):
            k0 = pl.multiple_of(j * TK, TK)
            masked = r0 is not None
            r0 = r0 or 0
            out = []
            for pr in range(PS):
                lanes = slice(pr * LANE, (pr + 1) * LANE)
                acc0, tail_a, tail_b = carry[3 * pr:3 * pr + 3]
                kab = k_ref[pl.ds(k0, TK), lanes]
                vab = v_ref[pl.ds(k0, TK), lanes]
                none = jnp.zeros_like(kab)
                k2 = jnp.concatenate([jnp.where(first, kab, none), jnp.where(first, none, kab)], axis=0)
                v2 = jnp.concatenate([jnp.where(first, vab, none), jnp.where(first, none, vab)], axis=0)
                z2 = _dot(q_ref[r0:, lanes], k2, "nt")
                tails, ws = [], []
                for hh, tail in enumerate((tail_a, tail_b)):
                    z = z2[:, hh * TK:(hh + 1) * TK]
                    sp = _softplus(z)
                    if masked:
                        strict = col[r0:] < row[r0:] - r0
                        sp = jnp.where(strict, sp, 0.0)
                    cs = _dot_split(sp, after_ones, "nn")
                    w = jnp.exp(z - sp - cs[:, :TK] - tail[r0:])
                    if masked:
                        w = jnp.where(strict, w, 0.0)
                    ws.append(w.astype(MXU_DTYPE))
                    stage[2 * pr + hh, r0:, :] = ws[-1]
                    tot = cs[:, TK:]
                    if r0:
                        tot = jnp.concatenate([jnp.zeros((r0, TK), F32), tot], axis=0)
                    tails.append(tail + tot)
                acc_r = acc0[r0:] + _dot(jnp.concatenate(ws, axis=1), v2, "nn")
                acc = jnp.concatenate([acc0[:r0], acc_r], axis=0) if r0 else acc_r
                out += [acc, tails[0], tails[1]]
            return tuple(out)

        carry = (jnp.zeros((TQ, LANE), F32), jnp.zeros((TQ, TK), F32), jnp.zeros((TQ, TK), F32)) * PS
        for u in reversed(range(R)):
            carry = block(R * i + u, carry, u * TK, bbuf.at[u])
        for u in range(R):
            keep_band(u).start()
        trips = R * i

        def below(n, c):
            slot = n % 2
            j = trips - 1 - n

            @pl.when(n >= 2)
            def _():
                keep(slot, j).wait()

            c = block(j, c, None, wbuf.at[slot])
            keep(slot, j).start()
            return c

        carry = lax.fori_loop(0, trips, below, carry)
        for back in (1, 2):

            @pl.when(trips >= back)
            def _():
                keep((trips - back) % 2, 0).wait()

        for u in range(R):
            keep_band(u).wait()
        for pr in range(PS):
            o_ref[:, pr * LANE:(pr + 1) * LANE] = carry[3 * pr].astype(o_ref.dtype)

    in_specs = [pl.BlockSpec((TQ, W), lambda p, i: (i, p)),
                pl.BlockSpec((S, W), lambda p, i: (0, NP // PS + p)),
                pl.BlockSpec((S, W), lambda p, i: (0, 2 * (NP // PS) + p))]
    ospec = pl.BlockSpec((TQ, W), lambda p, i: (i, p))
    x_args, x_in, x_out, x_shapes, x_sems, wrap = _riding_exchange(list(ride), grid)
    return pl.pallas_call(
        wrap(body, 3, 2) if ride else body, name=name, grid=grid,
        in_specs=in_specs + (x_in if ride else []),
        out_specs=[ospec, pl.BlockSpec(memory_space=pl.ANY)] + (x_out if ride else []),
        out_shape=[jax.ShapeDtypeStruct((S, SB_WIDTH), MXU_DTYPE),
                   jax.ShapeDtypeStruct((SB_HEADS, nq, nkb, TQ, TK), MXU_DTYPE)] + (x_shapes if ride else []),
        scratch_shapes=[pltpu.VMEM((2, NH, TQ, TK), MXU_DTYPE), pltpu.SemaphoreType.DMA((2,)),
                        pltpu.VMEM((R, NH, TQ, TK), MXU_DTYPE), pltpu.SemaphoreType.DMA((R,))]
        + (x_sems if ride else []),
        compiler_params=_cparams(("arbitrary", "arbitrary"), has_side_effects=True),
    )(qkv, qkv, qkv, *(x_args if ride else []))


def _sb_bwd(qkv, d_cat, w_kept, name, ride=()):
    S = qkv.shape[0]
    TK = SB_TK
    TQ = min(SB_TQ_BWD, S)
    assert TQ == min(SB_TQ_FWD, S)
    R = TQ // TK
    NP = SB_WIDTH // LANE
    grid = (NP, S // TQ)
    scale = SB_HEAD_DIM ** -0.5

    def body(q_ref, k_ref, v_ref, do_ref, wk_ref, dq_ref, dk_ref, dv_ref, dk_acc, dv_acc, wbuf, wsem, bbuf, bsem):
        p, i = pl.program_id(0), pl.program_id(1)

        @pl.when(i == 0)
        def _():
            dk_acc[...] = jnp.zeros_like(dk_acc)
            dv_acc[...] = jnp.zeros_like(dv_acc)

        row = lax.broadcasted_iota(jnp.int32, (TQ, TK), 0)
        col = lax.broadcasted_iota(jnp.int32, (TQ, TK), 1)
        before_ones = _tri_ones(False)
        first = lax.broadcasted_iota(jnp.int32, (TK, LANE), 1) < SB_HEAD_DIM
        qab = q_ref[...]
        doab = do_ref[...].astype(MXU_DTYPE)
        qdo = jnp.concatenate([qab, doab], axis=1)

        def fetch(slot, j):
            return pltpu.make_async_copy(wk_ref.at[pl.ds(2 * p, 2), i, j], wbuf.at[slot], wsem.at[slot])

        def fetch_band(u):
            return pltpu.make_async_copy(wk_ref.at[pl.ds(2 * p, 2), i, R * i + u], bbuf.at[u], bsem.at[u])

        def block(j, carry, r0, stage):
            k0 = pl.multiple_of(j * TK, TK)
            masked = r0 is not None
            r0 = r0 or 0
            kab = k_ref[pl.ds(k0, TK), :]
            vab = v_ref[pl.ds(k0, TK), :]
            none = jnp.zeros_like(kab)
            k_h = [jnp.where(first, kab, none), jnp.where(first, none, kab)]
            v_h = [jnp.where(first, vab, none), jnp.where(first, none, vab)]
            sums, dzs, ws = [], [], []
            for hh in range(2):
                gsum = carry[1 + hh]
                kv = jnp.concatenate([jnp.concatenate([k_h[hh], none], axis=1),
                                      jnp.concatenate([none, v_h[hh]], axis=1)], axis=0)
                zdw = _dot(qdo[r0:], kv, "nt")
                z = zdw[:, :TK]
                sig = jnp.exp(z - _softplus(z))
                wb = stage[hh, r0:, :]
                g = wb.astype(F32) * zdw[:, TK:]
                cg = _dot_split(g, before_ones, "nn")
                dz = g - sig * (g + cg[:, :TK] + gsum[r0:])
                if masked:
                    dz = jnp.where(col[r0:] < row[r0:] - r0, dz, 0.0)
                dzs.append(dz.astype(MXU_DTYPE))
                ws.append(wb)
                gsum_r = gsum[r0:] + cg[:, TK:]
                if r0:
                    gsum_r = jnp.concatenate([gsum[:r0], gsum_r], axis=0)
                sums.append(gsum_r)
            kvg = _dot(jnp.concatenate(dzs + ws, axis=1), qdo[r0:], "tn")
            dk_acc[pl.ds(k0, TK), :] += jnp.where(first, kvg[:TK, :LANE], kvg[TK:2 * TK, :LANE])
            dv_acc[pl.ds(k0, TK), :] += jnp.where(first, kvg[2 * TK:3 * TK, LANE:], kvg[3 * TK:, LANE:])
            dq_r = carry[0][r0:] + _dot(jnp.concatenate(dzs, axis=1), jnp.concatenate(k_h, axis=0), "nn")
            dq = jnp.concatenate([carry[0][:r0], dq_r], axis=0) if r0 else dq_r
            return (dq, *sums)

        trips = R * i
        for u in range(R):
            fetch_band(u).start()

        @pl.when(trips >= 1)
        def _():
            fetch(0, 0).start()

        def below(j, c):
            slot = j % 2

            @pl.when(j + 1 < trips)
            def _():
                fetch(1 - slot, j + 1).start()

            fetch(slot, j).wait()
            return block(j, c, None, wbuf.at[slot])

        zero = jnp.zeros((TQ, TK), F32)
        carry = lax.fori_loop(0, trips, below, (jnp.zeros((TQ, LANE), F32), zero, zero))
        for u in range(R):
            fetch_band(u).wait()
        for u in range(R):
            carry = block(trips + u, carry, u * TK, bbuf.at[u])
        dq_ref[...] = (carry[0] * scale).astype(dq_ref.dtype)

        @pl.when(i == S // TQ - 1)
        def _():
            dk_ref[...] = dk_acc[...].astype(dk_ref.dtype)
            dv_ref[...] = dv_acc[...].astype(dv_ref.dtype)

    qspec = pl.BlockSpec((TQ, LANE), lambda p, i: (i, p))
    full = pl.BlockSpec((S, LANE), lambda p, i: (0, p))
    sds = jax.ShapeDtypeStruct((S, SB_WIDTH), MXU_DTYPE)
    scratch = [pltpu.VMEM((S, LANE), F32), pltpu.VMEM((S, LANE), F32),
               pltpu.VMEM((2, 2, TQ, TK), MXU_DTYPE), pltpu.SemaphoreType.DMA((2,)),
               pltpu.VMEM((R, 2, TQ, TK), MXU_DTYPE), pltpu.SemaphoreType.DMA((R,))]
    x_args, x_in, x_out, x_shapes, x_sems, wrap = _riding_exchange(list(ride), grid)
    return pl.pallas_call(
        wrap(body, 5, 3) if ride else body, name=name, grid=grid,
        in_specs=_sb_block_specs(S, TQ) + [qspec, pl.BlockSpec(memory_space=pl.ANY)] + (x_in if ride else []),
        out_specs=[qspec, full, full] + (x_out if ride else []),
        out_shape=[sds, sds, sds] + (x_shapes if ride else []),
        scratch_shapes=scratch + (x_sems if ride else []),
        compiler_params=_cparams(("arbitrary", "arbitrary"), has_side_effects=bool(ride)),
    )(qkv, qkv, qkv, d_cat, w_kept, *(x_args if ride else []))


GLA_ROWS = 1024
GLA_GROUP_FWD = 16
GLA_GROUP_BWD = 8
GLA_PAIR_K = 2 * GLA_KEY_DIM
GLA_PAIR_V = 2 * GLA_VAL_DIM


def _log_sigmoid(x):
    return -_softplus(-x)


def _dot_split_lhs01(m01, x):
    hi = x.astype(BF16)
    lo = (x - hi.astype(F32)).astype(BF16)
    return lax.dot_general(jnp.concatenate([m01, m01], axis=1), jnp.concatenate([hi, lo], axis=0),
                           (((1,), (0,)), ((), ())), preferred_element_type=F32)


def _dot_split_tn(x, m01):
    hi = x.astype(BF16)
    lo = (x - hi.astype(F32)).astype(BF16)
    return lax.dot_general(jnp.concatenate([hi, lo], axis=0), jnp.concatenate([m01, m01], axis=0),
                           (((0,), (0,)), ((), ())), preferred_element_type=F32)


def _dot_split_nt01(m01, x):
    hi = x.astype(BF16)
    lo = (x - hi.astype(F32)).astype(BF16)
    return lax.dot_general(jnp.concatenate([m01, m01], axis=1), jnp.concatenate([hi, lo], axis=1),
                           (((1,), (1,)), ((), ())), preferred_element_type=F32)


def _gla_consts():
    C = CHUNK
    row = lax.broadcasted_iota(jnp.int32, (C, C), 0)
    col = lax.broadcasted_iota(jnp.int32, (C, C), 1)
    first = lax.broadcasted_iota(jnp.int32, (C, GLA_PAIR_K), 1) < GLA_KEY_DIM
    r = lax.broadcasted_iota(jnp.int32, (GLA_PAIR_K, GLA_PAIR_V), 0)
    c = lax.broadcasted_iota(jnp.int32, (GLA_PAIR_K, GLA_PAIR_V), 1)
    own = (r < GLA_KEY_DIM) == (c < GLA_VAL_DIM)
    rowk = lax.broadcasted_iota(jnp.int32, (C, GLA_PAIR_K), 0)
    return dict(row=row, col=col, first=first, own=own, rowk=rowk,
                incl=jnp.where(row >= col, 1.0, 0.0).astype(BF16),
                ones=jnp.ones((C, GLA_PAIR_V), BF16))


def _pack_state(state):
    top = lax.broadcasted_iota(jnp.int32, (GLA_PAIR_K, GLA_VAL_DIM), 0) < GLA_KEY_DIM
    return jnp.where(top, state[:, :GLA_VAL_DIM], state[:, GLA_VAL_DIM:])


def _unpack_state(packed):
    top = lax.broadcasted_iota(jnp.int32, (GLA_PAIR_K, GLA_VAL_DIM), 0) < GLA_KEY_DIM
    return jnp.concatenate([jnp.where(top, packed, 0.0), jnp.where(top, 0.0, packed)], axis=1)


def _gla_chunk(qc, kc, vc, gate, k):
    C = CHUNK
    row, col, first, rowk = k["row"], k["col"], k["first"], k["rowk"]
    la = _log_sigmoid(gate) * (1.0 / GLA_TAU)
    b = _dot_split_lhs01(k["incl"], la)
    b_ref = jnp.sum(jnp.where(rowk == C // 2 - 1, b, 0.0), axis=0, keepdims=True)
    b_last = jnp.sum(la, axis=0, keepdims=True)
    qs = qc * (GLA_KEY_DIM ** -0.5)
    q_in = qs * jnp.exp(b - b_ref)
    k_in = kc * jnp.exp(b_ref - b)
    k_dec = kc * jnp.exp(b_last - b)
    q_b = qs * jnp.exp(b)
    k_in_h = [jnp.where(first, k_in, 0.0), jnp.where(first, 0.0, k_in)]
    v_h = [vc[:, :GLA_VAL_DIM], vc[:, GLA_VAL_DIM:]]
    sc = [jnp.where(row >= col, _dot(q_in, k_in_h[hh], "nt"), 0.0) for hh in range(2)]
    o_intra = jnp.concatenate([_dot(sc[hh], v_h[hh], "nn") for hh in range(2)], axis=1)
    upd = jnp.where(k["own"], _dot(k_dec, vc, "tn"), 0.0)
    dec_col = jnp.exp(_dot_split_tn(la, k["ones"]))
    return dict(la=la, b=b, b_ref=b_ref, b_last=b_last, q_in=q_in, k_in=k_in, k_dec=k_dec, q_b=q_b,
                k_in_h=k_in_h, v_h=v_h, sc=sc, o_intra=o_intra, upd=upd, dec_col=dec_col)


def _rms_gate(o, gg):
    rinv = lax.rsqrt(jnp.mean(o * o, axis=-1, keepdims=True) + RMS_EPS)
    o_n = o * rinv
    sg = 1.0 / (1.0 + jnp.exp(-gg))
    return o_n, rinv, sg


def _gla_in_specs(rows_of, RB):
    PK, PV = GLA_PAIR_K, GLA_PAIR_V
    return [pl.BlockSpec((RB, PK), lambda i, p: (rows_of(i), p)),
            pl.BlockSpec((RB, PK), lambda i, p: (rows_of(i), (OFF_GK - OFF_GQ) // PK + p)),
            pl.BlockSpec((RB, PV), lambda i, p: (rows_of(i), (OFF_GV - OFF_GQ) // PV + p)),
            pl.BlockSpec((RB, PV), lambda i, p: (rows_of(i), (OFF_GG - OFF_GQ) // PV + p)),
            pl.BlockSpec((RB, LANE), lambda i, p: (rows_of(i), (OFF_GA - OFF_GQ) // LANE)),
            pl.BlockSpec((1, LANE, PK), lambda i, p: (p, 0, 0)),
            pl.BlockSpec((1, 1, PK), lambda i, p: (p, 0, 0)),
            pl.BlockSpec((1, GLA_VAL_DIM), lambda i, p: (0, 0))]


def _gla_fwd(proj, gate_up_p, gate_bias_p, gnorm, name):
    S = proj.shape[0]
    C, RB, PK, PV, dv = CHUNK, min(GLA_ROWS, S), GLA_PAIR_K, GLA_PAIR_V, GLA_VAL_DIM
    NP = GLA_HEADS // 2
    G = min(GLA_GROUP_FWD, RB // C)
    nchunk = S // C
    ngroup = RB // (C * G)

    def body(q_ref, k_ref, v_ref, gg_ref, ga_ref, gu_ref, gb_ref, gn_ref, o_ref, prev_ref, st_ref):
        i, p = pl.program_id(0), pl.program_id(1)
        k = _gla_consts()

        @pl.when(i == 0)
        def _():
            st_ref[p] = jnp.zeros((PK, PV), F32)

        def group(gi, state):
            for u in range(G):
                ci = gi * G + u
                rows = pl.ds(pl.multiple_of(ci * C, C), C)
                gate = _dot(ga_ref[rows, :], gu_ref[0], "nn") + gb_ref[0]
                f = _gla_chunk(q_ref[rows, :], k_ref[rows, :], v_ref[rows, :], gate, k)
                prev_ref[0, ci] = _pack_state(state)
                o = f["o_intra"] + _dot(f["q_b"], state, "nn")
                state = f["dec_col"] * state + f["upd"]
                ggv = gg_ref[rows, :]
                halves = []
                for hh in range(2):
                    lanes = slice(hh * dv, (hh + 1) * dv)
                    o_n, _, sg = _rms_gate(o[:, lanes], ggv[:, lanes])
                    halves.append(o_n * gn_ref[...] * (ggv[:, lanes] * sg))
                o_ref[rows, :] = jnp.concatenate(halves, axis=1).astype(o_ref.dtype)
            return state

        st_ref[p] = lax.fori_loop(0, ngroup, group, st_ref[p])

    return pl.pallas_call(
        body, name=name, grid=(S // RB, NP), in_specs=_gla_in_specs(lambda i: i, RB),
        out_specs=[pl.BlockSpec((RB, PV), lambda i, p: (i, p)),
                   pl.BlockSpec((1, RB // C, PK, dv), lambda i, p: (p, i, 0, 0))],
        out_shape=[jax.ShapeDtypeStruct((S, GLA_WIDTH), MXU_DTYPE),
                   jax.ShapeDtypeStruct((NP, nchunk, PK, dv), F32)],
        scratch_shapes=[pltpu.VMEM((NP, PK, PV), F32)],
        compiler_params=_cparams(("arbitrary", "arbitrary")),
    )(proj, proj, proj, proj, proj, gate_up_p, gate_bias_p, gnorm)


def _gla_bwd(proj, gate_up_p, gate_bias_p, gnorm, prev, d_cat, name):
    S = proj.shape[0]
    C, RB, PK, PV, dv = CHUNK, min(GLA_ROWS, S), GLA_PAIR_K, GLA_PAIR_V, GLA_VAL_DIM
    NP = GLA_HEADS // 2
    G = min(GLA_GROUP_BWD, RB // C)
    nb = S // RB
    ngroup = RB // (C * G)

    def body(q_ref, k_ref, v_ref, gg_ref, ga_ref, gu_ref, gb_ref, gn_ref, prev_ref, do_ref,
             dq_ref, dk_ref, dv_ref, dgg_ref, dga_ref, ggu_ref, ggb_ref, ggn_ref, st_ref):
        i, p = pl.program_id(0), pl.program_id(1)
        k = _gla_consts()
        row, col, first, rowk = k["row"], k["col"], k["first"], k["rowk"]
        gu = gu_ref[0]

        @pl.when(i == 0)
        def _():
            st_ref[p] = jnp.zeros((PK, PV), F32)
            ggu_ref[p] = jnp.zeros((LANE, PK), F32)
            ggb_ref[p] = jnp.zeros((1, PK), F32)

        @pl.when(jnp.logical_and(i == 0, p == 0))
        def _():
            ggn_ref[...] = jnp.zeros_like(ggn_ref)

        @pl.when(p == 0)
        def _():
            dga_ref[...] = jnp.zeros_like(dga_ref)

        upper_incl = jnp.where(col >= row, 1.0, 0.0).astype(BF16)
        ones_8 = jnp.ones((8, PV), BF16)

        def group(gn, dstate):
            gi = ngroup - 1 - gn
            for u in reversed(range(G)):
                ci = gi * G + u
                rows = pl.ds(pl.multiple_of(ci * C, C), C)
                ga = ga_ref[rows, :]
                gate = _dot(ga, gu, "nn") + gb_ref[0]
                vc = v_ref[rows, :]
                f = _gla_chunk(q_ref[rows, :], k_ref[rows, :], vc, gate, k)
                state = _unpack_state(prev_ref[0, ci])
                o = f["o_intra"] + _dot(f["q_b"], state, "nn")
                ggv = gg_ref[rows, :]
                dout = do_ref[rows, :]
                d_o_h, dgg_h = [], []
                for hh in range(2):
                    lanes = slice(hh * dv, (hh + 1) * dv)
                    o_n, rinv, sg = _rms_gate(o[:, lanes], ggv[:, lanes])
                    silu = ggv[:, lanes] * sg
                    dgg_h.append(dout[:, lanes] * o_n * gn_ref[...] * (sg * (1.0 + ggv[:, lanes] * (1.0 - sg))))
                    d_ong = dout[:, lanes] * silu
                    ggn_ref[...] += jnp.sum(d_ong * o_n, axis=0, keepdims=True)
                    d_on = d_ong * gn_ref[...]
                    d_o_h.append(rinv * (d_on - o_n * jnp.mean(d_on * o_n, axis=-1, keepdims=True)))
                dgg_ref[rows, :] = jnp.concatenate(dgg_h, axis=1).astype(dgg_ref.dtype)
                d_o = jnp.concatenate(d_o_h, axis=1)
                d_upd = jnp.where(k["own"], dstate, 0.0)
                d_dec_col = dstate * state * f["dec_col"]
                dstate = f["dec_col"] * dstate + _dot(f["q_b"], d_o, "tn")
                dsc = [jnp.where(row >= col, _dot(d_o_h[hh], f["v_h"][hh], "nt"), 0.0) for hh in range(2)]
                dv_ref[rows, :] = (jnp.concatenate([_dot(f["sc"][hh], d_o_h[hh], "tn") for hh in range(2)], axis=1)
                                   + _dot(f["k_dec"], d_upd, "nn")).astype(dv_ref.dtype)
                q_in_h = [jnp.where(first, f["q_in"], 0.0), jnp.where(first, 0.0, f["q_in"])]
                dq_in = _dot(dsc[0], f["k_in_h"][0], "nn") + _dot(dsc[1], f["k_in_h"][1], "nn")
                dk_in = _dot(dsc[0], q_in_h[0], "tn") + _dot(dsc[1], q_in_h[1], "tn")
                dq_b = _dot(d_o, state, "nt")
                dkdec = _dot(vc, d_upd, "nt")
                b = f["b"]
                e1 = jnp.exp(b - f["b_ref"])
                e2 = jnp.exp(f["b_ref"] - b)
                e3 = jnp.exp(f["b_last"] - b)
                eb = jnp.exp(b)
                dq_ref[rows, :] = ((dq_in * e1 + dq_b * eb) * (GLA_KEY_DIM ** -0.5)).astype(dq_ref.dtype)
                dk_ref[rows, :] = (dk_in * e2 + dkdec * e3).astype(dk_ref.dtype)
                t_q = dq_in * f["q_in"]
                t_k = dk_in * f["k_in"]
                t_d = dkdec * f["k_dec"]
                db = t_q - t_k - t_d + dq_b * f["q_b"]
                db_ref = jnp.sum(t_k - t_q, axis=0, keepdims=True)
                db_last = (jnp.sum(t_d, axis=0, keepdims=True)
                           + jnp.max(_dot_split_nt01(ones_8, d_dec_col), axis=0, keepdims=True))
                db = db + jnp.where(rowk == C // 2 - 1, db_ref, 0.0) + jnp.where(rowk == C - 1, db_last, 0.0)
                dla = _dot_split_lhs01(upper_incl, db)
                d_gate = dla * (1.0 / GLA_TAU) * (1.0 / (1.0 + jnp.exp(gate)))
                ggb_ref[p] += jnp.sum(d_gate, axis=0, keepdims=True)
                ggu_ref[p] += _dot(ga, d_gate, "tn")
                dga_ref[rows, :] += _dot(d_gate, gu, "nt")
            return dstate

        st_ref[p] = lax.fori_loop(0, ngroup, group, st_ref[p])

    back = lambda i: nb - 1 - i
    NPV = SB_WIDTH // PV
    return pl.pallas_call(
        body, name=name, grid=(nb, NP),
        in_specs=_gla_in_specs(back, RB) + [pl.BlockSpec((1, RB // C, PK, dv), lambda i, p: (p, back(i), 0, 0)),
                                            pl.BlockSpec((RB, PV), lambda i, p: (back(i), NPV + p))],
        out_specs=[pl.BlockSpec((RB, PK), lambda i, p: (back(i), p)),
                   pl.BlockSpec((RB, PK), lambda i, p: (back(i), p)),
                   pl.BlockSpec((RB, PV), lambda i, p: (back(i), p)),
                   pl.BlockSpec((RB, PV), lambda i, p: (back(i), p)),
                   pl.BlockSpec((RB, LANE), lambda i, p: (back(i), 0)),
                   pl.BlockSpec((NP, LANE, PK), lambda i, p: (0, 0, 0)),
                   pl.BlockSpec((NP, 1, PK), lambda i, p: (0, 0, 0)),
                   pl.BlockSpec((1, dv), lambda i, p: (0, 0))],
        out_shape=[jax.ShapeDtypeStruct((S, NP * PK), MXU_DTYPE), jax.ShapeDtypeStruct((S, NP * PK), MXU_DTYPE),
                   jax.ShapeDtypeStruct((S, GLA_WIDTH), MXU_DTYPE), jax.ShapeDtypeStruct((S, GLA_WIDTH), MXU_DTYPE),
                   jax.ShapeDtypeStruct((S, LANE), F32), jax.ShapeDtypeStruct((NP, LANE, PK), F32),
                   jax.ShapeDtypeStruct((NP, 1, PK), F32), jax.ShapeDtypeStruct((1, dv), F32)],
        scratch_shapes=[pltpu.VMEM((NP, PK, PV), F32)],
        compiler_params=_cparams(("arbitrary", "arbitrary")),
    )(proj, proj, proj, proj, proj, gate_up_p, gate_bias_p, gnorm, prev, d_cat)


def _exchange_copies(scatter_flags, ins, outs, send_sems, recv_sems, local_sems):
    n_peer = N_DEV - 1
    x, y, c = lax.axis_index("x"), lax.axis_index("y"), lax.axis_index("c")
    me = 4 * x + 2 * y + c
    copies = []
    for a, scatter in enumerate(scatter_flags):
        own = ins[a].at[me] if scatter else ins[a]
        copies.append(pltpu.make_async_copy(own, outs[a].at[me], local_sems.at[a]))
    for r in range(1, N_DEV):
        px = 1 - x if r & 4 else x
        py = 1 - y if r & 2 else y
        pc = 1 - c if r & 1 else c
        for a, scatter in enumerate(scatter_flags):
            src = ins[a].at[4 * px + 2 * py + pc] if scatter else ins[a]
            copies.append(pltpu.make_async_remote_copy(
                src_ref=src, dst_ref=outs[a].at[me],
                send_sem=send_sems.at[a * n_peer + r - 1], recv_sem=recv_sems.at[a * n_peer + r - 1],
                device_id=(px, py, pc), device_id_type=MESH_ID))
    return copies


def _exchange_shapes(items):
    out_shape = []
    for arr, scatter in items:
        shp = arr.shape if scatter else (N_DEV,) + arr.shape
        out_shape.append(jax.ShapeDtypeStruct(shp, arr.dtype))
    n = len(items)
    sems = [pltpu.SemaphoreType.DMA((n * (N_DEV - 1),)), pltpu.SemaphoreType.DMA((n * (N_DEV - 1),)),
            pltpu.SemaphoreType.DMA((n,))]
    return out_shape, sems


def _exchange(items, name):
    n = len(items)
    flags = [sc for _, sc in items]

    def body(*refs):
        copies = _exchange_copies(flags, refs[:n], refs[n:2 * n], *refs[2 * n:])
        for cp in copies:
            cp.start()
        for cp in copies:
            cp.wait()

    out_shape, sems = _exchange_shapes(items)
    any_spec = pl.BlockSpec(memory_space=pl.ANY)
    return pl.pallas_call(
        body, name=name, in_specs=[any_spec] * n, out_specs=[any_spec] * n, out_shape=out_shape,
        scratch_shapes=sems, compiler_params=pltpu.CompilerParams(has_side_effects=True),
    )(*[arr for arr, _ in items])


def _all_gather_two_level(arrays, name):
    n = len(arrays)
    K = N_DEV - 1

    def body(*refs):
        ins, outs = refs[:n], refs[n:2 * n]
        send_sems, recv_sems, local_sems = refs[2 * n:]
        x, y, c = lax.axis_index("x"), lax.axis_index("y"), lax.axis_index("c")
        sibling = (x, y, 1 - c)
        chips = [(1 - x, y), (x, 1 - y), (1 - x, 1 - y)]

        def slot(px, py, pc):
            return 4 * px + 2 * py + pc

        def copy(a, k, block, to, src=None):
            rows = outs[a].at[slot(*block)]
            return pltpu.make_async_remote_copy(
                src_ref=rows if src is None else src, dst_ref=rows,
                send_sem=send_sems.at[a * K + k], recv_sem=recv_sems.at[a * K + k],
                device_id=to, device_id_type=MESH_ID)

        me = (x, y, c)
        local, first, passed = [], [], []
        for a in range(n):
            cp = pltpu.make_async_copy(ins[a], outs[a].at[slot(*me)], local_sems.at[a])
            cp.start()
            local.append(cp)
            first.append(copy(a, 0, me, sibling, src=ins[a]))
            first += [copy(a, 1 + j, me, (*chip, c), src=ins[a]) for j, chip in enumerate(chips)]
        for cp in first:
            cp.start()
        for j, chip in enumerate(chips):
            for a in range(n):
                copy(a, 1 + j, (*chip, c), me).wait_recv()
                cp = copy(a, 4 + j, (*chip, c), sibling)
                cp.start()
                passed.append(cp)
        for a in range(n):
            copy(a, 0, sibling, me).wait_recv()
            for j, chip in enumerate(chips):
                copy(a, 4 + j, (*chip, 1 - c), me).wait_recv()
        for cp in first + passed:
            cp.wait_send()
        for cp in local:
            cp.wait()

    out_shape, sems = _exchange_shapes([(arr, False) for arr in arrays])
    any_spec = pl.BlockSpec(memory_space=pl.ANY)
    return pl.pallas_call(
        body, name=name, in_specs=[any_spec] * n, out_specs=[any_spec] * n, out_shape=out_shape,
        scratch_shapes=sems, compiler_params=pltpu.CompilerParams(has_side_effects=True),
    )(*arrays)


def _riding_exchange(items, grid):
    n = len(items)
    flags = [sc for _, sc in items]
    out_shape, sems = _exchange_shapes(items)
    any_spec = pl.BlockSpec(memory_space=pl.ANY)

    def wrap(body, n_in, n_out):
        def fused(*refs):
            ins = refs[:n_in]
            x_ins = refs[n_in:n_in + n]
            outs = refs[n_in + n:n_in + n + n_out]
            x_outs = refs[n_in + n + n_out:n_in + 2 * n + n_out]
            rest = refs[n_in + 2 * n + n_out:]
            x_sems, scratch = rest[len(rest) - 3:], rest[:len(rest) - 3]
            first = last = True
            for d, n_d in enumerate(grid):
                first = jnp.logical_and(first, pl.program_id(d) == 0)
                last = jnp.logical_and(last, pl.program_id(d) == n_d - 1)

            @pl.when(first)
            def _():
                for cp in _exchange_copies(flags, x_ins, x_outs, *x_sems):
                    cp.start()

            body(*ins, *outs, *scratch)

            @pl.when(last)
            def _():
                for cp in _exchange_copies(flags, x_ins, x_outs, *x_sems):
                    cp.wait()

        return fused

    return [arr for arr, _ in items], [any_spec] * n, [any_spec] * n, out_shape, sems, wrap


def _sum_devices(ref):
    g = ref[0].astype(F32)
    for q in range(1, N_DEV):
        g = g + ref[q].astype(F32)
    return g


def _adam_math(g, w, m, v):
    nm = ADAM_B1 * m + (1.0 - ADAM_B1) * g
    nv = ADAM_B2 * v + (1.0 - ADAM_B2) * (g * g)
    m_hat = nm / (1.0 - ADAM_B1 ** ADAM_STEP)
    v_hat = nv / (1.0 - ADAM_B2 ** ADAM_STEP)
    return -ADAM_LR * (m_hat / (jnp.sqrt(v_hat) + ADAM_EPS) + ADAM_WD * w), nm, nv


def _adamw(grecv, w, m, v, name):
    R, C = w.shape
    tile = _pick(R, (256, 176, 128)) if R * C > 65536 else R

    def body(gr_ref, w_ref, m_ref, v_ref, g_ref, d_ref, nm_ref, nv_ref):
        g = _sum_devices(gr_ref)
        g_ref[...] = g
        d_ref[...], nm_ref[...], nv_ref[...] = _adam_math(g, w_ref[...], m_ref[...], v_ref[...])

    blk = pl.BlockSpec((tile, C), lambda i: (i, 0))
    sds = jax.ShapeDtypeStruct((R, C), F32)
    return pl.pallas_call(
        body, name=name, grid=(R // tile,),
        in_specs=[pl.BlockSpec((N_DEV, tile, C), lambda i: (0, i, 0)), blk, blk, blk],
        out_specs=[blk, blk, blk, blk], out_shape=[sds, sds, sds, sds],
        compiler_params=_cparams(("parallel",)),
    )(grecv, w, m, v)


def _adamw_replicated(grecvs, loss_recv, ws, ms, vs, name):
    nt = len(ws)

    def body(*refs):
        gr, lr = refs[:nt], refs[nt]
        w, m, v = refs[nt + 1:2 * nt + 1], refs[2 * nt + 1:3 * nt + 1], refs[3 * nt + 1:4 * nt + 1]
        outs = refs[4 * nt + 1:]
        outs[0][...] = _sum_devices(lr)
        for t in range(nt):
            g_ref, d_ref, nm_ref, nv_ref = outs[1 + 4 * t:5 + 4 * t]
            g = _sum_devices(gr[t])
            g_ref[...] = g
            d_ref[...], nm_ref[...], nv_ref[...] = _adam_math(g, w[t][...], m[t][...], v[t][...])

    out_shape = [jax.ShapeDtypeStruct((1, LANE), F32)]
    for t in range(nt):
        out_shape += [jax.ShapeDtypeStruct(ws[t].shape, F32)] * 4
    outs = pl.pallas_call(body, name=name, out_shape=out_shape, compiler_params=_cparams())(
        *grecvs, loss_recv, *ws, *ms, *vs)
    return outs[0], [outs[1 + 4 * t:5 + 4 * t] for t in range(nt)]


def _shard_cols(g):
    rows, cols = g.shape
    return g.reshape(rows, N_DEV, cols // N_DEV).transpose(1, 0, 2)


def _unshard_cols(blocks):
    return blocks.transpose(1, 0, 2).reshape(blocks.shape[1], -1)


def _heads(t, n, d):
    return t.reshape(t.shape[0], n, d).transpose(1, 0, 2)


def _unheads(t):
    return t.transpose(1, 0, 2).reshape(t.shape[1], -1)


def kernel(x, w_in, gate_up, gate_bias, gla_norm_g, w_out, ln1_g, ln1_b, w_up, conv_w, conv_b, w_down, ln2_g, ln2_b, loss_target, m_w_in, m_gate_up, m_gate_bias, m_gla_norm_g, m_w_out, m_ln1_g, m_ln1_b, m_w_up, m_conv_w, m_conv_b, m_w_down, m_ln2_g, m_ln2_b, v_w_in, v_gate_up, v_gate_bias, v_gla_norm_g, v_w_out, v_ln1_g, v_ln1_b, v_w_up, v_conv_w, v_conv_b, v_w_down, v_ln2_g, v_ln2_b):
    S, D = x.shape[1], x.shape[2]
    x2, tgt = x[0], loss_target[0]

    gathered = _all_gather_two_level([w_in[0].astype(MXU_DTYPE), gate_up[0], conv_w[0]], "gather_w_in")
    w_in_f = _unshard_cols(gathered[0])
    gate_up_f = _unshard_cols(gathered[1])
    conv_w_f = _unshard_cols(gathered[2])
    w_in_pad = jnp.pad(w_in_f, ((0, 0), (0, IN_PAD - IN_WIDTH)))
    gate_up_p = _heads(jnp.pad(gate_up_f, ((0, LANE - GLA_GATE_RANK), (0, 0))), GLA_HEADS // 2, GLA_PAIR_K)
    gate_bias_p = gate_bias.reshape(GLA_HEADS // 2, 1, GLA_PAIR_K)

    w_qkv = jnp.concatenate([w_in_f[:, :OFF_SBK] * (SB_HEAD_DIM ** -0.5), w_in_f[:, OFF_SBK:OFF_GQ]], axis=1)
    qkv = _matmul(x2, w_qkv, "nn", MXU_DTYPE, "proj_sb")
    proj = _matmul(x2, w_in_pad[:, OFF_GQ:], "nn", F32, "proj_gla")
    sb_o, sb_w, g_out, g_up, g_down = _sb_fwd(
        qkv, "sb_fwd", ride=[(w_out[0].astype(MXU_DTYPE), False), (w_up[0].astype(MXU_DTYPE), False),
                             (w_down[0].astype(MXU_DTYPE), False)])
    w_out_f = g_out.reshape(-1, D)
    w_up_f = _unshard_cols(g_up)
    w_down_f = g_down.reshape(-1, D)
    gla_o, prev = _gla_fwd(proj, gate_up_p, gate_bias_p, gla_norm_g, "gla_fwd")
    cat = jnp.concatenate([sb_o, gla_o], axis=1)
    r1 = _matmul(cat, w_out_f, "nn", F32, "mix", res=x2, res_scale=DN_ALPHA)
    h = _ln_fwd(r1, ln1_g, ln1_b, "ln1")
    u0 = _matmul(h, w_up_f, "nn", F32, "ffn_up")
    p = _conv_gelu_fwd(u0, conv_w_f, conv_b, "conv_gelu")
    r2 = _matmul(p, w_down_f, "nn", F32, "ffn_down", res=h, res_scale=DN_ALPHA)
    d_r2, loss_p, g_ln2_g, g_ln2_b = _ln_loss_bwd(r2, tgt, ln2_g, ln2_b, "ln2_loss")

    d_p = _matmul(d_r2, w_down_f, "nt", MXU_DTYPE, "d_ffn_act")
    g_w_down = _matmul(p, d_r2, "tn", BF16, "grad_w_down")
    d_u0, g_conv_w, g_conv_b = _conv_gelu_bwd(u0, d_p, conv_w_f, conv_b, "conv_gelu_bwd")
    g_w_up = _matmul(h, d_u0, "tn", BF16, "grad_w_up")
    d_h = _matmul(d_u0, w_up_f, "nt", F32, "d_h", res=d_r2, res_scale=DN_ALPHA)
    d_r1, g_ln1_g, g_ln1_b = _ln_bwd(r1, d_h, ln1_g, "ln1_bwd")
    g_w_out = _matmul(cat, d_r1, "tn", BF16, "grad_w_out")
    d_cat = _matmul(d_r1, w_out_f, "nt", F32, "d_cat")
    (d_gq, d_gk, d_gv, d_gg, d_ga_pad, g_gu_p, g_gb_p, g_gnorm) = _gla_bwd(
        proj, gate_up_p, gate_bias_p, gla_norm_g, prev, d_cat, "gla_bwd")
    g_gate_up = _unheads(g_gu_p[:, :GLA_GATE_RANK, :])
    g_gate_bias = g_gb_p.reshape(1, -1)
    small_g = [g_gate_bias, g_gnorm, g_ln1_g, g_ln1_b, g_conv_b, g_ln2_g, g_ln2_b]
    d_sq, d_sk, d_sv, *recv_rest = _sb_bwd(
        qkv, d_cat, sb_w, "sb_bwd",
        ride=[(g_w_out.reshape(N_DEV, -1, D), True), (_shard_cols(g_w_up), True),
              (g_w_down.reshape(N_DEV, -1, D), True), (_shard_cols(g_gate_up), True),
              (_shard_cols(g_conv_w), True)] + [(t, False) for t in small_g] + [(loss_p, False)])
    d_proj = jnp.concatenate([d_sq, d_sk, d_sv, d_gq, d_gk, d_gv, d_gg,
                              d_ga_pad.astype(MXU_DTYPE)], axis=1)
    g_w_in = _matmul(x2, d_proj, "tn", BF16, "grad_w_in")[:, :IN_WIDTH]
    d_x, recv_in = _matmul(d_proj, w_in_pad, "nt", F32, "d_x", res=d_r1, res_scale=DN_ALPHA,
                           ride=[(_shard_cols(g_w_in), True)])

    recv = [recv_in] + recv_rest[:5]
    sharded = [(w_in, m_w_in, v_w_in), (w_out, m_w_out, v_w_out), (w_up, m_w_up, v_w_up),
               (w_down, m_w_down, v_w_down), (gate_up, m_gate_up, v_gate_up), (conv_w, m_conv_w, v_conv_w)]
    upd = [_adamw(recv[n], w[0], m[0], v[0], "adamw_%d" % n) for n, (w, m, v) in enumerate(sharded)]
    loss_row, small = _adamw_replicated(
        recv_rest[5:12], recv_rest[12], [gate_bias, gla_norm_g, ln1_g, ln1_b, conv_b, ln2_g, ln2_b],
        [m_gate_bias, m_gla_norm_g, m_ln1_g, m_ln1_b, m_conv_b, m_ln2_g, m_ln2_b],
        [v_gate_bias, v_gla_norm_g, v_ln1_g, v_ln1_b, v_conv_b, v_ln2_g, v_ln2_b], "adamw_replicated")
    outs = []
    for kind in range(4):
        b_w_in, b_w_out, b_w_up, b_w_down, b_gate_up, b_conv_w = [u[kind][None] for u in upd]
        s_gb, s_gn, s_l1g, s_l1b, s_cb, s_l2g, s_l2b = [t[kind] for t in small]
        outs += [b_w_in, b_gate_up, s_gb, s_gn, b_w_out, s_l1g, s_l1b, b_w_up, b_conv_w, s_cb, b_w_down,
                 s_l2g, s_l2b]
    return (loss_row[0, 0], d_x[None], *outs)
```

```python
import math

import jax
import jax.numpy as jnp
from jax import lax
from jax.experimental import pallas as pl
from jax.experimental.pallas import tpu as pltpu

F32 = jnp.float32
BF16 = jnp.bfloat16
MXU_DTYPE = jnp.bfloat16

N_DEV = 8
D_MODEL = 1024
SB_WIDTH = 512
SB_HEADS = 8
SB_HEAD_DIM = 64
GLA_HEADS = 4
GLA_KEY_DIM = 64
GLA_VAL_DIM = 128
GLA_WIDTH = 512
GLA_GATE_RANK = 16
GLA_TAU = 16.0
CHUNK = 64
D_FF = 2816
CONV_WIDTH = 3
LN_EPS = 1e-5
RMS_EPS = 1e-6
DN_ALPHA = 2.0 ** 0.25
IN_WIDTH = 3088
LANE = 128
IN_PAD = 3200
OFF_SBQ, OFF_SBK, OFF_SBV = 0, 512, 1024
OFF_GQ, OFF_GK, OFF_GV, OFF_GG, OFF_GA = 1536, 1792, 2048, 2560, 3072
GLA_PAD = IN_PAD - OFF_GQ

ADAM_LR = 0.001
ADAM_B1 = 0.9
ADAM_B2 = 0.999
ADAM_EPS = 1e-08
ADAM_WD = 0.01
ADAM_STEP = 10

VMEM_LIMIT = 48 * 1024 * 1024
MESH_ID = pl.DeviceIdType.MESH


def _cparams(sem=None, **kw):
    return pltpu.CompilerParams(dimension_semantics=sem, vmem_limit_bytes=VMEM_LIMIT, **kw)


def _dot(a, b, dims):
    ca, cb = {"nn": (1, 0), "nt": (1, 1), "tn": (0, 0)}[dims]
    return lax.dot_general(a.astype(MXU_DTYPE), b.astype(MXU_DTYPE), (((ca,), (cb,)), ((), ())),
                           preferred_element_type=F32)


def _dot_split(a, b, dims):
    assert dims == "nn"
    hi = a.astype(BF16)
    lo = (a - hi.astype(F32)).astype(BF16)
    return lax.dot_general(jnp.concatenate([hi, lo], axis=1), jnp.concatenate([b, b], axis=0),
                           (((1,), (0,)), ((), ())), preferred_element_type=F32)


def _pick(dim, prefs):
    for p in prefs:
        if dim % p == 0:
            return p
    return dim


def _matmul(a, b, dims, out_dtype, name, res=None, res_scale=1.0, ride=()):
    if dims == "nn":
        (M, K), (_, N) = a.shape, b.shape
    elif dims == "nt":
        (M, K), (N, _) = a.shape, b.shape
    else:
        (K, M), (_, N) = a.shape, b.shape
    tm = _pick(M, (1024, 1408, 512, 256, 128))
    tn = _pick(N, (1408, 1024, 640, 512))
    if tn == N and N > 2048:
        tn = _pick(N, (256, 128))
    tk = _pick(K, (1024, 1408, 640, 512, 256, 128))
    nk = K // tk
    grid = (M // tm, N // tn, nk)
    if dims == "tn":
        a_spec = pl.BlockSpec((tk, tm), lambda i, j, k: (k, i))
    else:
        a_spec = pl.BlockSpec((tm, tk), lambda i, j, k: (i, k))
    if dims == "nt":
        b_spec = pl.BlockSpec((tn, tk), lambda i, j, k: (j, k))
    else:
        b_spec = pl.BlockSpec((tk, tn), lambda i, j, k: (k, j))
    o_spec = pl.BlockSpec((tm, tn), lambda i, j, k: (i, j))
    in_specs = [a_spec, b_spec]
    args = [a, b]
    if res is not None:
        in_specs.append(o_spec)
        args.append(res)

    def body(*refs):
        if res is not None:
            a_ref, b_ref, r_ref, o_ref, acc_ref = refs
        else:
            a_ref, b_ref, o_ref, acc_ref = refs
            r_ref = None
        k = pl.program_id(2)
        part = _dot(a_ref[...], b_ref[...], dims)

        def finish(total):
            if r_ref is not None:
                total = total + res_scale * r_ref[...]
            o_ref[...] = total.astype(o_ref.dtype)

        if nk == 1:
            finish(part)
        else:
            @pl.when(k == 0)
            def _():
                acc_ref[...] = part

            @pl.when(jnp.logical_and(k > 0, k < nk - 1))
            def _():
                acc_ref[...] += part

            @pl.when(k == nk - 1)
            def _():
                finish(acc_ref[...] + part)

    out_sds = jax.ShapeDtypeStruct((M, N), out_dtype)
    acc = pltpu.VMEM((tm, tn), F32)
    if not ride:
        return pl.pallas_call(
            body, name=name, grid=grid, in_specs=in_specs, out_specs=o_spec, out_shape=out_sds,
            scratch_shapes=[acc], compiler_params=_cparams(("parallel", "parallel", "arbitrary")),
        )(*args)
    x_args, x_in, x_out, x_shapes, x_sems, wrap = _riding_exchange(list(ride), grid)
    return pl.pallas_call(
        wrap(body, len(args), 1), name=name, grid=grid, in_specs=in_specs + x_in, out_specs=[o_spec] + x_out,
        out_shape=[out_sds] + x_shapes, scratch_shapes=[acc] + x_sems,
        compiler_params=_cparams(("arbitrary",) * 3, has_side_effects=True),
    )(*args, *x_args)


LN_ROWS = 512


def _ln_stats(r):
    mu = jnp.mean(r, axis=-1, keepdims=True)
    xc = r - mu
    var = jnp.mean(xc * xc, axis=-1, keepdims=True)
    return xc * lax.rsqrt(var + LN_EPS)


def _ln_fwd(r, g, b, name):
    S, D = r.shape

    def body(r_ref, g_ref, b_ref, h_ref):
        h_ref[...] = _ln_stats(r_ref[...]) * g_ref[...] + b_ref[...]

    row = pl.BlockSpec((LN_ROWS, D), lambda i: (i, 0))
    vec = pl.BlockSpec((1, D), lambda i: (0, 0))
    return pl.pallas_call(
        body, name=name, grid=(S // LN_ROWS,), in_specs=[row, vec, vec], out_specs=row,
        out_shape=jax.ShapeDtypeStruct((S, D), F32),
        compiler_params=_cparams(("parallel",)),
    )(r, g, b)


def _ln_bwd_core(xhat, dy, g):
    dxh = dy * g
    m1 = jnp.mean(dxh, axis=-1, keepdims=True)
    m2 = jnp.mean(dxh * xhat, axis=-1, keepdims=True)
    return dxh - m1 - xhat * m2


def _ln_bwd(r, dy, g, name):
    S, D = r.shape

    def body(r_ref, dy_ref, g_ref, dr_ref, gg_ref, gb_ref):
        x = r_ref[...]
        mu = jnp.mean(x, axis=-1, keepdims=True)
        xc = x - mu
        rstd = lax.rsqrt(jnp.mean(xc * xc, axis=-1, keepdims=True) + LN_EPS)
        xhat = xc * rstd
        dy = dy_ref[...]
        dr_ref[...] = rstd * _ln_bwd_core(xhat, dy, g_ref[...])

        @pl.when(pl.program_id(0) == 0)
        def _():
            gg_ref[...] = jnp.zeros_like(gg_ref)
            gb_ref[...] = jnp.zeros_like(gb_ref)

        gg_ref[...] += jnp.sum(dy * xhat, axis=0, keepdims=True)
        gb_ref[...] += jnp.sum(dy, axis=0, keepdims=True)

    row = pl.BlockSpec((LN_ROWS, D), lambda i: (i, 0))
    vec = pl.BlockSpec((1, D), lambda i: (0, 0))
    return pl.pallas_call(
        body, name=name, grid=(S // LN_ROWS,), in_specs=[row, row, vec], out_specs=[row, vec, vec],
        out_shape=[jax.ShapeDtypeStruct((S, D), F32), jax.ShapeDtypeStruct((1, D), F32),
                   jax.ShapeDtypeStruct((1, D), F32)],
        compiler_params=_cparams(("arbitrary",)),
    )(r, dy, g)


def _ln_loss_bwd(r, target, g, b, name):
    S, D = r.shape

    def body(r_ref, t_ref, g_ref, b_ref, dr_ref, loss_ref, gg_ref, gb_ref):
        x = r_ref[...]
        mu = jnp.mean(x, axis=-1, keepdims=True)
        xc = x - mu
        rstd = lax.rsqrt(jnp.mean(xc * xc, axis=-1, keepdims=True) + LN_EPS)
        xhat = xc * rstd
        y = xhat * g_ref[...] + b_ref[...]
        err = y - t_ref[...]
        dy = err * (1.0 / D)
        dr_ref[...] = rstd * _ln_bwd_core(xhat, dy, g_ref[...])

        @pl.when(pl.program_id(0) == 0)
        def _():
            loss_ref[...] = jnp.zeros_like(loss_ref)
            gg_ref[...] = jnp.zeros_like(gg_ref)
            gb_ref[...] = jnp.zeros_like(gb_ref)

        per_row = jnp.sum(err * err, axis=-1, keepdims=True) * (0.5 / D)
        loss_ref[...] += jnp.broadcast_to(jnp.sum(per_row, axis=0, keepdims=True), loss_ref.shape)
        gg_ref[...] += jnp.sum(dy * xhat, axis=0, keepdims=True)
        gb_ref[...] += jnp.sum(dy, axis=0, keepdims=True)

    row = pl.BlockSpec((LN_ROWS, D), lambda i: (i, 0))
    vec = pl.BlockSpec((1, D), lambda i: (0, 0))
    lvec = pl.BlockSpec((1, LANE), lambda i: (0, 0))
    return pl.pallas_call(
        body, name=name, grid=(S // LN_ROWS,), in_specs=[row, row, vec, vec],
        out_specs=[row, lvec, vec, vec],
        out_shape=[jax.ShapeDtypeStruct((S, D), F32), jax.ShapeDtypeStruct((1, LANE), F32),
                   jax.ShapeDtypeStruct((1, D), F32), jax.ShapeDtypeStruct((1, D), F32)],
        compiler_params=_cparams(("arbitrary",)),
    )(r, target, g, b)


CONV_COLS = 256
CONV_ROWS = 256
HALO = 8
INV_SQRT2 = 1.0 / math.sqrt(2.0)
INV_SQRT2PI = 1.0 / math.sqrt(2.0 * math.pi)


def _gelu(x):
    return 0.5 * x * (1.0 + lax.erf(x * INV_SQRT2))


def _gelu_and_grad(x):
    cdf = 0.5 * (1.0 + lax.erf(x * INV_SQRT2))
    return x * cdf, cdf + x * jnp.exp(-0.5 * x * x) * INV_SQRT2PI


def _conv_rows(ext, w_ref, b_ref, n):
    total = ext.shape[0]
    s1 = pltpu.roll(ext, 1, 0)
    s2 = pltpu.roll(ext, 2, 0)
    u = w_ref[2:3, :] * ext + w_ref[1:2, :] * s1 + w_ref[0:1, :] * s2 + b_ref[...]
    return u[HALO:total], s1[HALO:total], s2[HALO:total]


def _conv_gelu_fwd(u0, conv_w, conv_b, name):
    S, C2 = u0.shape
    F = C2 // 2
    ncb = F // CONV_COLS
    nrc = S // CONV_ROWS

    def body(ua_ref, uc_ref, wa_ref, wc_ref, ba_ref, bc_ref, p_ref):
        def chunk(ci, _):
            r0 = pl.multiple_of(ci * CONV_ROWS, CONV_ROWS)
            p0 = pl.multiple_of(jnp.maximum(r0 - HALO, 0), HALO)
            keep = (ci > 0).astype(F32)

            def load(ref):
                prev = ref[pl.ds(p0, HALO), :] * keep
                return jnp.concatenate([prev, ref[pl.ds(r0, CONV_ROWS), :]], axis=0)

            a, _, _ = _conv_rows(load(ua_ref), wa_ref, ba_ref, CONV_ROWS)
            c, _, _ = _conv_rows(load(uc_ref), wc_ref, bc_ref, CONV_ROWS)
            p_ref[pl.ds(r0, CONV_ROWS), :] = (_gelu(a) * c).astype(p_ref.dtype)
            return 0

        lax.fori_loop(0, nrc, chunk, 0)

    col_a = pl.BlockSpec((S, CONV_COLS), lambda j: (0, j))
    col_c = pl.BlockSpec((S, CONV_COLS), lambda j: (0, j + ncb))
    w_a = pl.BlockSpec((CONV_WIDTH, CONV_COLS), lambda j: (0, j))
    w_c = pl.BlockSpec((CONV_WIDTH, CONV_COLS), lambda j: (0, j + ncb))
    b_a = pl.BlockSpec((1, CONV_COLS), lambda j: (0, j))
    b_c = pl.BlockSpec((1, CONV_COLS), lambda j: (0, j + ncb))
    return pl.pallas_call(
        body, name=name, grid=(ncb,), in_specs=[col_a, col_c, w_a, w_c, b_a, b_c], out_specs=col_a,
        out_shape=jax.ShapeDtypeStruct((S, F), MXU_DTYPE),
        compiler_params=_cparams(("parallel",)),
    )(u0, u0, conv_w, conv_w, conv_b, conv_b)


def _conv_gelu_bwd(u0, dp, conv_w, conv_b, name):
    S, C2 = u0.shape
    F = C2 // 2
    ncb = F // CONV_COLS
    nrc = S // CONV_ROWS
    EXT = CONV_ROWS + HALO

    def body(ua_ref, uc_ref, dp_ref, wa_ref, wc_ref, ba_ref, bc_ref,
             da_ref, dc_ref, gwa_ref, gwc_ref, gba_ref, gbc_ref):
        gwa_ref[...] = jnp.zeros_like(gwa_ref)
        gwc_ref[...] = jnp.zeros_like(gwc_ref)
        gba_ref[...] = jnp.zeros_like(gba_ref)
        gbc_ref[...] = jnp.zeros_like(gbc_ref)
        rid = lax.broadcasted_iota(jnp.int32, (EXT, CONV_COLS), 0)

        def chunk(ci, _):
            r0 = pl.multiple_of(ci * CONV_ROWS, CONV_ROWS)
            p0 = pl.multiple_of(jnp.maximum(r0 - HALO, 0), HALO)
            n0 = pl.multiple_of(jnp.minimum(r0 + CONV_ROWS, S - HALO), HALO)
            keep_prev = (ci > 0).astype(F32)
            keep_next = (ci < nrc - 1).astype(F32)

            def load(ref):
                return jnp.concatenate([ref[pl.ds(p0, HALO), :] * keep_prev,
                                        ref[pl.ds(r0, CONV_ROWS), :],
                                        ref[pl.ds(n0, HALO), :] * keep_next], axis=0)

            ext_a = load(ua_ref)
            ext_c = load(uc_ref)
            a, a1, a2 = _conv_rows(ext_a, wa_ref, ba_ref, EXT)
            c, c1, c2 = _conv_rows(ext_c, wc_ref, bc_ref, EXT)
            a0 = ext_a[HALO:HALO + EXT]
            c0 = ext_c[HALO:HALO + EXT]
            dpe = jnp.concatenate([dp_ref[pl.ds(r0, CONV_ROWS), :].astype(F32),
                                   dp_ref[pl.ds(n0, HALO), :].astype(F32) * keep_next], axis=0)
            gelu_a, slope_a = _gelu_and_grad(a)
            d_a = dpe * c * slope_a
            d_c = dpe * gelu_a
            own = rid < CONV_ROWS

            def back(d_u, w_ref, x0, x1, x2, d_ref, gw_ref, gb_ref):
                d_u0 = (w_ref[2:3, :] * d_u + w_ref[1:2, :] * pltpu.roll(d_u, EXT - 1, 0)
                        + w_ref[0:1, :] * pltpu.roll(d_u, EXT - 2, 0))
                d_ref[pl.ds(r0, CONV_ROWS), :] = d_u0[0:CONV_ROWS].astype(d_ref.dtype)
                d_own = jnp.where(own, d_u, 0.0)
                gw_ref[...] += jnp.concatenate(
                    [jnp.sum(d_own * x2, axis=0, keepdims=True),
                     jnp.sum(d_own * x1, axis=0, keepdims=True),
                     jnp.sum(d_own * x0, axis=0, keepdims=True)], axis=0)
                gb_ref[...] += jnp.sum(d_own, axis=0, keepdims=True)

            back(d_a, wa_ref, a0, a1, a2, da_ref, gwa_ref, gba_ref)
            back(d_c, wc_ref, c0, c1, c2, dc_ref, gwc_ref, gbc_ref)
            return 0

        lax.fori_loop(0, nrc, chunk, 0)

    col_a = pl.BlockSpec((S, CONV_COLS), lambda j: (0, j))
    col_c = pl.BlockSpec((S, CONV_COLS), lambda j: (0, j + ncb))
    w_a = pl.BlockSpec((CONV_WIDTH, CONV_COLS), lambda j: (0, j))
    w_c = pl.BlockSpec((CONV_WIDTH, CONV_COLS), lambda j: (0, j + ncb))
    b_a = pl.BlockSpec((1, CONV_COLS), lambda j: (0, j))
    b_c = pl.BlockSpec((1, CONV_COLS), lambda j: (0, j + ncb))
    outs = pl.pallas_call(
        body, name=name, grid=(ncb,),
        in_specs=[col_a, col_c, col_a, w_a, w_c, b_a, b_c],
        out_specs=[col_a, col_a, w_a, w_a, b_a, b_a],
        out_shape=[jax.ShapeDtypeStruct((S, F), MXU_DTYPE), jax.ShapeDtypeStruct((S, F), MXU_DTYPE),
                   jax.ShapeDtypeStruct((CONV_WIDTH, F), F32), jax.ShapeDtypeStruct((CONV_WIDTH, F), F32),
                   jax.ShapeDtypeStruct((1, F), F32), jax.ShapeDtypeStruct((1, F), F32)],
        compiler_params=_cparams(("parallel",)),
    )(u0, u0, dp, conv_w, conv_w, conv_b, conv_b)
    da, dc, gwa, gwc, gba, gbc = outs
    return (jnp.concatenate([da, dc], axis=1), jnp.concatenate([gwa, gwc], axis=1),
            jnp.concatenate([gba, gbc], axis=1))


SB_TK = 128
SB_TQ_FWD = 1024
SB_TQ_BWD = 1024
SB_PAIRS_FWD = 2


def _softplus(z):
    return jnp.maximum(z, 0.0) + jnp.log(1.0 + jnp.exp(-jnp.abs(z)))


def _tri_ones(after):
    r = lax.broadcasted_iota(jnp.int32, (SB_TK, 2 * SB_TK), 0)
    c = lax.broadcasted_iota(jnp.int32, (SB_TK, 2 * SB_TK), 1)
    tri = (r > c) if after else (r < c)
    return jnp.where(c >= SB_TK, 1.0, jnp.where(tri, 1.0, 0.0)).astype(BF16)


def _sb_block_specs(S, TQ):
    NP = SB_WIDTH // LANE
    return [pl.BlockSpec((TQ, LANE), lambda p, i: (i, p)),
            pl.BlockSpec((S, LANE), lambda p, i: (0, NP + p)),
            pl.BlockSpec((S, LANE), lambda p, i: (0, 2 * NP + p))]


def _sb_fwd(qkv, name, ride=()):
    S = qkv.shape[0]
    TK = SB_TK
    TQ = min(SB_TQ_FWD, S)
    R = TQ // TK
    NP = SB_WIDTH // LANE
    PS = SB_PAIRS_FWD
    W = PS * LANE
    NH = 2 * PS
    nq = S // TQ
    nkb = S // TK
    grid = (NP // PS, nq)

    def body(q_ref, k_ref, v_ref, o_ref, wk_ref, wbuf, wsem, bbuf, bsem):
        p, i = pl.program_id(0), pl.program_id(1)
        row = lax.broadcasted_iota(jnp.int32, (TQ, TK), 0)
        col = lax.broadcasted_iota(jnp.int32, (TQ, TK), 1)
        after_ones = _tri_ones(True)
        first = lax.broadcasted_iota(jnp.int32, (TK, LANE), 1) < SB_HEAD_DIM

        def keep(slot, j):
            return pltpu.make_async_copy(wbuf.at[slot], wk_ref.at[pl.ds(p * NH, NH), i, j], wsem.at[slot])

        def keep_band(u):
            return pltpu.make_async_copy(bbuf.at[u], wk_ref.at[pl.ds(p * NH, NH), i, R * i + u], bsem.at[u])

        def block(j, carry, r0, stage):
            k0 = pl.multiple_of(j * TK, TK)
            masked = r0 is not None
            r0 = r0 or 0
            out = []
            for pr in range(PS):
                lanes = slice(pr * LANE, (pr + 1) * LANE)
                acc0, tail_a, tail_b = carry[3 * pr:3 * pr + 3]
                kab = k_ref[pl.ds(k0, TK), lanes]
                vab = v_ref[pl.ds(k0, TK), lanes]
                none = jnp.zeros_like(kab)
                k2 = jnp.concatenate([jnp.where(first, kab, none), jnp.where(first, none, kab)], axis=0)
                v2 = jnp.concatenate([jnp.where(first, vab, none), jnp.where(first, none, vab)], axis=0)
                z2 = _dot(q_ref[r0:, lanes], k2, "nt")
                tails, ws = [], []
                for hh, tail in enumerate((tail_a, tail_b)):
                    z = z2[:, hh * TK:(hh + 1) * TK]
                    sp = _softplus(z)
                    if masked:
                        strict = col[r0:] < row[r0:] - r0
                        sp = jnp.where(strict, sp, 0.0)
                    cs = _dot_split(sp, after_ones, "nn")
                    w = jnp.exp(z - sp - cs[:, :TK] - tail[r0:])
                    if masked:
                        w = jnp.where(strict, w, 0.0)
                    ws.append(w.astype(MXU_DTYPE))
                    stage[2 * pr + hh, r0:, :] = ws[-1]
                    tot = cs[:, TK:]
                    if r0:
                        tot = jnp.concatenate([jnp.zeros((r0, TK), F32), tot], axis=0)
                    tails.append(tail + tot)
                acc_r = acc0[r0:] + _dot(jnp.concatenate(ws, axis=1), v2, "nn")
                acc = jnp.concatenate([acc0[:r0], acc_r], axis=0) if r0 else acc_r
                out += [acc, tails[0], tails[1]]
            return tuple(out)

        carry = (jnp.zeros((TQ, LANE), F32), jnp.zeros((TQ, TK), F32), jnp.zeros((TQ, TK), F32)) * PS
        for u in reversed(range(R)):
            carry = block(R * i + u, carry, u * TK, bbuf.at[u])
        for u in range(R):
            keep_band(u).start()
        trips = R * i

        def below(n, c):
            slot = n % 2
            j = trips - 1 - n

            @pl.when(n >= 2)
            def _():
                keep(slot, j).wait()

            c = block(j, c, None, wbuf.at[slot])
            keep(slot, j).start()
            return c

        carry = lax.fori_loop(0, trips, below, carry)
        for back in (1, 2):

            @pl.when(trips >= back)
            def _():
                keep((trips - back) % 2, 0).wait()

        for u in range(R):
            keep_band(u).wait()
        for pr in range(PS):
            o_ref[:, pr * LANE:(pr + 1) * LANE] = carry[3 * pr].astype(o_ref.dtype)

    in_specs = [pl.BlockSpec((TQ, W), lambda p, i: (i, p)),
                pl.BlockSpec((S, W), lambda p, i: (0, NP // PS + p)),
                pl.BlockSpec((S, W), lambda p, i: (0, 2 * (NP // PS) + p))]
    ospec = pl.BlockSpec((TQ, W), lambda p, i: (i, p))
    x_args, x_in, x_out, x_shapes, x_sems, wrap = _riding_exchange(list(ride), grid)
    return pl.pallas_call(
        wrap(body, 3, 2) if ride else body, name=name, grid=grid,
        in_specs=in_specs + (x_in if ride else []),
        out_specs=[ospec, pl.BlockSpec(memory_space=pl.ANY)] + (x_out if ride else []),
        out_shape=[jax.ShapeDtypeStruct((S, SB_WIDTH), MXU_DTYPE),
                   jax.ShapeDtypeStruct((SB_HEADS, nq, nkb, TQ, TK), MXU_DTYPE)] + (x_shapes if ride else []),
        scratch_shapes=[pltpu.VMEM((2, NH, TQ, TK), MXU_DTYPE), pltpu.SemaphoreType.DMA((2,)),
                        pltpu.VMEM((R, NH, TQ, TK), MXU_DTYPE), pltpu.SemaphoreType.DMA((R,))]
        + (x_sems if ride else []),
        compiler_params=_cparams(("arbitrary", "arbitrary"), has_side_effects=True),
    )(qkv, qkv, qkv, *(x_args if ride else []))


def _sb_bwd(qkv, d_cat, w_kept, name, ride=()):
    S = qkv.shape[0]
    TK = SB_TK
    TQ = min(SB_TQ_BWD, S)
    assert TQ == min(SB_TQ_FWD, S)
    R = TQ // TK
    NP = SB_WIDTH // LANE
    grid = (NP, S // TQ)
    scale = SB_HEAD_DIM ** -0.5

    def body(q_ref, k_ref, v_ref, do_ref, wk_ref, dq_ref, dk_ref, dv_ref, dk_acc, dv_acc, wbuf, wsem, bbuf, bsem):
        p, i = pl.program_id(0), pl.program_id(1)

        @pl.when(i == 0)
        def _():
            dk_acc[...] = jnp.zeros_like(dk_acc)
            dv_acc[...] = jnp.zeros_like(dv_acc)

        row = lax.broadcasted_iota(jnp.int32, (TQ, TK), 0)
        col = lax.broadcasted_iota(jnp.int32, (TQ, TK), 1)
        before_ones = _tri_ones(False)
        first = lax.broadcasted_iota(jnp.int32, (TK, LANE), 1) < SB_HEAD_DIM
        qab = q_ref[...]
        doab = do_ref[...].astype(MXU_DTYPE)
        qdo = jnp.concatenate([qab, doab], axis=1)

        def fetch(slot, j):
            return pltpu.make_async_copy(wk_ref.at[pl.ds(2 * p, 2), i, j], wbuf.at[slot], wsem.at[slot])

        def fetch_band(u):
            return pltpu.make_async_copy(wk_ref.at[pl.ds(2 * p, 2), i, R * i + u], bbuf.at[u], bsem.at[u])

        def block(j, carry, r0, stage):
            k0 = pl.multiple_of(j * TK, TK)
            masked = r0 is not None
            r0 = r0 or 0
            kab = k_ref[pl.ds(k0, TK), :]
            vab = v_ref[pl.ds(k0, TK), :]
            none = jnp.zeros_like(kab)
            k_h = [jnp.where(first, kab, none), jnp.where(first, none, kab)]
            v_h = [jnp.where(first, vab, none), jnp.where(first, none, vab)]
            sums, dzs, ws = [], [], []
            for hh in range(2):
                gsum = carry[1 + hh]
                kv = jnp.concatenate([jnp.concatenate([k_h[hh], none], axis=1),
                                      jnp.concatenate([none, v_h[hh]], axis=1)], axis=0)
                zdw = _dot(qdo[r0:], kv, "nt")
                z = zdw[:, :TK]
                sig = 1.0 / (1.0 + jnp.exp(-z))
                wb = stage[hh, r0:, :]
                g = wb.astype(F32) * zdw[:, TK:]
                cg = _dot_split(g, before_ones, "nn")
                dz = g - sig * (g + cg[:, :TK] + gsum[r0:])
                if masked:
                    dz = jnp.where(col[r0:] < row[r0:] - r0, dz, 0.0)
                dzs.append(dz.astype(MXU_DTYPE))
                ws.append(wb)
                gsum_r = gsum[r0:] + cg[:, TK:]
                if r0:
                    gsum_r = jnp.concatenate([gsum[:r0], gsum_r], axis=0)
                sums.append(gsum_r)
            kvg = _dot(jnp.concatenate(dzs + ws, axis=1), qdo[r0:], "tn")
            dk_acc[pl.ds(k0, TK), :] += jnp.where(first, kvg[:TK, :LANE], kvg[TK:2 * TK, :LANE])
            dv_acc[pl.ds(k0, TK), :] += jnp.where(first, kvg[2 * TK:3 * TK, LANE:], kvg[3 * TK:, LANE:])
            dq_r = carry[0][r0:] + _dot(jnp.concatenate(dzs, axis=1), jnp.concatenate(k_h, axis=0), "nn")
            dq = jnp.concatenate([carry[0][:r0], dq_r], axis=0) if r0 else dq_r
            return (dq, *sums)

        trips = R * i
        for u in range(R):
            fetch_band(u).start()

        @pl.when(trips >= 1)
        def _():
            fetch(0, 0).start()

        def below(j, c):
            slot = j % 2

            @pl.when(j + 1 < trips)
            def _():
                fetch(1 - slot, j + 1).start()

            fetch(slot, j).wait()
            return block(j, c, None, wbuf.at[slot])

        zero = jnp.zeros((TQ, TK), F32)
        carry = lax.fori_loop(0, trips, below, (jnp.zeros((TQ, LANE), F32), zero, zero))
        for u in range(R):
            fetch_band(u).wait()
        for u in range(R):
            carry = block(trips + u, carry, u * TK, bbuf.at[u])
        dq_ref[...] = (carry[0] * scale).astype(dq_ref.dtype)

        @pl.when(i == S // TQ - 1)
        def _():
            dk_ref[...] = dk_acc[...].astype(dk_ref.dtype)
            dv_ref[...] = dv_acc[...].astype(dv_ref.dtype)

    qspec = pl.BlockSpec((TQ, LANE), lambda p, i: (i, p))
    full = pl.BlockSpec((S, LANE), lambda p, i: (0, p))
    sds = jax.ShapeDtypeStruct((S, SB_WIDTH), MXU_DTYPE)
    scratch = [pltpu.VMEM((S, LANE), F32), pltpu.VMEM((S, LANE), F32),
               pltpu.VMEM((2, 2, TQ, TK), MXU_DTYPE), pltpu.SemaphoreType.DMA((2,)),
               pltpu.VMEM((R, 2, TQ, TK), MXU_DTYPE), pltpu.SemaphoreType.DMA((R,))]
    x_args, x_in, x_out, x_shapes, x_sems, wrap = _riding_exchange(list(ride), grid)
    return pl.pallas_call(
        wrap(body, 5, 3) if ride else body, name=name, grid=grid,
        in_specs=_sb_block_specs(S, TQ) + [qspec, pl.BlockSpec(memory_space=pl.ANY)] + (x_in if ride else []),
        out_specs=[qspec, full, full] + (x_out if ride else []),
        out_shape=[sds, sds, sds] + (x_shapes if ride else []),
        scratch_shapes=scratch + (x_sems if ride else []),
        compiler_params=_cparams(("arbitrary", "arbitrary"), has_side_effects=bool(ride)),
    )(qkv, qkv, qkv, d_cat, w_kept, *(x_args if ride else []))


GLA_ROWS = 1024
GLA_GROUP_FWD = 16
GLA_GROUP_BWD = 8
GLA_PAIR_K = 2 * GLA_KEY_DIM
GLA_PAIR_V = 2 * GLA_VAL_DIM


def _log_sigmoid(x):
    return -_softplus(-x)


def _dot_split_lhs01(m01, x):
    hi = x.astype(BF16)
    lo = (x - hi.astype(F32)).astype(BF16)
    return lax.dot_general(jnp.concatenate([m01, m01], axis=1), jnp.concatenate([hi, lo], axis=0),
                           (((1,), (0,)), ((), ())), preferred_element_type=F32)


def _dot_split_tn(x, m01):
    hi = x.astype(BF16)
    lo = (x - hi.astype(F32)).astype(BF16)
    return lax.dot_general(jnp.concatenate([hi, lo], axis=0), jnp.concatenate([m01, m01], axis=0),
                           (((0,), (0,)), ((), ())), preferred_element_type=F32)


def _dot_split_nt01(m01, x):
    hi = x.astype(BF16)
    lo = (x - hi.astype(F32)).astype(BF16)
    return lax.dot_general(jnp.concatenate([m01, m01], axis=1), jnp.concatenate([hi, lo], axis=1),
                           (((1,), (1,)), ((), ())), preferred_element_type=F32)


def _gla_consts():
    C = CHUNK
    row = lax.broadcasted_iota(jnp.int32, (C, C), 0)
    col = lax.broadcasted_iota(jnp.int32, (C, C), 1)
    first = lax.broadcasted_iota(jnp.int32, (C, GLA_PAIR_K), 1) < GLA_KEY_DIM
    r = lax.broadcasted_iota(jnp.int32, (GLA_PAIR_K, GLA_PAIR_V), 0)
    c = lax.broadcasted_iota(jnp.int32, (GLA_PAIR_K, GLA_PAIR_V), 1)
    own = (r < GLA_KEY_DIM) == (c < GLA_VAL_DIM)
    rowk = lax.broadcasted_iota(jnp.int32, (C, GLA_PAIR_K), 0)
    return dict(row=row, col=col, first=first, own=own, rowk=rowk,
                incl=jnp.where(row >= col, 1.0, 0.0).astype(BF16),
                ones=jnp.ones((C, GLA_PAIR_V), BF16))


def _pack_state(state):
    top = lax.broadcasted_iota(jnp.int32, (GLA_PAIR_K, GLA_VAL_DIM), 0) < GLA_KEY_DIM
    return jnp.where(top, state[:, :GLA_VAL_DIM], state[:, GLA_VAL_DIM:])


def _unpack_state(packed):
    top = lax.broadcasted_iota(jnp.int32, (GLA_PAIR_K, GLA_VAL_DIM), 0) < GLA_KEY_DIM
    return jnp.concatenate([jnp.where(top, packed, 0.0), jnp.where(top, 0.0, packed)], axis=1)


def _gla_chunk(qc, kc, vc, gate, k):
    C = CHUNK
    row, col, first, rowk = k["row"], k["col"], k["first"], k["rowk"]
    la = _log_sigmoid(gate) * (1.0 / GLA_TAU)
    b = _dot_split_lhs01(k["incl"], la)
    b_ref = jnp.sum(jnp.where(rowk == C // 2 - 1, b, 0.0), axis=0, keepdims=True)
    b_last = jnp.sum(la, axis=0, keepdims=True)
    qs = qc * (GLA_KEY_DIM ** -0.5)
    q_in = qs * jnp.exp(b - b_ref)
    k_in = kc * jnp.exp(b_ref - b)
    k_dec = kc * jnp.exp(b_last - b)
    q_b = qs * jnp.exp(b)
    k_in_h = [jnp.where(first, k_in, 0.0), jnp.where(first, 0.0, k_in)]
    v_h = [vc[:, :GLA_VAL_DIM], vc[:, GLA_VAL_DIM:]]
    sc = [jnp.where(row >= col, _dot(q_in, k_in_h[hh], "nt"), 0.0) for hh in range(2)]
    o_intra = jnp.concatenate([_dot(sc[hh], v_h[hh], "nn") for hh in range(2)], axis=1)
    upd = jnp.where(k["own"], _dot(k_dec, vc, "tn"), 0.0)
    dec_col = jnp.exp(_dot_split_tn(la, k["ones"]))
    return dict(la=la, b=b, b_ref=b_ref, b_last=b_last, q_in=q_in, k_in=k_in, k_dec=k_dec, q_b=q_b,
                k_in_h=k_in_h, v_h=v_h, sc=sc, o_intra=o_intra, upd=upd, dec_col=dec_col)


def _rms_gate(o, gg):
    rinv = lax.rsqrt(jnp.mean(o * o, axis=-1, keepdims=True) + RMS_EPS)
    o_n = o * rinv
    sg = 1.0 / (1.0 + jnp.exp(-gg))
    return o_n, rinv, sg


def _gla_in_specs(rows_of, RB):
    PK, PV = GLA_PAIR_K, GLA_PAIR_V
    return [pl.BlockSpec((RB, PK), lambda i, p: (rows_of(i), p)),
            pl.BlockSpec((RB, PK), lambda i, p: (rows_of(i), (OFF_GK - OFF_GQ) // PK + p)),
            pl.BlockSpec((RB, PV), lambda i, p: (rows_of(i), (OFF_GV - OFF_GQ) // PV + p)),
            pl.BlockSpec((RB, PV), lambda i, p: (rows_of(i), (OFF_GG - OFF_GQ) // PV + p)),
            pl.BlockSpec((RB, LANE), lambda i, p: (rows_of(i), (OFF_GA - OFF_GQ) // LANE)),
            pl.BlockSpec((1, LANE, PK), lambda i, p: (p, 0, 0)),
            pl.BlockSpec((1, 1, PK), lambda i, p: (p, 0, 0)),
            pl.BlockSpec((1, GLA_VAL_DIM), lambda i, p: (0, 0))]


def _gla_fwd(proj, gate_up_p, gate_bias_p, gnorm, name):
    S = proj.shape[0]
    C, RB, PK, PV, dv = CHUNK, min(GLA_ROWS, S), GLA_PAIR_K, GLA_PAIR_V, GLA_VAL_DIM
    NP = GLA_HEADS // 2
    G = min(GLA_GROUP_FWD, RB // C)
    nchunk = S // C
    ngroup = RB // (C * G)

    def body(q_ref, k_ref, v_ref, gg_ref, ga_ref, gu_ref, gb_ref, gn_ref, o_ref, prev_ref, st_ref):
        i, p = pl.program_id(0), pl.program_id(1)
        k = _gla_consts()

        @pl.when(i == 0)
        def _():
            st_ref[p] = jnp.zeros((PK, PV), F32)

        def group(gi, state):
            for u in range(G):
                ci = gi * G + u
                rows = pl.ds(pl.multiple_of(ci * C, C), C)
                gate = _dot(ga_ref[rows, :], gu_ref[0], "nn") + gb_ref[0]
                f = _gla_chunk(q_ref[rows, :], k_ref[rows, :], v_ref[rows, :], gate, k)
                prev_ref[0, ci] = _pack_state(state)
                o = f["o_intra"] + _dot(f["q_b"], state, "nn")
                state = f["dec_col"] * state + f["upd"]
                ggv = gg_ref[rows, :]
                halves = []
                for hh in range(2):
                    lanes = slice(hh * dv, (hh + 1) * dv)
                    o_n, _, sg = _rms_gate(o[:, lanes], ggv[:, lanes])
                    halves.append(o_n * gn_ref[...] * (ggv[:, lanes] * sg))
                o_ref[rows, :] = jnp.concatenate(halves, axis=1).astype(o_ref.dtype)
            return state

        st_ref[p] = lax.fori_loop(0, ngroup, group, st_ref[p])

    return pl.pallas_call(
        body, name=name, grid=(S // RB, NP), in_specs=_gla_in_specs(lambda i: i, RB),
        out_specs=[pl.BlockSpec((RB, PV), lambda i, p: (i, p)),
                   pl.BlockSpec((1, RB // C, PK, dv), lambda i, p: (p, i, 0, 0))],
        out_shape=[jax.ShapeDtypeStruct((S, GLA_WIDTH), MXU_DTYPE),
                   jax.ShapeDtypeStruct((NP, nchunk, PK, dv), F32)],
        scratch_shapes=[pltpu.VMEM((NP, PK, PV), F32)],
        compiler_params=_cparams(("arbitrary", "arbitrary")),
    )(proj, proj, proj, proj, proj, gate_up_p, gate_bias_p, gnorm)


def _gla_bwd(proj, gate_up_p, gate_bias_p, gnorm, prev, d_cat, name):
    S = proj.shape[0]
    C, RB, PK, PV, dv = CHUNK, min(GLA_ROWS, S), GLA_PAIR_K, GLA_PAIR_V, GLA_VAL_DIM
    NP = GLA_HEADS // 2
    G = min(GLA_GROUP_BWD, RB // C)
    nb = S // RB
    ngroup = RB // (C * G)

    def body(q_ref, k_ref, v_ref, gg_ref, ga_ref, gu_ref, gb_ref, gn_ref, prev_ref, do_ref,
             dq_ref, dk_ref, dv_ref, dgg_ref, dga_ref, ggu_ref, ggb_ref, ggn_ref, st_ref):
        i, p = pl.program_id(0), pl.program_id(1)
        k = _gla_consts()
        row, col, first, rowk = k["row"], k["col"], k["first"], k["rowk"]
        gu = gu_ref[0]

        @pl.when(i == 0)
        def _():
            st_ref[p] = jnp.zeros((PK, PV), F32)
            ggu_ref[p] = jnp.zeros((LANE, PK), F32)
            ggb_ref[p] = jnp.zeros((1, PK), F32)

        @pl.when(jnp.logical_and(i == 0, p == 0))
        def _():
            ggn_ref[...] = jnp.zeros_like(ggn_ref)

        @pl.when(p == 0)
        def _():
            dga_ref[...] = jnp.zeros_like(dga_ref)

        upper_incl = jnp.where(col >= row, 1.0, 0.0).astype(BF16)
        ones_8 = jnp.ones((8, PV), BF16)

        def group(gn, dstate):
            gi = ngroup - 1 - gn
            for u in reversed(range(G)):
                ci = gi * G + u
                rows = pl.ds(pl.multiple_of(ci * C, C), C)
                ga = ga_ref[rows, :]
                gate = _dot(ga, gu, "nn") + gb_ref[0]
                vc = v_ref[rows, :]
                f = _gla_chunk(q_ref[rows, :], k_ref[rows, :], vc, gate, k)
                state = _unpack_state(prev_ref[0, ci])
                o = f["o_intra"] + _dot(f["q_b"], state, "nn")
                ggv = gg_ref[rows, :]
                dout = do_ref[rows, :]
                d_o_h, dgg_h = [], []
                for hh in range(2):
                    lanes = slice(hh * dv, (hh + 1) * dv)
                    o_n, rinv, sg = _rms_gate(o[:, lanes], ggv[:, lanes])
                    silu = ggv[:, lanes] * sg
                    dgg_h.append(dout[:, lanes] * o_n * gn_ref[...] * (sg * (1.0 + ggv[:, lanes] * (1.0 - sg))))
                    d_ong = dout[:, lanes] * silu
                    ggn_ref[...] += jnp.sum(d_ong * o_n, axis=0, keepdims=True)
                    d_on = d_ong * gn_ref[...]
                    d_o_h.append(rinv * (d_on - o_n * jnp.mean(d_on * o_n, axis=-1, keepdims=True)))
                dgg_ref[rows, :] = jnp.concatenate(dgg_h, axis=1).astype(dgg_ref.dtype)
                d_o = jnp.concatenate(d_o_h, axis=1)
                d_upd = jnp.where(k["own"], dstate, 0.0)
                d_dec_col = dstate * state * f["dec_col"]
                dstate = f["dec_col"] * dstate + _dot(f["q_b"], d_o, "tn")
                dsc = [jnp.where(row >= col, _dot(d_o_h[hh], f["v_h"][hh], "nt"), 0.0) for hh in range(2)]
                dv_ref[rows, :] = (jnp.concatenate([_dot(f["sc"][hh], d_o_h[hh], "tn") for hh in range(2)], axis=1)
                                   + _dot(f["k_dec"], d_upd, "nn")).astype(dv_ref.dtype)
                q_in_h = [jnp.where(first, f["q_in"], 0.0), jnp.where(first, 0.0, f["q_in"])]
                dq_in = _dot(dsc[0], f["k_in_h"][0], "nn") + _dot(dsc[1], f["k_in_h"][1], "nn")
                dk_in = _dot(dsc[0], q_in_h[0], "tn") + _dot(dsc[1], q_in_h[1], "tn")
                dq_b = _dot(d_o, state, "nt")
                dkdec = _dot(vc, d_upd, "nt")
                b = f["b"]
                e1 = jnp.exp(b - f["b_ref"])
                e2 = jnp.exp(f["b_ref"] - b)
                e3 = jnp.exp(f["b_last"] - b)
                eb = jnp.exp(b)
                dq_ref[rows, :] = ((dq_in * e1 + dq_b * eb) * (GLA_KEY_DIM ** -0.5)).astype(dq_ref.dtype)
                dk_ref[rows, :] = (dk_in * e2 + dkdec * e3).astype(dk_ref.dtype)
                t_q = dq_in * f["q_in"]
                t_k = dk_in * f["k_in"]
                t_d = dkdec * f["k_dec"]
                db = t_q - t_k - t_d + dq_b * f["q_b"]
                db_ref = jnp.sum(t_k - t_q, axis=0, keepdims=True)
                db_last = (jnp.sum(t_d, axis=0, keepdims=True)
                           + jnp.max(_dot_split_nt01(ones_8, d_dec_col), axis=0, keepdims=True))
                db = db + jnp.where(rowk == C // 2 - 1, db_ref, 0.0) + jnp.where(rowk == C - 1, db_last, 0.0)
                dla = _dot_split_lhs01(upper_incl, db)
                d_gate = dla * (1.0 / GLA_TAU) * (1.0 / (1.0 + jnp.exp(gate)))
                ggb_ref[p] += jnp.sum(d_gate, axis=0, keepdims=True)
                ggu_ref[p] += _dot(ga, d_gate, "tn")
                dga_ref[rows, :] += _dot(d_gate, gu, "nt")
            return dstate

        st_ref[p] = lax.fori_loop(0, ngroup, group, st_ref[p])

    back = lambda i: nb - 1 - i
    NPV = SB_WIDTH // PV
    return pl.pallas_call(
        body, name=name, grid=(nb, NP),
        in_specs=_gla_in_specs(back, RB) + [pl.BlockSpec((1, RB // C, PK, dv), lambda i, p: (p, back(i), 0, 0)),
                                            pl.BlockSpec((RB, PV), lambda i, p: (back(i), NPV + p))],
        out_specs=[pl.BlockSpec((RB, PK), lambda i, p: (back(i), p)),
                   pl.BlockSpec((RB, PK), lambda i, p: (back(i), p)),
                   pl.BlockSpec((RB, PV), lambda i, p: (back(i), p)),
                   pl.BlockSpec((RB, PV), lambda i, p: (back(i), p)),
                   pl.BlockSpec((RB, LANE), lambda i, p: (back(i), 0)),
                   pl.BlockSpec((NP, LANE, PK), lambda i, p: (0, 0, 0)),
                   pl.BlockSpec((NP, 1, PK), lambda i, p: (0, 0, 0)),
                   pl.BlockSpec((1, dv), lambda i, p: (0, 0))],
        out_shape=[jax.ShapeDtypeStruct((S, NP * PK), MXU_DTYPE), jax.ShapeDtypeStruct((S, NP * PK), MXU_DTYPE),
                   jax.ShapeDtypeStruct((S, GLA_WIDTH), MXU_DTYPE), jax.ShapeDtypeStruct((S, GLA_WIDTH), MXU_DTYPE),
                   jax.ShapeDtypeStruct((S, LANE), F32), jax.ShapeDtypeStruct((NP, LANE, PK), F32),
                   jax.ShapeDtypeStruct((NP, 1, PK), F32), jax.ShapeDtypeStruct((1, dv), F32)],
        scratch_shapes=[pltpu.VMEM((NP, PK, PV), F32)],
        compiler_params=_cparams(("arbitrary", "arbitrary")),
    )(proj, proj, proj, proj, proj, gate_up_p, gate_bias_p, gnorm, prev, d_cat)


def _exchange_copies(scatter_flags, ins, outs, send_sems, recv_sems, local_sems):
    n_peer = N_DEV - 1
    x, y, c = lax.axis_index("x"), lax.axis_index("y"), lax.axis_index("c")
    me = 4 * x + 2 * y + c
    copies = []
    for a, scatter in enumerate(scatter_flags):
        own = ins[a].at[me] if scatter else ins[a]
        copies.append(pltpu.make_async_copy(own, outs[a].at[me], local_sems.at[a]))
    for r in range(1, N_DEV):
        px = 1 - x if r & 4 else x
        py = 1 - y if r & 2 else y
        pc = 1 - c if r & 1 else c
        for a, scatter in enumerate(scatter_flags):
            src = ins[a].at[4 * px + 2 * py + pc] if scatter else ins[a]
            copies.append(pltpu.make_async_remote_copy(
                src_ref=src, dst_ref=outs[a].at[me],
                send_sem=send_sems.at[a * n_peer + r - 1], recv_sem=recv_sems.at[a * n_peer + r - 1],
                device_id=(px, py, pc), device_id_type=MESH_ID))
    return copies


def _exchange_shapes(items):
    out_shape = []
    for arr, scatter in items:
        shp = arr.shape if scatter else (N_DEV,) + arr.shape
        out_shape.append(jax.ShapeDtypeStruct(shp, arr.dtype))
    n = len(items)
    sems = [pltpu.SemaphoreType.DMA((n * (N_DEV - 1),)), pltpu.SemaphoreType.DMA((n * (N_DEV - 1),)),
            pltpu.SemaphoreType.DMA((n,))]
    return out_shape, sems


def _exchange(items, name):
    n = len(items)
    flags = [sc for _, sc in items]

    def body(*refs):
        copies = _exchange_copies(flags, refs[:n], refs[n:2 * n], *refs[2 * n:])
        for cp in copies:
            cp.start()
        for cp in copies:
            cp.wait()

    out_shape, sems = _exchange_shapes(items)
    any_spec = pl.BlockSpec(memory_space=pl.ANY)
    return pl.pallas_call(
        body, name=name, in_specs=[any_spec] * n, out_specs=[any_spec] * n, out_shape=out_shape,
        scratch_shapes=sems, compiler_params=pltpu.CompilerParams(has_side_effects=True),
    )(*[arr for arr, _ in items])


def _all_gather_two_level(arrays, name):
    n = len(arrays)
    K = N_DEV - 1

    def body(*refs):
        ins, outs = refs[:n], refs[n:2 * n]
        send_sems, recv_sems, local_sems = refs[2 * n:]
        x, y, c = lax.axis_index("x"), lax.axis_index("y"), lax.axis_index("c")
        sibling = (x, y, 1 - c)
        chips = [(1 - x, y), (x, 1 - y), (1 - x, 1 - y)]

        def slot(px, py, pc):
            return 4 * px + 2 * py + pc

        def copy(a, k, block, to, src=None):
            rows = outs[a].at[slot(*block)]
            return pltpu.make_async_remote_copy(
                src_ref=rows if src is None else src, dst_ref=rows,
                send_sem=send_sems.at[a * K + k], recv_sem=recv_sems.at[a * K + k],
                device_id=to, device_id_type=MESH_ID)

        me = (x, y, c)
        local, first, passed = [], [], []
        for a in range(n):
            cp = pltpu.make_async_copy(ins[a], outs[a].at[slot(*me)], local_sems.at[a])
            cp.start()
            local.append(cp)
            first.append(copy(a, 0, me, sibling, src=ins[a]))
            first += [copy(a, 1 + j, me, (*chip, c), src=ins[a]) for j, chip in enumerate(chips)]
        for cp in first:
            cp.start()
        for j, chip in enumerate(chips):
            for a in range(n):
                copy(a, 1 + j, (*chip, c), me).wait_recv()
                cp = copy(a, 4 + j, (*chip, c), sibling)
                cp.start()
                passed.append(cp)
        for a in range(n):
            copy(a, 0, sibling, me).wait_recv()
            for j, chip in enumerate(chips):
                copy(a, 4 + j, (*chip, 1 - c), me).wait_recv()
        for cp in first + passed:
            cp.wait_send()
        for cp in local:
            cp.wait()

    out_shape, sems = _exchange_shapes([(arr, False) for arr in arrays])
    any_spec = pl.BlockSpec(memory_space=pl.ANY)
    return pl.pallas_call(
        body, name=name, in_specs=[any_spec] * n, out_specs=[any_spec] * n, out_shape=out_shape,
        scratch_shapes=sems, compiler_params=pltpu.CompilerParams(has_side_effects=True),
    )(*arrays)


def _riding_exchange(items, grid):
    n = len(items)
    flags = [sc for _, sc in items]
    out_shape, sems = _exchange_shapes(items)
    any_spec = pl.BlockSpec(memory_space=pl.ANY)

    def wrap(body, n_in, n_out):
        def fused(*refs):
            ins = refs[:n_in]
            x_ins = refs[n_in:n_in + n]
            outs = refs[n_in + n:n_in + n + n_out]
            x_outs = refs[n_in + n + n_out:n_in + 2 * n + n_out]
            rest = refs[n_in + 2 * n + n_out:]
            x_sems, scratch = rest[len(rest) - 3:], rest[:len(rest) - 3]
            first = last = True
            for d, n_d in enumerate(grid):
                first = jnp.logical_and(first, pl.program_id(d) == 0)
                last = jnp.logical_and(last, pl.program_id(d) == n_d - 1)

            @pl.when(first)
            def _():
                for cp in _exchange_copies(flags, x_ins, x_outs, *x_sems):
                    cp.start()

            body(*ins, *outs, *scratch)

            @pl.when(last)
            def _():
                for cp in _exchange_copies(flags, x_ins, x_outs, *x_sems):
                    cp.wait()

        return fused

    return [arr for arr, _ in items], [any_spec] * n, [any_spec] * n, out_shape, sems, wrap


def _sum_devices(ref):
    g = ref[0].astype(F32)
    for q in range(1, N_DEV):
        g = g + ref[q].astype(F32)
    return g


def _adam_math(g, w, m, v):
    nm = ADAM_B1 * m + (1.0 - ADAM_B1) * g
    nv = ADAM_B2 * v + (1.0 - ADAM_B2) * (g * g)
    m_hat = nm / (1.0 - ADAM_B1 ** ADAM_STEP)
    v_hat = nv / (1.0 - ADAM_B2 ** ADAM_STEP)
    return -ADAM_LR * (m_hat / (jnp.sqrt(v_hat) + ADAM_EPS) + ADAM_WD * w), nm, nv


def _adamw(grecv, w, m, v, name):
    R, C = w.shape
    tile = _pick(R, (256, 176, 128)) if R * C > 65536 else R

    def body(gr_ref, w_ref, m_ref, v_ref, g_ref, d_ref, nm_ref, nv_ref):
        g = _sum_devices(gr_ref)
        g_ref[...] = g
        d_ref[...], nm_ref[...], nv_ref[...] = _adam_math(g, w_ref[...], m_ref[...], v_ref[...])

    blk = pl.BlockSpec((tile, C), lambda i: (i, 0))
    sds = jax.ShapeDtypeStruct((R, C), F32)
    return pl.pallas_call(
        body, name=name, grid=(R // tile,),
        in_specs=[pl.BlockSpec((N_DEV, tile, C), lambda i: (0, i, 0)), blk, blk, blk],
        out_specs=[blk, blk, blk, blk], out_shape=[sds, sds, sds, sds],
        compiler_params=_cparams(("parallel",)),
    )(grecv, w, m, v)


def _adamw_replicated(grecvs, loss_recv, ws, ms, vs, name):
    nt = len(ws)

    def body(*refs):
        gr, lr = refs[:nt], refs[nt]
        w, m, v = refs[nt + 1:2 * nt + 1], refs[2 * nt + 1:3 * nt + 1], refs[3 * nt + 1:4 * nt + 1]
        outs = refs[4 * nt + 1:]
        outs[0][...] = _sum_devices(lr)
        for t in range(nt):
            g_ref, d_ref, nm_ref, nv_ref = outs[1 + 4 * t:5 + 4 * t]
            g = _sum_devices(gr[t])
            g_ref[...] = g
            d_ref[...], nm_ref[...], nv_ref[...] = _adam_math(g, w[t][...], m[t][...], v[t][...])

    out_shape = [jax.ShapeDtypeStruct((1, LANE), F32)]
    for t in range(nt):
        out_shape += [jax.ShapeDtypeStruct(ws[t].shape, F32)] * 4
    outs = pl.pallas_call(body, name=name, out_shape=out_shape, compiler_params=_cparams())(
        *grecvs, loss_recv, *ws, *ms, *vs)
    return outs[0], [outs[1 + 4 * t:5 + 4 * t] for t in range(nt)]


def _shard_cols(g):
    rows, cols = g.shape
    return g.reshape(rows, N_DEV, cols // N_DEV).transpose(1, 0, 2)


def _unshard_cols(blocks):
    return blocks.transpose(1, 0, 2).reshape(blocks.shape[1], -1)


def _heads(t, n, d):
    return t.reshape(t.shape[0], n, d).transpose(1, 0, 2)


def _unheads(t):
    return t.transpose(1, 0, 2).reshape(t.shape[1], -1)


def kernel(x, w_in, gate_up, gate_bias, gla_norm_g, w_out, ln1_g, ln1_b, w_up, conv_w, conv_b, w_down, ln2_g, ln2_b, loss_target, m_w_in, m_gate_up, m_gate_bias, m_gla_norm_g, m_w_out, m_ln1_g, m_ln1_b, m_w_up, m_conv_w, m_conv_b, m_w_down, m_ln2_g, m_ln2_b, v_w_in, v_gate_up, v_gate_bias, v_gla_norm_g, v_w_out, v_ln1_g, v_ln1_b, v_w_up, v_conv_w, v_conv_b, v_w_down, v_ln2_g, v_ln2_b):
    S, D = x.shape[1], x.shape[2]
    x2, tgt = x[0], loss_target[0]

    gathered = _all_gather_two_level([w_in[0].astype(MXU_DTYPE), gate_up[0], conv_w[0]], "gather_w_in")
    w_in_f = _unshard_cols(gathered[0])
    gate_up_f = _unshard_cols(gathered[1])
    conv_w_f = _unshard_cols(gathered[2])
    w_in_pad = jnp.pad(w_in_f, ((0, 0), (0, IN_PAD - IN_WIDTH)))
    gate_up_p = _heads(jnp.pad(gate_up_f, ((0, LANE - GLA_GATE_RANK), (0, 0))), GLA_HEADS // 2, GLA_PAIR_K)
    gate_bias_p = gate_bias.reshape(GLA_HEADS // 2, 1, GLA_PAIR_K)

    w_qkv = jnp.concatenate([w_in_f[:, :OFF_SBK] * (SB_HEAD_DIM ** -0.5), w_in_f[:, OFF_SBK:OFF_GQ]], axis=1)
    qkv = _matmul(x2, w_qkv, "nn", MXU_DTYPE, "proj_sb")
    proj = _matmul(x2, w_in_pad[:, OFF_GQ:], "nn", F32, "proj_gla")
    sb_o, sb_w, g_out, g_up, g_down = _sb_fwd(
        qkv, "sb_fwd", ride=[(w_out[0].astype(MXU_DTYPE), False), (w_up[0].astype(MXU_DTYPE), False),
                             (w_down[0].astype(MXU_DTYPE), False)])
    w_out_f = g_out.reshape(-1, D)
    w_up_f = _unshard_cols(g_up)
    w_down_f = g_down.reshape(-1, D)
    gla_o, prev = _gla_fwd(proj, gate_up_p, gate_bias_p, gla_norm_g, "gla_fwd")
    cat = jnp.concatenate([sb_o, gla_o], axis=1)
    r1 = _matmul(cat, w_out_f, "nn", F32, "mix", res=x2, res_scale=DN_ALPHA)
    h = _ln_fwd(r1, ln1_g, ln1_b, "ln1")
    u0 = _matmul(h, w_up_f, "nn", F32, "ffn_up")
    p = _conv_gelu_fwd(u0, conv_w_f, conv_b, "conv_gelu")
    r2 = _matmul(p, w_down_f, "nn", F32, "ffn_down", res=h, res_scale=DN_ALPHA)
    d_r2, loss_p, g_ln2_g, g_ln2_b = _ln_loss_bwd(r2, tgt, ln2_g, ln2_b, "ln2_loss")

    d_p = _matmul(d_r2, w_down_f, "nt", MXU_DTYPE, "d_ffn_act")
    g_w_down = _matmul(p, d_r2, "tn", BF16, "grad_w_down")
    d_u0, g_conv_w, g_conv_b = _conv_gelu_bwd(u0, d_p, conv_w_f, conv_b, "conv_gelu_bwd")
    g_w_up = _matmul(h, d_u0, "tn", BF16, "grad_w_up")
    d_h = _matmul(d_u0, w_up_f, "nt", F32, "d_h", res=d_r2, res_scale=DN_ALPHA)
    d_r1, g_ln1_g, g_ln1_b = _ln_bwd(r1, d_h, ln1_g, "ln1_bwd")
    g_w_out = _matmul(cat, d_r1, "tn", BF16, "grad_w_out")
    d_cat = _matmul(d_r1, w_out_f, "nt", F32, "d_cat")
    (d_gq, d_gk, d_gv, d_gg, d_ga_pad, g_gu_p, g_gb_p, g_gnorm) = _gla_bwd(
        proj, gate_up_p, gate_bias_p, gla_norm_g, prev, d_cat, "gla_bwd")
    g_gate_up = _unheads(g_gu_p[:, :GLA_GATE_RANK, :])
    g_gate_bias = g_gb_p.reshape(1, -1)
    small_g = [g_gate_bias, g_gnorm, g_ln1_g, g_ln1_b, g_conv_b, g_ln2_g, g_ln2_b]
    d_sq, d_sk, d_sv, *recv_rest = _sb_bwd(
        qkv, d_cat, sb_w, "sb_bwd",
        ride=[(g_w_out.reshape(N_DEV, -1, D), True), (_shard_cols(g_w_up), True),
              (g_w_down.reshape(N_DEV, -1, D), True), (_shard_cols(g_gate_up), True),
              (_shard_cols(g_conv_w), True)] + [(t, False) for t in small_g] + [(loss_p, False)])
    d_proj = jnp.concatenate([d_sq, d_sk, d_sv, d_gq, d_gk, d_gv, d_gg,
                              d_ga_pad.astype(MXU_DTYPE)], axis=1)
    g_w_in = _matmul(x2, d_proj, "tn", BF16, "grad_w_in")[:, :IN_WIDTH]
    d_x, recv_in = _matmul(d_proj, w_in_pad, "nt", F32, "d_x", res=d_r1, res_scale=DN_ALPHA,
                           ride=[(_shard_cols(g_w_in), True)])

    recv = [recv_in] + recv_rest[:5]
    sharded = [(w_in, m_w_in, v_w_in), (w_out, m_w_out, v_w_out), (w_up, m_w_up, v_w_up),
               (w_down, m_w_down, v_w_down), (gate_up, m_gate_up, v_gate_up), (conv_w, m_conv_w, v_conv_w)]
    upd = [_adamw(recv[n], w[0], m[0], v[0], "adamw_%d" % n) for n, (w, m, v) in enumerate(sharded)]
    loss_row, small = _adamw_replicated(
        recv_rest[5:12], recv_rest[12], [gate_bias, gla_norm_g, ln1_g, ln1_b, conv_b, ln2_g, ln2_b],
        [m_gate_bias, m_gla_norm_g, m_ln1_g, m_ln1_b, m_conv_b, m_ln2_g, m_ln2_b],
        [v_gate_bias, v_gla_norm_g, v_ln1_g, v_ln1_b, v_conv_b, v_ln2_g, v_ln2_b], "adamw_replicated")
    outs = []
    for kind in range(4):
        b_w_in, b_w_out, b_w_up, b_w_down, b_gate_up, b_conv_w = [u[kind][None] for u in upd]
        s_gb, s_gn, s_l1g, s_l1b, s_cb, s_l2g, s_l2b = [t[kind] for t in small]
        outs += [b_w_in, b_gate_up, s_gb, s_gn, b_w_out, s_l1g, s_l1b, b_w_up, b_conv_w, s_cb, b_w_down,
                 s_l2g, s_l2b]
    return (loss_row[0, 0], d_x[None], *outs)
```

```python
import math

import jax
import jax.numpy as jnp
from jax import lax
from jax.experimental import pallas as pl
from jax.experimental.pallas import tpu as pltpu

F32 = jnp.float32
BF16 = jnp.bfloat16
MXU_DTYPE = jnp.bfloat16

N_DEV = 8
D_MODEL = 1024
SB_WIDTH = 512
SB_HEADS = 8
SB_HEAD_DIM = 64
GLA_HEADS = 4
GLA_KEY_DIM = 64
GLA_VAL_DIM = 128
GLA_WIDTH = 512
GLA_GATE_RANK = 16
GLA_TAU = 16.0
CHUNK = 64
D_FF = 2816
CONV_WIDTH = 3
LN_EPS = 1e-5
RMS_EPS = 1e-6
DN_ALPHA = 2.0 ** 0.25
IN_WIDTH = 3088
LANE = 128
IN_PAD = 3200
OFF_SBQ, OFF_SBK, OFF_SBV = 0, 512, 1024
OFF_GQ, OFF_GK, OFF_GV, OFF_GG, OFF_GA = 1536, 1792, 2048, 2560, 3072
GLA_PAD = IN_PAD - OFF_GQ

ADAM_LR = 0.001
ADAM_B1 = 0.9
ADAM_B2 = 0.999
ADAM_EPS = 1e-08
ADAM_WD = 0.01
ADAM_STEP = 10

VMEM_LIMIT = 48 * 1024 * 1024
MESH_ID = pl.DeviceIdType.MESH


def _cparams(sem=None, **kw):
    return pltpu.CompilerParams(dimension_semantics=sem, vmem_limit_bytes=VMEM_LIMIT, **kw)


def _dot(a, b, dims):
    ca, cb = {"nn": (1, 0), "nt": (1, 1), "tn": (0, 0)}[dims]
    return lax.dot_general(a.astype(MXU_DTYPE), b.astype(MXU_DTYPE), (((ca,), (cb,)), ((), ())),
                           preferred_element_type=F32)


def _dot_split(a, b, dims):
    assert dims == "nn"
    hi = a.astype(BF16)
    lo = (a - hi.astype(F32)).astype(BF16)
    return lax.dot_general(jnp.concatenate([hi, lo], axis=1), jnp.concatenate([b, b], axis=0),
                           (((1,), (0,)), ((), ())), preferred_element_type=F32)


def _pick(dim, prefs):
    for p in prefs:
        if dim % p == 0:
            return p
    return dim


def _matmul(a, b, dims, out_dtype, name, res=None, res_scale=1.0, ride=()):
    if dims == "nn":
        (M, K), (_, N) = a.shape, b.shape
    elif dims == "nt":
        (M, K), (N, _) = a.shape, b.shape
    else:
        (K, M), (_, N) = a.shape, b.shape
    tm = _pick(M, (1024, 1408, 512, 256, 128))
    tn = _pick(N, (1408, 1024, 640, 512))
    if tn == N and N > 2048:
        tn = _pick(N, (256, 128))
    tk = _pick(K, (1024, 1408, 640, 512, 256, 128))
    nk = K // tk
    grid = (M // tm, N // tn, nk)
    if dims == "tn":
        a_spec = pl.BlockSpec((tk, tm), lambda i, j, k: (k, i))
    else:
        a_spec = pl.BlockSpec((tm, tk), lambda i, j, k: (i, k))
    if dims == "nt":
        b_spec = pl.BlockSpec((tn, tk), lambda i, j, k: (j, k))
    else:
        b_spec = pl.BlockSpec((tk, tn), lambda i, j, k: (k, j))
    o_spec = pl.BlockSpec((tm, tn), lambda i, j, k: (i, j))
    in_specs = [a_spec, b_spec]
    args = [a, b]
    if res is not None:
        in_specs.append(o_spec)
        args.append(res)

    def body(*refs):
        if res is not None:
            a_ref, b_ref, r_ref, o_ref, acc_ref = refs
        else:
            a_ref, b_ref, o_ref, acc_ref = refs
            r_ref = None
        k = pl.program_id(2)
        part = _dot(a_ref[...], b_ref[...], dims)

        def finish(total):
            if r_ref is not None:
                total = total + res_scale * r_ref[...]
            o_ref[...] = total.astype(o_ref.dtype)

        if nk == 1:
            finish(part)
        else:
            @pl.when(k == 0)
            def _():
                acc_ref[...] = part

            @pl.when(jnp.logical_and(k > 0, k < nk - 1))
            def _():
                acc_ref[...] += part

            @pl.when(k == nk - 1)
            def _():
                finish(acc_ref[...] + part)

    out_sds = jax.ShapeDtypeStruct((M, N), out_dtype)
    acc = pltpu.VMEM((tm, tn), F32)
    if not ride:
        return pl.pallas_call(
            body, name=name, grid=grid, in_specs=in_specs, out_specs=o_spec, out_shape=out_sds,
            scratch_shapes=[acc], compiler_params=_cparams(("parallel", "parallel", "arbitrary")),
        )(*args)
    x_args, x_in, x_out, x_shapes, x_sems, wrap = _riding_exchange(list(ride), grid)
    return pl.pallas_call(
        wrap(body, len(args), 1), name=name, grid=grid, in_specs=in_specs + x_in, out_specs=[o_spec] + x_out,
        out_shape=[out_sds] + x_shapes, scratch_shapes=[acc] + x_sems,
        compiler_params=_cparams(("arbitrary",) * 3, has_side_effects=True),
    )(*args, *x_args)


LN_ROWS = 512


def _ln_stats(r):
    mu = jnp.mean(r, axis=-1, keepdims=True)
    xc = r - mu
    var = jnp.mean(xc * xc, axis=-1, keepdims=True)
    return xc * lax.rsqrt(var + LN_EPS)


def _ln_fwd(r, g, b, name):
    S, D = r.shape

    def body(r_ref, g_ref, b_ref, h_ref):
        h_ref[...] = _ln_stats(r_ref[...]) * g_ref[...] + b_ref[...]

    row = pl.BlockSpec((LN_ROWS, D), lambda i: (i, 0))
    vec = pl.BlockSpec((1, D), lambda i: (0, 0))
    return pl.pallas_call(
        body, name=name, grid=(S // LN_ROWS,), in_specs=[row, vec, vec], out_specs=row,
        out_shape=jax.ShapeDtypeStruct((S, D), F32),
        compiler_params=_cparams(("parallel",)),
    )(r, g, b)


def _ln_bwd_core(xhat, dy, g):
    dxh = dy * g
    m1 = jnp.mean(dxh, axis=-1, keepdims=True)
    m2 = jnp.mean(dxh * xhat, axis=-1, keepdims=True)
    return dxh - m1 - xhat * m2


def _ln_bwd(r, dy, g, name):
    S, D = r.shape

    def body(r_ref, dy_ref, g_ref, dr_ref, gg_ref, gb_ref):
        x = r_ref[...]
        mu = jnp.mean(x, axis=-1, keepdims=True)
        xc = x - mu
        rstd = lax.rsqrt(jnp.mean(xc * xc, axis=-1, keepdims=True) + LN_EPS)
        xhat = xc * rstd
        dy = dy_ref[...]
        dr_ref[...] = rstd * _ln_bwd_core(xhat, dy, g_ref[...])

        @pl.when(pl.program_id(0) == 0)
        def _():
            gg_ref[...] = jnp.zeros_like(gg_ref)
            gb_ref[...] = jnp.zeros_like(gb_ref)

        gg_ref[...] += jnp.sum(dy * xhat, axis=0, keepdims=True)
        gb_ref[...] += jnp.sum(dy, axis=0, keepdims=True)

    row = pl.BlockSpec((LN_ROWS, D), lambda i: (i, 0))
    vec = pl.BlockSpec((1, D), lambda i: (0, 0))
    return pl.pallas_call(
        body, name=name, grid=(S // LN_ROWS,), in_specs=[row, row, vec], out_specs=[row, vec, vec],
        out_shape=[jax.ShapeDtypeStruct((S, D), F32), jax.ShapeDtypeStruct((1, D), F32),
                   jax.ShapeDtypeStruct((1, D), F32)],
        compiler_params=_cparams(("arbitrary",)),
    )(r, dy, g)


def _ln_loss_bwd(r, target, g, b, name):
    S, D = r.shape

    def body(r_ref, t_ref, g_ref, b_ref, dr_ref, loss_ref, gg_ref, gb_ref):
        x = r_ref[...]
        mu = jnp.mean(x, axis=-1, keepdims=True)
        xc = x - mu
        rstd = lax.rsqrt(jnp.mean(xc * xc, axis=-1, keepdims=True) + LN_EPS)
        xhat = xc * rstd
        y = xhat * g_ref[...] + b_ref[...]
        err = y - t_ref[...]
        dy = err * (1.0 / D)
        dr_ref[...] = rstd * _ln_bwd_core(xhat, dy, g_ref[...])

        @pl.when(pl.program_id(0) == 0)
        def _():
            loss_ref[...] = jnp.zeros_like(loss_ref)
            gg_ref[...] = jnp.zeros_like(gg_ref)
            gb_ref[...] = jnp.zeros_like(gb_ref)

        per_row = jnp.sum(err * err, axis=-1, keepdims=True) * (0.5 / D)
        loss_ref[...] += jnp.broadcast_to(jnp.sum(per_row, axis=0, keepdims=True), loss_ref.shape)
        gg_ref[...] += jnp.sum(dy * xhat, axis=0, keepdims=True)
        gb_ref[...] += jnp.sum(dy, axis=0, keepdims=True)

    row = pl.BlockSpec((LN_ROWS, D), lambda i: (i, 0))
    vec = pl.BlockSpec((1, D), lambda i: (0, 0))
    lvec = pl.BlockSpec((1, LANE), lambda i: (0, 0))
    return pl.pallas_call(
        body, name=name, grid=(S // LN_ROWS,), in_specs=[row, row, vec, vec],
        out_specs=[row, lvec, vec, vec],
        out_shape=[jax.ShapeDtypeStruct((S, D), F32), jax.ShapeDtypeStruct((1, LANE), F32),
                   jax.ShapeDtypeStruct((1, D), F32), jax.ShapeDtypeStruct((1, D), F32)],
        compiler_params=_cparams(("arbitrary",)),
    )(r, target, g, b)


CONV_COLS = 256
CONV_ROWS = 256
HALO = 8
INV_SQRT2 = 1.0 / math.sqrt(2.0)
INV_SQRT2PI = 1.0 / math.sqrt(2.0 * math.pi)


def _gelu(x):
    return 0.5 * x * (1.0 + lax.erf(x * INV_SQRT2))


def _gelu_and_grad(x):
    cdf = 0.5 * (1.0 + lax.erf(x * INV_SQRT2))
    return x * cdf, cdf + x * jnp.exp(-0.5 * x * x) * INV_SQRT2PI


def _conv_rows(ext, w_ref, b_ref, n):
    total = ext.shape[0]
    s1 = pltpu.roll(ext, 1, 0)
    s2 = pltpu.roll(ext, 2, 0)
    u = w_ref[2:3, :] * ext + w_ref[1:2, :] * s1 + w_ref[0:1, :] * s2 + b_ref[...]
    return u[HALO:total], s1[HALO:total], s2[HALO:total]


def _conv_gelu_fwd(u0, conv_w, conv_b, name):
    S, C2 = u0.shape
    F = C2 // 2
    ncb = F // CONV_COLS
    nrc = S // CONV_ROWS

    def body(ua_ref, uc_ref, wa_ref, wc_ref, ba_ref, bc_ref, p_ref):
        def chunk(ci, _):
            r0 = pl.multiple_of(ci * CONV_ROWS, CONV_ROWS)
            p0 = pl.multiple_of(jnp.maximum(r0 - HALO, 0), HALO)
            keep = (ci > 0).astype(F32)

            def load(ref):
                prev = ref[pl.ds(p0, HALO), :] * keep
                return jnp.concatenate([prev, ref[pl.ds(r0, CONV_ROWS), :]], axis=0)

            a, _, _ = _conv_rows(load(ua_ref), wa_ref, ba_ref, CONV_ROWS)
            c, _, _ = _conv_rows(load(uc_ref), wc_ref, bc_ref, CONV_ROWS)
            p_ref[pl.ds(r0, CONV_ROWS), :] = (_gelu(a) * c).astype(p_ref.dtype)
            return 0

        lax.fori_loop(0, nrc, chunk, 0)

    col_a = pl.BlockSpec((S, CONV_COLS), lambda j: (0, j))
    col_c = pl.BlockSpec((S, CONV_COLS), lambda j: (0, j + ncb))
    w_a = pl.BlockSpec((CONV_WIDTH, CONV_COLS), lambda j: (0, j))
    w_c = pl.BlockSpec((CONV_WIDTH, CONV_COLS), lambda j: (0, j + ncb))
    b_a = pl.BlockSpec((1, CONV_COLS), lambda j: (0, j))
    b_c = pl.BlockSpec((1, CONV_COLS), lambda j: (0, j + ncb))
    return pl.pallas_call(
        body, name=name, grid=(ncb,), in_specs=[col_a, col_c, w_a, w_c, b_a, b_c], out_specs=col_a,
        out_shape=jax.ShapeDtypeStruct((S, F), MXU_DTYPE),
        compiler_params=_cparams(("parallel",)),
    )(u0, u0, conv_w, conv_w, conv_b, conv_b)


def _conv_gelu_bwd(u0, dp, conv_w, conv_b, name):
    S, C2 = u0.shape
    F = C2 // 2
    ncb = F // CONV_COLS
    nrc = S // CONV_ROWS
    EXT = CONV_ROWS + HALO

    def body(ua_ref, uc_ref, dp_ref, wa_ref, wc_ref, ba_ref, bc_ref,
             da_ref, dc_ref, gwa_ref, gwc_ref, gba_ref, gbc_ref):
        gwa_ref[...] = jnp.zeros_like(gwa_ref)
        gwc_ref[...] = jnp.zeros_like(gwc_ref)
        gba_ref[...] = jnp.zeros_like(gba_ref)
        gbc_ref[...] = jnp.zeros_like(gbc_ref)
        rid = lax.broadcasted_iota(jnp.int32, (EXT, CONV_COLS), 0)

        def chunk(ci, _):
            r0 = pl.multiple_of(ci * CONV_ROWS, CONV_ROWS)
            p0 = pl.multiple_of(jnp.maximum(r0 - HALO, 0), HALO)
            n0 = pl.multiple_of(jnp.minimum(r0 + CONV_ROWS, S - HALO), HALO)
            keep_prev = (ci > 0).astype(F32)
            keep_next = (ci < nrc - 1).astype(F32)

            def load(ref):
                return jnp.concatenate([ref[pl.ds(p0, HALO), :] * keep_prev,
                                        ref[pl.ds(r0, CONV_ROWS), :],
                                        ref[pl.ds(n0, HALO), :] * keep_next], axis=0)

            ext_a = load(ua_ref)
            ext_c = load(uc_ref)
            a, a1, a2 = _conv_rows(ext_a, wa_ref, ba_ref, EXT)
            c, c1, c2 = _conv_rows(ext_c, wc_ref, bc_ref, EXT)
            a0 = ext_a[HALO:HALO + EXT]
            c0 = ext_c[HALO:HALO + EXT]
            dpe = jnp.concatenate([dp_ref[pl.ds(r0, CONV_ROWS), :].astype(F32),
                                   dp_ref[pl.ds(n0, HALO), :].astype(F32) * keep_next], axis=0)
            gelu_a, slope_a = _gelu_and_grad(a)
            d_a = dpe * c * slope_a
            d_c = dpe * gelu_a
            own = rid < CONV_ROWS

            def back(d_u, w_ref, x0, x1, x2, d_ref, gw_ref, gb_ref):
                d_u0 = (w_ref[2:3, :] * d_u + w_ref[1:2, :] * pltpu.roll(d_u, EXT - 1, 0)
                        + w_ref[0:1, :] * pltpu.roll(d_u, EXT - 2, 0))
                d_ref[pl.ds(r0, CONV_ROWS), :] = d_u0[0:CONV_ROWS].astype(d_ref.dtype)
                d_own = jnp.where(own, d_u, 0.0)
                gw_ref[...] += jnp.concatenate(
                    [jnp.sum(d_own * x2, axis=0, keepdims=True),
                     jnp.sum(d_own * x1, axis=0, keepdims=True),
                     jnp.sum(d_own * x0, axis=0, keepdims=True)], axis=0)
                gb_ref[...] += jnp.sum(d_own, axis=0, keepdims=True)

            back(d_a, wa_ref, a0, a1, a2, da_ref, gwa_ref, gba_ref)
            back(d_c, wc_ref, c0, c1, c2, dc_ref, gwc_ref, gbc_ref)
            return 0

        lax.fori_loop(0, nrc, chunk, 0)

    col_a = pl.BlockSpec((S, CONV_COLS), lambda j: (0, j))
    col_c = pl.BlockSpec((S, CONV_COLS), lambda j: (0, j + ncb))
    w_a = pl.BlockSpec((CONV_WIDTH, CONV_COLS), lambda j: (0, j))
    w_c = pl.BlockSpec((CONV_WIDTH, CONV_COLS), lambda j: (0, j + ncb))
    b_a = pl.BlockSpec((1, CONV_COLS), lambda j: (0, j))
    b_c = pl.BlockSpec((1, CONV_COLS), lambda j: (0, j + ncb))
    outs = pl.pallas_call(
        body, name=name, grid=(ncb,),
        in_specs=[col_a, col_c, col_a, w_a, w_c, b_a, b_c],
        out_specs=[col_a, col_a, w_a, w_a, b_a, b_a],
        out_shape=[jax.ShapeDtypeStruct((S, F), MXU_DTYPE), jax.ShapeDtypeStruct((S, F), MXU_DTYPE),
                   jax.ShapeDtypeStruct((CONV_WIDTH, F), F32), jax.ShapeDtypeStruct((CONV_WIDTH, F), F32),
                   jax.ShapeDtypeStruct((1, F), F32), jax.ShapeDtypeStruct((1, F), F32)],
        compiler_params=_cparams(("parallel",)),
    )(u0, u0, dp, conv_w, conv_w, conv_b, conv_b)
    da, dc, gwa, gwc, gba, gbc = outs
    return (jnp.concatenate([da, dc], axis=1), jnp.concatenate([gwa, gwc], axis=1),
            jnp.concatenate([gba, gbc], axis=1))


SB_TK = 128
SB_TQ_FWD = 1024
SB_TQ_BWD = 1024
SB_PAIRS_FWD = 4


def _softplus(z):
    return jnp.maximum(z, 0.0) + jnp.log(1.0 + jnp.exp(-jnp.abs(z)))


def _tri_ones(after):
    r = lax.broadcasted_iota(jnp.int32, (SB_TK, 2 * SB_TK), 0)
    c = lax.broadcasted_iota(jnp.int32, (SB_TK, 2 * SB_TK), 1)
    tri = (r > c) if after else (r < c)
    return jnp.where(c >= SB_TK, 1.0, jnp.where(tri, 1.0, 0.0)).astype(BF16)


def _sb_block_specs(S, TQ):
    NP = SB_WIDTH // LANE
    return [pl.BlockSpec((TQ, LANE), lambda p, i: (i, p)),
            pl.BlockSpec((S, LANE), lambda p, i: (0, NP + p)),
            pl.BlockSpec((S, LANE), lambda p, i: (0, 2 * NP + p))]


def _sb_fwd(qkv, name, ride=()):
    S = qkv.shape[0]
    TK = SB_TK
    TQ = min(SB_TQ_FWD, S)
    R = TQ // TK
    NP = SB_WIDTH // LANE
    PS = SB_PAIRS_FWD
    W = PS * LANE
    NH = 2 * PS
    nq = S // TQ
    nkb = S // TK
    grid = (NP // PS, nq)

    def body(q_ref, k_ref, v_ref, o_ref, wk_ref, wbuf, wsem, bbuf, bsem):
        p, i = pl.program_id(0), pl.program_id(1)
        row = lax.broadcasted_iota(jnp.int32, (TQ, TK), 0)
        col = lax.broadcasted_iota(jnp.int32, (TQ, TK), 1)
        after_ones = _tri_ones(True)
        first = lax.broadcasted_iota(jnp.int32, (TK, LANE), 1) < SB_HEAD_DIM

        def keep(slot, j):
            return pltpu.make_async_copy(wbuf.at[slot], wk_ref.at[pl.ds(p * NH, NH), i, j], wsem.at[slot])

        def keep_band(u):
            return pltpu.make_async_copy(bbuf.at[u], wk_ref.at[pl.ds(p * NH, NH), i, R * i + u], bsem.at[u])

        def block(j, carry, r0, stage):
            k0 = pl.multiple_of(j * TK, TK)
            masked = r0 is not None
            r0 = r0 or 0
            out = []
            for pr in range(PS):
                lanes = slice(pr * LANE, (pr + 1) * LANE)
                acc0, tail_a, tail_b = carry[3 * pr:3 * pr + 3]
                kab = k_ref[pl.ds(k0, TK), lanes]
                vab = v_ref[pl.ds(k0, TK), lanes]
                none = jnp.zeros_like(kab)
                k2 = jnp.concatenate([jnp.where(first, kab, none), jnp.where(first, none, kab)], axis=0)
                v2 = jnp.concatenate([jnp.where(first, vab, none), jnp.where(first, none, vab)], axis=0)
                z2 = _dot(q_ref[r0:, lanes], k2, "nt")
                tails, ws = [], []
                for hh, tail in enumerate((tail_a, tail_b)):
                    z = z2[:, hh * TK:(hh + 1) * TK]
                    sp = _softplus(z)
                    if masked:
                        strict = col[r0:] < row[r0:] - r0
                        sp = jnp.where(strict, sp, 0.0)
                    cs = _dot_split(sp, after_ones, "nn")
                    w = jnp.exp(z - sp - cs[:, :TK] - tail[r0:])
                    if masked:
                        w = jnp.where(strict, w, 0.0)
                    ws.append(w.astype(MXU_DTYPE))
                    stage[2 * pr + hh, r0:, :] = ws[-1]
                    tot = cs[:, TK:]
                    if r0:
                        tot = jnp.concatenate([jnp.zeros((r0, TK), F32), tot], axis=0)
                    tails.append(tail + tot)
                acc_r = acc0[r0:] + _dot(jnp.concatenate(ws, axis=1), v2, "nn")
                acc = jnp.concatenate([acc0[:r0], acc_r], axis=0) if r0 else acc_r
                out += [acc, tails[0], tails[1]]
            return tuple(out)

        carry = (jnp.zeros((TQ, LANE), F32), jnp.zeros((TQ, TK), F32), jnp.zeros((TQ, TK), F32)) * PS
        for u in reversed(range(R)):
            carry = block(R * i + u, carry, u * TK, bbuf.at[u])
        for u in range(R):
            keep_band(u).start()
        trips = R * i

        def below(n, c):
            slot = n % 2
            j = trips - 1 - n

            @pl.when(n >= 2)
            def _():
                keep(slot, j).wait()

            c = block(j, c, None, wbuf.at[slot])
            keep(slot, j).start()
            return c

        carry = lax.fori_loop(0, trips, below, carry)
        for back in (1, 2):

            @pl.when(trips >= back)
            def _():
                keep((trips - back) % 2, 0).wait()

        for u in range(R):
            keep_band(u).wait()
        for pr in range(PS):
            o_ref[:, pr * LANE:(pr + 1) * LANE] = carry[3 * pr].astype(o_ref.dtype)

    in_specs = [pl.BlockSpec((TQ, W), lambda p, i: (i, p)),
                pl.BlockSpec((S, W), lambda p, i: (0, NP // PS + p)),
                pl.BlockSpec((S, W), lambda p, i: (0, 2 * (NP // PS) + p))]
    ospec = pl.BlockSpec((TQ, W), lambda p, i: (i, p))
    x_args, x_in, x_out, x_shapes, x_sems, wrap = _riding_exchange(list(ride), grid)
    return pl.pallas_call(
        wrap(body, 3, 2) if ride else body, name=name, grid=grid,
        in_specs=in_specs + (x_in if ride else []),
        out_specs=[ospec, pl.BlockSpec(memory_space=pl.ANY)] + (x_out if ride else []),
        out_shape=[jax.ShapeDtypeStruct((S, SB_WIDTH), MXU_DTYPE),
                   jax.ShapeDtypeStruct((SB_HEADS, nq, nkb, TQ, TK), MXU_DTYPE)] + (x_shapes if ride else []),
        scratch_shapes=[pltpu.VMEM((2, NH, TQ, TK), MXU_DTYPE), pltpu.SemaphoreType.DMA((2,)),
                        pltpu.VMEM((R, NH, TQ, TK), MXU_DTYPE), pltpu.SemaphoreType.DMA((R,))]
        + (x_sems if ride else []),
        compiler_params=_cparams(("arbitrary", "arbitrary"), has_side_effects=True),
    )(qkv, qkv, qkv, *(x_args if ride else []))


def _sb_bwd(qkv, d_cat, w_kept, name, ride=()):
    S = qkv.shape[0]
    TK = SB_TK
    TQ = min(SB_TQ_BWD, S)
    assert TQ == min(SB_TQ_FWD, S)
    R = TQ // TK
    NP = SB_WIDTH // LANE
    grid = (NP, S // TQ)
    scale = SB_HEAD_DIM ** -0.5

    def body(q_ref, k_ref, v_ref, do_ref, wk_ref, dq_ref, dk_ref, dv_ref, dk_acc, dv_acc, wbuf, wsem, bbuf, bsem):
        p, i = pl.program_id(0), pl.program_id(1)

        @pl.when(i == 0)
        def _():
            dk_acc[...] = jnp.zeros_like(dk_acc)
            dv_acc[...] = jnp.zeros_like(dv_acc)

        row = lax.broadcasted_iota(jnp.int32, (TQ, TK), 0)
        col = lax.broadcasted_iota(jnp.int32, (TQ, TK), 1)
        before_ones = _tri_ones(False)
        first = lax.broadcasted_iota(jnp.int32, (TK, LANE), 1) < SB_HEAD_DIM
        qab = q_ref[...]
        doab = do_ref[...].astype(MXU_DTYPE)
        qdo = jnp.concatenate([qab, doab], axis=1)

        def fetch(slot, j):
            return pltpu.make_async_copy(wk_ref.at[pl.ds(2 * p, 2), i, j], wbuf.at[slot], wsem.at[slot])

        def fetch_band(u):
            return pltpu.make_async_copy(wk_ref.at[pl.ds(2 * p, 2), i, R * i + u], bbuf.at[u], bsem.at[u])

        def block(j, carry, r0, stage):
            k0 = pl.multiple_of(j * TK, TK)
            masked = r0 is not None
            r0 = r0 or 0
            kab = k_ref[pl.ds(k0, TK), :]
            vab = v_ref[pl.ds(k0, TK), :]
            none = jnp.zeros_like(kab)
            k_h = [jnp.where(first, kab, none), jnp.where(first, none, kab)]
            v_h = [jnp.where(first, vab, none), jnp.where(first, none, vab)]
            sums, dzs, ws = [], [], []
            for hh in range(2):
                gsum = carry[1 + hh]
                kv = jnp.concatenate([jnp.concatenate([k_h[hh], none], axis=1),
                                      jnp.concatenate([none, v_h[hh]], axis=1)], axis=0)
                zdw = _dot(qdo[r0:], kv, "nt")
                z = zdw[:, :TK]
                sig = 1.0 / (1.0 + jnp.exp(-z))
                wb = stage[hh, r0:, :]
                g = wb.astype(F32) * zdw[:, TK:]
                cg = _dot_split(g, before_ones, "nn")
                dz = g - sig * (g + cg[:, :TK] + gsum[r0:])
                if masked:
                    dz = jnp.where(col[r0:] < row[r0:] - r0, dz, 0.0)
                dzs.append(dz.astype(MXU_DTYPE))
                ws.append(wb)
                gsum_r = gsum[r0:] + cg[:, TK:]
                if r0:
                    gsum_r = jnp.concatenate([gsum[:r0], gsum_r], axis=0)
                sums.append(gsum_r)
            kvg = _dot(jnp.concatenate(dzs + ws, axis=1), qdo[r0:], "tn")
            dk_acc[pl.ds(k0, TK), :] += jnp.where(first, kvg[:TK, :LANE], kvg[TK:2 * TK, :LANE])
            dv_acc[pl.ds(k0, TK), :] += jnp.where(first, kvg[2 * TK:3 * TK, LANE:], kvg[3 * TK:, LANE:])
            dq_r = carry[0][r0:] + _dot(jnp.concatenate(dzs, axis=1), jnp.concatenate(k_h, axis=0), "nn")
            dq = jnp.concatenate([carry[0][:r0], dq_r], axis=0) if r0 else dq_r
            return (dq, *sums)

        trips = R * i
        for u in range(R):
            fetch_band(u).start()

        @pl.when(trips >= 1)
        def _():
            fetch(0, 0).start()

        def below(j, c):
            slot = j % 2

            @pl.when(j + 1 < trips)
            def _():
                fetch(1 - slot, j + 1).start()

            fetch(slot, j).wait()
            return block(j, c, None, wbuf.at[slot])

        zero = jnp.zeros((TQ, TK), F32)
        carry = lax.fori_loop(0, trips, below, (jnp.zeros((TQ, LANE), F32), zero, zero))
        for u in range(R):
            fetch_band(u).wait()
        for u in range(R):
            carry = block(trips + u, carry, u * TK, bbuf.at[u])
        dq_ref[...] = (carry[0] * scale).astype(dq_ref.dtype)

        @pl.when(i == S // TQ - 1)
        def _():
            dk_ref[...] = dk_acc[...].astype(dk_ref.dtype)
            dv_ref[...] = dv_acc[...].astype(dv_ref.dtype)

    qspec = pl.BlockSpec((TQ, LANE), lambda p, i: (i, p))
    full = pl.BlockSpec((S, LANE), lambda p, i: (0, p))
    sds = jax.ShapeDtypeStruct((S, SB_WIDTH), MXU_DTYPE)
    scratch = [pltpu.VMEM((S, LANE), F32), pltpu.VMEM((S, LANE), F32),
               pltpu.VMEM((2, 2, TQ, TK), MXU_DTYPE), pltpu.SemaphoreType.DMA((2,)),
               pltpu.VMEM((R, 2, TQ, TK), MXU_DTYPE), pltpu.SemaphoreType.DMA((R,))]
    x_args, x_in, x_out, x_shapes, x_sems, wrap = _riding_exchange(list(ride), grid)
    return pl.pallas_call(
        wrap(body, 5, 3) if ride else body, name=name, grid=grid,
        in_specs=_sb_block_specs(S, TQ) + [qspec, pl.BlockSpec(memory_space=pl.ANY)] + (x_in if ride else []),
        out_specs=[qspec, full, full] + (x_out if ride else []),
        out_shape=[sds, sds, sds] + (x_shapes if ride else []),
        scratch_shapes=scratch + (x_sems if ride else []),
        compiler_params=_cparams(("arbitrary", "arbitrary"), has_side_effects=bool(ride)),
    )(qkv, qkv, qkv, d_cat, w_kept, *(x_args if ride else []))


GLA_ROWS = 1024
GLA_GROUP_FWD = 16
GLA_GROUP_BWD = 8
GLA_PAIR_K = 2 * GLA_KEY_DIM
GLA_PAIR_V = 2 * GLA_VAL_DIM


def _log_sigmoid(x):
    return -_softplus(-x)


def _dot_split_lhs01(m01, x):
    hi = x.astype(BF16)
    lo = (x - hi.astype(F32)).astype(BF16)
    return lax.dot_general(jnp.concatenate([m01, m01], axis=1), jnp.concatenate([hi, lo], axis=0),
                           (((1,), (0,)), ((), ())), preferred_element_type=F32)


def _dot_split_tn(x, m01):
    hi = x.astype(BF16)
    lo = (x - hi.astype(F32)).astype(BF16)
    return lax.dot_general(jnp.concatenate([hi, lo], axis=0), jnp.concatenate([m01, m01], axis=0),
                           (((0,), (0,)), ((), ())), preferred_element_type=F32)


def _dot_split_nt01(m01, x):
    hi = x.astype(BF16)
    lo = (x - hi.astype(F32)).astype(BF16)
    return lax.dot_general(jnp.concatenate([m01, m01], axis=1), jnp.concatenate([hi, lo], axis=1),
                           (((1,), (1,)), ((), ())), preferred_element_type=F32)


def _gla_consts():
    C = CHUNK
    row = lax.broadcasted_iota(jnp.int32, (C, C), 0)
    col = lax.broadcasted_iota(jnp.int32, (C, C), 1)
    first = lax.broadcasted_iota(jnp.int32, (C, GLA_PAIR_K), 1) < GLA_KEY_DIM
    r = lax.broadcasted_iota(jnp.int32, (GLA_PAIR_K, GLA_PAIR_V), 0)
    c = lax.broadcasted_iota(jnp.int32, (GLA_PAIR_K, GLA_PAIR_V), 1)
    own = (r < GLA_KEY_DIM) == (c < GLA_VAL_DIM)
    rowk = lax.broadcasted_iota(jnp.int32, (C, GLA_PAIR_K), 0)
    return dict(row=row, col=col, first=first, own=own, rowk=rowk,
                incl=jnp.where(row >= col, 1.0, 0.0).astype(BF16),
                ones=jnp.ones((C, GLA_PAIR_V), BF16))


def _pack_state(state):
    top = lax.broadcasted_iota(jnp.int32, (GLA_PAIR_K, GLA_VAL_DIM), 0) < GLA_KEY_DIM
    return jnp.where(top, state[:, :GLA_VAL_DIM], state[:, GLA_VAL_DIM:])


def _unpack_state(packed):
    top = lax.broadcasted_iota(jnp.int32, (GLA_PAIR_K, GLA_VAL_DIM), 0) < GLA_KEY_DIM
    return jnp.concatenate([jnp.where(top, packed, 0.0), jnp.where(top, 0.0, packed)], axis=1)


def _gla_chunk(qc, kc, vc, gate, k):
    C = CHUNK
    row, col, first, rowk = k["row"], k["col"], k["first"], k["rowk"]
    la = _log_sigmoid(gate) * (1.0 / GLA_TAU)
    b = _dot_split_lhs01(k["incl"], la)
    b_ref = jnp.sum(jnp.where(rowk == C // 2 - 1, b, 0.0), axis=0, keepdims=True)
    b_last = jnp.sum(la, axis=0, keepdims=True)
    qs = qc * (GLA_KEY_DIM ** -0.5)
    q_in = qs * jnp.exp(b - b_ref)
    k_in = kc * jnp.exp(b_ref - b)
    k_dec = kc * jnp.exp(b_last - b)
    q_b = qs * jnp.exp(b)
    k_in_h = [jnp.where(first, k_in, 0.0), jnp.where(first, 0.0, k_in)]
    v_h = [vc[:, :GLA_VAL_DIM], vc[:, GLA_VAL_DIM:]]
    sc = [jnp.where(row >= col, _dot(q_in, k_in_h[hh], "nt"), 0.0) for hh in range(2)]
    o_intra = jnp.concatenate([_dot(sc[hh], v_h[hh], "nn") for hh in range(2)], axis=1)
    upd = jnp.where(k["own"], _dot(k_dec, vc, "tn"), 0.0)
    dec_col = jnp.exp(_dot_split_tn(la, k["ones"]))
    return dict(la=la, b=b, b_ref=b_ref, b_last=b_last, q_in=q_in, k_in=k_in, k_dec=k_dec, q_b=q_b,
                k_in_h=k_in_h, v_h=v_h, sc=sc, o_intra=o_intra, upd=upd, dec_col=dec_col)


def _rms_gate(o, gg):
    rinv = lax.rsqrt(jnp.mean(o * o, axis=-1, keepdims=True) + RMS_EPS)
    o_n = o * rinv
    sg = 1.0 / (1.0 + jnp.exp(-gg))
    return o_n, rinv, sg


def _gla_in_specs(rows_of, RB):
    PK, PV = GLA_PAIR_K, GLA_PAIR_V
    return [pl.BlockSpec((RB, PK), lambda i, p: (rows_of(i), p)),
            pl.BlockSpec((RB, PK), lambda i, p: (rows_of(i), (OFF_GK - OFF_GQ) // PK + p)),
            pl.BlockSpec((RB, PV), lambda i, p: (rows_of(i), (OFF_GV - OFF_GQ) // PV + p)),
            pl.BlockSpec((RB, PV), lambda i, p: (rows_of(i), (OFF_GG - OFF_GQ) // PV + p)),
            pl.BlockSpec((RB, LANE), lambda i, p: (rows_of(i), (OFF_GA - OFF_GQ) // LANE)),
            pl.BlockSpec((1, LANE, PK), lambda i, p: (p, 0, 0)),
            pl.BlockSpec((1, 1, PK), lambda i, p: (p, 0, 0)),
            pl.BlockSpec((1, GLA_VAL_DIM), lambda i, p: (0, 0))]


def _gla_fwd(proj, gate_up_p, gate_bias_p, gnorm, name):
    S = proj.shape[0]
    C, RB, PK, PV, dv = CHUNK, min(GLA_ROWS, S), GLA_PAIR_K, GLA_PAIR_V, GLA_VAL_DIM
    NP = GLA_HEADS // 2
    G = min(GLA_GROUP_FWD, RB // C)
    nchunk = S // C
    ngroup = RB // (C * G)

    def body(q_ref, k_ref, v_ref, gg_ref, ga_ref, gu_ref, gb_ref, gn_ref, o_ref, prev_ref, st_ref):
        i, p = pl.program_id(0), pl.program_id(1)
        k = _gla_consts()

        @pl.when(i == 0)
        def _():
            st_ref[p] = jnp.zeros((PK, PV), F32)

        def group(gi, state):
            for u in range(G):
                ci = gi * G + u
                rows = pl.ds(pl.multiple_of(ci * C, C), C)
                gate = _dot(ga_ref[rows, :], gu_ref[0], "nn") + gb_ref[0]
                f = _gla_chunk(q_ref[rows, :], k_ref[rows, :], v_ref[rows, :], gate, k)
                prev_ref[0, ci] = _pack_state(state)
                o = f["o_intra"] + _dot(f["q_b"], state, "nn")
                state = f["dec_col"] * state + f["upd"]
                ggv = gg_ref[rows, :]
                halves = []
                for hh in range(2):
                    lanes = slice(hh * dv, (hh + 1) * dv)
                    o_n, _, sg = _rms_gate(o[:, lanes], ggv[:, lanes])
                    halves.append(o_n * gn_ref[...] * (ggv[:, lanes] * sg))
                o_ref[rows, :] = jnp.concatenate(halves, axis=1).astype(o_ref.dtype)
            return state

        st_ref[p] = lax.fori_loop(0, ngroup, group, st_ref[p])

    return pl.pallas_call(
        body, name=name, grid=(S // RB, NP), in_specs=_gla_in_specs(lambda i: i, RB),
        out_specs=[pl.BlockSpec((RB, PV), lambda i, p: (i, p)),
                   pl.BlockSpec((1, RB // C, PK, dv), lambda i, p: (p, i, 0, 0))],
        out_shape=[jax.ShapeDtypeStruct((S, GLA_WIDTH), MXU_DTYPE),
                   jax.ShapeDtypeStruct((NP, nchunk, PK, dv), F32)],
        scratch_shapes=[pltpu.VMEM((NP, PK, PV), F32)],
        compiler_params=_cparams(("arbitrary", "arbitrary")),
    )(proj, proj, proj, proj, proj, gate_up_p, gate_bias_p, gnorm)


def _gla_bwd(proj, gate_up_p, gate_bias_p, gnorm, prev, d_cat, name):
    S = proj.shape[0]
    C, RB, PK, PV, dv = CHUNK, min(GLA_ROWS, S), GLA_PAIR_K, GLA_PAIR_V, GLA_VAL_DIM
    NP = GLA_HEADS // 2
    G = min(GLA_GROUP_BWD, RB // C)
    nb = S // RB
    ngroup = RB // (C * G)

    def body(q_ref, k_ref, v_ref, gg_ref, ga_ref, gu_ref, gb_ref, gn_ref, prev_ref, do_ref,
             dq_ref, dk_ref, dv_ref, dgg_ref, dga_ref, ggu_ref, ggb_ref, ggn_ref, st_ref):
        i, p = pl.program_id(0), pl.program_id(1)
        k = _gla_consts()
        row, col, first, rowk = k["row"], k["col"], k["first"], k["rowk"]
        gu = gu_ref[0]

        @pl.when(i == 0)
        def _():
            st_ref[p] = jnp.zeros((PK, PV), F32)
            ggu_ref[p] = jnp.zeros((LANE, PK), F32)
            ggb_ref[p] = jnp.zeros((1, PK), F32)

        @pl.when(jnp.logical_and(i == 0, p == 0))
        def _():
            ggn_ref[...] = jnp.zeros_like(ggn_ref)

        @pl.when(p == 0)
        def _():
            dga_ref[...] = jnp.zeros_like(dga_ref)

        upper_incl = jnp.where(col >= row, 1.0, 0.0).astype(BF16)
        ones_8 = jnp.ones((8, PV), BF16)

        def group(gn, dstate):
            gi = ngroup - 1 - gn
            for u in reversed(range(G)):
                ci = gi * G + u
                rows = pl.ds(pl.multiple_of(ci * C, C), C)
                ga = ga_ref[rows, :]
                gate = _dot(ga, gu, "nn") + gb_ref[0]
                vc = v_ref[rows, :]
                f = _gla_chunk(q_ref[rows, :], k_ref[rows, :], vc, gate, k)
                state = _unpack_state(prev_ref[0, ci])
                o = f["o_intra"] + _dot(f["q_b"], state, "nn")
                ggv = gg_ref[rows, :]
                dout = do_ref[rows, :]
                d_o_h, dgg_h = [], []
                for hh in range(2):
                    lanes = slice(hh * dv, (hh + 1) * dv)
                    o_n, rinv, sg = _rms_gate(o[:, lanes], ggv[:, lanes])
                    silu = ggv[:, lanes] * sg
                    dgg_h.append(dout[:, lanes] * o_n * gn_ref[...] * (sg * (1.0 + ggv[:, lanes] * (1.0 - sg))))
                    d_ong = dout[:, lanes] * silu
                    ggn_ref[...] += jnp.sum(d_ong * o_n, axis=0, keepdims=True)
                    d_on = d_ong * gn_ref[...]
                    d_o_h.append(rinv * (d_on - o_n * jnp.mean(d_on * o_n, axis=-1, keepdims=True)))
                dgg_ref[rows, :] = jnp.concatenate(dgg_h, axis=1).astype(dgg_ref.dtype)
                d_o = jnp.concatenate(d_o_h, axis=1)
                d_upd = jnp.where(k["own"], dstate, 0.0)
                d_dec_col = dstate * state * f["dec_col"]
                dstate = f["dec_col"] * dstate + _dot(f["q_b"], d_o, "tn")
                dsc = [jnp.where(row >= col, _dot(d_o_h[hh], f["v_h"][hh], "nt"), 0.0) for hh in range(2)]
                dv_ref[rows, :] = (jnp.concatenate([_dot(f["sc"][hh], d_o_h[hh], "tn") for hh in range(2)], axis=1)
                                   + _dot(f["k_dec"], d_upd, "nn")).astype(dv_ref.dtype)
                q_in_h = [jnp.where(first, f["q_in"], 0.0), jnp.where(first, 0.0, f["q_in"])]
                dq_in = _dot(dsc[0], f["k_in_h"][0], "nn") + _dot(dsc[1], f["k_in_h"][1], "nn")
                dk_in = _dot(dsc[0], q_in_h[0], "tn") + _dot(dsc[1], q_in_h[1], "tn")
                dq_b = _dot(d_o, state, "nt")
                dkdec = _dot(vc, d_upd, "nt")
                b = f["b"]
                e1 = jnp.exp(b - f["b_ref"])
                e2 = jnp.exp(f["b_ref"] - b)
                e3 = jnp.exp(f["b_last"] - b)
                eb = jnp.exp(b)
                dq_ref[rows, :] = ((dq_in * e1 + dq_b * eb) * (GLA_KEY_DIM ** -0.5)).astype(dq_ref.dtype)
                dk_ref[rows, :] = (dk_in * e2 + dkdec * e3).astype(dk_ref.dtype)
                t_q = dq_in * f["q_in"]
                t_k = dk_in * f["k_in"]
                t_d = dkdec * f["k_dec"]
                db = t_q - t_k - t_d + dq_b * f["q_b"]
                db_ref = jnp.sum(t_k - t_q, axis=0, keepdims=True)
                db_last = (jnp.sum(t_d, axis=0, keepdims=True)
                           + jnp.max(_dot_split_nt01(ones_8, d_dec_col), axis=0, keepdims=True))
                db = db + jnp.where(rowk == C // 2 - 1, db_ref, 0.0) + jnp.where(rowk == C - 1, db_last, 0.0)
                dla = _dot_split_lhs01(upper_incl, db)
                d_gate = dla * (1.0 / GLA_TAU) * (1.0 / (1.0 + jnp.exp(gate)))
                ggb_ref[p] += jnp.sum(d_gate, axis=0, keepdims=True)
                ggu_ref[p] += _dot(ga, d_gate, "tn")
                dga_ref[rows, :] += _dot(d_gate, gu, "nt")
            return dstate

        st_ref[p] = lax.fori_loop(0, ngroup, group, st_ref[p])

    back = lambda i: nb - 1 - i
    NPV = SB_WIDTH // PV
    return pl.pallas_call(
        body, name=name, grid=(nb, NP),
        in_specs=_gla_in_specs(back, RB) + [pl.BlockSpec((1, RB // C, PK, dv), lambda i, p: (p, back(i), 0, 0)),
                                            pl.BlockSpec((RB, PV), lambda i, p: (back(i), NPV + p))],
        out_specs=[pl.BlockSpec((RB, PK), lambda i, p: (back(i), p)),
                   pl.BlockSpec((RB, PK), lambda i, p: (back(i), p)),
                   pl.BlockSpec((RB, PV), lambda i, p: (back(i), p)),
                   pl.BlockSpec((RB, PV), lambda i, p: (back(i), p)),
                   pl.BlockSpec((RB, LANE), lambda i, p: (back(i), 0)),
                   pl.BlockSpec((NP, LANE, PK), lambda i, p: (0, 0, 0)),
                   pl.BlockSpec((NP, 1, PK), lambda i, p: (0, 0, 0)),
                   pl.BlockSpec((1, dv), lambda i, p: (0, 0))],
        out_shape=[jax.ShapeDtypeStruct((S, NP * PK), MXU_DTYPE), jax.ShapeDtypeStruct((S, NP * PK), MXU_DTYPE),
                   jax.ShapeDtypeStruct((S, GLA_WIDTH), MXU_DTYPE), jax.ShapeDtypeStruct((S, GLA_WIDTH), MXU_DTYPE),
                   jax.ShapeDtypeStruct((S, LANE), F32), jax.ShapeDtypeStruct((NP, LANE, PK), F32),
                   jax.ShapeDtypeStruct((NP, 1, PK), F32), jax.ShapeDtypeStruct((1, dv), F32)],
        scratch_shapes=[pltpu.VMEM((NP, PK, PV), F32)],
        compiler_params=_cparams(("arbitrary", "arbitrary")),
    )(proj, proj, proj, proj, proj, gate_up_p, gate_bias_p, gnorm, prev, d_cat)


def _exchange_copies(scatter_flags, ins, outs, send_sems, recv_sems, local_sems):
    n_peer = N_DEV - 1
    x, y, c = lax.axis_index("x"), lax.axis_index("y"), lax.axis_index("c")
    me = 4 * x + 2 * y + c
    copies = []
    for a, scatter in enumerate(scatter_flags):
        own = ins[a].at[me] if scatter else ins[a]
        copies.append(pltpu.make_async_copy(own, outs[a].at[me], local_sems.at[a]))
    for r in range(1, N_DEV):
        px = 1 - x if r & 4 else x
        py = 1 - y if r & 2 else y
        pc = 1 - c if r & 1 else c
        for a, scatter in enumerate(scatter_flags):
            src = ins[a].at[4 * px + 2 * py + pc] if scatter else ins[a]
            copies.append(pltpu.make_async_remote_copy(
                src_ref=src, dst_ref=outs[a].at[me],
                send_sem=send_sems.at[a * n_peer + r - 1], recv_sem=recv_sems.at[a * n_peer + r - 1],
                device_id=(px, py, pc), device_id_type=MESH_ID))
    return copies


def _exchange_shapes(items):
    out_shape = []
    for arr, scatter in items:
        shp = arr.shape if scatter else (N_DEV,) + arr.shape
        out_shape.append(jax.ShapeDtypeStruct(shp, arr.dtype))
    n = len(items)
    sems = [pltpu.SemaphoreType.DMA((n * (N_DEV - 1),)), pltpu.SemaphoreType.DMA((n * (N_DEV - 1),)),
            pltpu.SemaphoreType.DMA((n,))]
    return out_shape, sems


def _exchange(items, name):
    n = len(items)
    flags = [sc for _, sc in items]

    def body(*refs):
        copies = _exchange_copies(flags, refs[:n], refs[n:2 * n], *refs[2 * n:])
        for cp in copies:
            cp.start()
        for cp in copies:
            cp.wait()

    out_shape, sems = _exchange_shapes(items)
    any_spec = pl.BlockSpec(memory_space=pl.ANY)
    return pl.pallas_call(
        body, name=name, in_specs=[any_spec] * n, out_specs=[any_spec] * n, out_shape=out_shape,
        scratch_shapes=sems, compiler_params=pltpu.CompilerParams(has_side_effects=True),
    )(*[arr for arr, _ in items])


def _all_gather_two_level(arrays, name):
    n = len(arrays)
    K = N_DEV - 1

    def body(*refs):
        ins, outs = refs[:n], refs[n:2 * n]
        send_sems, recv_sems, local_sems = refs[2 * n:]
        x, y, c = lax.axis_index("x"), lax.axis_index("y"), lax.axis_index("c")
        sibling = (x, y, 1 - c)
        chips = [(1 - x, y), (x, 1 - y), (1 - x, 1 - y)]

        def slot(px, py, pc):
            return 4 * px + 2 * py + pc

        def copy(a, k, block, to, src=None):
            rows = outs[a].at[slot(*block)]
            return pltpu.make_async_remote_copy(
                src_ref=rows if src is None else src, dst_ref=rows,
                send_sem=send_sems.at[a * K + k], recv_sem=recv_sems.at[a * K + k],
                device_id=to, device_id_type=MESH_ID)

        me = (x, y, c)
        local, first, passed = [], [], []
        for a in range(n):
            cp = pltpu.make_async_copy(ins[a], outs[a].at[slot(*me)], local_sems.at[a])
            cp.start()
            local.append(cp)
            first.append(copy(a, 0, me, sibling, src=ins[a]))
            first += [copy(a, 1 + j, me, (*chip, c), src=ins[a]) for j, chip in enumerate(chips)]
        for cp in first:
            cp.start()
        for j, chip in enumerate(chips):
            for a in range(n):
                copy(a, 1 + j, (*chip, c), me).wait_recv()
                cp = copy(a, 4 + j, (*chip, c), sibling)
                cp.start()
                passed.append(cp)
        for a in range(n):
            copy(a, 0, sibling, me).wait_recv()
            for j, chip in enumerate(chips):
                copy(a, 4 + j, (*chip, 1 - c), me).wait_recv()
        for cp in first + passed:
            cp.wait_send()
        for cp in local:
            cp.wait()

    out_shape, sems = _exchange_shapes([(arr, False) for arr in arrays])
    any_spec = pl.BlockSpec(memory_space=pl.ANY)
    return pl.pallas_call(
        body, name=name, in_specs=[any_spec] * n, out_specs=[any_spec] * n, out_shape=out_shape,
        scratch_shapes=sems, compiler_params=pltpu.CompilerParams(has_side_effects=True),
    )(*arrays)


def _riding_exchange(items, grid):
    n = len(items)
    flags = [sc for _, sc in items]
    out_shape, sems = _exchange_shapes(items)
    any_spec = pl.BlockSpec(memory_space=pl.ANY)

    def wrap(body, n_in, n_out):
        def fused(*refs):
            ins = refs[:n_in]
            x_ins = refs[n_in:n_in + n]
            outs = refs[n_in + n:n_in + n + n_out]
            x_outs = refs[n_in + n + n_out:n_in + 2 * n + n_out]
            rest = refs[n_in + 2 * n + n_out:]
            x_sems, scratch = rest[len(rest) - 3:], rest[:len(rest) - 3]
            first = last = True
            for d, n_d in enumerate(grid):
                first = jnp.logical_and(first, pl.program_id(d) == 0)
                last = jnp.logical_and(last, pl.program_id(d) == n_d - 1)

            @pl.when(first)
            def _():
                for cp in _exchange_copies(flags, x_ins, x_outs, *x_sems):
                    cp.start()

            body(*ins, *outs, *scratch)

            @pl.when(last)
            def _():
                for cp in _exchange_copies(flags, x_ins, x_outs, *x_sems):
                    cp.wait()

        return fused

    return [arr for arr, _ in items], [any_spec] * n, [any_spec] * n, out_shape, sems, wrap


def _sum_devices(ref):
    g = ref[0].astype(F32)
    for q in range(1, N_DEV):
        g = g + ref[q].astype(F32)
    return g


def _adam_math(g, w, m, v):
    nm = ADAM_B1 * m + (1.0 - ADAM_B1) * g
    nv = ADAM_B2 * v + (1.0 - ADAM_B2) * (g * g)
    m_hat = nm / (1.0 - ADAM_B1 ** ADAM_STEP)
    v_hat = nv / (1.0 - ADAM_B2 ** ADAM_STEP)
    return -ADAM_LR * (m_hat / (jnp.sqrt(v_hat) + ADAM_EPS) + ADAM_WD * w), nm, nv


def _adamw(grecv, w, m, v, name):
    R, C = w.shape
    tile = _pick(R, (256, 176, 128)) if R * C > 65536 else R

    def body(gr_ref, w_ref, m_ref, v_ref, g_ref, d_ref, nm_ref, nv_ref):
        g = _sum_devices(gr_ref)
        g_ref[...] = g
        d_ref[...], nm_ref[...], nv_ref[...] = _adam_math(g, w_ref[...], m_ref[...], v_ref[...])

    blk = pl.BlockSpec((tile, C), lambda i: (i, 0))
    sds = jax.ShapeDtypeStruct((R, C), F32)
    return pl.pallas_call(
        body, name=name, grid=(R // tile,),
        in_specs=[pl.BlockSpec((N_DEV, tile, C), lambda i: (0, i, 0)), blk, blk, blk],
        out_specs=[blk, blk, blk, blk], out_shape=[sds, sds, sds, sds],
        compiler_params=_cparams(("parallel",)),
    )(grecv, w, m, v)


def _adamw_replicated(grecvs, loss_recv, ws, ms, vs, name):
    nt = len(ws)

    def body(*refs):
        gr, lr = refs[:nt], refs[nt]
        w, m, v = refs[nt + 1:2 * nt + 1], refs[2 * nt + 1:3 * nt + 1], refs[3 * nt + 1:4 * nt + 1]
        outs = refs[4 * nt + 1:]
        outs[0][...] = _sum_devices(lr)
        for t in range(nt):
            g_ref, d_ref, nm_ref, nv_ref = outs[1 + 4 * t:5 + 4 * t]
            g = _sum_devices(gr[t])
            g_ref[...] = g
            d_ref[...], nm_ref[...], nv_ref[...] = _adam_math(g, w[t][...], m[t][...], v[t][...])

    out_shape = [jax.ShapeDtypeStruct((1, LANE), F32)]
    for t in range(nt):
        out_shape += [jax.ShapeDtypeStruct(ws[t].shape, F32)] * 4
    outs = pl.pallas_call(body, name=name, out_shape=out_shape, compiler_params=_cparams())(
        *grecvs, loss_recv, *ws, *ms, *vs)
    return outs[0], [outs[1 + 4 * t:5 + 4 * t] for t in range(nt)]


def _shard_cols(g):
    rows, cols = g.shape
    return g.reshape(rows, N_DEV, cols // N_DEV).transpose(1, 0, 2)


def _unshard_cols(blocks):
    return blocks.transpose(1, 0, 2).reshape(blocks.shape[1], -1)


def _heads(t, n, d):
    return t.reshape(t.shape[0], n, d).transpose(1, 0, 2)


def _unheads(t):
    return t.transpose(1, 0, 2).reshape(t.shape[1], -1)


def kernel(x, w_in, gate_up, gate_bias, gla_norm_g, w_out, ln1_g, ln1_b, w_up, conv_w, conv_b, w_down, ln2_g, ln2_b, loss_target, m_w_in, m_gate_up, m_gate_bias, m_gla_norm_g, m_w_out, m_ln1_g, m_ln1_b, m_w_up, m_conv_w, m_conv_b, m_w_down, m_ln2_g, m_ln2_b, v_w_in, v_gate_up, v_gate_bias, v_gla_norm_g, v_w_out, v_ln1_g, v_ln1_b, v_w_up, v_conv_w, v_conv_b, v_w_down, v_ln2_g, v_ln2_b):
    S, D = x.shape[1], x.shape[2]
    x2, tgt = x[0], loss_target[0]

    gathered = _all_gather_two_level([w_in[0].astype(MXU_DTYPE), gate_up[0], conv_w[0]], "gather_w_in")
    w_in_f = _unshard_cols(gathered[0])
    gate_up_f = _unshard_cols(gathered[1])
    conv_w_f = _unshard_cols(gathered[2])
    w_in_pad = jnp.pad(w_in_f, ((0, 0), (0, IN_PAD - IN_WIDTH)))
    gate_up_p = _heads(jnp.pad(gate_up_f, ((0, LANE - GLA_GATE_RANK), (0, 0))), GLA_HEADS // 2, GLA_PAIR_K)
    gate_bias_p = gate_bias.reshape(GLA_HEADS // 2, 1, GLA_PAIR_K)

    w_qkv = jnp.concatenate([w_in_f[:, :OFF_SBK] * (SB_HEAD_DIM ** -0.5), w_in_f[:, OFF_SBK:OFF_GQ]], axis=1)
    qkv = _matmul(x2, w_qkv, "nn", MXU_DTYPE, "proj_sb")
    proj = _matmul(x2, w_in_pad[:, OFF_GQ:], "nn", F32, "proj_gla")
    sb_o, sb_w, g_out, g_up, g_down = _sb_fwd(
        qkv, "sb_fwd", ride=[(w_out[0].astype(MXU_DTYPE), False), (w_up[0].astype(MXU_DTYPE), False),
                             (w_down[0].astype(MXU_DTYPE), False)])
    w_out_f = g_out.reshape(-1, D)
    w_up_f = _unshard_cols(g_up)
    w_down_f = g_down.reshape(-1, D)
    gla_o, prev = _gla_fwd(proj, gate_up_p, gate_bias_p, gla_norm_g, "gla_fwd")
    cat = jnp.concatenate([sb_o, gla_o], axis=1)
    r1 = _matmul(cat, w_out_f, "nn", F32, "mix", res=x2, res_scale=DN_ALPHA)
    h = _ln_fwd(r1, ln1_g, ln1_b, "ln1")
    u0 = _matmul(h, w_up_f, "nn", F32, "ffn_up")
    p = _conv_gelu_fwd(u0, conv_w_f, conv_b, "conv_gelu")
    r2 = _matmul(p, w_down_f, "nn", F32, "ffn_down", res=h, res_scale=DN_ALPHA)
    d_r2, loss_p, g_ln2_g, g_ln2_b = _ln_loss_bwd(r2, tgt, ln2_g, ln2_b, "ln2_loss")

    d_p = _matmul(d_r2, w_down_f, "nt", MXU_DTYPE, "d_ffn_act")
    g_w_down = _matmul(p, d_r2, "tn", BF16, "grad_w_down")
    d_u0, g_conv_w, g_conv_b = _conv_gelu_bwd(u0, d_p, conv_w_f, conv_b, "conv_gelu_bwd")
    g_w_up = _matmul(h, d_u0, "tn", BF16, "grad_w_up")
    d_h = _matmul(d_u0, w_up_f, "nt", F32, "d_h", res=d_r2, res_scale=DN_ALPHA)
    d_r1, g_ln1_g, g_ln1_b = _ln_bwd(r1, d_h, ln1_g, "ln1_bwd")
    g_w_out = _matmul(cat, d_r1, "tn", BF16, "grad_w_out")
    d_cat = _matmul(d_r1, w_out_f, "nt", F32, "d_cat")
    (d_gq, d_gk, d_gv, d_gg, d_ga_pad, g_gu_p, g_gb_p, g_gnorm) = _gla_bwd(
        proj, gate_up_p, gate_bias_p, gla_norm_g, prev, d_cat, "gla_bwd")
    g_gate_up = _unheads(g_gu_p[:, :GLA_GATE_RANK, :])
    g_gate_bias = g_gb_p.reshape(1, -1)
    small_g = [g_gate_bias, g_gnorm, g_ln1_g, g_ln1_b, g_conv_b, g_ln2_g, g_ln2_b]
    d_sq, d_sk, d_sv, *recv_rest = _sb_bwd(
        qkv, d_cat, sb_w, "sb_bwd",
        ride=[(g_w_out.reshape(N_DEV, -1, D), True), (_shard_cols(g_w_up), True),
              (g_w_down.reshape(N_DEV, -1, D), True), (_shard_cols(g_gate_up), True),
              (_shard_cols(g_conv_w), True)] + [(t, False) for t in small_g] + [(loss_p, False)])
    d_proj = jnp.concatenate([d_sq, d_sk, d_sv, d_gq, d_gk, d_gv, d_gg,
                              d_ga_pad.astype(MXU_DTYPE)], axis=1)
    g_w_in = _matmul(x2, d_proj, "tn", BF16, "grad_w_in")[:, :IN_WIDTH]
    d_x, recv_in = _matmul(d_proj, w_in_pad, "nt", F32, "d_x", res=d_r1, res_scale=DN_ALPHA,
                           ride=[(_shard_cols(g_w_in), True)])

    recv = [recv_in] + recv_rest[:5]
    sharded = [(w_in, m_w_in, v_w_in), (w_out, m_w_out, v_w_out), (w_up, m_w_up, v_w_up),
               (w_down, m_w_down, v_w_down), (gate_up, m_gate_up, v_gate_up), (conv_w, m_conv_w, v_conv_w)]
    upd = [_adamw(recv[n], w[0], m[0], v[0], "adamw_%d" % n) for n, (w, m, v) in enumerate(sharded)]
    loss_row, small = _adamw_replicated(
        recv_rest[5:12], recv_rest[12], [gate_bias, gla_norm_g, ln1_g, ln1_b, conv_b, ln2_g, ln2_b],
        [m_gate_bias, m_gla_norm_g, m_ln1_g, m_ln1_b, m_conv_b, m_ln2_g, m_ln2_b],
        [v_gate_bias, v_gla_norm_g, v_ln1_g, v_ln1_b, v_conv_b, v_ln2_g, v_ln2_b], "adamw_replicated")
    outs = []
    for kind in range(4):
        b_w_in, b_w_out, b_w_up, b_w_down, b_gate_up, b_conv_w = [u[kind][None] for u in upd]
        s_gb, s_gn, s_l1g, s_l1b, s_cb, s_l2g, s_l2b = [t[kind] for t in small]
        outs += [b_w_in, b_gate_up, s_gb, s_gn, b_w_out, s_l1g, s_l1b, b_w_up, b_conv_w, s_cb, b_w_down,
                 s_l2g, s_l2b]
    return (loss_row[0, 0], d_x[None], *outs)
```

```python
import math

import jax
import jax.numpy as jnp
from jax import lax
from jax.experimental import pallas as pl
from jax.experimental.pallas import tpu as pltpu

F32 = jnp.float32
BF16 = jnp.bfloat16
MXU_DTYPE = jnp.bfloat16

N_DEV = 8
D_MODEL = 1024
SB_WIDTH = 512
SB_HEADS = 8
SB_HEAD_DIM = 64
GLA_HEADS = 4
GLA_KEY_DIM = 64
GLA_VAL_DIM = 128
GLA_WIDTH = 512
GLA_GATE_RANK = 16
GLA_TAU = 16.0
CHUNK = 64
D_FF = 2816
CONV_WIDTH = 3
LN_EPS = 1e-5
RMS_EPS = 1e-6
DN_ALPHA = 2.0 ** 0.25
IN_WIDTH = 3088
LANE = 128
IN_PAD = 3200
OFF_SBQ, OFF_SBK, OFF_SBV = 0, 512, 1024
OFF_GQ, OFF_GK, OFF_GV, OFF_GG, OFF_GA = 1536, 1792, 2048, 2560, 3072
GLA_PAD = IN_PAD - OFF_GQ

ADAM_LR = 0.001
ADAM_B1 = 0.9
ADAM_B2 = 0.999
ADAM_EPS = 1e-08
ADAM_WD = 0.01
ADAM_STEP = 10

VMEM_LIMIT = 48 * 1024 * 1024
MESH_ID = pl.DeviceIdType.MESH


def _cparams(sem=None, **kw):
    return pltpu.CompilerParams(dimension_semantics=sem, vmem_limit_bytes=VMEM_LIMIT, **kw)


def _dot(a, b, dims):
    ca, cb = {"nn": (1, 0), "nt": (1, 1), "tn": (0, 0)}[dims]
    return lax.dot_general(a.astype(MXU_DTYPE), b.astype(MXU_DTYPE), (((ca,), (cb,)), ((), ())),
                           preferred_element_type=F32)


def _dot_split(a, b, dims):
    assert dims == "nn"
    hi = a.astype(BF16)
    lo = (a - hi.astype(F32)).astype(BF16)
    return lax.dot_general(jnp.concatenate([hi, lo], axis=1), jnp.concatenate([b, b], axis=0),
                           (((1,), (0,)), ((), ())), preferred_element_type=F32)


def _pick(dim, prefs):
    for p in prefs:
        if dim % p == 0:
            return p
    return dim


def _matmul(a, b, dims, out_dtype, name, res=None, res_scale=1.0, ride=()):
    if dims == "nn":
        (M, K), (_, N) = a.shape, b.shape
    elif dims == "nt":
        (M, K), (N, _) = a.shape, b.shape
    else:
        (K, M), (_, N) = a.shape, b.shape
    tm = _pick(M, (1024, 1408, 512, 256, 128))
    tn = _pick(N, (1408, 1024, 640, 512))
    if tn == N and N > 2048:
        tn = _pick(N, (256, 128))
    tk = _pick(K, (1024, 1408, 640, 512, 256, 128))
    nk = K // tk
    grid = (M // tm, N // tn, nk)
    if dims == "tn":
        a_spec = pl.BlockSpec((tk, tm), lambda i, j, k: (k, i))
    else:
        a_spec = pl.BlockSpec((tm, tk), lambda i, j, k: (i, k))
    if dims == "nt":
        b_spec = pl.BlockSpec((tn, tk), lambda i, j, k: (j, k))
    else:
        b_spec = pl.BlockSpec((tk, tn), lambda i, j, k: (k, j))
    o_spec = pl.BlockSpec((tm, tn), lambda i, j, k: (i, j))
    in_specs = [a_spec, b_spec]
    args = [a, b]
    if res is not None:
        in_specs.append(o_spec)
        args.append(res)

    def body(*refs):
        if res is not None:
            a_ref, b_ref, r_ref, o_ref, acc_ref = refs
        else:
            a_ref, b_ref, o_ref, acc_ref = refs
            r_ref = None
        k = pl.program_id(2)
        part = _dot(a_ref[...], b_ref[...], dims)

        def finish(total):
            if r_ref is not None:
                total = total + res_scale * r_ref[...]
            o_ref[...] = total.astype(o_ref.dtype)

        if nk == 1:
            finish(part)
        else:
            @pl.when(k == 0)
            def _():
                acc_ref[...] = part

            @pl.when(jnp.logical_and(k > 0, k < nk - 1))
            def _():
                acc_ref[...] += part

            @pl.when(k == nk - 1)
            def _():
                finish(acc_ref[...] + part)

    out_sds = jax.ShapeDtypeStruct((M, N), out_dtype)
    acc = pltpu.VMEM((tm, tn), F32)
    if not ride:
        return pl.pallas_call(
            body, name=name, grid=grid, in_specs=in_specs, out_specs=o_spec, out_shape=out_sds,
            scratch_shapes=[acc], compiler_params=_cparams(("parallel", "parallel", "arbitrary")),
        )(*args)
    x_args, x_in, x_out, x_shapes, x_sems, wrap = _riding_exchange(list(ride), grid)
    return pl.pallas_call(
        wrap(body, len(args), 1), name=name, grid=grid, in_specs=in_specs + x_in, out_specs=[o_spec] + x_out,
        out_shape=[out_sds] + x_shapes, scratch_shapes=[acc] + x_sems,
        compiler_params=_cparams(("arbitrary",) * 3, has_side_effects=True),
    )(*args, *x_args)


LN_ROWS = 512


def _ln_stats(r):
    mu = jnp.mean(r, axis=-1, keepdims=True)
    xc = r - mu
    var = jnp.mean(xc * xc, axis=-1, keepdims=True)
    return xc * lax.rsqrt(var + LN_EPS)


def _ln_fwd(r, g, b, name):
    S, D = r.shape

    def body(r_ref, g_ref, b_ref, h_ref, hb_ref):
        h = _ln_stats(r_ref[...]) * g_ref[...] + b_ref[...]
        h_ref[...] = h
        hb_ref[...] = h.astype(hb_ref.dtype)

    row = pl.BlockSpec((LN_ROWS, D), lambda i: (i, 0))
    vec = pl.BlockSpec((1, D), lambda i: (0, 0))
    return pl.pallas_call(
        body, name=name, grid=(S // LN_ROWS,), in_specs=[row, vec, vec], out_specs=[row, row],
        out_shape=[jax.ShapeDtypeStruct((S, D), F32), jax.ShapeDtypeStruct((S, D), MXU_DTYPE)],
        compiler_params=_cparams(("parallel",)),
    )(r, g, b)


def _ln_bwd_core(xhat, dy, g):
    dxh = dy * g
    m1 = jnp.mean(dxh, axis=-1, keepdims=True)
    m2 = jnp.mean(dxh * xhat, axis=-1, keepdims=True)
    return dxh - m1 - xhat * m2


def _ln_bwd(r, dy, g, name):
    S, D = r.shape

    def body(r_ref, dy_ref, g_ref, dr_ref, drb_ref, gg_ref, gb_ref):
        x = r_ref[...]
        mu = jnp.mean(x, axis=-1, keepdims=True)
        xc = x - mu
        rstd = lax.rsqrt(jnp.mean(xc * xc, axis=-1, keepdims=True) + LN_EPS)
        xhat = xc * rstd
        dy = dy_ref[...]
        dr = rstd * _ln_bwd_core(xhat, dy, g_ref[...])
        dr_ref[...] = dr
        drb_ref[...] = dr.astype(drb_ref.dtype)

        @pl.when(pl.program_id(0) == 0)
        def _():
            gg_ref[...] = jnp.zeros_like(gg_ref)
            gb_ref[...] = jnp.zeros_like(gb_ref)

        gg_ref[...] += jnp.sum(dy * xhat, axis=0, keepdims=True)
        gb_ref[...] += jnp.sum(dy, axis=0, keepdims=True)

    row = pl.BlockSpec((LN_ROWS, D), lambda i: (i, 0))
    vec = pl.BlockSpec((1, D), lambda i: (0, 0))
    return pl.pallas_call(
        body, name=name, grid=(S // LN_ROWS,), in_specs=[row, row, vec], out_specs=[row, row, vec, vec],
        out_shape=[jax.ShapeDtypeStruct((S, D), F32), jax.ShapeDtypeStruct((S, D), MXU_DTYPE),
                   jax.ShapeDtypeStruct((1, D), F32), jax.ShapeDtypeStruct((1, D), F32)],
        compiler_params=_cparams(("arbitrary",)),
    )(r, dy, g)


def _ln_loss_bwd(r, target, g, b, name):
    S, D = r.shape

    def body(r_ref, t_ref, g_ref, b_ref, dr_ref, drb_ref, loss_ref, gg_ref, gb_ref):
        x = r_ref[...]
        mu = jnp.mean(x, axis=-1, keepdims=True)
        xc = x - mu
        rstd = lax.rsqrt(jnp.mean(xc * xc, axis=-1, keepdims=True) + LN_EPS)
        xhat = xc * rstd
        y = xhat * g_ref[...] + b_ref[...]
        err = y - t_ref[...]
        dy = err * (1.0 / D)
        dr = rstd * _ln_bwd_core(xhat, dy, g_ref[...])
        dr_ref[...] = dr
        drb_ref[...] = dr.astype(drb_ref.dtype)

        @pl.when(pl.program_id(0) == 0)
        def _():
            loss_ref[...] = jnp.zeros_like(loss_ref)
            gg_ref[...] = jnp.zeros_like(gg_ref)
            gb_ref[...] = jnp.zeros_like(gb_ref)

        per_row = jnp.sum(err * err, axis=-1, keepdims=True) * (0.5 / D)
        loss_ref[...] += jnp.broadcast_to(jnp.sum(per_row, axis=0, keepdims=True), loss_ref.shape)
        gg_ref[...] += jnp.sum(dy * xhat, axis=0, keepdims=True)
        gb_ref[...] += jnp.sum(dy, axis=0, keepdims=True)

    row = pl.BlockSpec((LN_ROWS, D), lambda i: (i, 0))
    vec = pl.BlockSpec((1, D), lambda i: (0, 0))
    lvec = pl.BlockSpec((1, LANE), lambda i: (0, 0))
    return pl.pallas_call(
        body, name=name, grid=(S // LN_ROWS,), in_specs=[row, row, vec, vec],
        out_specs=[row, row, lvec, vec, vec],
        out_shape=[jax.ShapeDtypeStruct((S, D), F32), jax.ShapeDtypeStruct((S, D), MXU_DTYPE),
                   jax.ShapeDtypeStruct((1, LANE), F32),
                   jax.ShapeDtypeStruct((1, D), F32), jax.ShapeDtypeStruct((1, D), F32)],
        compiler_params=_cparams(("arbitrary",)),
    )(r, target, g, b)


CONV_COLS = 256
CONV_ROWS = 256
HALO = 8
INV_SQRT2 = 1.0 / math.sqrt(2.0)
INV_SQRT2PI = 1.0 / math.sqrt(2.0 * math.pi)


def _gelu(x):
    return 0.5 * x * (1.0 + lax.erf(x * INV_SQRT2))


def _gelu_and_grad(x):
    cdf = 0.5 * (1.0 + lax.erf(x * INV_SQRT2))
    return x * cdf, cdf + x * jnp.exp(-0.5 * x * x) * INV_SQRT2PI


def _conv_rows(ext, w_ref, b_ref, n):
    total = ext.shape[0]
    s1 = pltpu.roll(ext, 1, 0)
    s2 = pltpu.roll(ext, 2, 0)
    u = w_ref[2:3, :] * ext + w_ref[1:2, :] * s1 + w_ref[0:1, :] * s2 + b_ref[...]
    return u[HALO:total], s1[HALO:total], s2[HALO:total]


def _conv_gelu_fwd(u0, conv_w, conv_b, name):
    S, C2 = u0.shape
    F = C2 // 2
    ncb = F // CONV_COLS
    nrc = S // CONV_ROWS

    def body(ua_ref, uc_ref, wa_ref, wc_ref, ba_ref, bc_ref, p_ref):
        def chunk(ci, _):
            r0 = pl.multiple_of(ci * CONV_ROWS, CONV_ROWS)
            p0 = pl.multiple_of(jnp.maximum(r0 - HALO, 0), HALO)
            keep = (ci > 0).astype(F32)

            def load(ref):
                prev = ref[pl.ds(p0, HALO), :] * keep
                return jnp.concatenate([prev, ref[pl.ds(r0, CONV_ROWS), :]], axis=0)

            a, _, _ = _conv_rows(load(ua_ref), wa_ref, ba_ref, CONV_ROWS)
            c, _, _ = _conv_rows(load(uc_ref), wc_ref, bc_ref, CONV_ROWS)
            p_ref[pl.ds(r0, CONV_ROWS), :] = (_gelu(a) * c).astype(p_ref.dtype)
            return 0

        lax.fori_loop(0, nrc, chunk, 0)

    col_a = pl.BlockSpec((S, CONV_COLS), lambda j: (0, j))
    col_c = pl.BlockSpec((S, CONV_COLS), lambda j: (0, j + ncb))
    w_a = pl.BlockSpec((CONV_WIDTH, CONV_COLS), lambda j: (0, j))
    w_c = pl.BlockSpec((CONV_WIDTH, CONV_COLS), lambda j: (0, j + ncb))
    b_a = pl.BlockSpec((1, CONV_COLS), lambda j: (0, j))
    b_c = pl.BlockSpec((1, CONV_COLS), lambda j: (0, j + ncb))
    return pl.pallas_call(
        body, name=name, grid=(ncb,), in_specs=[col_a, col_c, w_a, w_c, b_a, b_c], out_specs=col_a,
        out_shape=jax.ShapeDtypeStruct((S, F), MXU_DTYPE),
        compiler_params=_cparams(("parallel",)),
    )(u0, u0, conv_w, conv_w, conv_b, conv_b)


def _conv_gelu_bwd(u0, dp, conv_w, conv_b, name):
    S, C2 = u0.shape
    F = C2 // 2
    ncb = F // CONV_COLS
    nrc = S // CONV_ROWS
    EXT = CONV_ROWS + HALO

    def body(ua_ref, uc_ref, dp_ref, wa_ref, wc_ref, ba_ref, bc_ref,
             da_ref, dc_ref, gwa_ref, gwc_ref, gba_ref, gbc_ref):
        gwa_ref[...] = jnp.zeros_like(gwa_ref)
        gwc_ref[...] = jnp.zeros_like(gwc_ref)
        gba_ref[...] = jnp.zeros_like(gba_ref)
        gbc_ref[...] = jnp.zeros_like(gbc_ref)
        rid = lax.broadcasted_iota(jnp.int32, (EXT, CONV_COLS), 0)

        def chunk(ci, _):
            r0 = pl.multiple_of(ci * CONV_ROWS, CONV_ROWS)
            p0 = pl.multiple_of(jnp.maximum(r0 - HALO, 0), HALO)
            n0 = pl.multiple_of(jnp.minimum(r0 + CONV_ROWS, S - HALO), HALO)
            keep_prev = (ci > 0).astype(F32)
            keep_next = (ci < nrc - 1).astype(F32)

            def load(ref):
                return jnp.concatenate([ref[pl.ds(p0, HALO), :] * keep_prev,
                                        ref[pl.ds(r0, CONV_ROWS), :],
                                        ref[pl.ds(n0, HALO), :] * keep_next], axis=0)

            ext_a = load(ua_ref)
            ext_c = load(uc_ref)
            a, a1, a2 = _conv_rows(ext_a, wa_ref, ba_ref, EXT)
            c, c1, c2 = _conv_rows(ext_c, wc_ref, bc_ref, EXT)
            a0 = ext_a[HALO:HALO + EXT]
            c0 = ext_c[HALO:HALO + EXT]
            dpe = jnp.concatenate([dp_ref[pl.ds(r0, CONV_ROWS), :].astype(F32),
                                   dp_ref[pl.ds(n0, HALO), :].astype(F32) * keep_next], axis=0)
            gelu_a, slope_a = _gelu_and_grad(a)
            d_a = dpe * c * slope_a
            d_c = dpe * gelu_a
            own = rid < CONV_ROWS

            def back(d_u, w_ref, x0, x1, x2, d_ref, gw_ref, gb_ref):
                d_u0 = (w_ref[2:3, :] * d_u + w_ref[1:2, :] * pltpu.roll(d_u, EXT - 1, 0)
                        + w_ref[0:1, :] * pltpu.roll(d_u, EXT - 2, 0))
                d_ref[pl.ds(r0, CONV_ROWS), :] = d_u0[0:CONV_ROWS].astype(d_ref.dtype)
                d_own = jnp.where(own, d_u, 0.0)
                gw_ref[...] += jnp.concatenate(
                    [jnp.sum(d_own * x2, axis=0, keepdims=True),
                     jnp.sum(d_own * x1, axis=0, keepdims=True),
                     jnp.sum(d_own * x0, axis=0, keepdims=True)], axis=0)
                gb_ref[...] += jnp.sum(d_own, axis=0, keepdims=True)

            back(d_a, wa_ref, a0, a1, a2, da_ref, gwa_ref, gba_ref)
            back(d_c, wc_ref, c0, c1, c2, dc_ref, gwc_ref, gbc_ref)
            return 0

        lax.fori_loop(0, nrc, chunk, 0)

    col_a = pl.BlockSpec((S, CONV_COLS), lambda j: (0, j))
    col_c = pl.BlockSpec((S, CONV_COLS), lambda j: (0, j + ncb))
    w_a = pl.BlockSpec((CONV_WIDTH, CONV_COLS), lambda j: (0, j))
    w_c = pl.BlockSpec((CONV_WIDTH, CONV_COLS), lambda j: (0, j + ncb))
    b_a = pl.BlockSpec((1, CONV_COLS), lambda j: (0, j))
    b_c = pl.BlockSpec((1, CONV_COLS), lambda j: (0, j + ncb))
    outs = pl.pallas_call(
        body, name=name, grid=(ncb,),
        in_specs=[col_a, col_c, col_a, w_a, w_c, b_a, b_c],
        out_specs=[col_a, col_a, w_a, w_a, b_a, b_a],
        out_shape=[jax.ShapeDtypeStruct((S, F), MXU_DTYPE), jax.ShapeDtypeStruct((S, F), MXU_DTYPE),
                   jax.ShapeDtypeStruct((CONV_WIDTH, F), F32), jax.ShapeDtypeStruct((CONV_WIDTH, F), F32),
                   jax.ShapeDtypeStruct((1, F), F32), jax.ShapeDtypeStruct((1, F), F32)],
        compiler_params=_cparams(("parallel",)),
    )(u0, u0, dp, conv_w, conv_w, conv_b, conv_b)
    da, dc, gwa, gwc, gba, gbc = outs
    return (jnp.concatenate([da, dc], axis=1), jnp.concatenate([gwa, gwc], axis=1),
            jnp.concatenate([gba, gbc], axis=1))


SB_TK = 128
SB_TQ_FWD = 1024
SB_TQ_BWD = 1024
SB_PAIRS_FWD = 2


def _softplus(z):
    return jnp.maximum(z, 0.0) + jnp.log(1.0 + jnp.exp(-jnp.abs(z)))


def _tri_ones(after):
    r = lax.broadcasted_iota(jnp.int32, (SB_TK, 2 * SB_TK), 0)
    c = lax.broadcasted_iota(jnp.int32, (SB_TK, 2 * SB_TK), 1)
    tri = (r > c) if after else (r < c)
    return jnp.where(c >= SB_TK, 1.0, jnp.where(tri, 1.0, 0.0)).astype(BF16)


def _sb_block_specs(S, TQ):
    NP = SB_WIDTH // LANE
    return [pl.BlockSpec((TQ, LANE), lambda p, i: (i, p)),
            pl.BlockSpec((S, LANE), lambda p, i: (0, NP + p)),
            pl.BlockSpec((S, LANE), lambda p, i: (0, 2 * NP + p))]


def _sb_fwd(qkv, name, ride=()):
    S = qkv.shape[0]
    TK = SB_TK
    TQ = min(SB_TQ_FWD, S)
    R = TQ // TK
    NP = SB_WIDTH // LANE
    PS = SB_PAIRS_FWD
    W = PS * LANE
    NH = 2 * PS
    nq = S // TQ
    nkb = S // TK
    grid = (NP // PS, nq)

    def body(q_ref, k_ref, v_ref, o_ref, wk_ref, wbuf, wsem, bbuf, bsem):
        p, i = pl.program_id(0), pl.program_id(1)
        row = lax.broadcasted_iota(jnp.int32, (TQ, TK), 0)
        col = lax.broadcasted_iota(jnp.int32, (TQ, TK), 1)
        after_ones = _tri_ones(True)
        first = lax.broadcasted_iota(jnp.int32, (TK, LANE), 1) < SB_HEAD_DIM

        def keep(slot, j):
            return pltpu.make_async_copy(wbuf.at[slot], wk_ref.at[pl.ds(p * NH, NH), i, j], wsem.at[slot])

        def keep_band(u):
            return pltpu.make_async_copy(bbuf.at[u], wk_ref.at[pl.ds(p * NH, NH), i, R * i + u], bsem.at[u])

        def block(j, carry, r0, stage):
            k0 = pl.multiple_of(j * TK, TK)
            masked = r0 is not None
            r0 = r0 or 0
            out = []
            for pr in range(PS):
                lanes = slice(pr * LANE, (pr + 1) * LANE)
                acc0, tail_a, tail_b = carry[3 * pr:3 * pr + 3]
                kab = k_ref[pl.ds(k0, TK), lanes]
                vab = v_ref[pl.ds(k0, TK), lanes]
                none = jnp.zeros_like(kab)
                k2 = jnp.concatenate([jnp.where(first, kab, none), jnp.where(first, none, kab)], axis=0)
                v2 = jnp.concatenate([jnp.where(first, vab, none), jnp.where(first, none, vab)], axis=0)
                z2 = _dot(q_ref[r0:, lanes], k2, "nt")
                tails, ws = [], []
                for hh, tail in enumerate((tail_a, tail_b)):
                    z = z2[:, hh * TK:(hh + 1) * TK]
                    sp = _softplus(z)
                    if masked:
                        strict = col[r0:] < row[r0:] - r0
                        sp = jnp.where(strict, sp, 0.0)
                    cs = _dot_split(sp, after_ones, "nn")
                    w = jnp.exp(z - sp - cs[:, :TK] - tail[r0:])
                    if masked:
                        w = jnp.where(strict, w, 0.0)
                    ws.append(w.astype(MXU_DTYPE))
                    stage[2 * pr + hh, r0:, :] = ws[-1]
                    tot = cs[:, TK:]
                    if r0:
                        tot = jnp.concatenate([jnp.zeros((r0, TK), F32), tot], axis=0)
                    tails.append(tail + tot)
                acc_r = acc0[r0:] + _dot(jnp.concatenate(ws, axis=1), v2, "nn")
                acc = jnp.concatenate([acc0[:r0], acc_r], axis=0) if r0 else acc_r
                out += [acc, tails[0], tails[1]]
            return tuple(out)

        carry = (jnp.zeros((TQ, LANE), F32), jnp.zeros((TQ, TK), F32), jnp.zeros((TQ, TK), F32)) * PS
        for u in reversed(range(R)):
            carry = block(R * i + u, carry, u * TK, bbuf.at[u])
        for u in range(R):
            keep_band(u).start()
        trips = R * i

        def below(n, c):
            slot = n % 2
            j = trips - 1 - n

            @pl.when(n >= 2)
            def _():
                keep(slot, j).wait()

            c = block(j, c, None, wbuf.at[slot])
            keep(slot, j).start()
            return c

        carry = lax.fori_loop(0, trips, below, carry)
        for back in (1, 2):

            @pl.when(trips >= back)
            def _():
                keep((trips - back) % 2, 0).wait()

        for u in range(R):
            keep_band(u).wait()
        for pr in range(PS):
            o_ref[:, pr * LANE:(pr + 1) * LANE] = carry[3 * pr].astype(o_ref.dtype)

    in_specs = [pl.BlockSpec((TQ, W), lambda p, i: (i, p)),
                pl.BlockSpec((S, W), lambda p, i: (0, NP // PS + p)),
                pl.BlockSpec((S, W), lambda p, i: (0, 2 * (NP // PS) + p))]
    ospec = pl.BlockSpec((TQ, W), lambda p, i: (i, p))
    x_args, x_in, x_out, x_shapes, x_sems, wrap = _riding_exchange(list(ride), grid)
    return pl.pallas_call(
        wrap(body, 3, 2) if ride else body, name=name, grid=grid,
        in_specs=in_specs + (x_in if ride else []),
        out_specs=[ospec, pl.BlockSpec(memory_space=pl.ANY)] + (x_out if ride else []),
        out_shape=[jax.ShapeDtypeStruct((S, SB_WIDTH), MXU_DTYPE),
                   jax.ShapeDtypeStruct((SB_HEADS, nq, nkb, TQ, TK), MXU_DTYPE)] + (x_shapes if ride else []),
        scratch_shapes=[pltpu.VMEM((2, NH, TQ, TK), MXU_DTYPE), pltpu.SemaphoreType.DMA((2,)),
                        pltpu.VMEM((R, NH, TQ, TK), MXU_DTYPE), pltpu.SemaphoreType.DMA((R,))]
        + (x_sems if ride else []),
        compiler_params=_cparams(("arbitrary", "arbitrary"), has_side_effects=True),
    )(qkv, qkv, qkv, *(x_args if ride else []))


def _sb_bwd(qkv, d_cat, w_kept, name, ride=()):
    S = qkv.shape[0]
    TK = SB_TK
    TQ = min(SB_TQ_BWD, S)
    assert TQ == min(SB_TQ_FWD, S)
    R = TQ // TK
    NP = SB_WIDTH // LANE
    grid = (NP, S // TQ)
    scale = SB_HEAD_DIM ** -0.5

    def body(q_ref, k_ref, v_ref, do_ref, wk_ref, dq_ref, dk_ref, dv_ref, dk_acc, dv_acc, wbuf, wsem, bbuf, bsem):
        p, i = pl.program_id(0), pl.program_id(1)

        @pl.when(i == 0)
        def _():
            dk_acc[...] = jnp.zeros_like(dk_acc)
            dv_acc[...] = jnp.zeros_like(dv_acc)

        row = lax.broadcasted_iota(jnp.int32, (TQ, TK), 0)
        col = lax.broadcasted_iota(jnp.int32, (TQ, TK), 1)
        before_ones = _tri_ones(False)
        first = lax.broadcasted_iota(jnp.int32, (TK, LANE), 1) < SB_HEAD_DIM
        qab = q_ref[...]
        doab = do_ref[...].astype(MXU_DTYPE)
        qdo = jnp.concatenate([qab, doab], axis=1)

        def fetch(slot, j):
            return pltpu.make_async_copy(wk_ref.at[pl.ds(2 * p, 2), i, j], wbuf.at[slot], wsem.at[slot])

        def fetch_band(u):
            return pltpu.make_async_copy(wk_ref.at[pl.ds(2 * p, 2), i, R * i + u], bbuf.at[u], bsem.at[u])

        def block(j, carry, r0, stage):
            k0 = pl.multiple_of(j * TK, TK)
            masked = r0 is not None
            r0 = r0 or 0
            kab = k_ref[pl.ds(k0, TK), :]
            vab = v_ref[pl.ds(k0, TK), :]
            none = jnp.zeros_like(kab)
            k_h = [jnp.where(first, kab, none), jnp.where(first, none, kab)]
            v_h = [jnp.where(first, vab, none), jnp.where(first, none, vab)]
            sums, dzs, ws = [], [], []
            for hh in range(2):
                gsum = carry[1 + hh]
                kv = jnp.concatenate([jnp.concatenate([k_h[hh], none], axis=1),
                                      jnp.concatenate([none, v_h[hh]], axis=1)], axis=0)
                zdw = _dot(qdo[r0:], kv, "nt")
                z = zdw[:, :TK]
                sig = 1.0 / (1.0 + jnp.exp(-z))
                wb = stage[hh, r0:, :]
                g = wb.astype(F32) * zdw[:, TK:]
                cg = _dot_split(g, before_ones, "nn")
                dz = g - sig * (g + cg[:, :TK] + gsum[r0:])
                if masked:
                    dz = jnp.where(col[r0:] < row[r0:] - r0, dz, 0.0)
                dzs.append(dz.astype(MXU_DTYPE))
                ws.append(wb)
                gsum_r = gsum[r0:] + cg[:, TK:]
                if r0:
                    gsum_r = jnp.concatenate([gsum[:r0], gsum_r], axis=0)
                sums.append(gsum_r)
            kvg = _dot(jnp.concatenate(dzs + ws, axis=1), qdo[r0:], "tn")
            dk_acc[pl.ds(k0, TK), :] += jnp.where(first, kvg[:TK, :LANE], kvg[TK:2 * TK, :LANE])
            dv_acc[pl.ds(k0, TK), :] += jnp.where(first, kvg[2 * TK:3 * TK, LANE:], kvg[3 * TK:, LANE:])
            dq_r = carry[0][r0:] + _dot(jnp.concatenate(dzs, axis=1), jnp.concatenate(k_h, axis=0), "nn")
            dq = jnp.concatenate([carry[0][:r0], dq_r], axis=0) if r0 else dq_r
            return (dq, *sums)

        trips = R * i
        for u in range(R):
            fetch_band(u).start()

        @pl.when(trips >= 1)
        def _():
            fetch(0, 0).start()

        def below(j, c):
            slot = j % 2

            @pl.when(j + 1 < trips)
            def _():
                fetch(1 - slot, j + 1).start()

            fetch(slot, j).wait()
            return block(j, c, None, wbuf.at[slot])

        zero = jnp.zeros((TQ, TK), F32)
        carry = lax.fori_loop(0, trips, below, (jnp.zeros((TQ, LANE), F32), zero, zero))
        for u in range(R):
            fetch_band(u).wait()
        for u in range(R):
            carry = block(trips + u, carry, u * TK, bbuf.at[u])
        dq_ref[...] = (carry[0] * scale).astype(dq_ref.dtype)

        @pl.when(i == S // TQ - 1)
        def _():
            dk_ref[...] = dk_acc[...].astype(dk_ref.dtype)
            dv_ref[...] = dv_acc[...].astype(dv_ref.dtype)

    qspec = pl.BlockSpec((TQ, LANE), lambda p, i: (i, p))
    full = pl.BlockSpec((S, LANE), lambda p, i: (0, p))
    sds = jax.ShapeDtypeStruct((S, SB_WIDTH), MXU_DTYPE)
    scratch = [pltpu.VMEM((S, LANE), F32), pltpu.VMEM((S, LANE), F32),
               pltpu.VMEM((2, 2, TQ, TK), MXU_DTYPE), pltpu.SemaphoreType.DMA((2,)),
               pltpu.VMEM((R, 2, TQ, TK), MXU_DTYPE), pltpu.SemaphoreType.DMA((R,))]
    x_args, x_in, x_out, x_shapes, x_sems, wrap = _riding_exchange(list(ride), grid)
    return pl.pallas_call(
        wrap(body, 5, 3) if ride else body, name=name, grid=grid,
        in_specs=_sb_block_specs(S, TQ) + [qspec, pl.BlockSpec(memory_space=pl.ANY)] + (x_in if ride else []),
        out_specs=[qspec, full, full] + (x_out if ride else []),
        out_shape=[sds, sds, sds] + (x_shapes if ride else []),
        scratch_shapes=scratch + (x_sems if ride else []),
        compiler_params=_cparams(("arbitrary", "arbitrary"), has_side_effects=bool(ride)),
    )(qkv, qkv, qkv, d_cat, w_kept, *(x_args if ride else []))


GLA_ROWS = 1024
GLA_GROUP_FWD = 16
GLA_GROUP_BWD = 8
GLA_PAIR_K = 2 * GLA_KEY_DIM
GLA_PAIR_V = 2 * GLA_VAL_DIM


def _log_sigmoid(x):
    return -_softplus(-x)


def _dot_split_lhs01(m01, x):
    hi = x.astype(BF16)
    lo = (x - hi.astype(F32)).astype(BF16)
    return lax.dot_general(jnp.concatenate([m01, m01], axis=1), jnp.concatenate([hi, lo], axis=0),
                           (((1,), (0,)), ((), ())), preferred_element_type=F32)


def _dot_split_tn(x, m01):
    hi = x.astype(BF16)
    lo = (x - hi.astype(F32)).astype(BF16)
    return lax.dot_general(jnp.concatenate([hi, lo], axis=0), jnp.concatenate([m01, m01], axis=0),
                           (((0,), (0,)), ((), ())), preferred_element_type=F32)


def _dot_split_nt01(m01, x):
    hi = x.astype(BF16)
    lo = (x - hi.astype(F32)).astype(BF16)
    return lax.dot_general(jnp.concatenate([m01, m01], axis=1), jnp.concatenate([hi, lo], axis=1),
                           (((1,), (1,)), ((), ())), preferred_element_type=F32)


def _gla_consts():
    C = CHUNK
    row = lax.broadcasted_iota(jnp.int32, (C, C), 0)
    col = lax.broadcasted_iota(jnp.int32, (C, C), 1)
    first = lax.broadcasted_iota(jnp.int32, (C, GLA_PAIR_K), 1) < GLA_KEY_DIM
    r = lax.broadcasted_iota(jnp.int32, (GLA_PAIR_K, GLA_PAIR_V), 0)
    c = lax.broadcasted_iota(jnp.int32, (GLA_PAIR_K, GLA_PAIR_V), 1)
    own = (r < GLA_KEY_DIM) == (c < GLA_VAL_DIM)
    rowk = lax.broadcasted_iota(jnp.int32, (C, GLA_PAIR_K), 0)
    return dict(row=row, col=col, first=first, own=own, rowk=rowk,
                incl=jnp.where(row >= col, 1.0, 0.0).astype(BF16),
                ones=jnp.ones((C, GLA_PAIR_V), BF16))


def _pack_state(state):
    top = lax.broadcasted_iota(jnp.int32, (GLA_PAIR_K, GLA_VAL_DIM), 0) < GLA_KEY_DIM
    return jnp.where(top, state[:, :GLA_VAL_DIM], state[:, GLA_VAL_DIM:])


def _unpack_state(packed):
    top = lax.broadcasted_iota(jnp.int32, (GLA_PAIR_K, GLA_VAL_DIM), 0) < GLA_KEY_DIM
    return jnp.concatenate([jnp.where(top, packed, 0.0), jnp.where(top, 0.0, packed)], axis=1)


def _gla_chunk(qc, kc, vc, gate, k):
    C = CHUNK
    row, col, first, rowk = k["row"], k["col"], k["first"], k["rowk"]
    la = _log_sigmoid(gate) * (1.0 / GLA_TAU)
    b = _dot_split_lhs01(k["incl"], la)
    b_ref = jnp.sum(jnp.where(rowk == C // 2 - 1, b, 0.0), axis=0, keepdims=True)
    b_last = jnp.sum(la, axis=0, keepdims=True)
    qs = qc * (GLA_KEY_DIM ** -0.5)
    q_in = qs * jnp.exp(b - b_ref)
    k_in = kc * jnp.exp(b_ref - b)
    k_dec = kc * jnp.exp(b_last - b)
    q_b = qs * jnp.exp(b)
    k_in_h = [jnp.where(first, k_in, 0.0), jnp.where(first, 0.0, k_in)]
    v_h = [vc[:, :GLA_VAL_DIM], vc[:, GLA_VAL_DIM:]]
    sc = [jnp.where(row >= col, _dot(q_in, k_in_h[hh], "nt"), 0.0) for hh in range(2)]
    o_intra = jnp.concatenate([_dot(sc[hh], v_h[hh], "nn") for hh in range(2)], axis=1)
    upd = jnp.where(k["own"], _dot(k_dec, vc, "tn"), 0.0)
    dec_col = jnp.exp(_dot_split_tn(la, k["ones"]))
    return dict(la=la, b=b, b_ref=b_ref, b_last=b_last, q_in=q_in, k_in=k_in, k_dec=k_dec, q_b=q_b,
                k_in_h=k_in_h, v_h=v_h, sc=sc, o_intra=o_intra, upd=upd, dec_col=dec_col)


def _rms_gate(o, gg):
    rinv = lax.rsqrt(jnp.mean(o * o, axis=-1, keepdims=True) + RMS_EPS)
    o_n = o * rinv
    sg = 1.0 / (1.0 + jnp.exp(-gg))
    return o_n, rinv, sg


def _gla_in_specs(rows_of, RB):
    PK, PV = GLA_PAIR_K, GLA_PAIR_V
    return [pl.BlockSpec((RB, PK), lambda i, p: (rows_of(i), p)),
            pl.BlockSpec((RB, PK), lambda i, p: (rows_of(i), (OFF_GK - OFF_GQ) // PK + p)),
            pl.BlockSpec((RB, PV), lambda i, p: (rows_of(i), (OFF_GV - OFF_GQ) // PV + p)),
            pl.BlockSpec((RB, PV), lambda i, p: (rows_of(i), (OFF_GG - OFF_GQ) // PV + p)),
            pl.BlockSpec((RB, LANE), lambda i, p: (rows_of(i), (OFF_GA - OFF_GQ) // LANE)),
            pl.BlockSpec((1, LANE, PK), lambda i, p: (p, 0, 0)),
            pl.BlockSpec((1, 1, PK), lambda i, p: (p, 0, 0)),
            pl.BlockSpec((1, GLA_VAL_DIM), lambda i, p: (0, 0))]


def _gla_fwd(proj, gate_up_p, gate_bias_p, gnorm, name):
    S = proj.shape[0]
    C, RB, PK, PV, dv = CHUNK, min(GLA_ROWS, S), GLA_PAIR_K, GLA_PAIR_V, GLA_VAL_DIM
    NP = GLA_HEADS // 2
    G = min(GLA_GROUP_FWD, RB // C)
    nchunk = S // C
    ngroup = RB // (C * G)

    def body(q_ref, k_ref, v_ref, gg_ref, ga_ref, gu_ref, gb_ref, gn_ref, o_ref, prev_ref, st_ref):
        i, p = pl.program_id(0), pl.program_id(1)
        k = _gla_consts()

        @pl.when(i == 0)
        def _():
            st_ref[p] = jnp.zeros((PK, PV), F32)

        def group(gi, state):
            for u in range(G):
                ci = gi * G + u
                rows = pl.ds(pl.multiple_of(ci * C, C), C)
                gate = _dot(ga_ref[rows, :], gu_ref[0], "nn") + gb_ref[0]
                f = _gla_chunk(q_ref[rows, :], k_ref[rows, :], v_ref[rows, :], gate, k)
                prev_ref[0, ci] = _pack_state(state)
                o = f["o_intra"] + _dot(f["q_b"], state, "nn")
                state = f["dec_col"] * state + f["upd"]
                ggv = gg_ref[rows, :]
                halves = []
                for hh in range(2):
                    lanes = slice(hh * dv, (hh + 1) * dv)
                    o_n, _, sg = _rms_gate(o[:, lanes], ggv[:, lanes])
                    halves.append(o_n * gn_ref[...] * (ggv[:, lanes] * sg))
                o_ref[rows, :] = jnp.concatenate(halves, axis=1).astype(o_ref.dtype)
            return state

        st_ref[p] = lax.fori_loop(0, ngroup, group, st_ref[p])

    return pl.pallas_call(
        body, name=name, grid=(S // RB, NP), in_specs=_gla_in_specs(lambda i: i, RB),
        out_specs=[pl.BlockSpec((RB, PV), lambda i, p: (i, p)),
                   pl.BlockSpec((1, RB // C, PK, dv), lambda i, p: (p, i, 0, 0))],
        out_shape=[jax.ShapeDtypeStruct((S, GLA_WIDTH), MXU_DTYPE),
                   jax.ShapeDtypeStruct((NP, nchunk, PK, dv), F32)],
        scratch_shapes=[pltpu.VMEM((NP, PK, PV), F32)],
        compiler_params=_cparams(("arbitrary", "arbitrary")),
    )(proj, proj, proj, proj, proj, gate_up_p, gate_bias_p, gnorm)


def _gla_bwd(proj, gate_up_p, gate_bias_p, gnorm, prev, d_cat, name):
    S = proj.shape[0]
    C, RB, PK, PV, dv = CHUNK, min(GLA_ROWS, S), GLA_PAIR_K, GLA_PAIR_V, GLA_VAL_DIM
    NP = GLA_HEADS // 2
    G = min(GLA_GROUP_BWD, RB // C)
    nb = S // RB
    ngroup = RB // (C * G)

    def body(q_ref, k_ref, v_ref, gg_ref, ga_ref, gu_ref, gb_ref, gn_ref, prev_ref, do_ref,
             dq_ref, dk_ref, dv_ref, dgg_ref, dga_ref, ggu_ref, ggb_ref, ggn_ref, st_ref):
        i, p = pl.program_id(0), pl.program_id(1)
        k = _gla_consts()
        row, col, first, rowk = k["row"], k["col"], k["first"], k["rowk"]
        gu = gu_ref[0]

        @pl.when(i == 0)
        def _():
            st_ref[p] = jnp.zeros((PK, PV), F32)
            ggu_ref[p] = jnp.zeros((LANE, PK), F32)
            ggb_ref[p] = jnp.zeros((1, PK), F32)

        @pl.when(jnp.logical_and(i == 0, p == 0))
        def _():
            ggn_ref[...] = jnp.zeros_like(ggn_ref)

        @pl.when(p == 0)
        def _():
            dga_ref[...] = jnp.zeros_like(dga_ref)

        upper_incl = jnp.where(col >= row, 1.0, 0.0).astype(BF16)
        ones_8 = jnp.ones((8, PV), BF16)

        def group(gn, dstate):
            gi = ngroup - 1 - gn
            for u in reversed(range(G)):
                ci = gi * G + u
                rows = pl.ds(pl.multiple_of(ci * C, C), C)
                ga = ga_ref[rows, :]
                gate = _dot(ga, gu, "nn") + gb_ref[0]
                vc = v_ref[rows, :]
                f = _gla_chunk(q_ref[rows, :], k_ref[rows, :], vc, gate, k)
                state = _unpack_state(prev_ref[0, ci])
                o = f["o_intra"] + _dot(f["q_b"], state, "nn")
                ggv = gg_ref[rows, :]
                dout = do_ref[rows, :]
                d_o_h, dgg_h = [], []
                for hh in range(2):
                    lanes = slice(hh * dv, (hh + 1) * dv)
                    o_n, rinv, sg = _rms_gate(o[:, lanes], ggv[:, lanes])
                    silu = ggv[:, lanes] * sg
                    dgg_h.append(dout[:, lanes] * o_n * gn_ref[...] * (sg * (1.0 + ggv[:, lanes] * (1.0 - sg))))
                    d_ong = dout[:, lanes] * silu
                    ggn_ref[...] += jnp.sum(d_ong * o_n, axis=0, keepdims=True)
                    d_on = d_ong * gn_ref[...]
                    d_o_h.append(rinv * (d_on - o_n * jnp.mean(d_on * o_n, axis=-1, keepdims=True)))
                dgg_ref[rows, :] = jnp.concatenate(dgg_h, axis=1).astype(dgg_ref.dtype)
                d_o = jnp.concatenate(d_o_h, axis=1)
                d_upd = jnp.where(k["own"], dstate, 0.0)
                d_dec_col = dstate * state * f["dec_col"]
                dstate = f["dec_col"] * dstate + _dot(f["q_b"], d_o, "tn")
                dsc = [jnp.where(row >= col, _dot(d_o_h[hh], f["v_h"][hh], "nt"), 0.0) for hh in range(2)]
                dv_ref[rows, :] = (jnp.concatenate([_dot(f["sc"][hh], d_o_h[hh], "tn") for hh in range(2)], axis=1)
                                   + _dot(f["k_dec"], d_upd, "nn")).astype(dv_ref.dtype)
                q_in_h = [jnp.where(first, f["q_in"], 0.0), jnp.where(first, 0.0, f["q_in"])]
                dq_in = _dot(dsc[0], f["k_in_h"][0], "nn") + _dot(dsc[1], f["k_in_h"][1], "nn")
                dk_in = _dot(dsc[0], q_in_h[0], "tn") + _dot(dsc[1], q_in_h[1], "tn")
                dq_b = _dot(d_o, state, "nt")
                dkdec = _dot(vc, d_upd, "nt")
                b = f["b"]
                e1 = jnp.exp(b - f["b_ref"])
                e2 = jnp.exp(f["b_ref"] - b)
                e3 = jnp.exp(f["b_last"] - b)
                eb = jnp.exp(b)
                dq_ref[rows, :] = ((dq_in * e1 + dq_b * eb) * (GLA_KEY_DIM ** -0.5)).astype(dq_ref.dtype)
                dk_ref[rows, :] = (dk_in * e2 + dkdec * e3).astype(dk_ref.dtype)
                t_q = dq_in * f["q_in"]
                t_k = dk_in * f["k_in"]
                t_d = dkdec * f["k_dec"]
                db = t_q - t_k - t_d + dq_b * f["q_b"]
                db_ref = jnp.sum(t_k - t_q, axis=0, keepdims=True)
                db_last = (jnp.sum(t_d, axis=0, keepdims=True)
                           + jnp.max(_dot_split_nt01(ones_8, d_dec_col), axis=0, keepdims=True))
                db = db + jnp.where(rowk == C // 2 - 1, db_ref, 0.0) + jnp.where(rowk == C - 1, db_last, 0.0)
                dla = _dot_split_lhs01(upper_incl, db)
                d_gate = dla * (1.0 / GLA_TAU) * (1.0 / (1.0 + jnp.exp(gate)))
                ggb_ref[p] += jnp.sum(d_gate, axis=0, keepdims=True)
                ggu_ref[p] += _dot(ga, d_gate, "tn")
                dga_ref[rows, :] += _dot(d_gate, gu, "nt")
            return dstate

        st_ref[p] = lax.fori_loop(0, ngroup, group, st_ref[p])

    back = lambda i: nb - 1 - i
    NPV = SB_WIDTH // PV
    return pl.pallas_call(
        body, name=name, grid=(nb, NP),
        in_specs=_gla_in_specs(back, RB) + [pl.BlockSpec((1, RB // C, PK, dv), lambda i, p: (p, back(i), 0, 0)),
                                            pl.BlockSpec((RB, PV), lambda i, p: (back(i), NPV + p))],
        out_specs=[pl.BlockSpec((RB, PK), lambda i, p: (back(i), p)),
                   pl.BlockSpec((RB, PK), lambda i, p: (back(i), p)),
                   pl.BlockSpec((RB, PV), lambda i, p: (back(i), p)),
                   pl.BlockSpec((RB, PV), lambda i, p: (back(i), p)),
                   pl.BlockSpec((RB, LANE), lambda i, p: (back(i), 0)),
                   pl.BlockSpec((NP, LANE, PK), lambda i, p: (0, 0, 0)),
                   pl.BlockSpec((NP, 1, PK), lambda i, p: (0, 0, 0)),
                   pl.BlockSpec((1, dv), lambda i, p: (0, 0))],
        out_shape=[jax.ShapeDtypeStruct((S, NP * PK), MXU_DTYPE), jax.ShapeDtypeStruct((S, NP * PK), MXU_DTYPE),
                   jax.ShapeDtypeStruct((S, GLA_WIDTH), MXU_DTYPE), jax.ShapeDtypeStruct((S, GLA_WIDTH), MXU_DTYPE),
                   jax.ShapeDtypeStruct((S, LANE), F32), jax.ShapeDtypeStruct((NP, LANE, PK), F32),
                   jax.ShapeDtypeStruct((NP, 1, PK), F32), jax.ShapeDtypeStruct((1, dv), F32)],
        scratch_shapes=[pltpu.VMEM((NP, PK, PV), F32)],
        compiler_params=_cparams(("arbitrary", "arbitrary")),
    )(proj, proj, proj, proj, proj, gate_up_p, gate_bias_p, gnorm, prev, d_cat)


def _exchange_copies(scatter_flags, ins, outs, send_sems, recv_sems, local_sems):
    n_peer = N_DEV - 1
    x, y, c = lax.axis_index("x"), lax.axis_index("y"), lax.axis_index("c")
    me = 4 * x + 2 * y + c
    copies = []
    for a, scatter in enumerate(scatter_flags):
        own = ins[a].at[me] if scatter else ins[a]
        copies.append(pltpu.make_async_copy(own, outs[a].at[me], local_sems.at[a]))
    for r in range(1, N_DEV):
        px = 1 - x if r & 4 else x
        py = 1 - y if r & 2 else y
        pc = 1 - c if r & 1 else c
        for a, scatter in enumerate(scatter_flags):
            src = ins[a].at[4 * px + 2 * py + pc] if scatter else ins[a]
            copies.append(pltpu.make_async_remote_copy(
                src_ref=src, dst_ref=outs[a].at[me],
                send_sem=send_sems.at[a * n_peer + r - 1], recv_sem=recv_sems.at[a * n_peer + r - 1],
                device_id=(px, py, pc), device_id_type=MESH_ID))
    return copies


def _exchange_shapes(items):
    out_shape = []
    for arr, scatter in items:
        shp = arr.shape if scatter else (N_DEV,) + arr.shape
        out_shape.append(jax.ShapeDtypeStruct(shp, arr.dtype))
    n = len(items)
    sems = [pltpu.SemaphoreType.DMA((n * (N_DEV - 1),)), pltpu.SemaphoreType.DMA((n * (N_DEV - 1),)),
            pltpu.SemaphoreType.DMA((n,))]
    return out_shape, sems


def _exchange(items, name):
    n = len(items)
    flags = [sc for _, sc in items]

    def body(*refs):
        copies = _exchange_copies(flags, refs[:n], refs[n:2 * n], *refs[2 * n:])
        for cp in copies:
            cp.start()
        for cp in copies:
            cp.wait()

    out_shape, sems = _exchange_shapes(items)
    any_spec = pl.BlockSpec(memory_space=pl.ANY)
    return pl.pallas_call(
        body, name=name, in_specs=[any_spec] * n, out_specs=[any_spec] * n, out_shape=out_shape,
        scratch_shapes=sems, compiler_params=pltpu.CompilerParams(has_side_effects=True),
    )(*[arr for arr, _ in items])


def _all_gather_two_level(arrays, name):
    n = len(arrays)
    K = N_DEV - 1

    def body(*refs):
        ins, outs = refs[:n], refs[n:2 * n]
        send_sems, recv_sems, local_sems = refs[2 * n:]
        x, y, c = lax.axis_index("x"), lax.axis_index("y"), lax.axis_index("c")
        sibling = (x, y, 1 - c)
        chips = [(1 - x, y), (x, 1 - y), (1 - x, 1 - y)]

        def slot(px, py, pc):
            return 4 * px + 2 * py + pc

        def copy(a, k, block, to, src=None):
            rows = outs[a].at[slot(*block)]
            return pltpu.make_async_remote_copy(
                src_ref=rows if src is None else src, dst_ref=rows,
                send_sem=send_sems.at[a * K + k], recv_sem=recv_sems.at[a * K + k],
                device_id=to, device_id_type=MESH_ID)

        me = (x, y, c)
        local, first, passed = [], [], []
        for a in range(n):
            cp = pltpu.make_async_copy(ins[a], outs[a].at[slot(*me)], local_sems.at[a])
            cp.start()
            local.append(cp)
            first.append(copy(a, 0, me, sibling, src=ins[a]))
            first += [copy(a, 1 + j, me, (*chip, c), src=ins[a]) for j, chip in enumerate(chips)]
        for cp in first:
            cp.start()
        for j, chip in enumerate(chips):
            for a in range(n):
                copy(a, 1 + j, (*chip, c), me).wait_recv()
                cp = copy(a, 4 + j, (*chip, c), sibling)
                cp.start()
                passed.append(cp)
        for a in range(n):
            copy(a, 0, sibling, me).wait_recv()
            for j, chip in enumerate(chips):
                copy(a, 4 + j, (*chip, 1 - c), me).wait_recv()
        for cp in first + passed:
            cp.wait_send()
        for cp in local:
            cp.wait()

    out_shape, sems = _exchange_shapes([(arr, False) for arr in arrays])
    any_spec = pl.BlockSpec(memory_space=pl.ANY)
    return pl.pallas_call(
        body, name=name, in_specs=[any_spec] * n, out_specs=[any_spec] * n, out_shape=out_shape,
        scratch_shapes=sems, compiler_params=pltpu.CompilerParams(has_side_effects=True),
    )(*arrays)


def _riding_exchange(items, grid):
    n = len(items)
    flags = [sc for _, sc in items]
    out_shape, sems = _exchange_shapes(items)
    any_spec = pl.BlockSpec(memory_space=pl.ANY)

    def wrap(body, n_in, n_out):
        def fused(*refs):
            ins = refs[:n_in]
            x_ins = refs[n_in:n_in + n]
            outs = refs[n_in + n:n_in + n + n_out]
            x_outs = refs[n_in + n + n_out:n_in + 2 * n + n_out]
            rest = refs[n_in + 2 * n + n_out:]
            x_sems, scratch = rest[len(rest) - 3:], rest[:len(rest) - 3]
            first = last = True
            for d, n_d in enumerate(grid):
                first = jnp.logical_and(first, pl.program_id(d) == 0)
                last = jnp.logical_and(last, pl.program_id(d) == n_d - 1)

            @pl.when(first)
            def _():
                for cp in _exchange_copies(flags, x_ins, x_outs, *x_sems):
                    cp.start()

            body(*ins, *outs, *scratch)

            @pl.when(last)
            def _():
                for cp in _exchange_copies(flags, x_ins, x_outs, *x_sems):
                    cp.wait()

        return fused

    return [arr for arr, _ in items], [any_spec] * n, [any_spec] * n, out_shape, sems, wrap


def _sum_devices(ref):
    g = ref[0].astype(F32)
    for q in range(1, N_DEV):
        g = g + ref[q].astype(F32)
    return g


def _adam_math(g, w, m, v):
    nm = ADAM_B1 * m + (1.0 - ADAM_B1) * g
    nv = ADAM_B2 * v + (1.0 - ADAM_B2) * (g * g)
    m_hat = nm / (1.0 - ADAM_B1 ** ADAM_STEP)
    v_hat = nv / (1.0 - ADAM_B2 ** ADAM_STEP)
    return -ADAM_LR * (m_hat / (jnp.sqrt(v_hat) + ADAM_EPS) + ADAM_WD * w), nm, nv


def _adamw(grecv, w, m, v, name):
    R, C = w.shape
    tile = _pick(R, (256, 176, 128)) if R * C > 65536 else R

    def body(gr_ref, w_ref, m_ref, v_ref, g_ref, d_ref, nm_ref, nv_ref):
        g = _sum_devices(gr_ref)
        g_ref[...] = g
        d_ref[...], nm_ref[...], nv_ref[...] = _adam_math(g, w_ref[...], m_ref[...], v_ref[...])

    blk = pl.BlockSpec((tile, C), lambda i: (i, 0))
    sds = jax.ShapeDtypeStruct((R, C), F32)
    return pl.pallas_call(
        body, name=name, grid=(R // tile,),
        in_specs=[pl.BlockSpec((N_DEV, tile, C), lambda i: (0, i, 0)), blk, blk, blk],
        out_specs=[blk, blk, blk, blk], out_shape=[sds, sds, sds, sds],
        compiler_params=_cparams(("parallel",)),
    )(grecv, w, m, v)


def _adamw_replicated(grecvs, loss_recv, ws, ms, vs, name):
    nt = len(ws)

    def body(*refs):
        gr, lr = refs[:nt], refs[nt]
        w, m, v = refs[nt + 1:2 * nt + 1], refs[2 * nt + 1:3 * nt + 1], refs[3 * nt + 1:4 * nt + 1]
        outs = refs[4 * nt + 1:]
        outs[0][...] = _sum_devices(lr)
        for t in range(nt):
            g_ref, d_ref, nm_ref, nv_ref = outs[1 + 4 * t:5 + 4 * t]
            g = _sum_devices(gr[t])
            g_ref[...] = g
            d_ref[...], nm_ref[...], nv_ref[...] = _adam_math(g, w[t][...], m[t][...], v[t][...])

    out_shape = [jax.ShapeDtypeStruct((1, LANE), F32)]
    for t in range(nt):
        out_shape += [jax.ShapeDtypeStruct(ws[t].shape, F32)] * 4
    outs = pl.pallas_call(body, name=name, out_shape=out_shape, compiler_params=_cparams())(
        *grecvs, loss_recv, *ws, *ms, *vs)
    return outs[0], [outs[1 + 4 * t:5 + 4 * t] for t in range(nt)]


def _shard_cols(g):
    rows, cols = g.shape
    return g.reshape(rows, N_DEV, cols // N_DEV).transpose(1, 0, 2)


def _unshard_cols(blocks):
    return blocks.transpose(1, 0, 2).reshape(blocks.shape[1], -1)


def _heads(t, n, d):
    return t.reshape(t.shape[0], n, d).transpose(1, 0, 2)


def _unheads(t):
    return t.transpose(1, 0, 2).reshape(t.shape[1], -1)


def kernel(x, w_in, gate_up, gate_bias, gla_norm_g, w_out, ln1_g, ln1_b, w_up, conv_w, conv_b, w_down, ln2_g, ln2_b, loss_target, m_w_in, m_gate_up, m_gate_bias, m_gla_norm_g, m_w_out, m_ln1_g, m_ln1_b, m_w_up, m_conv_w, m_conv_b, m_w_down, m_ln2_g, m_ln2_b, v_w_in, v_gate_up, v_gate_bias, v_gla_norm_g, v_w_out, v_ln1_g, v_ln1_b, v_w_up, v_conv_w, v_conv_b, v_w_down, v_ln2_g, v_ln2_b):
    S, D = x.shape[1], x.shape[2]
    x2, tgt = x[0], loss_target[0]

    gathered = _all_gather_two_level([w_in[0].astype(MXU_DTYPE), gate_up[0], conv_w[0]], "gather_w_in")
    w_in_f = _unshard_cols(gathered[0])
    gate_up_f = _unshard_cols(gathered[1])
    conv_w_f = _unshard_cols(gathered[2])
    w_in_pad = jnp.pad(w_in_f, ((0, 0), (0, IN_PAD - IN_WIDTH)))
    gate_up_p = _heads(jnp.pad(gate_up_f, ((0, LANE - GLA_GATE_RANK), (0, 0))), GLA_HEADS // 2, GLA_PAIR_K)
    gate_bias_p = gate_bias.reshape(GLA_HEADS // 2, 1, GLA_PAIR_K)

    w_qkv = jnp.concatenate([w_in_f[:, :OFF_SBK] * (SB_HEAD_DIM ** -0.5), w_in_f[:, OFF_SBK:OFF_GQ]], axis=1)
    x2b = x2.astype(MXU_DTYPE)
    qkv = _matmul(x2b, w_qkv, "nn", MXU_DTYPE, "proj_sb")
    proj = _matmul(x2b, w_in_pad[:, OFF_GQ:], "nn", F32, "proj_gla")
    sb_o, sb_w, g_out, g_up, g_down = _sb_fwd(
        qkv, "sb_fwd", ride=[(w_out[0].astype(MXU_DTYPE), False), (w_up[0].astype(MXU_DTYPE), False),
                             (w_down[0].astype(MXU_DTYPE), False)])
    w_out_f = g_out.reshape(-1, D)
    w_up_f = _unshard_cols(g_up)
    w_down_f = g_down.reshape(-1, D)
    gla_o, prev = _gla_fwd(proj, gate_up_p, gate_bias_p, gla_norm_g, "gla_fwd")
    cat = jnp.concatenate([sb_o, gla_o], axis=1)
    r1 = _matmul(cat, w_out_f, "nn", F32, "mix", res=x2, res_scale=DN_ALPHA)
    h, hb = _ln_fwd(r1, ln1_g, ln1_b, "ln1")
    u0 = _matmul(hb, w_up_f, "nn", F32, "ffn_up")
    p = _conv_gelu_fwd(u0, conv_w_f, conv_b, "conv_gelu")
    r2 = _matmul(p, w_down_f, "nn", F32, "ffn_down", res=h, res_scale=DN_ALPHA)
    d_r2, d_r2b, loss_p, g_ln2_g, g_ln2_b = _ln_loss_bwd(r2, tgt, ln2_g, ln2_b, "ln2_loss")

    d_p = _matmul(d_r2b, w_down_f, "nt", MXU_DTYPE, "d_ffn_act")
    g_w_down = _matmul(p, d_r2b, "tn", BF16, "grad_w_down")
    d_u0, g_conv_w, g_conv_b = _conv_gelu_bwd(u0, d_p, conv_w_f, conv_b, "conv_gelu_bwd")
    g_w_up = _matmul(hb, d_u0, "tn", BF16, "grad_w_up")
    d_h = _matmul(d_u0, w_up_f, "nt", F32, "d_h", res=d_r2, res_scale=DN_ALPHA)
    d_r1, d_r1b, g_ln1_g, g_ln1_b = _ln_bwd(r1, d_h, ln1_g, "ln1_bwd")
    g_w_out = _matmul(cat, d_r1b, "tn", BF16, "grad_w_out")
    d_cat = _matmul(d_r1b, w_out_f, "nt", F32, "d_cat")
    (d_gq, d_gk, d_gv, d_gg, d_ga_pad, g_gu_p, g_gb_p, g_gnorm) = _gla_bwd(
        proj, gate_up_p, gate_bias_p, gla_norm_g, prev, d_cat, "gla_bwd")
    g_gate_up = _unheads(g_gu_p[:, :GLA_GATE_RANK, :])
    g_gate_bias = g_gb_p.reshape(1, -1)
    small_g = [g_gate_bias, g_gnorm, g_ln1_g, g_ln1_b, g_conv_b, g_ln2_g, g_ln2_b]
    d_sq, d_sk, d_sv, *recv_rest = _sb_bwd(
        qkv, d_cat, sb_w, "sb_bwd",
        ride=[(g_w_out.reshape(N_DEV, -1, D), True), (_shard_cols(g_w_up), True),
              (g_w_down.reshape(N_DEV, -1, D), True), (_shard_cols(g_gate_up), True),
              (_shard_cols(g_conv_w), True)] + [(t, False) for t in small_g] + [(loss_p, False)])
    d_proj = jnp.concatenate([d_sq, d_sk, d_sv, d_gq, d_gk, d_gv, d_gg,
                              d_ga_pad.astype(MXU_DTYPE)], axis=1)
    g_w_in = _matmul(x2b, d_proj, "tn", BF16, "grad_w_in")[:, :IN_WIDTH]
    d_x, recv_in = _matmul(d_proj, w_in_pad, "nt", F32, "d_x", res=d_r1, res_scale=DN_ALPHA,
                           ride=[(_shard_cols(g_w_in), True)])

    recv = [recv_in] + recv_rest[:5]
    sharded = [(w_in, m_w_in, v_w_in), (w_out, m_w_out, v_w_out), (w_up, m_w_up, v_w_up),
               (w_down, m_w_down, v_w_down), (gate_up, m_gate_up, v_gate_up), (conv_w, m_conv_w, v_conv_w)]
    upd = [_adamw(recv[n], w[0], m[0], v[0], "adamw_%d" % n) for n, (w, m, v) in enumerate(sharded)]
    loss_row, small = _adamw_replicated(
        recv_rest[5:12], recv_rest[12], [gate_bias, gla_norm_g, ln1_g, ln1_b, conv_b, ln2_g, ln2_b],
        [m_gate_bias, m_gla_norm_g, m_ln1_g, m_ln1_b, m_conv_b, m_ln2_g, m_ln2_b],
        [v_gate_bias, v_gla_norm_g, v_ln1_g, v_ln1_b, v_conv_b, v_ln2_g, v_ln2_b], "adamw_replicated")
    outs = []
    for kind in range(4):
        b_w_in, b_w_out, b_w_up, b_w_down, b_gate_up, b_conv_w = [u[kind][None] for u in upd]
        s_gb, s_gn, s_l1g, s_l1b, s_cb, s_l2g, s_l2b = [t[kind] for t in small]
        outs += [b_w_in, b_gate_up, s_gb, s_gn, b_w_out, s_l1g, s_l1b, b_w_up, b_conv_w, s_cb, b_w_down,
                 s_l2g, s_l2b]
    return (loss_row[0, 0], d_x[None], *outs)
```
